```python
import jax, jax.numpy as jnp
from jax import lax
import numpy as np

D_MODEL = 1024
BATCH = 8
SEQ = 8192
DEPTH = 2

N_META = 16
CONV_GROUPS = 8
CONV_GROUP_DIM = 64
D_CONV = CONV_GROUPS * CONV_GROUP_DIM
CONV_WIDTH = 3
MLA_HEADS = 8
QK_NOPE = 64
QK_ROPE = 32
V_HEAD = 64
Q_LORA = 256
KV_LORA = 128
D_MLA = MLA_HEADS * V_HEAD
ROPE_BASE = 10000.0
Q_BLOCK = 128
NEG_INF = -1e30
D_FF = 2816
N_BRANCH = 2
ALPHA = (2 * DEPTH) ** 0.25
BETA = (8 * DEPTH) ** -0.25
LN_EPS = 1e-5
RMS_EPS = 1e-6
IN_SPLITS = (D_CONV, D_CONV, D_CONV, Q_LORA, KV_LORA, QK_ROPE, D_MODEL, D_MODEL)
D_IN = sum(IN_SPLITS)

kernel_name = 'hybrid_shortconv_mla_macaron_deepnorm'


def layer_norm(x, g, b):
    xf = x.astype(jnp.float32)
    mu = jnp.mean(xf, axis=-1, keepdims=True)
    var = jnp.mean(jnp.square(xf - mu), axis=-1, keepdims=True)
    return ((xf - mu) * lax.rsqrt(var + LN_EPS) * g + b).astype(x.dtype)


def rms_norm(x, g):
    xf = x.astype(jnp.float32)
    return (xf * lax.rsqrt(jnp.mean(jnp.square(xf), axis=-1, keepdims=True) + RMS_EPS) * g).astype(x.dtype)


def rope_tables(T):
    inv_freq = 1.0 / (ROPE_BASE ** (jnp.arange(0, QK_ROPE, 2, dtype=jnp.float32) / QK_ROPE))
    ang = jnp.arange(T, dtype=jnp.float32)[:, None] * inv_freq[None, :]
    return jnp.cos(ang), jnp.sin(ang)


def apply_rope(x, cos, sin):
    x1, x2 = jnp.split(x.astype(jnp.float32), 2, axis=-1)
    return jnp.concatenate([x1 * cos - x2 * sin, x2 * cos + x1 * sin], axis=-1).astype(x.dtype)


def swiglu(x, w_up, w_down):
    gate, up = jnp.split(x @ w_up, 2, axis=-1)
    return (jax.nn.silu(gate) * up) @ w_down


def causal_short_conv(u, w):
    T = u.shape[1]
    up = jnp.pad(u, ((0, 0), (CONV_WIDTH - 1, 0), (0, 0)))
    out = up[:, 0:T] * w[0]
    for k in range(1, CONV_WIDTH):
        out = out + up[:, k:k + T] * w[k]
    return out


def mla_causal_attention(q_nope, q_rope, k_nope, k_rope, v):
    bsz, T = q_nope.shape[:2]
    n_blocks = -(-T // Q_BLOCK)
    pad = n_blocks * Q_BLOCK - T
    scale = (QK_NOPE + QK_ROPE) ** -0.5

    def to_blocks(a):
        a = jnp.pad(a, ((0, 0), (0, pad), (0, 0), (0, 0)))
        return jnp.moveaxis(a.reshape(bsz, n_blocks, Q_BLOCK, *a.shape[2:]), 1, 0)

    q_pos = jnp.arange(n_blocks * Q_BLOCK, dtype=jnp.int32).reshape(n_blocks, Q_BLOCK)
    k_pos = jnp.arange(T, dtype=jnp.int32)

    def one_block(args):
        qn, qr, qp = args
        s = (jnp.einsum('bqhd,bkhd->bhqk', qn, k_nope)
             + jnp.einsum('bqhr,bkr->bhqk', qr, k_rope)).astype(jnp.float32) * scale
        s = jnp.where(k_pos[None, :] <= qp[:, None], s, NEG_INF)
        p = jax.nn.softmax(s, axis=-1).astype(v.dtype)
        return jnp.einsum('bhqk,bkhd->bqhd', p, v)

    out = lax.map(one_block, (to_blocks(q_nope), to_blocks(q_rope), q_pos))
    out = jnp.moveaxis(out, 0, 1).reshape(bsz, n_blocks * Q_BLOCK, MLA_HEADS, V_HEAD)
    return out[:, :T]


def hybrid_mixer(x, w_in, b_gate, conv_w, q_norm_g, w_uq, kv_norm_g, w_ukv, w_br_conv, w_br_mla, w_o, cos, sin):
    bsz, T, _ = x.shape
    cuts = [int(c) for c in np.cumsum(IN_SPLITS)[:-1]]
    b_in, c_in, h_in, c_q, c_kv, k_r, g_conv, g_mla = jnp.split(x @ w_in, cuts, axis=-1)
    y_conv = b_in * causal_short_conv(c_in * h_in, conv_w)
    q = (rms_norm(c_q, q_norm_g) @ w_uq).reshape(bsz, T, MLA_HEADS, QK_NOPE + QK_ROPE)
    q_nope = q[..., :QK_NOPE]
    q_rope = apply_rope(q[..., QK_NOPE:], cos[None, :, None], sin[None, :, None])
    kv = (rms_norm(c_kv, kv_norm_g) @ w_ukv).reshape(bsz, T, MLA_HEADS, QK_NOPE + V_HEAD)
    k_nope, v = kv[..., :QK_NOPE], kv[..., QK_NOPE:]
    k_rope = apply_rope(k_r, cos[None], sin[None])
    y_mla = mla_causal_attention(q_nope, q_rope, k_nope, k_rope, v).reshape(bsz, T, D_MLA)
    merged = (jax.nn.sigmoid(g_conv + b_gate[0]) * (y_conv @ w_br_conv)
              + jax.nn.sigmoid(g_mla + b_gate[1]) * (y_mla @ w_br_mla))
    return merged @ w_o


def _fwd_setup_inputs(seed: int = 0) -> dict:
    key = jax.random.key(seed)
    ks = jax.random.split(key, 18)
    L = DEPTH

    def dense(k, shape, scale=1.0):
        return jax.random.normal(k, shape, jnp.float32) * (scale * shape[-2] ** -0.5)

    def near_one(k, shape):
        return 1.0 + 0.02 * jax.random.normal(k, shape, jnp.float32)

    def small(k, shape):
        return 0.02 * jax.random.normal(k, shape, jnp.float32)

    return {
        'x': jax.random.normal(ks[0], (BATCH, SEQ, D_MODEL), jnp.float32),
        'meta_tokens': jax.random.normal(ks[1], (N_META, D_MODEL), jnp.float32),
        'ffn1_w_up': dense(ks[2], (L, D_MODEL, 2 * D_FF)),
        'ffn1_w_down': dense(ks[3], (L, D_FF, D_MODEL), BETA),
        'mix_w_in': dense(ks[4], (L, D_MODEL, D_IN)),
        'mix_b_gate': small(ks[5], (L, N_BRANCH, D_MODEL)),
        'conv_w': dense(ks[6], (L, CONV_WIDTH, D_CONV)),
        'q_norm_g': near_one(ks[7], (L, Q_LORA)),
        'w_uq': dense(ks[8], (L, Q_LORA, MLA_HEADS * (QK_NOPE + QK_ROPE))),
        'kv_norm_g': near_one(ks[9], (L, KV_LORA)),
        'w_ukv': dense(ks[10], (L, KV_LORA, MLA_HEADS * (QK_NOPE + V_HEAD))),
        'w_br_conv': dense(ks[11], (L, D_CONV, D_MODEL)),
        'w_br_mla': dense(ks[12], (L, D_MLA, D_MODEL)),
        'w_o': dense(ks[13], (L, D_MODEL, D_MODEL), BETA),
        'ffn2_w_up': dense(ks[14], (L, D_MODEL, 2 * D_FF)),
        'ffn2_w_down': dense(ks[15], (L, D_FF, D_MODEL), BETA),
        'ln_g': near_one(ks[16], (L, 3, D_MODEL)),
        'ln_b': small(ks[17], (L, 3, D_MODEL)),
    }


def _fwd_reference(x, meta_tokens, ffn1_w_up, ffn1_w_down, mix_w_in, mix_b_gate, conv_w, q_norm_g, w_uq,
              kv_norm_g, w_ukv, w_br_conv, w_br_mla, w_o, ffn2_w_up, ffn2_w_down, ln_g, ln_b):
    bsz = x.shape[0]
    meta = jnp.broadcast_to(meta_tokens[None].astype(x.dtype), (bsz, N_META, D_MODEL))
    h = jnp.concatenate([meta, x], axis=1)
    cos, sin = rope_tables(h.shape[1])
    for l in range(DEPTH):
        h = layer_norm(ALPHA * h + 0.5 * swiglu(h, ffn1_w_up[l], ffn1_w_down[l]), ln_g[l, 0], ln_b[l, 0])
        mix = hybrid_mixer(h, mix_w_in[l], mix_b_gate[l], conv_w[l], q_norm_g[l], w_uq[l], kv_norm_g[l],
                           w_ukv[l], w_br_conv[l], w_br_mla[l], w_o[l], cos, sin)
        h = layer_norm(ALPHA * h + mix, ln_g[l, 1], ln_b[l, 1])
        h = layer_norm(ALPHA * h + 0.5 * swiglu(h, ffn2_w_up[l], ffn2_w_down[l]), ln_g[l, 2], ln_b[l, 2])
    return h[:, N_META:]


import jax as _jax
import jax.numpy as _jnp

TWIN_FORMAT = 'train_step'
FWD_PARAMS = ['x', 'meta_tokens', 'ffn1_w_up', 'ffn1_w_down', 'mix_w_in', 'mix_b_gate', 'conv_w', 'q_norm_g', 'w_uq', 'kv_norm_g', 'w_ukv', 'w_br_conv', 'w_br_mla', 'w_o', 'ffn2_w_up', 'ffn2_w_down', 'ln_g', 'ln_b']
TWIN_WEIGHTS = ['meta_tokens', 'ffn1_w_up', 'ffn1_w_down', 'mix_w_in', 'mix_b_gate', 'conv_w', 'q_norm_g', 'w_uq', 'kv_norm_g', 'w_ukv', 'w_br_conv', 'w_br_mla', 'w_o', 'ffn2_w_up', 'ffn2_w_down', 'ln_g', 'ln_b']
TWIN_DIFF_INPUT = 'x'
TWIN_INPUTS = ['x', 'meta_tokens', 'ffn1_w_up', 'ffn1_w_down', 'mix_w_in', 'mix_b_gate', 'conv_w', 'q_norm_g', 'w_uq', 'kv_norm_g', 'w_ukv', 'w_br_conv', 'w_br_mla', 'w_o', 'ffn2_w_up', 'ffn2_w_down', 'ln_g', 'ln_b', 'loss_target', 'm_meta_tokens', 'm_ffn1_w_up', 'm_ffn1_w_down', 'm_mix_w_in', 'm_mix_b_gate', 'm_conv_w', 'm_q_norm_g', 'm_w_uq', 'm_kv_norm_g', 'm_w_ukv', 'm_w_br_conv', 'm_w_br_mla', 'm_w_o', 'm_ffn2_w_up', 'm_ffn2_w_down', 'm_ln_g', 'm_ln_b', 'v_meta_tokens', 'v_ffn1_w_up', 'v_ffn1_w_down', 'v_mix_w_in', 'v_mix_b_gate', 'v_conv_w', 'v_q_norm_g', 'v_w_uq', 'v_kv_norm_g', 'v_w_ukv', 'v_w_br_conv', 'v_w_br_mla', 'v_w_o', 'v_ffn2_w_up', 'v_ffn2_w_down', 'v_ln_g', 'v_ln_b']
TWIN_OUTPUTS = ['loss', 'grad_x', 'grad_meta_tokens', 'grad_ffn1_w_up', 'grad_ffn1_w_down', 'grad_mix_w_in', 'grad_mix_b_gate', 'grad_conv_w', 'grad_q_norm_g', 'grad_w_uq', 'grad_kv_norm_g', 'grad_w_ukv', 'grad_w_br_conv', 'grad_w_br_mla', 'grad_w_o', 'grad_ffn2_w_up', 'grad_ffn2_w_down', 'grad_ln_g', 'grad_ln_b', 'delta_meta_tokens', 'delta_ffn1_w_up', 'delta_ffn1_w_down', 'delta_mix_w_in', 'delta_mix_b_gate', 'delta_conv_w', 'delta_q_norm_g', 'delta_w_uq', 'delta_kv_norm_g', 'delta_w_ukv', 'delta_w_br_conv', 'delta_w_br_mla', 'delta_w_o', 'delta_ffn2_w_up', 'delta_ffn2_w_down', 'delta_ln_g', 'delta_ln_b', 'new_m_meta_tokens', 'new_m_ffn1_w_up', 'new_m_ffn1_w_down', 'new_m_mix_w_in', 'new_m_mix_b_gate', 'new_m_conv_w', 'new_m_q_norm_g', 'new_m_w_uq', 'new_m_kv_norm_g', 'new_m_w_ukv', 'new_m_w_br_conv', 'new_m_w_br_mla', 'new_m_w_o', 'new_m_ffn2_w_up', 'new_m_ffn2_w_down', 'new_m_ln_g', 'new_m_ln_b', 'new_v_meta_tokens', 'new_v_ffn1_w_up', 'new_v_ffn1_w_down', 'new_v_mix_w_in', 'new_v_mix_b_gate', 'new_v_conv_w', 'new_v_q_norm_g', 'new_v_w_uq', 'new_v_kv_norm_g', 'new_v_w_ukv', 'new_v_w_br_conv', 'new_v_w_br_mla', 'new_v_w_o', 'new_v_ffn2_w_up', 'new_v_ffn2_w_down', 'new_v_ln_g', 'new_v_ln_b']
TWIN_LEAF_KINDS = {'loss': 'loss', 'grad_x': 'grad_x', 'grad_meta_tokens': 'grad_w', 'grad_ffn1_w_up': 'grad_w', 'grad_ffn1_w_down': 'grad_w', 'grad_mix_w_in': 'grad_w', 'grad_mix_b_gate': 'grad_w', 'grad_conv_w': 'grad_w', 'grad_q_norm_g': 'grad_w', 'grad_w_uq': 'grad_w', 'grad_kv_norm_g': 'grad_w', 'grad_w_ukv': 'grad_w', 'grad_w_br_conv': 'grad_w', 'grad_w_br_mla': 'grad_w', 'grad_w_o': 'grad_w', 'grad_ffn2_w_up': 'grad_w', 'grad_ffn2_w_down': 'grad_w', 'grad_ln_g': 'grad_w', 'grad_ln_b': 'grad_w', 'delta_meta_tokens': 'delta_w', 'delta_ffn1_w_up': 'delta_w', 'delta_ffn1_w_down': 'delta_w', 'delta_mix_w_in': 'delta_w', 'delta_mix_b_gate': 'delta_w', 'delta_conv_w': 'delta_w', 'delta_q_norm_g': 'delta_w', 'delta_w_uq': 'delta_w', 'delta_kv_norm_g': 'delta_w', 'delta_w_ukv': 'delta_w', 'delta_w_br_conv': 'delta_w', 'delta_w_br_mla': 'delta_w', 'delta_w_o': 'delta_w', 'delta_ffn2_w_up': 'delta_w', 'delta_ffn2_w_down': 'delta_w', 'delta_ln_g': 'delta_w', 'delta_ln_b': 'delta_w', 'new_m_meta_tokens': 'new_m', 'new_m_ffn1_w_up': 'new_m', 'new_m_ffn1_w_down': 'new_m', 'new_m_mix_w_in': 'new_m', 'new_m_mix_b_gate': 'new_m', 'new_m_conv_w': 'new_m', 'new_m_q_norm_g': 'new_m', 'new_m_w_uq': 'new_m', 'new_m_kv_norm_g': 'new_m', 'new_m_w_ukv': 'new_m', 'new_m_w_br_conv': 'new_m', 'new_m_w_br_mla': 'new_m', 'new_m_w_o': 'new_m', 'new_m_ffn2_w_up': 'new_m', 'new_m_ffn2_w_down': 'new_m', 'new_m_ln_g': 'new_m', 'new_m_ln_b': 'new_m', 'new_v_meta_tokens': 'new_v', 'new_v_ffn1_w_up': 'new_v', 'new_v_ffn1_w_down': 'new_v', 'new_v_mix_w_in': 'new_v', 'new_v_mix_b_gate': 'new_v', 'new_v_conv_w': 'new_v', 'new_v_q_norm_g': 'new_v', 'new_v_w_uq': 'new_v', 'new_v_kv_norm_g': 'new_v', 'new_v_w_ukv': 'new_v', 'new_v_w_br_conv': 'new_v', 'new_v_w_br_mla': 'new_v', 'new_v_w_o': 'new_v', 'new_v_ffn2_w_up': 'new_v', 'new_v_ffn2_w_down': 'new_v', 'new_v_ln_g': 'new_v', 'new_v_ln_b': 'new_v'}


def _forward(args):
    return _fwd_reference(*[args[k] for k in FWD_PARAMS])


def _output_shape():
    def fwd():
        inp = _fwd_setup_inputs(0)
        return _fwd_reference(*[inp[k] for k in FWD_PARAMS])
    out = _jax.eval_shape(fwd)
    return out.shape, out.dtype

N_MICROBATCH = 1
ADAM_LR = 0.001
ADAM_B1 = 0.9
ADAM_B2 = 0.999
ADAM_EPS = 1e-08
ADAM_WD = 0.01
ADAM_STEP = 10
PER_EXAMPLE_BATCH_AXIS = {'x': 0, 'loss_target': 0}
SHARED_INPUTS = []
_WEIGHT_DTYPES = {'meta_tokens': _jnp.float32, 'ffn1_w_up': _jnp.float32, 'ffn1_w_down': _jnp.float32, 'mix_w_in': _jnp.float32, 'mix_b_gate': _jnp.float32, 'conv_w': _jnp.float32, 'q_norm_g': _jnp.float32, 'w_uq': _jnp.float32, 'kv_norm_g': _jnp.float32, 'w_ukv': _jnp.float32, 'w_br_conv': _jnp.float32, 'w_br_mla': _jnp.float32, 'w_o': _jnp.float32, 'ffn2_w_up': _jnp.float32, 'ffn2_w_down': _jnp.float32, 'ln_g': _jnp.float32, 'ln_b': _jnp.float32}
MOMENT_SCALE = {'meta_tokens': 2.779282e-03, 'ffn1_w_up': 1.721787e-02, 'ffn1_w_down': 5.613971e-02, 'mix_w_in': 4.400858e-02, 'mix_b_gate': 1.384677e-02, 'conv_w': 7.019542e-02, 'q_norm_g': 1.582579e-02, 'w_uq': 9.344796e-03, 'kv_norm_g': 3.627821e-02, 'w_ukv': 1.146532e-02, 'w_br_conv': 4.838869e-02, 'w_br_mla': 9.033721e-03, 'w_o': 9.864739e-02, 'ffn2_w_up': 1.653636e-02, 'ffn2_w_down': 5.399686e-02, 'ln_g': 2.626951e+01, 'ln_b': 1.325860e+00}


def _to_microbatches(a, axis):
    t = _jnp.moveaxis(a, axis, 0)
    t = t.reshape((N_MICROBATCH, t.shape[0] // N_MICROBATCH) + t.shape[1:])
    return _jnp.moveaxis(t, 1, axis + 1)


def setup_inputs(seed: int = 0) -> dict:
    inp = _fwd_setup_inputs(seed)
    key = _jax.random.fold_in(_jax.random.key(seed), 7919)
    shape, _ = _output_shape()
    out = dict(inp)
    out["loss_target"] = _jax.random.normal(_jax.random.fold_in(key, 0), shape, _jnp.float32)
    for i, name in enumerate(TWIN_WEIGHTS):
        w = inp[name].astype(_jnp.float32)
        if MOMENT_SCALE is None:
            s = _jnp.sqrt(_jnp.mean(_jnp.square(w)) + 1e-30)
        else:
            s = MOMENT_SCALE[name]
        km, kv = _jax.random.split(_jax.random.fold_in(key, i + 1))
        out[name] = w
        out["m_" + name] = s * _jax.random.normal(km, w.shape, _jnp.float32)
        out["v_" + name] = (s * s) * _jax.random.uniform(kv, w.shape, _jnp.float32, 0.5, 1.5)
    if N_MICROBATCH > 1:
        for name, axis in PER_EXAMPLE_BATCH_AXIS.items():
            out[name] = _to_microbatches(out[name], axis)
    return {'x': out['x'], 'meta_tokens': out['meta_tokens'], 'ffn1_w_up': out['ffn1_w_up'], 'ffn1_w_down': out['ffn1_w_down'], 'mix_w_in': out['mix_w_in'], 'mix_b_gate': out['mix_b_gate'], 'conv_w': out['conv_w'], 'q_norm_g': out['q_norm_g'], 'w_uq': out['w_uq'], 'kv_norm_g': out['kv_norm_g'], 'w_ukv': out['w_ukv'], 'w_br_conv': out['w_br_conv'], 'w_br_mla': out['w_br_mla'], 'w_o': out['w_o'], 'ffn2_w_up': out['ffn2_w_up'], 'ffn2_w_down': out['ffn2_w_down'], 'ln_g': out['ln_g'], 'ln_b': out['ln_b'], 'loss_target': out['loss_target'], 'm_meta_tokens': out['m_meta_tokens'], 'm_ffn1_w_up': out['m_ffn1_w_up'], 'm_ffn1_w_down': out['m_ffn1_w_down'], 'm_mix_w_in': out['m_mix_w_in'], 'm_mix_b_gate': out['m_mix_b_gate'], 'm_conv_w': out['m_conv_w'], 'm_q_norm_g': out['m_q_norm_g'], 'm_w_uq': out['m_w_uq'], 'm_kv_norm_g': out['m_kv_norm_g'], 'm_w_ukv': out['m_w_ukv'], 'm_w_br_conv': out['m_w_br_conv'], 'm_w_br_mla': out['m_w_br_mla'], 'm_w_o': out['m_w_o'], 'm_ffn2_w_up': out['m_ffn2_w_up'], 'm_ffn2_w_down': out['m_ffn2_w_down'], 'm_ln_g': out['m_ln_g'], 'm_ln_b': out['m_ln_b'], 'v_meta_tokens': out['v_meta_tokens'], 'v_ffn1_w_up': out['v_ffn1_w_up'], 'v_ffn1_w_down': out['v_ffn1_w_down'], 'v_mix_w_in': out['v_mix_w_in'], 'v_mix_b_gate': out['v_mix_b_gate'], 'v_conv_w': out['v_conv_w'], 'v_q_norm_g': out['v_q_norm_g'], 'v_w_uq': out['v_w_uq'], 'v_kv_norm_g': out['v_kv_norm_g'], 'v_w_ukv': out['v_w_ukv'], 'v_w_br_conv': out['v_w_br_conv'], 'v_w_br_mla': out['v_w_br_mla'], 'v_w_o': out['v_w_o'], 'v_ffn2_w_up': out['v_ffn2_w_up'], 'v_ffn2_w_down': out['v_ffn2_w_down'], 'v_ln_g': out['v_ln_g'], 'v_ln_b': out['v_ln_b']}


def _loss(weights, diff, rest, loss_target):
    with _jax.named_scope("forward"):
        args = {**rest, TWIN_DIFF_INPUT: diff, **{k: w.astype(_WEIGHT_DTYPES[k]) for k, w in weights.items()}}
        y = _forward(args)
    with _jax.named_scope("loss_head"):
        err = _jnp.square(y.astype(_jnp.float32) - loss_target)
        return 0.5 * _jnp.sum(_jnp.mean(err, axis=-1)) if err.ndim else 0.5 * err


def _adamw(w, g, m, v):
    m = ADAM_B1 * m + (1.0 - ADAM_B1) * g
    v = ADAM_B2 * v + (1.0 - ADAM_B2) * _jnp.square(g)
    m_hat = m / (1.0 - ADAM_B1 ** ADAM_STEP)
    v_hat = v / (1.0 - ADAM_B2 ** ADAM_STEP)
    delta = -ADAM_LR * (m_hat / (_jnp.sqrt(v_hat) + ADAM_EPS) + ADAM_WD * w)
    return delta, m, v


def reference(x, meta_tokens, ffn1_w_up, ffn1_w_down, mix_w_in, mix_b_gate, conv_w, q_norm_g, w_uq, kv_norm_g, w_ukv, w_br_conv, w_br_mla, w_o, ffn2_w_up, ffn2_w_down, ln_g, ln_b, loss_target, m_meta_tokens, m_ffn1_w_up, m_ffn1_w_down, m_mix_w_in, m_mix_b_gate, m_conv_w, m_q_norm_g, m_w_uq, m_kv_norm_g, m_w_ukv, m_w_br_conv, m_w_br_mla, m_w_o, m_ffn2_w_up, m_ffn2_w_down, m_ln_g, m_ln_b, v_meta_tokens, v_ffn1_w_up, v_ffn1_w_down, v_mix_w_in, v_mix_b_gate, v_conv_w, v_q_norm_g, v_w_uq, v_kv_norm_g, v_w_ukv, v_w_br_conv, v_w_br_mla, v_w_o, v_ffn2_w_up, v_ffn2_w_down, v_ln_g, v_ln_b):
    given = dict(x=x, meta_tokens=meta_tokens, ffn1_w_up=ffn1_w_up, ffn1_w_down=ffn1_w_down, mix_w_in=mix_w_in, mix_b_gate=mix_b_gate, conv_w=conv_w, q_norm_g=q_norm_g, w_uq=w_uq, kv_norm_g=kv_norm_g, w_ukv=w_ukv, w_br_conv=w_br_conv, w_br_mla=w_br_mla, w_o=w_o, ffn2_w_up=ffn2_w_up, ffn2_w_down=ffn2_w_down, ln_g=ln_g, ln_b=ln_b, loss_target=loss_target, m_meta_tokens=m_meta_tokens, m_ffn1_w_up=m_ffn1_w_up, m_ffn1_w_down=m_ffn1_w_down, m_mix_w_in=m_mix_w_in, m_mix_b_gate=m_mix_b_gate, m_conv_w=m_conv_w, m_q_norm_g=m_q_norm_g, m_w_uq=m_w_uq, m_kv_norm_g=m_kv_norm_g, m_w_ukv=m_w_ukv, m_w_br_conv=m_w_br_conv, m_w_br_mla=m_w_br_mla, m_w_o=m_w_o, m_ffn2_w_up=m_ffn2_w_up, m_ffn2_w_down=m_ffn2_w_down, m_ln_g=m_ln_g, m_ln_b=m_ln_b, v_meta_tokens=v_meta_tokens, v_ffn1_w_up=v_ffn1_w_up, v_ffn1_w_down=v_ffn1_w_down, v_mix_w_in=v_mix_w_in, v_mix_b_gate=v_mix_b_gate, v_conv_w=v_conv_w, v_q_norm_g=v_q_norm_g, v_w_uq=v_w_uq, v_kv_norm_g=v_kv_norm_g, v_w_ukv=v_w_ukv, v_w_br_conv=v_w_br_conv, v_w_br_mla=v_w_br_mla, v_w_o=v_w_o, v_ffn2_w_up=v_ffn2_w_up, v_ffn2_w_down=v_ffn2_w_down, v_ln_g=v_ln_g, v_ln_b=v_ln_b)
    weights = {n: given[n] for n in TWIN_WEIGHTS}
    shared = {n: given[n] for n in SHARED_INPUTS}
    per_example = {n: given[n] for n in ['x']}
    grad_fn = _jax.value_and_grad(_loss, argnums=(0, 1))

    def one_microbatch(ex, loss_target):
        ex = dict(ex)
        diff = ex.pop(TWIN_DIFF_INPUT)
        return grad_fn(weights, diff, {**shared, **ex}, loss_target)

    if N_MICROBATCH == 1:
        loss, (grad_w, grad_x) = one_microbatch(per_example, given["loss_target"])
    else:
        def body(carry, xs):
            loss_sum, grad_sum = carry
            l_k, (gw_k, gx_k) = one_microbatch(xs[0], xs[1])
            with _jax.named_scope("update"):
                return (loss_sum + l_k, _jax.tree.map(_jnp.add, grad_sum, gw_k)), gx_k

        init = (_jnp.zeros((), _jnp.float32), _jax.tree.map(_jnp.zeros_like, weights))
        (loss, grad_w), grad_x = _jax.lax.scan(body, init, (per_example, given["loss_target"]))
    with _jax.named_scope("update"):
        delta_w, new_m, new_v = {}, {}, {}
        for n in TWIN_WEIGHTS:
            delta_w[n], new_m[n], new_v[n] = _adamw(weights[n], grad_w[n], given["m_" + n], given["v_" + n])
    return (loss, grad_x, *[grad_w[n] for n in TWIN_WEIGHTS], *[delta_w[n] for n in TWIN_WEIGHTS],
            *[new_m[n] for n in TWIN_WEIGHTS], *[new_v[n] for n in TWIN_WEIGHTS])
```

```python
import functools

import numpy as np
import jax
import jax.numpy as jnp
from jax import lax
from jax.experimental import pallas as pl
from jax.experimental.pallas import tpu as pltpu

F32 = jnp.float32
BF16 = jnp.bfloat16

D_MODEL = 1024
DEPTH = 2
N_META = 16
D_CONV = 512
CONV_WIDTH = 3
MLA_HEADS = 8
QK_NOPE = 64
QK_ROPE = 32
V_HEAD = 64
Q_LORA = 256
KV_LORA = 128
D_MLA = MLA_HEADS * V_HEAD
ROPE_BASE = 10000.0
NEG_INF = -1e30
D_FF = 2816
ALPHA = (2 * DEPTH) ** 0.25
LN_EPS = 1e-5
RMS_EPS = 1e-6
ATTN_SCALE = (QK_NOPE + QK_ROPE) ** -0.5
ADAM_LR = 0.001
ADAM_B1 = 0.9
ADAM_B2 = 0.999
ADAM_EPS = 1e-08
ADAM_WD = 0.01
ADAM_STEP = 10

N_DEV = 8
HEAD_PAD = 128
HEADS_PER_STEP = 2
D_QK = MLA_HEADS * HEAD_PAD
Z_CONV = 0
Z_MID = 1536
Z_GATE = 2048
D_IN_PAD = 4096
D_IN_REAL = 4000
Z_KR_END = Z_MID + Q_LORA + KV_LORA + QK_ROPE

V7X_VMEM_LIMIT = 56 * 1024 * 1024
LANE = 128
ROW_ALIGN = 256


def _tile(n, cands):
    for c in cands:
        if n % c == 0:
            return c
    raise ValueError(f"no tile for {n} in {cands}")


def _params(sem):
    return pltpu.CompilerParams(dimension_semantics=sem, vmem_limit_bytes=V7X_VMEM_LIMIT)


def _mm(a, b, *, out_dtype, name, res=None, res_scale=1.0, tm, tn, tk):
    split = a.ndim == 3
    if split:
        S, M, Ks = a.shape
        K = S * Ks
    else:
        M, K = a.shape
        Ks = K
    N = b.shape[1]
    assert b.shape[0] == K and M % tm == 0 and N % tn == 0 and Ks % tk == 0
    nk = K // tk
    nks = Ks // tk
    has_res = res is not None

    def body(*refs):
        if has_res:
            a_ref, b_ref, r_ref, o_ref, acc = refs
        else:
            a_ref, b_ref, o_ref, acc = refs
        k = pl.program_id(2)
        part = jnp.dot(a_ref[...], b_ref[...], preferred_element_type=F32)

        @pl.when(k == 0)
        def _():
            acc[...] = part

        @pl.when(k > 0)
        def _():
            acc[...] += part

        @pl.when(k == nk - 1)
        def _():
            out = acc[...]
            if has_res:
                out = out + res_scale * r_ref[...]
            o_ref[...] = out.astype(out_dtype)

    if split:
        a_spec = pl.BlockSpec((None, tm, tk), lambda i, j, k: (k // nks, i, k % nks))
    else:
        a_spec = pl.BlockSpec((tm, tk), lambda i, j, k: (i, k))
    in_specs = [a_spec, pl.BlockSpec((tk, tn), lambda i, j, k: (k, j))]
    args = [a, b]
    if has_res:
        in_specs.append(pl.BlockSpec((tm, tn), lambda i, j, k: (i, j)))
        args.append(res)
    return pl.pallas_call(
        body, name=name, grid=(M // tm, N // tn, nk),
        in_specs=in_specs, out_specs=pl.BlockSpec((tm, tn), lambda i, j, k: (i, j)),
        out_shape=jax.ShapeDtypeStruct((M, N), out_dtype),
        scratch_shapes=[pltpu.VMEM((tm, tn), F32)],
        compiler_params=_params(("parallel", "parallel", "arbitrary")),
    )(*args)


def _mm_tn(a, b, *, name, tm, tn, tk):
    T, M = a.shape
    split = b.ndim == 3
    if split:
        S, _, Ns = b.shape
        N = S * Ns
    else:
        N = b.shape[1]
        Ns = N
    assert M % tm == 0 and Ns % tn == 0 and T % tk == 0
    nns = Ns // tn

    def body(a_ref, b_ref, o_ref):
        k = pl.program_id(2)
        part = lax.dot_general(a_ref[...], b_ref[...], (((0,), (0,)), ((), ())),
                               preferred_element_type=F32)

        @pl.when(k == 0)
        def _():
            o_ref[...] = part

        @pl.when(k > 0)
        def _():
            o_ref[...] += part

    if split:
        b_spec = pl.BlockSpec((None, tk, tn), lambda i, j, k: (j // nns, k, j % nns))
    else:
        b_spec = pl.BlockSpec((tk, tn), lambda i, j, k: (k, j))
    return pl.pallas_call(
        body, name=name, grid=(M // tm, N // tn, T // tk),
        in_specs=[pl.BlockSpec((tk, tm), lambda i, j, k: (k, i)), b_spec],
        out_specs=pl.BlockSpec((tm, tn), lambda i, j, k: (i, j)),
        out_shape=jax.ShapeDtypeStruct((M, N), F32),
        compiler_params=_params(("parallel", "parallel", "arbitrary")),
    )(a, b)


def _ffn_up(hb, w_up, *, name, tm, tn):
    T = hb.shape[0]
    nb = D_FF // tn

    def body(h_ref, wg_ref, wu_ref, gu_ref, a_ref):
        h = h_ref[...]
        g = jnp.dot(h, wg_ref[...], preferred_element_type=F32)
        u = jnp.dot(h, wu_ref[...], preferred_element_type=F32)
        gu_ref[0] = g.astype(BF16)
        gu_ref[1] = u.astype(BF16)
        a_ref[...] = (g * jax.nn.sigmoid(g) * u).astype(BF16)

    return pl.pallas_call(
        body, name=name, grid=(nb, T // tm),
        in_specs=[pl.BlockSpec((tm, D_MODEL), lambda j, i: (i, 0)),
                  pl.BlockSpec((D_MODEL, tn), lambda j, i: (0, j)),
                  pl.BlockSpec((D_MODEL, tn), lambda j, i: (0, j + nb))],
        out_specs=[pl.BlockSpec((2, tm, tn), lambda j, i: (0, i, j)),
                   pl.BlockSpec((tm, tn), lambda j, i: (i, j))],
        out_shape=[jax.ShapeDtypeStruct((2, T, D_FF), BF16), jax.ShapeDtypeStruct((T, D_FF), BF16)],
        compiler_params=_params(("parallel", "parallel")),
    )(hb, w_up, w_up)


def _mm_res_ln(a, w, res, g, b, *, scale, name, tm):
    T, K = a.shape

    def body(a_ref, w_ref, res_ref, g_ref, b_ref, r_ref, y_ref, yb_ref):
        f = jnp.dot(a_ref[...], w_ref[...], preferred_element_type=F32)
        r = ALPHA * res_ref[...] + scale * f
        mu = jnp.mean(r, axis=-1, keepdims=True)
        xc = r - mu
        var = jnp.mean(xc * xc, axis=-1, keepdims=True)
        y = xc * lax.rsqrt(var + LN_EPS) * g_ref[...] + b_ref[...]
        r_ref[...] = r
        y_ref[...] = y
        yb_ref[...] = y.astype(BF16)

    row = pl.BlockSpec((tm, D_MODEL), lambda i: (i, 0))
    vec = pl.BlockSpec((1, D_MODEL), lambda i: (0, 0))
    return pl.pallas_call(
        body, name=name, grid=(T // tm,),
        in_specs=[pl.BlockSpec((tm, K), lambda i: (i, 0)), pl.BlockSpec((K, D_MODEL), lambda i: (0, 0)),
                  row, vec, vec],
        out_specs=[row, row, row],
        out_shape=[jax.ShapeDtypeStruct((T, D_MODEL), F32), jax.ShapeDtypeStruct((T, D_MODEL), F32),
                   jax.ShapeDtypeStruct((T, D_MODEL), BF16)],
        compiler_params=_params(("parallel",)),
    )(a, w, res, g, b)


def _conv_fwd(z, conv_w8, *, name, tm):
    T = z.shape[0]
    hb = tm // 8

    def body(b_ref, c_ref, h_ref, cp_ref, hp_ref, w_ref, y_ref):
        i = pl.program_id(0)
        u = c_ref[...] * h_ref[...]
        up = jnp.where(i > 0, cp_ref[...] * hp_ref[...], 0.0)
        ue = jnp.concatenate([up, u], axis=0)
        s1 = pltpu.roll(ue, 1, 0)[8:]
        s2 = pltpu.roll(ue, 2, 0)[8:]
        w = w_ref[...]
        conv = w[0:1] * s2 + w[1:2] * s1 + w[2:3] * u
        y_ref[...] = (b_ref[...] * conv).astype(BF16)

    def col(c):
        return pl.BlockSpec((tm, D_CONV), lambda i: (i, c))

    def prev(c):
        return pl.BlockSpec((8, D_CONV), lambda i: (jnp.maximum(i * hb - 1, 0), c))

    return pl.pallas_call(
        body, name=name, grid=(T // tm,),
        in_specs=[col(0), col(1), col(2), prev(1), prev(2), pl.BlockSpec((8, D_CONV), lambda i: (0, 0))],
        out_specs=pl.BlockSpec((tm, D_CONV), lambda i: (i, 0)),
        out_shape=jax.ShapeDtypeStruct((T, D_CONV), BF16),
        compiler_params=_params(("parallel",)),
    )(z, z, z, z, z, conv_w8)


def _rope(x, c, s1, s2):
    n = x.shape[-1]
    return x * c + pltpu.roll(x, 16, 1) * s1 + pltpu.roll(x, n - 16, 1) * s2


def _rope_t(d, c, s1, s2):
    n = d.shape[-1]
    return d * c + pltpu.roll(d * s1, n - 16, 1) + pltpu.roll(d * s2, 16, 1)


def _rms(x, g):
    rstd = lax.rsqrt(jnp.mean(x * x, axis=-1, keepdims=True) + RMS_EPS)
    return x * rstd * g


def _qkv_proj(z, gq, gkv, wq, wk, wv, tabs, *, name, tm):
    T = z.shape[0]

    def body(z_ref, gq_ref, gkv_ref, wq_ref, wk_ref, wv_ref, c_ref, s1_ref, s2_ref,
             q_ref, k_ref, v_ref, qn_ref, kin_ref):
        zz = z_ref[...]
        qn = _rms(zz[:, :Q_LORA], gq_ref[...]).astype(BF16)
        kvn = _rms(zz[:, Q_LORA:Q_LORA + KV_LORA], gkv_ref[...]).astype(BF16)
        kin = jnp.concatenate([kvn, zz[:, Q_LORA + KV_LORA:].astype(BF16)], axis=-1)
        c = jnp.tile(c_ref[...], (1, MLA_HEADS))
        s1 = jnp.tile(s1_ref[...], (1, MLA_HEADS))
        s2 = jnp.tile(s2_ref[...], (1, MLA_HEADS))
        qpre = jnp.dot(qn, wq_ref[...], preferred_element_type=F32)
        kpre = jnp.dot(kin, wk_ref[...], preferred_element_type=F32)
        q_ref[...] = (_rope(qpre, c, s1, s2) * ATTN_SCALE).astype(BF16)
        k_ref[...] = _rope(kpre, c, s1, s2).astype(BF16)
        v_ref[...] = jnp.dot(kvn, wv_ref[...], preferred_element_type=F32).astype(BF16)
        qn_ref[...] = qn
        kin_ref[...] = kin

    def full(shape):
        return pl.BlockSpec(shape, lambda i: (0, 0))

    def rows(w, c=0):
        return pl.BlockSpec((tm, w), lambda i: (i, c))

    return pl.pallas_call(
        body, name=name, grid=(T // tm,),
        in_specs=[rows(512, Z_MID // 512), full((1, Q_LORA)), full((1, KV_LORA)),
                  full((Q_LORA, D_QK)), full((Q_LORA, D_QK)), full((KV_LORA, D_MLA)),
                  rows(LANE), rows(LANE), rows(LANE)],
        out_specs=[rows(D_QK), rows(D_QK), rows(D_MLA), rows(Q_LORA), rows(Q_LORA)],
        out_shape=[jax.ShapeDtypeStruct((T, D_QK), BF16), jax.ShapeDtypeStruct((T, D_QK), BF16),
                   jax.ShapeDtypeStruct((T, D_MLA), BF16), jax.ShapeDtypeStruct((T, Q_LORA), BF16),
                   jax.ShapeDtypeStruct((T, Q_LORA), BF16)],
        compiler_params=_params(("parallel",)),
    )(z, gq, gkv, wq, wk, wv, *tabs)


def _causal_mask(s, i, j, blk):
    row = lax.broadcasted_iota(jnp.int32, s.shape, 0) + i * blk
    col = lax.broadcasted_iota(jnp.int32, s.shape, 1) + j * blk
    return jnp.where(col <= row, s, NEG_INF)


def _attn_fwd(q, k, v, *, name, blk):
    T = q.shape[0]
    n = T // blk
    hp = HEADS_PER_STEP
    qi = np.array([i for i in range(n) for j in range(i + 1)], np.int32)
    kj = np.array([j for i in range(n) for j in range(i + 1)], np.int32)

    def body(qi_ref, kj_ref, q_ref, k_ref, v_ref, o_ref, lse_ref, m_sc, l_sc, acc_sc):
        s_id = pl.program_id(1)
        i = qi_ref[s_id]
        j = kj_ref[s_id]

        @pl.when(j == 0)
        def _():
            m_sc[...] = jnp.full(m_sc.shape, NEG_INF, F32)
            l_sc[...] = jnp.zeros(l_sc.shape, F32)
            acc_sc[...] = jnp.zeros(acc_sc.shape, F32)

        for hh in range(hp):
            qh = q_ref[:, hh * HEAD_PAD:(hh + 1) * HEAD_PAD]
            kh = k_ref[:, hh * HEAD_PAD:(hh + 1) * HEAD_PAD]
            vh = v_ref[:, hh * V_HEAD:(hh + 1) * V_HEAD]
            s = lax.dot_general(qh, kh, (((1,), (1,)), ((), ())), preferred_element_type=F32)
            s = _causal_mask(s, i, j, blk)
            m_old = m_sc[hh]
            m_new = jnp.maximum(m_old, jnp.max(s, axis=-1, keepdims=True))
            a = jnp.exp(m_old - m_new)
            p = jnp.exp(s - m_new)
            l_sc[hh] = a * l_sc[hh] + jnp.sum(p, axis=-1, keepdims=True)
            acc_sc[hh] = a * acc_sc[hh] + jnp.dot(p.astype(BF16), vh, preferred_element_type=F32)
            m_sc[hh] = m_new

        @pl.when(j == i)
        def _():
            for hh in range(hp):
                l = l_sc[hh]
                o_ref[:, hh * V_HEAD:(hh + 1) * V_HEAD] = (acc_sc[hh] / l).astype(BF16)
                lse_ref[hh] = m_sc[hh] + jnp.log(l)

    grid_spec = pltpu.PrefetchScalarGridSpec(
        num_scalar_prefetch=2, grid=(MLA_HEADS // hp, len(qi)),
        in_specs=[pl.BlockSpec((blk, hp * HEAD_PAD), lambda g, s, qi, kj: (qi[s], g)),
                  pl.BlockSpec((blk, hp * HEAD_PAD), lambda g, s, qi, kj: (kj[s], g)),
                  pl.BlockSpec((blk, hp * V_HEAD), lambda g, s, qi, kj: (kj[s], g))],
        out_specs=[pl.BlockSpec((blk, hp * V_HEAD), lambda g, s, qi, kj: (qi[s], g)),
                   pl.BlockSpec((hp, blk, 1), lambda g, s, qi, kj: (g, qi[s], 0))],
        scratch_shapes=[pltpu.VMEM((hp, blk, 1), F32), pltpu.VMEM((hp, blk, 1), F32),
                        pltpu.VMEM((hp, blk, V_HEAD), F32)])
    return pl.pallas_call(
        body, name=name, grid_spec=grid_spec,
        out_shape=[jax.ShapeDtypeStruct((T, D_MLA), BF16), jax.ShapeDtypeStruct((MLA_HEADS, T, 1), F32)],
        compiler_params=_params(("parallel", "arbitrary")),
    )(jnp.asarray(qi), jnp.asarray(kj), q, k, v)


def _merge(yc, ym, wbc, wbm, z, bg, *, name, tm):
    T = yc.shape[0]

    def body(yc_ref, ym_ref, wbc_ref, wbm_ref, gc_ref, gm_ref, bg_ref, mg_ref, pa_ref, pb_ref):
        pa = jnp.dot(yc_ref[...], wbc_ref[...], preferred_element_type=F32)
        pb = jnp.dot(ym_ref[...], wbm_ref[...], preferred_element_type=F32)
        bgv = bg_ref[...]
        sa = jax.nn.sigmoid(gc_ref[...] + bgv[0:1])
        sb = jax.nn.sigmoid(gm_ref[...] + bgv[1:2])
        mg_ref[...] = (sa * pa + sb * pb).astype(BF16)
        pa_ref[...] = pa.astype(BF16)
        pb_ref[...] = pb.astype(BF16)

    row = pl.BlockSpec((tm, D_MODEL), lambda i: (i, 0))
    return pl.pallas_call(
        body, name=name, grid=(T // tm,),
        in_specs=[pl.BlockSpec((tm, D_CONV), lambda i: (i, 0)), pl.BlockSpec((tm, D_MLA), lambda i: (i, 0)),
                  pl.BlockSpec((D_CONV, D_MODEL), lambda i: (0, 0)), pl.BlockSpec((D_MLA, D_MODEL), lambda i: (0, 0)),
                  pl.BlockSpec((tm, D_MODEL), lambda i: (i, Z_GATE // D_MODEL)),
                  pl.BlockSpec((tm, D_MODEL), lambda i: (i, Z_GATE // D_MODEL + 1)),
                  pl.BlockSpec((8, D_MODEL), lambda i: (0, 0))],
        out_specs=[row, row, row],
        out_shape=[jax.ShapeDtypeStruct((T, D_MODEL), BF16)] * 3,
        compiler_params=_params(("parallel",)),
    )(yc, ym, wbc, wbm, z, z, bg)


def _loss_head(h, tgt, *, t_real, name, tm):
    T = h.shape[0]

    def body(h_ref, t_ref, dy_ref, loss_ref):
        i = pl.program_id(0)
        row = lax.broadcasted_iota(jnp.int32, (tm, 1), 0) + i * tm
        valid = (row >= N_META) & (row < t_real)
        err = jnp.where(valid, h_ref[...] - t_ref[...], 0.0)
        dy_ref[...] = err * (1.0 / D_MODEL)
        part = 0.5 * jnp.sum(jnp.sum(err * err, axis=-1, keepdims=True) * (1.0 / D_MODEL), axis=0, keepdims=True)

        @pl.when(i == 0)
        def _():
            loss_ref[...] = jnp.zeros(loss_ref.shape, F32)

        loss_ref[...] += jnp.broadcast_to(part, loss_ref.shape)

    row_spec = pl.BlockSpec((tm, D_MODEL), lambda i: (i, 0))
    return pl.pallas_call(
        body, name=name, grid=(T // tm,),
        in_specs=[row_spec, row_spec],
        out_specs=[row_spec, pl.BlockSpec((8, LANE), lambda i: (0, 0))],
        out_shape=[jax.ShapeDtypeStruct((T, D_MODEL), F32), jax.ShapeDtypeStruct((8, LANE), F32)],
        compiler_params=_params(("arbitrary",)),
    )(h, tgt)


def _ln_bwd(dy, r, g, *, scale, name, tm):
    T = dy.shape[0]

    def body(dy_ref, r_ref, g_ref, dr_ref, drb_ref, dg_ref, db_ref):
        i = pl.program_id(0)
        rr = r_ref[...]
        dyv = dy_ref[...]
        mu = jnp.mean(rr, axis=-1, keepdims=True)
        xc = rr - mu
        rstd = lax.rsqrt(jnp.mean(xc * xc, axis=-1, keepdims=True) + LN_EPS)
        xh = xc * rstd
        dxh = dyv * g_ref[...]
        m1 = jnp.mean(dxh, axis=-1, keepdims=True)
        m2 = jnp.mean(dxh * xh, axis=-1, keepdims=True)
        dr = rstd * (dxh - m1 - xh * m2)
        dr_ref[...] = dr
        drb_ref[...] = (scale * dr).astype(BF16)

        @pl.when(i == 0)
        def _():
            dg_ref[...] = jnp.zeros(dg_ref.shape, F32)
            db_ref[...] = jnp.zeros(db_ref.shape, F32)

        dg_ref[0:1, :] += jnp.sum(dyv * xh, axis=0, keepdims=True)
        db_ref[0:1, :] += jnp.sum(dyv, axis=0, keepdims=True)

    row = pl.BlockSpec((tm, D_MODEL), lambda i: (i, 0))
    acc = pl.BlockSpec((8, D_MODEL), lambda i: (0, 0))
    return pl.pallas_call(
        body, name=name, grid=(T // tm,),
        in_specs=[row, row, pl.BlockSpec((1, D_MODEL), lambda i: (0, 0))],
        out_specs=[row, row, acc, acc],
        out_shape=[jax.ShapeDtypeStruct((T, D_MODEL), F32), jax.ShapeDtypeStruct((T, D_MODEL), BF16),
                   jax.ShapeDtypeStruct((8, D_MODEL), F32), jax.ShapeDtypeStruct((8, D_MODEL), F32)],
        compiler_params=_params(("arbitrary",)),
    )(dy, r, g)


def _ffn_bwd_mid(dfb, w_down_t, gu, *, name, tm, tn):
    T = dfb.shape[0]

    def body(df_ref, w_ref, gu_ref, o_ref):
        da = jnp.dot(df_ref[...], w_ref[...], preferred_element_type=F32)
        g = gu_ref[0].astype(F32)
        u = gu_ref[1].astype(F32)
        sg = jax.nn.sigmoid(g)
        o_ref[0] = (da * u * (sg * (1.0 + g * (1.0 - sg)))).astype(BF16)
        o_ref[1] = (da * (g * sg)).astype(BF16)

    return pl.pallas_call(
        body, name=name, grid=(D_FF // tn, T // tm),
        in_specs=[pl.BlockSpec((tm, D_MODEL), lambda j, i: (i, 0)),
                  pl.BlockSpec((D_MODEL, tn), lambda j, i: (0, j)),
                  pl.BlockSpec((2, tm, tn), lambda j, i: (0, i, j))],
        out_specs=pl.BlockSpec((2, tm, tn), lambda j, i: (0, i, j)),
        out_shape=jax.ShapeDtypeStruct((2, T, D_FF), BF16),
        compiler_params=_params(("parallel", "parallel")),
    )(dfb, w_down_t, gu)


def _wo_bwd(dmb, wo_t, z, bg, pa, pb, *, name, tm):
    T = dmb.shape[0]

    def body(dm_ref, w_ref, gc_ref, gm_ref, bg_ref, pa_ref, pb_ref, dpa_ref, dpb_ref, dg_ref, dbg_ref):
        i = pl.program_id(0)
        dm = jnp.dot(dm_ref[...], w_ref[...], preferred_element_type=F32)
        bgv = bg_ref[...]
        sa = jax.nn.sigmoid(gc_ref[...] + bgv[0:1])
        sb = jax.nn.sigmoid(gm_ref[...] + bgv[1:2])
        dpa_ref[...] = (dm * sa).astype(BF16)
        dpb_ref[...] = (dm * sb).astype(BF16)
        dga = dm * pa_ref[...].astype(F32) * (sa * (1.0 - sa))
        dgb = dm * pb_ref[...].astype(F32) * (sb * (1.0 - sb))
        dg_ref[:, :D_MODEL] = dga.astype(BF16)
        dg_ref[:, D_MODEL:] = dgb.astype(BF16)

        @pl.when(i == 0)
        def _():
            dbg_ref[...] = jnp.zeros(dbg_ref.shape, F32)

        dbg_ref[0:1, :] += jnp.sum(dga, axis=0, keepdims=True)
        dbg_ref[1:2, :] += jnp.sum(dgb, axis=0, keepdims=True)

    row = pl.BlockSpec((tm, D_MODEL), lambda i: (i, 0))
    return pl.pallas_call(
        body, name=name, grid=(T // tm,),
        in_specs=[row, pl.BlockSpec((D_MODEL, D_MODEL), lambda i: (0, 0)),
                  pl.BlockSpec((tm, D_MODEL), lambda i: (i, Z_GATE // D_MODEL)),
                  pl.BlockSpec((tm, D_MODEL), lambda i: (i, Z_GATE // D_MODEL + 1)),
                  pl.BlockSpec((8, D_MODEL), lambda i: (0, 0)), row, row],
        out_specs=[row, row, pl.BlockSpec((tm, 2 * D_MODEL), lambda i: (i, 0)),
                   pl.BlockSpec((8, D_MODEL), lambda i: (0, 0))],
        out_shape=[jax.ShapeDtypeStruct((T, D_MODEL), BF16), jax.ShapeDtypeStruct((T, D_MODEL), BF16),
                   jax.ShapeDtypeStruct((T, 2 * D_MODEL), BF16), jax.ShapeDtypeStruct((8, D_MODEL), F32)],
        compiler_params=_params(("arbitrary",)),
    )(dmb, wo_t, z, z, bg, pa, pb)


def _conv_bwd(dy, z, conv_w8, *, name, tm):
    T = dy.shape[0]
    n = T // tm
    hb = tm // 8

    def body(dy_ref, b_ref, c_ref, h_ref, cp_ref, hp_ref, dyn_ref, bn_ref, w_ref, dz_ref, dw_ref):
        i = pl.program_id(0)
        u = c_ref[...] * h_ref[...]
        up = jnp.where(i > 0, cp_ref[...] * hp_ref[...], 0.0)
        ue = jnp.concatenate([up, u], axis=0)
        s1 = pltpu.roll(ue, 1, 0)[8:]
        s2 = pltpu.roll(ue, 2, 0)[8:]
        w = w_ref[...]
        conv = w[0:1] * s2 + w[1:2] * s1 + w[2:3] * u
        dyv = dy_ref[...]
        e = dyv * b_ref[...]
        en = jnp.where(i < n - 1, dyn_ref[...] * bn_ref[...], 0.0)
        ee = jnp.concatenate([e, en], axis=0)
        e1 = pltpu.roll(ee, tm + 8 - 1, 0)[:tm]
        e2 = pltpu.roll(ee, tm + 8 - 2, 0)[:tm]
        du = w[2:3] * e + w[1:2] * e1 + w[0:1] * e2
        dz_ref[:, 0:D_CONV] = (dyv * conv).astype(BF16)
        dz_ref[:, D_CONV:2 * D_CONV] = (du * h_ref[...]).astype(BF16)
        dz_ref[:, 2 * D_CONV:] = (du * c_ref[...]).astype(BF16)

        @pl.when(i == 0)
        def _():
            dw_ref[...] = jnp.zeros(dw_ref.shape, F32)

        dw_ref[0:1, :] += jnp.sum(e * s2, axis=0, keepdims=True)
        dw_ref[1:2, :] += jnp.sum(e * s1, axis=0, keepdims=True)
        dw_ref[2:3, :] += jnp.sum(e * u, axis=0, keepdims=True)

    def col(c):
        return pl.BlockSpec((tm, D_CONV), lambda i: (i, c))

    def prev(c):
        return pl.BlockSpec((8, D_CONV), lambda i: (jnp.maximum(i * hb - 1, 0), c))

    def nxt(c):
        return pl.BlockSpec((8, D_CONV), lambda i: (jnp.minimum((i + 1) * hb, T // 8 - 1), c))

    return pl.pallas_call(
        body, name=name, grid=(n,),
        in_specs=[col(0), col(0), col(1), col(2), prev(1), prev(2), nxt(0), nxt(0),
                  pl.BlockSpec((8, D_CONV), lambda i: (0, 0))],
        out_specs=[pl.BlockSpec((tm, 3 * D_CONV), lambda i: (i, 0)), pl.BlockSpec((8, D_CONV), lambda i: (0, 0))],
        out_shape=[jax.ShapeDtypeStruct((T, 3 * D_CONV), BF16), jax.ShapeDtypeStruct((8, D_CONV), F32)],
        compiler_params=_params(("arbitrary",)),
    )(dy, z, z, z, z, z, dy, z, conv_w8)


def _attn_bwd(q, k, v, o, do, lse, *, name, blk):
    T = q.shape[0]
    n = T // blk
    hp = HEADS_PER_STEP
    qi = np.array([i for j in range(n) for i in range(j, n)], np.int32)
    kj = np.array([j for j in range(n) for i in range(j, n)], np.int32)

    def body(qi_ref, kj_ref, q_ref, k_ref, v_ref, o_ref, do_ref, lse_ref, dq_ref, dk_ref, dv_ref, dk_sc, dv_sc):
        s_id = pl.program_id(1)
        i = qi_ref[s_id]
        j = kj_ref[s_id]

        @pl.when(s_id == 0)
        def _():
            dq_ref[...] = jnp.zeros(dq_ref.shape, F32)

        @pl.when(i == j)
        def _():
            dk_sc[...] = jnp.zeros(dk_sc.shape, F32)
            dv_sc[...] = jnp.zeros(dv_sc.shape, F32)

        rows = pl.ds(pl.multiple_of(i * blk, blk), blk)
        for hh in range(hp):
            hs = slice(hh * HEAD_PAD, (hh + 1) * HEAD_PAD)
            vs = slice(hh * V_HEAD, (hh + 1) * V_HEAD)
            qh = q_ref[:, hs]
            kh = k_ref[:, hs]
            vh = v_ref[:, vs]
            doh = do_ref[:, vs]
            oh = o_ref[:, vs]
            s = lax.dot_general(qh, kh, (((1,), (1,)), ((), ())), preferred_element_type=F32)
            s = _causal_mask(s, i, j, blk)
            p = jnp.exp(s - lse_ref[hh])
            pb = p.astype(BF16)
            dv_sc[hh] += lax.dot_general(pb, doh, (((0,), (0,)), ((), ())), preferred_element_type=F32)
            dp = lax.dot_general(doh, vh, (((1,), (1,)), ((), ())), preferred_element_type=F32)
            delta = jnp.sum(doh.astype(F32) * oh.astype(F32), axis=-1, keepdims=True)
            ds = (p * (dp - delta)).astype(BF16)
            dk_sc[hh] += lax.dot_general(ds, qh, (((0,), (0,)), ((), ())), preferred_element_type=F32)
            dq_ref[rows, hs] += jnp.dot(ds, kh, preferred_element_type=F32)

        @pl.when(i == n - 1)
        def _():
            for hh in range(hp):
                dk_ref[:, hh * HEAD_PAD:(hh + 1) * HEAD_PAD] = dk_sc[hh]
                dv_ref[:, hh * V_HEAD:(hh + 1) * V_HEAD] = dv_sc[hh]

    wq = hp * HEAD_PAD
    wv = hp * V_HEAD
    grid_spec = pltpu.PrefetchScalarGridSpec(
        num_scalar_prefetch=2, grid=(MLA_HEADS // hp, len(qi)),
        in_specs=[pl.BlockSpec((blk, wq), lambda g, s, qi, kj: (qi[s], g)),
                  pl.BlockSpec((blk, wq), lambda g, s, qi, kj: (kj[s], g)),
                  pl.BlockSpec((blk, wv), lambda g, s, qi, kj: (kj[s], g)),
                  pl.BlockSpec((blk, wv), lambda g, s, qi, kj: (qi[s], g)),
                  pl.BlockSpec((blk, wv), lambda g, s, qi, kj: (qi[s], g)),
                  pl.BlockSpec((hp, blk, 1), lambda g, s, qi, kj: (g, qi[s], 0))],
        out_specs=[pl.BlockSpec((T, wq), lambda g, s, qi, kj: (0, g)),
                   pl.BlockSpec((blk, wq), lambda g, s, qi, kj: (kj[s], g)),
                   pl.BlockSpec((blk, wv), lambda g, s, qi, kj: (kj[s], g))],
        scratch_shapes=[pltpu.VMEM((hp, blk, HEAD_PAD), F32), pltpu.VMEM((hp, blk, V_HEAD), F32)])
    return pl.pallas_call(
        body, name=name, grid_spec=grid_spec,
        out_shape=[jax.ShapeDtypeStruct((T, D_QK), F32), jax.ShapeDtypeStruct((T, D_QK), F32),
                   jax.ShapeDtypeStruct((T, D_MLA), F32)],
        compiler_params=_params(("parallel", "arbitrary")),
    )(jnp.asarray(qi), jnp.asarray(kj), q, k, v, o, do, lse)


def _qkv_bwd(dq, dk, dv, z, gq, gkv, wq_t, wk_t, wv_t, tabs, *, name, tm):
    T = dq.shape[0]

    def body(dq_ref, dk_ref, dv_ref, z_ref, gq_ref, gkv_ref, wq_ref, wk_ref, wv_ref, c_ref, s1_ref, s2_ref,
             dz_ref, dqb_ref, dkb_ref, dvb_ref, dgq_ref, dgkv_ref):
        i = pl.program_id(0)
        c = jnp.tile(c_ref[...], (1, MLA_HEADS))
        s1 = jnp.tile(s1_ref[...], (1, MLA_HEADS))
        s2 = jnp.tile(s2_ref[...], (1, MLA_HEADS))
        dqp = _rope_t(dq_ref[...] * ATTN_SCALE, c, s1, s2).astype(BF16)
        dkp = _rope_t(dk_ref[...], c, s1, s2).astype(BF16)
        dvb = dv_ref[...].astype(BF16)
        dqb_ref[...] = dqp
        dkb_ref[...] = dkp
        dvb_ref[...] = dvb
        dqn = jnp.dot(dqp, wq_ref[...], preferred_element_type=F32)
        dkin = jnp.dot(dkp, wk_ref[...], preferred_element_type=F32)
        dkvn = dkin[:, :KV_LORA] + jnp.dot(dvb, wv_ref[...], preferred_element_type=F32)
        zz = z_ref[...]

        def rms_bwd(x, g, dy):
            rstd = lax.rsqrt(jnp.mean(x * x, axis=-1, keepdims=True) + RMS_EPS)
            xh = x * rstd
            dxh = dy * g
            dx = rstd * (dxh - xh * jnp.mean(dxh * xh, axis=-1, keepdims=True))
            return dx, jnp.sum(dy * xh, axis=0, keepdims=True)

        dcq, dgq = rms_bwd(zz[:, :Q_LORA], gq_ref[...], dqn)
        dckv, dgkv = rms_bwd(zz[:, Q_LORA:Q_LORA + KV_LORA], gkv_ref[...], dkvn)
        dz_ref[:, :Q_LORA] = dcq.astype(BF16)
        dz_ref[:, Q_LORA:Q_LORA + KV_LORA] = dckv.astype(BF16)
        dz_ref[:, Q_LORA + KV_LORA:] = dkin[:, KV_LORA:].astype(BF16)

        @pl.when(i == 0)
        def _():
            dgq_ref[...] = jnp.zeros(dgq_ref.shape, F32)
            dgkv_ref[...] = jnp.zeros(dgkv_ref.shape, F32)

        dgq_ref[0:1, :] += dgq
        dgkv_ref[0:1, :] += dgkv

    def full(shape):
        return pl.BlockSpec(shape, lambda i: (0, 0))

    def rows(w, c=0):
        return pl.BlockSpec((tm, w), lambda i: (i, c))

    return pl.pallas_call(
        body, name=name, grid=(T // tm,),
        in_specs=[rows(D_QK), rows(D_QK), rows(D_MLA), rows(512, Z_MID // 512),
                  full((1, Q_LORA)), full((1, KV_LORA)),
                  full((D_QK, Q_LORA)), full((D_QK, Q_LORA)), full((D_MLA, KV_LORA)),
                  rows(LANE), rows(LANE), rows(LANE)],
        out_specs=[rows(512), rows(D_QK), rows(D_QK), rows(D_MLA), full((8, Q_LORA)), full((8, KV_LORA))],
        out_shape=[jax.ShapeDtypeStruct((T, 512), BF16), jax.ShapeDtypeStruct((T, D_QK), BF16),
                   jax.ShapeDtypeStruct((T, D_QK), BF16), jax.ShapeDtypeStruct((T, D_MLA), BF16),
                   jax.ShapeDtypeStruct((8, Q_LORA), F32), jax.ShapeDtypeStruct((8, KV_LORA), F32)],
        compiler_params=_params(("arbitrary",)),
    )(dq, dk, dv, z, gq, gkv, wq_t, wk_t, wv_t, *tabs)


FLAT_W = 1024


def _adamw(w, m, v, parts, part_index, *, name, tr):
    R = w.shape[0]
    bc1 = 1.0 - ADAM_B1 ** ADAM_STEP
    bc2 = 1.0 - ADAM_B2 ** ADAM_STEP
    n_parts = len(parts)

    def body(idx_ref, w_ref, m_ref, v_ref, *refs):
        g_refs = refs[:n_parts]
        g_out, d_out, m_out, v_out = refs[n_parts:]
        g = g_refs[0][...]
        for r in g_refs[1:]:
            g = g + r[...]
        wv = w_ref[...]
        mn = ADAM_B1 * m_ref[...] + (1.0 - ADAM_B1) * g
        vn = ADAM_B2 * v_ref[...] + (1.0 - ADAM_B2) * (g * g)
        m_hat = mn / bc1
        v_hat = vn / bc2
        g_out[...] = g
        d_out[...] = -ADAM_LR * (m_hat / (jnp.sqrt(v_hat) + ADAM_EPS) + ADAM_WD * wv)
        m_out[...] = mn
        v_out[...] = vn

    row = pl.BlockSpec((tr, FLAT_W), lambda i, idx: (i, 0))
    in_specs = [row, row, row]
    args = [w, m, v]
    for arr, slot in parts:
        if slot is None:
            in_specs.append(pl.BlockSpec((None, tr, FLAT_W), lambda i, idx: (idx[0], i, 0)))
        else:
            in_specs.append(pl.BlockSpec((None, tr, FLAT_W), lambda i, idx, slot=slot: (slot, i, 0)))
        args.append(arr)
    grid_spec = pltpu.PrefetchScalarGridSpec(
        num_scalar_prefetch=1, grid=(R // tr,), in_specs=in_specs, out_specs=[row] * 4)
    return pl.pallas_call(
        body, name=name, grid_spec=grid_spec,
        out_shape=[jax.ShapeDtypeStruct((R, FLAT_W), F32)] * 4,
        compiler_params=_params(("parallel",)),
    )(part_index, w, m, v, *args[3:])


def _pair_sum(g, recv, c_index, *, name, tr):
    _, R, _ = g.shape

    def body(c_ref, g_ref, r_ref, o_ref):
        o_ref[...] = g_ref[...] + r_ref[...]

    grid_spec = pltpu.PrefetchScalarGridSpec(
        num_scalar_prefetch=1, grid=(4, R // tr),
        in_specs=[pl.BlockSpec((None, tr, FLAT_W), lambda q, i, c: (2 * q + c[0], i, 0)),
                  pl.BlockSpec((None, tr, FLAT_W), lambda q, i, c: (q, i, 0))],
        out_specs=pl.BlockSpec((None, tr, FLAT_W), lambda q, i, c: (q, i, 0)))
    return pl.pallas_call(
        body, name=name, grid_spec=grid_spec,
        out_shape=jax.ShapeDtypeStruct((4, R, FLAT_W), F32),
        compiler_params=_params(("parallel", "parallel")),
    )(c_index, g, recv)


def _sum8(parts, *, name):
    _, R, _ = parts.shape

    def body(p_ref, o_ref):
        acc = p_ref[0]
        for d in range(1, N_DEV):
            acc = acc + p_ref[d]
        o_ref[...] = acc

    return pl.pallas_call(
        body, name=name, grid=(1,),
        in_specs=[pl.BlockSpec((N_DEV, R, FLAT_W), lambda i: (0, 0, 0))],
        out_specs=pl.BlockSpec((R, FLAT_W), lambda i: (0, 0)),
        out_shape=jax.ShapeDtypeStruct((R, FLAT_W), F32),
        compiler_params=_params(("arbitrary",)),
    )(parts)


_MESH_ID = pl.DeviceIdType.MESH
_ANY = pl.BlockSpec(memory_space=pl.ANY)


def _all_gather(shard, *, name):
    def body(x_ref, out_ref, send_sems, recv_sems, local_sem):
        x, y, c = lax.axis_index("x"), lax.axis_index("y"), lax.axis_index("c")
        me, sibling = (x, y, c), (x, y, 1 - c)
        chips = [(1 - x, y), (x, 1 - y), (1 - x, 1 - y)]

        def blk(px, py, pc):
            return out_ref.at[4 * px + 2 * py + pc]

        def copy(k, block, to, src=None):
            return pltpu.make_async_remote_copy(
                src_ref=blk(*block) if src is None else src, dst_ref=blk(*block),
                send_sem=send_sems.at[k], recv_sem=recv_sems.at[k],
                device_id=to, device_id_type=_MESH_ID)

        mine = pltpu.make_async_copy(x_ref, blk(*me), local_sem)
        mine.start()
        first = [copy(0, me, sibling, src=x_ref)]
        first += [copy(1 + j, me, (*chip, c), src=x_ref) for j, chip in enumerate(chips)]
        for cp in first:
            cp.start()
        passed = [copy(4 + j, (*chip, c), sibling) for j, chip in enumerate(chips)]
        for j, chip in enumerate(chips):
            copy(1 + j, (*chip, c), me).wait_recv()
            passed[j].start()
        copy(0, sibling, me).wait_recv()
        for j, chip in enumerate(chips):
            copy(4 + j, (*chip, 1 - c), me).wait_recv()
        for cp in first + passed:
            cp.wait_send()
        mine.wait()

    return pl.pallas_call(
        body, name=name,
        out_shape=jax.ShapeDtypeStruct((N_DEV,) + shard.shape, shard.dtype),
        in_specs=[_ANY], out_specs=_ANY,
        scratch_shapes=[pltpu.SemaphoreType.DMA((7,)), pltpu.SemaphoreType.DMA((7,)), pltpu.SemaphoreType.DMA],
        compiler_params=pltpu.CompilerParams(has_side_effects=True),
    )(shard)


def _exchange_sibling(g, *, name):
    _, R, C = g.shape

    def body(g_ref, out_ref, send_sems, recv_sems):
        x, y, c = lax.axis_index("x"), lax.axis_index("y"), lax.axis_index("c")
        copies = []
        for q in range(4):
            copies.append(pltpu.make_async_remote_copy(
                src_ref=g_ref.at[2 * q + (1 - c)], dst_ref=out_ref.at[q],
                send_sem=send_sems.at[q], recv_sem=recv_sems.at[q],
                device_id=(x, y, 1 - c), device_id_type=_MESH_ID))
        for cp in copies:
            cp.start()
        for cp in copies:
            cp.wait_recv()
        for cp in copies:
            cp.wait_send()

    return pl.pallas_call(
        body, name=name,
        out_shape=jax.ShapeDtypeStruct((4, R, C), g.dtype),
        in_specs=[_ANY], out_specs=_ANY,
        scratch_shapes=[pltpu.SemaphoreType.DMA((4,)), pltpu.SemaphoreType.DMA((4,))],
        compiler_params=pltpu.CompilerParams(has_side_effects=True),
    )(g)


def _exchange_chips(h, *, name):
    _, R, C = h.shape

    def body(h_ref, out_ref, send_sems, recv_sems):
        x, y, c = lax.axis_index("x"), lax.axis_index("y"), lax.axis_index("c")
        chips = [(1 - x, y), (x, 1 - y), (1 - x, 1 - y)]
        copies = []
        for k, (px, py) in enumerate(chips):
            copies.append(pltpu.make_async_remote_copy(
                src_ref=h_ref.at[2 * px + py], dst_ref=out_ref.at[k],
                send_sem=send_sems.at[k], recv_sem=recv_sems.at[k],
                device_id=(px, py, c), device_id_type=_MESH_ID))
        for cp in copies:
            cp.start()
        for cp in copies:
            cp.wait_recv()
        for cp in copies:
            cp.wait_send()

    return pl.pallas_call(
        body, name=name,
        out_shape=jax.ShapeDtypeStruct((3, R, C), h.dtype),
        in_specs=[_ANY], out_specs=_ANY,
        scratch_shapes=[pltpu.SemaphoreType.DMA((3,)), pltpu.SemaphoreType.DMA((3,))],
        compiler_params=pltpu.CompilerParams(has_side_effects=True),
    )(h)


_BIG = (("ffn1_w_up", 2), ("ffn1_w_down", 1), ("mix_w_in", 2), ("w_uq", 2), ("w_ukv", 2),
        ("w_br_conv", 2), ("w_br_mla", 2), ("w_o", 1), ("ffn2_w_up", 2), ("ffn2_w_down", 1))
_SMALL_SHARDED = (("meta_tokens", 1), ("mix_b_gate", 2), ("conv_w", 2), ("ln_g", 2), ("ln_b", 2))
_SMALL_REPL = ("q_norm_g", "kv_norm_g")


BIG_ROW_ALIGN = 64


def _pack(arrs, dtype, row_align=8):
    flat = jnp.concatenate([a.reshape(-1).astype(dtype) for a in arrs])
    n = flat.shape[0]
    rows = -(-n // (row_align * FLAT_W)) * row_align
    return jnp.pad(flat, (0, rows * FLAT_W - n)).reshape(rows, FLAT_W)


def _unpack(flat, shapes):
    flat = flat.reshape(-1)
    out, off = [], 0
    for s in shapes:
        n = int(np.prod(s))
        out.append(flat[off:off + n].reshape(s))
        off += n
    return out


def _unpack_gathered(gathered, shapes, axes):
    g2 = gathered.reshape(N_DEV, -1)
    out, off = [], 0
    for s, ax in zip(shapes, axes):
        n = int(np.prod(s))
        blocks = g2[:, off:off + n].reshape((N_DEV,) + tuple(s))
        full = jnp.moveaxis(blocks, 0, ax)
        out.append(full.reshape(tuple(s[:ax]) + (N_DEV * s[ax],) + tuple(s[ax + 1:])))
        off += n
    return out


def _to_dest_blocks(full, shard_shape, ax):
    s = tuple(shard_shape)
    split = full.reshape(s[:ax] + (N_DEV, s[ax]) + s[ax + 1:])
    return jnp.moveaxis(split, ax, 0).reshape(N_DEV, -1)


def _rope_tables(T):
    inv_freq = 1.0 / (ROPE_BASE ** (jnp.arange(0, QK_ROPE, 2, dtype=F32) / QK_ROPE))
    ang = jnp.arange(T, dtype=F32)[:, None] * inv_freq[None, :]
    cos, sin = jnp.cos(ang), jnp.sin(ang)
    half = QK_ROPE // 2
    ones = jnp.ones((T, QK_NOPE), F32)
    zeros = lambda w: jnp.zeros((T, w), F32)
    c = jnp.concatenate([ones, cos, cos, zeros(HEAD_PAD - QK_NOPE - QK_ROPE)], axis=1)
    s1 = jnp.concatenate([zeros(QK_NOPE + half), sin, zeros(HEAD_PAD - QK_NOPE - QK_ROPE)], axis=1)
    s2 = jnp.concatenate([zeros(QK_NOPE), -sin, zeros(HEAD_PAD - QK_NOPE - half)], axis=1)
    return c, s1, s2


def _layer_weights(W, l):
    w_in = W["mix_w_in"][l]
    w_in_p = jnp.concatenate([w_in[:, :Z_KR_END], jnp.zeros((D_MODEL, D_IN_PAD - D_IN_REAL), BF16),
                              w_in[:, Z_KR_END:]], axis=1)
    w_uq = W["w_uq"][l].reshape(Q_LORA, MLA_HEADS, QK_NOPE + QK_ROPE)
    wq = jnp.pad(w_uq, ((0, 0), (0, 0), (0, HEAD_PAD - QK_NOPE - QK_ROPE))).reshape(Q_LORA, D_QK)
    w_ukv = W["w_ukv"][l].reshape(KV_LORA, MLA_HEADS, QK_NOPE + V_HEAD)
    wk_top = jnp.pad(w_ukv[:, :, :QK_NOPE], ((0, 0), (0, 0), (0, HEAD_PAD - QK_NOPE))).reshape(KV_LORA, D_QK)
    place = np.zeros((Q_LORA - KV_LORA, MLA_HEADS, HEAD_PAD), np.float32)
    for r in range(QK_ROPE):
        place[r, :, QK_NOPE + r] = 1.0
    wk = jnp.concatenate([wk_top, jnp.asarray(place.reshape(Q_LORA - KV_LORA, D_QK), BF16)], axis=0)
    wv = w_ukv[:, :, QK_NOPE:].reshape(KV_LORA, D_MLA)
    lw = dict(
        ffn1_up=W["ffn1_w_up"][l], ffn1_down=W["ffn1_w_down"][l],
        ffn2_up=W["ffn2_w_up"][l], ffn2_down=W["ffn2_w_down"][l],
        w_in=w_in_p, wq=wq, wk=wk, wv=wv,
        wbc=W["w_br_conv"][l], wbm=W["w_br_mla"][l], wo=W["w_o"][l])
    for k in ("ffn1_up", "ffn1_down", "ffn2_up", "ffn2_down", "w_in", "wq", "wk", "wv", "wbc", "wbm", "wo"):
        lw[k + "_t"] = lw[k].T
    return lw


def _row8(v):
    return jnp.pad(v, ((0, 8 - v.shape[0]), (0, 0)))


def _local_step(x, tgt, W, S):
    t_real = N_META + x.shape[0]
    T = -(-t_real // ROW_ALIGN) * ROW_ALIGN
    pad = T - t_real
    tm = _tile(T, (384, 256, 128))
    tms = _tile(T, (768, 256, 128))
    blk = _tile(T, (768, 256, 128))
    tabs = _rope_tables(T)

    h0 = jnp.concatenate([S["meta_tokens"], x, jnp.zeros((pad, D_MODEL), F32)], axis=0)
    tgt_p = jnp.concatenate([jnp.zeros((N_META, D_MODEL), F32), tgt, jnp.zeros((pad, D_MODEL), F32)], axis=0)

    def ffn_fwd(h, hb, up, down, g, b, tag):
        gu, a = _ffn_up(hb, up, name=f"ffn_up_{tag}", tm=tms, tn=_tile(D_FF, (1408, 704)))
        r, y, yb = _mm_res_ln(a, down, h, g, b, scale=0.5, name=f"ffn_down_ln_{tag}", tm=tm)
        return dict(gu=gu, a=a, r=r), y, yb

    saved = []
    h, hb = h0, h0.astype(BF16)
    for l in range(DEPTH):
        lw = _layer_weights(W, l)
        sv = dict(lw=lw, h_in=h, h_in_b=hb)
        sv["f1"], h1, h1b = ffn_fwd(h, hb, lw["ffn1_up"], lw["ffn1_down"],
                                    S["ln_g"][l, 0:1], S["ln_b"][l, 0:1], f"{l}a")
        z = _mm(h1b, lw["w_in"], out_dtype=F32, name=f"mix_in_{l}", tm=tms, tn=1024, tk=D_MODEL)
        conv_w8 = _row8(S["conv_w"][l])
        bg8 = _row8(S["mix_b_gate"][l])
        yc = _conv_fwd(z, conv_w8, name=f"conv_fwd_{l}", tm=tms)
        gq, gkv = S["q_norm_g"][l:l + 1], S["kv_norm_g"][l:l + 1]
        q, k, v, qn, kin = _qkv_proj(z, gq, gkv, lw["wq"], lw["wk"], lw["wv"], tabs, name=f"qkv_proj_{l}", tm=tm)
        o, lse = _attn_fwd(q, k, v, name=f"attn_fwd_{l}", blk=blk)
        mg, pa, pb = _merge(yc, o, lw["wbc"], lw["wbm"], z, bg8, name=f"merge_{l}", tm=tm)
        r2, h2, h2b = _mm_res_ln(mg, lw["wo"], h1, S["ln_g"][l, 1:2], S["ln_b"][l, 1:2], scale=1.0,
                                 name=f"wo_ln_{l}", tm=tm)
        sv["f2"], h3, h3b = ffn_fwd(h2, h2b, lw["ffn2_up"], lw["ffn2_down"],
                                    S["ln_g"][l, 2:3], S["ln_b"][l, 2:3], f"{l}b")
        sv.update(h1b=h1b, z=z, conv_w8=conv_w8, bg8=bg8, yc=yc, gq=gq, gkv=gkv, q=q, k=k, v=v, qn=qn, kin=kin,
                  o=o, lse=lse, mg=mg, pa=pa, pb=pb, r2=r2, h2b=h2b)
        saved.append(sv)
        h, hb = h3, h3b

    dh, loss8 = _loss_head(h, tgt_p, t_real=t_real, name="loss_head", tm=tm)
    loss = loss8[0, 0]

    tk = blk
    grads = {n: [None] * DEPTH for n, _ in _BIG}
    for n in ("mix_b_gate", "conv_w", "q_norm_g", "kv_norm_g"):
        grads[n] = [None] * DEPTH
    grads["ln_g"] = [[None] * 3 for _ in range(DEPTH)]
    grads["ln_b"] = [[None] * 3 for _ in range(DEPTH)]

    def ffn_bwd(dy, f, h_in_b, up_t, down_t, g, tag):
        dr, dfb, dg, db = _ln_bwd(dy, f["r"], g, scale=0.5, name=f"ln_bwd_{tag}", tm=tm)
        dgu = _ffn_bwd_mid(dfb, down_t, f["gu"], name=f"ffn_bwd_mid_{tag}", tm=tms, tn=_tile(D_FF, (1408, 704)))
        d_down = _mm_tn(f["a"], dfb, name=f"dw_down_{tag}", tm=_tile(D_FF, (1408, 704)), tn=D_MODEL, tk=tk)
        d_up = _mm_tn(h_in_b, dgu, name=f"dw_up_{tag}", tm=D_MODEL, tn=_tile(D_FF, (1408, 704)), tk=tk)
        dh_in = _mm(dgu, up_t, out_dtype=F32, name=f"ffn_dx_{tag}", res=dr, res_scale=ALPHA,
                    tm=tms, tn=D_MODEL, tk=_tile(D_FF, (1408, 704)))
        return dh_in, d_up, d_down, dg[0], db[0]

    for l in reversed(range(DEPTH)):
        sv = saved[l]
        lw = sv["lw"]
        dh, grads["ffn2_w_up"][l], grads["ffn2_w_down"][l], grads["ln_g"][l][2], grads["ln_b"][l][2] = ffn_bwd(
            dh, sv["f2"], sv["h2b"], lw["ffn2_up_t"], lw["ffn2_down_t"], S["ln_g"][l, 2:3], f"{l}b")
        dr2, dmb, dg, db = _ln_bwd(dh, sv["r2"], S["ln_g"][l, 1:2], scale=1.0, name=f"ln_bwd_{l}m", tm=tm)
        grads["ln_g"][l][1], grads["ln_b"][l][1] = dg[0], db[0]
        grads["w_o"][l] = _mm_tn(sv["mg"], dmb, name=f"dw_o_{l}", tm=D_MODEL, tn=D_MODEL, tk=tk)
        dpa, dpb, dgate, dbg = _wo_bwd(dmb, lw["wo_t"], sv["z"], sv["bg8"], sv["pa"], sv["pb"], name=f"wo_bwd_{l}", tm=tm)
        grads["mix_b_gate"][l] = dbg[0:2]
        grads["w_br_conv"][l] = _mm_tn(sv["yc"], dpa, name=f"dw_br_conv_{l}", tm=D_CONV, tn=D_MODEL, tk=tk)
        grads["w_br_mla"][l] = _mm_tn(sv["o"], dpb, name=f"dw_br_mla_{l}", tm=D_MLA, tn=D_MODEL, tk=tk)
        dyc = _mm(dpa, lw["wbc_t"], out_dtype=F32, name=f"d_yconv_{l}", tm=tms, tn=D_CONV, tk=D_MODEL)
        dym = _mm(dpb, lw["wbm_t"], out_dtype=BF16, name=f"d_ymla_{l}", tm=tms, tn=D_MLA, tk=D_MODEL)
        dz_conv, dcw = _conv_bwd(dyc, sv["z"], sv["conv_w8"], name=f"conv_bwd_{l}", tm=tms)
        grads["conv_w"][l] = dcw[0:CONV_WIDTH]
        dq, dk, dv = _attn_bwd(sv["q"], sv["k"], sv["v"], sv["o"], dym, sv["lse"], name=f"attn_bwd_{l}", blk=blk)
        dz_mid, dqb, dkb, dvb, dgq, dgkv = _qkv_bwd(dq, dk, dv, sv["z"], sv["gq"], sv["gkv"], lw["wq_t"], lw["wk_t"],
                                                    lw["wv_t"], tabs, name=f"qkv_bwd_{l}", tm=tm)
        grads["q_norm_g"][l], grads["kv_norm_g"][l] = dgq[0], dgkv[0]
        d_wq = _mm_tn(sv["qn"], dqb, name=f"dw_uq_{l}", tm=Q_LORA, tn=D_QK, tk=tk)
        d_wk = _mm_tn(sv["kin"], dkb, name=f"dw_uk_{l}", tm=Q_LORA, tn=D_QK, tk=tk)
        d_wv = _mm_tn(sv["kin"], dvb, name=f"dw_uv_{l}", tm=Q_LORA, tn=D_MLA, tk=tk)
        grads["w_uq"][l] = d_wq.reshape(Q_LORA, MLA_HEADS, HEAD_PAD)[:, :, :QK_NOPE + QK_ROPE].reshape(Q_LORA, -1)
        d_kn = d_wk[:KV_LORA].reshape(KV_LORA, MLA_HEADS, HEAD_PAD)[:, :, :QK_NOPE]
        d_vv = d_wv[:KV_LORA].reshape(KV_LORA, MLA_HEADS, V_HEAD)
        grads["w_ukv"][l] = jnp.concatenate([d_kn, d_vv], axis=-1).reshape(KV_LORA, -1)
        dz = jnp.concatenate([dz_conv, dz_mid, dgate], axis=1)
        d_win = _mm_tn(sv["h1b"], dz, name=f"dw_in_{l}", tm=D_MODEL, tn=1024, tk=tk)
        grads["mix_w_in"][l] = jnp.concatenate([d_win[:, :Z_KR_END], d_win[:, Z_KR_END + D_IN_PAD - D_IN_REAL:]], axis=1)
        dh = _mm(dz, lw["w_in_t"], out_dtype=F32, name=f"mix_dx_{l}", res=dr2, res_scale=ALPHA,
                 tm=tms, tn=D_MODEL, tk=1024)
        dh, grads["ffn1_w_up"][l], grads["ffn1_w_down"][l], grads["ln_g"][l][0], grads["ln_b"][l][0] = ffn_bwd(
            dh, sv["f1"], sv["h_in_b"], lw["ffn1_up_t"], lw["ffn1_down_t"], S["ln_g"][l, 0:1], f"{l}a")

    out = {n: jnp.stack(grads[n]) for n, _ in _BIG}
    for n in ("mix_b_gate", "conv_w", "q_norm_g", "kv_norm_g"):
        out[n] = jnp.stack(grads[n])
    out["ln_g"] = jnp.stack([jnp.stack(g) for g in grads["ln_g"]])
    out["ln_b"] = jnp.stack([jnp.stack(g) for g in grads["ln_b"]])
    out["meta_tokens"] = dh[:N_META]
    return loss, dh[N_META:t_real], out


def kernel(x, meta_tokens, ffn1_w_up, ffn1_w_down, mix_w_in, mix_b_gate, conv_w, q_norm_g, w_uq, kv_norm_g, w_ukv, w_br_conv, w_br_mla, w_o, ffn2_w_up, ffn2_w_down, ln_g, ln_b, loss_target, m_meta_tokens, m_ffn1_w_up, m_ffn1_w_down, m_mix_w_in, m_mix_b_gate, m_conv_w, m_q_norm_g, m_w_uq, m_kv_norm_g, m_w_ukv, m_w_br_conv, m_w_br_mla, m_w_o, m_ffn2_w_up, m_ffn2_w_down, m_ln_g, m_ln_b, v_meta_tokens, v_ffn1_w_up, v_ffn1_w_down, v_mix_w_in, v_mix_b_gate, v_conv_w, v_q_norm_g, v_w_uq, v_kv_norm_g, v_w_ukv, v_w_br_conv, v_w_br_mla, v_w_o, v_ffn2_w_up, v_ffn2_w_down, v_ln_g, v_ln_b):
    names = ["meta_tokens", "ffn1_w_up", "ffn1_w_down", "mix_w_in", "mix_b_gate", "conv_w", "q_norm_g", "w_uq",
             "kv_norm_g", "w_ukv", "w_br_conv", "w_br_mla", "w_o", "ffn2_w_up", "ffn2_w_down", "ln_g", "ln_b"]
    w = dict(zip(names, (meta_tokens, ffn1_w_up, ffn1_w_down, mix_w_in, mix_b_gate, conv_w, q_norm_g, w_uq,
                         kv_norm_g, w_ukv, w_br_conv, w_br_mla, w_o, ffn2_w_up, ffn2_w_down, ln_g, ln_b)))
    m = dict(zip(names, (m_meta_tokens, m_ffn1_w_up, m_ffn1_w_down, m_mix_w_in, m_mix_b_gate, m_conv_w, m_q_norm_g,
                         m_w_uq, m_kv_norm_g, m_w_ukv, m_w_br_conv, m_w_br_mla, m_w_o, m_ffn2_w_up, m_ffn2_w_down,
                         m_ln_g, m_ln_b)))
    v = dict(zip(names, (v_meta_tokens, v_ffn1_w_up, v_ffn1_w_down, v_mix_w_in, v_mix_b_gate, v_conv_w, v_q_norm_g,
                         v_w_uq, v_kv_norm_g, v_w_ukv, v_w_br_conv, v_w_br_mla, v_w_o, v_ffn2_w_up, v_ffn2_w_down,
                         v_ln_g, v_ln_b)))
    ix, iy, ic = lax.axis_index("x"), lax.axis_index("y"), lax.axis_index("c")
    dev = 4 * ix + 2 * iy + ic

    big_names = [n for n, _ in _BIG]
    big_axes = [a for _, a in _BIG]
    big_shapes = [w[n].shape for n in big_names]
    small_names = [n for n, _ in _SMALL_SHARDED]
    small_axes = [a for _, a in _SMALL_SHARDED]
    small_shapes = [w[n].shape for n in small_names]
    big_g = _all_gather(_pack([w[n] for n in big_names], BF16), name="all_gather_weights")
    small_g = _all_gather(_pack([w[n] for n in small_names], F32), name="all_gather_small")
    W = dict(zip(big_names, _unpack_gathered(big_g, big_shapes, big_axes)))
    S = dict(zip(small_names, _unpack_gathered(small_g, small_shapes, small_axes)))
    S["q_norm_g"], S["kv_norm_g"] = q_norm_g, kv_norm_g

    loss_local, grad_x, G = _local_step(x[0], loss_target[0], W, S)
    loss = lax.psum(loss_local, ("x", "y", "c"))

    g_dest = jnp.concatenate([_to_dest_blocks(G[n], s, a) for n, s, a in zip(big_names, big_shapes, big_axes)], axis=1)
    n_flat = g_dest.shape[1]
    rows = -(-n_flat // (BIG_ROW_ALIGN * FLAT_W)) * BIG_ROW_ALIGN
    g_dest = jnp.pad(g_dest, ((0, 0), (0, rows * FLAT_W - n_flat))).reshape(N_DEV, rows, FLAT_W)
    tr = _tile(rows, (256, 208, 192, 128, 64))
    from_sibling = _exchange_sibling(g_dest, name="rs_sibling")
    chip_sum = _pair_sum(g_dest, from_sibling, ic.reshape(1).astype(jnp.int32), name="rs_sibling_sum", tr=tr)
    from_chips = _exchange_chips(chip_sum, name="rs_chips")
    my_chip = (2 * ix + iy).reshape(1).astype(jnp.int32)
    big_out = _adamw(_pack([w[n] for n in big_names], F32, BIG_ROW_ALIGN),
                     _pack([m[n] for n in big_names], F32, BIG_ROW_ALIGN),
                     _pack([v[n] for n in big_names], F32, BIG_ROW_ALIGN),
                     [(chip_sum, None), (from_chips, 0), (from_chips, 1), (from_chips, 2)], my_chip,
                     name="adamw_big", tr=tr)
    big_res = [dict(zip(big_names, _unpack(o, big_shapes))) for o in big_out]

    small_all = small_names + list(_SMALL_REPL)
    part = _pack([G[n] for n in small_all], F32)
    full_shapes = [G[n].shape for n in small_all]
    summed = _sum8(_all_gather(part, name="all_gather_small_grads"), name="sum_small_grads")
    g_full = dict(zip(small_all, _unpack(summed, full_shapes)))
    g_loc = []
    for n in small_all:
        if n in _SMALL_REPL:
            g_loc.append(g_full[n])
        else:
            ax = dict(_SMALL_SHARDED)[n]
            g_loc.append(lax.dynamic_slice_in_dim(g_full[n], dev * w[n].shape[ax], w[n].shape[ax], axis=ax))
    loc_shapes = [w[n].shape for n in small_all]
    g_pack = _pack(g_loc, F32)
    small_out = _adamw(_pack([w[n] for n in small_all], F32), _pack([m[n] for n in small_all], F32),
                       _pack([v[n] for n in small_all], F32), [(g_pack[None], 0)], jnp.zeros((1,), jnp.int32),
                       name="adamw_small", tr=g_pack.shape[0])
    small_res = [dict(zip(small_all, _unpack(o, loc_shapes))) for o in small_out]

    outs = [loss, grad_x[None]]
    for kind in range(4):
        for n in names:
            outs.append(big_res[kind][n] if n in big_res[kind] else small_res[kind][n])
    return tuple(outs)
```

```python
import functools

import numpy as np
import jax
import jax.numpy as jnp
from jax import lax
from jax.experimental import pallas as pl
from jax.experimental.pallas import tpu as pltpu

F32 = jnp.float32
BF16 = jnp.bfloat16

D_MODEL = 1024
DEPTH = 2
N_META = 16
D_CONV = 512
CONV_WIDTH = 3
MLA_HEADS = 8
QK_NOPE = 64
QK_ROPE = 32
V_HEAD = 64
Q_LORA = 256
KV_LORA = 128
D_MLA = MLA_HEADS * V_HEAD
ROPE_BASE = 10000.0
NEG_INF = -1e30
D_FF = 2816
ALPHA = (2 * DEPTH) ** 0.25
LN_EPS = 1e-5
RMS_EPS = 1e-6
ATTN_SCALE = (QK_NOPE + QK_ROPE) ** -0.5
ADAM_LR = 0.001
ADAM_B1 = 0.9
ADAM_B2 = 0.999
ADAM_EPS = 1e-08
ADAM_WD = 0.01
ADAM_STEP = 10

N_DEV = 8
HEAD_PAD = 128
HEADS_PER_STEP = 2
FF_BLK = 2 * D_FF // N_DEV
FF_HALF_BLOCKS = N_DEV // 2
D_QK = MLA_HEADS * HEAD_PAD
Z_CONV = 0
Z_MID = 1536
Z_GATE = 2048
D_IN_PAD = 4096
D_IN_REAL = 4000
Z_KR_END = Z_MID + Q_LORA + KV_LORA + QK_ROPE

V7X_VMEM_LIMIT = 56 * 1024 * 1024
LANE = 128
ROW_ALIGN = 256


def _tile(n, cands):
    for c in cands:
        if n % c == 0:
            return c
    raise ValueError(f"no tile for {n} in {cands}")


def _params(sem):
    return pltpu.CompilerParams(dimension_semantics=sem, vmem_limit_bytes=V7X_VMEM_LIMIT)


def _mm_call(a, b, a_spec, b_spec, *, out_shape, out_spec, acc_shape, grid, name, trans_b=False,
             res=None, res_spec=None, res_scale=1.0):
    nk = grid[2]
    has_res = res is not None
    out_dtype = out_shape.dtype
    dims = (((1,), (1,)), ((), ())) if trans_b else (((1,), (0,)), ((), ()))

    def body(*refs):
        if has_res:
            a_ref, b_ref, r_ref, o_ref, acc = refs
        else:
            a_ref, b_ref, o_ref, acc = refs
        k = pl.program_id(2)
        part = lax.dot_general(a_ref[...], b_ref[...], dims, preferred_element_type=F32)

        @pl.when(k == 0)
        def _():
            acc[...] = part

        @pl.when(k > 0)
        def _():
            acc[...] += part

        @pl.when(k == nk - 1)
        def _():
            out = acc[...]
            if has_res:
                out = out + res_scale * r_ref[...]
            o_ref[...] = out.astype(out_dtype)

    in_specs = [a_spec, b_spec]
    args = [a, b]
    if has_res:
        in_specs.append(res_spec)
        args.append(res)
    return pl.pallas_call(
        body, name=name, grid=grid, in_specs=in_specs, out_specs=out_spec, out_shape=out_shape,
        scratch_shapes=[pltpu.VMEM(acc_shape, F32)],
        compiler_params=_params(("parallel", "parallel", "arbitrary")),
    )(*args)


def _mm(a, b, *, out_dtype, name, trans_b=False, res=None, res_scale=1.0, tm, tn, tk):
    M, K = a.shape
    N = b.shape[0] if trans_b else b.shape[1]
    assert M % tm == 0 and N % tn == 0 and K % tk == 0
    b_spec = (pl.BlockSpec((tn, tk), lambda i, j, k: (j, k)) if trans_b
              else pl.BlockSpec((tk, tn), lambda i, j, k: (k, j)))
    tile = pl.BlockSpec((tm, tn), lambda i, j, k: (i, j))
    return _mm_call(a, b, pl.BlockSpec((tm, tk), lambda i, j, k: (i, k)), b_spec,
                    out_shape=jax.ShapeDtypeStruct((M, N), out_dtype), out_spec=tile, acc_shape=(tm, tn),
                    grid=(M // tm, N // tn, K // tk), name=name, trans_b=trans_b,
                    res=res, res_spec=tile, res_scale=res_scale)


def _mm_tn_call(a, b, a_spec, b_spec, *, out_shape, out_spec, grid, name):
    def body(a_ref, b_ref, o_ref):
        k = pl.program_id(2)
        part = lax.dot_general(a_ref[...], b_ref[...], (((0,), (0,)), ((), ())),
                               preferred_element_type=F32)

        @pl.when(k == 0)
        def _():
            o_ref[...] = part

        @pl.when(k > 0)
        def _():
            o_ref[...] += part

    return pl.pallas_call(
        body, name=name, grid=grid, in_specs=[a_spec, b_spec], out_specs=out_spec, out_shape=out_shape,
        compiler_params=_params(("parallel", "parallel", "arbitrary")),
    )(a, b)


def _mm_tn(a, b, *, name, tm, tn, tk):
    T, M = a.shape
    N = b.shape[1]
    assert M % tm == 0 and N % tn == 0 and T % tk == 0
    return _mm_tn_call(a, b, pl.BlockSpec((tk, tm), lambda i, j, k: (k, i)),
                       pl.BlockSpec((tk, tn), lambda i, j, k: (k, j)),
                       out_shape=jax.ShapeDtypeStruct((M, N), F32),
                       out_spec=pl.BlockSpec((tm, tn), lambda i, j, k: (i, j)),
                       grid=(M // tm, N // tn, T // tk), name=name)


def _ffn_up(hb, w_up8, l, *, name, tm):
    T = hb.shape[0]

    def body(h_ref, wg_ref, wu_ref, gu_ref, a_ref):
        h = h_ref[...]
        g = jnp.dot(h, wg_ref[...], preferred_element_type=F32)
        u = jnp.dot(h, wu_ref[...], preferred_element_type=F32)
        gu_ref[0] = g.astype(BF16)
        gu_ref[1] = u.astype(BF16)
        a_ref[...] = (g * jax.nn.sigmoid(g) * u).astype(BF16)

    return pl.pallas_call(
        body, name=name, grid=(FF_HALF_BLOCKS, T // tm),
        in_specs=[pl.BlockSpec((tm, D_MODEL), lambda j, i: (i, 0)),
                  pl.BlockSpec((None, None, D_MODEL, FF_BLK), lambda j, i: (j, l, 0, 0)),
                  pl.BlockSpec((None, None, D_MODEL, FF_BLK), lambda j, i: (j + FF_HALF_BLOCKS, l, 0, 0))],
        out_specs=[pl.BlockSpec((2, None, tm, FF_BLK), lambda j, i: (0, j, i, 0)),
                   pl.BlockSpec((None, tm, FF_BLK), lambda j, i: (j, i, 0))],
        out_shape=[jax.ShapeDtypeStruct((2, FF_HALF_BLOCKS, T, FF_BLK), BF16),
                   jax.ShapeDtypeStruct((FF_HALF_BLOCKS, T, FF_BLK), BF16)],
        compiler_params=_params(("parallel", "parallel")),
    )(hb, w_up8, w_up8)


def _mm_res_ln(a, w, res, g, b, *, scale, name, tm):
    split = a.ndim == 3
    if split:
        S, T, Ks = a.shape
        K = S * Ks
    else:
        T, K = a.shape

    def body(a_ref, w_ref, res_ref, g_ref, b_ref, r_ref, y_ref, yb_ref):
        if split:
            f = jnp.dot(a_ref[0], w_ref[0:Ks, :], preferred_element_type=F32)
            for s in range(1, S):
                f = f + jnp.dot(a_ref[s], w_ref[s * Ks:(s + 1) * Ks, :], preferred_element_type=F32)
        else:
            f = jnp.dot(a_ref[...], w_ref[...], preferred_element_type=F32)
        r = ALPHA * res_ref[...] + scale * f
        mu = jnp.mean(r, axis=-1, keepdims=True)
        xc = r - mu
        var = jnp.mean(xc * xc, axis=-1, keepdims=True)
        y = xc * lax.rsqrt(var + LN_EPS) * g_ref[...] + b_ref[...]
        r_ref[...] = r
        y_ref[...] = y
        yb_ref[...] = y.astype(BF16)

    row = pl.BlockSpec((tm, D_MODEL), lambda i: (i, 0))
    vec = pl.BlockSpec((1, D_MODEL), lambda i: (0, 0))
    return pl.pallas_call(
        body, name=name, grid=(T // tm,),
        in_specs=[pl.BlockSpec((S, tm, Ks), lambda i: (0, i, 0)) if split else pl.BlockSpec((tm, K), lambda i: (i, 0)),
                  pl.BlockSpec((K, D_MODEL), lambda i: (0, 0)), row, vec, vec],
        out_specs=[row, row, row],
        out_shape=[jax.ShapeDtypeStruct((T, D_MODEL), F32), jax.ShapeDtypeStruct((T, D_MODEL), F32),
                   jax.ShapeDtypeStruct((T, D_MODEL), BF16)],
        compiler_params=_params(("parallel",)),
    )(a, w, res, g, b)


def _conv_fwd(z, conv_w8, *, name, tm):
    T = z.shape[0]
    hb = tm // 8

    def body(b_ref, c_ref, h_ref, cp_ref, hp_ref, w_ref, y_ref):
        i = pl.program_id(0)
        u = c_ref[...] * h_ref[...]
        up = jnp.where(i > 0, cp_ref[...] * hp_ref[...], 0.0)
        ue = jnp.concatenate([up, u], axis=0)
        s1 = pltpu.roll(ue, 1, 0)[8:]
        s2 = pltpu.roll(ue, 2, 0)[8:]
        w = w_ref[...]
        conv = w[0:1] * s2 + w[1:2] * s1 + w[2:3] * u
        y_ref[...] = (b_ref[...] * conv).astype(BF16)

    def col(c):
        return pl.BlockSpec((tm, D_CONV), lambda i: (i, c))

    def prev(c):
        return pl.BlockSpec((8, D_CONV), lambda i: (jnp.maximum(i * hb - 1, 0), c))

    return pl.pallas_call(
        body, name=name, grid=(T // tm,),
        in_specs=[col(0), col(1), col(2), prev(1), prev(2), pl.BlockSpec((8, D_CONV), lambda i: (0, 0))],
        out_specs=pl.BlockSpec((tm, D_CONV), lambda i: (i, 0)),
        out_shape=jax.ShapeDtypeStruct((T, D_CONV), BF16),
        compiler_params=_params(("parallel",)),
    )(z, z, z, z, z, conv_w8)


def _rope(x, c, s1, s2):
    n = x.shape[-1]
    return x * c + pltpu.roll(x, 16, 1) * s1 + pltpu.roll(x, n - 16, 1) * s2


def _rope_t(d, c, s1, s2):
    n = d.shape[-1]
    return d * c + pltpu.roll(d * s1, n - 16, 1) + pltpu.roll(d * s2, 16, 1)


def _rms(x, g):
    rstd = lax.rsqrt(jnp.mean(x * x, axis=-1, keepdims=True) + RMS_EPS)
    return x * rstd * g


def _qkv_proj(z, gq, gkv, wq, wk, wv, tabs, *, name, tm):
    T = z.shape[0]

    def body(z_ref, gq_ref, gkv_ref, wq_ref, wk_ref, wv_ref, c_ref, s1_ref, s2_ref,
             q_ref, k_ref, v_ref, qn_ref, kin_ref):
        zz = z_ref[...]
        qn = _rms(zz[:, :Q_LORA], gq_ref[...]).astype(BF16)
        kvn = _rms(zz[:, Q_LORA:Q_LORA + KV_LORA], gkv_ref[...]).astype(BF16)
        kin = jnp.concatenate([kvn, zz[:, Q_LORA + KV_LORA:].astype(BF16)], axis=-1)
        c = jnp.tile(c_ref[...], (1, MLA_HEADS))
        s1 = jnp.tile(s1_ref[...], (1, MLA_HEADS))
        s2 = jnp.tile(s2_ref[...], (1, MLA_HEADS))
        qpre = jnp.dot(qn, wq_ref[...], preferred_element_type=F32)
        kpre = jnp.dot(kin, wk_ref[...], preferred_element_type=F32)
        q_ref[...] = (_rope(qpre, c, s1, s2) * ATTN_SCALE).astype(BF16)
        k_ref[...] = _rope(kpre, c, s1, s2).astype(BF16)
        v_ref[...] = jnp.dot(kvn, wv_ref[...], preferred_element_type=F32).astype(BF16)
        qn_ref[...] = qn
        kin_ref[...] = kin

    def full(shape):
        return pl.BlockSpec(shape, lambda i: (0, 0))

    def rows(w, c=0):
        return pl.BlockSpec((tm, w), lambda i: (i, c))

    return pl.pallas_call(
        body, name=name, grid=(T // tm,),
        in_specs=[rows(512, Z_MID // 512), full((1, Q_LORA)), full((1, KV_LORA)),
                  full((Q_LORA, D_QK)), full((Q_LORA, D_QK)), full((KV_LORA, D_MLA)),
                  rows(LANE), rows(LANE), rows(LANE)],
        out_specs=[rows(D_QK), rows(D_QK), rows(D_MLA), rows(Q_LORA), rows(Q_LORA)],
        out_shape=[jax.ShapeDtypeStruct((T, D_QK), BF16), jax.ShapeDtypeStruct((T, D_QK), BF16),
                   jax.ShapeDtypeStruct((T, D_MLA), BF16), jax.ShapeDtypeStruct((T, Q_LORA), BF16),
                   jax.ShapeDtypeStruct((T, Q_LORA), BF16)],
        compiler_params=_params(("parallel",)),
    )(z, gq, gkv, wq, wk, wv, *tabs)


def _causal_mask(s, i, j, blk):
    row = lax.broadcasted_iota(jnp.int32, s.shape, 0) + i * blk
    col = lax.broadcasted_iota(jnp.int32, s.shape, 1) + j * blk
    return jnp.where(col <= row, s, NEG_INF)


def _attn_fwd(q, k, v, *, name, blk):
    T = q.shape[0]
    n = T // blk
    hp = HEADS_PER_STEP
    qi = np.array([i for i in range(n) for j in range(i + 1)], np.int32)
    kj = np.array([j for i in range(n) for j in range(i + 1)], np.int32)

    def body(qi_ref, kj_ref, q_ref, k_ref, v_ref, o_ref, lse_ref, m_sc, l_sc, acc_sc):
        s_id = pl.program_id(1)
        i = qi_ref[s_id]
        j = kj_ref[s_id]

        @pl.when(j == 0)
        def _():
            m_sc[...] = jnp.full(m_sc.shape, NEG_INF, F32)
            l_sc[...] = jnp.zeros(l_sc.shape, F32)
            acc_sc[...] = jnp.zeros(acc_sc.shape, F32)

        for hh in range(hp):
            qh = q_ref[:, hh * HEAD_PAD:(hh + 1) * HEAD_PAD]
            kh = k_ref[:, hh * HEAD_PAD:(hh + 1) * HEAD_PAD]
            vh = v_ref[:, hh * V_HEAD:(hh + 1) * V_HEAD]
            s = lax.dot_general(qh, kh, (((1,), (1,)), ((), ())), preferred_element_type=F32)
            s = _causal_mask(s, i, j, blk)
            m_old = m_sc[hh]
            m_new = jnp.maximum(m_old, jnp.max(s, axis=-1, keepdims=True))
            a = jnp.exp(m_old - m_new)
            p = jnp.exp(s - m_new)
            l_sc[hh] = a * l_sc[hh] + jnp.sum(p, axis=-1, keepdims=True)
            acc_sc[hh] = a * acc_sc[hh] + jnp.dot(p.astype(BF16), vh, preferred_element_type=F32)
            m_sc[hh] = m_new

        @pl.when(j == i)
        def _():
            for hh in range(hp):
                l = l_sc[hh]
                o_ref[:, hh * V_HEAD:(hh + 1) * V_HEAD] = (acc_sc[hh] / l).astype(BF16)
                lse_ref[hh] = m_sc[hh] + jnp.log(l)

    grid_spec = pltpu.PrefetchScalarGridSpec(
        num_scalar_prefetch=2, grid=(MLA_HEADS // hp, len(qi)),
        in_specs=[pl.BlockSpec((blk, hp * HEAD_PAD), lambda g, s, qi, kj: (qi[s], g)),
                  pl.BlockSpec((blk, hp * HEAD_PAD), lambda g, s, qi, kj: (kj[s], g)),
                  pl.BlockSpec((blk, hp * V_HEAD), lambda g, s, qi, kj: (kj[s], g))],
        out_specs=[pl.BlockSpec((blk, hp * V_HEAD), lambda g, s, qi, kj: (qi[s], g)),
                   pl.BlockSpec((hp, blk, 1), lambda g, s, qi, kj: (g, qi[s], 0))],
        scratch_shapes=[pltpu.VMEM((hp, blk, 1), F32), pltpu.VMEM((hp, blk, 1), F32),
                        pltpu.VMEM((hp, blk, V_HEAD), F32)])
    return pl.pallas_call(
        body, name=name, grid_spec=grid_spec,
        out_shape=[jax.ShapeDtypeStruct((T, D_MLA), BF16), jax.ShapeDtypeStruct((MLA_HEADS, T, 1), F32)],
        compiler_params=_params(("parallel", "arbitrary")),
    )(jnp.asarray(qi), jnp.asarray(kj), q, k, v)


def _merge(yc, ym, wbc, wbm, z, bg, *, name, tm):
    T = yc.shape[0]

    def body(yc_ref, ym_ref, wbc_ref, wbm_ref, gc_ref, gm_ref, bg_ref, mg_ref, pa_ref, pb_ref):
        pa = jnp.dot(yc_ref[...], wbc_ref[...], preferred_element_type=F32)
        pb = jnp.dot(ym_ref[...], wbm_ref[...], preferred_element_type=F32)
        bgv = bg_ref[...]
        sa = jax.nn.sigmoid(gc_ref[...] + bgv[0:1])
        sb = jax.nn.sigmoid(gm_ref[...] + bgv[1:2])
        mg_ref[...] = (sa * pa + sb * pb).astype(BF16)
        pa_ref[...] = pa.astype(BF16)
        pb_ref[...] = pb.astype(BF16)

    row = pl.BlockSpec((tm, D_MODEL), lambda i: (i, 0))
    return pl.pallas_call(
        body, name=name, grid=(T // tm,),
        in_specs=[pl.BlockSpec((tm, D_CONV), lambda i: (i, 0)), pl.BlockSpec((tm, D_MLA), lambda i: (i, 0)),
                  pl.BlockSpec((D_CONV, D_MODEL), lambda i: (0, 0)), pl.BlockSpec((D_MLA, D_MODEL), lambda i: (0, 0)),
                  pl.BlockSpec((tm, D_MODEL), lambda i: (i, Z_GATE // D_MODEL)),
                  pl.BlockSpec((tm, D_MODEL), lambda i: (i, Z_GATE // D_MODEL + 1)),
                  pl.BlockSpec((8, D_MODEL), lambda i: (0, 0))],
        out_specs=[row, row, row],
        out_shape=[jax.ShapeDtypeStruct((T, D_MODEL), BF16)] * 3,
        compiler_params=_params(("parallel",)),
    )(yc, ym, wbc, wbm, z, z, bg)


def _loss_head(h, tgt, *, t_real, name, tm):
    T = h.shape[0]

    def body(h_ref, t_ref, dy_ref, loss_ref):
        i = pl.program_id(0)
        row = lax.broadcasted_iota(jnp.int32, (tm, 1), 0) + i * tm
        valid = (row >= N_META) & (row < t_real)
        err = jnp.where(valid, h_ref[...] - t_ref[...], 0.0)
        dy_ref[...] = err * (1.0 / D_MODEL)
        part = 0.5 * jnp.sum(jnp.sum(err * err, axis=-1, keepdims=True) * (1.0 / D_MODEL), axis=0, keepdims=True)

        @pl.when(i == 0)
        def _():
            loss_ref[...] = jnp.zeros(loss_ref.shape, F32)

        loss_ref[...] += jnp.broadcast_to(part, loss_ref.shape)

    row_spec = pl.BlockSpec((tm, D_MODEL), lambda i: (i, 0))
    return pl.pallas_call(
        body, name=name, grid=(T // tm,),
        in_specs=[row_spec, row_spec],
        out_specs=[row_spec, pl.BlockSpec((8, LANE), lambda i: (0, 0))],
        out_shape=[jax.ShapeDtypeStruct((T, D_MODEL), F32), jax.ShapeDtypeStruct((8, LANE), F32)],
        compiler_params=_params(("arbitrary",)),
    )(h, tgt)


def _ln_bwd(dy, r, g, *, scale, name, tm):
    T = dy.shape[0]

    def body(dy_ref, r_ref, g_ref, dr_ref, drb_ref, dg_ref, db_ref):
        i = pl.program_id(0)
        rr = r_ref[...]
        dyv = dy_ref[...]
        mu = jnp.mean(rr, axis=-1, keepdims=True)
        xc = rr - mu
        rstd = lax.rsqrt(jnp.mean(xc * xc, axis=-1, keepdims=True) + LN_EPS)
        xh = xc * rstd
        dxh = dyv * g_ref[...]
        m1 = jnp.mean(dxh, axis=-1, keepdims=True)
        m2 = jnp.mean(dxh * xh, axis=-1, keepdims=True)
        dr = rstd * (dxh - m1 - xh * m2)
        dr_ref[...] = dr
        drb_ref[...] = (scale * dr).astype(BF16)

        @pl.when(i == 0)
        def _():
            dg_ref[...] = jnp.zeros(dg_ref.shape, F32)
            db_ref[...] = jnp.zeros(db_ref.shape, F32)

        dg_ref[0:1, :] += jnp.sum(dyv * xh, axis=0, keepdims=True)
        db_ref[0:1, :] += jnp.sum(dyv, axis=0, keepdims=True)

    row = pl.BlockSpec((tm, D_MODEL), lambda i: (i, 0))
    acc = pl.BlockSpec((8, D_MODEL), lambda i: (0, 0))
    return pl.pallas_call(
        body, name=name, grid=(T // tm,),
        in_specs=[row, row, pl.BlockSpec((1, D_MODEL), lambda i: (0, 0))],
        out_specs=[row, row, acc, acc],
        out_shape=[jax.ShapeDtypeStruct((T, D_MODEL), F32), jax.ShapeDtypeStruct((T, D_MODEL), BF16),
                   jax.ShapeDtypeStruct((8, D_MODEL), F32), jax.ShapeDtypeStruct((8, D_MODEL), F32)],
        compiler_params=_params(("arbitrary",)),
    )(dy, r, g)


def _ffn_bwd_mid(dfb, w_down, gu, *, name, tm):
    T = dfb.shape[0]

    def body(df_ref, w_ref, gu_ref, o_ref):
        da = lax.dot_general(df_ref[...], w_ref[...], (((1,), (1,)), ((), ())), preferred_element_type=F32)
        g = gu_ref[0].astype(F32)
        u = gu_ref[1].astype(F32)
        sg = jax.nn.sigmoid(g)
        o_ref[0] = (da * u * (sg * (1.0 + g * (1.0 - sg)))).astype(BF16)
        o_ref[1] = (da * (g * sg)).astype(BF16)

    return pl.pallas_call(
        body, name=name, grid=(FF_HALF_BLOCKS, T // tm),
        in_specs=[pl.BlockSpec((tm, D_MODEL), lambda j, i: (i, 0)),
                  pl.BlockSpec((FF_BLK, D_MODEL), lambda j, i: (j, 0)),
                  pl.BlockSpec((2, None, tm, FF_BLK), lambda j, i: (0, j, i, 0))],
        out_specs=pl.BlockSpec((2, None, tm, FF_BLK), lambda j, i: (0, j, i, 0)),
        out_shape=jax.ShapeDtypeStruct((2, FF_HALF_BLOCKS, T, FF_BLK), BF16),
        compiler_params=_params(("parallel", "parallel")),
    )(dfb, w_down, gu)


_NT = (((1,), (1,)), ((), ()))


def _wo_bwd(dmb, wo, z, bg, pa, pb, *, name, tm):
    T = dmb.shape[0]

    def body(dm_ref, w_ref, gc_ref, gm_ref, bg_ref, pa_ref, pb_ref, dpa_ref, dpb_ref, dg_ref, dbg_ref):
        i = pl.program_id(0)
        dm = lax.dot_general(dm_ref[...], w_ref[...], _NT, preferred_element_type=F32)
        bgv = bg_ref[...]
        sa = jax.nn.sigmoid(gc_ref[...] + bgv[0:1])
        sb = jax.nn.sigmoid(gm_ref[...] + bgv[1:2])
        dpa_ref[...] = (dm * sa).astype(BF16)
        dpb_ref[...] = (dm * sb).astype(BF16)
        dga = dm * pa_ref[...].astype(F32) * (sa * (1.0 - sa))
        dgb = dm * pb_ref[...].astype(F32) * (sb * (1.0 - sb))
        dg_ref[:, :D_MODEL] = dga.astype(BF16)
        dg_ref[:, D_MODEL:] = dgb.astype(BF16)

        @pl.when(i == 0)
        def _():
            dbg_ref[...] = jnp.zeros(dbg_ref.shape, F32)

        dbg_ref[0:1, :] += jnp.sum(dga, axis=0, keepdims=True)
        dbg_ref[1:2, :] += jnp.sum(dgb, axis=0, keepdims=True)

    row = pl.BlockSpec((tm, D_MODEL), lambda i: (i, 0))
    return pl.pallas_call(
        body, name=name, grid=(T // tm,),
        in_specs=[row, pl.BlockSpec((D_MODEL, D_MODEL), lambda i: (0, 0)),
                  pl.BlockSpec((tm, D_MODEL), lambda i: (i, Z_GATE // D_MODEL)),
                  pl.BlockSpec((tm, D_MODEL), lambda i: (i, Z_GATE // D_MODEL + 1)),
                  pl.BlockSpec((8, D_MODEL), lambda i: (0, 0)), row, row],
        out_specs=[row, row, pl.BlockSpec((tm, 2 * D_MODEL), lambda i: (i, 0)),
                   pl.BlockSpec((8, D_MODEL), lambda i: (0, 0))],
        out_shape=[jax.ShapeDtypeStruct((T, D_MODEL), BF16), jax.ShapeDtypeStruct((T, D_MODEL), BF16),
                   jax.ShapeDtypeStruct((T, 2 * D_MODEL), BF16), jax.ShapeDtypeStruct((8, D_MODEL), F32)],
        compiler_params=_params(("arbitrary",)),
    )(dmb, wo, z, z, bg, pa, pb)


def _conv_bwd(dy, z, conv_w8, *, name, tm):
    T = dy.shape[0]
    n = T // tm
    hb = tm // 8

    def body(dy_ref, b_ref, c_ref, h_ref, cp_ref, hp_ref, dyn_ref, bn_ref, w_ref, dz_ref, dw_ref):
        i = pl.program_id(0)
        u = c_ref[...] * h_ref[...]
        up = jnp.where(i > 0, cp_ref[...] * hp_ref[...], 0.0)
        ue = jnp.concatenate([up, u], axis=0)
        s1 = pltpu.roll(ue, 1, 0)[8:]
        s2 = pltpu.roll(ue, 2, 0)[8:]
        w = w_ref[...]
        conv = w[0:1] * s2 + w[1:2] * s1 + w[2:3] * u
        dyv = dy_ref[...]
        e = dyv * b_ref[...]
        en = jnp.where(i < n - 1, dyn_ref[...] * bn_ref[...], 0.0)
        ee = jnp.concatenate([e, en], axis=0)
        e1 = pltpu.roll(ee, tm + 8 - 1, 0)[:tm]
        e2 = pltpu.roll(ee, tm + 8 - 2, 0)[:tm]
        du = w[2:3] * e + w[1:2] * e1 + w[0:1] * e2
        dz_ref[:, 0:D_CONV] = (dyv * conv).astype(BF16)
        dz_ref[:, D_CONV:2 * D_CONV] = (du * h_ref[...]).astype(BF16)
        dz_ref[:, 2 * D_CONV:] = (du * c_ref[...]).astype(BF16)

        @pl.when(i == 0)
        def _():
            dw_ref[...] = jnp.zeros(dw_ref.shape, F32)

        dw_ref[0:1, :] += jnp.sum(e * s2, axis=0, keepdims=True)
        dw_ref[1:2, :] += jnp.sum(e * s1, axis=0, keepdims=True)
        dw_ref[2:3, :] += jnp.sum(e * u, axis=0, keepdims=True)

    def col(c):
        return pl.BlockSpec((tm, D_CONV), lambda i: (i, c))

    def prev(c):
        return pl.BlockSpec((8, D_CONV), lambda i: (jnp.maximum(i * hb - 1, 0), c))

    def nxt(c):
        return pl.BlockSpec((8, D_CONV), lambda i: (jnp.minimum((i + 1) * hb, T // 8 - 1), c))

    return pl.pallas_call(
        body, name=name, grid=(n,),
        in_specs=[col(0), col(0), col(1), col(2), prev(1), prev(2), nxt(0), nxt(0),
                  pl.BlockSpec((8, D_CONV), lambda i: (0, 0))],
        out_specs=[pl.BlockSpec((tm, 3 * D_CONV), lambda i: (i, 0)), pl.BlockSpec((8, D_CONV), lambda i: (0, 0))],
        out_shape=[jax.ShapeDtypeStruct((T, 3 * D_CONV), BF16), jax.ShapeDtypeStruct((8, D_CONV), F32)],
        compiler_params=_params(("arbitrary",)),
    )(dy, z, z, z, z, z, dy, z, conv_w8)


def _attn_bwd(q, k, v, o, do, lse, *, name, blk):
    T = q.shape[0]
    n = T // blk
    hp = HEADS_PER_STEP
    qi = np.array([i for j in range(n) for i in range(j, n)], np.int32)
    kj = np.array([j for j in range(n) for i in range(j, n)], np.int32)

    def body(qi_ref, kj_ref, q_ref, k_ref, v_ref, o_ref, do_ref, lse_ref, dq_ref, dk_ref, dv_ref, dk_sc, dv_sc):
        s_id = pl.program_id(1)
        i = qi_ref[s_id]
        j = kj_ref[s_id]

        @pl.when(s_id == 0)
        def _():
            dq_ref[...] = jnp.zeros(dq_ref.shape, F32)

        @pl.when(i == j)
        def _():
            dk_sc[...] = jnp.zeros(dk_sc.shape, F32)
            dv_sc[...] = jnp.zeros(dv_sc.shape, F32)

        rows = pl.ds(pl.multiple_of(i * blk, blk), blk)
        for hh in range(hp):
            hs = slice(hh * HEAD_PAD, (hh + 1) * HEAD_PAD)
            vs = slice(hh * V_HEAD, (hh + 1) * V_HEAD)
            qh = q_ref[:, hs]
            kh = k_ref[:, hs]
            vh = v_ref[:, vs]
            doh = do_ref[:, vs]
            oh = o_ref[:, vs]
            s = lax.dot_general(qh, kh, (((1,), (1,)), ((), ())), preferred_element_type=F32)
            s = _causal_mask(s, i, j, blk)
            p = jnp.exp(s - lse_ref[hh])
            pb = p.astype(BF16)
            dv_sc[hh] += lax.dot_general(pb, doh, (((0,), (0,)), ((), ())), preferred_element_type=F32)
            dp = lax.dot_general(doh, vh, (((1,), (1,)), ((), ())), preferred_element_type=F32)
            delta = jnp.sum(doh.astype(F32) * oh.astype(F32), axis=-1, keepdims=True)
            ds = (p * (dp - delta)).astype(BF16)
            dk_sc[hh] += lax.dot_general(ds, qh, (((0,), (0,)), ((), ())), preferred_element_type=F32)
            dq_ref[rows, hs] += jnp.dot(ds, kh, preferred_element_type=F32)

        @pl.when(i == n - 1)
        def _():
            for hh in range(hp):
                dk_ref[:, hh * HEAD_PAD:(hh + 1) * HEAD_PAD] = dk_sc[hh]
                dv_ref[:, hh * V_HEAD:(hh + 1) * V_HEAD] = dv_sc[hh]

    wq = hp * HEAD_PAD
    wv = hp * V_HEAD
    grid_spec = pltpu.PrefetchScalarGridSpec(
        num_scalar_prefetch=2, grid=(MLA_HEADS // hp, len(qi)),
        in_specs=[pl.BlockSpec((blk, wq), lambda g, s, qi, kj: (qi[s], g)),
                  pl.BlockSpec((blk, wq), lambda g, s, qi, kj: (kj[s], g)),
                  pl.BlockSpec((blk, wv), lambda g, s, qi, kj: (kj[s], g)),
                  pl.BlockSpec((blk, wv), lambda g, s, qi, kj: (qi[s], g)),
                  pl.BlockSpec((blk, wv), lambda g, s, qi, kj: (qi[s], g)),
                  pl.BlockSpec((hp, blk, 1), lambda g, s, qi, kj: (g, qi[s], 0))],
        out_specs=[pl.BlockSpec((T, wq), lambda g, s, qi, kj: (0, g)),
                   pl.BlockSpec((blk, wq), lambda g, s, qi, kj: (kj[s], g)),
                   pl.BlockSpec((blk, wv), lambda g, s, qi, kj: (kj[s], g))],
        scratch_shapes=[pltpu.VMEM((hp, blk, HEAD_PAD), F32), pltpu.VMEM((hp, blk, V_HEAD), F32)])
    return pl.pallas_call(
        body, name=name, grid_spec=grid_spec,
        out_shape=[jax.ShapeDtypeStruct((T, D_QK), F32), jax.ShapeDtypeStruct((T, D_QK), F32),
                   jax.ShapeDtypeStruct((T, D_MLA), F32)],
        compiler_params=_params(("parallel", "arbitrary")),
    )(jnp.asarray(qi), jnp.asarray(kj), q, k, v, o, do, lse)


def _qkv_bwd(dq, dk, dv, z, gq, gkv, wq, wk, wv, tabs, *, name, tm):
    T = dq.shape[0]

    def body(dq_ref, dk_ref, dv_ref, z_ref, gq_ref, gkv_ref, wq_ref, wk_ref, wv_ref, c_ref, s1_ref, s2_ref,
             dz_ref, dqb_ref, dkb_ref, dvb_ref, dgq_ref, dgkv_ref):
        i = pl.program_id(0)
        c = jnp.tile(c_ref[...], (1, MLA_HEADS))
        s1 = jnp.tile(s1_ref[...], (1, MLA_HEADS))
        s2 = jnp.tile(s2_ref[...], (1, MLA_HEADS))
        dqp = _rope_t(dq_ref[...] * ATTN_SCALE, c, s1, s2).astype(BF16)
        dkp = _rope_t(dk_ref[...], c, s1, s2).astype(BF16)
        dvb = dv_ref[...].astype(BF16)
        dqb_ref[...] = dqp
        dkb_ref[...] = dkp
        dvb_ref[...] = dvb
        dqn = lax.dot_general(dqp, wq_ref[...], _NT, preferred_element_type=F32)
        dkin = lax.dot_general(dkp, wk_ref[...], _NT, preferred_element_type=F32)
        dkvn = dkin[:, :KV_LORA] + lax.dot_general(dvb, wv_ref[...], _NT, preferred_element_type=F32)
        zz = z_ref[...]

        def rms_bwd(x, g, dy):
            rstd = lax.rsqrt(jnp.mean(x * x, axis=-1, keepdims=True) + RMS_EPS)
            xh = x * rstd
            dxh = dy * g
            dx = rstd * (dxh - xh * jnp.mean(dxh * xh, axis=-1, keepdims=True))
            return dx, jnp.sum(dy * xh, axis=0, keepdims=True)

        dcq, dgq = rms_bwd(zz[:, :Q_LORA], gq_ref[...], dqn)
        dckv, dgkv = rms_bwd(zz[:, Q_LORA:Q_LORA + KV_LORA], gkv_ref[...], dkvn)
        dz_ref[:, :Q_LORA] = dcq.astype(BF16)
        dz_ref[:, Q_LORA:Q_LORA + KV_LORA] = dckv.astype(BF16)
        dz_ref[:, Q_LORA + KV_LORA:] = dkin[:, KV_LORA:].astype(BF16)

        @pl.when(i == 0)
        def _():
            dgq_ref[...] = jnp.zeros(dgq_ref.shape, F32)
            dgkv_ref[...] = jnp.zeros(dgkv_ref.shape, F32)

        dgq_ref[0:1, :] += dgq
        dgkv_ref[0:1, :] += dgkv

    def full(shape):
        return pl.BlockSpec(shape, lambda i: (0, 0))

    def rows(w, c=0):
        return pl.BlockSpec((tm, w), lambda i: (i, c))

    return pl.pallas_call(
        body, name=name, grid=(T // tm,),
        in_specs=[rows(D_QK), rows(D_QK), rows(D_MLA), rows(512, Z_MID // 512),
                  full((1, Q_LORA)), full((1, KV_LORA)),
                  full((Q_LORA, D_QK)), full((Q_LORA, D_QK)), full((KV_LORA, D_MLA)),
                  rows(LANE), rows(LANE), rows(LANE)],
        out_specs=[rows(512), rows(D_QK), rows(D_QK), rows(D_MLA), full((8, Q_LORA)), full((8, KV_LORA))],
        out_shape=[jax.ShapeDtypeStruct((T, 512), BF16), jax.ShapeDtypeStruct((T, D_QK), BF16),
                   jax.ShapeDtypeStruct((T, D_QK), BF16), jax.ShapeDtypeStruct((T, D_MLA), BF16),
                   jax.ShapeDtypeStruct((8, Q_LORA), F32), jax.ShapeDtypeStruct((8, KV_LORA), F32)],
        compiler_params=_params(("arbitrary",)),
    )(dq, dk, dv, z, gq, gkv, wq, wk, wv, *tabs)


FLAT_W = 1024


ELEMENTWISE_TILE_BYTES = 768 * 1024


def _row_tile(R, C):
    width = -(-C // LANE) * LANE * 4
    best = None
    for t in range(8, R + 1, 8):
        if R % t == 0 and t * width <= ELEMENTWISE_TILE_BYTES:
            best = t
    if best is None:
        best = R
    return best


def _adamw(w, m, v, parts, part_index, *, name):
    L, R, C = w.shape
    tr = _row_tile(R, C)
    bc1 = 1.0 - ADAM_B1 ** ADAM_STEP
    bc2 = 1.0 - ADAM_B2 ** ADAM_STEP
    n_parts = len(parts)

    def body(idx_ref, w_ref, m_ref, v_ref, *refs):
        g_refs = refs[:n_parts]
        g_out, d_out, m_out, v_out = refs[n_parts:]
        g = g_refs[0][...]
        for r in g_refs[1:]:
            g = g + r[...]
        wv = w_ref[...]
        mn = ADAM_B1 * m_ref[...] + (1.0 - ADAM_B1) * g
        vn = ADAM_B2 * v_ref[...] + (1.0 - ADAM_B2) * (g * g)
        m_hat = mn / bc1
        v_hat = vn / bc2
        g_out[...] = g
        d_out[...] = -ADAM_LR * (m_hat / (jnp.sqrt(v_hat) + ADAM_EPS) + ADAM_WD * wv)
        m_out[...] = mn
        v_out[...] = vn

    row = pl.BlockSpec((None, tr, C), lambda l, i, idx: (l, i, 0))
    in_specs = [row, row, row]
    args = [w, m, v]
    for arr, slot in parts:
        if slot is None:
            in_specs.append(pl.BlockSpec((None, None, tr, C), lambda l, i, idx: (idx[0], l, i, 0)))
        else:
            in_specs.append(pl.BlockSpec((None, None, tr, C), lambda l, i, idx, slot=slot: (slot, l, i, 0)))
        args.append(arr)
    grid_spec = pltpu.PrefetchScalarGridSpec(
        num_scalar_prefetch=1, grid=(L, R // tr), in_specs=in_specs, out_specs=[row] * 4)
    return pl.pallas_call(
        body, name=name, grid_spec=grid_spec,
        out_shape=[jax.ShapeDtypeStruct((L, R, C), F32)] * 4,
        compiler_params=_params(("parallel", "parallel")),
    )(part_index, *args)


def _pair_sum(g, recv, c_index, *, name):
    _, L, R, C = g.shape
    tr = _row_tile(R, C)

    def body(c_ref, g_ref, r_ref, o_ref):
        o_ref[...] = g_ref[...] + r_ref[...]

    def spec(index):
        return pl.BlockSpec((None, None, tr, C), index)

    grid_spec = pltpu.PrefetchScalarGridSpec(
        num_scalar_prefetch=1, grid=(4, L, R // tr),
        in_specs=[spec(lambda q, l, i, c: (2 * q + c[0], l, i, 0)), spec(lambda q, l, i, c: (q, l, i, 0))],
        out_specs=spec(lambda q, l, i, c: (q, l, i, 0)))
    return pl.pallas_call(
        body, name=name, grid_spec=grid_spec,
        out_shape=jax.ShapeDtypeStruct((4, L, R, C), F32),
        compiler_params=_params(("parallel", "parallel", "parallel")),
    )(c_index, g, recv)


def _sum8(parts, *, name):
    _, R, _ = parts.shape

    def body(p_ref, o_ref):
        acc = p_ref[0]
        for d in range(1, N_DEV):
            acc = acc + p_ref[d]
        o_ref[...] = acc

    return pl.pallas_call(
        body, name=name, grid=(1,),
        in_specs=[pl.BlockSpec((N_DEV, R, FLAT_W), lambda i: (0, 0, 0))],
        out_specs=pl.BlockSpec((R, FLAT_W), lambda i: (0, 0)),
        out_shape=jax.ShapeDtypeStruct((R, FLAT_W), F32),
        compiler_params=_params(("arbitrary",)),
    )(parts)


_MESH_ID = pl.DeviceIdType.MESH
_ANY = pl.BlockSpec(memory_space=pl.ANY)


def _all_gather(shards, *, name):
    n = len(shards)

    def body(*refs):
        x_refs, out_refs = refs[:n], refs[n:2 * n]
        send_sems, recv_sems, local_sems = refs[2 * n:]
        x, y, c = lax.axis_index("x"), lax.axis_index("y"), lax.axis_index("c")
        me, sibling = (x, y, c), (x, y, 1 - c)
        chips = [(1 - x, y), (x, 1 - y), (1 - x, 1 - y)]

        def blk(a, px, py, pc):
            return out_refs[a].at[4 * px + 2 * py + pc]

        def copy(a, k, block, to, src=None):
            return pltpu.make_async_remote_copy(
                src_ref=blk(a, *block) if src is None else src, dst_ref=blk(a, *block),
                send_sem=send_sems.at[7 * a + k], recv_sem=recv_sems.at[7 * a + k],
                device_id=to, device_id_type=_MESH_ID)

        mine = [pltpu.make_async_copy(x_refs[a], blk(a, *me), local_sems.at[a]) for a in range(n)]
        for cp in mine:
            cp.start()
        first = []
        for a in range(n):
            first.append(copy(a, 0, me, sibling, src=x_refs[a]))
            first += [copy(a, 1 + j, me, (*chip, c), src=x_refs[a]) for j, chip in enumerate(chips)]
        for cp in first:
            cp.start()
        passed = []
        for j, chip in enumerate(chips):
            for a in range(n):
                copy(a, 1 + j, (*chip, c), me).wait_recv()
                fwd = copy(a, 4 + j, (*chip, c), sibling)
                fwd.start()
                passed.append(fwd)
        for a in range(n):
            copy(a, 0, sibling, me).wait_recv()
        for j, chip in enumerate(chips):
            for a in range(n):
                copy(a, 4 + j, (*chip, 1 - c), me).wait_recv()
        for cp in first + passed:
            cp.wait_send()
        for cp in mine:
            cp.wait()

    return pl.pallas_call(
        body, name=name,
        out_shape=[jax.ShapeDtypeStruct((N_DEV,) + s.shape, s.dtype) for s in shards],
        in_specs=[_ANY] * n, out_specs=[_ANY] * n,
        scratch_shapes=[pltpu.SemaphoreType.DMA((7 * n,)), pltpu.SemaphoreType.DMA((7 * n,)),
                        pltpu.SemaphoreType.DMA((n,))],
    )(*shards)


def _exchange_sibling(gs, *, name):
    n = len(gs)

    def body(*refs):
        g_refs, out_refs = refs[:n], refs[n:2 * n]
        send_sems, recv_sems = refs[2 * n:]
        x, y, c = lax.axis_index("x"), lax.axis_index("y"), lax.axis_index("c")
        copies = []
        for a in range(n):
            for q in range(4):
                copies.append(pltpu.make_async_remote_copy(
                    src_ref=g_refs[a].at[2 * q + (1 - c)], dst_ref=out_refs[a].at[q],
                    send_sem=send_sems.at[4 * a + q], recv_sem=recv_sems.at[4 * a + q],
                    device_id=(x, y, 1 - c), device_id_type=_MESH_ID))
        for cp in copies:
            cp.start()
        for cp in copies:
            cp.wait_recv()
        for cp in copies:
            cp.wait_send()

    return pl.pallas_call(
        body, name=name,
        out_shape=[jax.ShapeDtypeStruct((4,) + g.shape[1:], g.dtype) for g in gs],
        in_specs=[_ANY] * n, out_specs=[_ANY] * n,
        scratch_shapes=[pltpu.SemaphoreType.DMA((4 * n,)), pltpu.SemaphoreType.DMA((4 * n,))],
    )(*gs)


def _exchange_chips(hs, *, name):
    n = len(hs)

    def body(*refs):
        h_refs, out_refs = refs[:n], refs[n:2 * n]
        send_sems, recv_sems = refs[2 * n:]
        x, y, c = lax.axis_index("x"), lax.axis_index("y"), lax.axis_index("c")
        chips = [(1 - x, y), (x, 1 - y), (1 - x, 1 - y)]
        copies = []
        for a in range(n):
            for k, (px, py) in enumerate(chips):
                copies.append(pltpu.make_async_remote_copy(
                    src_ref=h_refs[a].at[2 * px + py], dst_ref=out_refs[a].at[k],
                    send_sem=send_sems.at[3 * a + k], recv_sem=recv_sems.at[3 * a + k],
                    device_id=(px, py, c), device_id_type=_MESH_ID))
        for cp in copies:
            cp.start()
        for cp in copies:
            cp.wait_recv()
        for cp in copies:
            cp.wait_send()

    return pl.pallas_call(
        body, name=name,
        out_shape=[jax.ShapeDtypeStruct((3,) + h.shape[1:], h.dtype) for h in hs],
        in_specs=[_ANY] * n, out_specs=[_ANY] * n,
        scratch_shapes=[pltpu.SemaphoreType.DMA((3 * n,)), pltpu.SemaphoreType.DMA((3 * n,))],
    )(*hs)


_BIG = (("ffn1_w_up", 2), ("ffn1_w_down", 1), ("mix_w_in", 2), ("w_uq", 2), ("w_ukv", 2),
        ("w_br_conv", 2), ("w_br_mla", 2), ("w_o", 1), ("ffn2_w_up", 2), ("ffn2_w_down", 1))
_SMALL_SHARDED = (("meta_tokens", 1), ("mix_b_gate", 2), ("conv_w", 2), ("ln_g", 2), ("ln_b", 2))
_SMALL_REPL = ("q_norm_g", "kv_norm_g")


BIG_ROW_ALIGN = 64


def _pack(arrs, dtype, row_align=8):
    flat = jnp.concatenate([a.reshape(-1).astype(dtype) for a in arrs])
    n = flat.shape[0]
    rows = -(-n // (row_align * FLAT_W)) * row_align
    return jnp.pad(flat, (0, rows * FLAT_W - n)).reshape(rows, FLAT_W)


def _unpack(flat, shapes):
    flat = flat.reshape(-1)
    out, off = [], 0
    for s in shapes:
        n = int(np.prod(s))
        out.append(flat[off:off + n].reshape(s))
        off += n
    return out


def _unpack_gathered(gathered, shapes, axes):
    g2 = gathered.reshape(N_DEV, -1)
    out, off = [], 0
    for s, ax in zip(shapes, axes):
        n = int(np.prod(s))
        blocks = g2[:, off:off + n].reshape((N_DEV,) + tuple(s))
        full = jnp.moveaxis(blocks, 0, ax)
        out.append(full.reshape(tuple(s[:ax]) + (N_DEV * s[ax],) + tuple(s[ax + 1:])))
        off += n
    return out


def _to_dest(full, ax):
    s = full.shape
    split = full.reshape(s[:ax] + (N_DEV, s[ax] // N_DEV) + s[ax + 1:])
    return jnp.moveaxis(split, ax, 0)


def _from_blocks(g, ax):
    full = jnp.moveaxis(g, 0, ax)
    s = full.shape
    return full.reshape(s[:ax] + (s[ax] * s[ax + 1],) + s[ax + 2:])


def _rope_tables(T):
    inv_freq = 1.0 / (ROPE_BASE ** (jnp.arange(0, QK_ROPE, 2, dtype=F32) / QK_ROPE))
    ang = jnp.arange(T, dtype=F32)[:, None] * inv_freq[None, :]
    cos, sin = jnp.cos(ang), jnp.sin(ang)
    half = QK_ROPE // 2
    ones = jnp.ones((T, QK_NOPE), F32)
    zeros = lambda w: jnp.zeros((T, w), F32)
    c = jnp.concatenate([ones, cos, cos, zeros(HEAD_PAD - QK_NOPE - QK_ROPE)], axis=1)
    s1 = jnp.concatenate([zeros(QK_NOPE + half), sin, zeros(HEAD_PAD - QK_NOPE - QK_ROPE)], axis=1)
    s2 = jnp.concatenate([zeros(QK_NOPE), -sin, zeros(HEAD_PAD - QK_NOPE - half)], axis=1)
    return c, s1, s2


def _layer_weights(W, l):
    w_in = W["mix_w_in"][l]
    w_in_p = jnp.concatenate([w_in[:, :Z_KR_END], jnp.zeros((D_MODEL, D_IN_PAD - D_IN_REAL), BF16),
                              w_in[:, Z_KR_END:]], axis=1)
    w_uq = W["w_uq"][l].reshape(Q_LORA, MLA_HEADS, QK_NOPE + QK_ROPE)
    wq = jnp.pad(w_uq, ((0, 0), (0, 0), (0, HEAD_PAD - QK_NOPE - QK_ROPE))).reshape(Q_LORA, D_QK)
    w_ukv = W["w_ukv"][l].reshape(KV_LORA, MLA_HEADS, QK_NOPE + V_HEAD)
    wk_top = jnp.pad(w_ukv[:, :, :QK_NOPE], ((0, 0), (0, 0), (0, HEAD_PAD - QK_NOPE))).reshape(KV_LORA, D_QK)
    place = np.zeros((Q_LORA - KV_LORA, MLA_HEADS, HEAD_PAD), np.float32)
    for r in range(QK_ROPE):
        place[r, :, QK_NOPE + r] = 1.0
    wk = jnp.concatenate([wk_top, jnp.asarray(place.reshape(Q_LORA - KV_LORA, D_QK), BF16)], axis=0)
    wv = w_ukv[:, :, QK_NOPE:].reshape(KV_LORA, D_MLA)
    return dict(
        ffn1_down=W["ffn1_w_down"][l], ffn2_down=W["ffn2_w_down"][l],
        w_in=w_in_p, wq=wq, wk=wk, wv=wv,
        wbc=W["w_br_conv"][l], wbm=W["w_br_mla"][l], wo=W["w_o"][l])


def _row8(v):
    return jnp.pad(v, ((0, 8 - v.shape[0]), (0, 0)))


def _local_step(x, tgt, W, Wg, S):
    t_real = N_META + x.shape[0]
    T = -(-t_real // ROW_ALIGN) * ROW_ALIGN
    pad = T - t_real
    tm = _tile(T, (384, 256, 128))
    tms = _tile(T, (768, 256, 128))
    blk = _tile(T, (768, 256, 128))
    tabs = _rope_tables(T)

    h0 = jnp.concatenate([S["meta_tokens"], x, jnp.zeros((pad, D_MODEL), F32)], axis=0)
    tgt_p = jnp.concatenate([jnp.zeros((N_META, D_MODEL), F32), tgt, jnp.zeros((pad, D_MODEL), F32)], axis=0)

    def ffn_fwd(h, hb, up8, l, down, g, b, tag):
        gu, a = _ffn_up(hb, up8, l, name=f"ffn_up_{tag}", tm=tm)
        r, y, yb = _mm_res_ln(a, down, h, g, b, scale=0.5, name=f"ffn_down_ln_{tag}", tm=tm)
        return dict(gu=gu, a=a, r=r), y, yb

    saved = []
    h, hb = h0, h0.astype(BF16)
    for l in range(DEPTH):
        lw = _layer_weights(W, l)
        sv = dict(lw=lw, h_in=h, h_in_b=hb)
        sv["f1"], h1, h1b = ffn_fwd(h, hb, Wg["ffn1_w_up"], l, lw["ffn1_down"],
                                    S["ln_g"][l, 0:1], S["ln_b"][l, 0:1], f"{l}a")
        z = _mm(h1b, lw["w_in"], out_dtype=F32, name=f"mix_in_{l}", tm=tms, tn=1024, tk=D_MODEL)
        conv_w8 = _row8(S["conv_w"][l])
        bg8 = _row8(S["mix_b_gate"][l])
        yc = _conv_fwd(z, conv_w8, name=f"conv_fwd_{l}", tm=tms)
        gq, gkv = S["q_norm_g"][l:l + 1], S["kv_norm_g"][l:l + 1]
        q, k, v, qn, kin = _qkv_proj(z, gq, gkv, lw["wq"], lw["wk"], lw["wv"], tabs, name=f"qkv_proj_{l}", tm=tm)
        o, lse = _attn_fwd(q, k, v, name=f"attn_fwd_{l}", blk=blk)
        mg, pa, pb = _merge(yc, o, lw["wbc"], lw["wbm"], z, bg8, name=f"merge_{l}", tm=tm)
        r2, h2, h2b = _mm_res_ln(mg, lw["wo"], h1, S["ln_g"][l, 1:2], S["ln_b"][l, 1:2], scale=1.0,
                                 name=f"wo_ln_{l}", tm=tm)
        sv["f2"], h3, h3b = ffn_fwd(h2, h2b, Wg["ffn2_w_up"], l, lw["ffn2_down"],
                                    S["ln_g"][l, 2:3], S["ln_b"][l, 2:3], f"{l}b")
        sv.update(h1b=h1b, z=z, conv_w8=conv_w8, bg8=bg8, yc=yc, gq=gq, gkv=gkv, q=q, k=k, v=v, qn=qn, kin=kin,
                  o=o, lse=lse, mg=mg, pa=pa, pb=pb, r2=r2, h2b=h2b)
        saved.append(sv)
        h, hb = h3, h3b

    dh, loss8 = _loss_head(h, tgt_p, t_real=t_real, name="loss_head", tm=tm)

    tk = blk
    grads = {n: [None] * DEPTH for n, _ in _BIG}
    for n in ("mix_b_gate", "conv_w", "q_norm_g", "kv_norm_g"):
        grads[n] = [None] * DEPTH
    grads["ln_g"] = [[None] * 3 for _ in range(DEPTH)]
    grads["ln_b"] = [[None] * 3 for _ in range(DEPTH)]

    def ffn_bwd(dy, f, h_in_b, up8, l, down, g, tag):
        dr, dfb, dg, db = _ln_bwd(dy, f["r"], g, scale=0.5, name=f"ln_bwd_{tag}", tm=tm)
        dgu = _ffn_bwd_mid(dfb, down, f["gu"], name=f"ffn_bwd_mid_{tag}", tm=tm)
        d_down = _mm_tn_call(
            f["a"], dfb,
            pl.BlockSpec((None, tk, FF_BLK), lambda i, j, k: (i, k, 0)),
            pl.BlockSpec((tk, D_MODEL), lambda i, j, k: (k, 0)),
            out_shape=jax.ShapeDtypeStruct((D_FF, D_MODEL), F32),
            out_spec=pl.BlockSpec((FF_BLK, D_MODEL), lambda i, j, k: (i, 0)),
            grid=(FF_HALF_BLOCKS, 1, T // tk), name=f"dw_down_{tag}")
        d_up = _mm_tn_call(
            h_in_b, dgu,
            pl.BlockSpec((tk, D_MODEL), lambda i, j, k: (k, 0)),
            pl.BlockSpec((None, None, tk, FF_BLK),
                         lambda i, j, k: (j // FF_HALF_BLOCKS, j % FF_HALF_BLOCKS, k, 0)),
            out_shape=jax.ShapeDtypeStruct((N_DEV, D_MODEL, FF_BLK), F32),
            out_spec=pl.BlockSpec((None, D_MODEL, FF_BLK), lambda i, j, k: (j, 0, 0)),
            grid=(1, N_DEV, T // tk), name=f"dw_up_{tag}")
        row = pl.BlockSpec((tms, D_MODEL), lambda i, j, k: (i, 0))
        dh_in = _mm_call(
            dgu, up8,
            pl.BlockSpec((None, None, tms, FF_BLK),
                         lambda i, j, k: (k // FF_HALF_BLOCKS, k % FF_HALF_BLOCKS, i, 0)),
            pl.BlockSpec((None, None, D_MODEL, FF_BLK), lambda i, j, k: (k, l, 0, 0)),
            out_shape=jax.ShapeDtypeStruct((T, D_MODEL), F32), out_spec=row, acc_shape=(tms, D_MODEL),
            grid=(T // tms, 1, N_DEV), name=f"ffn_dx_{tag}", trans_b=True, res=dr, res_spec=row, res_scale=ALPHA)
        return dh_in, d_up, d_down, dg[0], db[0]

    for l in reversed(range(DEPTH)):
        sv = saved[l]
        lw = sv["lw"]
        dh, grads["ffn2_w_up"][l], grads["ffn2_w_down"][l], grads["ln_g"][l][2], grads["ln_b"][l][2] = ffn_bwd(
            dh, sv["f2"], sv["h2b"], Wg["ffn2_w_up"], l, lw["ffn2_down"], S["ln_g"][l, 2:3], f"{l}b")
        dr2, dmb, dg, db = _ln_bwd(dh, sv["r2"], S["ln_g"][l, 1:2], scale=1.0, name=f"ln_bwd_{l}m", tm=tm)
        grads["ln_g"][l][1], grads["ln_b"][l][1] = dg[0], db[0]
        grads["w_o"][l] = _mm_tn(sv["mg"], dmb, name=f"dw_o_{l}", tm=D_MODEL, tn=D_MODEL, tk=tk)
        dpa, dpb, dgate, dbg = _wo_bwd(dmb, lw["wo"], sv["z"], sv["bg8"], sv["pa"], sv["pb"], name=f"wo_bwd_{l}", tm=tm)
        grads["mix_b_gate"][l] = dbg[0:2]
        grads["w_br_conv"][l] = _mm_tn(sv["yc"], dpa, name=f"dw_br_conv_{l}", tm=D_CONV, tn=D_MODEL, tk=tk)
        grads["w_br_mla"][l] = _mm_tn(sv["o"], dpb, name=f"dw_br_mla_{l}", tm=D_MLA, tn=D_MODEL, tk=tk)
        dyc = _mm(dpa, lw["wbc"], trans_b=True, out_dtype=F32, name=f"d_yconv_{l}", tm=tms, tn=D_CONV, tk=D_MODEL)
        dym = _mm(dpb, lw["wbm"], trans_b=True, out_dtype=BF16, name=f"d_ymla_{l}", tm=tms, tn=D_MLA, tk=D_MODEL)
        dz_conv, dcw = _conv_bwd(dyc, sv["z"], sv["conv_w8"], name=f"conv_bwd_{l}", tm=tms)
        grads["conv_w"][l] = dcw[0:CONV_WIDTH]
        dq, dk, dv = _attn_bwd(sv["q"], sv["k"], sv["v"], sv["o"], dym, sv["lse"], name=f"attn_bwd_{l}", blk=blk)
        dz_mid, dqb, dkb, dvb, dgq, dgkv = _qkv_bwd(dq, dk, dv, sv["z"], sv["gq"], sv["gkv"], lw["wq"], lw["wk"],
                                                    lw["wv"], tabs, name=f"qkv_bwd_{l}", tm=tm)
        grads["q_norm_g"][l], grads["kv_norm_g"][l] = dgq[0], dgkv[0]
        d_wq = _mm_tn(sv["qn"], dqb, name=f"dw_uq_{l}", tm=Q_LORA, tn=D_QK, tk=tk)
        d_wk = _mm_tn(sv["kin"], dkb, name=f"dw_uk_{l}", tm=Q_LORA, tn=D_QK, tk=tk)
        d_wv = _mm_tn(sv["kin"], dvb, name=f"dw_uv_{l}", tm=Q_LORA, tn=D_MLA, tk=tk)
        grads["w_uq"][l] = d_wq.reshape(Q_LORA, MLA_HEADS, HEAD_PAD)[:, :, :QK_NOPE + QK_ROPE].reshape(Q_LORA, -1)
        d_kn = d_wk[:KV_LORA].reshape(KV_LORA, MLA_HEADS, HEAD_PAD)[:, :, :QK_NOPE]
        d_vv = d_wv[:KV_LORA].reshape(KV_LORA, MLA_HEADS, V_HEAD)
        grads["w_ukv"][l] = jnp.concatenate([d_kn, d_vv], axis=-1).reshape(KV_LORA, -1)
        dz = jnp.concatenate([dz_conv, dz_mid, dgate], axis=1)
        d_win = _mm_tn(sv["h1b"], dz, name=f"dw_in_{l}", tm=D_MODEL, tn=1024, tk=tk)
        grads["mix_w_in"][l] = jnp.concatenate([d_win[:, :Z_KR_END], d_win[:, Z_KR_END + D_IN_PAD - D_IN_REAL:]], axis=1)
        dh = _mm(dz, lw["w_in"], trans_b=True, out_dtype=F32, name=f"mix_dx_{l}", res=dr2, res_scale=ALPHA,
                 tm=tms, tn=D_MODEL, tk=1024)
        dh, grads["ffn1_w_up"][l], grads["ffn1_w_down"][l], grads["ln_g"][l][0], grads["ln_b"][l][0] = ffn_bwd(
            dh, sv["f1"], sv["h_in_b"], Wg["ffn1_w_up"], l, lw["ffn1_down"], S["ln_g"][l, 0:1], f"{l}a")

    big = {}
    for n, ax in _BIG:
        if n in Wg:
            big[n] = jnp.stack(grads[n], axis=1)
        else:
            big[n] = _to_dest(jnp.stack(grads[n]), ax)
    small = {n: jnp.stack(grads[n]) for n in ("mix_b_gate", "conv_w", "q_norm_g", "kv_norm_g")}
    small["ln_g"] = jnp.stack([jnp.stack(g) for g in grads["ln_g"]])
    small["ln_b"] = jnp.stack([jnp.stack(g) for g in grads["ln_b"]])
    small["meta_tokens"] = dh[:N_META]
    return loss8, dh[N_META:t_real], big, small


def kernel(x, meta_tokens, ffn1_w_up, ffn1_w_down, mix_w_in, mix_b_gate, conv_w, q_norm_g, w_uq, kv_norm_g, w_ukv, w_br_conv, w_br_mla, w_o, ffn2_w_up, ffn2_w_down, ln_g, ln_b, loss_target, m_meta_tokens, m_ffn1_w_up, m_ffn1_w_down, m_mix_w_in, m_mix_b_gate, m_conv_w, m_q_norm_g, m_w_uq, m_kv_norm_g, m_w_ukv, m_w_br_conv, m_w_br_mla, m_w_o, m_ffn2_w_up, m_ffn2_w_down, m_ln_g, m_ln_b, v_meta_tokens, v_ffn1_w_up, v_ffn1_w_down, v_mix_w_in, v_mix_b_gate, v_conv_w, v_q_norm_g, v_w_uq, v_kv_norm_g, v_w_ukv, v_w_br_conv, v_w_br_mla, v_w_o, v_ffn2_w_up, v_ffn2_w_down, v_ln_g, v_ln_b):
    names = ["meta_tokens", "ffn1_w_up", "ffn1_w_down", "mix_w_in", "mix_b_gate", "conv_w", "q_norm_g", "w_uq",
             "kv_norm_g", "w_ukv", "w_br_conv", "w_br_mla", "w_o", "ffn2_w_up", "ffn2_w_down", "ln_g", "ln_b"]
    w = dict(zip(names, (meta_tokens, ffn1_w_up, ffn1_w_down, mix_w_in, mix_b_gate, conv_w, q_norm_g, w_uq,
                         kv_norm_g, w_ukv, w_br_conv, w_br_mla, w_o, ffn2_w_up, ffn2_w_down, ln_g, ln_b)))
    m = dict(zip(names, (m_meta_tokens, m_ffn1_w_up, m_ffn1_w_down, m_mix_w_in, m_mix_b_gate, m_conv_w, m_q_norm_g,
                         m_w_uq, m_kv_norm_g, m_w_ukv, m_w_br_conv, m_w_br_mla, m_w_o, m_ffn2_w_up, m_ffn2_w_down,
                         m_ln_g, m_ln_b)))
    v = dict(zip(names, (v_meta_tokens, v_ffn1_w_up, v_ffn1_w_down, v_mix_w_in, v_mix_b_gate, v_conv_w, v_q_norm_g,
                         v_w_uq, v_kv_norm_g, v_w_ukv, v_w_br_conv, v_w_br_mla, v_w_o, v_ffn2_w_up, v_ffn2_w_down,
                         v_ln_g, v_ln_b)))
    ix, iy, ic = lax.axis_index("x"), lax.axis_index("y"), lax.axis_index("c")
    dev = 4 * ix + 2 * iy + ic

    big_names = [n for n, _ in _BIG]
    big_axes = [a for _, a in _BIG]
    big_shapes = [w[n].shape for n in big_names]
    small_names = [n for n, _ in _SMALL_SHARDED]
    small_axes = [a for _, a in _SMALL_SHARDED]
    small_shapes = [w[n].shape for n in small_names]
    gathered = _all_gather([w[n].astype(BF16) for n in big_names] + [_pack([w[n] for n in small_names], F32)],
                           name="all_gather_weights")
    Wg = dict(zip(big_names, gathered[:-1]))
    W = {n: _from_blocks(Wg[n], ax) for n, ax in _BIG if not n.endswith("w_up")}
    S = dict(zip(small_names, _unpack_gathered(gathered[-1], small_shapes, small_axes)))
    S["q_norm_g"], S["kv_norm_g"] = q_norm_g, kv_norm_g

    loss8, grad_x, g_dest, G = _local_step(x[0], loss_target[0], W, {n: Wg[n] for n in ("ffn1_w_up", "ffn2_w_up")}, S)

    from_sibling = _exchange_sibling([g_dest[n] for n in big_names], name="rs_sibling")
    c_index = ic.reshape(1).astype(jnp.int32)
    chip_sum = [_pair_sum(g_dest[n], r, c_index, name=f"rs_sibling_sum_{n}") for n, r in zip(big_names, from_sibling)]
    from_chips = _exchange_chips(chip_sum, name="rs_chips")
    my_chip = (2 * ix + iy).reshape(1).astype(jnp.int32)
    big_res = [{}, {}, {}, {}]
    for n, cs, fc in zip(big_names, chip_sum, from_chips):
        res = _adamw(w[n], m[n], v[n], [(cs, None), (fc, 0), (fc, 1), (fc, 2)], my_chip, name=f"adamw_{n}")
        for kind in range(4):
            big_res[kind][n] = res[kind]

    small_all = small_names + list(_SMALL_REPL)
    part = _pack([G[n] for n in small_all] + [loss8[0, 0:1]], F32)
    full_shapes = [G[n].shape for n in small_all] + [(1,)]
    summed = _sum8(_all_gather([part], name="all_gather_small_grads")[0], name="sum_small_grads")
    unpacked = _unpack(summed, full_shapes)
    loss = unpacked[-1][0]
    g_full = dict(zip(small_all, unpacked[:-1]))
    g_loc = []
    for n in small_all:
        if n in _SMALL_REPL:
            g_loc.append(g_full[n])
        else:
            ax = dict(_SMALL_SHARDED)[n]
            g_loc.append(lax.dynamic_slice_in_dim(g_full[n], dev * w[n].shape[ax], w[n].shape[ax], axis=ax))
    loc_shapes = [w[n].shape for n in small_all]
    g_pack = _pack(g_loc, F32)
    small_out = _adamw(_pack([w[n] for n in small_all], F32)[None], _pack([m[n] for n in small_all], F32)[None],
                       _pack([v[n] for n in small_all], F32)[None], [(g_pack[None, None], 0)],
                       jnp.zeros((1,), jnp.int32), name="adamw_small")
    small_res = [dict(zip(small_all, _unpack(o, loc_shapes))) for o in small_out]

    outs = [loss, grad_x[None]]
    for kind in range(4):
        for n in names:
            outs.append(big_res[kind][n] if n in big_res[kind] else small_res[kind][n])
    return tuple(outs)
```

```python
import functools

import numpy as np
import jax
import jax.numpy as jnp
from jax import lax
from jax.experimental import pallas as pl
from jax.experimental.pallas import tpu as pltpu

F32 = jnp.float32
BF16 = jnp.bfloat16

D_MODEL = 1024
DEPTH = 2
N_META = 16
D_CONV = 512
CONV_WIDTH = 3
MLA_HEADS = 8
QK_NOPE = 64
QK_ROPE = 32
V_HEAD = 64
Q_LORA = 256
KV_LORA = 128
D_MLA = MLA_HEADS * V_HEAD
ROPE_BASE = 10000.0
NEG_INF = -1e30
D_FF = 2816
ALPHA = (2 * DEPTH) ** 0.25
LN_EPS = 1e-5
RMS_EPS = 1e-6
ATTN_SCALE = (QK_NOPE + QK_ROPE) ** -0.5
ADAM_LR = 0.001
ADAM_B1 = 0.9
ADAM_B2 = 0.999
ADAM_EPS = 1e-08
ADAM_WD = 0.01
ADAM_STEP = 10

N_DEV = 8
HEAD_PAD = 128
HEADS_PER_STEP = 2
FF_BLK = 2 * D_FF // N_DEV
FF_HALF_BLOCKS = N_DEV // 2
D_QK = MLA_HEADS * HEAD_PAD
Z_CONV = 0
Z_MID = 1536
Z_GATE = 2048
D_IN_PAD = 4096
D_IN_REAL = 4000
Z_KR_END = Z_MID + Q_LORA + KV_LORA + QK_ROPE

V7X_VMEM_LIMIT = 56 * 1024 * 1024
LANE = 128
ROW_ALIGN = 256


def _tile(n, cands):
    for c in cands:
        if n % c == 0:
            return c
    raise ValueError(f"no tile for {n} in {cands}")


def _params(sem):
    return pltpu.CompilerParams(dimension_semantics=sem, vmem_limit_bytes=V7X_VMEM_LIMIT)


def _mm_call(a, b, a_spec, b_spec, *, out_shape, out_spec, acc_shape, grid, name, trans_b=False,
             res=None, res_spec=None, res_scale=1.0):
    nk = grid[2]
    has_res = res is not None
    out_dtype = out_shape.dtype
    dims = (((1,), (1,)), ((), ())) if trans_b else (((1,), (0,)), ((), ()))

    def body(*refs):
        if has_res:
            a_ref, b_ref, r_ref, o_ref, acc = refs
        else:
            a_ref, b_ref, o_ref, acc = refs
        k = pl.program_id(2)
        part = lax.dot_general(a_ref[...], b_ref[...], dims, preferred_element_type=F32)

        @pl.when(k == 0)
        def _():
            acc[...] = part

        @pl.when(k > 0)
        def _():
            acc[...] += part

        @pl.when(k == nk - 1)
        def _():
            out = acc[...]
            if has_res:
                out = out + res_scale * r_ref[...]
            o_ref[...] = out.astype(out_dtype)

    in_specs = [a_spec, b_spec]
    args = [a, b]
    if has_res:
        in_specs.append(res_spec)
        args.append(res)
    return pl.pallas_call(
        body, name=name, grid=grid, in_specs=in_specs, out_specs=out_spec, out_shape=out_shape,
        scratch_shapes=[pltpu.VMEM(acc_shape, F32)],
        compiler_params=_params(("parallel", "parallel", "arbitrary")),
    )(*args)


def _mm(a, b, *, out_dtype, name, trans_b=False, res=None, res_scale=1.0, tm, tn, tk):
    M, K = a.shape
    N = b.shape[0] if trans_b else b.shape[1]
    assert M % tm == 0 and N % tn == 0 and K % tk == 0
    b_spec = (pl.BlockSpec((tn, tk), lambda i, j, k: (j, k)) if trans_b
              else pl.BlockSpec((tk, tn), lambda i, j, k: (k, j)))
    tile = pl.BlockSpec((tm, tn), lambda i, j, k: (i, j))
    return _mm_call(a, b, pl.BlockSpec((tm, tk), lambda i, j, k: (i, k)), b_spec,
                    out_shape=jax.ShapeDtypeStruct((M, N), out_dtype), out_spec=tile, acc_shape=(tm, tn),
                    grid=(M // tm, N // tn, K // tk), name=name, trans_b=trans_b,
                    res=res, res_spec=tile, res_scale=res_scale)


def _mm_tn_call(a, b, a_spec, b_spec, *, out_shape, out_spec, grid, name):
    def body(a_ref, b_ref, o_ref):
        k = pl.program_id(2)
        part = lax.dot_general(a_ref[...], b_ref[...], (((0,), (0,)), ((), ())),
                               preferred_element_type=F32)

        @pl.when(k == 0)
        def _():
            o_ref[...] = part

        @pl.when(k > 0)
        def _():
            o_ref[...] += part

    return pl.pallas_call(
        body, name=name, grid=grid, in_specs=[a_spec, b_spec], out_specs=out_spec, out_shape=out_shape,
        compiler_params=_params(("parallel", "parallel", "arbitrary")),
    )(a, b)


def _mm_tn(a, b, *, name, tm, tn, tk):
    T, M = a.shape
    N = b.shape[1]
    assert M % tm == 0 and N % tn == 0 and T % tk == 0
    return _mm_tn_call(a, b, pl.BlockSpec((tk, tm), lambda i, j, k: (k, i)),
                       pl.BlockSpec((tk, tn), lambda i, j, k: (k, j)),
                       out_shape=jax.ShapeDtypeStruct((M, N), F32),
                       out_spec=pl.BlockSpec((tm, tn), lambda i, j, k: (i, j)),
                       grid=(M // tm, N // tn, T // tk), name=name)


def _ffn_up(hb, w_up8, l, *, name, tm):
    T = hb.shape[0]

    def body(h_ref, wg_ref, wu_ref, gu_ref, a_ref):
        h = h_ref[...]
        g = jnp.dot(h, wg_ref[...], preferred_element_type=F32)
        u = jnp.dot(h, wu_ref[...], preferred_element_type=F32)
        gu_ref[0] = g.astype(BF16)
        gu_ref[1] = u.astype(BF16)
        a_ref[...] = (g * jax.nn.sigmoid(g) * u).astype(BF16)

    return pl.pallas_call(
        body, name=name, grid=(FF_HALF_BLOCKS, T // tm),
        in_specs=[pl.BlockSpec((tm, D_MODEL), lambda j, i: (i, 0)),
                  pl.BlockSpec((None, None, D_MODEL, FF_BLK), lambda j, i: (j, l, 0, 0)),
                  pl.BlockSpec((None, None, D_MODEL, FF_BLK), lambda j, i: (j + FF_HALF_BLOCKS, l, 0, 0))],
        out_specs=[pl.BlockSpec((2, None, tm, FF_BLK), lambda j, i: (0, j, i, 0)),
                   pl.BlockSpec((None, tm, FF_BLK), lambda j, i: (j, i, 0))],
        out_shape=[jax.ShapeDtypeStruct((2, FF_HALF_BLOCKS, T, FF_BLK), BF16),
                   jax.ShapeDtypeStruct((FF_HALF_BLOCKS, T, FF_BLK), BF16)],
        compiler_params=_params(("parallel", "parallel")),
    )(hb, w_up8, w_up8)


def _mm_res_ln(a, w, res, g, b, *, scale, name, tm):
    split = a.ndim == 3
    if split:
        S, T, Ks = a.shape
        K = S * Ks
    else:
        T, K = a.shape

    def body(a_ref, w_ref, res_ref, g_ref, b_ref, r_ref, y_ref, yb_ref):
        if split:
            f = jnp.dot(a_ref[0], w_ref[0:Ks, :], preferred_element_type=F32)
            for s in range(1, S):
                f = f + jnp.dot(a_ref[s], w_ref[s * Ks:(s + 1) * Ks, :], preferred_element_type=F32)
        else:
            f = jnp.dot(a_ref[...], w_ref[...], preferred_element_type=F32)
        r = ALPHA * res_ref[...] + scale * f
        mu = jnp.mean(r, axis=-1, keepdims=True)
        xc = r - mu
        var = jnp.mean(xc * xc, axis=-1, keepdims=True)
        y = xc * lax.rsqrt(var + LN_EPS) * g_ref[...] + b_ref[...]
        r_ref[...] = r
        y_ref[...] = y
        yb_ref[...] = y.astype(BF16)

    row = pl.BlockSpec((tm, D_MODEL), lambda i: (i, 0))
    vec = pl.BlockSpec((1, D_MODEL), lambda i: (0, 0))
    return pl.pallas_call(
        body, name=name, grid=(T // tm,),
        in_specs=[pl.BlockSpec((S, tm, Ks), lambda i: (0, i, 0)) if split else pl.BlockSpec((tm, K), lambda i: (i, 0)),
                  pl.BlockSpec((K, D_MODEL), lambda i: (0, 0)), row, vec, vec],
        out_specs=[row, row, row],
        out_shape=[jax.ShapeDtypeStruct((T, D_MODEL), F32), jax.ShapeDtypeStruct((T, D_MODEL), F32),
                   jax.ShapeDtypeStruct((T, D_MODEL), BF16)],
        compiler_params=_params(("parallel",)),
    )(a, w, res, g, b)


def _conv_fwd(z, conv_w8, *, name, tm):
    T = z.shape[0]
    hb = tm // 8

    def body(b_ref, c_ref, h_ref, cp_ref, hp_ref, w_ref, y_ref):
        i = pl.program_id(0)
        u = c_ref[...] * h_ref[...]
        up = jnp.where(i > 0, cp_ref[...] * hp_ref[...], 0.0)
        ue = jnp.concatenate([up, u], axis=0)
        s1 = pltpu.roll(ue, 1, 0)[8:]
        s2 = pltpu.roll(ue, 2, 0)[8:]
        w = w_ref[...]
        conv = w[0:1] * s2 + w[1:2] * s1 + w[2:3] * u
        y_ref[...] = (b_ref[...] * conv).astype(BF16)

    def col(c):
        return pl.BlockSpec((tm, D_CONV), lambda i: (i, c))

    def prev(c):
        return pl.BlockSpec((8, D_CONV), lambda i: (jnp.maximum(i * hb - 1, 0), c))

    return pl.pallas_call(
        body, name=name, grid=(T // tm,),
        in_specs=[col(0), col(1), col(2), prev(1), prev(2), pl.BlockSpec((8, D_CONV), lambda i: (0, 0))],
        out_specs=pl.BlockSpec((tm, D_CONV), lambda i: (i, 0)),
        out_shape=jax.ShapeDtypeStruct((T, D_CONV), BF16),
        compiler_params=_params(("parallel",)),
    )(z, z, z, z, z, conv_w8)


def _rope(x, c, s1, s2):
    n = x.shape[-1]
    return x * c + pltpu.roll(x, 16, 1) * s1 + pltpu.roll(x, n - 16, 1) * s2


def _rope_t(d, c, s1, s2):
    n = d.shape[-1]
    return d * c + pltpu.roll(d * s1, n - 16, 1) + pltpu.roll(d * s2, 16, 1)


def _rms(x, g):
    rstd = lax.rsqrt(jnp.mean(x * x, axis=-1, keepdims=True) + RMS_EPS)
    return x * rstd * g


def _qkv_proj(z, gq, gkv, wq, wk, wv_ext, tabs, *, name, tm):
    T = z.shape[0]

    def body(z_ref, gq_ref, gkv_ref, wq_ref, wk_ref, wv_ref, c_ref, s1_ref, s2_ref,
             q_ref, k_ref, v_ref, qn_ref, kin_ref):
        zz = z_ref[...]
        qn = _rms(zz[:, :Q_LORA], gq_ref[...]).astype(BF16)
        kvn = _rms(zz[:, Q_LORA:Q_LORA + KV_LORA], gkv_ref[...]).astype(BF16)
        kin = jnp.concatenate([kvn, zz[:, Q_LORA + KV_LORA:].astype(BF16)], axis=-1)
        c = jnp.tile(c_ref[...], (1, MLA_HEADS))
        s1 = jnp.tile(s1_ref[...], (1, MLA_HEADS))
        s2 = jnp.tile(s2_ref[...], (1, MLA_HEADS))
        qpre = jnp.dot(qn, wq_ref[...], preferred_element_type=F32)
        kpre = jnp.dot(kin, wk_ref[...], preferred_element_type=F32)
        q_ref[...] = (_rope(qpre, c, s1, s2) * (ATTN_SCALE * LOG2_E)).astype(BF16)
        k_ref[...] = _rope(kpre, c, s1, s2).astype(BF16)
        vv = jnp.dot(kvn, wv_ref[...], preferred_element_type=F32)
        lane = lax.broadcasted_iota(jnp.int32, vv.shape, 1)
        v_ref[...] = jnp.where((lane & (HEAD_PAD - 1)) < V_HEAD, vv, 1.0).astype(BF16)
        qn_ref[...] = qn
        kin_ref[...] = kin

    def full(shape):
        return pl.BlockSpec(shape, lambda i: (0, 0))

    def rows(w, c=0):
        return pl.BlockSpec((tm, w), lambda i: (i, c))

    return pl.pallas_call(
        body, name=name, grid=(T // tm,),
        in_specs=[rows(512, Z_MID // 512), full((1, Q_LORA)), full((1, KV_LORA)),
                  full((Q_LORA, D_QK)), full((Q_LORA, D_QK)), full((KV_LORA, D_QK)),
                  rows(LANE), rows(LANE), rows(LANE)],
        out_specs=[rows(D_QK), rows(D_QK), rows(D_QK), rows(Q_LORA), rows(Q_LORA)],
        out_shape=[jax.ShapeDtypeStruct((T, D_QK), BF16), jax.ShapeDtypeStruct((T, D_QK), BF16),
                   jax.ShapeDtypeStruct((T, D_QK), BF16), jax.ShapeDtypeStruct((T, Q_LORA), BF16),
                   jax.ShapeDtypeStruct((T, Q_LORA), BF16)],
        compiler_params=_params(("parallel",)),
    )(z, gq, gkv, wq, wk, wv_ext, *tabs)


SOFTMAX_ROWS = 32
LOG2_E = 1.4426950408889634
_NT = (((1,), (1,)), ((), ()))
_TN = (((0,), (0,)), ((), ()))


def _diag_mask(s, row0, col0=0):
    row = lax.broadcasted_iota(jnp.int32, s.shape, 0) + row0
    col = lax.broadcasted_iota(jnp.int32, s.shape, 1) + col0
    return jnp.where(col <= row, s, NEG_INF)


def _attn_fwd(q, k, v, *, name, blk):
    T = q.shape[0]
    n = T // blk
    hp = HEADS_PER_STEP
    qi = np.array([i for i in range(n) for j in range(i + 1)], np.int32)
    kj = np.array([j for i in range(n) for j in range(i + 1)], np.int32)

    rc = _tile(blk, (SOFTMAX_ROWS,))

    def body(qi_ref, kj_ref, q_ref, k_ref, v_ref, o_ref, lse_ref, m_sc, acc_sc, s_sc, p_sc, red_sc):
        s_id = pl.program_id(1)
        i = qi_ref[s_id]
        j = kj_ref[s_id]

        @pl.when(j == 0)
        def _():
            m_sc[...] = jnp.full(m_sc.shape, NEG_INF, F32)
            acc_sc[...] = jnp.zeros(acc_sc.shape, F32)

        def head_step(hh, diagonal):
            hs = slice(hh * HEAD_PAD, (hh + 1) * HEAD_PAD)
            s_sc[hh] = lax.dot_general(q_ref[:, hs], k_ref[:, hs], _NT, preferred_element_type=F32)
            lanes = [slice(t * LANE, (t + 1) * LANE) for t in range(blk // LANE)]
            for r in range(blk // rc):
                rows = slice(r * rc, (r + 1) * rc)
                s = s_sc[hh, rows, :]
                if diagonal:
                    s = _diag_mask(s, r * rc)
                    s_sc[hh, rows, :] = s
                pm = s[:, lanes[0]]
                for t in lanes[1:]:
                    pm = jnp.maximum(pm, s[:, t])
                red_sc[hh, rows, :] = pm
            m_old = m_sc[hh]
            row_max = jnp.max(red_sc[hh], axis=-1, keepdims=True)
            m_new = jnp.maximum(m_old, jnp.broadcast_to(row_max, (blk, LANE)))
            a = jnp.exp2(m_old - m_new)
            m_sc[hh] = m_new
            for r in range(blk // rc):
                rows = slice(r * rc, (r + 1) * rc)
                mb = m_sc[hh, rows, :]
                for t in lanes:
                    p_sc[hh, rows, t] = jnp.exp2(s_sc[hh, rows, t] - mb).astype(BF16)
            acc_sc[hh] = a * acc_sc[hh] + jnp.dot(p_sc[hh], v_ref[:, hs], preferred_element_type=F32)

        @pl.when(j < i)
        def _():
            for hh in range(hp):
                head_step(hh, False)

        @pl.when(j == i)
        def _():
            for hh in range(hp):
                head_step(hh, True)
            for hh in range(hp):
                acc = acc_sc[hh]
                swapped = pltpu.roll(acc, V_HEAD, 1)
                o_ref[:, hh * V_HEAD:(hh + 1) * V_HEAD] = (acc / swapped)[:, :V_HEAD].astype(BF16)
                lane = lax.broadcasted_iota(jnp.int32, acc.shape, 1)
                denom = jnp.where(lane < V_HEAD, swapped, acc)
                lse_ref[hh] = m_sc[hh] + jnp.log(denom) * LOG2_E

    grid_spec = pltpu.PrefetchScalarGridSpec(
        num_scalar_prefetch=2, grid=(MLA_HEADS // hp, len(qi)),
        in_specs=[pl.BlockSpec((blk, hp * HEAD_PAD), lambda g, s, qi, kj: (qi[s], g)),
                  pl.BlockSpec((blk, hp * HEAD_PAD), lambda g, s, qi, kj: (kj[s], g)),
                  pl.BlockSpec((blk, hp * HEAD_PAD), lambda g, s, qi, kj: (kj[s], g))],
        out_specs=[pl.BlockSpec((blk, hp * V_HEAD), lambda g, s, qi, kj: (qi[s], g)),
                   pl.BlockSpec((hp, blk, LANE), lambda g, s, qi, kj: (g, qi[s], 0))],
        scratch_shapes=[pltpu.VMEM((hp, blk, LANE), F32), pltpu.VMEM((hp, blk, HEAD_PAD), F32),
                        pltpu.VMEM((hp, blk, blk), F32), pltpu.VMEM((hp, blk, blk), BF16),
                        pltpu.VMEM((hp, blk, LANE), F32)])
    return pl.pallas_call(
        body, name=name, grid_spec=grid_spec,
        out_shape=[jax.ShapeDtypeStruct((T, D_MLA), BF16), jax.ShapeDtypeStruct((MLA_HEADS, T, LANE), F32)],
        compiler_params=_params(("parallel", "arbitrary")),
    )(jnp.asarray(qi), jnp.asarray(kj), q, k, v)


def _merge(yc, ym, wbc, wbm, z, bg, *, name, tm):
    T = yc.shape[0]

    def body(yc_ref, ym_ref, wbc_ref, wbm_ref, gc_ref, gm_ref, bg_ref, mg_ref, pa_ref, pb_ref):
        pa = jnp.dot(yc_ref[...], wbc_ref[...], preferred_element_type=F32)
        pb = jnp.dot(ym_ref[...], wbm_ref[...], preferred_element_type=F32)
        bgv = bg_ref[...]
        sa = jax.nn.sigmoid(gc_ref[...] + bgv[0:1])
        sb = jax.nn.sigmoid(gm_ref[...] + bgv[1:2])
        mg_ref[...] = (sa * pa + sb * pb).astype(BF16)
        pa_ref[...] = pa.astype(BF16)
        pb_ref[...] = pb.astype(BF16)

    row = pl.BlockSpec((tm, D_MODEL), lambda i: (i, 0))
    return pl.pallas_call(
        body, name=name, grid=(T // tm,),
        in_specs=[pl.BlockSpec((tm, D_CONV), lambda i: (i, 0)), pl.BlockSpec((tm, D_MLA), lambda i: (i, 0)),
                  pl.BlockSpec((D_CONV, D_MODEL), lambda i: (0, 0)), pl.BlockSpec((D_MLA, D_MODEL), lambda i: (0, 0)),
                  pl.BlockSpec((tm, D_MODEL), lambda i: (i, Z_GATE // D_MODEL)),
                  pl.BlockSpec((tm, D_MODEL), lambda i: (i, Z_GATE // D_MODEL + 1)),
                  pl.BlockSpec((8, D_MODEL), lambda i: (0, 0))],
        out_specs=[row, row, row],
        out_shape=[jax.ShapeDtypeStruct((T, D_MODEL), BF16)] * 3,
        compiler_params=_params(("parallel",)),
    )(yc, ym, wbc, wbm, z, z, bg)


def _loss_head(h, tgt, *, t_real, name, tm):
    T = h.shape[0]

    def body(h_ref, t_ref, dy_ref, loss_ref):
        i = pl.program_id(0)
        row = lax.broadcasted_iota(jnp.int32, (tm, 1), 0) + i * tm
        valid = (row >= N_META) & (row < t_real)
        err = jnp.where(valid, h_ref[...] - t_ref[...], 0.0)
        dy_ref[...] = err * (1.0 / D_MODEL)
        part = 0.5 * jnp.sum(jnp.sum(err * err, axis=-1, keepdims=True) * (1.0 / D_MODEL), axis=0, keepdims=True)

        @pl.when(i == 0)
        def _():
            loss_ref[...] = jnp.zeros(loss_ref.shape, F32)

        loss_ref[...] += jnp.broadcast_to(part, loss_ref.shape)

    row_spec = pl.BlockSpec((tm, D_MODEL), lambda i: (i, 0))
    return pl.pallas_call(
        body, name=name, grid=(T // tm,),
        in_specs=[row_spec, row_spec],
        out_specs=[row_spec, pl.BlockSpec((8, LANE), lambda i: (0, 0))],
        out_shape=[jax.ShapeDtypeStruct((T, D_MODEL), F32), jax.ShapeDtypeStruct((8, LANE), F32)],
        compiler_params=_params(("arbitrary",)),
    )(h, tgt)


def _ln_bwd(dy, r, g, *, scale, name, tm):
    T = dy.shape[0]

    def body(dy_ref, r_ref, g_ref, dr_ref, drb_ref, dg_ref, db_ref):
        i = pl.program_id(0)
        rr = r_ref[...]
        dyv = dy_ref[...]
        mu = jnp.mean(rr, axis=-1, keepdims=True)
        xc = rr - mu
        rstd = lax.rsqrt(jnp.mean(xc * xc, axis=-1, keepdims=True) + LN_EPS)
        xh = xc * rstd
        dxh = dyv * g_ref[...]
        m1 = jnp.mean(dxh, axis=-1, keepdims=True)
        m2 = jnp.mean(dxh * xh, axis=-1, keepdims=True)
        dr = rstd * (dxh - m1 - xh * m2)
        dr_ref[...] = dr
        drb_ref[...] = (scale * dr).astype(BF16)

        @pl.when(i == 0)
        def _():
            dg_ref[...] = jnp.zeros(dg_ref.shape, F32)
            db_ref[...] = jnp.zeros(db_ref.shape, F32)

        dg_ref[0:1, :] += jnp.sum(dyv * xh, axis=0, keepdims=True)
        db_ref[0:1, :] += jnp.sum(dyv, axis=0, keepdims=True)

    row = pl.BlockSpec((tm, D_MODEL), lambda i: (i, 0))
    acc = pl.BlockSpec((8, D_MODEL), lambda i: (0, 0))
    return pl.pallas_call(
        body, name=name, grid=(T // tm,),
        in_specs=[row, row, pl.BlockSpec((1, D_MODEL), lambda i: (0, 0))],
        out_specs=[row, row, acc, acc],
        out_shape=[jax.ShapeDtypeStruct((T, D_MODEL), F32), jax.ShapeDtypeStruct((T, D_MODEL), BF16),
                   jax.ShapeDtypeStruct((8, D_MODEL), F32), jax.ShapeDtypeStruct((8, D_MODEL), F32)],
        compiler_params=_params(("arbitrary",)),
    )(dy, r, g)


def _ffn_bwd_mid(dfb, w_down, gu, *, name, tm):
    T = dfb.shape[0]

    def body(df_ref, w_ref, gu_ref, o_ref):
        da = lax.dot_general(df_ref[...], w_ref[...], (((1,), (1,)), ((), ())), preferred_element_type=F32)
        g = gu_ref[0].astype(F32)
        u = gu_ref[1].astype(F32)
        sg = jax.nn.sigmoid(g)
        o_ref[0] = (da * u * (sg * (1.0 + g * (1.0 - sg)))).astype(BF16)
        o_ref[1] = (da * (g * sg)).astype(BF16)

    return pl.pallas_call(
        body, name=name, grid=(FF_HALF_BLOCKS, T // tm),
        in_specs=[pl.BlockSpec((tm, D_MODEL), lambda j, i: (i, 0)),
                  pl.BlockSpec((FF_BLK, D_MODEL), lambda j, i: (j, 0)),
                  pl.BlockSpec((2, None, tm, FF_BLK), lambda j, i: (0, j, i, 0))],
        out_specs=pl.BlockSpec((2, None, tm, FF_BLK), lambda j, i: (0, j, i, 0)),
        out_shape=jax.ShapeDtypeStruct((2, FF_HALF_BLOCKS, T, FF_BLK), BF16),
        compiler_params=_params(("parallel", "parallel")),
    )(dfb, w_down, gu)


def _wo_bwd(dmb, wo, z, bg, pa, pb, *, name, tm):
    T = dmb.shape[0]

    def body(dm_ref, w_ref, gc_ref, gm_ref, bg_ref, pa_ref, pb_ref, dpa_ref, dpb_ref, dg_ref, dbg_ref):
        i = pl.program_id(0)
        dm = lax.dot_general(dm_ref[...], w_ref[...], _NT, preferred_element_type=F32)
        bgv = bg_ref[...]
        sa = jax.nn.sigmoid(gc_ref[...] + bgv[0:1])
        sb = jax.nn.sigmoid(gm_ref[...] + bgv[1:2])
        dpa_ref[...] = (dm * sa).astype(BF16)
        dpb_ref[...] = (dm * sb).astype(BF16)
        dga = dm * pa_ref[...].astype(F32) * (sa * (1.0 - sa))
        dgb = dm * pb_ref[...].astype(F32) * (sb * (1.0 - sb))
        dg_ref[:, :D_MODEL] = dga.astype(BF16)
        dg_ref[:, D_MODEL:] = dgb.astype(BF16)

        @pl.when(i == 0)
        def _():
            dbg_ref[...] = jnp.zeros(dbg_ref.shape, F32)

        dbg_ref[0:1, :] += jnp.sum(dga, axis=0, keepdims=True)
        dbg_ref[1:2, :] += jnp.sum(dgb, axis=0, keepdims=True)

    row = pl.BlockSpec((tm, D_MODEL), lambda i: (i, 0))
    return pl.pallas_call(
        body, name=name, grid=(T // tm,),
        in_specs=[row, pl.BlockSpec((D_MODEL, D_MODEL), lambda i: (0, 0)),
                  pl.BlockSpec((tm, D_MODEL), lambda i: (i, Z_GATE // D_MODEL)),
                  pl.BlockSpec((tm, D_MODEL), lambda i: (i, Z_GATE // D_MODEL + 1)),
                  pl.BlockSpec((8, D_MODEL), lambda i: (0, 0)), row, row],
        out_specs=[row, row, pl.BlockSpec((tm, 2 * D_MODEL), lambda i: (i, 0)),
                   pl.BlockSpec((8, D_MODEL), lambda i: (0, 0))],
        out_shape=[jax.ShapeDtypeStruct((T, D_MODEL), BF16), jax.ShapeDtypeStruct((T, D_MODEL), BF16),
                   jax.ShapeDtypeStruct((T, 2 * D_MODEL), BF16), jax.ShapeDtypeStruct((8, D_MODEL), F32)],
        compiler_params=_params(("arbitrary",)),
    )(dmb, wo, z, z, bg, pa, pb)


def _conv_bwd(dy, z, conv_w8, *, name, tm):
    T = dy.shape[0]
    n = T // tm
    hb = tm // 8

    def body(dy_ref, b_ref, c_ref, h_ref, cp_ref, hp_ref, dyn_ref, bn_ref, w_ref, dz_ref, dw_ref):
        i = pl.program_id(0)
        u = c_ref[...] * h_ref[...]
        up = jnp.where(i > 0, cp_ref[...] * hp_ref[...], 0.0)
        ue = jnp.concatenate([up, u], axis=0)
        s1 = pltpu.roll(ue, 1, 0)[8:]
        s2 = pltpu.roll(ue, 2, 0)[8:]
        w = w_ref[...]
        conv = w[0:1] * s2 + w[1:2] * s1 + w[2:3] * u
        dyv = dy_ref[...]
        e = dyv * b_ref[...]
        en = jnp.where(i < n - 1, dyn_ref[...] * bn_ref[...], 0.0)
        ee = jnp.concatenate([e, en], axis=0)
        e1 = pltpu.roll(ee, tm + 8 - 1, 0)[:tm]
        e2 = pltpu.roll(ee, tm + 8 - 2, 0)[:tm]
        du = w[2:3] * e + w[1:2] * e1 + w[0:1] * e2
        dz_ref[:, 0:D_CONV] = (dyv * conv).astype(BF16)
        dz_ref[:, D_CONV:2 * D_CONV] = (du * h_ref[...]).astype(BF16)
        dz_ref[:, 2 * D_CONV:] = (du * c_ref[...]).astype(BF16)

        @pl.when(i == 0)
        def _():
            dw_ref[...] = jnp.zeros(dw_ref.shape, F32)

        dw_ref[0:1, :] += jnp.sum(e * s2, axis=0, keepdims=True)
        dw_ref[1:2, :] += jnp.sum(e * s1, axis=0, keepdims=True)
        dw_ref[2:3, :] += jnp.sum(e * u, axis=0, keepdims=True)

    def col(c):
        return pl.BlockSpec((tm, D_CONV), lambda i: (i, c))

    def prev(c):
        return pl.BlockSpec((8, D_CONV), lambda i: (jnp.maximum(i * hb - 1, 0), c))

    def nxt(c):
        return pl.BlockSpec((8, D_CONV), lambda i: (jnp.minimum((i + 1) * hb, T // 8 - 1), c))

    return pl.pallas_call(
        body, name=name, grid=(n,),
        in_specs=[col(0), col(0), col(1), col(2), prev(1), prev(2), nxt(0), nxt(0),
                  pl.BlockSpec((8, D_CONV), lambda i: (0, 0))],
        out_specs=[pl.BlockSpec((tm, 3 * D_CONV), lambda i: (i, 0)), pl.BlockSpec((8, D_CONV), lambda i: (0, 0))],
        out_shape=[jax.ShapeDtypeStruct((T, 3 * D_CONV), BF16), jax.ShapeDtypeStruct((8, D_CONV), F32)],
        compiler_params=_params(("arbitrary",)),
    )(dy, z, z, z, z, z, dy, z, conv_w8)


def _attn_bwd(q, k, v, o, do, lse, *, name, blk):
    T = q.shape[0]
    n = T // blk
    hp = HEADS_PER_STEP
    qi = np.array([i for j in range(n) for i in range(j, n)], np.int32)
    kj = np.array([j for j in range(n) for i in range(j, n)], np.int32)

    rc = _tile(blk, (SOFTMAX_ROWS,))

    def body(qi_ref, kj_ref, q_ref, k_ref, v_ref, o_ref, do_ref, lse_ref, dq_ref, dk_ref, dv_ref,
             dk_sc, dv_sc, s_sc, dp_sc, p_sc, ds_sc, delta_sc):
        s_id = pl.program_id(1)
        i = qi_ref[s_id]
        j = kj_ref[s_id]

        @pl.when(s_id == 0)
        def _():
            dq_ref[...] = jnp.zeros(dq_ref.shape, F32)

        @pl.when(i == j)
        def _():
            dk_sc[...] = jnp.zeros(dk_sc.shape, F32)
            dv_sc[...] = jnp.zeros(dv_sc.shape, F32)

        q_rows = pl.ds(pl.multiple_of(i * blk, blk), blk)

        def head_step(hh, diagonal):
            hs = slice(hh * HEAD_PAD, (hh + 1) * HEAD_PAD)
            vs = slice(hh * V_HEAD, (hh + 1) * V_HEAD)
            qh = q_ref[:, hs]
            kh = k_ref[:, hs]
            doh = do_ref[:, vs]
            s_sc[hh] = lax.dot_general(qh, kh, _NT, preferred_element_type=F32)
            vh = v_ref[:, hh * HEAD_PAD:hh * HEAD_PAD + V_HEAD]
            dp_sc[hh] = lax.dot_general(doh, vh, _NT, preferred_element_type=F32)
            delta = jnp.sum(doh.astype(F32) * o_ref[:, vs].astype(F32), axis=-1, keepdims=True)
            delta_sc[hh] = jnp.broadcast_to(delta, (blk, LANE))
            for r in range(blk // rc):
                rows = slice(r * rc, (r + 1) * rc)
                lse = lse_ref[hh, rows, :]
                dl = delta_sc[hh, rows, :]
                for t in range(blk // LANE):
                    cols = slice(t * LANE, (t + 1) * LANE)
                    s = s_sc[hh, rows, cols]
                    if diagonal:
                        s = _diag_mask(s, r * rc, t * LANE)
                    p = jnp.exp2(s - lse)
                    p_sc[hh, rows, cols] = p.astype(BF16)
                    ds_sc[hh, rows, cols] = (p * (dp_sc[hh, rows, cols] - dl)).astype(BF16)
            dv_sc[hh] += lax.dot_general(p_sc[hh], doh, _TN, preferred_element_type=F32)
            dk_sc[hh] += lax.dot_general(ds_sc[hh], qh, _TN, preferred_element_type=F32)
            dq_ref[q_rows, hs] += jnp.dot(ds_sc[hh], kh, preferred_element_type=F32)

        @pl.when(j < i)
        def _():
            for hh in range(hp):
                head_step(hh, False)

        @pl.when(j == i)
        def _():
            for hh in range(hp):
                head_step(hh, True)

        @pl.when(i == n - 1)
        def _():
            for hh in range(hp):
                dk_ref[:, hh * HEAD_PAD:(hh + 1) * HEAD_PAD] = dk_sc[hh] * (1.0 / LOG2_E)
                dv_ref[:, hh * V_HEAD:(hh + 1) * V_HEAD] = dv_sc[hh]

    wq = hp * HEAD_PAD
    wv = hp * V_HEAD
    grid_spec = pltpu.PrefetchScalarGridSpec(
        num_scalar_prefetch=2, grid=(MLA_HEADS // hp, len(qi)),
        in_specs=[pl.BlockSpec((blk, wq), lambda g, s, qi, kj: (qi[s], g)),
                  pl.BlockSpec((blk, wq), lambda g, s, qi, kj: (kj[s], g)),
                  pl.BlockSpec((blk, wq), lambda g, s, qi, kj: (kj[s], g)),
                  pl.BlockSpec((blk, wv), lambda g, s, qi, kj: (qi[s], g)),
                  pl.BlockSpec((blk, wv), lambda g, s, qi, kj: (qi[s], g)),
                  pl.BlockSpec((hp, blk, LANE), lambda g, s, qi, kj: (g, qi[s], 0))],
        out_specs=[pl.BlockSpec((T, wq), lambda g, s, qi, kj: (0, g)),
                   pl.BlockSpec((blk, wq), lambda g, s, qi, kj: (kj[s], g)),
                   pl.BlockSpec((blk, wv), lambda g, s, qi, kj: (kj[s], g))],
        scratch_shapes=[pltpu.VMEM((hp, blk, HEAD_PAD), F32), pltpu.VMEM((hp, blk, V_HEAD), F32),
                        pltpu.VMEM((hp, blk, blk), F32), pltpu.VMEM((hp, blk, blk), F32),
                        pltpu.VMEM((hp, blk, blk), BF16), pltpu.VMEM((hp, blk, blk), BF16),
                        pltpu.VMEM((hp, blk, LANE), F32)])
    return pl.pallas_call(
        body, name=name, grid_spec=grid_spec,
        out_shape=[jax.ShapeDtypeStruct((T, D_QK), F32), jax.ShapeDtypeStruct((T, D_QK), F32),
                   jax.ShapeDtypeStruct((T, D_MLA), F32)],
        compiler_params=_params(("parallel", "arbitrary")),
    )(jnp.asarray(qi), jnp.asarray(kj), q, k, v, o, do, lse)


def _qkv_bwd(dq, dk, dv, z, gq, gkv, wq, wk, wv, tabs, *, name, tm):
    T = dq.shape[0]

    def body(dq_ref, dk_ref, dv_ref, z_ref, gq_ref, gkv_ref, wq_ref, wk_ref, wv_ref, c_ref, s1_ref, s2_ref,
             dz_ref, dqb_ref, dkb_ref, dvb_ref, dgq_ref, dgkv_ref):
        i = pl.program_id(0)
        c = jnp.tile(c_ref[...], (1, MLA_HEADS))
        s1 = jnp.tile(s1_ref[...], (1, MLA_HEADS))
        s2 = jnp.tile(s2_ref[...], (1, MLA_HEADS))
        dqp = _rope_t(dq_ref[...] * ATTN_SCALE, c, s1, s2).astype(BF16)
        dkp = _rope_t(dk_ref[...], c, s1, s2).astype(BF16)
        dvb = dv_ref[...].astype(BF16)
        dqb_ref[...] = dqp
        dkb_ref[...] = dkp
        dvb_ref[...] = dvb
        dqn = lax.dot_general(dqp, wq_ref[...], _NT, preferred_element_type=F32)
        dkin = lax.dot_general(dkp, wk_ref[...], _NT, preferred_element_type=F32)
        dkvn = dkin[:, :KV_LORA] + lax.dot_general(dvb, wv_ref[...], _NT, preferred_element_type=F32)
        zz = z_ref[...]

        def rms_bwd(x, g, dy):
            rstd = lax.rsqrt(jnp.mean(x * x, axis=-1, keepdims=True) + RMS_EPS)
            xh = x * rstd
            dxh = dy * g
            dx = rstd * (dxh - xh * jnp.mean(dxh * xh, axis=-1, keepdims=True))
            return dx, jnp.sum(dy * xh, axis=0, keepdims=True)

        dcq, dgq = rms_bwd(zz[:, :Q_LORA], gq_ref[...], dqn)
        dckv, dgkv = rms_bwd(zz[:, Q_LORA:Q_LORA + KV_LORA], gkv_ref[...], dkvn)
        dz_ref[:, :Q_LORA] = dcq.astype(BF16)
        dz_ref[:, Q_LORA:Q_LORA + KV_LORA] = dckv.astype(BF16)
        dz_ref[:, Q_LORA + KV_LORA:] = dkin[:, KV_LORA:].astype(BF16)

        @pl.when(i == 0)
        def _():
            dgq_ref[...] = jnp.zeros(dgq_ref.shape, F32)
            dgkv_ref[...] = jnp.zeros(dgkv_ref.shape, F32)

        dgq_ref[0:1, :] += dgq
        dgkv_ref[0:1, :] += dgkv

    def full(shape):
        return pl.BlockSpec(shape, lambda i: (0, 0))

    def rows(w, c=0):
        return pl.BlockSpec((tm, w), lambda i: (i, c))

    return pl.pallas_call(
        body, name=name, grid=(T // tm,),
        in_specs=[rows(D_QK), rows(D_QK), rows(D_MLA), rows(512, Z_MID // 512),
                  full((1, Q_LORA)), full((1, KV_LORA)),
                  full((Q_LORA, D_QK)), full((Q_LORA, D_QK)), full((KV_LORA, D_MLA)),
                  rows(LANE), rows(LANE), rows(LANE)],
        out_specs=[rows(512), rows(D_QK), rows(D_QK), rows(D_MLA), full((8, Q_LORA)), full((8, KV_LORA))],
        out_shape=[jax.ShapeDtypeStruct((T, 512), BF16), jax.ShapeDtypeStruct((T, D_QK), BF16),
                   jax.ShapeDtypeStruct((T, D_QK), BF16), jax.ShapeDtypeStruct((T, D_MLA), BF16),
                   jax.ShapeDtypeStruct((8, Q_LORA), F32), jax.ShapeDtypeStruct((8, KV_LORA), F32)],
        compiler_params=_params(("arbitrary",)),
    )(dq, dk, dv, z, gq, gkv, wq, wk, wv, *tabs)


FLAT_W = 1024


ELEMENTWISE_TILE_BYTES = 768 * 1024


def _row_tile(R, C):
    width = -(-C // LANE) * LANE * 4
    best = None
    for t in range(16, R + 1, 16):
        if R % t == 0 and t * width <= ELEMENTWISE_TILE_BYTES:
            best = t
    if best is None:
        best = R
    return best


def _adamw(w, m, v, parts, part_index, *, name):
    L, R, C = w.shape
    tr = _row_tile(R, C)
    bc1 = 1.0 - ADAM_B1 ** ADAM_STEP
    bc2 = 1.0 - ADAM_B2 ** ADAM_STEP
    n_parts = len(parts)

    def body(idx_ref, w_ref, m_ref, v_ref, *refs):
        g_refs = refs[:n_parts]
        g_out, d_out, m_out, v_out = refs[n_parts:]
        g = g_refs[0][...].astype(F32)
        for r in g_refs[1:]:
            g = g + r[...].astype(F32)
        wv = w_ref[...]
        mn = ADAM_B1 * m_ref[...] + (1.0 - ADAM_B1) * g
        vn = ADAM_B2 * v_ref[...] + (1.0 - ADAM_B2) * (g * g)
        m_hat = mn / bc1
        v_hat = vn / bc2
        g_out[...] = g
        d_out[...] = -ADAM_LR * (m_hat / (jnp.sqrt(v_hat) + ADAM_EPS) + ADAM_WD * wv)
        m_out[...] = mn
        v_out[...] = vn

    row = pl.BlockSpec((None, tr, C), lambda l, i, idx: (l, i, 0))
    in_specs = [row, row, row]
    args = [w, m, v]
    for arr, slot in parts:
        if slot is None:
            in_specs.append(pl.BlockSpec((None, None, tr, C), lambda l, i, idx: (idx[0], l, i, 0)))
        else:
            in_specs.append(pl.BlockSpec((None, None, tr, C), lambda l, i, idx, slot=slot: (slot, l, i, 0)))
        args.append(arr)
    grid_spec = pltpu.PrefetchScalarGridSpec(
        num_scalar_prefetch=1, grid=(L, R // tr), in_specs=in_specs, out_specs=[row] * 4)
    return pl.pallas_call(
        body, name=name, grid_spec=grid_spec,
        out_shape=[jax.ShapeDtypeStruct((L, R, C), F32)] * 4,
        compiler_params=_params(("parallel", "parallel")),
    )(part_index, *args)


def _pair_sum(g, recv, c_index, *, name):
    _, L, R, C = g.shape
    tr = _row_tile(R, C)

    def body(c_ref, g_ref, r_ref, o_ref, ob_ref):
        total = g_ref[...] + r_ref[...]
        o_ref[...] = total
        ob_ref[...] = total.astype(BF16)

    def spec(index):
        return pl.BlockSpec((None, None, tr, C), index)

    grid_spec = pltpu.PrefetchScalarGridSpec(
        num_scalar_prefetch=1, grid=(4, L, R // tr),
        in_specs=[spec(lambda q, l, i, c: (2 * q + c[0], l, i, 0)), spec(lambda q, l, i, c: (q, l, i, 0))],
        out_specs=[spec(lambda q, l, i, c: (q, l, i, 0)), spec(lambda q, l, i, c: (q, l, i, 0))])
    return pl.pallas_call(
        body, name=name, grid_spec=grid_spec,
        out_shape=[jax.ShapeDtypeStruct((4, L, R, C), F32), jax.ShapeDtypeStruct((4, L, R, C), BF16)],
        compiler_params=_params(("parallel", "parallel", "parallel")),
    )(c_index, g, recv)


def _sum8(parts, *, name):
    _, R, _ = parts.shape

    def body(p_ref, o_ref):
        acc = p_ref[0]
        for d in range(1, N_DEV):
            acc = acc + p_ref[d]
        o_ref[...] = acc

    return pl.pallas_call(
        body, name=name, grid=(1,),
        in_specs=[pl.BlockSpec((N_DEV, R, FLAT_W), lambda i: (0, 0, 0))],
        out_specs=pl.BlockSpec((R, FLAT_W), lambda i: (0, 0)),
        out_shape=jax.ShapeDtypeStruct((R, FLAT_W), F32),
        compiler_params=_params(("arbitrary",)),
    )(parts)


_MESH_ID = pl.DeviceIdType.MESH
_ANY = pl.BlockSpec(memory_space=pl.ANY)


def _all_gather(shards, *, name):
    n = len(shards)

    def body(*refs):
        x_refs, out_refs = refs[:n], refs[n:2 * n]
        send_sems, recv_sems, local_sems = refs[2 * n:]
        x, y, c = lax.axis_index("x"), lax.axis_index("y"), lax.axis_index("c")
        me, sibling = (x, y, c), (x, y, 1 - c)
        chips = [(1 - x, y), (x, 1 - y), (1 - x, 1 - y)]

        def blk(a, px, py, pc):
            return out_refs[a].at[4 * px + 2 * py + pc]

        def copy(a, k, block, to, src=None):
            return pltpu.make_async_remote_copy(
                src_ref=blk(a, *block) if src is None else src, dst_ref=blk(a, *block),
                send_sem=send_sems.at[7 * a + k], recv_sem=recv_sems.at[7 * a + k],
                device_id=to, device_id_type=_MESH_ID)

        mine = [pltpu.make_async_copy(x_refs[a], blk(a, *me), local_sems.at[a]) for a in range(n)]
        for cp in mine:
            cp.start()
        first = []
        for a in range(n):
            first.append(copy(a, 0, me, sibling, src=x_refs[a]))
            first += [copy(a, 1 + j, me, (*chip, c), src=x_refs[a]) for j, chip in enumerate(chips)]
        for cp in first:
            cp.start()
        passed = []
        for j, chip in enumerate(chips):
            for a in range(n):
                copy(a, 1 + j, (*chip, c), me).wait_recv()
                fwd = copy(a, 4 + j, (*chip, c), sibling)
                fwd.start()
                passed.append(fwd)
        for a in range(n):
            copy(a, 0, sibling, me).wait_recv()
        for j, chip in enumerate(chips):
            for a in range(n):
                copy(a, 4 + j, (*chip, 1 - c), me).wait_recv()
        for cp in first + passed:
            cp.wait_send()
        for cp in mine:
            cp.wait()

    return pl.pallas_call(
        body, name=name,
        out_shape=[jax.ShapeDtypeStruct((N_DEV,) + s.shape, s.dtype) for s in shards],
        in_specs=[_ANY] * n, out_specs=[_ANY] * n,
        scratch_shapes=[pltpu.SemaphoreType.DMA((7 * n,)), pltpu.SemaphoreType.DMA((7 * n,)),
                        pltpu.SemaphoreType.DMA((n,))],
    )(*shards)


def _exchange_sibling(gs, *, name):
    n = len(gs)

    def body(*refs):
        g_refs, out_refs = refs[:n], refs[n:2 * n]
        send_sems, recv_sems = refs[2 * n:]
        x, y, c = lax.axis_index("x"), lax.axis_index("y"), lax.axis_index("c")
        copies = []
        for a in range(n):
            for q in range(4):
                copies.append(pltpu.make_async_remote_copy(
                    src_ref=g_refs[a].at[2 * q + (1 - c)], dst_ref=out_refs[a].at[q],
                    send_sem=send_sems.at[4 * a + q], recv_sem=recv_sems.at[4 * a + q],
                    device_id=(x, y, 1 - c), device_id_type=_MESH_ID))
        for cp in copies:
            cp.start()
        for cp in copies:
            cp.wait_recv()
        for cp in copies:
            cp.wait_send()

    return pl.pallas_call(
        body, name=name,
        out_shape=[jax.ShapeDtypeStruct((4,) + g.shape[1:], g.dtype) for g in gs],
        in_specs=[_ANY] * n, out_specs=[_ANY] * n,
        scratch_shapes=[pltpu.SemaphoreType.DMA((4 * n,)), pltpu.SemaphoreType.DMA((4 * n,))],
    )(*gs)


def _exchange_chips(hs, *, name):
    n = len(hs)

    def body(*refs):
        h_refs, out_refs = refs[:n], refs[n:2 * n]
        send_sems, recv_sems = refs[2 * n:]
        x, y, c = lax.axis_index("x"), lax.axis_index("y"), lax.axis_index("c")
        chips = [(1 - x, y), (x, 1 - y), (1 - x, 1 - y)]
        copies = []
        for a in range(n):
            for k, (px, py) in enumerate(chips):
                copies.append(pltpu.make_async_remote_copy(
                    src_ref=h_refs[a].at[2 * px + py], dst_ref=out_refs[a].at[k],
                    send_sem=send_sems.at[3 * a + k], recv_sem=recv_sems.at[3 * a + k],
                    device_id=(px, py, c), device_id_type=_MESH_ID))
        for cp in copies:
            cp.start()
        for cp in copies:
            cp.wait_recv()
        for cp in copies:
            cp.wait_send()

    return pl.pallas_call(
        body, name=name,
        out_shape=[jax.ShapeDtypeStruct((3,) + h.shape[1:], h.dtype) for h in hs],
        in_specs=[_ANY] * n, out_specs=[_ANY] * n,
        scratch_shapes=[pltpu.SemaphoreType.DMA((3 * n,)), pltpu.SemaphoreType.DMA((3 * n,))],
    )(*hs)


_BIG = (("ffn1_w_up", 2), ("ffn1_w_down", 1), ("mix_w_in", 2), ("w_uq", 2), ("w_ukv", 2),
        ("w_br_conv", 2), ("w_br_mla", 2), ("w_o", 1), ("ffn2_w_up", 2), ("ffn2_w_down", 1))
_SMALL_SHARDED = (("meta_tokens", 1), ("mix_b_gate", 2), ("conv_w", 2), ("ln_g", 2), ("ln_b", 2))
_SMALL_REPL = ("q_norm_g", "kv_norm_g")


BIG_ROW_ALIGN = 64


def _pack(arrs, dtype, row_align=8):
    flat = jnp.concatenate([a.reshape(-1).astype(dtype) for a in arrs])
    n = flat.shape[0]
    rows = -(-n // (row_align * FLAT_W)) * row_align
    return jnp.pad(flat, (0, rows * FLAT_W - n)).reshape(rows, FLAT_W)


def _unpack(flat, shapes):
    flat = flat.reshape(-1)
    out, off = [], 0
    for s in shapes:
        n = int(np.prod(s))
        out.append(flat[off:off + n].reshape(s))
        off += n
    return out


def _unpack_gathered(gathered, shapes, axes):
    g2 = gathered.reshape(N_DEV, -1)
    out, off = [], 0
    for s, ax in zip(shapes, axes):
        n = int(np.prod(s))
        blocks = g2[:, off:off + n].reshape((N_DEV,) + tuple(s))
        full = jnp.moveaxis(blocks, 0, ax)
        out.append(full.reshape(tuple(s[:ax]) + (N_DEV * s[ax],) + tuple(s[ax + 1:])))
        off += n
    return out


def _to_dest(full, ax):
    s = full.shape
    split = full.reshape(s[:ax] + (N_DEV, s[ax] // N_DEV) + s[ax + 1:])
    return jnp.moveaxis(split, ax, 0)


def _from_blocks(g, ax):
    full = jnp.moveaxis(g, 0, ax)
    s = full.shape
    return full.reshape(s[:ax] + (s[ax] * s[ax + 1],) + s[ax + 2:])


def _rope_tables(T):
    inv_freq = 1.0 / (ROPE_BASE ** (jnp.arange(0, QK_ROPE, 2, dtype=F32) / QK_ROPE))
    ang = jnp.arange(T, dtype=F32)[:, None] * inv_freq[None, :]
    cos, sin = jnp.cos(ang), jnp.sin(ang)
    half = QK_ROPE // 2
    ones = jnp.ones((T, QK_NOPE), F32)
    zeros = lambda w: jnp.zeros((T, w), F32)
    c = jnp.concatenate([ones, cos, cos, zeros(HEAD_PAD - QK_NOPE - QK_ROPE)], axis=1)
    s1 = jnp.concatenate([zeros(QK_NOPE + half), sin, zeros(HEAD_PAD - QK_NOPE - QK_ROPE)], axis=1)
    s2 = jnp.concatenate([zeros(QK_NOPE), -sin, zeros(HEAD_PAD - QK_NOPE - half)], axis=1)
    return c, s1, s2


def _layer_weights(W, l):
    w_in = W["mix_w_in"][l]
    w_in_p = jnp.concatenate([w_in[:, :Z_KR_END], jnp.zeros((D_MODEL, D_IN_PAD - D_IN_REAL), BF16),
                              w_in[:, Z_KR_END:]], axis=1)
    w_uq = W["w_uq"][l].reshape(Q_LORA, MLA_HEADS, QK_NOPE + QK_ROPE)
    wq = jnp.pad(w_uq, ((0, 0), (0, 0), (0, HEAD_PAD - QK_NOPE - QK_ROPE))).reshape(Q_LORA, D_QK)
    w_ukv = W["w_ukv"][l].reshape(KV_LORA, MLA_HEADS, QK_NOPE + V_HEAD)
    wk_top = jnp.pad(w_ukv[:, :, :QK_NOPE], ((0, 0), (0, 0), (0, HEAD_PAD - QK_NOPE))).reshape(KV_LORA, D_QK)
    place = np.zeros((Q_LORA - KV_LORA, MLA_HEADS, HEAD_PAD), np.float32)
    for r in range(QK_ROPE):
        place[r, :, QK_NOPE + r] = 1.0
    wk = jnp.concatenate([wk_top, jnp.asarray(place.reshape(Q_LORA - KV_LORA, D_QK), BF16)], axis=0)
    wv = w_ukv[:, :, QK_NOPE:].reshape(KV_LORA, D_MLA)
    wv_ext = jnp.pad(w_ukv[:, :, QK_NOPE:], ((0, 0), (0, 0), (0, HEAD_PAD - V_HEAD))).reshape(KV_LORA, D_QK)
    return dict(
        ffn1_down=W["ffn1_w_down"][l], ffn2_down=W["ffn2_w_down"][l],
        w_in=w_in_p, wq=wq, wk=wk, wv=wv, wv_ext=wv_ext,
        wbc=W["w_br_conv"][l], wbm=W["w_br_mla"][l], wo=W["w_o"][l])


def _row8(v):
    return jnp.pad(v, ((0, 8 - v.shape[0]), (0, 0)))


def _local_step(x, tgt, W, Wg, S):
    t_real = N_META + x.shape[0]
    T = -(-t_real // ROW_ALIGN) * ROW_ALIGN
    pad = T - t_real
    tm = _tile(T, (384, 256, 128))
    tms = _tile(T, (768, 256, 128))
    blk = _tile(T, (768, 256, 128))
    tabs = _rope_tables(T)

    h0 = jnp.concatenate([S["meta_tokens"], x, jnp.zeros((pad, D_MODEL), F32)], axis=0)
    tgt_p = jnp.concatenate([jnp.zeros((N_META, D_MODEL), F32), tgt, jnp.zeros((pad, D_MODEL), F32)], axis=0)

    def ffn_fwd(h, hb, up8, l, down, g, b, tag):
        gu, a = _ffn_up(hb, up8, l, name=f"ffn_up_{tag}", tm=tm)
        r, y, yb = _mm_res_ln(a, down, h, g, b, scale=0.5, name=f"ffn_down_ln_{tag}", tm=tm)
        return dict(gu=gu, a=a, r=r), y, yb

    saved = []
    h, hb = h0, h0.astype(BF16)
    for l in range(DEPTH):
        lw = _layer_weights(W, l)
        sv = dict(lw=lw, h_in=h, h_in_b=hb)
        sv["f1"], h1, h1b = ffn_fwd(h, hb, Wg["ffn1_w_up"], l, lw["ffn1_down"],
                                    S["ln_g"][l, 0:1], S["ln_b"][l, 0:1], f"{l}a")
        z = _mm(h1b, lw["w_in"], out_dtype=F32, name=f"mix_in_{l}", tm=tms, tn=1024, tk=D_MODEL)
        conv_w8 = _row8(S["conv_w"][l])
        bg8 = _row8(S["mix_b_gate"][l])
        yc = _conv_fwd(z, conv_w8, name=f"conv_fwd_{l}", tm=tms)
        gq, gkv = S["q_norm_g"][l:l + 1], S["kv_norm_g"][l:l + 1]
        q, k, v, qn, kin = _qkv_proj(z, gq, gkv, lw["wq"], lw["wk"], lw["wv_ext"], tabs, name=f"qkv_proj_{l}", tm=tm)
        o, lse = _attn_fwd(q, k, v, name=f"attn_fwd_{l}", blk=blk)
        mg, pa, pb = _merge(yc, o, lw["wbc"], lw["wbm"], z, bg8, name=f"merge_{l}", tm=tm)
        r2, h2, h2b = _mm_res_ln(mg, lw["wo"], h1, S["ln_g"][l, 1:2], S["ln_b"][l, 1:2], scale=1.0,
                                 name=f"wo_ln_{l}", tm=tm)
        sv["f2"], h3, h3b = ffn_fwd(h2, h2b, Wg["ffn2_w_up"], l, lw["ffn2_down"],
                                    S["ln_g"][l, 2:3], S["ln_b"][l, 2:3], f"{l}b")
        sv.update(h1b=h1b, z=z, conv_w8=conv_w8, bg8=bg8, yc=yc, gq=gq, gkv=gkv, q=q, k=k, v=v, qn=qn, kin=kin,
                  o=o, lse=lse, mg=mg, pa=pa, pb=pb, r2=r2, h2b=h2b)
        saved.append(sv)
        h, hb = h3, h3b

    dh, loss8 = _loss_head(h, tgt_p, t_real=t_real, name="loss_head", tm=tm)

    tk = blk
    grads = {n: [None] * DEPTH for n, _ in _BIG}
    for n in ("mix_b_gate", "conv_w", "q_norm_g", "kv_norm_g"):
        grads[n] = [None] * DEPTH
    grads["ln_g"] = [[None] * 3 for _ in range(DEPTH)]
    grads["ln_b"] = [[None] * 3 for _ in range(DEPTH)]

    def ffn_bwd(dy, f, h_in_b, up8, l, down, g, tag):
        dr, dfb, dg, db = _ln_bwd(dy, f["r"], g, scale=0.5, name=f"ln_bwd_{tag}", tm=tm)
        dgu = _ffn_bwd_mid(dfb, down, f["gu"], name=f"ffn_bwd_mid_{tag}", tm=tm)
        d_down = _mm_tn_call(
            f["a"], dfb,
            pl.BlockSpec((None, tk, FF_BLK), lambda i, j, k: (i, k, 0)),
            pl.BlockSpec((tk, D_MODEL), lambda i, j, k: (k, 0)),
            out_shape=jax.ShapeDtypeStruct((D_FF, D_MODEL), F32),
            out_spec=pl.BlockSpec((FF_BLK, D_MODEL), lambda i, j, k: (i, 0)),
            grid=(FF_HALF_BLOCKS, 1, T // tk), name=f"dw_down_{tag}")
        d_up = _mm_tn_call(
            h_in_b, dgu,
            pl.BlockSpec((tk, D_MODEL), lambda i, j, k: (k, 0)),
            pl.BlockSpec((None, None, tk, FF_BLK),
                         lambda i, j, k: (j // FF_HALF_BLOCKS, j % FF_HALF_BLOCKS, k, 0)),
            out_shape=jax.ShapeDtypeStruct((N_DEV, D_MODEL, FF_BLK), F32),
            out_spec=pl.BlockSpec((None, D_MODEL, FF_BLK), lambda i, j, k: (j, 0, 0)),
            grid=(1, N_DEV, T // tk), name=f"dw_up_{tag}")
        row = pl.BlockSpec((tms, D_MODEL), lambda i, j, k: (i, 0))
        dh_in = _mm_call(
            dgu, up8,
            pl.BlockSpec((None, None, tms, FF_BLK),
                         lambda i, j, k: (k // FF_HALF_BLOCKS, k % FF_HALF_BLOCKS, i, 0)),
            pl.BlockSpec((None, None, D_MODEL, FF_BLK), lambda i, j, k: (k, l, 0, 0)),
            out_shape=jax.ShapeDtypeStruct((T, D_MODEL), F32), out_spec=row, acc_shape=(tms, D_MODEL),
            grid=(T // tms, 1, N_DEV), name=f"ffn_dx_{tag}", trans_b=True, res=dr, res_spec=row, res_scale=ALPHA)
        return dh_in, d_up, d_down, dg[0], db[0]

    for l in reversed(range(DEPTH)):
        sv = saved[l]
        lw = sv["lw"]
        dh, grads["ffn2_w_up"][l], grads["ffn2_w_down"][l], grads["ln_g"][l][2], grads["ln_b"][l][2] = ffn_bwd(
            dh, sv["f2"], sv["h2b"], Wg["ffn2_w_up"], l, lw["ffn2_down"], S["ln_g"][l, 2:3], f"{l}b")
        dr2, dmb, dg, db = _ln_bwd(dh, sv["r2"], S["ln_g"][l, 1:2], scale=1.0, name=f"ln_bwd_{l}m", tm=tm)
        grads["ln_g"][l][1], grads["ln_b"][l][1] = dg[0], db[0]
        grads["w_o"][l] = _mm_tn(sv["mg"], dmb, name=f"dw_o_{l}", tm=D_MODEL, tn=D_MODEL, tk=tk)
        dpa, dpb, dgate, dbg = _wo_bwd(dmb, lw["wo"], sv["z"], sv["bg8"], sv["pa"], sv["pb"], name=f"wo_bwd_{l}", tm=tm)
        grads["mix_b_gate"][l] = dbg[0:2]
        grads["w_br_conv"][l] = _mm_tn(sv["yc"], dpa, name=f"dw_br_conv_{l}", tm=D_CONV, tn=D_MODEL, tk=tk)
        grads["w_br_mla"][l] = _mm_tn(sv["o"], dpb, name=f"dw_br_mla_{l}", tm=D_MLA, tn=D_MODEL, tk=tk)
        dyc = _mm(dpa, lw["wbc"], trans_b=True, out_dtype=F32, name=f"d_yconv_{l}", tm=tms, tn=D_CONV, tk=D_MODEL)
        dym = _mm(dpb, lw["wbm"], trans_b=True, out_dtype=BF16, name=f"d_ymla_{l}", tm=tms, tn=D_MLA, tk=D_MODEL)
        dz_conv, dcw = _conv_bwd(dyc, sv["z"], sv["conv_w8"], name=f"conv_bwd_{l}", tm=tms)
        grads["conv_w"][l] = dcw[0:CONV_WIDTH]
        dq, dk, dv = _attn_bwd(sv["q"], sv["k"], sv["v"], sv["o"], dym, sv["lse"], name=f"attn_bwd_{l}", blk=blk)
        dz_mid, dqb, dkb, dvb, dgq, dgkv = _qkv_bwd(dq, dk, dv, sv["z"], sv["gq"], sv["gkv"], lw["wq"], lw["wk"],
                                                    lw["wv"], tabs, name=f"qkv_bwd_{l}", tm=tm)
        grads["q_norm_g"][l], grads["kv_norm_g"][l] = dgq[0], dgkv[0]
        d_wq = _mm_tn(sv["qn"], dqb, name=f"dw_uq_{l}", tm=Q_LORA, tn=D_QK, tk=tk)
        d_wk = _mm_tn(sv["kin"], dkb, name=f"dw_uk_{l}", tm=Q_LORA, tn=D_QK, tk=tk)
        d_wv = _mm_tn(sv["kin"], dvb, name=f"dw_uv_{l}", tm=Q_LORA, tn=D_MLA, tk=tk)
        grads["w_uq"][l] = d_wq.reshape(Q_LORA, MLA_HEADS, HEAD_PAD)[:, :, :QK_NOPE + QK_ROPE].reshape(Q_LORA, -1)
        d_kn = d_wk[:KV_LORA].reshape(KV_LORA, MLA_HEADS, HEAD_PAD)[:, :, :QK_NOPE]
        d_vv = d_wv[:KV_LORA].reshape(KV_LORA, MLA_HEADS, V_HEAD)
        grads["w_ukv"][l] = jnp.concatenate([d_kn, d_vv], axis=-1).reshape(KV_LORA, -1)
        dz = jnp.concatenate([dz_conv, dz_mid, dgate], axis=1)
        d_win = _mm_tn(sv["h1b"], dz, name=f"dw_in_{l}", tm=D_MODEL, tn=1024, tk=tk)
        grads["mix_w_in"][l] = jnp.concatenate([d_win[:, :Z_KR_END], d_win[:, Z_KR_END + D_IN_PAD - D_IN_REAL:]], axis=1)
        dh = _mm(dz, lw["w_in"], trans_b=True, out_dtype=F32, name=f"mix_dx_{l}", res=dr2, res_scale=ALPHA,
                 tm=tms, tn=D_MODEL, tk=1024)
        dh, grads["ffn1_w_up"][l], grads["ffn1_w_down"][l], grads["ln_g"][l][0], grads["ln_b"][l][0] = ffn_bwd(
            dh, sv["f1"], sv["h_in_b"], Wg["ffn1_w_up"], l, lw["ffn1_down"], S["ln_g"][l, 0:1], f"{l}a")

    big = {}
    for n, ax in _BIG:
        if n in Wg:
            big[n] = jnp.stack(grads[n], axis=1)
        else:
            big[n] = _to_dest(jnp.stack(grads[n]), ax)
    small = {n: jnp.stack(grads[n]) for n in ("mix_b_gate", "conv_w", "q_norm_g", "kv_norm_g")}
    small["ln_g"] = jnp.stack([jnp.stack(g) for g in grads["ln_g"]])
    small["ln_b"] = jnp.stack([jnp.stack(g) for g in grads["ln_b"]])
    small["meta_tokens"] = dh[:N_META]
    return loss8, dh[N_META:t_real], big, small


def kernel(x, meta_tokens, ffn1_w_up, ffn1_w_down, mix_w_in, mix_b_gate, conv_w, q_norm_g, w_uq, kv_norm_g, w_ukv, w_br_conv, w_br_mla, w_o, ffn2_w_up, ffn2_w_down, ln_g, ln_b, loss_target, m_meta_tokens, m_ffn1_w_up, m_ffn1_w_down, m_mix_w_in, m_mix_b_gate, m_conv_w, m_q_norm_g, m_w_uq, m_kv_norm_g, m_w_ukv, m_w_br_conv, m_w_br_mla, m_w_o, m_ffn2_w_up, m_ffn2_w_down, m_ln_g, m_ln_b, v_meta_tokens, v_ffn1_w_up, v_ffn1_w_down, v_mix_w_in, v_mix_b_gate, v_conv_w, v_q_norm_g, v_w_uq, v_kv_norm_g, v_w_ukv, v_w_br_conv, v_w_br_mla, v_w_o, v_ffn2_w_up, v_ffn2_w_down, v_ln_g, v_ln_b):
    names = ["meta_tokens", "ffn1_w_up", "ffn1_w_down", "mix_w_in", "mix_b_gate", "conv_w", "q_norm_g", "w_uq",
             "kv_norm_g", "w_ukv", "w_br_conv", "w_br_mla", "w_o", "ffn2_w_up", "ffn2_w_down", "ln_g", "ln_b"]
    w = dict(zip(names, (meta_tokens, ffn1_w_up, ffn1_w_down, mix_w_in, mix_b_gate, conv_w, q_norm_g, w_uq,
                         kv_norm_g, w_ukv, w_br_conv, w_br_mla, w_o, ffn2_w_up, ffn2_w_down, ln_g, ln_b)))
    m = dict(zip(names, (m_meta_tokens, m_ffn1_w_up, m_ffn1_w_down, m_mix_w_in, m_mix_b_gate, m_conv_w, m_q_norm_g,
                         m_w_uq, m_kv_norm_g, m_w_ukv, m_w_br_conv, m_w_br_mla, m_w_o, m_ffn2_w_up, m_ffn2_w_down,
                         m_ln_g, m_ln_b)))
    v = dict(zip(names, (v_meta_tokens, v_ffn1_w_up, v_ffn1_w_down, v_mix_w_in, v_mix_b_gate, v_conv_w, v_q_norm_g,
                         v_w_uq, v_kv_norm_g, v_w_ukv, v_w_br_conv, v_w_br_mla, v_w_o, v_ffn2_w_up, v_ffn2_w_down,
                         v_ln_g, v_ln_b)))
    ix, iy, ic = lax.axis_index("x"), lax.axis_index("y"), lax.axis_index("c")
    dev = 4 * ix + 2 * iy + ic

    big_names = [n for n, _ in _BIG]
    big_axes = [a for _, a in _BIG]
    big_shapes = [w[n].shape for n in big_names]
    small_names = [n for n, _ in _SMALL_SHARDED]
    small_axes = [a for _, a in _SMALL_SHARDED]
    small_shapes = [w[n].shape for n in small_names]
    gathered = _all_gather([w[n].astype(BF16) for n in big_names] + [_pack([w[n] for n in small_names], F32)],
                           name="all_gather_weights")
    Wg = dict(zip(big_names, gathered[:-1]))
    W = {n: _from_blocks(Wg[n], ax) for n, ax in _BIG if not n.endswith("w_up")}
    S = dict(zip(small_names, _unpack_gathered(gathered[-1], small_shapes, small_axes)))
    S["q_norm_g"], S["kv_norm_g"] = q_norm_g, kv_norm_g

    loss8, grad_x, g_dest, G = _local_step(x[0], loss_target[0], W, {n: Wg[n] for n in ("ffn1_w_up", "ffn2_w_up")}, S)

    from_sibling = _exchange_sibling([g_dest[n] for n in big_names], name="rs_sibling")
    c_index = ic.reshape(1).astype(jnp.int32)
    pair = [_pair_sum(g_dest[n], r, c_index, name=f"rs_sibling_sum_{n}") for n, r in zip(big_names, from_sibling)]
    chip_sum = [p[0] for p in pair]
    from_chips = _exchange_chips([p[1] for p in pair], name="rs_chips")
    my_chip = (2 * ix + iy).reshape(1).astype(jnp.int32)
    big_res = [{}, {}, {}, {}]
    for n, cs, fc in zip(big_names, chip_sum, from_chips):
        res = _adamw(w[n], m[n], v[n], [(cs, None), (fc, 0), (fc, 1), (fc, 2)], my_chip, name=f"adamw_{n}")
        for kind in range(4):
            big_res[kind][n] = res[kind]

    small_all = small_names + list(_SMALL_REPL)
    part = _pack([G[n] for n in small_all] + [loss8[0, 0:1]], F32)
    full_shapes = [G[n].shape for n in small_all] + [(1,)]
    summed = _sum8(_all_gather([part], name="all_gather_small_grads")[0], name="sum_small_grads")
    unpacked = _unpack(summed, full_shapes)
    loss = unpacked[-1][0]
    g_full = dict(zip(small_all, unpacked[:-1]))
    g_loc = []
    for n in small_all:
        if n in _SMALL_REPL:
            g_loc.append(g_full[n])
        else:
            ax = dict(_SMALL_SHARDED)[n]
            g_loc.append(lax.dynamic_slice_in_dim(g_full[n], dev * w[n].shape[ax], w[n].shape[ax], axis=ax))
    loc_shapes = [w[n].shape for n in small_all]
    g_pack = _pack(g_loc, F32)
    small_out = _adamw(_pack([w[n] for n in small_all], F32)[None], _pack([m[n] for n in small_all], F32)[None],
                       _pack([v[n] for n in small_all], F32)[None], [(g_pack[None, None], 0)],
                       jnp.zeros((1,), jnp.int32), name="adamw_small")
    small_res = [dict(zip(small_all, _unpack(o, loc_shapes))) for o in small_out]

    outs = [loss, grad_x[None]]
    for kind in range(4):
        for n in names:
            outs.append(big_res[kind][n] if n in big_res[kind] else small_res[kind][n])
    return tuple(outs)
```

```python
import functools

import numpy as np
import jax
import jax.numpy as jnp
from jax import lax
from jax.experimental import pallas as pl
from jax.experimental.pallas import tpu as pltpu

F32 = jnp.float32
BF16 = jnp.bfloat16

D_MODEL = 1024
DEPTH = 2
N_META = 16
D_CONV = 512
CONV_WIDTH = 3
MLA_HEADS = 8
QK_NOPE = 64
QK_ROPE = 32
V_HEAD = 64
Q_LORA = 256
KV_LORA = 128
D_MLA = MLA_HEADS * V_HEAD
ROPE_BASE = 10000.0
NEG_INF = -1e30
D_FF = 2816
ALPHA = (2 * DEPTH) ** 0.25
LN_EPS = 1e-5
RMS_EPS = 1e-6
ATTN_SCALE = (QK_NOPE + QK_ROPE) ** -0.5
ADAM_LR = 0.001
ADAM_B1 = 0.9
ADAM_B2 = 0.999
ADAM_EPS = 1e-08
ADAM_WD = 0.01
ADAM_STEP = 10

N_DEV = 8
HEAD_PAD = 128
HEADS_PER_STEP = 2
FF_BLK = 2 * D_FF // N_DEV
FF_HALF_BLOCKS = N_DEV // 2
D_QK = MLA_HEADS * HEAD_PAD
Z_CONV = 0
Z_MID = 1536
Z_GATE = 2048
D_IN_PAD = 4096
D_IN_REAL = 4000
Z_KR_END = Z_MID + Q_LORA + KV_LORA + QK_ROPE

V7X_VMEM_LIMIT = 56 * 1024 * 1024
LANE = 128
ROW_ALIGN = 256


def _tile(n, cands):
    for c in cands:
        if n % c == 0:
            return c
    raise ValueError(f"no tile for {n} in {cands}")


def _params(sem):
    return pltpu.CompilerParams(dimension_semantics=sem, vmem_limit_bytes=V7X_VMEM_LIMIT)


def _mm_call(a, b, a_spec, b_spec, *, out_shape, out_spec, acc_shape, grid, name, trans_b=False,
             res=None, res_spec=None, res_scale=1.0, pieces=1):
    nk = grid[2]
    has_res = res is not None
    out_dtype = out_shape.dtype
    dims = (((1,), (1,)), ((), ())) if trans_b else (((1,), (0,)), ((), ()))

    def body(*refs):
        if has_res:
            a_ref, b_ref, r_ref, o_ref, acc = refs
        else:
            a_ref, b_ref, o_ref, acc = refs
        k = pl.program_id(2)
        if pieces == 1:
            part = lax.dot_general(a_ref[...], b_ref[...], dims, preferred_element_type=F32)
        else:
            part = lax.dot_general(a_ref[0], b_ref[0], dims, preferred_element_type=F32)
            for p in range(1, pieces):
                part = part + lax.dot_general(a_ref[p], b_ref[p], dims, preferred_element_type=F32)

        @pl.when(k == 0)
        def _():
            acc[...] = part

        @pl.when(k > 0)
        def _():
            acc[...] += part

        @pl.when(k == nk - 1)
        def _():
            out = acc[...]
            if has_res:
                out = out + res_scale * r_ref[...]
            o_ref[...] = out.astype(out_dtype)

    in_specs = [a_spec, b_spec]
    args = [a, b]
    if has_res:
        in_specs.append(res_spec)
        args.append(res)
    return pl.pallas_call(
        body, name=name, grid=grid, in_specs=in_specs, out_specs=out_spec, out_shape=out_shape,
        scratch_shapes=[pltpu.VMEM(acc_shape, F32)],
        compiler_params=_params(("parallel", "parallel", "arbitrary")),
    )(*args)


def _mm(a, b, *, out_dtype, name, trans_b=False, res=None, res_scale=1.0, tm, tn, tk):
    M, K = a.shape
    N = b.shape[0] if trans_b else b.shape[1]
    assert M % tm == 0 and N % tn == 0 and K % tk == 0
    b_spec = (pl.BlockSpec((tn, tk), lambda i, j, k: (j, k)) if trans_b
              else pl.BlockSpec((tk, tn), lambda i, j, k: (k, j)))
    tile = pl.BlockSpec((tm, tn), lambda i, j, k: (i, j))
    return _mm_call(a, b, pl.BlockSpec((tm, tk), lambda i, j, k: (i, k)), b_spec,
                    out_shape=jax.ShapeDtypeStruct((M, N), out_dtype), out_spec=tile, acc_shape=(tm, tn),
                    grid=(M // tm, N // tn, K // tk), name=name, trans_b=trans_b,
                    res=res, res_spec=tile, res_scale=res_scale)


def _mm_tn_call(a, b, a_spec, b_spec, *, out_shape, out_spec, grid, name):
    def body(a_ref, b_ref, o_ref):
        k = pl.program_id(2)
        part = lax.dot_general(a_ref[...], b_ref[...], (((0,), (0,)), ((), ())),
                               preferred_element_type=F32)

        @pl.when(k == 0)
        def _():
            o_ref[...] = part

        @pl.when(k > 0)
        def _():
            o_ref[...] += part

    return pl.pallas_call(
        body, name=name, grid=grid, in_specs=[a_spec, b_spec], out_specs=out_spec, out_shape=out_shape,
        compiler_params=_params(("parallel", "parallel", "arbitrary")),
    )(a, b)


def _mm_tn(a, b, *, name, tm, tn, tk):
    T, M = a.shape
    N = b.shape[1]
    assert M % tm == 0 and N % tn == 0 and T % tk == 0
    return _mm_tn_call(a, b, pl.BlockSpec((tk, tm), lambda i, j, k: (k, i)),
                       pl.BlockSpec((tk, tn), lambda i, j, k: (k, j)),
                       out_shape=jax.ShapeDtypeStruct((M, N), F32),
                       out_spec=pl.BlockSpec((tm, tn), lambda i, j, k: (i, j)),
                       grid=(M // tm, N // tn, T // tk), name=name)


def _ffn_up(hb, w_up8, l, *, name, tm):
    T = hb.shape[0]

    def body(h_ref, wg_ref, wu_ref, gu_ref, a_ref):
        h = h_ref[...]
        g = jnp.dot(h, wg_ref[...], preferred_element_type=F32)
        u = jnp.dot(h, wu_ref[...], preferred_element_type=F32)
        gu_ref[0] = g.astype(BF16)
        gu_ref[1] = u.astype(BF16)
        a_ref[...] = (g * jax.nn.sigmoid(g) * u).astype(BF16)

    return pl.pallas_call(
        body, name=name, grid=(FF_HALF_BLOCKS, T // tm),
        in_specs=[pl.BlockSpec((tm, D_MODEL), lambda j, i: (i, 0)),
                  pl.BlockSpec((None, None, D_MODEL, FF_BLK), lambda j, i: (j, l, 0, 0)),
                  pl.BlockSpec((None, None, D_MODEL, FF_BLK), lambda j, i: (j + FF_HALF_BLOCKS, l, 0, 0))],
        out_specs=[pl.BlockSpec((2, None, tm, FF_BLK), lambda j, i: (0, j, i, 0)),
                   pl.BlockSpec((None, tm, FF_BLK), lambda j, i: (j, i, 0))],
        out_shape=[jax.ShapeDtypeStruct((2, FF_HALF_BLOCKS, T, FF_BLK), BF16),
                   jax.ShapeDtypeStruct((FF_HALF_BLOCKS, T, FF_BLK), BF16)],
        compiler_params=_params(("parallel", "parallel")),
    )(hb, w_up8, w_up8)


def _mm_res_ln(a, w, res, g, b, *, scale, name, tm):
    split = a.ndim == 3
    if split:
        S, T, Ks = a.shape
        K = S * Ks
    else:
        T, K = a.shape

    def body(a_ref, w_ref, res_ref, g_ref, b_ref, r_ref, y_ref, yb_ref):
        if split:
            f = jnp.dot(a_ref[0], w_ref[0:Ks, :], preferred_element_type=F32)
            for s in range(1, S):
                f = f + jnp.dot(a_ref[s], w_ref[s * Ks:(s + 1) * Ks, :], preferred_element_type=F32)
        else:
            f = jnp.dot(a_ref[...], w_ref[...], preferred_element_type=F32)
        r = ALPHA * res_ref[...] + scale * f
        mu = jnp.mean(r, axis=-1, keepdims=True)
        xc = r - mu
        var = jnp.mean(xc * xc, axis=-1, keepdims=True)
        y = xc * lax.rsqrt(var + LN_EPS) * g_ref[...] + b_ref[...]
        r_ref[...] = r
        y_ref[...] = y
        yb_ref[...] = y.astype(BF16)

    row = pl.BlockSpec((tm, D_MODEL), lambda i: (i, 0))
    vec = pl.BlockSpec((1, D_MODEL), lambda i: (0, 0))
    return pl.pallas_call(
        body, name=name, grid=(T // tm,),
        in_specs=[pl.BlockSpec((S, tm, Ks), lambda i: (0, i, 0)) if split else pl.BlockSpec((tm, K), lambda i: (i, 0)),
                  pl.BlockSpec((K, D_MODEL), lambda i: (0, 0)), row, vec, vec],
        out_specs=[row, row, row],
        out_shape=[jax.ShapeDtypeStruct((T, D_MODEL), F32), jax.ShapeDtypeStruct((T, D_MODEL), F32),
                   jax.ShapeDtypeStruct((T, D_MODEL), BF16)],
        compiler_params=_params(("parallel",)),
    )(a, w, res, g, b)


def _conv_fwd(z, conv_w8, *, name, tm):
    T = z.shape[0]
    hb = tm // 8

    def body(b_ref, c_ref, h_ref, cp_ref, hp_ref, w_ref, y_ref):
        i = pl.program_id(0)
        u = c_ref[...] * h_ref[...]
        up = jnp.where(i > 0, cp_ref[...] * hp_ref[...], 0.0)
        ue = jnp.concatenate([up, u], axis=0)
        s1 = pltpu.roll(ue, 1, 0)[8:]
        s2 = pltpu.roll(ue, 2, 0)[8:]
        w = w_ref[...]
        conv = w[0:1] * s2 + w[1:2] * s1 + w[2:3] * u
        y_ref[...] = (b_ref[...] * conv).astype(BF16)

    def col(c):
        return pl.BlockSpec((tm, D_CONV), lambda i: (i, c))

    def prev(c):
        return pl.BlockSpec((8, D_CONV), lambda i: (jnp.maximum(i * hb - 1, 0), c))

    return pl.pallas_call(
        body, name=name, grid=(T // tm,),
        in_specs=[col(0), col(1), col(2), prev(1), prev(2), pl.BlockSpec((8, D_CONV), lambda i: (0, 0))],
        out_specs=pl.BlockSpec((tm, D_CONV), lambda i: (i, 0)),
        out_shape=jax.ShapeDtypeStruct((T, D_CONV), BF16),
        compiler_params=_params(("parallel",)),
    )(z, z, z, z, z, conv_w8)


def _rope(x, c, s1, s2):
    n = x.shape[-1]
    return x * c + pltpu.roll(x, 16, 1) * s1 + pltpu.roll(x, n - 16, 1) * s2


def _rope_t(d, c, s1, s2):
    n = d.shape[-1]
    return d * c + pltpu.roll(d * s1, n - 16, 1) + pltpu.roll(d * s2, 16, 1)


def _rms(x, g):
    rstd = lax.rsqrt(jnp.mean(x * x, axis=-1, keepdims=True) + RMS_EPS)
    return x * rstd * g


def _qkv_proj(z, gq, gkv, wq, wk, wv_ext, tabs, *, name, tm):
    T = z.shape[0]

    def body(z_ref, gq_ref, gkv_ref, wq_ref, wk_ref, wv_ref, c_ref, s1_ref, s2_ref,
             q_ref, k_ref, v_ref, qn_ref, kin_ref):
        zz = z_ref[...]
        qn = _rms(zz[:, :Q_LORA], gq_ref[...]).astype(BF16)
        kvn = _rms(zz[:, Q_LORA:Q_LORA + KV_LORA], gkv_ref[...]).astype(BF16)
        kin = jnp.concatenate([kvn, zz[:, Q_LORA + KV_LORA:].astype(BF16)], axis=-1)
        c = jnp.tile(c_ref[...], (1, MLA_HEADS))
        s1 = jnp.tile(s1_ref[...], (1, MLA_HEADS))
        s2 = jnp.tile(s2_ref[...], (1, MLA_HEADS))
        qpre = jnp.dot(qn, wq_ref[...], preferred_element_type=F32)
        kpre = jnp.dot(kin, wk_ref[...], preferred_element_type=F32)
        q_ref[...] = (_rope(qpre, c, s1, s2) * (ATTN_SCALE * LOG2_E)).astype(BF16)
        k_ref[...] = _rope(kpre, c, s1, s2).astype(BF16)
        vv = jnp.dot(kvn, wv_ref[...], preferred_element_type=F32)
        lane = lax.broadcasted_iota(jnp.int32, vv.shape, 1)
        v_ref[...] = jnp.where((lane & (HEAD_PAD - 1)) < V_HEAD, vv, 1.0).astype(BF16)
        qn_ref[...] = qn
        kin_ref[...] = kin

    def full(shape):
        return pl.BlockSpec(shape, lambda i: (0, 0))

    def rows(w, c=0):
        return pl.BlockSpec((tm, w), lambda i: (i, c))

    return pl.pallas_call(
        body, name=name, grid=(T // tm,),
        in_specs=[rows(512, Z_MID // 512), full((1, Q_LORA)), full((1, KV_LORA)),
                  full((Q_LORA, D_QK)), full((Q_LORA, D_QK)), full((KV_LORA, D_QK)),
                  rows(LANE), rows(LANE), rows(LANE)],
        out_specs=[rows(D_QK), rows(D_QK), rows(D_QK), rows(Q_LORA), rows(Q_LORA)],
        out_shape=[jax.ShapeDtypeStruct((T, D_QK), BF16), jax.ShapeDtypeStruct((T, D_QK), BF16),
                   jax.ShapeDtypeStruct((T, D_QK), BF16), jax.ShapeDtypeStruct((T, Q_LORA), BF16),
                   jax.ShapeDtypeStruct((T, Q_LORA), BF16)],
        compiler_params=_params(("parallel",)),
    )(z, gq, gkv, wq, wk, wv_ext, *tabs)


SOFTMAX_ROWS = 32
LOG2_E = 1.4426950408889634
_NT = (((1,), (1,)), ((), ()))
_TN = (((0,), (0,)), ((), ()))


def _diag_mask(s, row0, col0=0):
    row = lax.broadcasted_iota(jnp.int32, s.shape, 0) + row0
    col = lax.broadcasted_iota(jnp.int32, s.shape, 1) + col0
    return jnp.where(col <= row, s, NEG_INF)


def _attn_fwd(q, k, v, *, name, blk):
    T = q.shape[0]
    n = T // blk
    hp = HEADS_PER_STEP
    qi = np.array([i for i in range(n) for j in range(i + 1)], np.int32)
    kj = np.array([j for i in range(n) for j in range(i + 1)], np.int32)

    rc = _tile(blk, (SOFTMAX_ROWS,))

    def body(qi_ref, kj_ref, q_ref, k_ref, v_ref, o_ref, lse_ref, m_sc, acc_sc, s_sc, p_sc, red_sc):
        s_id = pl.program_id(1)
        i = qi_ref[s_id]
        j = kj_ref[s_id]

        @pl.when(j == 0)
        def _():
            m_sc[...] = jnp.full(m_sc.shape, NEG_INF, F32)
            acc_sc[...] = jnp.zeros(acc_sc.shape, F32)

        def head_step(hh, diagonal):
            hs = slice(hh * HEAD_PAD, (hh + 1) * HEAD_PAD)
            s_sc[hh] = lax.dot_general(q_ref[:, hs], k_ref[:, hs], _NT, preferred_element_type=F32)
            lanes = [slice(t * LANE, (t + 1) * LANE) for t in range(blk // LANE)]
            for r in range(blk // rc):
                rows = slice(r * rc, (r + 1) * rc)
                s = s_sc[hh, rows, :]
                if diagonal:
                    s = _diag_mask(s, r * rc)
                    s_sc[hh, rows, :] = s
                pm = s[:, lanes[0]]
                for t in lanes[1:]:
                    pm = jnp.maximum(pm, s[:, t])
                red_sc[hh, rows, :] = pm
            m_old = m_sc[hh]
            row_max = jnp.max(red_sc[hh], axis=-1, keepdims=True)
            m_new = jnp.maximum(m_old, jnp.broadcast_to(row_max, (blk, LANE)))
            a = jnp.exp2(m_old - m_new)
            m_sc[hh] = m_new
            for r in range(blk // rc):
                rows = slice(r * rc, (r + 1) * rc)
                mb = m_sc[hh, rows, :]
                for t in lanes:
                    p_sc[hh, rows, t] = jnp.exp2(s_sc[hh, rows, t] - mb).astype(BF16)
            acc_sc[hh] = a * acc_sc[hh] + jnp.dot(p_sc[hh], v_ref[:, hs], preferred_element_type=F32)

        @pl.when(j < i)
        def _():
            for hh in range(hp):
                head_step(hh, False)

        @pl.when(j == i)
        def _():
            for hh in range(hp):
                head_step(hh, True)
            for hh in range(hp):
                acc = acc_sc[hh]
                swapped = pltpu.roll(acc, V_HEAD, 1)
                o_ref[:, hh * V_HEAD:(hh + 1) * V_HEAD] = (acc / swapped)[:, :V_HEAD].astype(BF16)
                lane = lax.broadcasted_iota(jnp.int32, acc.shape, 1)
                denom = jnp.where(lane < V_HEAD, swapped, acc)
                lse_ref[hh] = m_sc[hh] + jnp.log(denom) * LOG2_E

    grid_spec = pltpu.PrefetchScalarGridSpec(
        num_scalar_prefetch=2, grid=(MLA_HEADS // hp, len(qi)),
        in_specs=[pl.BlockSpec((blk, hp * HEAD_PAD), lambda g, s, qi, kj: (qi[s], g)),
                  pl.BlockSpec((blk, hp * HEAD_PAD), lambda g, s, qi, kj: (kj[s], g)),
                  pl.BlockSpec((blk, hp * HEAD_PAD), lambda g, s, qi, kj: (kj[s], g))],
        out_specs=[pl.BlockSpec((blk, hp * V_HEAD), lambda g, s, qi, kj: (qi[s], g)),
                   pl.BlockSpec((hp, blk, LANE), lambda g, s, qi, kj: (g, qi[s], 0))],
        scratch_shapes=[pltpu.VMEM((hp, blk, LANE), F32), pltpu.VMEM((hp, blk, HEAD_PAD), F32),
                        pltpu.VMEM((hp, blk, blk), F32), pltpu.VMEM((hp, blk, blk), BF16),
                        pltpu.VMEM((hp, blk, LANE), F32)])
    return pl.pallas_call(
        body, name=name, grid_spec=grid_spec,
        out_shape=[jax.ShapeDtypeStruct((T, D_MLA), BF16), jax.ShapeDtypeStruct((MLA_HEADS, T, LANE), F32)],
        compiler_params=_params(("parallel", "arbitrary")),
    )(jnp.asarray(qi), jnp.asarray(kj), q, k, v)


def _merge(yc, ym, wbc, wbm, z, bg, *, name, tm):
    T = yc.shape[0]

    def body(yc_ref, ym_ref, wbc_ref, wbm_ref, gc_ref, gm_ref, bg_ref, mg_ref, pa_ref, pb_ref):
        pa = jnp.dot(yc_ref[...], wbc_ref[...], preferred_element_type=F32)
        pb = jnp.dot(ym_ref[...], wbm_ref[...], preferred_element_type=F32)
        bgv = bg_ref[...]
        sa = jax.nn.sigmoid(gc_ref[...] + bgv[0:1])
        sb = jax.nn.sigmoid(gm_ref[...] + bgv[1:2])
        mg_ref[...] = (sa * pa + sb * pb).astype(BF16)
        pa_ref[...] = pa.astype(BF16)
        pb_ref[...] = pb.astype(BF16)

    row = pl.BlockSpec((tm, D_MODEL), lambda i: (i, 0))
    return pl.pallas_call(
        body, name=name, grid=(T // tm,),
        in_specs=[pl.BlockSpec((tm, D_CONV), lambda i: (i, 0)), pl.BlockSpec((tm, D_MLA), lambda i: (i, 0)),
                  pl.BlockSpec((D_CONV, D_MODEL), lambda i: (0, 0)), pl.BlockSpec((D_MLA, D_MODEL), lambda i: (0, 0)),
                  pl.BlockSpec((tm, D_MODEL), lambda i: (i, Z_GATE // D_MODEL)),
                  pl.BlockSpec((tm, D_MODEL), lambda i: (i, Z_GATE // D_MODEL + 1)),
                  pl.BlockSpec((8, D_MODEL), lambda i: (0, 0))],
        out_specs=[row, row, row],
        out_shape=[jax.ShapeDtypeStruct((T, D_MODEL), BF16)] * 3,
        compiler_params=_params(("parallel",)),
    )(yc, ym, wbc, wbm, z, z, bg)


def _loss_head(h, tgt, *, t_real, name, tm):
    T = h.shape[0]

    def body(h_ref, t_ref, dy_ref, loss_ref):
        i = pl.program_id(0)
        row = lax.broadcasted_iota(jnp.int32, (tm, 1), 0) + i * tm
        valid = (row >= N_META) & (row < t_real)
        err = jnp.where(valid, h_ref[...] - t_ref[...], 0.0)
        dy_ref[...] = err * (1.0 / D_MODEL)
        part = 0.5 * jnp.sum(jnp.sum(err * err, axis=-1, keepdims=True) * (1.0 / D_MODEL), axis=0, keepdims=True)

        @pl.when(i == 0)
        def _():
            loss_ref[...] = jnp.zeros(loss_ref.shape, F32)

        loss_ref[...] += jnp.broadcast_to(part, loss_ref.shape)

    row_spec = pl.BlockSpec((tm, D_MODEL), lambda i: (i, 0))
    return pl.pallas_call(
        body, name=name, grid=(T // tm,),
        in_specs=[row_spec, row_spec],
        out_specs=[row_spec, pl.BlockSpec((8, LANE), lambda i: (0, 0))],
        out_shape=[jax.ShapeDtypeStruct((T, D_MODEL), F32), jax.ShapeDtypeStruct((8, LANE), F32)],
        compiler_params=_params(("arbitrary",)),
    )(h, tgt)


def _ln_bwd(dy, r, g, *, scale, name, tm):
    T = dy.shape[0]

    def body(dy_ref, r_ref, g_ref, dr_ref, drb_ref, dg_ref, db_ref):
        i = pl.program_id(0)
        rr = r_ref[...]
        dyv = dy_ref[...]
        mu = jnp.mean(rr, axis=-1, keepdims=True)
        xc = rr - mu
        rstd = lax.rsqrt(jnp.mean(xc * xc, axis=-1, keepdims=True) + LN_EPS)
        xh = xc * rstd
        dxh = dyv * g_ref[...]
        m1 = jnp.mean(dxh, axis=-1, keepdims=True)
        m2 = jnp.mean(dxh * xh, axis=-1, keepdims=True)
        dr = rstd * (dxh - m1 - xh * m2)
        dr_ref[...] = dr
        drb_ref[...] = (scale * dr).astype(BF16)

        @pl.when(i == 0)
        def _():
            dg_ref[...] = jnp.zeros(dg_ref.shape, F32)
            db_ref[...] = jnp.zeros(db_ref.shape, F32)

        dg_ref[0:1, :] += jnp.sum(dyv * xh, axis=0, keepdims=True)
        db_ref[0:1, :] += jnp.sum(dyv, axis=0, keepdims=True)

    row = pl.BlockSpec((tm, D_MODEL), lambda i: (i, 0))
    acc = pl.BlockSpec((8, D_MODEL), lambda i: (0, 0))
    return pl.pallas_call(
        body, name=name, grid=(T // tm,),
        in_specs=[row, row, pl.BlockSpec((1, D_MODEL), lambda i: (0, 0))],
        out_specs=[row, row, acc, acc],
        out_shape=[jax.ShapeDtypeStruct((T, D_MODEL), F32), jax.ShapeDtypeStruct((T, D_MODEL), BF16),
                   jax.ShapeDtypeStruct((8, D_MODEL), F32), jax.ShapeDtypeStruct((8, D_MODEL), F32)],
        compiler_params=_params(("arbitrary",)),
    )(dy, r, g)


def _ffn_bwd_mid(dfb, w_down, gu, *, name, tm):
    T = dfb.shape[0]

    def body(df_ref, w_ref, gu_ref, o_ref):
        da = lax.dot_general(df_ref[...], w_ref[...], (((1,), (1,)), ((), ())), preferred_element_type=F32)
        g = gu_ref[0].astype(F32)
        u = gu_ref[1].astype(F32)
        sg = jax.nn.sigmoid(g)
        o_ref[0] = (da * u * (sg * (1.0 + g * (1.0 - sg)))).astype(BF16)
        o_ref[1] = (da * (g * sg)).astype(BF16)

    return pl.pallas_call(
        body, name=name, grid=(FF_HALF_BLOCKS, T // tm),
        in_specs=[pl.BlockSpec((tm, D_MODEL), lambda j, i: (i, 0)),
                  pl.BlockSpec((FF_BLK, D_MODEL), lambda j, i: (j, 0)),
                  pl.BlockSpec((2, None, tm, FF_BLK), lambda j, i: (0, j, i, 0))],
        out_specs=pl.BlockSpec((2, None, tm, FF_BLK), lambda j, i: (0, j, i, 0)),
        out_shape=jax.ShapeDtypeStruct((2, FF_HALF_BLOCKS, T, FF_BLK), BF16),
        compiler_params=_params(("parallel", "parallel")),
    )(dfb, w_down, gu)


def _wo_bwd(dmb, wo, z, bg, pa, pb, *, name, tm):
    T = dmb.shape[0]

    def body(dm_ref, w_ref, gc_ref, gm_ref, bg_ref, pa_ref, pb_ref, dpa_ref, dpb_ref, dg_ref, dbg_ref):
        i = pl.program_id(0)
        dm = lax.dot_general(dm_ref[...], w_ref[...], _NT, preferred_element_type=F32)
        bgv = bg_ref[...]
        sa = jax.nn.sigmoid(gc_ref[...] + bgv[0:1])
        sb = jax.nn.sigmoid(gm_ref[...] + bgv[1:2])
        dpa_ref[...] = (dm * sa).astype(BF16)
        dpb_ref[...] = (dm * sb).astype(BF16)
        dga = dm * pa_ref[...].astype(F32) * (sa * (1.0 - sa))
        dgb = dm * pb_ref[...].astype(F32) * (sb * (1.0 - sb))
        dg_ref[:, :D_MODEL] = dga.astype(BF16)
        dg_ref[:, D_MODEL:] = dgb.astype(BF16)

        @pl.when(i == 0)
        def _():
            dbg_ref[...] = jnp.zeros(dbg_ref.shape, F32)

        dbg_ref[0:1, :] += jnp.sum(dga, axis=0, keepdims=True)
        dbg_ref[1:2, :] += jnp.sum(dgb, axis=0, keepdims=True)

    row = pl.BlockSpec((tm, D_MODEL), lambda i: (i, 0))
    return pl.pallas_call(
        body, name=name, grid=(T // tm,),
        in_specs=[row, pl.BlockSpec((D_MODEL, D_MODEL), lambda i: (0, 0)),
                  pl.BlockSpec((tm, D_MODEL), lambda i: (i, Z_GATE // D_MODEL)),
                  pl.BlockSpec((tm, D_MODEL), lambda i: (i, Z_GATE // D_MODEL + 1)),
                  pl.BlockSpec((8, D_MODEL), lambda i: (0, 0)), row, row],
        out_specs=[row, row, pl.BlockSpec((tm, 2 * D_MODEL), lambda i: (i, 0)),
                   pl.BlockSpec((8, D_MODEL), lambda i: (0, 0))],
        out_shape=[jax.ShapeDtypeStruct((T, D_MODEL), BF16), jax.ShapeDtypeStruct((T, D_MODEL), BF16),
                   jax.ShapeDtypeStruct((T, 2 * D_MODEL), BF16), jax.ShapeDtypeStruct((8, D_MODEL), F32)],
        compiler_params=_params(("arbitrary",)),
    )(dmb, wo, z, z, bg, pa, pb)


def _conv_bwd(dy, z, conv_w8, *, name, tm):
    T = dy.shape[0]
    n = T // tm
    hb = tm // 8

    def body(dy_ref, b_ref, c_ref, h_ref, cp_ref, hp_ref, dyn_ref, bn_ref, w_ref, dz_ref, dw_ref):
        i = pl.program_id(0)
        u = c_ref[...] * h_ref[...]
        up = jnp.where(i > 0, cp_ref[...] * hp_ref[...], 0.0)
        ue = jnp.concatenate([up, u], axis=0)
        s1 = pltpu.roll(ue, 1, 0)[8:]
        s2 = pltpu.roll(ue, 2, 0)[8:]
        w = w_ref[...]
        conv = w[0:1] * s2 + w[1:2] * s1 + w[2:3] * u
        dyv = dy_ref[...]
        e = dyv * b_ref[...]
        en = jnp.where(i < n - 1, dyn_ref[...] * bn_ref[...], 0.0)
        ee = jnp.concatenate([e, en], axis=0)
        e1 = pltpu.roll(ee, tm + 8 - 1, 0)[:tm]
        e2 = pltpu.roll(ee, tm + 8 - 2, 0)[:tm]
        du = w[2:3] * e + w[1:2] * e1 + w[0:1] * e2
        dz_ref[:, 0:D_CONV] = (dyv * conv).astype(BF16)
        dz_ref[:, D_CONV:2 * D_CONV] = (du * h_ref[...]).astype(BF16)
        dz_ref[:, 2 * D_CONV:] = (du * c_ref[...]).astype(BF16)

        @pl.when(i == 0)
        def _():
            dw_ref[...] = jnp.zeros(dw_ref.shape, F32)

        dw_ref[0:1, :] += jnp.sum(e * s2, axis=0, keepdims=True)
        dw_ref[1:2, :] += jnp.sum(e * s1, axis=0, keepdims=True)
        dw_ref[2:3, :] += jnp.sum(e * u, axis=0, keepdims=True)

    def col(c):
        return pl.BlockSpec((tm, D_CONV), lambda i: (i, c))

    def prev(c):
        return pl.BlockSpec((8, D_CONV), lambda i: (jnp.maximum(i * hb - 1, 0), c))

    def nxt(c):
        return pl.BlockSpec((8, D_CONV), lambda i: (jnp.minimum((i + 1) * hb, T // 8 - 1), c))

    return pl.pallas_call(
        body, name=name, grid=(n,),
        in_specs=[col(0), col(0), col(1), col(2), prev(1), prev(2), nxt(0), nxt(0),
                  pl.BlockSpec((8, D_CONV), lambda i: (0, 0))],
        out_specs=[pl.BlockSpec((tm, 3 * D_CONV), lambda i: (i, 0)), pl.BlockSpec((8, D_CONV), lambda i: (0, 0))],
        out_shape=[jax.ShapeDtypeStruct((T, 3 * D_CONV), BF16), jax.ShapeDtypeStruct((8, D_CONV), F32)],
        compiler_params=_params(("arbitrary",)),
    )(dy, z, z, z, z, z, dy, z, conv_w8)


def _attn_bwd(q, k, v, o, do, lse, *, name, blk):
    T = q.shape[0]
    n = T // blk
    hp = HEADS_PER_STEP
    qi = np.array([i for j in range(n) for i in range(j, n)], np.int32)
    kj = np.array([j for j in range(n) for i in range(j, n)], np.int32)

    rc = _tile(blk, (SOFTMAX_ROWS,))

    def body(qi_ref, kj_ref, q_ref, k_ref, v_ref, o_ref, do_ref, lse_ref, dq_ref, dk_ref, dv_ref,
             dk_sc, dv_sc, s_sc, dp_sc, p_sc, ds_sc, delta_sc):
        s_id = pl.program_id(1)
        i = qi_ref[s_id]
        j = kj_ref[s_id]

        @pl.when(s_id == 0)
        def _():
            dq_ref[...] = jnp.zeros(dq_ref.shape, F32)

        @pl.when(i == j)
        def _():
            dk_sc[...] = jnp.zeros(dk_sc.shape, F32)
            dv_sc[...] = jnp.zeros(dv_sc.shape, F32)

        q_rows = pl.ds(pl.multiple_of(i * blk, blk), blk)

        def head_step(hh, diagonal):
            hs = slice(hh * HEAD_PAD, (hh + 1) * HEAD_PAD)
            vs = slice(hh * V_HEAD, (hh + 1) * V_HEAD)
            qh = q_ref[:, hs]
            kh = k_ref[:, hs]
            doh = do_ref[:, vs]
            s_sc[hh] = lax.dot_general(qh, kh, _NT, preferred_element_type=F32)
            vh = v_ref[:, hh * HEAD_PAD:hh * HEAD_PAD + V_HEAD]
            dp_sc[hh] = lax.dot_general(doh, vh, _NT, preferred_element_type=F32)
            delta = jnp.sum(doh.astype(F32) * o_ref[:, vs].astype(F32), axis=-1, keepdims=True)
            delta_sc[hh] = jnp.broadcast_to(delta, (blk, LANE))
            for r in range(blk // rc):
                rows = slice(r * rc, (r + 1) * rc)
                lse = lse_ref[hh, rows, :]
                dl = delta_sc[hh, rows, :]
                for t in range(blk // LANE):
                    cols = slice(t * LANE, (t + 1) * LANE)
                    s = s_sc[hh, rows, cols]
                    if diagonal:
                        s = _diag_mask(s, r * rc, t * LANE)
                    p = jnp.exp2(s - lse)
                    p_sc[hh, rows, cols] = p.astype(BF16)
                    ds_sc[hh, rows, cols] = (p * (dp_sc[hh, rows, cols] - dl)).astype(BF16)
            dv_sc[hh] += lax.dot_general(p_sc[hh], doh, _TN, preferred_element_type=F32)
            dk_sc[hh] += lax.dot_general(ds_sc[hh], qh, _TN, preferred_element_type=F32)
            dq_ref[q_rows, hs] += jnp.dot(ds_sc[hh], kh, preferred_element_type=F32)

        @pl.when(j < i)
        def _():
            for hh in range(hp):
                head_step(hh, False)

        @pl.when(j == i)
        def _():
            for hh in range(hp):
                head_step(hh, True)

        @pl.when(i == n - 1)
        def _():
            for hh in range(hp):
                dk_ref[:, hh * HEAD_PAD:(hh + 1) * HEAD_PAD] = dk_sc[hh] * (1.0 / LOG2_E)
                dv_ref[:, hh * V_HEAD:(hh + 1) * V_HEAD] = dv_sc[hh]

    wq = hp * HEAD_PAD
    wv = hp * V_HEAD
    grid_spec = pltpu.PrefetchScalarGridSpec(
        num_scalar_prefetch=2, grid=(MLA_HEADS // hp, len(qi)),
        in_specs=[pl.BlockSpec((blk, wq), lambda g, s, qi, kj: (qi[s], g)),
                  pl.BlockSpec((blk, wq), lambda g, s, qi, kj: (kj[s], g)),
                  pl.BlockSpec((blk, wq), lambda g, s, qi, kj: (kj[s], g)),
                  pl.BlockSpec((blk, wv), lambda g, s, qi, kj: (qi[s], g)),
                  pl.BlockSpec((blk, wv), lambda g, s, qi, kj: (qi[s], g)),
                  pl.BlockSpec((hp, blk, LANE), lambda g, s, qi, kj: (g, qi[s], 0))],
        out_specs=[pl.BlockSpec((T, wq), lambda g, s, qi, kj: (0, g)),
                   pl.BlockSpec((blk, wq), lambda g, s, qi, kj: (kj[s], g)),
                   pl.BlockSpec((blk, wv), lambda g, s, qi, kj: (kj[s], g))],
        scratch_shapes=[pltpu.VMEM((hp, blk, HEAD_PAD), F32), pltpu.VMEM((hp, blk, V_HEAD), F32),
                        pltpu.VMEM((hp, blk, blk), F32), pltpu.VMEM((hp, blk, blk), F32),
                        pltpu.VMEM((hp, blk, blk), BF16), pltpu.VMEM((hp, blk, blk), BF16),
                        pltpu.VMEM((hp, blk, LANE), F32)])
    return pl.pallas_call(
        body, name=name, grid_spec=grid_spec,
        out_shape=[jax.ShapeDtypeStruct((T, D_QK), F32), jax.ShapeDtypeStruct((T, D_QK), F32),
                   jax.ShapeDtypeStruct((T, D_MLA), F32)],
        compiler_params=_params(("parallel", "arbitrary")),
    )(jnp.asarray(qi), jnp.asarray(kj), q, k, v, o, do, lse)


def _qkv_bwd(dq, dk, dv, z, gq, gkv, wq, wk, wv, tabs, *, name, tm):
    T = dq.shape[0]

    def body(dq_ref, dk_ref, dv_ref, z_ref, gq_ref, gkv_ref, wq_ref, wk_ref, wv_ref, c_ref, s1_ref, s2_ref,
             dz_ref, dqb_ref, dkb_ref, dvb_ref, dgq_ref, dgkv_ref):
        i = pl.program_id(0)
        c = jnp.tile(c_ref[...], (1, MLA_HEADS))
        s1 = jnp.tile(s1_ref[...], (1, MLA_HEADS))
        s2 = jnp.tile(s2_ref[...], (1, MLA_HEADS))
        dqp = _rope_t(dq_ref[...] * ATTN_SCALE, c, s1, s2).astype(BF16)
        dkp = _rope_t(dk_ref[...], c, s1, s2).astype(BF16)
        dvb = dv_ref[...].astype(BF16)
        dqb_ref[...] = dqp
        dkb_ref[...] = dkp
        dvb_ref[...] = dvb
        dqn = lax.dot_general(dqp, wq_ref[...], _NT, preferred_element_type=F32)
        dkin = lax.dot_general(dkp, wk_ref[...], _NT, preferred_element_type=F32)
        dkvn = dkin[:, :KV_LORA] + lax.dot_general(dvb, wv_ref[...], _NT, preferred_element_type=F32)
        zz = z_ref[...]

        def rms_bwd(x, g, dy):
            rstd = lax.rsqrt(jnp.mean(x * x, axis=-1, keepdims=True) + RMS_EPS)
            xh = x * rstd
            dxh = dy * g
            dx = rstd * (dxh - xh * jnp.mean(dxh * xh, axis=-1, keepdims=True))
            return dx, jnp.sum(dy * xh, axis=0, keepdims=True)

        dcq, dgq = rms_bwd(zz[:, :Q_LORA], gq_ref[...], dqn)
        dckv, dgkv = rms_bwd(zz[:, Q_LORA:Q_LORA + KV_LORA], gkv_ref[...], dkvn)
        dz_ref[:, :Q_LORA] = dcq.astype(BF16)
        dz_ref[:, Q_LORA:Q_LORA + KV_LORA] = dckv.astype(BF16)
        dz_ref[:, Q_LORA + KV_LORA:] = dkin[:, KV_LORA:].astype(BF16)

        @pl.when(i == 0)
        def _():
            dgq_ref[...] = jnp.zeros(dgq_ref.shape, F32)
            dgkv_ref[...] = jnp.zeros(dgkv_ref.shape, F32)

        dgq_ref[0:1, :] += dgq
        dgkv_ref[0:1, :] += dgkv

    def full(shape):
        return pl.BlockSpec(shape, lambda i: (0, 0))

    def rows(w, c=0):
        return pl.BlockSpec((tm, w), lambda i: (i, c))

    return pl.pallas_call(
        body, name=name, grid=(T // tm,),
        in_specs=[rows(D_QK), rows(D_QK), rows(D_MLA), rows(512, Z_MID // 512),
                  full((1, Q_LORA)), full((1, KV_LORA)),
                  full((Q_LORA, D_QK)), full((Q_LORA, D_QK)), full((KV_LORA, D_MLA)),
                  rows(LANE), rows(LANE), rows(LANE)],
        out_specs=[rows(512), rows(D_QK), rows(D_QK), rows(D_MLA), full((8, Q_LORA)), full((8, KV_LORA))],
        out_shape=[jax.ShapeDtypeStruct((T, 512), BF16), jax.ShapeDtypeStruct((T, D_QK), BF16),
                   jax.ShapeDtypeStruct((T, D_QK), BF16), jax.ShapeDtypeStruct((T, D_MLA), BF16),
                   jax.ShapeDtypeStruct((8, Q_LORA), F32), jax.ShapeDtypeStruct((8, KV_LORA), F32)],
        compiler_params=_params(("arbitrary",)),
    )(dq, dk, dv, z, gq, gkv, wq, wk, wv, *tabs)


FLAT_W = 1024


ELEMENTWISE_TILE_BYTES = 768 * 1024


def _row_tile(R, C):
    width = -(-C // LANE) * LANE * 4
    best = None
    for t in range(16, R + 1, 16):
        if R % t == 0 and t * width <= ELEMENTWISE_TILE_BYTES:
            best = t
    if best is None:
        best = R
    return best


def _adamw(w, m, v, parts, part_index, *, name):
    L, R, C = w.shape
    tr = _row_tile(R, C)
    bc1 = 1.0 - ADAM_B1 ** ADAM_STEP
    bc2 = 1.0 - ADAM_B2 ** ADAM_STEP
    n_parts = len(parts)

    def body(idx_ref, w_ref, m_ref, v_ref, *refs):
        g_refs = refs[:n_parts]
        g_out, d_out, m_out, v_out = refs[n_parts:]
        g = g_refs[0][...].astype(F32)
        for r in g_refs[1:]:
            g = g + r[...].astype(F32)
        wv = w_ref[...]
        mn = ADAM_B1 * m_ref[...] + (1.0 - ADAM_B1) * g
        vn = ADAM_B2 * v_ref[...] + (1.0 - ADAM_B2) * (g * g)
        m_hat = mn / bc1
        v_hat = vn / bc2
        g_out[...] = g
        d_out[...] = -ADAM_LR * (m_hat / (jnp.sqrt(v_hat) + ADAM_EPS) + ADAM_WD * wv)
        m_out[...] = mn
        v_out[...] = vn

    row = pl.BlockSpec((None, tr, C), lambda l, i, idx: (l, i, 0))
    in_specs = [row, row, row]
    args = [w, m, v]
    for arr, slot in parts:
        if slot is None:
            in_specs.append(pl.BlockSpec((None, None, tr, C), lambda l, i, idx: (idx[0], l, i, 0)))
        else:
            in_specs.append(pl.BlockSpec((None, None, tr, C), lambda l, i, idx, slot=slot: (slot, l, i, 0)))
        args.append(arr)
    grid_spec = pltpu.PrefetchScalarGridSpec(
        num_scalar_prefetch=1, grid=(L, R // tr), in_specs=in_specs, out_specs=[row] * 4)
    return pl.pallas_call(
        body, name=name, grid_spec=grid_spec,
        out_shape=[jax.ShapeDtypeStruct((L, R, C), F32)] * 4,
        compiler_params=_params(("parallel", "parallel")),
    )(part_index, *args)


def _pair_sum(g, recv, c_index, *, name):
    _, L, R, C = g.shape
    tr = _row_tile(R, C)

    def body(c_ref, g_ref, r_ref, o_ref, ob_ref):
        total = g_ref[...] + r_ref[...]
        o_ref[...] = total
        ob_ref[...] = total.astype(BF16)

    def spec(index):
        return pl.BlockSpec((None, None, tr, C), index)

    grid_spec = pltpu.PrefetchScalarGridSpec(
        num_scalar_prefetch=1, grid=(4, L, R // tr),
        in_specs=[spec(lambda q, l, i, c: (2 * q + c[0], l, i, 0)), spec(lambda q, l, i, c: (q, l, i, 0))],
        out_specs=[spec(lambda q, l, i, c: (q, l, i, 0)), spec(lambda q, l, i, c: (q, l, i, 0))])
    return pl.pallas_call(
        body, name=name, grid_spec=grid_spec,
        out_shape=[jax.ShapeDtypeStruct((4, L, R, C), F32), jax.ShapeDtypeStruct((4, L, R, C), BF16)],
        compiler_params=_params(("parallel", "parallel", "parallel")),
    )(c_index, g, recv)


def _sum8(parts, *, name):
    _, R, _ = parts.shape

    def body(p_ref, o_ref):
        acc = p_ref[0]
        for d in range(1, N_DEV):
            acc = acc + p_ref[d]
        o_ref[...] = acc

    return pl.pallas_call(
        body, name=name, grid=(1,),
        in_specs=[pl.BlockSpec((N_DEV, R, FLAT_W), lambda i: (0, 0, 0))],
        out_specs=pl.BlockSpec((R, FLAT_W), lambda i: (0, 0)),
        out_shape=jax.ShapeDtypeStruct((R, FLAT_W), F32),
        compiler_params=_params(("arbitrary",)),
    )(parts)


_MESH_ID = pl.DeviceIdType.MESH
_ANY = pl.BlockSpec(memory_space=pl.ANY)


def _all_gather(shards, *, name):
    n = len(shards)

    def body(*refs):
        x_refs, out_refs = refs[:n], refs[n:2 * n]
        send_sems, recv_sems, local_sems = refs[2 * n:]
        x, y, c = lax.axis_index("x"), lax.axis_index("y"), lax.axis_index("c")
        me, sibling = (x, y, c), (x, y, 1 - c)
        chips = [(1 - x, y), (x, 1 - y), (1 - x, 1 - y)]

        def blk(a, px, py, pc):
            return out_refs[a].at[4 * px + 2 * py + pc]

        def copy(a, k, block, to, src=None):
            return pltpu.make_async_remote_copy(
                src_ref=blk(a, *block) if src is None else src, dst_ref=blk(a, *block),
                send_sem=send_sems.at[7 * a + k], recv_sem=recv_sems.at[7 * a + k],
                device_id=to, device_id_type=_MESH_ID)

        mine = [pltpu.make_async_copy(x_refs[a], blk(a, *me), local_sems.at[a]) for a in range(n)]
        for cp in mine:
            cp.start()
        first = []
        for a in range(n):
            first.append(copy(a, 0, me, sibling, src=x_refs[a]))
            first += [copy(a, 1 + j, me, (*chip, c), src=x_refs[a]) for j, chip in enumerate(chips)]
        for cp in first:
            cp.start()
        passed = []
        for j, chip in enumerate(chips):
            for a in range(n):
                copy(a, 1 + j, (*chip, c), me).wait_recv()
                fwd = copy(a, 4 + j, (*chip, c), sibling)
                fwd.start()
                passed.append(fwd)
        for a in range(n):
            copy(a, 0, sibling, me).wait_recv()
        for j, chip in enumerate(chips):
            for a in range(n):
                copy(a, 4 + j, (*chip, 1 - c), me).wait_recv()
        for cp in first + passed:
            cp.wait_send()
        for cp in mine:
            cp.wait()

    return pl.pallas_call(
        body, name=name,
        out_shape=[jax.ShapeDtypeStruct((N_DEV,) + s.shape, s.dtype) for s in shards],
        in_specs=[_ANY] * n, out_specs=[_ANY] * n,
        scratch_shapes=[pltpu.SemaphoreType.DMA((7 * n,)), pltpu.SemaphoreType.DMA((7 * n,)),
                        pltpu.SemaphoreType.DMA((n,))],
    )(*shards)


def _exchange_sibling(gs, *, name):
    n = len(gs)

    def body(*refs):
        g_refs, out_refs = refs[:n], refs[n:2 * n]
        send_sems, recv_sems = refs[2 * n:]
        x, y, c = lax.axis_index("x"), lax.axis_index("y"), lax.axis_index("c")
        copies = []
        for a in range(n):
            for q in range(4):
                copies.append(pltpu.make_async_remote_copy(
                    src_ref=g_refs[a].at[2 * q + (1 - c)], dst_ref=out_refs[a].at[q],
                    send_sem=send_sems.at[4 * a + q], recv_sem=recv_sems.at[4 * a + q],
                    device_id=(x, y, 1 - c), device_id_type=_MESH_ID))
        for cp in copies:
            cp.start()
        for cp in copies:
            cp.wait_recv()
        for cp in copies:
            cp.wait_send()

    return pl.pallas_call(
        body, name=name,
        out_shape=[jax.ShapeDtypeStruct((4,) + g.shape[1:], g.dtype) for g in gs],
        in_specs=[_ANY] * n, out_specs=[_ANY] * n,
        scratch_shapes=[pltpu.SemaphoreType.DMA((4 * n,)), pltpu.SemaphoreType.DMA((4 * n,))],
    )(*gs)


def _exchange_chips(hs, *, name):
    n = len(hs)

    def body(*refs):
        h_refs, out_refs = refs[:n], refs[n:2 * n]
        send_sems, recv_sems = refs[2 * n:]
        x, y, c = lax.axis_index("x"), lax.axis_index("y"), lax.axis_index("c")
        chips = [(1 - x, y), (x, 1 - y), (1 - x, 1 - y)]
        copies = []
        for a in range(n):
            for k, (px, py) in enumerate(chips):
                copies.append(pltpu.make_async_remote_copy(
                    src_ref=h_refs[a].at[2 * px + py], dst_ref=out_refs[a].at[k],
                    send_sem=send_sems.at[3 * a + k], recv_sem=recv_sems.at[3 * a + k],
                    device_id=(px, py, c), device_id_type=_MESH_ID))
        for cp in copies:
            cp.start()
        for cp in copies:
            cp.wait_recv()
        for cp in copies:
            cp.wait_send()

    return pl.pallas_call(
        body, name=name,
        out_shape=[jax.ShapeDtypeStruct((3,) + h.shape[1:], h.dtype) for h in hs],
        in_specs=[_ANY] * n, out_specs=[_ANY] * n,
        scratch_shapes=[pltpu.SemaphoreType.DMA((3 * n,)), pltpu.SemaphoreType.DMA((3 * n,))],
    )(*hs)


_BIG = (("ffn1_w_up", 2), ("ffn1_w_down", 1), ("mix_w_in", 2), ("w_uq", 2), ("w_ukv", 2),
        ("w_br_conv", 2), ("w_br_mla", 2), ("w_o", 1), ("ffn2_w_up", 2), ("ffn2_w_down", 1))
_SMALL_SHARDED = (("meta_tokens", 1), ("mix_b_gate", 2), ("conv_w", 2), ("ln_g", 2), ("ln_b", 2))
_SMALL_REPL = ("q_norm_g", "kv_norm_g")


BIG_ROW_ALIGN = 64


def _pack(arrs, dtype, row_align=8):
    flat = jnp.concatenate([a.reshape(-1).astype(dtype) for a in arrs])
    n = flat.shape[0]
    rows = -(-n // (row_align * FLAT_W)) * row_align
    return jnp.pad(flat, (0, rows * FLAT_W - n)).reshape(rows, FLAT_W)


def _unpack(flat, shapes):
    flat = flat.reshape(-1)
    out, off = [], 0
    for s in shapes:
        n = int(np.prod(s))
        out.append(flat[off:off + n].reshape(s))
        off += n
    return out


def _unpack_gathered(gathered, shapes, axes):
    g2 = gathered.reshape(N_DEV, -1)
    out, off = [], 0
    for s, ax in zip(shapes, axes):
        n = int(np.prod(s))
        blocks = g2[:, off:off + n].reshape((N_DEV,) + tuple(s))
        full = jnp.moveaxis(blocks, 0, ax)
        out.append(full.reshape(tuple(s[:ax]) + (N_DEV * s[ax],) + tuple(s[ax + 1:])))
        off += n
    return out


def _to_dest(full, ax):
    s = full.shape
    split = full.reshape(s[:ax] + (N_DEV, s[ax] // N_DEV) + s[ax + 1:])
    return jnp.moveaxis(split, ax, 0)


def _from_blocks(g, ax):
    full = jnp.moveaxis(g, 0, ax)
    s = full.shape
    return full.reshape(s[:ax] + (s[ax] * s[ax + 1],) + s[ax + 2:])


def _rope_tables(T):
    inv_freq = 1.0 / (ROPE_BASE ** (jnp.arange(0, QK_ROPE, 2, dtype=F32) / QK_ROPE))
    ang = jnp.arange(T, dtype=F32)[:, None] * inv_freq[None, :]
    cos, sin = jnp.cos(ang), jnp.sin(ang)
    half = QK_ROPE // 2
    ones = jnp.ones((T, QK_NOPE), F32)
    zeros = lambda w: jnp.zeros((T, w), F32)
    c = jnp.concatenate([ones, cos, cos, zeros(HEAD_PAD - QK_NOPE - QK_ROPE)], axis=1)
    s1 = jnp.concatenate([zeros(QK_NOPE + half), sin, zeros(HEAD_PAD - QK_NOPE - QK_ROPE)], axis=1)
    s2 = jnp.concatenate([zeros(QK_NOPE), -sin, zeros(HEAD_PAD - QK_NOPE - half)], axis=1)
    return c, s1, s2


def _layer_weights(W, l):
    w_in = W["mix_w_in"][l]
    w_in_p = jnp.concatenate([w_in[:, :Z_KR_END], jnp.zeros((D_MODEL, D_IN_PAD - D_IN_REAL), BF16),
                              w_in[:, Z_KR_END:]], axis=1)
    w_uq = W["w_uq"][l].reshape(Q_LORA, MLA_HEADS, QK_NOPE + QK_ROPE)
    wq = jnp.pad(w_uq, ((0, 0), (0, 0), (0, HEAD_PAD - QK_NOPE - QK_ROPE))).reshape(Q_LORA, D_QK)
    w_ukv = W["w_ukv"][l].reshape(KV_LORA, MLA_HEADS, QK_NOPE + V_HEAD)
    wk_top = jnp.pad(w_ukv[:, :, :QK_NOPE], ((0, 0), (0, 0), (0, HEAD_PAD - QK_NOPE))).reshape(KV_LORA, D_QK)
    place = np.zeros((Q_LORA - KV_LORA, MLA_HEADS, HEAD_PAD), np.float32)
    for r in range(QK_ROPE):
        place[r, :, QK_NOPE + r] = 1.0
    wk = jnp.concatenate([wk_top, jnp.asarray(place.reshape(Q_LORA - KV_LORA, D_QK), BF16)], axis=0)
    wv = w_ukv[:, :, QK_NOPE:].reshape(KV_LORA, D_MLA)
    wv_ext = jnp.pad(w_ukv[:, :, QK_NOPE:], ((0, 0), (0, 0), (0, HEAD_PAD - V_HEAD))).reshape(KV_LORA, D_QK)
    return dict(
        ffn1_down=W["ffn1_w_down"][l], ffn2_down=W["ffn2_w_down"][l],
        w_in=w_in_p, wq=wq, wk=wk, wv=wv, wv_ext=wv_ext,
        wbc=W["w_br_conv"][l], wbm=W["w_br_mla"][l], wo=W["w_o"][l])


def _row8(v):
    return jnp.pad(v, ((0, 8 - v.shape[0]), (0, 0)))


def _local_step(x, tgt, W, Wg, S):
    t_real = N_META + x.shape[0]
    T = -(-t_real // ROW_ALIGN) * ROW_ALIGN
    pad = T - t_real
    tm = _tile(T, (384, 256, 128))
    tms = _tile(T, (768, 256, 128))
    blk = _tile(T, (768, 256, 128))
    tabs = _rope_tables(T)

    h0 = jnp.concatenate([S["meta_tokens"], x, jnp.zeros((pad, D_MODEL), F32)], axis=0)
    tgt_p = jnp.concatenate([jnp.zeros((N_META, D_MODEL), F32), tgt, jnp.zeros((pad, D_MODEL), F32)], axis=0)

    def ffn_fwd(h, hb, up8, l, down, g, b, tag):
        gu, a = _ffn_up(hb, up8, l, name=f"ffn_up_{tag}", tm=tms)
        r, y, yb = _mm_res_ln(a, down, h, g, b, scale=0.5, name=f"ffn_down_ln_{tag}", tm=tm)
        return dict(gu=gu, a=a, r=r), y, yb

    saved = []
    h, hb = h0, h0.astype(BF16)
    for l in range(DEPTH):
        lw = _layer_weights(W, l)
        sv = dict(lw=lw, h_in=h, h_in_b=hb)
        sv["f1"], h1, h1b = ffn_fwd(h, hb, Wg["ffn1_w_up"], l, lw["ffn1_down"],
                                    S["ln_g"][l, 0:1], S["ln_b"][l, 0:1], f"{l}a")
        z = _mm(h1b, lw["w_in"], out_dtype=F32, name=f"mix_in_{l}", tm=tms, tn=1024, tk=D_MODEL)
        conv_w8 = _row8(S["conv_w"][l])
        bg8 = _row8(S["mix_b_gate"][l])
        yc = _conv_fwd(z, conv_w8, name=f"conv_fwd_{l}", tm=tms)
        gq, gkv = S["q_norm_g"][l:l + 1], S["kv_norm_g"][l:l + 1]
        q, k, v, qn, kin = _qkv_proj(z, gq, gkv, lw["wq"], lw["wk"], lw["wv_ext"], tabs, name=f"qkv_proj_{l}", tm=tm)
        o, lse = _attn_fwd(q, k, v, name=f"attn_fwd_{l}", blk=blk)
        mg, pa, pb = _merge(yc, o, lw["wbc"], lw["wbm"], z, bg8, name=f"merge_{l}", tm=tm)
        r2, h2, h2b = _mm_res_ln(mg, lw["wo"], h1, S["ln_g"][l, 1:2], S["ln_b"][l, 1:2], scale=1.0,
                                 name=f"wo_ln_{l}", tm=tm)
        sv["f2"], h3, h3b = ffn_fwd(h2, h2b, Wg["ffn2_w_up"], l, lw["ffn2_down"],
                                    S["ln_g"][l, 2:3], S["ln_b"][l, 2:3], f"{l}b")
        sv.update(h1b=h1b, z=z, conv_w8=conv_w8, bg8=bg8, yc=yc, gq=gq, gkv=gkv, q=q, k=k, v=v, qn=qn, kin=kin,
                  o=o, lse=lse, mg=mg, pa=pa, pb=pb, r2=r2, h2b=h2b)
        saved.append(sv)
        h, hb = h3, h3b

    dh, loss8 = _loss_head(h, tgt_p, t_real=t_real, name="loss_head", tm=tm)

    tk = _tile(T, (2816, 768, 256, 128))
    grads = {n: [None] * DEPTH for n, _ in _BIG}
    for n in ("mix_b_gate", "conv_w", "q_norm_g", "kv_norm_g"):
        grads[n] = [None] * DEPTH
    grads["ln_g"] = [[None] * 3 for _ in range(DEPTH)]
    grads["ln_b"] = [[None] * 3 for _ in range(DEPTH)]

    def ffn_bwd(dy, f, h_in_b, up8, l, down, g, tag):
        dr, dfb, dg, db = _ln_bwd(dy, f["r"], g, scale=0.5, name=f"ln_bwd_{tag}", tm=tm)
        dgu = _ffn_bwd_mid(dfb, down, f["gu"], name=f"ffn_bwd_mid_{tag}", tm=tms)
        d_down = _mm_tn_call(
            f["a"], dfb,
            pl.BlockSpec((None, tk, FF_BLK), lambda i, j, k: (i, k, 0)),
            pl.BlockSpec((tk, D_MODEL), lambda i, j, k: (k, 0)),
            out_shape=jax.ShapeDtypeStruct((D_FF, D_MODEL), F32),
            out_spec=pl.BlockSpec((FF_BLK, D_MODEL), lambda i, j, k: (i, 0)),
            grid=(FF_HALF_BLOCKS, 1, T // tk), name=f"dw_down_{tag}")
        d_up = _mm_tn_call(
            h_in_b, dgu,
            pl.BlockSpec((tk, D_MODEL), lambda i, j, k: (k, 0)),
            pl.BlockSpec((None, None, tk, FF_BLK),
                         lambda i, j, k: (j // FF_HALF_BLOCKS, j % FF_HALF_BLOCKS, k, 0)),
            out_shape=jax.ShapeDtypeStruct((N_DEV, D_MODEL, FF_BLK), F32),
            out_spec=pl.BlockSpec((None, D_MODEL, FF_BLK), lambda i, j, k: (j, 0, 0)),
            grid=(1, N_DEV, T // tk), name=f"dw_up_{tag}")
        row = pl.BlockSpec((tms, D_MODEL), lambda i, j, k: (i, 0))
        dh_in = _mm_call(
            dgu, up8,
            pl.BlockSpec((None, 2, tms, FF_BLK), lambda i, j, k: (k // 2, k % 2, i, 0)),
            pl.BlockSpec((2, None, D_MODEL, FF_BLK), lambda i, j, k: (k, l, 0, 0)),
            out_shape=jax.ShapeDtypeStruct((T, D_MODEL), F32), out_spec=row, acc_shape=(tms, D_MODEL),
            grid=(T // tms, 1, N_DEV // 2), name=f"ffn_dx_{tag}", trans_b=True, res=dr, res_spec=row,
            res_scale=ALPHA, pieces=2)
        return dh_in, d_up, d_down, dg[0], db[0]

    for l in reversed(range(DEPTH)):
        sv = saved[l]
        lw = sv["lw"]
        dh, grads["ffn2_w_up"][l], grads["ffn2_w_down"][l], grads["ln_g"][l][2], grads["ln_b"][l][2] = ffn_bwd(
            dh, sv["f2"], sv["h2b"], Wg["ffn2_w_up"], l, lw["ffn2_down"], S["ln_g"][l, 2:3], f"{l}b")
        dr2, dmb, dg, db = _ln_bwd(dh, sv["r2"], S["ln_g"][l, 1:2], scale=1.0, name=f"ln_bwd_{l}m", tm=tm)
        grads["ln_g"][l][1], grads["ln_b"][l][1] = dg[0], db[0]
        grads["w_o"][l] = _mm_tn(sv["mg"], dmb, name=f"dw_o_{l}", tm=D_MODEL, tn=D_MODEL, tk=tk)
        dpa, dpb, dgate, dbg = _wo_bwd(dmb, lw["wo"], sv["z"], sv["bg8"], sv["pa"], sv["pb"], name=f"wo_bwd_{l}", tm=tm)
        grads["mix_b_gate"][l] = dbg[0:2]
        grads["w_br_conv"][l] = _mm_tn(sv["yc"], dpa, name=f"dw_br_conv_{l}", tm=D_CONV, tn=D_MODEL, tk=tk)
        grads["w_br_mla"][l] = _mm_tn(sv["o"], dpb, name=f"dw_br_mla_{l}", tm=D_MLA, tn=D_MODEL, tk=tk)
        dyc = _mm(dpa, lw["wbc"], trans_b=True, out_dtype=F32, name=f"d_yconv_{l}", tm=tms, tn=D_CONV, tk=D_MODEL)
        dym = _mm(dpb, lw["wbm"], trans_b=True, out_dtype=BF16, name=f"d_ymla_{l}", tm=tms, tn=D_MLA, tk=D_MODEL)
        dz_conv, dcw = _conv_bwd(dyc, sv["z"], sv["conv_w8"], name=f"conv_bwd_{l}", tm=tms)
        grads["conv_w"][l] = dcw[0:CONV_WIDTH]
        dq, dk, dv = _attn_bwd(sv["q"], sv["k"], sv["v"], sv["o"], dym, sv["lse"], name=f"attn_bwd_{l}", blk=blk)
        dz_mid, dqb, dkb, dvb, dgq, dgkv = _qkv_bwd(dq, dk, dv, sv["z"], sv["gq"], sv["gkv"], lw["wq"], lw["wk"],
                                                    lw["wv"], tabs, name=f"qkv_bwd_{l}", tm=tm)
        grads["q_norm_g"][l], grads["kv_norm_g"][l] = dgq[0], dgkv[0]
        d_wq = _mm_tn(sv["qn"], dqb, name=f"dw_uq_{l}", tm=Q_LORA, tn=D_QK, tk=tk)
        d_wk = _mm_tn(sv["kin"], dkb, name=f"dw_uk_{l}", tm=Q_LORA, tn=D_QK, tk=tk)
        d_wv = _mm_tn(sv["kin"], dvb, name=f"dw_uv_{l}", tm=Q_LORA, tn=D_MLA, tk=tk)
        grads["w_uq"][l] = d_wq.reshape(Q_LORA, MLA_HEADS, HEAD_PAD)[:, :, :QK_NOPE + QK_ROPE].reshape(Q_LORA, -1)
        d_kn = d_wk[:KV_LORA].reshape(KV_LORA, MLA_HEADS, HEAD_PAD)[:, :, :QK_NOPE]
        d_vv = d_wv[:KV_LORA].reshape(KV_LORA, MLA_HEADS, V_HEAD)
        grads["w_ukv"][l] = jnp.concatenate([d_kn, d_vv], axis=-1).reshape(KV_LORA, -1)
        dz = jnp.concatenate([dz_conv, dz_mid, dgate], axis=1)
        d_win = _mm_tn(sv["h1b"], dz, name=f"dw_in_{l}", tm=D_MODEL, tn=1024, tk=tk)
        grads["mix_w_in"][l] = jnp.concatenate([d_win[:, :Z_KR_END], d_win[:, Z_KR_END + D_IN_PAD - D_IN_REAL:]], axis=1)
        dh = _mm(dz, lw["w_in"], trans_b=True, out_dtype=F32, name=f"mix_dx_{l}", res=dr2, res_scale=ALPHA,
                 tm=tms, tn=D_MODEL, tk=2048)
        dh, grads["ffn1_w_up"][l], grads["ffn1_w_down"][l], grads["ln_g"][l][0], grads["ln_b"][l][0] = ffn_bwd(
            dh, sv["f1"], sv["h_in_b"], Wg["ffn1_w_up"], l, lw["ffn1_down"], S["ln_g"][l, 0:1], f"{l}a")

    big = {}
    for n, ax in _BIG:
        if n in Wg:
            big[n] = jnp.stack(grads[n], axis=1)
        else:
            big[n] = _to_dest(jnp.stack(grads[n]), ax)
    small = {n: jnp.stack(grads[n]) for n in ("mix_b_gate", "conv_w", "q_norm_g", "kv_norm_g")}
    small["ln_g"] = jnp.stack([jnp.stack(g) for g in grads["ln_g"]])
    small["ln_b"] = jnp.stack([jnp.stack(g) for g in grads["ln_b"]])
    small["meta_tokens"] = dh[:N_META]
    return loss8, dh[N_META:t_real], big, small


def kernel(x, meta_tokens, ffn1_w_up, ffn1_w_down, mix_w_in, mix_b_gate, conv_w, q_norm_g, w_uq, kv_norm_g, w_ukv, w_br_conv, w_br_mla, w_o, ffn2_w_up, ffn2_w_down, ln_g, ln_b, loss_target, m_meta_tokens, m_ffn1_w_up, m_ffn1_w_down, m_mix_w_in, m_mix_b_gate, m_conv_w, m_q_norm_g, m_w_uq, m_kv_norm_g, m_w_ukv, m_w_br_conv, m_w_br_mla, m_w_o, m_ffn2_w_up, m_ffn2_w_down, m_ln_g, m_ln_b, v_meta_tokens, v_ffn1_w_up, v_ffn1_w_down, v_mix_w_in, v_mix_b_gate, v_conv_w, v_q_norm_g, v_w_uq, v_kv_norm_g, v_w_ukv, v_w_br_conv, v_w_br_mla, v_w_o, v_ffn2_w_up, v_ffn2_w_down, v_ln_g, v_ln_b):
    names = ["meta_tokens", "ffn1_w_up", "ffn1_w_down", "mix_w_in", "mix_b_gate", "conv_w", "q_norm_g", "w_uq",
             "kv_norm_g", "w_ukv", "w_br_conv", "w_br_mla", "w_o", "ffn2_w_up", "ffn2_w_down", "ln_g", "ln_b"]
    w = dict(zip(names, (meta_tokens, ffn1_w_up, ffn1_w_down, mix_w_in, mix_b_gate, conv_w, q_norm_g, w_uq,
                         kv_norm_g, w_ukv, w_br_conv, w_br_mla, w_o, ffn2_w_up, ffn2_w_down, ln_g, ln_b)))
    m = dict(zip(names, (m_meta_tokens, m_ffn1_w_up, m_ffn1_w_down, m_mix_w_in, m_mix_b_gate, m_conv_w, m_q_norm_g,
                         m_w_uq, m_kv_norm_g, m_w_ukv, m_w_br_conv, m_w_br_mla, m_w_o, m_ffn2_w_up, m_ffn2_w_down,
                         m_ln_g, m_ln_b)))
    v = dict(zip(names, (v_meta_tokens, v_ffn1_w_up, v_ffn1_w_down, v_mix_w_in, v_mix_b_gate, v_conv_w, v_q_norm_g,
                         v_w_uq, v_kv_norm_g, v_w_ukv, v_w_br_conv, v_w_br_mla, v_w_o, v_ffn2_w_up, v_ffn2_w_down,
                         v_ln_g, v_ln_b)))
    ix, iy, ic = lax.axis_index("x"), lax.axis_index("y"), lax.axis_index("c")
    dev = 4 * ix + 2 * iy + ic

    big_names = [n for n, _ in _BIG]
    big_axes = [a for _, a in _BIG]
    big_shapes = [w[n].shape for n in big_names]
    small_names = [n for n, _ in _SMALL_SHARDED]
    small_axes = [a for _, a in _SMALL_SHARDED]
    small_shapes = [w[n].shape for n in small_names]
    gathered = _all_gather([w[n].astype(BF16) for n in big_names] + [_pack([w[n] for n in small_names], F32)],
                           name="all_gather_weights")
    Wg = dict(zip(big_names, gathered[:-1]))
    W = {n: _from_blocks(Wg[n], ax) for n, ax in _BIG if not n.endswith("w_up")}
    S = dict(zip(small_names, _unpack_gathered(gathered[-1], small_shapes, small_axes)))
    S["q_norm_g"], S["kv_norm_g"] = q_norm_g, kv_norm_g

    loss8, grad_x, g_dest, G = _local_step(x[0], loss_target[0], W, {n: Wg[n] for n in ("ffn1_w_up", "ffn2_w_up")}, S)

    from_sibling = _exchange_sibling([g_dest[n] for n in big_names], name="rs_sibling")
    c_index = ic.reshape(1).astype(jnp.int32)
    pair = [_pair_sum(g_dest[n], r, c_index, name=f"rs_sibling_sum_{n}") for n, r in zip(big_names, from_sibling)]
    chip_sum = [p[0] for p in pair]
    from_chips = _exchange_chips([p[1] for p in pair], name="rs_chips")
    my_chip = (2 * ix + iy).reshape(1).astype(jnp.int32)
    big_res = [{}, {}, {}, {}]
    for n, cs, fc in zip(big_names, chip_sum, from_chips):
        res = _adamw(w[n], m[n], v[n], [(cs, None), (fc, 0), (fc, 1), (fc, 2)], my_chip, name=f"adamw_{n}")
        for kind in range(4):
            big_res[kind][n] = res[kind]

    small_all = small_names + list(_SMALL_REPL)
    part = _pack([G[n] for n in small_all] + [loss8[0, 0:1]], F32)
    full_shapes = [G[n].shape for n in small_all] + [(1,)]
    summed = _sum8(_all_gather([part], name="all_gather_small_grads")[0], name="sum_small_grads")
    unpacked = _unpack(summed, full_shapes)
    loss = unpacked[-1][0]
    g_full = dict(zip(small_all, unpacked[:-1]))
    g_loc = []
    for n in small_all:
        if n in _SMALL_REPL:
            g_loc.append(g_full[n])
        else:
            ax = dict(_SMALL_SHARDED)[n]
            g_loc.append(lax.dynamic_slice_in_dim(g_full[n], dev * w[n].shape[ax], w[n].shape[ax], axis=ax))
    loc_shapes = [w[n].shape for n in small_all]
    g_pack = _pack(g_loc, F32)
    small_out = _adamw(_pack([w[n] for n in small_all], F32)[None], _pack([m[n] for n in small_all], F32)[None],
                       _pack([v[n] for n in small_all], F32)[None], [(g_pack[None, None], 0)],
                       jnp.zeros((1,), jnp.int32), name="adamw_small")
    small_res = [dict(zip(small_all, _unpack(o, loc_shapes))) for o in small_out]

    outs = [loss, grad_x[None]]
    for kind in range(4):
        for n in names:
            outs.append(big_res[kind][n] if n in big_res[kind] else small_res[kind][n])
    return tuple(outs)
```

```python
import functools

import numpy as np
import jax
import jax.numpy as jnp
from jax import lax
from jax.experimental import pallas as pl
from jax.experimental.pallas import tpu as pltpu

F32 = jnp.float32
BF16 = jnp.bfloat16

D_MODEL = 1024
DEPTH = 2
N_META = 16
D_CONV = 512
CONV_WIDTH = 3
MLA_HEADS = 8
QK_NOPE = 64
QK_ROPE = 32
V_HEAD = 64
Q_LORA = 256
KV_LORA = 128
D_MLA = MLA_HEADS * V_HEAD
ROPE_BASE = 10000.0
NEG_INF = -1e30
D_FF = 2816
ALPHA = (2 * DEPTH) ** 0.25
LN_EPS = 1e-5
RMS_EPS = 1e-6
ATTN_SCALE = (QK_NOPE + QK_ROPE) ** -0.5
ADAM_LR = 0.001
ADAM_B1 = 0.9
ADAM_B2 = 0.999
ADAM_EPS = 1e-08
ADAM_WD = 0.01
ADAM_STEP = 10

N_DEV = 8
HEAD_PAD = 128
HEADS_PER_STEP = 2
FF_BLK = 2 * D_FF // N_DEV
FF_HALF_BLOCKS = N_DEV // 2
D_QK = MLA_HEADS * HEAD_PAD
Z_CONV = 0
Z_MID = 1536
Z_GATE = 2048
D_IN_PAD = 4096
D_IN_REAL = 4000
Z_KR_END = Z_MID + Q_LORA + KV_LORA + QK_ROPE

V7X_VMEM_LIMIT = 56 * 1024 * 1024
LANE = 128
ROW_ALIGN = 256


def _tile(n, cands):
    for c in cands:
        if n % c == 0:
            return c
    raise ValueError(f"no tile for {n} in {cands}")


def _params(sem):
    return pltpu.CompilerParams(dimension_semantics=sem, vmem_limit_bytes=V7X_VMEM_LIMIT)


def _mm_call(a, b, a_spec, b_spec, *, out_shape, out_spec, acc_shape, grid, name, trans_b=False,
             res=None, res_spec=None, res_scale=1.0, pieces=1):
    nk = grid[2]
    has_res = res is not None
    out_dtype = out_shape.dtype
    dims = (((1,), (1,)), ((), ())) if trans_b else (((1,), (0,)), ((), ()))

    def body(*refs):
        if has_res:
            a_ref, b_ref, r_ref, o_ref, acc = refs
        else:
            a_ref, b_ref, o_ref, acc = refs
        k = pl.program_id(2)
        if pieces == 1:
            part = lax.dot_general(a_ref[...], b_ref[...], dims, preferred_element_type=F32)
        else:
            part = lax.dot_general(a_ref[0], b_ref[0], dims, preferred_element_type=F32)
            for p in range(1, pieces):
                part = part + lax.dot_general(a_ref[p], b_ref[p], dims, preferred_element_type=F32)

        @pl.when(k == 0)
        def _():
            acc[...] = part

        @pl.when(k > 0)
        def _():
            acc[...] += part

        @pl.when(k == nk - 1)
        def _():
            out = acc[...]
            if has_res:
                out = out + res_scale * r_ref[...]
            o_ref[...] = out.astype(out_dtype)

    in_specs = [a_spec, b_spec]
    args = [a, b]
    if has_res:
        in_specs.append(res_spec)
        args.append(res)
    return pl.pallas_call(
        body, name=name, grid=grid, in_specs=in_specs, out_specs=out_spec, out_shape=out_shape,
        scratch_shapes=[pltpu.VMEM(acc_shape, F32)],
        compiler_params=_params(("parallel", "parallel", "arbitrary")),
    )(*args)


def _mm(a, b, *, out_dtype, name, trans_b=False, res=None, res_scale=1.0, tm, tn, tk):
    M, K = a.shape
    N = b.shape[0] if trans_b else b.shape[1]
    assert M % tm == 0 and N % tn == 0 and K % tk == 0
    b_spec = (pl.BlockSpec((tn, tk), lambda i, j, k: (j, k)) if trans_b
              else pl.BlockSpec((tk, tn), lambda i, j, k: (k, j)))
    tile = pl.BlockSpec((tm, tn), lambda i, j, k: (i, j))
    return _mm_call(a, b, pl.BlockSpec((tm, tk), lambda i, j, k: (i, k)), b_spec,
                    out_shape=jax.ShapeDtypeStruct((M, N), out_dtype), out_spec=tile, acc_shape=(tm, tn),
                    grid=(M // tm, N // tn, K // tk), name=name, trans_b=trans_b,
                    res=res, res_spec=tile, res_scale=res_scale)


def _mm_tn_call(a, b, a_spec, b_spec, *, out_shape, out_spec, grid, name):
    def body(a_ref, b_ref, o_ref):
        k = pl.program_id(2)
        part = lax.dot_general(a_ref[...], b_ref[...], (((0,), (0,)), ((), ())),
                               preferred_element_type=F32)

        @pl.when(k == 0)
        def _():
            o_ref[...] = part

        @pl.when(k > 0)
        def _():
            o_ref[...] += part

    return pl.pallas_call(
        body, name=name, grid=grid, in_specs=[a_spec, b_spec], out_specs=out_spec, out_shape=out_shape,
        compiler_params=_params(("parallel", "parallel", "arbitrary")),
    )(a, b)


def _mm_tn(a, b, *, name, tm, tn, tk):
    T, M = a.shape
    N = b.shape[1]
    assert M % tm == 0 and N % tn == 0 and T % tk == 0
    return _mm_tn_call(a, b, pl.BlockSpec((tk, tm), lambda i, j, k: (k, i)),
                       pl.BlockSpec((tk, tn), lambda i, j, k: (k, j)),
                       out_shape=jax.ShapeDtypeStruct((M, N), F32),
                       out_spec=pl.BlockSpec((tm, tn), lambda i, j, k: (i, j)),
                       grid=(M // tm, N // tn, T // tk), name=name)


def _ffn_up(hb, w_up8, *, name, tm):
    T = hb.shape[0]

    def body(h_ref, wg_ref, wu_ref, gu_ref, a_ref):
        h = h_ref[...]
        g = jnp.dot(h, wg_ref[...], preferred_element_type=F32)
        u = jnp.dot(h, wu_ref[...], preferred_element_type=F32)
        gu_ref[0] = g.astype(BF16)
        gu_ref[1] = u.astype(BF16)
        a_ref[...] = (g * jax.nn.sigmoid(g) * u).astype(BF16)

    return pl.pallas_call(
        body, name=name, grid=(FF_HALF_BLOCKS, T // tm),
        in_specs=[pl.BlockSpec((tm, D_MODEL), lambda j, i: (i, 0)),
                  pl.BlockSpec((None, D_MODEL, FF_BLK), lambda j, i: (j, 0, 0)),
                  pl.BlockSpec((None, D_MODEL, FF_BLK), lambda j, i: (j + FF_HALF_BLOCKS, 0, 0))],
        out_specs=[pl.BlockSpec((2, None, tm, FF_BLK), lambda j, i: (0, j, i, 0)),
                   pl.BlockSpec((None, tm, FF_BLK), lambda j, i: (j, i, 0))],
        out_shape=[jax.ShapeDtypeStruct((2, FF_HALF_BLOCKS, T, FF_BLK), BF16),
                   jax.ShapeDtypeStruct((FF_HALF_BLOCKS, T, FF_BLK), BF16)],
        compiler_params=_params(("parallel", "parallel")),
    )(hb, w_up8, w_up8)


def _mm_res_ln(a, w, res, g, b, *, scale, name, tm):
    split = a.ndim == 3
    if split:
        S, T, Ks = a.shape
        K = S * Ks
    else:
        T, K = a.shape

    def body(a_ref, w_ref, res_ref, g_ref, b_ref, r_ref, y_ref, yb_ref):
        if split:
            f = jnp.dot(a_ref[0], w_ref[0:Ks, :], preferred_element_type=F32)
            for s in range(1, S):
                f = f + jnp.dot(a_ref[s], w_ref[s * Ks:(s + 1) * Ks, :], preferred_element_type=F32)
        else:
            f = jnp.dot(a_ref[...], w_ref[...], preferred_element_type=F32)
        r = ALPHA * res_ref[...] + scale * f
        mu = jnp.mean(r, axis=-1, keepdims=True)
        xc = r - mu
        var = jnp.mean(xc * xc, axis=-1, keepdims=True)
        y = xc * lax.rsqrt(var + LN_EPS) * g_ref[...] + b_ref[...]
        r_ref[...] = r
        y_ref[...] = y
        yb_ref[...] = y.astype(BF16)

    row = pl.BlockSpec((tm, D_MODEL), lambda i: (i, 0))
    vec = pl.BlockSpec((1, D_MODEL), lambda i: (0, 0))
    return pl.pallas_call(
        body, name=name, grid=(T // tm,),
        in_specs=[pl.BlockSpec((S, tm, Ks), lambda i: (0, i, 0)) if split else pl.BlockSpec((tm, K), lambda i: (i, 0)),
                  pl.BlockSpec((K, D_MODEL), lambda i: (0, 0)), row, vec, vec],
        out_specs=[row, row, row],
        out_shape=[jax.ShapeDtypeStruct((T, D_MODEL), F32), jax.ShapeDtypeStruct((T, D_MODEL), F32),
                   jax.ShapeDtypeStruct((T, D_MODEL), BF16)],
        compiler_params=_params(("parallel",)),
    )(a, w, res, g, b)


def _conv_fwd(z, conv_w8, *, name, tm):
    T = z.shape[0]
    hb = tm // 8

    def body(b_ref, c_ref, h_ref, cp_ref, hp_ref, w_ref, y_ref):
        i = pl.program_id(0)
        u = c_ref[...] * h_ref[...]
        up = jnp.where(i > 0, cp_ref[...] * hp_ref[...], 0.0)
        ue = jnp.concatenate([up, u], axis=0)
        s1 = pltpu.roll(ue, 1, 0)[8:]
        s2 = pltpu.roll(ue, 2, 0)[8:]
        w = w_ref[...]
        conv = w[0:1] * s2 + w[1:2] * s1 + w[2:3] * u
        y_ref[...] = (b_ref[...] * conv).astype(BF16)

    def col(c):
        return pl.BlockSpec((tm, D_CONV), lambda i: (i, c))

    def prev(c):
        return pl.BlockSpec((8, D_CONV), lambda i: (jnp.maximum(i * hb - 1, 0), c))

    return pl.pallas_call(
        body, name=name, grid=(T // tm,),
        in_specs=[col(0), col(1), col(2), prev(1), prev(2), pl.BlockSpec((8, D_CONV), lambda i: (0, 0))],
        out_specs=pl.BlockSpec((tm, D_CONV), lambda i: (i, 0)),
        out_shape=jax.ShapeDtypeStruct((T, D_CONV), BF16),
        compiler_params=_params(("parallel",)),
    )(z, z, z, z, z, conv_w8)


def _rope(x, c, s1, s2):
    n = x.shape[-1]
    return x * c + pltpu.roll(x, 16, 1) * s1 + pltpu.roll(x, n - 16, 1) * s2


def _rope_t(d, c, s1, s2):
    n = d.shape[-1]
    return d * c + pltpu.roll(d * s1, n - 16, 1) + pltpu.roll(d * s2, 16, 1)


def _rms(x, g):
    rstd = lax.rsqrt(jnp.mean(x * x, axis=-1, keepdims=True) + RMS_EPS)
    return x * rstd * g


def _qkv_proj(z, gq, gkv, wq, wk, wv_ext, tabs, *, name, tm):
    T = z.shape[0]

    def body(z_ref, gq_ref, gkv_ref, wq_ref, wk_ref, wv_ref, c_ref, s1_ref, s2_ref,
             q_ref, k_ref, v_ref, qn_ref, kin_ref):
        zz = z_ref[...]
        qn = _rms(zz[:, :Q_LORA], gq_ref[...]).astype(BF16)
        kvn = _rms(zz[:, Q_LORA:Q_LORA + KV_LORA], gkv_ref[...]).astype(BF16)
        kin = jnp.concatenate([kvn, zz[:, Q_LORA + KV_LORA:].astype(BF16)], axis=-1)
        c = jnp.tile(c_ref[...], (1, MLA_HEADS))
        s1 = jnp.tile(s1_ref[...], (1, MLA_HEADS))
        s2 = jnp.tile(s2_ref[...], (1, MLA_HEADS))
        qpre = jnp.dot(qn, wq_ref[...], preferred_element_type=F32)
        kpre = jnp.dot(kin, wk_ref[...], preferred_element_type=F32)
        q_ref[...] = (_rope(qpre, c, s1, s2) * (ATTN_SCALE * LOG2_E)).astype(BF16)
        k_ref[...] = _rope(kpre, c, s1, s2).astype(BF16)
        vv = jnp.dot(kvn, wv_ref[...], preferred_element_type=F32)
        lane = lax.broadcasted_iota(jnp.int32, vv.shape, 1)
        v_ref[...] = jnp.where((lane & (HEAD_PAD - 1)) < V_HEAD, vv, 1.0).astype(BF16)
        qn_ref[...] = qn
        kin_ref[...] = kin

    def full(shape):
        return pl.BlockSpec(shape, lambda i: (0, 0))

    def rows(w, c=0):
        return pl.BlockSpec((tm, w), lambda i: (i, c))

    return pl.pallas_call(
        body, name=name, grid=(T // tm,),
        in_specs=[rows(512, Z_MID // 512), full((1, Q_LORA)), full((1, KV_LORA)),
                  full((Q_LORA, D_QK)), full((Q_LORA, D_QK)), full((KV_LORA, D_QK)),
                  rows(LANE), rows(LANE), rows(LANE)],
        out_specs=[rows(D_QK), rows(D_QK), rows(D_QK), rows(Q_LORA), rows(Q_LORA)],
        out_shape=[jax.ShapeDtypeStruct((T, D_QK), BF16), jax.ShapeDtypeStruct((T, D_QK), BF16),
                   jax.ShapeDtypeStruct((T, D_QK), BF16), jax.ShapeDtypeStruct((T, Q_LORA), BF16),
                   jax.ShapeDtypeStruct((T, Q_LORA), BF16)],
        compiler_params=_params(("parallel",)),
    )(z, gq, gkv, wq, wk, wv_ext, *tabs)


SOFTMAX_ROWS = 32
LOG2_E = 1.4426950408889634
_NT = (((1,), (1,)), ((), ()))
_TN = (((0,), (0,)), ((), ()))


def _diag_mask(s, row0, col0=0):
    row = lax.broadcasted_iota(jnp.int32, s.shape, 0) + row0
    col = lax.broadcasted_iota(jnp.int32, s.shape, 1) + col0
    return jnp.where(col <= row, s, NEG_INF)


_RELATIONS = tuple((rx, ry, rc) for rx in (0, 1) for ry in (0, 1) for rc in (0, 1))[1:]


class _Exchange:
    def __init__(self, kind, arrays):
        assert kind in ("gather", "scatter")
        self.kind, self.arrays, self.n = kind, list(arrays), len(arrays)

    def out_shapes(self):
        if self.kind == "gather":
            return [jax.ShapeDtypeStruct((N_DEV,) + a.shape, a.dtype) for a in self.arrays]
        return [jax.ShapeDtypeStruct((N_DEV - 1,) + a.shape[1:], a.dtype) for a in self.arrays]

    def scratch_shapes(self):
        sems = [pltpu.SemaphoreType.DMA((7 * self.n,)), pltpu.SemaphoreType.DMA((7 * self.n,))]
        if self.kind == "gather":
            sems.append(pltpu.SemaphoreType.DMA((self.n,)))
        return sems

    def _copies(self, src_refs, out_refs, sems):
        x, y, c = lax.axis_index("x"), lax.axis_index("y"), lax.axis_index("c")
        me = 4 * x + 2 * y + c
        sends, recvs, local = [], [], []
        for a in range(self.n):
            for k, rel in enumerate(_RELATIONS):
                peer = tuple((1 - p) if r else p for p, r in zip((x, y, c), rel))
                peer_index = 4 * peer[0] + 2 * peer[1] + peer[2]
                if self.kind == "gather":
                    src, lands_there, lands_here = src_refs[a], out_refs[a].at[me], out_refs[a].at[peer_index]
                else:
                    src, lands_there, lands_here = src_refs[a].at[peer_index], out_refs[a].at[k], out_refs[a].at[k]
                for dst, group in ((lands_there, sends), (lands_here, recvs)):
                    group.append(pltpu.make_async_remote_copy(
                        src_ref=src, dst_ref=dst, send_sem=sems[0].at[7 * a + k], recv_sem=sems[1].at[7 * a + k],
                        device_id=peer, device_id_type=_MESH_ID))
            if self.kind == "gather":
                local.append(pltpu.make_async_copy(src_refs[a], out_refs[a].at[me], sems[2].at[a]))
        return sends, recvs, local

    def start(self, src_refs, out_refs, sems):
        sends, _, local = self._copies(src_refs, out_refs, sems)
        for cp in local + sends:
            cp.start()

    def wait(self, src_refs, out_refs, sems):
        sends, recvs, local = self._copies(src_refs, out_refs, sems)
        for cp in recvs:
            cp.wait_recv()
        for cp in sends:
            cp.wait_send()
        for cp in local:
            cp.wait()


def _exchange_operands(exchange):
    if exchange is None:
        return [], [], [], []
    return exchange.arrays, [_ANY] * exchange.n, exchange.out_shapes(), exchange.scratch_shapes()


def _carry_exchange(body, exchange, n_prefetch, n_in, n_out, grid):
    if exchange is None:
        return body
    n = exchange.n
    last = tuple(g - 1 for g in grid)

    def wrapped(*refs):
        head = refs[:n_prefetch + n_in]
        src_refs = refs[n_prefetch + n_in:n_prefetch + n_in + n]
        rest = refs[n_prefetch + n_in + n:]
        outs, out_refs, rest = rest[:n_out], rest[n_out:n_out + n], rest[n_out + n:]
        n_sems = len(exchange.scratch_shapes())
        scratch, sems = rest[:len(rest) - n_sems], rest[len(rest) - n_sems:]
        at_first = functools.reduce(jnp.logical_and, [pl.program_id(d) == 0 for d in range(len(grid))])
        at_last = functools.reduce(jnp.logical_and, [pl.program_id(d) == last[d] for d in range(len(grid))])

        @pl.when(at_first)
        def _():
            exchange.start(src_refs, out_refs, sems)

        body(*head, *outs, *scratch)

        @pl.when(at_last)
        def _():
            exchange.wait(src_refs, out_refs, sems)

    return wrapped


def _attn_fwd(q, k, v, *, name, blk, exchange=None):
    T = q.shape[0]
    n = T // blk
    hp = HEADS_PER_STEP
    qi = np.array([i for i in range(n) for j in range(i + 1)], np.int32)
    kj = np.array([j for i in range(n) for j in range(i + 1)], np.int32)

    rc = _tile(blk, (SOFTMAX_ROWS,))

    def body(qi_ref, kj_ref, q_ref, k_ref, v_ref, o_ref, lse_ref, m_sc, acc_sc, s_sc, p_sc, red_sc):
        s_id = pl.program_id(1)
        i = qi_ref[s_id]
        j = kj_ref[s_id]

        @pl.when(j == 0)
        def _():
            m_sc[...] = jnp.full(m_sc.shape, NEG_INF, F32)
            acc_sc[...] = jnp.zeros(acc_sc.shape, F32)

        def head_step(hh, diagonal):
            hs = slice(hh * HEAD_PAD, (hh + 1) * HEAD_PAD)
            s_sc[hh] = lax.dot_general(q_ref[:, hs], k_ref[:, hs], _NT, preferred_element_type=F32)
            lanes = [slice(t * LANE, (t + 1) * LANE) for t in range(blk // LANE)]
            for r in range(blk // rc):
                rows = slice(r * rc, (r + 1) * rc)
                s = s_sc[hh, rows, :]
                if diagonal:
                    s = _diag_mask(s, r * rc)
                    s_sc[hh, rows, :] = s
                pm = s[:, lanes[0]]
                for t in lanes[1:]:
                    pm = jnp.maximum(pm, s[:, t])
                red_sc[hh, rows, :] = pm
            m_old = m_sc[hh]
            row_max = jnp.max(red_sc[hh], axis=-1, keepdims=True)
            m_new = jnp.maximum(m_old, jnp.broadcast_to(row_max, (blk, LANE)))
            a = jnp.exp2(m_old - m_new)
            m_sc[hh] = m_new
            for r in range(blk // rc):
                rows = slice(r * rc, (r + 1) * rc)
                mb = m_sc[hh, rows, :]
                for t in lanes:
                    p_sc[hh, rows, t] = jnp.exp2(s_sc[hh, rows, t] - mb).astype(BF16)
            acc_sc[hh] = a * acc_sc[hh] + jnp.dot(p_sc[hh], v_ref[:, hs], preferred_element_type=F32)

        @pl.when(j < i)
        def _():
            for hh in range(hp):
                head_step(hh, False)

        @pl.when(j == i)
        def _():
            for hh in range(hp):
                head_step(hh, True)
            for hh in range(hp):
                acc = acc_sc[hh]
                swapped = pltpu.roll(acc, V_HEAD, 1)
                o_ref[:, hh * V_HEAD:(hh + 1) * V_HEAD] = (acc / swapped)[:, :V_HEAD].astype(BF16)
                lane = lax.broadcasted_iota(jnp.int32, acc.shape, 1)
                denom = jnp.where(lane < V_HEAD, swapped, acc)
                lse_ref[hh] = m_sc[hh] + jnp.log(denom) * LOG2_E

    grid = (MLA_HEADS // hp, len(qi))
    ex_args, ex_specs, ex_out, ex_scratch = _exchange_operands(exchange)
    grid_spec = pltpu.PrefetchScalarGridSpec(
        num_scalar_prefetch=2, grid=grid,
        in_specs=[pl.BlockSpec((blk, hp * HEAD_PAD), lambda g, s, qi, kj: (qi[s], g)),
                  pl.BlockSpec((blk, hp * HEAD_PAD), lambda g, s, qi, kj: (kj[s], g)),
                  pl.BlockSpec((blk, hp * HEAD_PAD), lambda g, s, qi, kj: (kj[s], g))] + ex_specs,
        out_specs=[pl.BlockSpec((blk, hp * V_HEAD), lambda g, s, qi, kj: (qi[s], g)),
                   pl.BlockSpec((hp, blk, LANE), lambda g, s, qi, kj: (g, qi[s], 0))] + ex_specs,
        scratch_shapes=[pltpu.VMEM((hp, blk, LANE), F32), pltpu.VMEM((hp, blk, HEAD_PAD), F32),
                        pltpu.VMEM((hp, blk, blk), F32), pltpu.VMEM((hp, blk, blk), BF16),
                        pltpu.VMEM((hp, blk, LANE), F32)] + ex_scratch)
    return pl.pallas_call(
        _carry_exchange(body, exchange, 2, 3, 2, grid), name=name, grid_spec=grid_spec,
        out_shape=[jax.ShapeDtypeStruct((T, D_MLA), BF16),
                   jax.ShapeDtypeStruct((MLA_HEADS, T, LANE), F32)] + ex_out,
        compiler_params=_params(("arbitrary", "arbitrary") if exchange else ("parallel", "arbitrary")),
    )(jnp.asarray(qi), jnp.asarray(kj), q, k, v, *ex_args)


def _merge(yc, ym, wbc, wbm, z, bg, *, name, tm):
    T = yc.shape[0]

    def body(yc_ref, ym_ref, wbc_ref, wbm_ref, gc_ref, gm_ref, bg_ref, mg_ref, pa_ref, pb_ref):
        pa = jnp.dot(yc_ref[...], wbc_ref[...], preferred_element_type=F32)
        pb = jnp.dot(ym_ref[...], wbm_ref[...], preferred_element_type=F32)
        bgv = bg_ref[...]
        sa = jax.nn.sigmoid(gc_ref[...] + bgv[0:1])
        sb = jax.nn.sigmoid(gm_ref[...] + bgv[1:2])
        mg_ref[...] = (sa * pa + sb * pb).astype(BF16)
        pa_ref[...] = pa.astype(BF16)
        pb_ref[...] = pb.astype(BF16)

    row = pl.BlockSpec((tm, D_MODEL), lambda i: (i, 0))
    return pl.pallas_call(
        body, name=name, grid=(T // tm,),
        in_specs=[pl.BlockSpec((tm, D_CONV), lambda i: (i, 0)), pl.BlockSpec((tm, D_MLA), lambda i: (i, 0)),
                  pl.BlockSpec((D_CONV, D_MODEL), lambda i: (0, 0)), pl.BlockSpec((D_MLA, D_MODEL), lambda i: (0, 0)),
                  pl.BlockSpec((tm, D_MODEL), lambda i: (i, Z_GATE // D_MODEL)),
                  pl.BlockSpec((tm, D_MODEL), lambda i: (i, Z_GATE // D_MODEL + 1)),
                  pl.BlockSpec((8, D_MODEL), lambda i: (0, 0))],
        out_specs=[row, row, row],
        out_shape=[jax.ShapeDtypeStruct((T, D_MODEL), BF16)] * 3,
        compiler_params=_params(("parallel",)),
    )(yc, ym, wbc, wbm, z, z, bg)


def _loss_head(h, tgt, *, t_real, name, tm):
    T = h.shape[0]

    def body(h_ref, t_ref, dy_ref, loss_ref):
        i = pl.program_id(0)
        row = lax.broadcasted_iota(jnp.int32, (tm, 1), 0) + i * tm
        valid = (row >= N_META) & (row < t_real)
        err = jnp.where(valid, h_ref[...] - t_ref[...], 0.0)
        dy_ref[...] = err * (1.0 / D_MODEL)
        part = 0.5 * jnp.sum(jnp.sum(err * err, axis=-1, keepdims=True) * (1.0 / D_MODEL), axis=0, keepdims=True)

        @pl.when(i == 0)
        def _():
            loss_ref[...] = jnp.zeros(loss_ref.shape, F32)

        loss_ref[...] += jnp.broadcast_to(part, loss_ref.shape)

    row_spec = pl.BlockSpec((tm, D_MODEL), lambda i: (i, 0))
    return pl.pallas_call(
        body, name=name, grid=(T // tm,),
        in_specs=[row_spec, row_spec],
        out_specs=[row_spec, pl.BlockSpec((8, LANE), lambda i: (0, 0))],
        out_shape=[jax.ShapeDtypeStruct((T, D_MODEL), F32), jax.ShapeDtypeStruct((8, LANE), F32)],
        compiler_params=_params(("arbitrary",)),
    )(h, tgt)


def _ln_bwd(dy, r, g, *, scale, name, tm):
    T = dy.shape[0]

    def body(dy_ref, r_ref, g_ref, dr_ref, drb_ref, dg_ref, db_ref):
        i = pl.program_id(0)
        rr = r_ref[...]
        dyv = dy_ref[...]
        mu = jnp.mean(rr, axis=-1, keepdims=True)
        xc = rr - mu
        rstd = lax.rsqrt(jnp.mean(xc * xc, axis=-1, keepdims=True) + LN_EPS)
        xh = xc * rstd
        dxh = dyv * g_ref[...]
        m1 = jnp.mean(dxh, axis=-1, keepdims=True)
        m2 = jnp.mean(dxh * xh, axis=-1, keepdims=True)
        dr = rstd * (dxh - m1 - xh * m2)
        dr_ref[...] = dr
        drb_ref[...] = (scale * dr).astype(BF16)

        @pl.when(i == 0)
        def _():
            dg_ref[...] = jnp.zeros(dg_ref.shape, F32)
            db_ref[...] = jnp.zeros(db_ref.shape, F32)

        dg_ref[0:1, :] += jnp.sum(dyv * xh, axis=0, keepdims=True)
        db_ref[0:1, :] += jnp.sum(dyv, axis=0, keepdims=True)

    row = pl.BlockSpec((tm, D_MODEL), lambda i: (i, 0))
    acc = pl.BlockSpec((8, D_MODEL), lambda i: (0, 0))
    return pl.pallas_call(
        body, name=name, grid=(T // tm,),
        in_specs=[row, row, pl.BlockSpec((1, D_MODEL), lambda i: (0, 0))],
        out_specs=[row, row, acc, acc],
        out_shape=[jax.ShapeDtypeStruct((T, D_MODEL), F32), jax.ShapeDtypeStruct((T, D_MODEL), BF16),
                   jax.ShapeDtypeStruct((8, D_MODEL), F32), jax.ShapeDtypeStruct((8, D_MODEL), F32)],
        compiler_params=_params(("arbitrary",)),
    )(dy, r, g)


def _ffn_bwd_mid(dfb, w_down, gu, *, name, tm):
    T = dfb.shape[0]

    def body(df_ref, w_ref, gu_ref, o_ref):
        da = lax.dot_general(df_ref[...], w_ref[...], (((1,), (1,)), ((), ())), preferred_element_type=F32)
        g = gu_ref[0].astype(F32)
        u = gu_ref[1].astype(F32)
        sg = jax.nn.sigmoid(g)
        o_ref[0] = (da * u * (sg * (1.0 + g * (1.0 - sg)))).astype(BF16)
        o_ref[1] = (da * (g * sg)).astype(BF16)

    return pl.pallas_call(
        body, name=name, grid=(FF_HALF_BLOCKS, T // tm),
        in_specs=[pl.BlockSpec((tm, D_MODEL), lambda j, i: (i, 0)),
                  pl.BlockSpec((FF_BLK, D_MODEL), lambda j, i: (j, 0)),
                  pl.BlockSpec((2, None, tm, FF_BLK), lambda j, i: (0, j, i, 0))],
        out_specs=pl.BlockSpec((2, None, tm, FF_BLK), lambda j, i: (0, j, i, 0)),
        out_shape=jax.ShapeDtypeStruct((2, FF_HALF_BLOCKS, T, FF_BLK), BF16),
        compiler_params=_params(("parallel", "parallel")),
    )(dfb, w_down, gu)


def _wo_bwd(dmb, wo, z, bg, pa, pb, *, name, tm):
    T = dmb.shape[0]

    def body(dm_ref, w_ref, gc_ref, gm_ref, bg_ref, pa_ref, pb_ref, dpa_ref, dpb_ref, dg_ref, dbg_ref):
        i = pl.program_id(0)
        dm = lax.dot_general(dm_ref[...], w_ref[...], _NT, preferred_element_type=F32)
        bgv = bg_ref[...]
        sa = jax.nn.sigmoid(gc_ref[...] + bgv[0:1])
        sb = jax.nn.sigmoid(gm_ref[...] + bgv[1:2])
        dpa_ref[...] = (dm * sa).astype(BF16)
        dpb_ref[...] = (dm * sb).astype(BF16)
        dga = dm * pa_ref[...].astype(F32) * (sa * (1.0 - sa))
        dgb = dm * pb_ref[...].astype(F32) * (sb * (1.0 - sb))
        dg_ref[:, :D_MODEL] = dga.astype(BF16)
        dg_ref[:, D_MODEL:] = dgb.astype(BF16)

        @pl.when(i == 0)
        def _():
            dbg_ref[...] = jnp.zeros(dbg_ref.shape, F32)

        dbg_ref[0:1, :] += jnp.sum(dga, axis=0, keepdims=True)
        dbg_ref[1:2, :] += jnp.sum(dgb, axis=0, keepdims=True)

    row = pl.BlockSpec((tm, D_MODEL), lambda i: (i, 0))
    return pl.pallas_call(
        body, name=name, grid=(T // tm,),
        in_specs=[row, pl.BlockSpec((D_MODEL, D_MODEL), lambda i: (0, 0)),
                  pl.BlockSpec((tm, D_MODEL), lambda i: (i, Z_GATE // D_MODEL)),
                  pl.BlockSpec((tm, D_MODEL), lambda i: (i, Z_GATE // D_MODEL + 1)),
                  pl.BlockSpec((8, D_MODEL), lambda i: (0, 0)), row, row],
        out_specs=[row, row, pl.BlockSpec((tm, 2 * D_MODEL), lambda i: (i, 0)),
                   pl.BlockSpec((8, D_MODEL), lambda i: (0, 0))],
        out_shape=[jax.ShapeDtypeStruct((T, D_MODEL), BF16), jax.ShapeDtypeStruct((T, D_MODEL), BF16),
                   jax.ShapeDtypeStruct((T, 2 * D_MODEL), BF16), jax.ShapeDtypeStruct((8, D_MODEL), F32)],
        compiler_params=_params(("arbitrary",)),
    )(dmb, wo, z, z, bg, pa, pb)


def _conv_bwd(dy, z, conv_w8, *, name, tm):
    T = dy.shape[0]
    n = T // tm
    hb = tm // 8

    def body(dy_ref, b_ref, c_ref, h_ref, cp_ref, hp_ref, dyn_ref, bn_ref, w_ref, dz_ref, dw_ref):
        i = pl.program_id(0)
        u = c_ref[...] * h_ref[...]
        up = jnp.where(i > 0, cp_ref[...] * hp_ref[...], 0.0)
        ue = jnp.concatenate([up, u], axis=0)
        s1 = pltpu.roll(ue, 1, 0)[8:]
        s2 = pltpu.roll(ue, 2, 0)[8:]
        w = w_ref[...]
        conv = w[0:1] * s2 + w[1:2] * s1 + w[2:3] * u
        dyv = dy_ref[...]
        e = dyv * b_ref[...]
        en = jnp.where(i < n - 1, dyn_ref[...] * bn_ref[...], 0.0)
        ee = jnp.concatenate([e, en], axis=0)
        e1 = pltpu.roll(ee, tm + 8 - 1, 0)[:tm]
        e2 = pltpu.roll(ee, tm + 8 - 2, 0)[:tm]
        du = w[2:3] * e + w[1:2] * e1 + w[0:1] * e2
        dz_ref[:, 0:D_CONV] = (dyv * conv).astype(BF16)
        dz_ref[:, D_CONV:2 * D_CONV] = (du * h_ref[...]).astype(BF16)
        dz_ref[:, 2 * D_CONV:] = (du * c_ref[...]).astype(BF16)

        @pl.when(i == 0)
        def _():
            dw_ref[...] = jnp.zeros(dw_ref.shape, F32)

        dw_ref[0:1, :] += jnp.sum(e * s2, axis=0, keepdims=True)
        dw_ref[1:2, :] += jnp.sum(e * s1, axis=0, keepdims=True)
        dw_ref[2:3, :] += jnp.sum(e * u, axis=0, keepdims=True)

    def col(c):
        return pl.BlockSpec((tm, D_CONV), lambda i: (i, c))

    def prev(c):
        return pl.BlockSpec((8, D_CONV), lambda i: (jnp.maximum(i * hb - 1, 0), c))

    def nxt(c):
        return pl.BlockSpec((8, D_CONV), lambda i: (jnp.minimum((i + 1) * hb, T // 8 - 1), c))

    return pl.pallas_call(
        body, name=name, grid=(n,),
        in_specs=[col(0), col(0), col(1), col(2), prev(1), prev(2), nxt(0), nxt(0),
                  pl.BlockSpec((8, D_CONV), lambda i: (0, 0))],
        out_specs=[pl.BlockSpec((tm, 3 * D_CONV), lambda i: (i, 0)), pl.BlockSpec((8, D_CONV), lambda i: (0, 0))],
        out_shape=[jax.ShapeDtypeStruct((T, 3 * D_CONV), BF16), jax.ShapeDtypeStruct((8, D_CONV), F32)],
        compiler_params=_params(("arbitrary",)),
    )(dy, z, z, z, z, z, dy, z, conv_w8)


def _attn_bwd(q, k, v, o, do, lse, *, name, blk, exchange=None):
    T = q.shape[0]
    n = T // blk
    hp = HEADS_PER_STEP
    qi = np.array([i for j in range(n) for i in range(j, n)], np.int32)
    kj = np.array([j for j in range(n) for i in range(j, n)], np.int32)

    rc = _tile(blk, (SOFTMAX_ROWS,))

    def body(qi_ref, kj_ref, q_ref, k_ref, v_ref, o_ref, do_ref, lse_ref, dq_ref, dk_ref, dv_ref,
             dk_sc, dv_sc, s_sc, dp_sc, p_sc, ds_sc, delta_sc):
        s_id = pl.program_id(1)
        i = qi_ref[s_id]
        j = kj_ref[s_id]

        @pl.when(s_id == 0)
        def _():
            dq_ref[...] = jnp.zeros(dq_ref.shape, F32)

        @pl.when(i == j)
        def _():
            dk_sc[...] = jnp.zeros(dk_sc.shape, F32)
            dv_sc[...] = jnp.zeros(dv_sc.shape, F32)

        q_rows = pl.ds(pl.multiple_of(i * blk, blk), blk)

        def head_step(hh, diagonal):
            hs = slice(hh * HEAD_PAD, (hh + 1) * HEAD_PAD)
            vs = slice(hh * V_HEAD, (hh + 1) * V_HEAD)
            qh = q_ref[:, hs]
            kh = k_ref[:, hs]
            doh = do_ref[:, vs]
            s_sc[hh] = lax.dot_general(qh, kh, _NT, preferred_element_type=F32)
            vh = v_ref[:, hh * HEAD_PAD:hh * HEAD_PAD + V_HEAD]
            dp_sc[hh] = lax.dot_general(doh, vh, _NT, preferred_element_type=F32)
            delta = jnp.sum(doh.astype(F32) * o_ref[:, vs].astype(F32), axis=-1, keepdims=True)
            delta_sc[hh] = jnp.broadcast_to(delta, (blk, LANE))
            for r in range(blk // rc):
                rows = slice(r * rc, (r + 1) * rc)
                lse = lse_ref[hh, rows, :]
                dl = delta_sc[hh, rows, :]
                for t in range(blk // LANE):
                    cols = slice(t * LANE, (t + 1) * LANE)
                    s = s_sc[hh, rows, cols]
                    if diagonal:
                        s = _diag_mask(s, r * rc, t * LANE)
                    p = jnp.exp2(s - lse)
                    p_sc[hh, rows, cols] = p.astype(BF16)
                    ds_sc[hh, rows, cols] = (p * (dp_sc[hh, rows, cols] - dl)).astype(BF16)
            dv_sc[hh] += lax.dot_general(p_sc[hh], doh, _TN, preferred_element_type=F32)
            dk_sc[hh] += lax.dot_general(ds_sc[hh], qh, _TN, preferred_element_type=F32)
            dq_ref[q_rows, hs] += jnp.dot(ds_sc[hh], kh, preferred_element_type=F32)

        @pl.when(j < i)
        def _():
            for hh in range(hp):
                head_step(hh, False)

        @pl.when(j == i)
        def _():
            for hh in range(hp):
                head_step(hh, True)

        @pl.when(i == n - 1)
        def _():
            for hh in range(hp):
                dk_ref[:, hh * HEAD_PAD:(hh + 1) * HEAD_PAD] = dk_sc[hh] * (1.0 / LOG2_E)
                dv_ref[:, hh * V_HEAD:(hh + 1) * V_HEAD] = dv_sc[hh]

    wq = hp * HEAD_PAD
    wv = hp * V_HEAD
    grid = (MLA_HEADS // hp, len(qi))
    ex_args, ex_specs, ex_out, ex_scratch = _exchange_operands(exchange)
    grid_spec = pltpu.PrefetchScalarGridSpec(
        num_scalar_prefetch=2, grid=grid,
        in_specs=[pl.BlockSpec((blk, wq), lambda g, s, qi, kj: (qi[s], g)),
                  pl.BlockSpec((blk, wq), lambda g, s, qi, kj: (kj[s], g)),
                  pl.BlockSpec((blk, wq), lambda g, s, qi, kj: (kj[s], g)),
                  pl.BlockSpec((blk, wv), lambda g, s, qi, kj: (qi[s], g)),
                  pl.BlockSpec((blk, wv), lambda g, s, qi, kj: (qi[s], g)),
                  pl.BlockSpec((hp, blk, LANE), lambda g, s, qi, kj: (g, qi[s], 0))] + ex_specs,
        out_specs=[pl.BlockSpec((T, wq), lambda g, s, qi, kj: (0, g)),
                   pl.BlockSpec((blk, wq), lambda g, s, qi, kj: (kj[s], g)),
                   pl.BlockSpec((blk, wv), lambda g, s, qi, kj: (kj[s], g))] + ex_specs,
        scratch_shapes=[pltpu.VMEM((hp, blk, HEAD_PAD), F32), pltpu.VMEM((hp, blk, V_HEAD), F32),
                        pltpu.VMEM((hp, blk, blk), F32), pltpu.VMEM((hp, blk, blk), F32),
                        pltpu.VMEM((hp, blk, blk), BF16), pltpu.VMEM((hp, blk, blk), BF16),
                        pltpu.VMEM((hp, blk, LANE), F32)] + ex_scratch)
    return pl.pallas_call(
        _carry_exchange(body, exchange, 2, 6, 3, grid), name=name, grid_spec=grid_spec,
        out_shape=[jax.ShapeDtypeStruct((T, D_QK), F32), jax.ShapeDtypeStruct((T, D_QK), F32),
                   jax.ShapeDtypeStruct((T, D_MLA), F32)] + ex_out,
        compiler_params=_params(("arbitrary", "arbitrary") if exchange else ("parallel", "arbitrary")),
    )(jnp.asarray(qi), jnp.asarray(kj), q, k, v, o, do, lse, *ex_args)


def _qkv_bwd(dq, dk, dv, z, gq, gkv, wq, wk, wv, tabs, *, name, tm):
    T = dq.shape[0]

    def body(dq_ref, dk_ref, dv_ref, z_ref, gq_ref, gkv_ref, wq_ref, wk_ref, wv_ref, c_ref, s1_ref, s2_ref,
             dz_ref, dqb_ref, dkb_ref, dvb_ref, dgq_ref, dgkv_ref):
        i = pl.program_id(0)
        c = jnp.tile(c_ref[...], (1, MLA_HEADS))
        s1 = jnp.tile(s1_ref[...], (1, MLA_HEADS))
        s2 = jnp.tile(s2_ref[...], (1, MLA_HEADS))
        dqp = _rope_t(dq_ref[...] * ATTN_SCALE, c, s1, s2).astype(BF16)
        dkp = _rope_t(dk_ref[...], c, s1, s2).astype(BF16)
        dvb = dv_ref[...].astype(BF16)
        dqb_ref[...] = dqp
        dkb_ref[...] = dkp
        dvb_ref[...] = dvb
        dqn = lax.dot_general(dqp, wq_ref[...], _NT, preferred_element_type=F32)
        dkin = lax.dot_general(dkp, wk_ref[...], _NT, preferred_element_type=F32)
        dkvn = dkin[:, :KV_LORA] + lax.dot_general(dvb, wv_ref[...], _NT, preferred_element_type=F32)
        zz = z_ref[...]

        def rms_bwd(x, g, dy):
            rstd = lax.rsqrt(jnp.mean(x * x, axis=-1, keepdims=True) + RMS_EPS)
            xh = x * rstd
            dxh = dy * g
            dx = rstd * (dxh - xh * jnp.mean(dxh * xh, axis=-1, keepdims=True))
            return dx, jnp.sum(dy * xh, axis=0, keepdims=True)

        dcq, dgq = rms_bwd(zz[:, :Q_LORA], gq_ref[...], dqn)
        dckv, dgkv = rms_bwd(zz[:, Q_LORA:Q_LORA + KV_LORA], gkv_ref[...], dkvn)
        dz_ref[:, :Q_LORA] = dcq.astype(BF16)
        dz_ref[:, Q_LORA:Q_LORA + KV_LORA] = dckv.astype(BF16)
        dz_ref[:, Q_LORA + KV_LORA:] = dkin[:, KV_LORA:].astype(BF16)

        @pl.when(i == 0)
        def _():
            dgq_ref[...] = jnp.zeros(dgq_ref.shape, F32)
            dgkv_ref[...] = jnp.zeros(dgkv_ref.shape, F32)

        dgq_ref[0:1, :] += dgq
        dgkv_ref[0:1, :] += dgkv

    def full(shape):
        return pl.BlockSpec(shape, lambda i: (0, 0))

    def rows(w, c=0):
        return pl.BlockSpec((tm, w), lambda i: (i, c))

    return pl.pallas_call(
        body, name=name, grid=(T // tm,),
        in_specs=[rows(D_QK), rows(D_QK), rows(D_MLA), rows(512, Z_MID // 512),
                  full((1, Q_LORA)), full((1, KV_LORA)),
                  full((Q_LORA, D_QK)), full((Q_LORA, D_QK)), full((KV_LORA, D_MLA)),
                  rows(LANE), rows(LANE), rows(LANE)],
        out_specs=[rows(512), rows(D_QK), rows(D_QK), rows(D_MLA), full((8, Q_LORA)), full((8, KV_LORA))],
        out_shape=[jax.ShapeDtypeStruct((T, 512), BF16), jax.ShapeDtypeStruct((T, D_QK), BF16),
                   jax.ShapeDtypeStruct((T, D_QK), BF16), jax.ShapeDtypeStruct((T, D_MLA), BF16),
                   jax.ShapeDtypeStruct((8, Q_LORA), F32), jax.ShapeDtypeStruct((8, KV_LORA), F32)],
        compiler_params=_params(("arbitrary",)),
    )(dq, dk, dv, z, gq, gkv, wq, wk, wv, *tabs)


FLAT_W = 1024


ELEMENTWISE_TILE_BYTES = 768 * 1024


def _row_tile(R, C):
    width = -(-C // LANE) * LANE * 4
    best = None
    for t in range(16, R + 1, 16):
        if R % t == 0 and t * width <= ELEMENTWISE_TILE_BYTES:
            best = t
    if best is None:
        best = R
    return best


def _adamw(w, m, v, layer, parts, part_index, *, name):
    _, R, C = w.shape
    tr = _row_tile(R, C)
    bc1 = 1.0 - ADAM_B1 ** ADAM_STEP
    bc2 = 1.0 - ADAM_B2 ** ADAM_STEP
    n_parts = len(parts)

    def body(idx_ref, w_ref, m_ref, v_ref, *refs):
        g_refs = refs[:n_parts]
        g_out, d_out, m_out, v_out = refs[n_parts:]
        g = g_refs[0][...].astype(F32)
        for r in g_refs[1:]:
            g = g + r[...].astype(F32)
        wv = w_ref[...]
        mn = ADAM_B1 * m_ref[...] + (1.0 - ADAM_B1) * g
        vn = ADAM_B2 * v_ref[...] + (1.0 - ADAM_B2) * (g * g)
        m_hat = mn / bc1
        v_hat = vn / bc2
        g_out[...] = g
        d_out[...] = -ADAM_LR * (m_hat / (jnp.sqrt(v_hat) + ADAM_EPS) + ADAM_WD * wv)
        m_out[...] = mn
        v_out[...] = vn

    layer_row = pl.BlockSpec((None, tr, C), lambda i, idx: (layer, i, 0))
    in_specs = [layer_row, layer_row, layer_row]
    args = [w, m, v]
    for arr, slot in parts:
        if slot is None:
            in_specs.append(pl.BlockSpec((None, tr, C), lambda i, idx: (idx[0], i, 0)))
        else:
            in_specs.append(pl.BlockSpec((None, tr, C), lambda i, idx, slot=slot: (slot, i, 0)))
        args.append(arr)
    grid_spec = pltpu.PrefetchScalarGridSpec(
        num_scalar_prefetch=1, grid=(R // tr,), in_specs=in_specs,
        out_specs=[pl.BlockSpec((tr, C), lambda i, idx: (i, 0))] * 4)
    return pl.pallas_call(
        body, name=name, grid_spec=grid_spec,
        out_shape=[jax.ShapeDtypeStruct((R, C), F32)] * 4,
        compiler_params=_params(("parallel",)),
    )(part_index, *args)


def _pair_sum(g, recv, c_index, *, name):
    _, R, C = g.shape
    tr = _row_tile(R, C)

    def body(c_ref, g_ref, r_ref, o_ref, ob_ref):
        total = g_ref[...] + r_ref[...]
        o_ref[...] = total
        ob_ref[...] = total.astype(BF16)

    def spec(index):
        return pl.BlockSpec((None, tr, C), index)

    grid_spec = pltpu.PrefetchScalarGridSpec(
        num_scalar_prefetch=1, grid=(4, R // tr),
        in_specs=[spec(lambda q, i, c: (2 * q + c[0], i, 0)), spec(lambda q, i, c: (q, i, 0))],
        out_specs=[spec(lambda q, i, c: (q, i, 0)), spec(lambda q, i, c: (q, i, 0))])
    return pl.pallas_call(
        body, name=name, grid_spec=grid_spec,
        out_shape=[jax.ShapeDtypeStruct((4, R, C), F32), jax.ShapeDtypeStruct((4, R, C), BF16)],
        compiler_params=_params(("parallel", "parallel")),
    )(c_index, g, recv)


def _sum8(parts, *, name):
    _, R, _ = parts.shape

    def body(p_ref, o_ref):
        acc = p_ref[0]
        for d in range(1, N_DEV):
            acc = acc + p_ref[d]
        o_ref[...] = acc

    return pl.pallas_call(
        body, name=name, grid=(1,),
        in_specs=[pl.BlockSpec((N_DEV, R, FLAT_W), lambda i: (0, 0, 0))],
        out_specs=pl.BlockSpec((R, FLAT_W), lambda i: (0, 0)),
        out_shape=jax.ShapeDtypeStruct((R, FLAT_W), F32),
        compiler_params=_params(("arbitrary",)),
    )(parts)


_MESH_ID = pl.DeviceIdType.MESH
_ANY = pl.BlockSpec(memory_space=pl.ANY)


def _all_gather(shards, *, name):
    n = len(shards)

    def body(*refs):
        x_refs, out_refs = refs[:n], refs[n:2 * n]
        send_sems, recv_sems, local_sems = refs[2 * n:]
        x, y, c = lax.axis_index("x"), lax.axis_index("y"), lax.axis_index("c")
        me, sibling = (x, y, c), (x, y, 1 - c)
        chips = [(1 - x, y), (x, 1 - y), (1 - x, 1 - y)]

        def blk(a, px, py, pc):
            return out_refs[a].at[4 * px + 2 * py + pc]

        def copy(a, k, block, to, src=None):
            return pltpu.make_async_remote_copy(
                src_ref=blk(a, *block) if src is None else src, dst_ref=blk(a, *block),
                send_sem=send_sems.at[7 * a + k], recv_sem=recv_sems.at[7 * a + k],
                device_id=to, device_id_type=_MESH_ID)

        mine = [pltpu.make_async_copy(x_refs[a], blk(a, *me), local_sems.at[a]) for a in range(n)]
        for cp in mine:
            cp.start()
        first = []
        for a in range(n):
            first.append(copy(a, 0, me, sibling, src=x_refs[a]))
            first += [copy(a, 1 + j, me, (*chip, c), src=x_refs[a]) for j, chip in enumerate(chips)]
        for cp in first:
            cp.start()
        passed = []
        for j, chip in enumerate(chips):
            for a in range(n):
                copy(a, 1 + j, (*chip, c), me).wait_recv()
                fwd = copy(a, 4 + j, (*chip, c), sibling)
                fwd.start()
                passed.append(fwd)
        for a in range(n):
            copy(a, 0, sibling, me).wait_recv()
        for j, chip in enumerate(chips):
            for a in range(n):
                copy(a, 4 + j, (*chip, 1 - c), me).wait_recv()
        for cp in first + passed:
            cp.wait_send()
        for cp in mine:
            cp.wait()

    return pl.pallas_call(
        body, name=name,
        out_shape=[jax.ShapeDtypeStruct((N_DEV,) + s.shape, s.dtype) for s in shards],
        in_specs=[_ANY] * n, out_specs=[_ANY] * n,
        scratch_shapes=[pltpu.SemaphoreType.DMA((7 * n,)), pltpu.SemaphoreType.DMA((7 * n,)),
                        pltpu.SemaphoreType.DMA((n,))],
    )(*shards)


def _exchange_sibling(gs, *, name):
    n = len(gs)

    def body(*refs):
        g_refs, out_refs = refs[:n], refs[n:2 * n]
        send_sems, recv_sems = refs[2 * n:]
        x, y, c = lax.axis_index("x"), lax.axis_index("y"), lax.axis_index("c")
        copies = []
        for a in range(n):
            for q in range(4):
                copies.append(pltpu.make_async_remote_copy(
                    src_ref=g_refs[a].at[2 * q + (1 - c)], dst_ref=out_refs[a].at[q],
                    send_sem=send_sems.at[4 * a + q], recv_sem=recv_sems.at[4 * a + q],
                    device_id=(x, y, 1 - c), device_id_type=_MESH_ID))
        for cp in copies:
            cp.start()
        for cp in copies:
            cp.wait_recv()
        for cp in copies:
            cp.wait_send()

    return pl.pallas_call(
        body, name=name,
        out_shape=[jax.ShapeDtypeStruct((4,) + g.shape[1:], g.dtype) for g in gs],
        in_specs=[_ANY] * n, out_specs=[_ANY] * n,
        scratch_shapes=[pltpu.SemaphoreType.DMA((4 * n,)), pltpu.SemaphoreType.DMA((4 * n,))],
    )(*gs)


def _exchange_chips(hs, *, name):
    n = len(hs)

    def body(*refs):
        h_refs, out_refs = refs[:n], refs[n:2 * n]
        send_sems, recv_sems = refs[2 * n:]
        x, y, c = lax.axis_index("x"), lax.axis_index("y"), lax.axis_index("c")
        chips = [(1 - x, y), (x, 1 - y), (1 - x, 1 - y)]
        copies = []
        for a in range(n):
            for k, (px, py) in enumerate(chips):
                copies.append(pltpu.make_async_remote_copy(
                    src_ref=h_refs[a].at[2 * px + py], dst_ref=out_refs[a].at[k],
                    send_sem=send_sems.at[3 * a + k], recv_sem=recv_sems.at[3 * a + k],
                    device_id=(px, py, c), device_id_type=_MESH_ID))
        for cp in copies:
            cp.start()
        for cp in copies:
            cp.wait_recv()
        for cp in copies:
            cp.wait_send()

    return pl.pallas_call(
        body, name=name,
        out_shape=[jax.ShapeDtypeStruct((3,) + h.shape[1:], h.dtype) for h in hs],
        in_specs=[_ANY] * n, out_specs=[_ANY] * n,
        scratch_shapes=[pltpu.SemaphoreType.DMA((3 * n,)), pltpu.SemaphoreType.DMA((3 * n,))],
    )(*hs)


_BIG = (("ffn1_w_up", 2), ("ffn1_w_down", 1), ("mix_w_in", 2), ("w_uq", 2), ("w_ukv", 2),
        ("w_br_conv", 2), ("w_br_mla", 2), ("w_o", 1), ("ffn2_w_up", 2), ("ffn2_w_down", 1))
_SMALL_SHARDED = (("meta_tokens", 1), ("mix_b_gate", 2), ("conv_w", 2), ("ln_g", 2), ("ln_b", 2))
_SMALL_REPL = ("q_norm_g", "kv_norm_g")


BIG_ROW_ALIGN = 64


def _pack(arrs, dtype, row_align=8):
    flat = jnp.concatenate([a.reshape(-1).astype(dtype) for a in arrs])
    n = flat.shape[0]
    rows = -(-n // (row_align * FLAT_W)) * row_align
    return jnp.pad(flat, (0, rows * FLAT_W - n)).reshape(rows, FLAT_W)


def _unpack(flat, shapes):
    flat = flat.reshape(-1)
    out, off = [], 0
    for s in shapes:
        n = int(np.prod(s))
        out.append(flat[off:off + n].reshape(s))
        off += n
    return out


def _unpack_gathered(gathered, shapes, axes):
    g2 = gathered.reshape(N_DEV, -1)
    out, off = [], 0
    for s, ax in zip(shapes, axes):
        n = int(np.prod(s))
        blocks = g2[:, off:off + n].reshape((N_DEV,) + tuple(s))
        full = jnp.moveaxis(blocks, 0, ax)
        out.append(full.reshape(tuple(s[:ax]) + (N_DEV * s[ax],) + tuple(s[ax + 1:])))
        off += n
    return out


def _to_dest(full, ax):
    s = full.shape
    split = full.reshape(s[:ax] + (N_DEV, s[ax] // N_DEV) + s[ax + 1:])
    return jnp.moveaxis(split, ax, 0)


def _from_blocks(g, ax):
    full = jnp.moveaxis(g, 0, ax)
    s = full.shape
    return full.reshape(s[:ax] + (s[ax] * s[ax + 1],) + s[ax + 2:])


def _rope_tables(T):
    inv_freq = 1.0 / (ROPE_BASE ** (jnp.arange(0, QK_ROPE, 2, dtype=F32) / QK_ROPE))
    ang = jnp.arange(T, dtype=F32)[:, None] * inv_freq[None, :]
    cos, sin = jnp.cos(ang), jnp.sin(ang)
    half = QK_ROPE // 2
    ones = jnp.ones((T, QK_NOPE), F32)
    zeros = lambda w: jnp.zeros((T, w), F32)
    c = jnp.concatenate([ones, cos, cos, zeros(HEAD_PAD - QK_NOPE - QK_ROPE)], axis=1)
    s1 = jnp.concatenate([zeros(QK_NOPE + half), sin, zeros(HEAD_PAD - QK_NOPE - QK_ROPE)], axis=1)
    s2 = jnp.concatenate([zeros(QK_NOPE), -sin, zeros(HEAD_PAD - QK_NOPE - half)], axis=1)
    return c, s1, s2


def _assemble_layer(gathered):
    W = {n: _from_blocks(gathered[n], ax - 1) for n, ax in _BIG if not n.endswith("w_up")}
    return _layer_weights(W), {n: gathered[n] for n in ("ffn1_w_up", "ffn2_w_up")}


def _layer_weights(W):
    w_in = W["mix_w_in"]
    w_in_p = jnp.concatenate([w_in[:, :Z_KR_END], jnp.zeros((D_MODEL, D_IN_PAD - D_IN_REAL), BF16),
                              w_in[:, Z_KR_END:]], axis=1)
    w_uq = W["w_uq"].reshape(Q_LORA, MLA_HEADS, QK_NOPE + QK_ROPE)
    wq = jnp.pad(w_uq, ((0, 0), (0, 0), (0, HEAD_PAD - QK_NOPE - QK_ROPE))).reshape(Q_LORA, D_QK)
    w_ukv = W["w_ukv"].reshape(KV_LORA, MLA_HEADS, QK_NOPE + V_HEAD)
    wk_top = jnp.pad(w_ukv[:, :, :QK_NOPE], ((0, 0), (0, 0), (0, HEAD_PAD - QK_NOPE))).reshape(KV_LORA, D_QK)
    place = np.zeros((Q_LORA - KV_LORA, MLA_HEADS, HEAD_PAD), np.float32)
    for r in range(QK_ROPE):
        place[r, :, QK_NOPE + r] = 1.0
    wk = jnp.concatenate([wk_top, jnp.asarray(place.reshape(Q_LORA - KV_LORA, D_QK), BF16)], axis=0)
    wv = w_ukv[:, :, QK_NOPE:].reshape(KV_LORA, D_MLA)
    wv_ext = jnp.pad(w_ukv[:, :, QK_NOPE:], ((0, 0), (0, 0), (0, HEAD_PAD - V_HEAD))).reshape(KV_LORA, D_QK)
    return dict(
        ffn1_down=W["ffn1_w_down"], ffn2_down=W["ffn2_w_down"],
        w_in=w_in_p, wq=wq, wk=wk, wv=wv, wv_ext=wv_ext,
        wbc=W["w_br_conv"], wbm=W["w_br_mla"], wo=W["w_o"])


def _row8(v):
    return jnp.pad(v, ((0, 8 - v.shape[0]), (0, 0)))


def _local_step(x, tgt, gathered0, w1_blocks, S):
    big_names = [n for n, _ in _BIG]
    t_real = N_META + x.shape[0]
    T = -(-t_real // ROW_ALIGN) * ROW_ALIGN
    pad = T - t_real
    tm = _tile(T, (384, 256, 128))
    tms = _tile(T, (768, 256, 128))
    blk = _tile(T, (768, 256, 128))
    tabs = _rope_tables(T)

    h0 = jnp.concatenate([S["meta_tokens"], x, jnp.zeros((pad, D_MODEL), F32)], axis=0)
    tgt_p = jnp.concatenate([jnp.zeros((N_META, D_MODEL), F32), tgt, jnp.zeros((pad, D_MODEL), F32)], axis=0)

    def ffn_fwd(h, hb, up8, down, g, b, tag):
        gu, a = _ffn_up(hb, up8, name=f"ffn_up_{tag}", tm=tms)
        r, y, yb = _mm_res_ln(a, down, h, g, b, scale=0.5, name=f"ffn_down_ln_{tag}", tm=tm)
        return dict(gu=gu, a=a, r=r), y, yb

    saved = []
    h, hb = h0, h0.astype(BF16)
    gathered = gathered0
    for l in range(DEPTH):
        lw, up = _assemble_layer(gathered)
        sv = dict(lw=lw, up=up, h_in=h, h_in_b=hb)
        sv["f1"], h1, h1b = ffn_fwd(h, hb, up["ffn1_w_up"], lw["ffn1_down"],
                                    S["ln_g"][l, 0:1], S["ln_b"][l, 0:1], f"{l}a")
        z = _mm(h1b, lw["w_in"], out_dtype=F32, name=f"mix_in_{l}", tm=tms, tn=1024, tk=D_MODEL)
        conv_w8 = _row8(S["conv_w"][l])
        bg8 = _row8(S["mix_b_gate"][l])
        yc = _conv_fwd(z, conv_w8, name=f"conv_fwd_{l}", tm=tms)
        gq, gkv = S["q_norm_g"][l:l + 1], S["kv_norm_g"][l:l + 1]
        q, k, v, qn, kin = _qkv_proj(z, gq, gkv, lw["wq"], lw["wk"], lw["wv_ext"], tabs, name=f"qkv_proj_{l}", tm=tm)
        if l == 0:
            o, lse, *nxt = _attn_fwd(q, k, v, name=f"attn_fwd_{l}", blk=blk,
                                     exchange=_Exchange("gather", [w1_blocks[n] for n in big_names]))
            gathered = dict(zip(big_names, nxt))
        else:
            o, lse = _attn_fwd(q, k, v, name=f"attn_fwd_{l}", blk=blk)
        mg, pa, pb = _merge(yc, o, lw["wbc"], lw["wbm"], z, bg8, name=f"merge_{l}", tm=tm)
        r2, h2, h2b = _mm_res_ln(mg, lw["wo"], h1, S["ln_g"][l, 1:2], S["ln_b"][l, 1:2], scale=1.0,
                                 name=f"wo_ln_{l}", tm=tm)
        sv["f2"], h3, h3b = ffn_fwd(h2, h2b, up["ffn2_w_up"], lw["ffn2_down"],
                                    S["ln_g"][l, 2:3], S["ln_b"][l, 2:3], f"{l}b")
        sv.update(h1b=h1b, z=z, conv_w8=conv_w8, bg8=bg8, yc=yc, gq=gq, gkv=gkv, q=q, k=k, v=v, qn=qn, kin=kin,
                  o=o, lse=lse, mg=mg, pa=pa, pb=pb, r2=r2, h2b=h2b)
        saved.append(sv)
        h, hb = h3, h3b

    dh, loss8 = _loss_head(h, tgt_p, t_real=t_real, name="loss_head", tm=tm)

    tk = _tile(T, (2816, 768, 256, 128))
    grads = {n: [None] * DEPTH for n, _ in _BIG}
    for n in ("mix_b_gate", "conv_w", "q_norm_g", "kv_norm_g"):
        grads[n] = [None] * DEPTH
    grads["ln_g"] = [[None] * 3 for _ in range(DEPTH)]
    grads["ln_b"] = [[None] * 3 for _ in range(DEPTH)]

    def ffn_bwd(dy, f, h_in_b, up8, down, g, tag):
        dr, dfb, dg, db = _ln_bwd(dy, f["r"], g, scale=0.5, name=f"ln_bwd_{tag}", tm=tm)
        dgu = _ffn_bwd_mid(dfb, down, f["gu"], name=f"ffn_bwd_mid_{tag}", tm=tms)
        d_down = _mm_tn_call(
            f["a"], dfb,
            pl.BlockSpec((None, tk, FF_BLK), lambda i, j, k: (i, k, 0)),
            pl.BlockSpec((tk, D_MODEL), lambda i, j, k: (k, 0)),
            out_shape=jax.ShapeDtypeStruct((D_FF, D_MODEL), F32),
            out_spec=pl.BlockSpec((FF_BLK, D_MODEL), lambda i, j, k: (i, 0)),
            grid=(FF_HALF_BLOCKS, 1, T // tk), name=f"dw_down_{tag}")
        d_up = _mm_tn_call(
            h_in_b, dgu,
            pl.BlockSpec((tk, D_MODEL), lambda i, j, k: (k, 0)),
            pl.BlockSpec((None, None, tk, FF_BLK),
                         lambda i, j, k: (j // FF_HALF_BLOCKS, j % FF_HALF_BLOCKS, k, 0)),
            out_shape=jax.ShapeDtypeStruct((N_DEV, D_MODEL, FF_BLK), F32),
            out_spec=pl.BlockSpec((None, D_MODEL, FF_BLK), lambda i, j, k: (j, 0, 0)),
            grid=(1, N_DEV, T // tk), name=f"dw_up_{tag}")
        row = pl.BlockSpec((tms, D_MODEL), lambda i, j, k: (i, 0))
        dh_in = _mm_call(
            dgu, up8,
            pl.BlockSpec((None, 2, tms, FF_BLK), lambda i, j, k: (k // 2, k % 2, i, 0)),
            pl.BlockSpec((2, D_MODEL, FF_BLK), lambda i, j, k: (k, 0, 0)),
            out_shape=jax.ShapeDtypeStruct((T, D_MODEL), F32), out_spec=row, acc_shape=(tms, D_MODEL),
            grid=(T // tms, 1, N_DEV // 2), name=f"ffn_dx_{tag}", trans_b=True, res=dr, res_spec=row,
            res_scale=ALPHA, pieces=2)
        return dh_in, d_up, d_down, dg[0], db[0]

    def to_dest(l):
        return {n: grads[n][l] if n.endswith("w_up") else _to_dest(grads[n][l], ax - 1) for n, ax in _BIG}

    received1 = None
    for l in reversed(range(DEPTH)):
        sv = saved[l]
        lw, up = sv["lw"], sv["up"]
        dh, grads["ffn2_w_up"][l], grads["ffn2_w_down"][l], grads["ln_g"][l][2], grads["ln_b"][l][2] = ffn_bwd(
            dh, sv["f2"], sv["h2b"], up["ffn2_w_up"], lw["ffn2_down"], S["ln_g"][l, 2:3], f"{l}b")
        dr2, dmb, dg, db = _ln_bwd(dh, sv["r2"], S["ln_g"][l, 1:2], scale=1.0, name=f"ln_bwd_{l}m", tm=tm)
        grads["ln_g"][l][1], grads["ln_b"][l][1] = dg[0], db[0]
        grads["w_o"][l] = _mm_tn(sv["mg"], dmb, name=f"dw_o_{l}", tm=D_MODEL, tn=D_MODEL, tk=tk)
        dpa, dpb, dgate, dbg = _wo_bwd(dmb, lw["wo"], sv["z"], sv["bg8"], sv["pa"], sv["pb"], name=f"wo_bwd_{l}", tm=tm)
        grads["mix_b_gate"][l] = dbg[0:2]
        grads["w_br_conv"][l] = _mm_tn(sv["yc"], dpa, name=f"dw_br_conv_{l}", tm=D_CONV, tn=D_MODEL, tk=tk)
        grads["w_br_mla"][l] = _mm_tn(sv["o"], dpb, name=f"dw_br_mla_{l}", tm=D_MLA, tn=D_MODEL, tk=tk)
        dyc = _mm(dpa, lw["wbc"], trans_b=True, out_dtype=F32, name=f"d_yconv_{l}", tm=tms, tn=D_CONV, tk=D_MODEL)
        dym = _mm(dpb, lw["wbm"], trans_b=True, out_dtype=BF16, name=f"d_ymla_{l}", tm=tms, tn=D_MLA, tk=D_MODEL)
        dz_conv, dcw = _conv_bwd(dyc, sv["z"], sv["conv_w8"], name=f"conv_bwd_{l}", tm=tms)
        grads["conv_w"][l] = dcw[0:CONV_WIDTH]
        if l == 0:
            dest1 = to_dest(1)
            dq, dk, dv, *got = _attn_bwd(
                sv["q"], sv["k"], sv["v"], sv["o"], dym, sv["lse"], name=f"attn_bwd_{l}", blk=blk,
                exchange=_Exchange("scatter", [dest1[n].astype(BF16) for n in big_names]))
            received1 = dict(zip(big_names, got))
        else:
            dq, dk, dv = _attn_bwd(sv["q"], sv["k"], sv["v"], sv["o"], dym, sv["lse"], name=f"attn_bwd_{l}", blk=blk)
        dz_mid, dqb, dkb, dvb, dgq, dgkv = _qkv_bwd(dq, dk, dv, sv["z"], sv["gq"], sv["gkv"], lw["wq"], lw["wk"],
                                                    lw["wv"], tabs, name=f"qkv_bwd_{l}", tm=tm)
        grads["q_norm_g"][l], grads["kv_norm_g"][l] = dgq[0], dgkv[0]
        d_wq = _mm_tn(sv["qn"], dqb, name=f"dw_uq_{l}", tm=Q_LORA, tn=D_QK, tk=tk)
        d_wk = _mm_tn(sv["kin"], dkb, name=f"dw_uk_{l}", tm=Q_LORA, tn=D_QK, tk=tk)
        d_wv = _mm_tn(sv["kin"], dvb, name=f"dw_uv_{l}", tm=Q_LORA, tn=D_MLA, tk=tk)
        grads["w_uq"][l] = d_wq.reshape(Q_LORA, MLA_HEADS, HEAD_PAD)[:, :, :QK_NOPE + QK_ROPE].reshape(Q_LORA, -1)
        d_kn = d_wk[:KV_LORA].reshape(KV_LORA, MLA_HEADS, HEAD_PAD)[:, :, :QK_NOPE]
        d_vv = d_wv[:KV_LORA].reshape(KV_LORA, MLA_HEADS, V_HEAD)
        grads["w_ukv"][l] = jnp.concatenate([d_kn, d_vv], axis=-1).reshape(KV_LORA, -1)
        dz = jnp.concatenate([dz_conv, dz_mid, dgate], axis=1)
        d_win = _mm_tn(sv["h1b"], dz, name=f"dw_in_{l}", tm=D_MODEL, tn=1024, tk=tk)
        grads["mix_w_in"][l] = jnp.concatenate([d_win[:, :Z_KR_END], d_win[:, Z_KR_END + D_IN_PAD - D_IN_REAL:]], axis=1)
        dh = _mm(dz, lw["w_in"], trans_b=True, out_dtype=F32, name=f"mix_dx_{l}", res=dr2, res_scale=ALPHA,
                 tm=tms, tn=D_MODEL, tk=2048)
        dh, grads["ffn1_w_up"][l], grads["ffn1_w_down"][l], grads["ln_g"][l][0], grads["ln_b"][l][0] = ffn_bwd(
            dh, sv["f1"], sv["h_in_b"], up["ffn1_w_up"], lw["ffn1_down"], S["ln_g"][l, 0:1], f"{l}a")

    small = {n: jnp.stack(grads[n]) for n in ("mix_b_gate", "conv_w", "q_norm_g", "kv_norm_g")}
    small["ln_g"] = jnp.stack([jnp.stack(g) for g in grads["ln_g"]])
    small["ln_b"] = jnp.stack([jnp.stack(g) for g in grads["ln_b"]])
    small["meta_tokens"] = dh[:N_META]
    return loss8, dh[N_META:t_real], to_dest(0), dest1, received1, small


def kernel(x, meta_tokens, ffn1_w_up, ffn1_w_down, mix_w_in, mix_b_gate, conv_w, q_norm_g, w_uq, kv_norm_g, w_ukv, w_br_conv, w_br_mla, w_o, ffn2_w_up, ffn2_w_down, ln_g, ln_b, loss_target, m_meta_tokens, m_ffn1_w_up, m_ffn1_w_down, m_mix_w_in, m_mix_b_gate, m_conv_w, m_q_norm_g, m_w_uq, m_kv_norm_g, m_w_ukv, m_w_br_conv, m_w_br_mla, m_w_o, m_ffn2_w_up, m_ffn2_w_down, m_ln_g, m_ln_b, v_meta_tokens, v_ffn1_w_up, v_ffn1_w_down, v_mix_w_in, v_mix_b_gate, v_conv_w, v_q_norm_g, v_w_uq, v_kv_norm_g, v_w_ukv, v_w_br_conv, v_w_br_mla, v_w_o, v_ffn2_w_up, v_ffn2_w_down, v_ln_g, v_ln_b):
    names = ["meta_tokens", "ffn1_w_up", "ffn1_w_down", "mix_w_in", "mix_b_gate", "conv_w", "q_norm_g", "w_uq",
             "kv_norm_g", "w_ukv", "w_br_conv", "w_br_mla", "w_o", "ffn2_w_up", "ffn2_w_down", "ln_g", "ln_b"]
    w = dict(zip(names, (meta_tokens, ffn1_w_up, ffn1_w_down, mix_w_in, mix_b_gate, conv_w, q_norm_g, w_uq,
                         kv_norm_g, w_ukv, w_br_conv, w_br_mla, w_o, ffn2_w_up, ffn2_w_down, ln_g, ln_b)))
    m = dict(zip(names, (m_meta_tokens, m_ffn1_w_up, m_ffn1_w_down, m_mix_w_in, m_mix_b_gate, m_conv_w, m_q_norm_g,
                         m_w_uq, m_kv_norm_g, m_w_ukv, m_w_br_conv, m_w_br_mla, m_w_o, m_ffn2_w_up, m_ffn2_w_down,
                         m_ln_g, m_ln_b)))
    v = dict(zip(names, (v_meta_tokens, v_ffn1_w_up, v_ffn1_w_down, v_mix_w_in, v_mix_b_gate, v_conv_w, v_q_norm_g,
                         v_w_uq, v_kv_norm_g, v_w_ukv, v_w_br_conv, v_w_br_mla, v_w_o, v_ffn2_w_up, v_ffn2_w_down,
                         v_ln_g, v_ln_b)))
    ix, iy, ic = lax.axis_index("x"), lax.axis_index("y"), lax.axis_index("c")
    dev = 4 * ix + 2 * iy + ic

    big_names = [n for n, _ in _BIG]
    big_axes = [a for _, a in _BIG]
    big_shapes = [w[n].shape for n in big_names]
    small_names = [n for n, _ in _SMALL_SHARDED]
    small_axes = [a for _, a in _SMALL_SHARDED]
    small_shapes = [w[n].shape for n in small_names]
    gathered = _all_gather([w[n][0].astype(BF16) for n in big_names] + [_pack([w[n] for n in small_names], F32)],
                           name="all_gather_weights")
    gathered0 = dict(zip(big_names, gathered[:-1]))
    w1_blocks = {n: w[n][1].astype(BF16) for n in big_names}
    S = dict(zip(small_names, _unpack_gathered(gathered[-1], small_shapes, small_axes)))
    S["q_norm_g"], S["kv_norm_g"] = q_norm_g, kv_norm_g

    loss8, grad_x, dest0, dest1, received1, G = _local_step(x[0], loss_target[0], gathered0, w1_blocks, S)

    from_sibling = _exchange_sibling([dest0[n] for n in big_names], name="rs_sibling")
    c_index = ic.reshape(1).astype(jnp.int32)
    pair = [_pair_sum(dest0[n], r, c_index, name=f"rs_sibling_sum_{n}") for n, r in zip(big_names, from_sibling)]
    from_chips = _exchange_chips([p[1] for p in pair], name="rs_chips")
    my_chip = (2 * ix + iy).reshape(1).astype(jnp.int32)
    my_dev = dev.reshape(1).astype(jnp.int32)
    big_res = [{}, {}, {}, {}]
    for n, p, fc in zip(big_names, pair, from_chips):
        res0 = _adamw(w[n], m[n], v[n], 0, [(p[0], None), (fc, 0), (fc, 1), (fc, 2)], my_chip, name=f"adamw0_{n}")
        res1 = _adamw(w[n], m[n], v[n], 1, [(dest1[n], None)] + [(received1[n], k) for k in range(N_DEV - 1)],
                      my_dev, name=f"adamw1_{n}")
        for kind in range(4):
            big_res[kind][n] = jnp.stack([res0[kind], res1[kind]])

    small_all = small_names + list(_SMALL_REPL)
    part = _pack([G[n] for n in small_all] + [loss8[0, 0:1]], F32)
    full_shapes = [G[n].shape for n in small_all] + [(1,)]
    summed = _sum8(_all_gather([part], name="all_gather_small_grads")[0], name="sum_small_grads")
    unpacked = _unpack(summed, full_shapes)
    loss = unpacked[-1][0]
    g_full = dict(zip(small_all, unpacked[:-1]))
    g_loc = []
    for n in small_all:
        if n in _SMALL_REPL:
            g_loc.append(g_full[n])
        else:
            ax = dict(_SMALL_SHARDED)[n]
            g_loc.append(lax.dynamic_slice_in_dim(g_full[n], dev * w[n].shape[ax], w[n].shape[ax], axis=ax))
    loc_shapes = [w[n].shape for n in small_all]
    g_pack = _pack(g_loc, F32)
    small_out = _adamw(_pack([w[n] for n in small_all], F32)[None], _pack([m[n] for n in small_all], F32)[None],
                       _pack([v[n] for n in small_all], F32)[None], 0, [(g_pack[None], 0)],
                       jnp.zeros((1,), jnp.int32), name="adamw_small")
    small_res = [dict(zip(small_all, _unpack(o, loc_shapes))) for o in small_out]

    outs = [loss, grad_x[None]]
    for kind in range(4):
        for n in names:
            outs.append(big_res[kind][n] if n in big_res[kind] else small_res[kind][n])
    return tuple(outs)
```

```python
import functools

import numpy as np
import jax
import jax.numpy as jnp
from jax import lax
from jax.experimental import pallas as pl
from jax.experimental.pallas import tpu as pltpu

F32 = jnp.float32
BF16 = jnp.bfloat16

D_MODEL = 1024
DEPTH = 2
N_META = 16
D_CONV = 512
CONV_WIDTH = 3
MLA_HEADS = 8
QK_NOPE = 64
QK_ROPE = 32
V_HEAD = 64
Q_LORA = 256
KV_LORA = 128
D_MLA = MLA_HEADS * V_HEAD
ROPE_BASE = 10000.0
NEG_INF = -1e30
D_FF = 2816
ALPHA = (2 * DEPTH) ** 0.25
LN_EPS = 1e-5
RMS_EPS = 1e-6
ATTN_SCALE = (QK_NOPE + QK_ROPE) ** -0.5
ADAM_LR = 0.001
ADAM_B1 = 0.9
ADAM_B2 = 0.999
ADAM_EPS = 1e-08
ADAM_WD = 0.01
ADAM_STEP = 10

N_DEV = 8
HEAD_PAD = 128
HEADS_PER_STEP = 2
FF_BLK = 2 * D_FF // N_DEV
FF_HALF_BLOCKS = N_DEV // 2
D_QK = MLA_HEADS * HEAD_PAD
Z_CONV = 0
Z_MID = 1536
Z_GATE = 2048
D_IN_PAD = 4096
D_IN_REAL = 4000
Z_KR_END = Z_MID + Q_LORA + KV_LORA + QK_ROPE

V7X_VMEM_LIMIT = 56 * 1024 * 1024
LANE = 128
ROW_ALIGN = 256


def _tile(n, cands):
    for c in cands:
        if n % c == 0:
            return c
    raise ValueError(f"no tile for {n} in {cands}")


def _params(sem):
    return pltpu.CompilerParams(dimension_semantics=sem, vmem_limit_bytes=V7X_VMEM_LIMIT)


def _mm_call(a, b, a_spec, b_spec, *, out_shape, out_spec, acc_shape, grid, name, trans_b=False,
             res=None, res_spec=None, res_scale=1.0, pieces=1):
    nk = grid[2]
    has_res = res is not None
    out_dtype = out_shape.dtype
    dims = (((1,), (1,)), ((), ())) if trans_b else (((1,), (0,)), ((), ()))

    def body(*refs):
        if has_res:
            a_ref, b_ref, r_ref, o_ref, acc = refs
        else:
            a_ref, b_ref, o_ref, acc = refs
        k = pl.program_id(2)
        if pieces == 1:
            part = lax.dot_general(a_ref[...], b_ref[...], dims, preferred_element_type=F32)
        else:
            part = lax.dot_general(a_ref[0], b_ref[0], dims, preferred_element_type=F32)
            for p in range(1, pieces):
                part = part + lax.dot_general(a_ref[p], b_ref[p], dims, preferred_element_type=F32)

        @pl.when(k == 0)
        def _():
            acc[...] = part

        @pl.when(k > 0)
        def _():
            acc[...] += part

        @pl.when(k == nk - 1)
        def _():
            out = acc[...]
            if has_res:
                out = out + res_scale * r_ref[...]
            o_ref[...] = out.astype(out_dtype)

    in_specs = [a_spec, b_spec]
    args = [a, b]
    if has_res:
        in_specs.append(res_spec)
        args.append(res)
    return pl.pallas_call(
        body, name=name, grid=grid, in_specs=in_specs, out_specs=out_spec, out_shape=out_shape,
        scratch_shapes=[pltpu.VMEM(acc_shape, F32)],
        compiler_params=_params(("parallel", "parallel", "arbitrary")),
    )(*args)


def _mm(a, b, *, out_dtype, name, trans_b=False, res=None, res_scale=1.0, tm, tn, tk):
    M, K = a.shape
    N = b.shape[0] if trans_b else b.shape[1]
    assert M % tm == 0 and N % tn == 0 and K % tk == 0
    b_spec = (pl.BlockSpec((tn, tk), lambda i, j, k: (j, k)) if trans_b
              else pl.BlockSpec((tk, tn), lambda i, j, k: (k, j)))
    tile = pl.BlockSpec((tm, tn), lambda i, j, k: (i, j))
    return _mm_call(a, b, pl.BlockSpec((tm, tk), lambda i, j, k: (i, k)), b_spec,
                    out_shape=jax.ShapeDtypeStruct((M, N), out_dtype), out_spec=tile, acc_shape=(tm, tn),
                    grid=(M // tm, N // tn, K // tk), name=name, trans_b=trans_b,
                    res=res, res_spec=tile, res_scale=res_scale)


def _mm_tn_call(a, b, a_spec, b_spec, *, out_shape, out_spec, grid, name):
    def body(a_ref, b_ref, o_ref):
        k = pl.program_id(2)
        part = lax.dot_general(a_ref[...], b_ref[...], (((0,), (0,)), ((), ())),
                               preferred_element_type=F32)

        @pl.when(k == 0)
        def _():
            o_ref[...] = part

        @pl.when(k > 0)
        def _():
            o_ref[...] += part

    return pl.pallas_call(
        body, name=name, grid=grid, in_specs=[a_spec, b_spec], out_specs=out_spec, out_shape=out_shape,
        compiler_params=_params(("parallel", "parallel", "arbitrary")),
    )(a, b)


def _mm_tn(a, b, *, name, tm, tn, tk):
    T, M = a.shape
    N = b.shape[1]
    assert M % tm == 0 and N % tn == 0 and T % tk == 0
    return _mm_tn_call(a, b, pl.BlockSpec((tk, tm), lambda i, j, k: (k, i)),
                       pl.BlockSpec((tk, tn), lambda i, j, k: (k, j)),
                       out_shape=jax.ShapeDtypeStruct((M, N), F32),
                       out_spec=pl.BlockSpec((tm, tn), lambda i, j, k: (i, j)),
                       grid=(M // tm, N // tn, T // tk), name=name)


def _ffn_up(hb, w_up8, *, name, tm):
    T = hb.shape[0]

    def body(h_ref, wg_ref, wu_ref, gu_ref, a_ref):
        h = h_ref[...]
        g = jnp.dot(h, wg_ref[...], preferred_element_type=F32)
        u = jnp.dot(h, wu_ref[...], preferred_element_type=F32)
        sg = jax.nn.sigmoid(g)
        silu = g * sg
        gu_ref[0] = (u * (sg * (1.0 + g * (1.0 - sg)))).astype(BF16)
        gu_ref[1] = silu.astype(BF16)
        a_ref[...] = (silu * u).astype(BF16)

    return pl.pallas_call(
        body, name=name, grid=(FF_HALF_BLOCKS, T // tm),
        in_specs=[pl.BlockSpec((tm, D_MODEL), lambda j, i: (i, 0)),
                  pl.BlockSpec((None, D_MODEL, FF_BLK), lambda j, i: (j, 0, 0)),
                  pl.BlockSpec((None, D_MODEL, FF_BLK), lambda j, i: (j + FF_HALF_BLOCKS, 0, 0))],
        out_specs=[pl.BlockSpec((2, None, tm, FF_BLK), lambda j, i: (0, j, i, 0)),
                   pl.BlockSpec((None, tm, FF_BLK), lambda j, i: (j, i, 0))],
        out_shape=[jax.ShapeDtypeStruct((2, FF_HALF_BLOCKS, T, FF_BLK), BF16),
                   jax.ShapeDtypeStruct((FF_HALF_BLOCKS, T, FF_BLK), BF16)],
        compiler_params=_params(("parallel", "parallel")),
    )(hb, w_up8, w_up8)


def _mm_res_ln(a, w, res, g, b, *, scale, name, tm):
    split = a.ndim == 3
    if split:
        S, T, Ks = a.shape
        K = S * Ks
    else:
        T, K = a.shape

    def body(a_ref, w_ref, res_ref, g_ref, b_ref, r_ref, y_ref, yb_ref):
        if split:
            f = jnp.dot(a_ref[0], w_ref[0:Ks, :], preferred_element_type=F32)
            for s in range(1, S):
                f = f + jnp.dot(a_ref[s], w_ref[s * Ks:(s + 1) * Ks, :], preferred_element_type=F32)
        else:
            f = jnp.dot(a_ref[...], w_ref[...], preferred_element_type=F32)
        r = ALPHA * res_ref[...] + scale * f
        mu = jnp.mean(r, axis=-1, keepdims=True)
        xc = r - mu
        var = jnp.mean(xc * xc, axis=-1, keepdims=True)
        y = xc * lax.rsqrt(var + LN_EPS) * g_ref[...] + b_ref[...]
        r_ref[...] = r
        y_ref[...] = y
        yb_ref[...] = y.astype(BF16)

    row = pl.BlockSpec((tm, D_MODEL), lambda i: (i, 0))
    vec = pl.BlockSpec((1, D_MODEL), lambda i: (0, 0))
    return pl.pallas_call(
        body, name=name, grid=(T // tm,),
        in_specs=[pl.BlockSpec((S, tm, Ks), lambda i: (0, i, 0)) if split else pl.BlockSpec((tm, K), lambda i: (i, 0)),
                  pl.BlockSpec((K, D_MODEL), lambda i: (0, 0)), row, vec, vec],
        out_specs=[row, row, row],
        out_shape=[jax.ShapeDtypeStruct((T, D_MODEL), F32), jax.ShapeDtypeStruct((T, D_MODEL), F32),
                   jax.ShapeDtypeStruct((T, D_MODEL), BF16)],
        compiler_params=_params(("parallel",)),
    )(a, w, res, g, b)


def _conv_fwd(z, conv_w8, *, name, tm):
    T = z.shape[0]
    hb = tm // 8

    def body(b_ref, c_ref, h_ref, cp_ref, hp_ref, w_ref, y_ref):
        i = pl.program_id(0)
        u = c_ref[...] * h_ref[...]
        up = jnp.where(i > 0, cp_ref[...] * hp_ref[...], 0.0)
        ue = jnp.concatenate([up, u], axis=0)
        s1 = pltpu.roll(ue, 1, 0)[8:]
        s2 = pltpu.roll(ue, 2, 0)[8:]
        w = w_ref[...]
        conv = w[0:1] * s2 + w[1:2] * s1 + w[2:3] * u
        y_ref[...] = (b_ref[...] * conv).astype(BF16)

    def col(c):
        return pl.BlockSpec((tm, D_CONV), lambda i: (i, c))

    def prev(c):
        return pl.BlockSpec((8, D_CONV), lambda i: (jnp.maximum(i * hb - 1, 0), c))

    return pl.pallas_call(
        body, name=name, grid=(T // tm,),
        in_specs=[col(0), col(1), col(2), prev(1), prev(2), pl.BlockSpec((8, D_CONV), lambda i: (0, 0))],
        out_specs=pl.BlockSpec((tm, D_CONV), lambda i: (i, 0)),
        out_shape=jax.ShapeDtypeStruct((T, D_CONV), BF16),
        compiler_params=_params(("parallel",)),
    )(z, z, z, z, z, conv_w8)


def _rope(x, c, s1, s2):
    n = x.shape[-1]
    return x * c + pltpu.roll(x, 16, 1) * s1 + pltpu.roll(x, n - 16, 1) * s2


def _rope_t(d, c, s1, s2):
    n = d.shape[-1]
    return d * c + pltpu.roll(d * s1, n - 16, 1) + pltpu.roll(d * s2, 16, 1)


def _rms(x, g):
    rstd = lax.rsqrt(jnp.mean(x * x, axis=-1, keepdims=True) + RMS_EPS)
    return x * rstd * g


def _qkv_proj(z, gq, gkv, wq, wk, wv_ext, tabs, *, name, tm):
    T = z.shape[0]

    def body(z_ref, gq_ref, gkv_ref, wq_ref, wk_ref, wv_ref, c_ref, s1_ref, s2_ref,
             q_ref, k_ref, v_ref, qn_ref, kin_ref):
        zz = z_ref[...]
        qn = _rms(zz[:, :Q_LORA], gq_ref[...]).astype(BF16)
        kvn = _rms(zz[:, Q_LORA:Q_LORA + KV_LORA], gkv_ref[...]).astype(BF16)
        kin = jnp.concatenate([kvn, zz[:, Q_LORA + KV_LORA:].astype(BF16)], axis=-1)
        c = jnp.tile(c_ref[...], (1, MLA_HEADS))
        s1 = jnp.tile(s1_ref[...], (1, MLA_HEADS))
        s2 = jnp.tile(s2_ref[...], (1, MLA_HEADS))
        qpre = jnp.dot(qn, wq_ref[...], preferred_element_type=F32)
        kpre = jnp.dot(kin, wk_ref[...], preferred_element_type=F32)
        q_ref[...] = (_rope(qpre, c, s1, s2) * (ATTN_SCALE * LOG2_E)).astype(BF16)
        k_ref[...] = _rope(kpre, c, s1, s2).astype(BF16)
        vv = jnp.dot(kvn, wv_ref[...], preferred_element_type=F32)
        lane = lax.broadcasted_iota(jnp.int32, vv.shape, 1)
        v_ref[...] = jnp.where((lane & (HEAD_PAD - 1)) < V_HEAD, vv, 1.0).astype(BF16)
        qn_ref[...] = qn
        kin_ref[...] = kin

    def full(shape):
        return pl.BlockSpec(shape, lambda i: (0, 0))

    def rows(w, c=0):
        return pl.BlockSpec((tm, w), lambda i: (i, c))

    return pl.pallas_call(
        body, name=name, grid=(T // tm,),
        in_specs=[rows(512, Z_MID // 512), full((1, Q_LORA)), full((1, KV_LORA)),
                  full((Q_LORA, D_QK)), full((Q_LORA, D_QK)), full((KV_LORA, D_QK)),
                  rows(LANE), rows(LANE), rows(LANE)],
        out_specs=[rows(D_QK), rows(D_QK), rows(D_QK), rows(Q_LORA), rows(Q_LORA)],
        out_shape=[jax.ShapeDtypeStruct((T, D_QK), BF16), jax.ShapeDtypeStruct((T, D_QK), BF16),
                   jax.ShapeDtypeStruct((T, D_QK), BF16), jax.ShapeDtypeStruct((T, Q_LORA), BF16),
                   jax.ShapeDtypeStruct((T, Q_LORA), BF16)],
        compiler_params=_params(("parallel",)),
    )(z, gq, gkv, wq, wk, wv_ext, *tabs)


SOFTMAX_ROWS = 32
LOG2_E = 1.4426950408889634
_NT = (((1,), (1,)), ((), ()))
_TN = (((0,), (0,)), ((), ()))


def _diag_mask(s, row0, col0=0):
    row = lax.broadcasted_iota(jnp.int32, s.shape, 0) + row0
    col = lax.broadcasted_iota(jnp.int32, s.shape, 1) + col0
    return jnp.where(col <= row, s, NEG_INF)


_RELATIONS = tuple((rx, ry, rc) for rx in (0, 1) for ry in (0, 1) for rc in (0, 1))[1:]


class _Exchange:
    def __init__(self, kind, arrays):
        assert kind in ("gather", "scatter")
        self.kind, self.arrays, self.n = kind, list(arrays), len(arrays)

    def out_shapes(self):
        if self.kind == "gather":
            return [jax.ShapeDtypeStruct((N_DEV,) + a.shape, a.dtype) for a in self.arrays]
        return [jax.ShapeDtypeStruct((N_DEV - 1,) + a.shape[1:], a.dtype) for a in self.arrays]

    def scratch_shapes(self):
        sems = [pltpu.SemaphoreType.DMA((7 * self.n,)), pltpu.SemaphoreType.DMA((7 * self.n,))]
        if self.kind == "gather":
            sems.append(pltpu.SemaphoreType.DMA((self.n,)))
        return sems

    def _copies(self, src_refs, out_refs, sems):
        x, y, c = lax.axis_index("x"), lax.axis_index("y"), lax.axis_index("c")
        me = 4 * x + 2 * y + c
        sends, recvs, local = [], [], []
        for a in range(self.n):
            for k, rel in enumerate(_RELATIONS):
                peer = tuple((1 - p) if r else p for p, r in zip((x, y, c), rel))
                peer_index = 4 * peer[0] + 2 * peer[1] + peer[2]
                if self.kind == "gather":
                    src, lands_there, lands_here = src_refs[a], out_refs[a].at[me], out_refs[a].at[peer_index]
                else:
                    src, lands_there, lands_here = src_refs[a].at[peer_index], out_refs[a].at[k], out_refs[a].at[k]
                for dst, group in ((lands_there, sends), (lands_here, recvs)):
                    group.append(pltpu.make_async_remote_copy(
                        src_ref=src, dst_ref=dst, send_sem=sems[0].at[7 * a + k], recv_sem=sems[1].at[7 * a + k],
                        device_id=peer, device_id_type=_MESH_ID))
            if self.kind == "gather":
                local.append(pltpu.make_async_copy(src_refs[a], out_refs[a].at[me], sems[2].at[a]))
        return sends, recvs, local

    def start(self, src_refs, out_refs, sems):
        sends, _, local = self._copies(src_refs, out_refs, sems)
        for cp in local + sends:
            cp.start()

    def wait(self, src_refs, out_refs, sems):
        sends, recvs, local = self._copies(src_refs, out_refs, sems)
        for cp in recvs:
            cp.wait_recv()
        for cp in sends:
            cp.wait_send()
        for cp in local:
            cp.wait()


def _exchange_operands(exchange):
    if exchange is None:
        return [], [], [], []
    return exchange.arrays, [_ANY] * exchange.n, exchange.out_shapes(), exchange.scratch_shapes()


def _carry_exchange(body, exchange, n_prefetch, n_in, n_out, grid):
    if exchange is None:
        return body
    n = exchange.n
    last = tuple(g - 1 for g in grid)

    def wrapped(*refs):
        head = refs[:n_prefetch + n_in]
        src_refs = refs[n_prefetch + n_in:n_prefetch + n_in + n]
        rest = refs[n_prefetch + n_in + n:]
        outs, out_refs, rest = rest[:n_out], rest[n_out:n_out + n], rest[n_out + n:]
        n_sems = len(exchange.scratch_shapes())
        scratch, sems = rest[:len(rest) - n_sems], rest[len(rest) - n_sems:]
        at_first = functools.reduce(jnp.logical_and, [pl.program_id(d) == 0 for d in range(len(grid))])
        at_last = functools.reduce(jnp.logical_and, [pl.program_id(d) == last[d] for d in range(len(grid))])

        @pl.when(at_first)
        def _():
            exchange.start(src_refs, out_refs, sems)

        body(*head, *outs, *scratch)

        @pl.when(at_last)
        def _():
            exchange.wait(src_refs, out_refs, sems)

    return wrapped


def _attn_fwd(q, k, v, *, name, blk, exchange=None):
    T = q.shape[0]
    n = T // blk
    hp = HEADS_PER_STEP
    qi = np.array([i for i in range(n) for j in range(i + 1)], np.int32)
    kj = np.array([j for i in range(n) for j in range(i + 1)], np.int32)

    rc = _tile(blk, (SOFTMAX_ROWS,))

    def body(qi_ref, kj_ref, q_ref, k_ref, v_ref, o_ref, lse_ref, m_sc, acc_sc, s_sc, p_sc, red_sc):
        s_id = pl.program_id(1)
        i = qi_ref[s_id]
        j = kj_ref[s_id]

        @pl.when(j == 0)
        def _():
            m_sc[...] = jnp.full(m_sc.shape, NEG_INF, F32)
            acc_sc[...] = jnp.zeros(acc_sc.shape, F32)

        def head_step(hh, diagonal):
            hs = slice(hh * HEAD_PAD, (hh + 1) * HEAD_PAD)
            s_sc[hh] = lax.dot_general(q_ref[:, hs], k_ref[:, hs], _NT, preferred_element_type=F32)
            lanes = [slice(t * LANE, (t + 1) * LANE) for t in range(blk // LANE)]
            for r in range(blk // rc):
                rows = slice(r * rc, (r + 1) * rc)
                s = s_sc[hh, rows, :]
                if diagonal:
                    s = _diag_mask(s, r * rc)
                    s_sc[hh, rows, :] = s
                pm = s[:, lanes[0]]
                for t in lanes[1:]:
                    pm = jnp.maximum(pm, s[:, t])
                red_sc[hh, rows, :] = pm
            m_old = m_sc[hh]
            row_max = jnp.max(red_sc[hh], axis=-1, keepdims=True)
            m_new = jnp.maximum(m_old, jnp.broadcast_to(row_max, (blk, LANE)))
            a = jnp.exp2(m_old - m_new)
            m_sc[hh] = m_new
            for r in range(blk // rc):
                rows = slice(r * rc, (r + 1) * rc)
                mb = m_sc[hh, rows, :]
                for t in lanes:
                    p_sc[hh, rows, t] = jnp.exp2(s_sc[hh, rows, t] - mb).astype(BF16)
            acc_sc[hh] = a * acc_sc[hh] + jnp.dot(p_sc[hh], v_ref[:, hs], preferred_element_type=F32)

        @pl.when(j < i)
        def _():
            for hh in range(hp):
                head_step(hh, False)

        @pl.when(j == i)
        def _():
            for hh in range(hp):
                head_step(hh, True)
            for hh in range(hp):
                acc = acc_sc[hh]
                swapped = pltpu.roll(acc, V_HEAD, 1)
                o_ref[:, hh * V_HEAD:(hh + 1) * V_HEAD] = (acc / swapped)[:, :V_HEAD].astype(BF16)
                lane = lax.broadcasted_iota(jnp.int32, acc.shape, 1)
                denom = jnp.where(lane < V_HEAD, swapped, acc)
                lse_ref[hh] = m_sc[hh] + jnp.log(denom) * LOG2_E

    grid = (MLA_HEADS // hp, len(qi))
    ex_args, ex_specs, ex_out, ex_scratch = _exchange_operands(exchange)
    grid_spec = pltpu.PrefetchScalarGridSpec(
        num_scalar_prefetch=2, grid=grid,
        in_specs=[pl.BlockSpec((blk, hp * HEAD_PAD), lambda g, s, qi, kj: (qi[s], g)),
                  pl.BlockSpec((blk, hp * HEAD_PAD), lambda g, s, qi, kj: (kj[s], g)),
                  pl.BlockSpec((blk, hp * HEAD_PAD), lambda g, s, qi, kj: (kj[s], g))] + ex_specs,
        out_specs=[pl.BlockSpec((blk, hp * V_HEAD), lambda g, s, qi, kj: (qi[s], g)),
                   pl.BlockSpec((hp, blk, LANE), lambda g, s, qi, kj: (g, qi[s], 0))] + ex_specs,
        scratch_shapes=[pltpu.VMEM((hp, blk, LANE), F32), pltpu.VMEM((hp, blk, HEAD_PAD), F32),
                        pltpu.VMEM((hp, blk, blk), F32), pltpu.VMEM((hp, blk, blk), BF16),
                        pltpu.VMEM((hp, blk, LANE), F32)] + ex_scratch)
    return pl.pallas_call(
        _carry_exchange(body, exchange, 2, 3, 2, grid), name=name, grid_spec=grid_spec,
        out_shape=[jax.ShapeDtypeStruct((T, D_MLA), BF16),
                   jax.ShapeDtypeStruct((MLA_HEADS, T, LANE), F32)] + ex_out,
        compiler_params=_params(("arbitrary", "arbitrary") if exchange else ("parallel", "arbitrary")),
    )(jnp.asarray(qi), jnp.asarray(kj), q, k, v, *ex_args)


def _merge(yc, ym, wbc, wbm, z, bg, *, name, tm):
    T = yc.shape[0]

    def body(yc_ref, ym_ref, wbc_ref, wbm_ref, gc_ref, gm_ref, bg_ref, mg_ref, pa_ref, pb_ref):
        pa = jnp.dot(yc_ref[...], wbc_ref[...], preferred_element_type=F32)
        pb = jnp.dot(ym_ref[...], wbm_ref[...], preferred_element_type=F32)
        bgv = bg_ref[...]
        sa = jax.nn.sigmoid(gc_ref[...] + bgv[0:1])
        sb = jax.nn.sigmoid(gm_ref[...] + bgv[1:2])
        mg_ref[...] = (sa * pa + sb * pb).astype(BF16)
        pa_ref[...] = pa.astype(BF16)
        pb_ref[...] = pb.astype(BF16)

    row = pl.BlockSpec((tm, D_MODEL), lambda i: (i, 0))
    return pl.pallas_call(
        body, name=name, grid=(T // tm,),
        in_specs=[pl.BlockSpec((tm, D_CONV), lambda i: (i, 0)), pl.BlockSpec((tm, D_MLA), lambda i: (i, 0)),
                  pl.BlockSpec((D_CONV, D_MODEL), lambda i: (0, 0)), pl.BlockSpec((D_MLA, D_MODEL), lambda i: (0, 0)),
                  pl.BlockSpec((tm, D_MODEL), lambda i: (i, Z_GATE // D_MODEL)),
                  pl.BlockSpec((tm, D_MODEL), lambda i: (i, Z_GATE // D_MODEL + 1)),
                  pl.BlockSpec((8, D_MODEL), lambda i: (0, 0))],
        out_specs=[row, row, row],
        out_shape=[jax.ShapeDtypeStruct((T, D_MODEL), BF16)] * 3,
        compiler_params=_params(("parallel",)),
    )(yc, ym, wbc, wbm, z, z, bg)


def _loss_head(h, tgt, *, t_real, name, tm):
    T = h.shape[0]

    def body(h_ref, t_ref, dy_ref, loss_ref):
        i = pl.program_id(0)
        row = lax.broadcasted_iota(jnp.int32, (tm, 1), 0) + i * tm
        valid = (row >= N_META) & (row < t_real)
        err = jnp.where(valid, h_ref[...] - t_ref[...], 0.0)
        dy_ref[...] = err * (1.0 / D_MODEL)
        part = 0.5 * jnp.sum(jnp.sum(err * err, axis=-1, keepdims=True) * (1.0 / D_MODEL), axis=0, keepdims=True)

        @pl.when(i == 0)
        def _():
            loss_ref[...] = jnp.zeros(loss_ref.shape, F32)

        loss_ref[...] += jnp.broadcast_to(part, loss_ref.shape)

    row_spec = pl.BlockSpec((tm, D_MODEL), lambda i: (i, 0))
    return pl.pallas_call(
        body, name=name, grid=(T // tm,),
        in_specs=[row_spec, row_spec],
        out_specs=[row_spec, pl.BlockSpec((8, LANE), lambda i: (0, 0))],
        out_shape=[jax.ShapeDtypeStruct((T, D_MODEL), F32), jax.ShapeDtypeStruct((8, LANE), F32)],
        compiler_params=_params(("arbitrary",)),
    )(h, tgt)


def _ln_bwd(dy, r, g, *, scale, name, tm):
    T = dy.shape[0]

    def body(dy_ref, r_ref, g_ref, dr_ref, drb_ref, dg_ref, db_ref):
        i = pl.program_id(0)
        rr = r_ref[...]
        dyv = dy_ref[...]
        mu = jnp.mean(rr, axis=-1, keepdims=True)
        xc = rr - mu
        rstd = lax.rsqrt(jnp.mean(xc * xc, axis=-1, keepdims=True) + LN_EPS)
        xh = xc * rstd
        dxh = dyv * g_ref[...]
        m1 = jnp.mean(dxh, axis=-1, keepdims=True)
        m2 = jnp.mean(dxh * xh, axis=-1, keepdims=True)
        dr = rstd * (dxh - m1 - xh * m2)
        dr_ref[...] = dr
        drb_ref[...] = (scale * dr).astype(BF16)

        @pl.when(i == 0)
        def _():
            dg_ref[...] = jnp.zeros(dg_ref.shape, F32)
            db_ref[...] = jnp.zeros(db_ref.shape, F32)

        dg_ref[0:1, :] += jnp.sum(dyv * xh, axis=0, keepdims=True)
        db_ref[0:1, :] += jnp.sum(dyv, axis=0, keepdims=True)

    row = pl.BlockSpec((tm, D_MODEL), lambda i: (i, 0))
    acc = pl.BlockSpec((8, D_MODEL), lambda i: (0, 0))
    return pl.pallas_call(
        body, name=name, grid=(T // tm,),
        in_specs=[row, row, pl.BlockSpec((1, D_MODEL), lambda i: (0, 0))],
        out_specs=[row, row, acc, acc],
        out_shape=[jax.ShapeDtypeStruct((T, D_MODEL), F32), jax.ShapeDtypeStruct((T, D_MODEL), BF16),
                   jax.ShapeDtypeStruct((8, D_MODEL), F32), jax.ShapeDtypeStruct((8, D_MODEL), F32)],
        compiler_params=_params(("arbitrary",)),
    )(dy, r, g)


def _ffn_bwd_mid(dfb, w_down, gu, *, name, tm):
    T = dfb.shape[0]

    def body(df_ref, w_ref, gu_ref, o_ref):
        da = lax.dot_general(df_ref[...], w_ref[...], (((1,), (1,)), ((), ())), preferred_element_type=F32)
        o_ref[0] = (da * gu_ref[0].astype(F32)).astype(BF16)
        o_ref[1] = (da * gu_ref[1].astype(F32)).astype(BF16)

    return pl.pallas_call(
        body, name=name, grid=(FF_HALF_BLOCKS, T // tm),
        in_specs=[pl.BlockSpec((tm, D_MODEL), lambda j, i: (i, 0)),
                  pl.BlockSpec((FF_BLK, D_MODEL), lambda j, i: (j, 0)),
                  pl.BlockSpec((2, None, tm, FF_BLK), lambda j, i: (0, j, i, 0))],
        out_specs=pl.BlockSpec((2, None, tm, FF_BLK), lambda j, i: (0, j, i, 0)),
        out_shape=jax.ShapeDtypeStruct((2, FF_HALF_BLOCKS, T, FF_BLK), BF16),
        compiler_params=_params(("parallel", "parallel")),
    )(dfb, w_down, gu)


def _wo_bwd(dmb, wo, z, bg, pa, pb, *, name, tm):
    T = dmb.shape[0]

    def body(dm_ref, w_ref, gc_ref, gm_ref, bg_ref, pa_ref, pb_ref, dpa_ref, dpb_ref, dg_ref, dbg_ref):
        i = pl.program_id(0)
        dm = lax.dot_general(dm_ref[...], w_ref[...], _NT, preferred_element_type=F32)
        bgv = bg_ref[...]
        sa = jax.nn.sigmoid(gc_ref[...] + bgv[0:1])
        sb = jax.nn.sigmoid(gm_ref[...] + bgv[1:2])
        dpa_ref[...] = (dm * sa).astype(BF16)
        dpb_ref[...] = (dm * sb).astype(BF16)
        dga = dm * pa_ref[...].astype(F32) * (sa * (1.0 - sa))
        dgb = dm * pb_ref[...].astype(F32) * (sb * (1.0 - sb))
        dg_ref[:, :D_MODEL] = dga.astype(BF16)
        dg_ref[:, D_MODEL:] = dgb.astype(BF16)

        @pl.when(i == 0)
        def _():
            dbg_ref[...] = jnp.zeros(dbg_ref.shape, F32)

        dbg_ref[0:1, :] += jnp.sum(dga, axis=0, keepdims=True)
        dbg_ref[1:2, :] += jnp.sum(dgb, axis=0, keepdims=True)

    row = pl.BlockSpec((tm, D_MODEL), lambda i: (i, 0))
    return pl.pallas_call(
        body, name=name, grid=(T // tm,),
        in_specs=[row, pl.BlockSpec((D_MODEL, D_MODEL), lambda i: (0, 0)),
                  pl.BlockSpec((tm, D_MODEL), lambda i: (i, Z_GATE // D_MODEL)),
                  pl.BlockSpec((tm, D_MODEL), lambda i: (i, Z_GATE // D_MODEL + 1)),
                  pl.BlockSpec((8, D_MODEL), lambda i: (0, 0)), row, row],
        out_specs=[row, row, pl.BlockSpec((tm, 2 * D_MODEL), lambda i: (i, 0)),
                   pl.BlockSpec((8, D_MODEL), lambda i: (0, 0))],
        out_shape=[jax.ShapeDtypeStruct((T, D_MODEL), BF16), jax.ShapeDtypeStruct((T, D_MODEL), BF16),
                   jax.ShapeDtypeStruct((T, 2 * D_MODEL), BF16), jax.ShapeDtypeStruct((8, D_MODEL), F32)],
        compiler_params=_params(("arbitrary",)),
    )(dmb, wo, z, z, bg, pa, pb)


def _conv_bwd(dy, z, conv_w8, *, name, tm):
    T = dy.shape[0]
    n = T // tm
    hb = tm // 8

    def body(dy_ref, b_ref, c_ref, h_ref, cp_ref, hp_ref, dyn_ref, bn_ref, w_ref, dz_ref, dw_ref):
        i = pl.program_id(0)
        u = c_ref[...] * h_ref[...]
        up = jnp.where(i > 0, cp_ref[...] * hp_ref[...], 0.0)
        ue = jnp.concatenate([up, u], axis=0)
        s1 = pltpu.roll(ue, 1, 0)[8:]
        s2 = pltpu.roll(ue, 2, 0)[8:]
        w = w_ref[...]
        conv = w[0:1] * s2 + w[1:2] * s1 + w[2:3] * u
        dyv = dy_ref[...]
        e = dyv * b_ref[...]
        en = jnp.where(i < n - 1, dyn_ref[...] * bn_ref[...], 0.0)
        ee = jnp.concatenate([e, en], axis=0)
        e1 = pltpu.roll(ee, tm + 8 - 1, 0)[:tm]
        e2 = pltpu.roll(ee, tm + 8 - 2, 0)[:tm]
        du = w[2:3] * e + w[1:2] * e1 + w[0:1] * e2
        dz_ref[:, 0:D_CONV] = (dyv * conv).astype(BF16)
        dz_ref[:, D_CONV:2 * D_CONV] = (du * h_ref[...]).astype(BF16)
        dz_ref[:, 2 * D_CONV:] = (du * c_ref[...]).astype(BF16)

        @pl.when(i == 0)
        def _():
            dw_ref[...] = jnp.zeros(dw_ref.shape, F32)

        dw_ref[0:1, :] += jnp.sum(e * s2, axis=0, keepdims=True)
        dw_ref[1:2, :] += jnp.sum(e * s1, axis=0, keepdims=True)
        dw_ref[2:3, :] += jnp.sum(e * u, axis=0, keepdims=True)

    def col(c):
        return pl.BlockSpec((tm, D_CONV), lambda i: (i, c))

    def prev(c):
        return pl.BlockSpec((8, D_CONV), lambda i: (jnp.maximum(i * hb - 1, 0), c))

    def nxt(c):
        return pl.BlockSpec((8, D_CONV), lambda i: (jnp.minimum((i + 1) * hb, T // 8 - 1), c))

    return pl.pallas_call(
        body, name=name, grid=(n,),
        in_specs=[col(0), col(0), col(1), col(2), prev(1), prev(2), nxt(0), nxt(0),
                  pl.BlockSpec((8, D_CONV), lambda i: (0, 0))],
        out_specs=[pl.BlockSpec((tm, 3 * D_CONV), lambda i: (i, 0)), pl.BlockSpec((8, D_CONV), lambda i: (0, 0))],
        out_shape=[jax.ShapeDtypeStruct((T, 3 * D_CONV), BF16), jax.ShapeDtypeStruct((8, D_CONV), F32)],
        compiler_params=_params(("arbitrary",)),
    )(dy, z, z, z, z, z, dy, z, conv_w8)


def _attn_bwd(q, k, v, o, do, lse, *, name, blk, exchange=None):
    T = q.shape[0]
    n = T // blk
    hp = HEADS_PER_STEP
    qi = np.array([i for j in range(n) for i in range(j, n)], np.int32)
    kj = np.array([j for j in range(n) for i in range(j, n)], np.int32)

    rc = _tile(blk, (SOFTMAX_ROWS,))

    def body(qi_ref, kj_ref, q_ref, k_ref, v_ref, o_ref, do_ref, lse_ref, dq_ref, dk_ref, dv_ref,
             dk_sc, dv_sc, s_sc, dp_sc, p_sc, ds_sc, delta_sc):
        s_id = pl.program_id(1)
        i = qi_ref[s_id]
        j = kj_ref[s_id]

        @pl.when(s_id == 0)
        def _():
            dq_ref[...] = jnp.zeros(dq_ref.shape, F32)

        @pl.when(i == j)
        def _():
            dk_sc[...] = jnp.zeros(dk_sc.shape, F32)
            dv_sc[...] = jnp.zeros(dv_sc.shape, F32)

        q_rows = pl.ds(pl.multiple_of(i * blk, blk), blk)

        def head_step(hh, diagonal):
            hs = slice(hh * HEAD_PAD, (hh + 1) * HEAD_PAD)
            vs = slice(hh * V_HEAD, (hh + 1) * V_HEAD)
            qh = q_ref[:, hs]
            kh = k_ref[:, hs]
            doh = do_ref[:, vs]
            s_sc[hh] = lax.dot_general(qh, kh, _NT, preferred_element_type=F32)
            vh = v_ref[:, hh * HEAD_PAD:hh * HEAD_PAD + V_HEAD]
            dp_sc[hh] = lax.dot_general(doh, vh, _NT, preferred_element_type=F32)
            delta = jnp.sum(doh.astype(F32) * o_ref[:, vs].astype(F32), axis=-1, keepdims=True)
            delta_sc[hh] = jnp.broadcast_to(delta, (blk, LANE))
            for r in range(blk // rc):
                rows = slice(r * rc, (r + 1) * rc)
                lse = lse_ref[hh, rows, :]
                dl = delta_sc[hh, rows, :]
                for t in range(blk // LANE):
                    cols = slice(t * LANE, (t + 1) * LANE)
                    s = s_sc[hh, rows, cols]
                    if diagonal:
                        s = _diag_mask(s, r * rc, t * LANE)
                    p = jnp.exp2(s - lse)
                    p_sc[hh, rows, cols] = p.astype(BF16)
                    ds_sc[hh, rows, cols] = (p * (dp_sc[hh, rows, cols] - dl)).astype(BF16)
            dv_sc[hh] += lax.dot_general(p_sc[hh], doh, _TN, preferred_element_type=F32)
            dk_sc[hh] += lax.dot_general(ds_sc[hh], qh, _TN, preferred_element_type=F32)
            dq_ref[q_rows, hs] += jnp.dot(ds_sc[hh], kh, preferred_element_type=F32)

        @pl.when(j < i)
        def _():
            for hh in range(hp):
                head_step(hh, False)

        @pl.when(j == i)
        def _():
            for hh in range(hp):
                head_step(hh, True)

        @pl.when(i == n - 1)
        def _():
            for hh in range(hp):
                dk_ref[:, hh * HEAD_PAD:(hh + 1) * HEAD_PAD] = dk_sc[hh] * (1.0 / LOG2_E)
                dv_ref[:, hh * V_HEAD:(hh + 1) * V_HEAD] = dv_sc[hh]

    wq = hp * HEAD_PAD
    wv = hp * V_HEAD
    grid = (MLA_HEADS // hp, len(qi))
    ex_args, ex_specs, ex_out, ex_scratch = _exchange_operands(exchange)
    grid_spec = pltpu.PrefetchScalarGridSpec(
        num_scalar_prefetch=2, grid=grid,
        in_specs=[pl.BlockSpec((blk, wq), lambda g, s, qi, kj: (qi[s], g)),
                  pl.BlockSpec((blk, wq), lambda g, s, qi, kj: (kj[s], g)),
                  pl.BlockSpec((blk, wq), lambda g, s, qi, kj: (kj[s], g)),
                  pl.BlockSpec((blk, wv), lambda g, s, qi, kj: (qi[s], g)),
                  pl.BlockSpec((blk, wv), lambda g, s, qi, kj: (qi[s], g)),
                  pl.BlockSpec((hp, blk, LANE), lambda g, s, qi, kj: (g, qi[s], 0))] + ex_specs,
        out_specs=[pl.BlockSpec((T, wq), lambda g, s, qi, kj: (0, g)),
                   pl.BlockSpec((blk, wq), lambda g, s, qi, kj: (kj[s], g)),
                   pl.BlockSpec((blk, wv), lambda g, s, qi, kj: (kj[s], g))] + ex_specs,
        scratch_shapes=[pltpu.VMEM((hp, blk, HEAD_PAD), F32), pltpu.VMEM((hp, blk, V_HEAD), F32),
                        pltpu.VMEM((hp, blk, blk), F32), pltpu.VMEM((hp, blk, blk), F32),
                        pltpu.VMEM((hp, blk, blk), BF16), pltpu.VMEM((hp, blk, blk), BF16),
                        pltpu.VMEM((hp, blk, LANE), F32)] + ex_scratch)
    return pl.pallas_call(
        _carry_exchange(body, exchange, 2, 6, 3, grid), name=name, grid_spec=grid_spec,
        out_shape=[jax.ShapeDtypeStruct((T, D_QK), F32), jax.ShapeDtypeStruct((T, D_QK), F32),
                   jax.ShapeDtypeStruct((T, D_MLA), F32)] + ex_out,
        compiler_params=_params(("arbitrary", "arbitrary") if exchange else ("parallel", "arbitrary")),
    )(jnp.asarray(qi), jnp.asarray(kj), q, k, v, o, do, lse, *ex_args)


def _qkv_bwd(dq, dk, dv, z, gq, gkv, wq, wk, wv, tabs, *, name, tm):
    T = dq.shape[0]

    def body(dq_ref, dk_ref, dv_ref, z_ref, gq_ref, gkv_ref, wq_ref, wk_ref, wv_ref, c_ref, s1_ref, s2_ref,
             dz_ref, dqb_ref, dkb_ref, dvb_ref, dgq_ref, dgkv_ref):
        i = pl.program_id(0)
        c = jnp.tile(c_ref[...], (1, MLA_HEADS))
        s1 = jnp.tile(s1_ref[...], (1, MLA_HEADS))
        s2 = jnp.tile(s2_ref[...], (1, MLA_HEADS))
        dqp = _rope_t(dq_ref[...] * ATTN_SCALE, c, s1, s2).astype(BF16)
        dkp = _rope_t(dk_ref[...], c, s1, s2).astype(BF16)
        dvb = dv_ref[...].astype(BF16)
        dqb_ref[...] = dqp
        dkb_ref[...] = dkp
        dvb_ref[...] = dvb
        dqn = lax.dot_general(dqp, wq_ref[...], _NT, preferred_element_type=F32)
        dkin = lax.dot_general(dkp, wk_ref[...], _NT, preferred_element_type=F32)
        dkvn = dkin[:, :KV_LORA] + lax.dot_general(dvb, wv_ref[...], _NT, preferred_element_type=F32)
        zz = z_ref[...]

        def rms_bwd(x, g, dy):
            rstd = lax.rsqrt(jnp.mean(x * x, axis=-1, keepdims=True) + RMS_EPS)
            xh = x * rstd
            dxh = dy * g
            dx = rstd * (dxh - xh * jnp.mean(dxh * xh, axis=-1, keepdims=True))
            return dx, jnp.sum(dy * xh, axis=0, keepdims=True)

        dcq, dgq = rms_bwd(zz[:, :Q_LORA], gq_ref[...], dqn)
        dckv, dgkv = rms_bwd(zz[:, Q_LORA:Q_LORA + KV_LORA], gkv_ref[...], dkvn)
        dz_ref[:, :Q_LORA] = dcq.astype(BF16)
        dz_ref[:, Q_LORA:Q_LORA + KV_LORA] = dckv.astype(BF16)
        dz_ref[:, Q_LORA + KV_LORA:] = dkin[:, KV_LORA:].astype(BF16)

        @pl.when(i == 0)
        def _():
            dgq_ref[...] = jnp.zeros(dgq_ref.shape, F32)
            dgkv_ref[...] = jnp.zeros(dgkv_ref.shape, F32)

        dgq_ref[0:1, :] += dgq
        dgkv_ref[0:1, :] += dgkv

    def full(shape):
        return pl.BlockSpec(shape, lambda i: (0, 0))

    def rows(w, c=0):
        return pl.BlockSpec((tm, w), lambda i: (i, c))

    return pl.pallas_call(
        body, name=name, grid=(T // tm,),
        in_specs=[rows(D_QK), rows(D_QK), rows(D_MLA), rows(512, Z_MID // 512),
                  full((1, Q_LORA)), full((1, KV_LORA)),
                  full((Q_LORA, D_QK)), full((Q_LORA, D_QK)), full((KV_LORA, D_MLA)),
                  rows(LANE), rows(LANE), rows(LANE)],
        out_specs=[rows(512), rows(D_QK), rows(D_QK), rows(D_MLA), full((8, Q_LORA)), full((8, KV_LORA))],
        out_shape=[jax.ShapeDtypeStruct((T, 512), BF16), jax.ShapeDtypeStruct((T, D_QK), BF16),
                   jax.ShapeDtypeStruct((T, D_QK), BF16), jax.ShapeDtypeStruct((T, D_MLA), BF16),
                   jax.ShapeDtypeStruct((8, Q_LORA), F32), jax.ShapeDtypeStruct((8, KV_LORA), F32)],
        compiler_params=_params(("arbitrary",)),
    )(dq, dk, dv, z, gq, gkv, wq, wk, wv, *tabs)


FLAT_W = 1024


ELEMENTWISE_TILE_BYTES = 768 * 1024


def _row_tile(R, C):
    width = -(-C // LANE) * LANE * 4
    best = None
    for t in range(16, R + 1, 16):
        if R % t == 0 and t * width <= ELEMENTWISE_TILE_BYTES:
            best = t
    if best is None:
        best = R
    return best


def _adamw(w, m, v, layer, parts, part_index, *, name):
    _, R, C = w.shape
    tr = _row_tile(R, C)
    bc1 = 1.0 - ADAM_B1 ** ADAM_STEP
    bc2 = 1.0 - ADAM_B2 ** ADAM_STEP
    n_parts = len(parts)

    def body(idx_ref, w_ref, m_ref, v_ref, *refs):
        g_refs = refs[:n_parts]
        g_out, d_out, m_out, v_out = refs[n_parts:]
        g = g_refs[0][...].astype(F32)
        for r in g_refs[1:]:
            g = g + r[...].astype(F32)
        wv = w_ref[...]
        mn = ADAM_B1 * m_ref[...] + (1.0 - ADAM_B1) * g
        vn = ADAM_B2 * v_ref[...] + (1.0 - ADAM_B2) * (g * g)
        m_hat = mn / bc1
        v_hat = vn / bc2
        g_out[...] = g
        d_out[...] = -ADAM_LR * (m_hat / (jnp.sqrt(v_hat) + ADAM_EPS) + ADAM_WD * wv)
        m_out[...] = mn
        v_out[...] = vn

    layer_row = pl.BlockSpec((None, tr, C), lambda i, idx: (layer, i, 0))
    in_specs = [layer_row, layer_row, layer_row]
    args = [w, m, v]
    for arr, slot in parts:
        if slot is None:
            in_specs.append(pl.BlockSpec((None, tr, C), lambda i, idx: (idx[0], i, 0)))
        else:
            in_specs.append(pl.BlockSpec((None, tr, C), lambda i, idx, slot=slot: (slot, i, 0)))
        args.append(arr)
    grid_spec = pltpu.PrefetchScalarGridSpec(
        num_scalar_prefetch=1, grid=(R // tr,), in_specs=in_specs,
        out_specs=[pl.BlockSpec((tr, C), lambda i, idx: (i, 0))] * 4)
    return pl.pallas_call(
        body, name=name, grid_spec=grid_spec,
        out_shape=[jax.ShapeDtypeStruct((R, C), F32)] * 4,
        compiler_params=_params(("parallel",)),
    )(part_index, *args)


def _pair_sum(g, recv, c_index, *, name):
    _, R, C = g.shape
    tr = _row_tile(R, C)

    def body(c_ref, g_ref, r_ref, o_ref, ob_ref):
        total = g_ref[...] + r_ref[...]
        o_ref[...] = total
        ob_ref[...] = total.astype(BF16)

    def spec(index):
        return pl.BlockSpec((None, tr, C), index)

    grid_spec = pltpu.PrefetchScalarGridSpec(
        num_scalar_prefetch=1, grid=(4, R // tr),
        in_specs=[spec(lambda q, i, c: (2 * q + c[0], i, 0)), spec(lambda q, i, c: (q, i, 0))],
        out_specs=[spec(lambda q, i, c: (q, i, 0)), spec(lambda q, i, c: (q, i, 0))])
    return pl.pallas_call(
        body, name=name, grid_spec=grid_spec,
        out_shape=[jax.ShapeDtypeStruct((4, R, C), F32), jax.ShapeDtypeStruct((4, R, C), BF16)],
        compiler_params=_params(("parallel", "parallel")),
    )(c_index, g, recv)


def _sum8(parts, *, name):
    _, R, _ = parts.shape

    def body(p_ref, o_ref):
        acc = p_ref[0]
        for d in range(1, N_DEV):
            acc = acc + p_ref[d]
        o_ref[...] = acc

    return pl.pallas_call(
        body, name=name, grid=(1,),
        in_specs=[pl.BlockSpec((N_DEV, R, FLAT_W), lambda i: (0, 0, 0))],
        out_specs=pl.BlockSpec((R, FLAT_W), lambda i: (0, 0)),
        out_shape=jax.ShapeDtypeStruct((R, FLAT_W), F32),
        compiler_params=_params(("arbitrary",)),
    )(parts)


_MESH_ID = pl.DeviceIdType.MESH
_ANY = pl.BlockSpec(memory_space=pl.ANY)


def _all_gather(shards, *, name):
    n = len(shards)

    def body(*refs):
        x_refs, out_refs = refs[:n], refs[n:2 * n]
        send_sems, recv_sems, local_sems = refs[2 * n:]
        x, y, c = lax.axis_index("x"), lax.axis_index("y"), lax.axis_index("c")
        me, sibling = (x, y, c), (x, y, 1 - c)
        chips = [(1 - x, y), (x, 1 - y), (1 - x, 1 - y)]

        def blk(a, px, py, pc):
            return out_refs[a].at[4 * px + 2 * py + pc]

        def copy(a, k, block, to, src=None):
            return pltpu.make_async_remote_copy(
                src_ref=blk(a, *block) if src is None else src, dst_ref=blk(a, *block),
                send_sem=send_sems.at[7 * a + k], recv_sem=recv_sems.at[7 * a + k],
                device_id=to, device_id_type=_MESH_ID)

        mine = [pltpu.make_async_copy(x_refs[a], blk(a, *me), local_sems.at[a]) for a in range(n)]
        for cp in mine:
            cp.start()
        first = []
        for a in range(n):
            first.append(copy(a, 0, me, sibling, src=x_refs[a]))
            first += [copy(a, 1 + j, me, (*chip, c), src=x_refs[a]) for j, chip in enumerate(chips)]
        for cp in first:
            cp.start()
        passed = []
        for j, chip in enumerate(chips):
            for a in range(n):
                copy(a, 1 + j, (*chip, c), me).wait_recv()
                fwd = copy(a, 4 + j, (*chip, c), sibling)
                fwd.start()
                passed.append(fwd)
        for a in range(n):
            copy(a, 0, sibling, me).wait_recv()
        for j, chip in enumerate(chips):
            for a in range(n):
                copy(a, 4 + j, (*chip, 1 - c), me).wait_recv()
        for cp in first + passed:
            cp.wait_send()
        for cp in mine:
            cp.wait()

    return pl.pallas_call(
        body, name=name,
        out_shape=[jax.ShapeDtypeStruct((N_DEV,) + s.shape, s.dtype) for s in shards],
        in_specs=[_ANY] * n, out_specs=[_ANY] * n,
        scratch_shapes=[pltpu.SemaphoreType.DMA((7 * n,)), pltpu.SemaphoreType.DMA((7 * n,)),
                        pltpu.SemaphoreType.DMA((n,))],
    )(*shards)


def _exchange_sibling(gs, *, name):
    n = len(gs)

    def body(*refs):
        g_refs, out_refs = refs[:n], refs[n:2 * n]
        send_sems, recv_sems = refs[2 * n:]
        x, y, c = lax.axis_index("x"), lax.axis_index("y"), lax.axis_index("c")
        copies = []
        for a in range(n):
            for q in range(4):
                copies.append(pltpu.make_async_remote_copy(
                    src_ref=g_refs[a].at[2 * q + (1 - c)], dst_ref=out_refs[a].at[q],
                    send_sem=send_sems.at[4 * a + q], recv_sem=recv_sems.at[4 * a + q],
                    device_id=(x, y, 1 - c), device_id_type=_MESH_ID))
        for cp in copies:
            cp.start()
        for cp in copies:
            cp.wait_recv()
        for cp in copies:
            cp.wait_send()

    return pl.pallas_call(
        body, name=name,
        out_shape=[jax.ShapeDtypeStruct((4,) + g.shape[1:], g.dtype) for g in gs],
        in_specs=[_ANY] * n, out_specs=[_ANY] * n,
        scratch_shapes=[pltpu.SemaphoreType.DMA((4 * n,)), pltpu.SemaphoreType.DMA((4 * n,))],
    )(*gs)


def _exchange_chips(hs, *, name):
    n = len(hs)

    def body(*refs):
        h_refs, out_refs = refs[:n], refs[n:2 * n]
        send_sems, recv_sems = refs[2 * n:]
        x, y, c = lax.axis_index("x"), lax.axis_index("y"), lax.axis_index("c")
        chips = [(1 - x, y), (x, 1 - y), (1 - x, 1 - y)]
        copies = []
        for a in range(n):
            for k, (px, py) in enumerate(chips):
                copies.append(pltpu.make_async_remote_copy(
                    src_ref=h_refs[a].at[2 * px + py], dst_ref=out_refs[a].at[k],
                    send_sem=send_sems.at[3 * a + k], recv_sem=recv_sems.at[3 * a + k],
                    device_id=(px, py, c), device_id_type=_MESH_ID))
        for cp in copies:
            cp.start()
        for cp in copies:
            cp.wait_recv()
        for cp in copies:
            cp.wait_send()

    return pl.pallas_call(
        body, name=name,
        out_shape=[jax.ShapeDtypeStruct((3,) + h.shape[1:], h.dtype) for h in hs],
        in_specs=[_ANY] * n, out_specs=[_ANY] * n,
        scratch_shapes=[pltpu.SemaphoreType.DMA((3 * n,)), pltpu.SemaphoreType.DMA((3 * n,))],
    )(*hs)


_BIG = (("ffn1_w_up", 2), ("ffn1_w_down", 1), ("mix_w_in", 2), ("w_uq", 2), ("w_ukv", 2),
        ("w_br_conv", 2), ("w_br_mla", 2), ("w_o", 1), ("ffn2_w_up", 2), ("ffn2_w_down", 1))
_SMALL_SHARDED = (("meta_tokens", 1), ("mix_b_gate", 2), ("conv_w", 2), ("ln_g", 2), ("ln_b", 2))
_SMALL_REPL = ("q_norm_g", "kv_norm_g")


BIG_ROW_ALIGN = 64


def _pack(arrs, dtype, row_align=8):
    flat = jnp.concatenate([a.reshape(-1).astype(dtype) for a in arrs])
    n = flat.shape[0]
    rows = -(-n // (row_align * FLAT_W)) * row_align
    return jnp.pad(flat, (0, rows * FLAT_W - n)).reshape(rows, FLAT_W)


def _unpack(flat, shapes):
    flat = flat.reshape(-1)
    out, off = [], 0
    for s in shapes:
        n = int(np.prod(s))
        out.append(flat[off:off + n].reshape(s))
        off += n
    return out


def _unpack_gathered(gathered, shapes, axes):
    g2 = gathered.reshape(N_DEV, -1)
    out, off = [], 0
    for s, ax in zip(shapes, axes):
        n = int(np.prod(s))
        blocks = g2[:, off:off + n].reshape((N_DEV,) + tuple(s))
        full = jnp.moveaxis(blocks, 0, ax)
        out.append(full.reshape(tuple(s[:ax]) + (N_DEV * s[ax],) + tuple(s[ax + 1:])))
        off += n
    return out


def _to_dest(full, ax):
    s = full.shape
    split = full.reshape(s[:ax] + (N_DEV, s[ax] // N_DEV) + s[ax + 1:])
    return jnp.moveaxis(split, ax, 0)


def _from_blocks(g, ax):
    full = jnp.moveaxis(g, 0, ax)
    s = full.shape
    return full.reshape(s[:ax] + (s[ax] * s[ax + 1],) + s[ax + 2:])


def _rope_tables(T):
    inv_freq = 1.0 / (ROPE_BASE ** (jnp.arange(0, QK_ROPE, 2, dtype=F32) / QK_ROPE))
    ang = jnp.arange(T, dtype=F32)[:, None] * inv_freq[None, :]
    cos, sin = jnp.cos(ang), jnp.sin(ang)
    half = QK_ROPE // 2
    ones = jnp.ones((T, QK_NOPE), F32)
    zeros = lambda w: jnp.zeros((T, w), F32)
    c = jnp.concatenate([ones, cos, cos, zeros(HEAD_PAD - QK_NOPE - QK_ROPE)], axis=1)
    s1 = jnp.concatenate([zeros(QK_NOPE + half), sin, zeros(HEAD_PAD - QK_NOPE - QK_ROPE)], axis=1)
    s2 = jnp.concatenate([zeros(QK_NOPE), -sin, zeros(HEAD_PAD - QK_NOPE - half)], axis=1)
    return c, s1, s2


def _assemble_layer(gathered):
    W = {n: _from_blocks(gathered[n], ax - 1) for n, ax in _BIG if not n.endswith("w_up")}
    return _layer_weights(W), {n: gathered[n] for n in ("ffn1_w_up", "ffn2_w_up")}


def _layer_weights(W):
    w_in = W["mix_w_in"]
    w_in_p = jnp.concatenate([w_in[:, :Z_KR_END], jnp.zeros((D_MODEL, D_IN_PAD - D_IN_REAL), BF16),
                              w_in[:, Z_KR_END:]], axis=1)
    w_uq = W["w_uq"].reshape(Q_LORA, MLA_HEADS, QK_NOPE + QK_ROPE)
    wq = jnp.pad(w_uq, ((0, 0), (0, 0), (0, HEAD_PAD - QK_NOPE - QK_ROPE))).reshape(Q_LORA, D_QK)
    w_ukv = W["w_ukv"].reshape(KV_LORA, MLA_HEADS, QK_NOPE + V_HEAD)
    wk_top = jnp.pad(w_ukv[:, :, :QK_NOPE], ((0, 0), (0, 0), (0, HEAD_PAD - QK_NOPE))).reshape(KV_LORA, D_QK)
    place = np.zeros((Q_LORA - KV_LORA, MLA_HEADS, HEAD_PAD), np.float32)
    for r in range(QK_ROPE):
        place[r, :, QK_NOPE + r] = 1.0
    wk = jnp.concatenate([wk_top, jnp.asarray(place.reshape(Q_LORA - KV_LORA, D_QK), BF16)], axis=0)
    wv = w_ukv[:, :, QK_NOPE:].reshape(KV_LORA, D_MLA)
    wv_ext = jnp.pad(w_ukv[:, :, QK_NOPE:], ((0, 0), (0, 0), (0, HEAD_PAD - V_HEAD))).reshape(KV_LORA, D_QK)
    return dict(
        ffn1_down=W["ffn1_w_down"], ffn2_down=W["ffn2_w_down"],
        w_in=w_in_p, wq=wq, wk=wk, wv=wv, wv_ext=wv_ext,
        wbc=W["w_br_conv"], wbm=W["w_br_mla"], wo=W["w_o"])


def _row8(v):
    return jnp.pad(v, ((0, 8 - v.shape[0]), (0, 0)))


def _local_step(x, tgt, gathered0, w1_blocks, S):
    big_names = [n for n, _ in _BIG]
    t_real = N_META + x.shape[0]
    T = -(-t_real // ROW_ALIGN) * ROW_ALIGN
    pad = T - t_real
    tm = _tile(T, (384, 256, 128))
    tms = _tile(T, (768, 256, 128))
    blk = _tile(T, (768, 256, 128))
    tabs = _rope_tables(T)

    h0 = jnp.concatenate([S["meta_tokens"], x, jnp.zeros((pad, D_MODEL), F32)], axis=0)
    tgt_p = jnp.concatenate([jnp.zeros((N_META, D_MODEL), F32), tgt, jnp.zeros((pad, D_MODEL), F32)], axis=0)

    def ffn_fwd(h, hb, up8, down, g, b, tag):
        gu, a = _ffn_up(hb, up8, name=f"ffn_up_{tag}", tm=tms)
        r, y, yb = _mm_res_ln(a, down, h, g, b, scale=0.5, name=f"ffn_down_ln_{tag}", tm=tm)
        return dict(gu=gu, a=a, r=r), y, yb

    saved = []
    h, hb = h0, h0.astype(BF16)
    gathered = gathered0
    for l in range(DEPTH):
        lw, up = _assemble_layer(gathered)
        sv = dict(lw=lw, up=up, h_in=h, h_in_b=hb)
        sv["f1"], h1, h1b = ffn_fwd(h, hb, up["ffn1_w_up"], lw["ffn1_down"],
                                    S["ln_g"][l, 0:1], S["ln_b"][l, 0:1], f"{l}a")
        z = _mm(h1b, lw["w_in"], out_dtype=F32, name=f"mix_in_{l}", tm=tms, tn=1024, tk=D_MODEL)
        conv_w8 = _row8(S["conv_w"][l])
        bg8 = _row8(S["mix_b_gate"][l])
        yc = _conv_fwd(z, conv_w8, name=f"conv_fwd_{l}", tm=tms)
        gq, gkv = S["q_norm_g"][l:l + 1], S["kv_norm_g"][l:l + 1]
        q, k, v, qn, kin = _qkv_proj(z, gq, gkv, lw["wq"], lw["wk"], lw["wv_ext"], tabs, name=f"qkv_proj_{l}", tm=tm)
        if l == 0:
            o, lse, *nxt = _attn_fwd(q, k, v, name=f"attn_fwd_{l}", blk=blk,
                                     exchange=_Exchange("gather", [w1_blocks[n] for n in big_names]))
            gathered = dict(zip(big_names, nxt))
        else:
            o, lse = _attn_fwd(q, k, v, name=f"attn_fwd_{l}", blk=blk)
        mg, pa, pb = _merge(yc, o, lw["wbc"], lw["wbm"], z, bg8, name=f"merge_{l}", tm=tm)
        r2, h2, h2b = _mm_res_ln(mg, lw["wo"], h1, S["ln_g"][l, 1:2], S["ln_b"][l, 1:2], scale=1.0,
                                 name=f"wo_ln_{l}", tm=tm)
        sv["f2"], h3, h3b = ffn_fwd(h2, h2b, up["ffn2_w_up"], lw["ffn2_down"],
                                    S["ln_g"][l, 2:3], S["ln_b"][l, 2:3], f"{l}b")
        sv.update(h1b=h1b, z=z, conv_w8=conv_w8, bg8=bg8, yc=yc, gq=gq, gkv=gkv, q=q, k=k, v=v, qn=qn, kin=kin,
                  o=o, lse=lse, mg=mg, pa=pa, pb=pb, r2=r2, h2b=h2b)
        saved.append(sv)
        h, hb = h3, h3b

    dh, loss8 = _loss_head(h, tgt_p, t_real=t_real, name="loss_head", tm=tm)

    tk = _tile(T, (2816, 768, 256, 128))
    grads = {n: [None] * DEPTH for n, _ in _BIG}
    for n in ("mix_b_gate", "conv_w", "q_norm_g", "kv_norm_g"):
        grads[n] = [None] * DEPTH
    grads["ln_g"] = [[None] * 3 for _ in range(DEPTH)]
    grads["ln_b"] = [[None] * 3 for _ in range(DEPTH)]

    def ffn_bwd(dy, f, h_in_b, up8, down, g, tag):
        dr, dfb, dg, db = _ln_bwd(dy, f["r"], g, scale=0.5, name=f"ln_bwd_{tag}", tm=tm)
        dgu = _ffn_bwd_mid(dfb, down, f["gu"], name=f"ffn_bwd_mid_{tag}", tm=tms)
        d_down = _mm_tn_call(
            f["a"], dfb,
            pl.BlockSpec((None, tk, FF_BLK), lambda i, j, k: (i, k, 0)),
            pl.BlockSpec((tk, D_MODEL), lambda i, j, k: (k, 0)),
            out_shape=jax.ShapeDtypeStruct((D_FF, D_MODEL), F32),
            out_spec=pl.BlockSpec((FF_BLK, D_MODEL), lambda i, j, k: (i, 0)),
            grid=(FF_HALF_BLOCKS, 1, T // tk), name=f"dw_down_{tag}")
        d_up = _mm_tn_call(
            h_in_b, dgu,
            pl.BlockSpec((tk, D_MODEL), lambda i, j, k: (k, 0)),
            pl.BlockSpec((None, None, tk, FF_BLK),
                         lambda i, j, k: (j // FF_HALF_BLOCKS, j % FF_HALF_BLOCKS, k, 0)),
            out_shape=jax.ShapeDtypeStruct((N_DEV, D_MODEL, FF_BLK), F32),
            out_spec=pl.BlockSpec((None, D_MODEL, FF_BLK), lambda i, j, k: (j, 0, 0)),
            grid=(1, N_DEV, T // tk), name=f"dw_up_{tag}")
        row = pl.BlockSpec((tms, D_MODEL), lambda i, j, k: (i, 0))
        dh_in = _mm_call(
            dgu, up8,
            pl.BlockSpec((None, FF_HALF_BLOCKS, tms, FF_BLK), lambda i, j, k: (k, 0, i, 0)),
            pl.BlockSpec((FF_HALF_BLOCKS, D_MODEL, FF_BLK), lambda i, j, k: (k, 0, 0)),
            out_shape=jax.ShapeDtypeStruct((T, D_MODEL), F32), out_spec=row, acc_shape=(tms, D_MODEL),
            grid=(T // tms, 1, 2), name=f"ffn_dx_{tag}", trans_b=True, res=dr, res_spec=row,
            res_scale=ALPHA, pieces=FF_HALF_BLOCKS)
        return dh_in, d_up, d_down, dg[0], db[0]

    def to_dest(l):
        return {n: grads[n][l] if n.endswith("w_up") else _to_dest(grads[n][l], ax - 1) for n, ax in _BIG}

    received1 = None
    for l in reversed(range(DEPTH)):
        sv = saved[l]
        lw, up = sv["lw"], sv["up"]
        dh, grads["ffn2_w_up"][l], grads["ffn2_w_down"][l], grads["ln_g"][l][2], grads["ln_b"][l][2] = ffn_bwd(
            dh, sv["f2"], sv["h2b"], up["ffn2_w_up"], lw["ffn2_down"], S["ln_g"][l, 2:3], f"{l}b")
        dr2, dmb, dg, db = _ln_bwd(dh, sv["r2"], S["ln_g"][l, 1:2], scale=1.0, name=f"ln_bwd_{l}m", tm=tm)
        grads["ln_g"][l][1], grads["ln_b"][l][1] = dg[0], db[0]
        grads["w_o"][l] = _mm_tn(sv["mg"], dmb, name=f"dw_o_{l}", tm=D_MODEL, tn=D_MODEL, tk=tk)
        dpa, dpb, dgate, dbg = _wo_bwd(dmb, lw["wo"], sv["z"], sv["bg8"], sv["pa"], sv["pb"], name=f"wo_bwd_{l}", tm=tm)
        grads["mix_b_gate"][l] = dbg[0:2]
        grads["w_br_conv"][l] = _mm_tn(sv["yc"], dpa, name=f"dw_br_conv_{l}", tm=D_CONV, tn=D_MODEL, tk=tk)
        grads["w_br_mla"][l] = _mm_tn(sv["o"], dpb, name=f"dw_br_mla_{l}", tm=D_MLA, tn=D_MODEL, tk=tk)
        dyc = _mm(dpa, lw["wbc"], trans_b=True, out_dtype=F32, name=f"d_yconv_{l}", tm=tms, tn=D_CONV, tk=D_MODEL)
        dym = _mm(dpb, lw["wbm"], trans_b=True, out_dtype=BF16, name=f"d_ymla_{l}", tm=tms, tn=D_MLA, tk=D_MODEL)
        dz_conv, dcw = _conv_bwd(dyc, sv["z"], sv["conv_w8"], name=f"conv_bwd_{l}", tm=tms)
        grads["conv_w"][l] = dcw[0:CONV_WIDTH]
        if l == 0:
            dest1 = to_dest(1)
            dq, dk, dv, *got = _attn_bwd(
                sv["q"], sv["k"], sv["v"], sv["o"], dym, sv["lse"], name=f"attn_bwd_{l}", blk=blk,
                exchange=_Exchange("scatter", [dest1[n].astype(BF16) for n in big_names]))
            received1 = dict(zip(big_names, got))
        else:
            dq, dk, dv = _attn_bwd(sv["q"], sv["k"], sv["v"], sv["o"], dym, sv["lse"], name=f"attn_bwd_{l}", blk=blk)
        dz_mid, dqb, dkb, dvb, dgq, dgkv = _qkv_bwd(dq, dk, dv, sv["z"], sv["gq"], sv["gkv"], lw["wq"], lw["wk"],
                                                    lw["wv"], tabs, name=f"qkv_bwd_{l}", tm=tm)
        grads["q_norm_g"][l], grads["kv_norm_g"][l] = dgq[0], dgkv[0]
        d_wq = _mm_tn(sv["qn"], dqb, name=f"dw_uq_{l}", tm=Q_LORA, tn=D_QK, tk=tk)
        d_wk = _mm_tn(sv["kin"], dkb, name=f"dw_uk_{l}", tm=Q_LORA, tn=D_QK, tk=tk)
        d_wv = _mm_tn(sv["kin"], dvb, name=f"dw_uv_{l}", tm=Q_LORA, tn=D_MLA, tk=tk)
        grads["w_uq"][l] = d_wq.reshape(Q_LORA, MLA_HEADS, HEAD_PAD)[:, :, :QK_NOPE + QK_ROPE].reshape(Q_LORA, -1)
        d_kn = d_wk[:KV_LORA].reshape(KV_LORA, MLA_HEADS, HEAD_PAD)[:, :, :QK_NOPE]
        d_vv = d_wv[:KV_LORA].reshape(KV_LORA, MLA_HEADS, V_HEAD)
        grads["w_ukv"][l] = jnp.concatenate([d_kn, d_vv], axis=-1).reshape(KV_LORA, -1)
        dz = jnp.concatenate([dz_conv, dz_mid, dgate], axis=1)
        d_win = _mm_tn(sv["h1b"], dz, name=f"dw_in_{l}", tm=D_MODEL, tn=1024, tk=tk)
        grads["mix_w_in"][l] = jnp.concatenate([d_win[:, :Z_KR_END], d_win[:, Z_KR_END + D_IN_PAD - D_IN_REAL:]], axis=1)
        dh = _mm(dz, lw["w_in"], trans_b=True, out_dtype=F32, name=f"mix_dx_{l}", res=dr2, res_scale=ALPHA,
                 tm=tms, tn=D_MODEL, tk=2048)
        dh, grads["ffn1_w_up"][l], grads["ffn1_w_down"][l], grads["ln_g"][l][0], grads["ln_b"][l][0] = ffn_bwd(
            dh, sv["f1"], sv["h_in_b"], up["ffn1_w_up"], lw["ffn1_down"], S["ln_g"][l, 0:1], f"{l}a")

    small = {n: jnp.stack(grads[n]) for n in ("mix_b_gate", "conv_w", "q_norm_g", "kv_norm_g")}
    small["ln_g"] = jnp.stack([jnp.stack(g) for g in grads["ln_g"]])
    small["ln_b"] = jnp.stack([jnp.stack(g) for g in grads["ln_b"]])
    small["meta_tokens"] = dh[:N_META]
    return loss8, dh[N_META:t_real], to_dest(0), dest1, received1, small


def kernel(x, meta_tokens, ffn1_w_up, ffn1_w_down, mix_w_in, mix_b_gate, conv_w, q_norm_g, w_uq, kv_norm_g, w_ukv, w_br_conv, w_br_mla, w_o, ffn2_w_up, ffn2_w_down, ln_g, ln_b, loss_target, m_meta_tokens, m_ffn1_w_up, m_ffn1_w_down, m_mix_w_in, m_mix_b_gate, m_conv_w, m_q_norm_g, m_w_uq, m_kv_norm_g, m_w_ukv, m_w_br_conv, m_w_br_mla, m_w_o, m_ffn2_w_up, m_ffn2_w_down, m_ln_g, m_ln_b, v_meta_tokens, v_ffn1_w_up, v_ffn1_w_down, v_mix_w_in, v_mix_b_gate, v_conv_w, v_q_norm_g, v_w_uq, v_kv_norm_g, v_w_ukv, v_w_br_conv, v_w_br_mla, v_w_o, v_ffn2_w_up, v_ffn2_w_down, v_ln_g, v_ln_b):
    names = ["meta_tokens", "ffn1_w_up", "ffn1_w_down", "mix_w_in", "mix_b_gate", "conv_w", "q_norm_g", "w_uq",
             "kv_norm_g", "w_ukv", "w_br_conv", "w_br_mla", "w_o", "ffn2_w_up", "ffn2_w_down", "ln_g", "ln_b"]
    w = dict(zip(names, (meta_tokens, ffn1_w_up, ffn1_w_down, mix_w_in, mix_b_gate, conv_w, q_norm_g, w_uq,
                         kv_norm_g, w_ukv, w_br_conv, w_br_mla, w_o, ffn2_w_up, ffn2_w_down, ln_g, ln_b)))
    m = dict(zip(names, (m_meta_tokens, m_ffn1_w_up, m_ffn1_w_down, m_mix_w_in, m_mix_b_gate, m_conv_w, m_q_norm_g,
                         m_w_uq, m_kv_norm_g, m_w_ukv, m_w_br_conv, m_w_br_mla, m_w_o, m_ffn2_w_up, m_ffn2_w_down,
                         m_ln_g, m_ln_b)))
    v = dict(zip(names, (v_meta_tokens, v_ffn1_w_up, v_ffn1_w_down, v_mix_w_in, v_mix_b_gate, v_conv_w, v_q_norm_g,
                         v_w_uq, v_kv_norm_g, v_w_ukv, v_w_br_conv, v_w_br_mla, v_w_o, v_ffn2_w_up, v_ffn2_w_down,
                         v_ln_g, v_ln_b)))
    ix, iy, ic = lax.axis_index("x"), lax.axis_index("y"), lax.axis_index("c")
    dev = 4 * ix + 2 * iy + ic

    big_names = [n for n, _ in _BIG]
    big_axes = [a for _, a in _BIG]
    big_shapes = [w[n].shape for n in big_names]
    small_names = [n for n, _ in _SMALL_SHARDED]
    small_axes = [a for _, a in _SMALL_SHARDED]
    small_shapes = [w[n].shape for n in small_names]
    gathered = _all_gather([w[n][0].astype(BF16) for n in big_names] + [_pack([w[n] for n in small_names], F32)],
                           name="all_gather_weights")
    gathered0 = dict(zip(big_names, gathered[:-1]))
    w1_blocks = {n: w[n][1].astype(BF16) for n in big_names}
    S = dict(zip(small_names, _unpack_gathered(gathered[-1], small_shapes, small_axes)))
    S["q_norm_g"], S["kv_norm_g"] = q_norm_g, kv_norm_g

    loss8, grad_x, dest0, dest1, received1, G = _local_step(x[0], loss_target[0], gathered0, w1_blocks, S)

    from_sibling = _exchange_sibling([dest0[n] for n in big_names], name="rs_sibling")
    c_index = ic.reshape(1).astype(jnp.int32)
    pair = [_pair_sum(dest0[n], r, c_index, name=f"rs_sibling_sum_{n}") for n, r in zip(big_names, from_sibling)]
    from_chips = _exchange_chips([p[1] for p in pair], name="rs_chips")
    my_chip = (2 * ix + iy).reshape(1).astype(jnp.int32)
    my_dev = dev.reshape(1).astype(jnp.int32)
    big_res = [{}, {}, {}, {}]
    for n, p, fc in zip(big_names, pair, from_chips):
        res0 = _adamw(w[n], m[n], v[n], 0, [(p[0], None), (fc, 0), (fc, 1), (fc, 2)], my_chip, name=f"adamw0_{n}")
        res1 = _adamw(w[n], m[n], v[n], 1, [(dest1[n], None)] + [(received1[n], k) for k in range(N_DEV - 1)],
                      my_dev, name=f"adamw1_{n}")
        for kind in range(4):
            big_res[kind][n] = jnp.stack([res0[kind], res1[kind]])

    small_all = small_names + list(_SMALL_REPL)
    part = _pack([G[n] for n in small_all] + [loss8[0, 0:1]], F32)
    full_shapes = [G[n].shape for n in small_all] + [(1,)]
    summed = _sum8(_all_gather([part], name="all_gather_small_grads")[0], name="sum_small_grads")
    unpacked = _unpack(summed, full_shapes)
    loss = unpacked[-1][0]
    g_full = dict(zip(small_all, unpacked[:-1]))
    g_loc = []
    for n in small_all:
        if n in _SMALL_REPL:
            g_loc.append(g_full[n])
        else:
            ax = dict(_SMALL_SHARDED)[n]
            g_loc.append(lax.dynamic_slice_in_dim(g_full[n], dev * w[n].shape[ax], w[n].shape[ax], axis=ax))
    loc_shapes = [w[n].shape for n in small_all]
    g_pack = _pack(g_loc, F32)
    small_out = _adamw(_pack([w[n] for n in small_all], F32)[None], _pack([m[n] for n in small_all], F32)[None],
                       _pack([v[n] for n in small_all], F32)[None], 0, [(g_pack[None], 0)],
                       jnp.zeros((1,), jnp.int32), name="adamw_small")
    small_res = [dict(zip(small_all, _unpack(o, loc_shapes))) for o in small_out]

    outs = [loss, grad_x[None]]
    for kind in range(4):
        for n in names:
            outs.append(big_res[kind][n] if n in big_res[kind] else small_res[kind][n])
    return tuple(outs)
```

```python
import functools

import numpy as np
import jax
import jax.numpy as jnp
from jax import lax
from jax.experimental import pallas as pl
from jax.experimental.pallas import tpu as pltpu

F32 = jnp.float32
BF16 = jnp.bfloat16

D_MODEL = 1024
DEPTH = 2
N_META = 16
D_CONV = 512
CONV_WIDTH = 3
MLA_HEADS = 8
QK_NOPE = 64
QK_ROPE = 32
V_HEAD = 64
Q_LORA = 256
KV_LORA = 128
D_MLA = MLA_HEADS * V_HEAD
ROPE_BASE = 10000.0
NEG_INF = -1e30
D_FF = 2816
ALPHA = (2 * DEPTH) ** 0.25
LN_EPS = 1e-5
RMS_EPS = 1e-6
ATTN_SCALE = (QK_NOPE + QK_ROPE) ** -0.5
ADAM_LR = 0.001
ADAM_B1 = 0.9
ADAM_B2 = 0.999
ADAM_EPS = 1e-08
ADAM_WD = 0.01
ADAM_STEP = 10

N_DEV = 8
HEAD_PAD = 128
HEADS_PER_STEP = 2
FF_BLK = 2 * D_FF // N_DEV
FF_HALF_BLOCKS = N_DEV // 2
D_QK = MLA_HEADS * HEAD_PAD
Z_CONV = 0
Z_MID = 1536
Z_GATE = 2048
D_IN_PAD = 4096
D_IN_REAL = 4000
Z_KR_END = Z_MID + Q_LORA + KV_LORA + QK_ROPE

V7X_VMEM_LIMIT = 56 * 1024 * 1024
LANE = 128
ROW_ALIGN = 256


def _tile(n, cands):
    for c in cands:
        if n % c == 0:
            return c
    raise ValueError(f"no tile for {n} in {cands}")


def _params(sem):
    return pltpu.CompilerParams(dimension_semantics=sem, vmem_limit_bytes=V7X_VMEM_LIMIT)


def _mm_call(a, b, a_spec, b_spec, *, out_shape, out_spec, acc_shape, grid, name, trans_b=False,
             res=None, res_spec=None, res_scale=1.0, pieces=1):
    nk = grid[2]
    has_res = res is not None
    out_dtype = out_shape.dtype
    dims = (((1,), (1,)), ((), ())) if trans_b else (((1,), (0,)), ((), ()))

    def body(*refs):
        if has_res:
            a_ref, b_ref, r_ref, o_ref, acc = refs
        else:
            a_ref, b_ref, o_ref, acc = refs
        k = pl.program_id(2)
        if pieces == 1:
            part = lax.dot_general(a_ref[...], b_ref[...], dims, preferred_element_type=F32)
        else:
            part = lax.dot_general(a_ref[0], b_ref[0], dims, preferred_element_type=F32)
            for p in range(1, pieces):
                part = part + lax.dot_general(a_ref[p], b_ref[p], dims, preferred_element_type=F32)

        @pl.when(k == 0)
        def _():
            acc[...] = part

        @pl.when(k > 0)
        def _():
            acc[...] += part

        @pl.when(k == nk - 1)
        def _():
            out = acc[...]
            if has_res:
                out = out + res_scale * r_ref[...]
            o_ref[...] = out.astype(out_dtype)

    in_specs = [a_spec, b_spec]
    args = [a, b]
    if has_res:
        in_specs.append(res_spec)
        args.append(res)
    return pl.pallas_call(
        body, name=name, grid=grid, in_specs=in_specs, out_specs=out_spec, out_shape=out_shape,
        scratch_shapes=[pltpu.VMEM(acc_shape, F32)],
        compiler_params=_params(("parallel", "parallel", "arbitrary")),
    )(*args)


def _mm(a, b, *, out_dtype, name, trans_b=False, res=None, res_scale=1.0, tm, tn, tk):
    M, K = a.shape
    N = b.shape[0] if trans_b else b.shape[1]
    assert M % tm == 0 and N % tn == 0 and K % tk == 0
    b_spec = (pl.BlockSpec((tn, tk), lambda i, j, k: (j, k)) if trans_b
              else pl.BlockSpec((tk, tn), lambda i, j, k: (k, j)))
    tile = pl.BlockSpec((tm, tn), lambda i, j, k: (i, j))
    return _mm_call(a, b, pl.BlockSpec((tm, tk), lambda i, j, k: (i, k)), b_spec,
                    out_shape=jax.ShapeDtypeStruct((M, N), out_dtype), out_spec=tile, acc_shape=(tm, tn),
                    grid=(M // tm, N // tn, K // tk), name=name, trans_b=trans_b,
                    res=res, res_spec=tile, res_scale=res_scale)


def _mm_tn_call(a, b, a_spec, b_spec, *, out_shape, out_spec, grid, name):
    def body(a_ref, b_ref, o_ref):
        k = pl.program_id(2)
        part = lax.dot_general(a_ref[...], b_ref[...], (((0,), (0,)), ((), ())),
                               preferred_element_type=F32)

        @pl.when(k == 0)
        def _():
            o_ref[...] = part

        @pl.when(k > 0)
        def _():
            o_ref[...] += part

    return pl.pallas_call(
        body, name=name, grid=grid, in_specs=[a_spec, b_spec], out_specs=out_spec, out_shape=out_shape,
        compiler_params=_params(("parallel", "parallel", "arbitrary")),
    )(a, b)


def _mm_tn(a, b, *, name, tm, tn, tk):
    T, M = a.shape
    N = b.shape[1]
    assert M % tm == 0 and N % tn == 0 and T % tk == 0
    return _mm_tn_call(a, b, pl.BlockSpec((tk, tm), lambda i, j, k: (k, i)),
                       pl.BlockSpec((tk, tn), lambda i, j, k: (k, j)),
                       out_shape=jax.ShapeDtypeStruct((M, N), F32),
                       out_spec=pl.BlockSpec((tm, tn), lambda i, j, k: (i, j)),
                       grid=(M // tm, N // tn, T // tk), name=name)


def _ffn_up(hb, w_up8, *, name, tm):
    T = hb.shape[0]

    def body(h_ref, wg_ref, wu_ref, gu_ref, a_ref):
        h = h_ref[...]
        g = jnp.dot(h, wg_ref[...], preferred_element_type=F32)
        u = jnp.dot(h, wu_ref[...], preferred_element_type=F32)
        sg = jax.nn.sigmoid(g)
        silu = g * sg
        gu_ref[0] = (u * (sg * (1.0 + g * (1.0 - sg)))).astype(BF16)
        gu_ref[1] = silu.astype(BF16)
        a_ref[...] = (silu * u).astype(BF16)

    return pl.pallas_call(
        body, name=name, grid=(FF_HALF_BLOCKS, T // tm),
        in_specs=[pl.BlockSpec((tm, D_MODEL), lambda j, i: (i, 0)),
                  pl.BlockSpec((None, D_MODEL, FF_BLK), lambda j, i: (j, 0, 0)),
                  pl.BlockSpec((None, D_MODEL, FF_BLK), lambda j, i: (j + FF_HALF_BLOCKS, 0, 0))],
        out_specs=[pl.BlockSpec((2, None, tm, FF_BLK), lambda j, i: (0, j, i, 0)),
                   pl.BlockSpec((None, tm, FF_BLK), lambda j, i: (j, i, 0))],
        out_shape=[jax.ShapeDtypeStruct((2, FF_HALF_BLOCKS, T, FF_BLK), BF16),
                   jax.ShapeDtypeStruct((FF_HALF_BLOCKS, T, FF_BLK), BF16)],
        compiler_params=_params(("parallel", "parallel")),
    )(hb, w_up8, w_up8)


def _mm_res_ln(a, w, res, g, b, *, scale, name, tm):
    split = a.ndim == 3
    if split:
        S, T, Ks = a.shape
        K = S * Ks
    else:
        T, K = a.shape

    def body(a_ref, w_ref, res_ref, g_ref, b_ref, r_ref, y_ref, yb_ref):
        if split:
            f = jnp.dot(a_ref[0], w_ref[0:Ks, :], preferred_element_type=F32)
            for s in range(1, S):
                f = f + jnp.dot(a_ref[s], w_ref[s * Ks:(s + 1) * Ks, :], preferred_element_type=F32)
        else:
            f = jnp.dot(a_ref[...], w_ref[...], preferred_element_type=F32)
        r = ALPHA * res_ref[...] + scale * f
        mu = jnp.mean(r, axis=-1, keepdims=True)
        xc = r - mu
        var = jnp.mean(xc * xc, axis=-1, keepdims=True)
        y = xc * lax.rsqrt(var + LN_EPS) * g_ref[...] + b_ref[...]
        r_ref[...] = r
        y_ref[...] = y
        yb_ref[...] = y.astype(BF16)

    row = pl.BlockSpec((tm, D_MODEL), lambda i: (i, 0))
    vec = pl.BlockSpec((1, D_MODEL), lambda i: (0, 0))
    return pl.pallas_call(
        body, name=name, grid=(T // tm,),
        in_specs=[pl.BlockSpec((S, tm, Ks), lambda i: (0, i, 0)) if split else pl.BlockSpec((tm, K), lambda i: (i, 0)),
                  pl.BlockSpec((K, D_MODEL), lambda i: (0, 0)), row, vec, vec],
        out_specs=[row, row, row],
        out_shape=[jax.ShapeDtypeStruct((T, D_MODEL), F32), jax.ShapeDtypeStruct((T, D_MODEL), F32),
                   jax.ShapeDtypeStruct((T, D_MODEL), BF16)],
        compiler_params=_params(("parallel",)),
    )(a, w, res, g, b)


def _conv_fwd(z, conv_w8, *, name, tm):
    T = z.shape[0]
    hb = tm // 8

    def body(b_ref, c_ref, h_ref, cp_ref, hp_ref, w_ref, y_ref):
        i = pl.program_id(0)
        u = c_ref[...] * h_ref[...]
        up = jnp.where(i > 0, cp_ref[...] * hp_ref[...], 0.0)
        ue = jnp.concatenate([up, u], axis=0)
        s1 = pltpu.roll(ue, 1, 0)[8:]
        s2 = pltpu.roll(ue, 2, 0)[8:]
        w = w_ref[...]
        conv = w[0:1] * s2 + w[1:2] * s1 + w[2:3] * u
        y_ref[...] = (b_ref[...] * conv).astype(BF16)

    def col(c):
        return pl.BlockSpec((tm, D_CONV), lambda i: (i, c))

    def prev(c):
        return pl.BlockSpec((8, D_CONV), lambda i: (jnp.maximum(i * hb - 1, 0), c))

    return pl.pallas_call(
        body, name=name, grid=(T // tm,),
        in_specs=[col(0), col(1), col(2), prev(1), prev(2), pl.BlockSpec((8, D_CONV), lambda i: (0, 0))],
        out_specs=pl.BlockSpec((tm, D_CONV), lambda i: (i, 0)),
        out_shape=jax.ShapeDtypeStruct((T, D_CONV), BF16),
        compiler_params=_params(("parallel",)),
    )(z, z, z, z, z, conv_w8)


def _rope(x, c, s1, s2):
    n = x.shape[-1]
    return x * c + pltpu.roll(x, 16, 1) * s1 + pltpu.roll(x, n - 16, 1) * s2


def _rope_t(d, c, s1, s2):
    n = d.shape[-1]
    return d * c + pltpu.roll(d * s1, n - 16, 1) + pltpu.roll(d * s2, 16, 1)


def _rms(x, g):
    rstd = lax.rsqrt(jnp.mean(x * x, axis=-1, keepdims=True) + RMS_EPS)
    return x * rstd * g


def _qkv_proj(z, gq, gkv, wq, wk, wv_ext, tabs, *, name, tm):
    T = z.shape[0]

    def body(z_ref, gq_ref, gkv_ref, wq_ref, wk_ref, wv_ref, c_ref, s1_ref, s2_ref,
             q_ref, k_ref, v_ref, qn_ref, kin_ref):
        zz = z_ref[...]
        qn = _rms(zz[:, :Q_LORA], gq_ref[...]).astype(BF16)
        kvn = _rms(zz[:, Q_LORA:Q_LORA + KV_LORA], gkv_ref[...]).astype(BF16)
        kin = jnp.concatenate([kvn, zz[:, Q_LORA + KV_LORA:].astype(BF16)], axis=-1)
        c = jnp.tile(c_ref[...], (1, MLA_HEADS))
        s1 = jnp.tile(s1_ref[...], (1, MLA_HEADS))
        s2 = jnp.tile(s2_ref[...], (1, MLA_HEADS))
        qpre = jnp.dot(qn, wq_ref[...], preferred_element_type=F32)
        kpre = jnp.dot(kin, wk_ref[...], preferred_element_type=F32)
        q_ref[...] = (_rope(qpre, c, s1, s2) * (ATTN_SCALE * LOG2_E)).astype(BF16)
        k_ref[...] = _rope(kpre, c, s1, s2).astype(BF16)
        vv = jnp.dot(kvn, wv_ref[...], preferred_element_type=F32)
        lane = lax.broadcasted_iota(jnp.int32, vv.shape, 1)
        v_ref[...] = jnp.where((lane & (HEAD_PAD - 1)) < V_HEAD, vv, 1.0).astype(BF16)
        qn_ref[...] = qn
        kin_ref[...] = kin

    def full(shape):
        return pl.BlockSpec(shape, lambda i: (0, 0))

    def rows(w, c=0):
        return pl.BlockSpec((tm, w), lambda i: (i, c))

    return pl.pallas_call(
        body, name=name, grid=(T // tm,),
        in_specs=[rows(512, Z_MID // 512), full((1, Q_LORA)), full((1, KV_LORA)),
                  full((Q_LORA, D_QK)), full((Q_LORA, D_QK)), full((KV_LORA, D_QK)),
                  rows(LANE), rows(LANE), rows(LANE)],
        out_specs=[rows(D_QK), rows(D_QK), rows(D_QK), rows(Q_LORA), rows(Q_LORA)],
        out_shape=[jax.ShapeDtypeStruct((T, D_QK), BF16), jax.ShapeDtypeStruct((T, D_QK), BF16),
                   jax.ShapeDtypeStruct((T, D_QK), BF16), jax.ShapeDtypeStruct((T, Q_LORA), BF16),
                   jax.ShapeDtypeStruct((T, Q_LORA), BF16)],
        compiler_params=_params(("parallel",)),
    )(z, gq, gkv, wq, wk, wv_ext, *tabs)


SOFTMAX_ROWS = 32
LOG2_E = 1.4426950408889634
_NT = (((1,), (1,)), ((), ()))
_TN = (((0,), (0,)), ((), ()))


def _diag_mask(s, row0, col0=0):
    row = lax.broadcasted_iota(jnp.int32, s.shape, 0) + row0
    col = lax.broadcasted_iota(jnp.int32, s.shape, 1) + col0
    return jnp.where(col <= row, s, NEG_INF)


_RELATIONS = tuple((rx, ry, rc) for rx in (0, 1) for ry in (0, 1) for rc in (0, 1))[1:]


class _Exchange:
    def __init__(self, kind, arrays):
        assert kind in ("gather", "scatter")
        self.kind, self.arrays, self.n = kind, list(arrays), len(arrays)

    def out_shapes(self):
        if self.kind == "gather":
            return [jax.ShapeDtypeStruct((N_DEV,) + a.shape, a.dtype) for a in self.arrays]
        return [jax.ShapeDtypeStruct((N_DEV - 1,) + a.shape[1:], a.dtype) for a in self.arrays]

    def scratch_shapes(self):
        sems = [pltpu.SemaphoreType.DMA((7 * self.n,)), pltpu.SemaphoreType.DMA((7 * self.n,))]
        if self.kind == "gather":
            sems.append(pltpu.SemaphoreType.DMA((self.n,)))
        return sems

    def _copies(self, src_refs, out_refs, sems):
        x, y, c = lax.axis_index("x"), lax.axis_index("y"), lax.axis_index("c")
        me = 4 * x + 2 * y + c
        sends, recvs, local = [], [], []
        for a in range(self.n):
            for k, rel in enumerate(_RELATIONS):
                peer = tuple((1 - p) if r else p for p, r in zip((x, y, c), rel))
                peer_index = 4 * peer[0] + 2 * peer[1] + peer[2]
                if self.kind == "gather":
                    src, lands_there, lands_here = src_refs[a], out_refs[a].at[me], out_refs[a].at[peer_index]
                else:
                    src, lands_there, lands_here = src_refs[a].at[peer_index], out_refs[a].at[k], out_refs[a].at[k]
                for dst, group in ((lands_there, sends), (lands_here, recvs)):
                    group.append(pltpu.make_async_remote_copy(
                        src_ref=src, dst_ref=dst, send_sem=sems[0].at[7 * a + k], recv_sem=sems[1].at[7 * a + k],
                        device_id=peer, device_id_type=_MESH_ID))
            if self.kind == "gather":
                local.append(pltpu.make_async_copy(src_refs[a], out_refs[a].at[me], sems[2].at[a]))
        return sends, recvs, local

    def start(self, src_refs, out_refs, sems):
        sends, _, local = self._copies(src_refs, out_refs, sems)
        for cp in local + sends:
            cp.start()

    def wait(self, src_refs, out_refs, sems):
        sends, recvs, local = self._copies(src_refs, out_refs, sems)
        for cp in recvs:
            cp.wait_recv()
        for cp in sends:
            cp.wait_send()
        for cp in local:
            cp.wait()


def _exchange_operands(exchange):
    if exchange is None:
        return [], [], [], []
    return exchange.arrays, [_ANY] * exchange.n, exchange.out_shapes(), exchange.scratch_shapes()


def _carry_exchange(body, exchange, n_prefetch, n_in, n_out, grid):
    if exchange is None:
        return body
    n = exchange.n
    last = tuple(g - 1 for g in grid)

    def wrapped(*refs):
        head = refs[:n_prefetch + n_in]
        src_refs = refs[n_prefetch + n_in:n_prefetch + n_in + n]
        rest = refs[n_prefetch + n_in + n:]
        outs, out_refs, rest = rest[:n_out], rest[n_out:n_out + n], rest[n_out + n:]
        n_sems = len(exchange.scratch_shapes())
        scratch, sems = rest[:len(rest) - n_sems], rest[len(rest) - n_sems:]
        at_first = functools.reduce(jnp.logical_and, [pl.program_id(d) == 0 for d in range(len(grid))])
        at_last = functools.reduce(jnp.logical_and, [pl.program_id(d) == last[d] for d in range(len(grid))])

        @pl.when(at_first)
        def _():
            exchange.start(src_refs, out_refs, sems)

        body(*head, *outs, *scratch)

        @pl.when(at_last)
        def _():
            exchange.wait(src_refs, out_refs, sems)

    return wrapped


def _attn_fwd(q, k, v, *, name, blk, exchange=None):
    T = q.shape[0]
    n = T // blk
    hp = HEADS_PER_STEP
    qi = np.array([i for i in range(n) for j in range(i + 1)], np.int32)
    kj = np.array([j for i in range(n) for j in range(i + 1)], np.int32)

    rc = _tile(blk, (SOFTMAX_ROWS,))

    def body(qi_ref, kj_ref, q_ref, k_ref, v_ref, o_ref, lse_ref, m_sc, acc_sc, s_sc, p_sc, red_sc):
        s_id = pl.program_id(1)
        i = qi_ref[s_id]
        j = kj_ref[s_id]

        @pl.when(j == 0)
        def _():
            m_sc[...] = jnp.full(m_sc.shape, NEG_INF, F32)
            acc_sc[...] = jnp.zeros(acc_sc.shape, F32)

        def head_step(hh, diagonal):
            hs = slice(hh * HEAD_PAD, (hh + 1) * HEAD_PAD)
            s_sc[hh] = lax.dot_general(q_ref[:, hs], k_ref[:, hs], _NT, preferred_element_type=F32)
            lanes = [slice(t * LANE, (t + 1) * LANE) for t in range(blk // LANE)]
            for r in range(blk // rc):
                rows = slice(r * rc, (r + 1) * rc)
                s = s_sc[hh, rows, :]
                if diagonal:
                    s = _diag_mask(s, r * rc)
                    s_sc[hh, rows, :] = s
                pm = s[:, lanes[0]]
                for t in lanes[1:]:
                    pm = jnp.maximum(pm, s[:, t])
                red_sc[hh, rows, :] = pm
            m_old = m_sc[hh]
            row_max = jnp.max(red_sc[hh], axis=-1, keepdims=True)
            m_new = jnp.maximum(m_old, jnp.broadcast_to(row_max, (blk, LANE)))
            a = jnp.exp2(m_old - m_new)
            m_sc[hh] = m_new
            for r in range(blk // rc):
                rows = slice(r * rc, (r + 1) * rc)
                mb = m_sc[hh, rows, :]
                for t in lanes:
                    p_sc[hh, rows, t] = jnp.exp2(s_sc[hh, rows, t] - mb).astype(BF16)
            acc_sc[hh] = a * acc_sc[hh] + jnp.dot(p_sc[hh], v_ref[:, hs], preferred_element_type=F32)

        @pl.when(j < i)
        def _():
            for hh in range(hp):
                head_step(hh, False)

        @pl.when(j == i)
        def _():
            for hh in range(hp):
                head_step(hh, True)
            for hh in range(hp):
                acc = acc_sc[hh]
                swapped = pltpu.roll(acc, V_HEAD, 1)
                o_ref[:, hh * V_HEAD:(hh + 1) * V_HEAD] = (acc / swapped)[:, :V_HEAD].astype(BF16)
                lane = lax.broadcasted_iota(jnp.int32, acc.shape, 1)
                denom = jnp.where(lane < V_HEAD, swapped, acc)
                lse_ref[hh] = m_sc[hh] + jnp.log(denom) * LOG2_E

    grid = (MLA_HEADS // hp, len(qi))
    ex_args, ex_specs, ex_out, ex_scratch = _exchange_operands(exchange)
    grid_spec = pltpu.PrefetchScalarGridSpec(
        num_scalar_prefetch=2, grid=grid,
        in_specs=[pl.BlockSpec((blk, hp * HEAD_PAD), lambda g, s, qi, kj: (qi[s], g)),
                  pl.BlockSpec((blk, hp * HEAD_PAD), lambda g, s, qi, kj: (kj[s], g)),
                  pl.BlockSpec((blk, hp * HEAD_PAD), lambda g, s, qi, kj: (kj[s], g))] + ex_specs,
        out_specs=[pl.BlockSpec((blk, hp * V_HEAD), lambda g, s, qi, kj: (qi[s], g)),
                   pl.BlockSpec((hp, blk, LANE), lambda g, s, qi, kj: (g, qi[s], 0))] + ex_specs,
        scratch_shapes=[pltpu.VMEM((hp, blk, LANE), F32), pltpu.VMEM((hp, blk, HEAD_PAD), F32),
                        pltpu.VMEM((hp, blk, blk), F32), pltpu.VMEM((hp, blk, blk), BF16),
                        pltpu.VMEM((hp, blk, LANE), F32)] + ex_scratch)
    return pl.pallas_call(
        _carry_exchange(body, exchange, 2, 3, 2, grid), name=name, grid_spec=grid_spec,
        out_shape=[jax.ShapeDtypeStruct((T, D_MLA), BF16),
                   jax.ShapeDtypeStruct((MLA_HEADS, T, LANE), F32)] + ex_out,
        compiler_params=_params(("arbitrary", "arbitrary") if exchange else ("parallel", "arbitrary")),
    )(jnp.asarray(qi), jnp.asarray(kj), q, k, v, *ex_args)


def _merge(yc, ym, wbc, wbm, z, bg, *, name, tm):
    T = yc.shape[0]

    def body(yc_ref, ym_ref, wbc_ref, wbm_ref, gc_ref, gm_ref, bg_ref, mg_ref, pa_ref, pb_ref):
        pa = jnp.dot(yc_ref[...], wbc_ref[...], preferred_element_type=F32)
        pb = jnp.dot(ym_ref[...], wbm_ref[...], preferred_element_type=F32)
        bgv = bg_ref[...]
        sa = jax.nn.sigmoid(gc_ref[...] + bgv[0:1])
        sb = jax.nn.sigmoid(gm_ref[...] + bgv[1:2])
        mg_ref[...] = (sa * pa + sb * pb).astype(BF16)
        pa_ref[...] = pa.astype(BF16)
        pb_ref[...] = pb.astype(BF16)

    row = pl.BlockSpec((tm, D_MODEL), lambda i: (i, 0))
    return pl.pallas_call(
        body, name=name, grid=(T // tm,),
        in_specs=[pl.BlockSpec((tm, D_CONV), lambda i: (i, 0)), pl.BlockSpec((tm, D_MLA), lambda i: (i, 0)),
                  pl.BlockSpec((D_CONV, D_MODEL), lambda i: (0, 0)), pl.BlockSpec((D_MLA, D_MODEL), lambda i: (0, 0)),
                  pl.BlockSpec((tm, D_MODEL), lambda i: (i, Z_GATE // D_MODEL)),
                  pl.BlockSpec((tm, D_MODEL), lambda i: (i, Z_GATE // D_MODEL + 1)),
                  pl.BlockSpec((8, D_MODEL), lambda i: (0, 0))],
        out_specs=[row, row, row],
        out_shape=[jax.ShapeDtypeStruct((T, D_MODEL), BF16)] * 3,
        compiler_params=_params(("parallel",)),
    )(yc, ym, wbc, wbm, z, z, bg)


def _loss_head(h, tgt, *, t_real, name, tm):
    T = h.shape[0]

    def body(h_ref, t_ref, dy_ref, loss_ref):
        i = pl.program_id(0)
        row = lax.broadcasted_iota(jnp.int32, (tm, 1), 0) + i * tm
        valid = (row >= N_META) & (row < t_real)
        err = jnp.where(valid, h_ref[...] - t_ref[...], 0.0)
        dy_ref[...] = err * (1.0 / D_MODEL)
        part = 0.5 * jnp.sum(jnp.sum(err * err, axis=-1, keepdims=True) * (1.0 / D_MODEL), axis=0, keepdims=True)

        @pl.when(i == 0)
        def _():
            loss_ref[...] = jnp.zeros(loss_ref.shape, F32)

        loss_ref[...] += jnp.broadcast_to(part, loss_ref.shape)

    row_spec = pl.BlockSpec((tm, D_MODEL), lambda i: (i, 0))
    return pl.pallas_call(
        body, name=name, grid=(T // tm,),
        in_specs=[row_spec, row_spec],
        out_specs=[row_spec, pl.BlockSpec((8, LANE), lambda i: (0, 0))],
        out_shape=[jax.ShapeDtypeStruct((T, D_MODEL), F32), jax.ShapeDtypeStruct((8, LANE), F32)],
        compiler_params=_params(("arbitrary",)),
    )(h, tgt)


def _ln_bwd(dy, r, g, *, scale, name, tm):
    T = dy.shape[0]

    def body(dy_ref, r_ref, g_ref, dr_ref, drb_ref, dg_ref, db_ref):
        i = pl.program_id(0)
        rr = r_ref[...]
        dyv = dy_ref[...]
        mu = jnp.mean(rr, axis=-1, keepdims=True)
        xc = rr - mu
        rstd = lax.rsqrt(jnp.mean(xc * xc, axis=-1, keepdims=True) + LN_EPS)
        xh = xc * rstd
        dxh = dyv * g_ref[...]
        m1 = jnp.mean(dxh, axis=-1, keepdims=True)
        m2 = jnp.mean(dxh * xh, axis=-1, keepdims=True)
        dr = rstd * (dxh - m1 - xh * m2)
        dr_ref[...] = dr
        drb_ref[...] = (scale * dr).astype(BF16)

        @pl.when(i == 0)
        def _():
            dg_ref[...] = jnp.zeros(dg_ref.shape, F32)
            db_ref[...] = jnp.zeros(db_ref.shape, F32)

        dg_ref[0:1, :] += jnp.sum(dyv * xh, axis=0, keepdims=True)
        db_ref[0:1, :] += jnp.sum(dyv, axis=0, keepdims=True)

    row = pl.BlockSpec((tm, D_MODEL), lambda i: (i, 0))
    acc = pl.BlockSpec((8, D_MODEL), lambda i: (0, 0))
    return pl.pallas_call(
        body, name=name, grid=(T // tm,),
        in_specs=[row, row, pl.BlockSpec((1, D_MODEL), lambda i: (0, 0))],
        out_specs=[row, row, acc, acc],
        out_shape=[jax.ShapeDtypeStruct((T, D_MODEL), F32), jax.ShapeDtypeStruct((T, D_MODEL), BF16),
                   jax.ShapeDtypeStruct((8, D_MODEL), F32), jax.ShapeDtypeStruct((8, D_MODEL), F32)],
        compiler_params=_params(("arbitrary",)),
    )(dy, r, g)


def _ffn_bwd_mid(dfb, w_down_t4, gu, *, name, tm):
    T = dfb.shape[0]

    def body(df_ref, w_ref, gu_ref, o_ref):
        da = jnp.dot(df_ref[...], w_ref[...], preferred_element_type=F32)
        o_ref[0] = (da * gu_ref[0].astype(F32)).astype(BF16)
        o_ref[1] = (da * gu_ref[1].astype(F32)).astype(BF16)

    return pl.pallas_call(
        body, name=name, grid=(FF_HALF_BLOCKS, T // tm),
        in_specs=[pl.BlockSpec((tm, D_MODEL), lambda j, i: (i, 0)),
                  pl.BlockSpec((None, D_MODEL, FF_BLK), lambda j, i: (j, 0, 0)),
                  pl.BlockSpec((2, None, tm, FF_BLK), lambda j, i: (0, j, i, 0))],
        out_specs=pl.BlockSpec((2, None, tm, FF_BLK), lambda j, i: (0, j, i, 0)),
        out_shape=jax.ShapeDtypeStruct((2, FF_HALF_BLOCKS, T, FF_BLK), BF16),
        compiler_params=_params(("parallel", "parallel")),
    )(dfb, w_down_t4, gu)


def _wo_bwd(dmb, wo_t, z, bg, pa, pb, *, name, tm):
    T = dmb.shape[0]

    def body(dm_ref, w_ref, gc_ref, gm_ref, bg_ref, pa_ref, pb_ref, dpa_ref, dpb_ref, dg_ref, dbg_ref):
        i = pl.program_id(0)
        dm = jnp.dot(dm_ref[...], w_ref[...], preferred_element_type=F32)
        bgv = bg_ref[...]
        sa = jax.nn.sigmoid(gc_ref[...] + bgv[0:1])
        sb = jax.nn.sigmoid(gm_ref[...] + bgv[1:2])
        dpa_ref[...] = (dm * sa).astype(BF16)
        dpb_ref[...] = (dm * sb).astype(BF16)
        dga = dm * pa_ref[...].astype(F32) * (sa * (1.0 - sa))
        dgb = dm * pb_ref[...].astype(F32) * (sb * (1.0 - sb))
        dg_ref[:, :D_MODEL] = dga.astype(BF16)
        dg_ref[:, D_MODEL:] = dgb.astype(BF16)

        @pl.when(i == 0)
        def _():
            dbg_ref[...] = jnp.zeros(dbg_ref.shape, F32)

        dbg_ref[0:1, :] += jnp.sum(dga, axis=0, keepdims=True)
        dbg_ref[1:2, :] += jnp.sum(dgb, axis=0, keepdims=True)

    row = pl.BlockSpec((tm, D_MODEL), lambda i: (i, 0))
    return pl.pallas_call(
        body, name=name, grid=(T // tm,),
        in_specs=[row, pl.BlockSpec((D_MODEL, D_MODEL), lambda i: (0, 0)),
                  pl.BlockSpec((tm, D_MODEL), lambda i: (i, Z_GATE // D_MODEL)),
                  pl.BlockSpec((tm, D_MODEL), lambda i: (i, Z_GATE // D_MODEL + 1)),
                  pl.BlockSpec((8, D_MODEL), lambda i: (0, 0)), row, row],
        out_specs=[row, row, pl.BlockSpec((tm, 2 * D_MODEL), lambda i: (i, 0)),
                   pl.BlockSpec((8, D_MODEL), lambda i: (0, 0))],
        out_shape=[jax.ShapeDtypeStruct((T, D_MODEL), BF16), jax.ShapeDtypeStruct((T, D_MODEL), BF16),
                   jax.ShapeDtypeStruct((T, 2 * D_MODEL), BF16), jax.ShapeDtypeStruct((8, D_MODEL), F32)],
        compiler_params=_params(("arbitrary",)),
    )(dmb, wo_t, z, z, bg, pa, pb)


def _conv_bwd(dy, z, conv_w8, *, name, tm):
    T = dy.shape[0]
    n = T // tm
    hb = tm // 8

    def body(dy_ref, b_ref, c_ref, h_ref, cp_ref, hp_ref, dyn_ref, bn_ref, w_ref, dz_ref, dw_ref):
        i = pl.program_id(0)
        u = c_ref[...] * h_ref[...]
        up = jnp.where(i > 0, cp_ref[...] * hp_ref[...], 0.0)
        ue = jnp.concatenate([up, u], axis=0)
        s1 = pltpu.roll(ue, 1, 0)[8:]
        s2 = pltpu.roll(ue, 2, 0)[8:]
        w = w_ref[...]
        conv = w[0:1] * s2 + w[1:2] * s1 + w[2:3] * u
        dyv = dy_ref[...]
        e = dyv * b_ref[...]
        en = jnp.where(i < n - 1, dyn_ref[...] * bn_ref[...], 0.0)
        ee = jnp.concatenate([e, en], axis=0)
        e1 = pltpu.roll(ee, tm + 8 - 1, 0)[:tm]
        e2 = pltpu.roll(ee, tm + 8 - 2, 0)[:tm]
        du = w[2:3] * e + w[1:2] * e1 + w[0:1] * e2
        dz_ref[:, 0:D_CONV] = (dyv * conv).astype(BF16)
        dz_ref[:, D_CONV:2 * D_CONV] = (du * h_ref[...]).astype(BF16)
        dz_ref[:, 2 * D_CONV:] = (du * c_ref[...]).astype(BF16)

        @pl.when(i == 0)
        def _():
            dw_ref[...] = jnp.zeros(dw_ref.shape, F32)

        dw_ref[0:1, :] += jnp.sum(e * s2, axis=0, keepdims=True)
        dw_ref[1:2, :] += jnp.sum(e * s1, axis=0, keepdims=True)
        dw_ref[2:3, :] += jnp.sum(e * u, axis=0, keepdims=True)

    def col(c):
        return pl.BlockSpec((tm, D_CONV), lambda i: (i, c))

    def prev(c):
        return pl.BlockSpec((8, D_CONV), lambda i: (jnp.maximum(i * hb - 1, 0), c))

    def nxt(c):
        return pl.BlockSpec((8, D_CONV), lambda i: (jnp.minimum((i + 1) * hb, T // 8 - 1), c))

    return pl.pallas_call(
        body, name=name, grid=(n,),
        in_specs=[col(0), col(0), col(1), col(2), prev(1), prev(2), nxt(0), nxt(0),
                  pl.BlockSpec((8, D_CONV), lambda i: (0, 0))],
        out_specs=[pl.BlockSpec((tm, 3 * D_CONV), lambda i: (i, 0)), pl.BlockSpec((8, D_CONV), lambda i: (0, 0))],
        out_shape=[jax.ShapeDtypeStruct((T, 3 * D_CONV), BF16), jax.ShapeDtypeStruct((8, D_CONV), F32)],
        compiler_params=_params(("arbitrary",)),
    )(dy, z, z, z, z, z, dy, z, conv_w8)


def _attn_bwd(q, k, v, o, do, lse, *, name, blk, exchange=None):
    T = q.shape[0]
    n = T // blk
    hp = HEADS_PER_STEP
    qi = np.array([i for j in range(n) for i in range(j, n)], np.int32)
    kj = np.array([j for j in range(n) for i in range(j, n)], np.int32)

    rc = _tile(blk, (SOFTMAX_ROWS,))

    def body(qi_ref, kj_ref, q_ref, k_ref, v_ref, o_ref, do_ref, lse_ref, dq_ref, dk_ref, dv_ref,
             dk_sc, dv_sc, s_sc, dp_sc, p_sc, ds_sc, delta_sc):
        s_id = pl.program_id(1)
        i = qi_ref[s_id]
        j = kj_ref[s_id]

        @pl.when(s_id == 0)
        def _():
            dq_ref[...] = jnp.zeros(dq_ref.shape, F32)

        @pl.when(i == j)
        def _():
            dk_sc[...] = jnp.zeros(dk_sc.shape, F32)
            dv_sc[...] = jnp.zeros(dv_sc.shape, F32)

        q_rows = pl.ds(pl.multiple_of(i * blk, blk), blk)

        def head_step(hh, diagonal):
            hs = slice(hh * HEAD_PAD, (hh + 1) * HEAD_PAD)
            vs = slice(hh * V_HEAD, (hh + 1) * V_HEAD)
            qh = q_ref[:, hs]
            kh = k_ref[:, hs]
            doh = do_ref[:, vs]
            s_sc[hh] = lax.dot_general(qh, kh, _NT, preferred_element_type=F32)
            vh = v_ref[:, hh * HEAD_PAD:hh * HEAD_PAD + V_HEAD]
            dp_sc[hh] = lax.dot_general(doh, vh, _NT, preferred_element_type=F32)
            delta = jnp.sum(doh.astype(F32) * o_ref[:, vs].astype(F32), axis=-1, keepdims=True)
            delta_sc[hh] = jnp.broadcast_to(delta, (blk, LANE))
            for r in range(blk // rc):
                rows = slice(r * rc, (r + 1) * rc)
                lse = lse_ref[hh, rows, :]
                dl = delta_sc[hh, rows, :]
                for t in range(blk // LANE):
                    cols = slice(t * LANE, (t + 1) * LANE)
                    s = s_sc[hh, rows, cols]
                    if diagonal:
                        s = _diag_mask(s, r * rc, t * LANE)
                    p = jnp.exp2(s - lse)
                    p_sc[hh, rows, cols] = p.astype(BF16)
                    ds_sc[hh, rows, cols] = (p * (dp_sc[hh, rows, cols] - dl)).astype(BF16)
            dv_sc[hh] += lax.dot_general(p_sc[hh], doh, _TN, preferred_element_type=F32)
            dk_sc[hh] += lax.dot_general(ds_sc[hh], qh, _TN, preferred_element_type=F32)
            dq_ref[q_rows, hs] += jnp.dot(ds_sc[hh], kh, preferred_element_type=F32)

        @pl.when(j < i)
        def _():
            for hh in range(hp):
                head_step(hh, False)

        @pl.when(j == i)
        def _():
            for hh in range(hp):
                head_step(hh, True)

        @pl.when(i == n - 1)
        def _():
            for hh in range(hp):
                dk_ref[:, hh * HEAD_PAD:(hh + 1) * HEAD_PAD] = dk_sc[hh] * (1.0 / LOG2_E)
                dv_ref[:, hh * V_HEAD:(hh + 1) * V_HEAD] = dv_sc[hh]

    wq = hp * HEAD_PAD
    wv = hp * V_HEAD
    grid = (MLA_HEADS // hp, len(qi))
    ex_args, ex_specs, ex_out, ex_scratch = _exchange_operands(exchange)
    grid_spec = pltpu.PrefetchScalarGridSpec(
        num_scalar_prefetch=2, grid=grid,
        in_specs=[pl.BlockSpec((blk, wq), lambda g, s, qi, kj: (qi[s], g)),
                  pl.BlockSpec((blk, wq), lambda g, s, qi, kj: (kj[s], g)),
                  pl.BlockSpec((blk, wq), lambda g, s, qi, kj: (kj[s], g)),
                  pl.BlockSpec((blk, wv), lambda g, s, qi, kj: (qi[s], g)),
                  pl.BlockSpec((blk, wv), lambda g, s, qi, kj: (qi[s], g)),
                  pl.BlockSpec((hp, blk, LANE), lambda g, s, qi, kj: (g, qi[s], 0))] + ex_specs,
        out_specs=[pl.BlockSpec((T, wq), lambda g, s, qi, kj: (0, g)),
                   pl.BlockSpec((blk, wq), lambda g, s, qi, kj: (kj[s], g)),
                   pl.BlockSpec((blk, wv), lambda g, s, qi, kj: (kj[s], g))] + ex_specs,
        scratch_shapes=[pltpu.VMEM((hp, blk, HEAD_PAD), F32), pltpu.VMEM((hp, blk, V_HEAD), F32),
                        pltpu.VMEM((hp, blk, blk), F32), pltpu.VMEM((hp, blk, blk), F32),
                        pltpu.VMEM((hp, blk, blk), BF16), pltpu.VMEM((hp, blk, blk), BF16),
                        pltpu.VMEM((hp, blk, LANE), F32)] + ex_scratch)
    return pl.pallas_call(
        _carry_exchange(body, exchange, 2, 6, 3, grid), name=name, grid_spec=grid_spec,
        out_shape=[jax.ShapeDtypeStruct((T, D_QK), F32), jax.ShapeDtypeStruct((T, D_QK), F32),
                   jax.ShapeDtypeStruct((T, D_MLA), F32)] + ex_out,
        compiler_params=_params(("arbitrary", "arbitrary") if exchange else ("parallel", "arbitrary")),
    )(jnp.asarray(qi), jnp.asarray(kj), q, k, v, o, do, lse, *ex_args)


def _qkv_bwd(dq, dk, dv, z, gq, gkv, wq_t, wk_t, wv_t, tabs, *, name, tm):
    T = dq.shape[0]

    def body(dq_ref, dk_ref, dv_ref, z_ref, gq_ref, gkv_ref, wq_ref, wk_ref, wv_ref, c_ref, s1_ref, s2_ref,
             dz_ref, dqb_ref, dkb_ref, dvb_ref, dgq_ref, dgkv_ref):
        i = pl.program_id(0)
        c = jnp.tile(c_ref[...], (1, MLA_HEADS))
        s1 = jnp.tile(s1_ref[...], (1, MLA_HEADS))
        s2 = jnp.tile(s2_ref[...], (1, MLA_HEADS))
        dqp = _rope_t(dq_ref[...] * ATTN_SCALE, c, s1, s2).astype(BF16)
        dkp = _rope_t(dk_ref[...], c, s1, s2).astype(BF16)
        dvb = dv_ref[...].astype(BF16)
        dqb_ref[...] = dqp
        dkb_ref[...] = dkp
        dvb_ref[...] = dvb
        dqn = jnp.dot(dqp, wq_ref[...], preferred_element_type=F32)
        dkin = jnp.dot(dkp, wk_ref[...], preferred_element_type=F32)
        dkvn = dkin[:, :KV_LORA] + jnp.dot(dvb, wv_ref[...], preferred_element_type=F32)
        zz = z_ref[...]

        def rms_bwd(x, g, dy):
            rstd = lax.rsqrt(jnp.mean(x * x, axis=-1, keepdims=True) + RMS_EPS)
            xh = x * rstd
            dxh = dy * g
            dx = rstd * (dxh - xh * jnp.mean(dxh * xh, axis=-1, keepdims=True))
            return dx, jnp.sum(dy * xh, axis=0, keepdims=True)

        dcq, dgq = rms_bwd(zz[:, :Q_LORA], gq_ref[...], dqn)
        dckv, dgkv = rms_bwd(zz[:, Q_LORA:Q_LORA + KV_LORA], gkv_ref[...], dkvn)
        dz_ref[:, :Q_LORA] = dcq.astype(BF16)
        dz_ref[:, Q_LORA:Q_LORA + KV_LORA] = dckv.astype(BF16)
        dz_ref[:, Q_LORA + KV_LORA:] = dkin[:, KV_LORA:].astype(BF16)

        @pl.when(i == 0)
        def _():
            dgq_ref[...] = jnp.zeros(dgq_ref.shape, F32)
            dgkv_ref[...] = jnp.zeros(dgkv_ref.shape, F32)

        dgq_ref[0:1, :] += dgq
        dgkv_ref[0:1, :] += dgkv

    def full(shape):
        return pl.BlockSpec(shape, lambda i: (0, 0))

    def rows(w, c=0):
        return pl.BlockSpec((tm, w), lambda i: (i, c))

    return pl.pallas_call(
        body, name=name, grid=(T // tm,),
        in_specs=[rows(D_QK), rows(D_QK), rows(D_MLA), rows(512, Z_MID // 512),
                  full((1, Q_LORA)), full((1, KV_LORA)),
                  full((D_QK, Q_LORA)), full((D_QK, Q_LORA)), full((D_MLA, KV_LORA)),
                  rows(LANE), rows(LANE), rows(LANE)],
        out_specs=[rows(512), rows(D_QK), rows(D_QK), rows(D_MLA), full((8, Q_LORA)), full((8, KV_LORA))],
        out_shape=[jax.ShapeDtypeStruct((T, 512), BF16), jax.ShapeDtypeStruct((T, D_QK), BF16),
                   jax.ShapeDtypeStruct((T, D_QK), BF16), jax.ShapeDtypeStruct((T, D_MLA), BF16),
                   jax.ShapeDtypeStruct((8, Q_LORA), F32), jax.ShapeDtypeStruct((8, KV_LORA), F32)],
        compiler_params=_params(("arbitrary",)),
    )(dq, dk, dv, z, gq, gkv, wq_t, wk_t, wv_t, *tabs)


FLAT_W = 1024


ELEMENTWISE_TILE_BYTES = 768 * 1024


def _row_tile(R, C):
    width = -(-C // LANE) * LANE * 4
    best = None
    for t in range(16, R + 1, 16):
        if R % t == 0 and t * width <= ELEMENTWISE_TILE_BYTES:
            best = t
    if best is None:
        best = R
    return best


def _adamw(w, m, v, layer, parts, part_index, *, name):
    _, R, C = w.shape
    tr = _row_tile(R, C)
    bc1 = 1.0 - ADAM_B1 ** ADAM_STEP
    bc2 = 1.0 - ADAM_B2 ** ADAM_STEP
    n_parts = len(parts)

    def body(idx_ref, w_ref, m_ref, v_ref, *refs):
        g_refs = refs[:n_parts]
        g_out, d_out, m_out, v_out = refs[n_parts:]
        g = g_refs[0][...].astype(F32)
        for r in g_refs[1:]:
            g = g + r[...].astype(F32)
        wv = w_ref[...]
        mn = ADAM_B1 * m_ref[...] + (1.0 - ADAM_B1) * g
        vn = ADAM_B2 * v_ref[...] + (1.0 - ADAM_B2) * (g * g)
        m_hat = mn / bc1
        v_hat = vn / bc2
        g_out[...] = g
        d_out[...] = -ADAM_LR * (m_hat / (jnp.sqrt(v_hat) + ADAM_EPS) + ADAM_WD * wv)
        m_out[...] = mn
        v_out[...] = vn

    layer_row = pl.BlockSpec((None, tr, C), lambda i, idx: (layer, i, 0))
    in_specs = [layer_row, layer_row, layer_row]
    args = [w, m, v]
    for arr, slot in parts:
        if slot is None:
            in_specs.append(pl.BlockSpec((None, tr, C), lambda i, idx: (idx[0], i, 0)))
        else:
            in_specs.append(pl.BlockSpec((None, tr, C), lambda i, idx, slot=slot: (slot, i, 0)))
        args.append(arr)
    grid_spec = pltpu.PrefetchScalarGridSpec(
        num_scalar_prefetch=1, grid=(R // tr,), in_specs=in_specs,
        out_specs=[pl.BlockSpec((tr, C), lambda i, idx: (i, 0))] * 4)
    return pl.pallas_call(
        body, name=name, grid_spec=grid_spec,
        out_shape=[jax.ShapeDtypeStruct((R, C), F32)] * 4,
        compiler_params=_params(("parallel",)),
    )(part_index, *args)


def _pair_sum(g, recv, c_index, *, name):
    _, R, C = g.shape
    tr = _row_tile(R, C)

    def body(c_ref, g_ref, r_ref, o_ref, ob_ref):
        total = g_ref[...] + r_ref[...]
        o_ref[...] = total
        ob_ref[...] = total.astype(BF16)

    def spec(index):
        return pl.BlockSpec((None, tr, C), index)

    grid_spec = pltpu.PrefetchScalarGridSpec(
        num_scalar_prefetch=1, grid=(4, R // tr),
        in_specs=[spec(lambda q, i, c: (2 * q + c[0], i, 0)), spec(lambda q, i, c: (q, i, 0))],
        out_specs=[spec(lambda q, i, c: (q, i, 0)), spec(lambda q, i, c: (q, i, 0))])
    return pl.pallas_call(
        body, name=name, grid_spec=grid_spec,
        out_shape=[jax.ShapeDtypeStruct((4, R, C), F32), jax.ShapeDtypeStruct((4, R, C), BF16)],
        compiler_params=_params(("parallel", "parallel")),
    )(c_index, g, recv)


def _sum8(parts, *, name):
    _, R, _ = parts.shape

    def body(p_ref, o_ref):
        acc = p_ref[0]
        for d in range(1, N_DEV):
            acc = acc + p_ref[d]
        o_ref[...] = acc

    return pl.pallas_call(
        body, name=name, grid=(1,),
        in_specs=[pl.BlockSpec((N_DEV, R, FLAT_W), lambda i: (0, 0, 0))],
        out_specs=pl.BlockSpec((R, FLAT_W), lambda i: (0, 0)),
        out_shape=jax.ShapeDtypeStruct((R, FLAT_W), F32),
        compiler_params=_params(("arbitrary",)),
    )(parts)


_MESH_ID = pl.DeviceIdType.MESH
_ANY = pl.BlockSpec(memory_space=pl.ANY)


def _all_gather(shards, *, name):
    n = len(shards)

    def body(*refs):
        x_refs, out_refs = refs[:n], refs[n:2 * n]
        send_sems, recv_sems, local_sems = refs[2 * n:]
        x, y, c = lax.axis_index("x"), lax.axis_index("y"), lax.axis_index("c")
        me, sibling = (x, y, c), (x, y, 1 - c)
        chips = [(1 - x, y), (x, 1 - y), (1 - x, 1 - y)]

        def blk(a, px, py, pc):
            return out_refs[a].at[4 * px + 2 * py + pc]

        def copy(a, k, block, to, src=None):
            return pltpu.make_async_remote_copy(
                src_ref=blk(a, *block) if src is None else src, dst_ref=blk(a, *block),
                send_sem=send_sems.at[7 * a + k], recv_sem=recv_sems.at[7 * a + k],
                device_id=to, device_id_type=_MESH_ID)

        mine = [pltpu.make_async_copy(x_refs[a], blk(a, *me), local_sems.at[a]) for a in range(n)]
        for cp in mine:
            cp.start()
        first = []
        for a in range(n):
            first.append(copy(a, 0, me, sibling, src=x_refs[a]))
            first += [copy(a, 1 + j, me, (*chip, c), src=x_refs[a]) for j, chip in enumerate(chips)]
        for cp in first:
            cp.start()
        passed = []
        for j, chip in enumerate(chips):
            for a in range(n):
                copy(a, 1 + j, (*chip, c), me).wait_recv()
                fwd = copy(a, 4 + j, (*chip, c), sibling)
                fwd.start()
                passed.append(fwd)
        for a in range(n):
            copy(a, 0, sibling, me).wait_recv()
        for j, chip in enumerate(chips):
            for a in range(n):
                copy(a, 4 + j, (*chip, 1 - c), me).wait_recv()
        for cp in first + passed:
            cp.wait_send()
        for cp in mine:
            cp.wait()

    return pl.pallas_call(
        body, name=name,
        out_shape=[jax.ShapeDtypeStruct((N_DEV,) + s.shape, s.dtype) for s in shards],
        in_specs=[_ANY] * n, out_specs=[_ANY] * n,
        scratch_shapes=[pltpu.SemaphoreType.DMA((7 * n,)), pltpu.SemaphoreType.DMA((7 * n,)),
                        pltpu.SemaphoreType.DMA((n,))],
    )(*shards)


def _exchange_sibling(gs, *, name):
    n = len(gs)

    def body(*refs):
        g_refs, out_refs = refs[:n], refs[n:2 * n]
        send_sems, recv_sems = refs[2 * n:]
        x, y, c = lax.axis_index("x"), lax.axis_index("y"), lax.axis_index("c")
        copies = []
        for a in range(n):
            for q in range(4):
                copies.append(pltpu.make_async_remote_copy(
                    src_ref=g_refs[a].at[2 * q + (1 - c)], dst_ref=out_refs[a].at[q],
                    send_sem=send_sems.at[4 * a + q], recv_sem=recv_sems.at[4 * a + q],
                    device_id=(x, y, 1 - c), device_id_type=_MESH_ID))
        for cp in copies:
            cp.start()
        for cp in copies:
            cp.wait_recv()
        for cp in copies:
            cp.wait_send()

    return pl.pallas_call(
        body, name=name,
        out_shape=[jax.ShapeDtypeStruct((4,) + g.shape[1:], g.dtype) for g in gs],
        in_specs=[_ANY] * n, out_specs=[_ANY] * n,
        scratch_shapes=[pltpu.SemaphoreType.DMA((4 * n,)), pltpu.SemaphoreType.DMA((4 * n,))],
    )(*gs)


def _exchange_chips(hs, *, name):
    n = len(hs)

    def body(*refs):
        h_refs, out_refs = refs[:n], refs[n:2 * n]
        send_sems, recv_sems = refs[2 * n:]
        x, y, c = lax.axis_index("x"), lax.axis_index("y"), lax.axis_index("c")
        chips = [(1 - x, y), (x, 1 - y), (1 - x, 1 - y)]
        copies = []
        for a in range(n):
            for k, (px, py) in enumerate(chips):
                copies.append(pltpu.make_async_remote_copy(
                    src_ref=h_refs[a].at[2 * px + py], dst_ref=out_refs[a].at[k],
                    send_sem=send_sems.at[3 * a + k], recv_sem=recv_sems.at[3 * a + k],
                    device_id=(px, py, c), device_id_type=_MESH_ID))
        for cp in copies:
            cp.start()
        for cp in copies:
            cp.wait_recv()
        for cp in copies:
            cp.wait_send()

    return pl.pallas_call(
        body, name=name,
        out_shape=[jax.ShapeDtypeStruct((3,) + h.shape[1:], h.dtype) for h in hs],
        in_specs=[_ANY] * n, out_specs=[_ANY] * n,
        scratch_shapes=[pltpu.SemaphoreType.DMA((3 * n,)), pltpu.SemaphoreType.DMA((3 * n,))],
    )(*hs)


_BIG = (("ffn1_w_up", 2), ("ffn1_w_down", 1), ("mix_w_in", 2), ("w_uq", 2), ("w_ukv", 2),
        ("w_br_conv", 2), ("w_br_mla", 2), ("w_o", 1), ("ffn2_w_up", 2), ("ffn2_w_down", 1))
_SMALL_SHARDED = (("meta_tokens", 1), ("mix_b_gate", 2), ("conv_w", 2), ("ln_g", 2), ("ln_b", 2))
_SMALL_REPL = ("q_norm_g", "kv_norm_g")


BIG_ROW_ALIGN = 64


def _pack(arrs, dtype, row_align=8):
    flat = jnp.concatenate([a.reshape(-1).astype(dtype) for a in arrs])
    n = flat.shape[0]
    rows = -(-n // (row_align * FLAT_W)) * row_align
    return jnp.pad(flat, (0, rows * FLAT_W - n)).reshape(rows, FLAT_W)


def _unpack(flat, shapes):
    flat = flat.reshape(-1)
    out, off = [], 0
    for s in shapes:
        n = int(np.prod(s))
        out.append(flat[off:off + n].reshape(s))
        off += n
    return out


def _unpack_gathered(gathered, shapes, axes):
    g2 = gathered.reshape(N_DEV, -1)
    out, off = [], 0
    for s, ax in zip(shapes, axes):
        n = int(np.prod(s))
        blocks = g2[:, off:off + n].reshape((N_DEV,) + tuple(s))
        full = jnp.moveaxis(blocks, 0, ax)
        out.append(full.reshape(tuple(s[:ax]) + (N_DEV * s[ax],) + tuple(s[ax + 1:])))
        off += n
    return out


def _to_dest(full, ax):
    s = full.shape
    split = full.reshape(s[:ax] + (N_DEV, s[ax] // N_DEV) + s[ax + 1:])
    return jnp.moveaxis(split, ax, 0)


def _from_blocks(g, ax):
    full = jnp.moveaxis(g, 0, ax)
    s = full.shape
    return full.reshape(s[:ax] + (s[ax] * s[ax + 1],) + s[ax + 2:])


def _rope_tables(T):
    inv_freq = 1.0 / (ROPE_BASE ** (jnp.arange(0, QK_ROPE, 2, dtype=F32) / QK_ROPE))
    ang = jnp.arange(T, dtype=F32)[:, None] * inv_freq[None, :]
    cos, sin = jnp.cos(ang), jnp.sin(ang)
    half = QK_ROPE // 2
    ones = jnp.ones((T, QK_NOPE), F32)
    zeros = lambda w: jnp.zeros((T, w), F32)
    c = jnp.concatenate([ones, cos, cos, zeros(HEAD_PAD - QK_NOPE - QK_ROPE)], axis=1)
    s1 = jnp.concatenate([zeros(QK_NOPE + half), sin, zeros(HEAD_PAD - QK_NOPE - QK_ROPE)], axis=1)
    s2 = jnp.concatenate([zeros(QK_NOPE), -sin, zeros(HEAD_PAD - QK_NOPE - half)], axis=1)
    return c, s1, s2


def _assemble_layer(gathered):
    W = {n: _from_blocks(gathered[n], ax - 1) for n, ax in _BIG if not n.endswith("w_up")}
    up = {n: gathered[n] for n in ("ffn1_w_up", "ffn2_w_up")}
    up.update({n + "_t": jnp.swapaxes(gathered[n], 1, 2) for n in ("ffn1_w_up", "ffn2_w_up")})
    return _layer_weights(W), up


def _layer_weights(W):
    w_in = W["mix_w_in"]
    w_in_p = jnp.concatenate([w_in[:, :Z_KR_END], jnp.zeros((D_MODEL, D_IN_PAD - D_IN_REAL), BF16),
                              w_in[:, Z_KR_END:]], axis=1)
    w_uq = W["w_uq"].reshape(Q_LORA, MLA_HEADS, QK_NOPE + QK_ROPE)
    wq = jnp.pad(w_uq, ((0, 0), (0, 0), (0, HEAD_PAD - QK_NOPE - QK_ROPE))).reshape(Q_LORA, D_QK)
    w_ukv = W["w_ukv"].reshape(KV_LORA, MLA_HEADS, QK_NOPE + V_HEAD)
    wk_top = jnp.pad(w_ukv[:, :, :QK_NOPE], ((0, 0), (0, 0), (0, HEAD_PAD - QK_NOPE))).reshape(KV_LORA, D_QK)
    place = np.zeros((Q_LORA - KV_LORA, MLA_HEADS, HEAD_PAD), np.float32)
    for r in range(QK_ROPE):
        place[r, :, QK_NOPE + r] = 1.0
    wk = jnp.concatenate([wk_top, jnp.asarray(place.reshape(Q_LORA - KV_LORA, D_QK), BF16)], axis=0)
    wv = w_ukv[:, :, QK_NOPE:].reshape(KV_LORA, D_MLA)
    wv_ext = jnp.pad(w_ukv[:, :, QK_NOPE:], ((0, 0), (0, 0), (0, HEAD_PAD - V_HEAD))).reshape(KV_LORA, D_QK)
    lw = dict(
        ffn1_down=W["ffn1_w_down"], ffn2_down=W["ffn2_w_down"],
        w_in=w_in_p, wq=wq, wk=wk, wv=wv, wv_ext=wv_ext,
        wbc=W["w_br_conv"], wbm=W["w_br_mla"], wo=W["w_o"])
    for n in ("w_in", "wq", "wk", "wv", "wbc", "wbm", "wo"):
        lw[n + "_t"] = lw[n].T
    for n in ("ffn1_down", "ffn2_down"):
        lw[n + "_t4"] = jnp.swapaxes(lw[n].reshape(FF_HALF_BLOCKS, FF_BLK, D_MODEL), 1, 2)
    return lw


def _row8(v):
    return jnp.pad(v, ((0, 8 - v.shape[0]), (0, 0)))


def _local_step(x, tgt, gathered0, w1_blocks, S):
    big_names = [n for n, _ in _BIG]
    t_real = N_META + x.shape[0]
    T = -(-t_real // ROW_ALIGN) * ROW_ALIGN
    pad = T - t_real
    tm = _tile(T, (384, 256, 128))
    tms = _tile(T, (768, 256, 128))
    blk = _tile(T, (768, 256, 128))
    tabs = _rope_tables(T)

    h0 = jnp.concatenate([S["meta_tokens"], x, jnp.zeros((pad, D_MODEL), F32)], axis=0)
    tgt_p = jnp.concatenate([jnp.zeros((N_META, D_MODEL), F32), tgt, jnp.zeros((pad, D_MODEL), F32)], axis=0)

    def ffn_fwd(h, hb, up8, down, g, b, tag):
        gu, a = _ffn_up(hb, up8, name=f"ffn_up_{tag}", tm=tms)
        r, y, yb = _mm_res_ln(a, down, h, g, b, scale=0.5, name=f"ffn_down_ln_{tag}", tm=tm)
        return dict(gu=gu, a=a, r=r), y, yb

    saved = []
    h, hb = h0, h0.astype(BF16)
    gathered = gathered0
    for l in range(DEPTH):
        lw, up = _assemble_layer(gathered)
        sv = dict(lw=lw, up=up, h_in=h, h_in_b=hb)
        sv["f1"], h1, h1b = ffn_fwd(h, hb, up["ffn1_w_up"], lw["ffn1_down"],
                                    S["ln_g"][l, 0:1], S["ln_b"][l, 0:1], f"{l}a")
        z = _mm(h1b, lw["w_in"], out_dtype=F32, name=f"mix_in_{l}", tm=tms, tn=1024, tk=D_MODEL)
        conv_w8 = _row8(S["conv_w"][l])
        bg8 = _row8(S["mix_b_gate"][l])
        yc = _conv_fwd(z, conv_w8, name=f"conv_fwd_{l}", tm=tms)
        gq, gkv = S["q_norm_g"][l:l + 1], S["kv_norm_g"][l:l + 1]
        q, k, v, qn, kin = _qkv_proj(z, gq, gkv, lw["wq"], lw["wk"], lw["wv_ext"], tabs, name=f"qkv_proj_{l}", tm=tm)
        if l == 0:
            o, lse, *nxt = _attn_fwd(q, k, v, name=f"attn_fwd_{l}", blk=blk,
                                     exchange=_Exchange("gather", [w1_blocks[n] for n in big_names]))
            gathered = dict(zip(big_names, nxt))
        else:
            o, lse = _attn_fwd(q, k, v, name=f"attn_fwd_{l}", blk=blk)
        mg, pa, pb = _merge(yc, o, lw["wbc"], lw["wbm"], z, bg8, name=f"merge_{l}", tm=tm)
        r2, h2, h2b = _mm_res_ln(mg, lw["wo"], h1, S["ln_g"][l, 1:2], S["ln_b"][l, 1:2], scale=1.0,
                                 name=f"wo_ln_{l}", tm=tm)
        sv["f2"], h3, h3b = ffn_fwd(h2, h2b, up["ffn2_w_up"], lw["ffn2_down"],
                                    S["ln_g"][l, 2:3], S["ln_b"][l, 2:3], f"{l}b")
        sv.update(h1b=h1b, z=z, conv_w8=conv_w8, bg8=bg8, yc=yc, gq=gq, gkv=gkv, q=q, k=k, v=v, qn=qn, kin=kin,
                  o=o, lse=lse, mg=mg, pa=pa, pb=pb, r2=r2, h2b=h2b)
        saved.append(sv)
        h, hb = h3, h3b

    dh, loss8 = _loss_head(h, tgt_p, t_real=t_real, name="loss_head", tm=tm)

    tk = _tile(T, (2816, 768, 256, 128))
    grads = {n: [None] * DEPTH for n, _ in _BIG}
    for n in ("mix_b_gate", "conv_w", "q_norm_g", "kv_norm_g"):
        grads[n] = [None] * DEPTH
    grads["ln_g"] = [[None] * 3 for _ in range(DEPTH)]
    grads["ln_b"] = [[None] * 3 for _ in range(DEPTH)]

    def ffn_bwd(dy, f, h_in_b, up8_t, down_t4, g, tag):
        dr, dfb, dg, db = _ln_bwd(dy, f["r"], g, scale=0.5, name=f"ln_bwd_{tag}", tm=tm)
        dgu = _ffn_bwd_mid(dfb, down_t4, f["gu"], name=f"ffn_bwd_mid_{tag}", tm=tms)
        d_down = _mm_tn_call(
            f["a"], dfb,
            pl.BlockSpec((None, tk, FF_BLK), lambda i, j, k: (i, k, 0)),
            pl.BlockSpec((tk, D_MODEL), lambda i, j, k: (k, 0)),
            out_shape=jax.ShapeDtypeStruct((D_FF, D_MODEL), F32),
            out_spec=pl.BlockSpec((FF_BLK, D_MODEL), lambda i, j, k: (i, 0)),
            grid=(FF_HALF_BLOCKS, 1, T // tk), name=f"dw_down_{tag}")
        d_up = _mm_tn_call(
            h_in_b, dgu,
            pl.BlockSpec((tk, D_MODEL), lambda i, j, k: (k, 0)),
            pl.BlockSpec((None, None, tk, FF_BLK),
                         lambda i, j, k: (j // FF_HALF_BLOCKS, j % FF_HALF_BLOCKS, k, 0)),
            out_shape=jax.ShapeDtypeStruct((N_DEV, D_MODEL, FF_BLK), F32),
            out_spec=pl.BlockSpec((None, D_MODEL, FF_BLK), lambda i, j, k: (j, 0, 0)),
            grid=(1, N_DEV, T // tk), name=f"dw_up_{tag}")
        row = pl.BlockSpec((tms, D_MODEL), lambda i, j, k: (i, 0))
        dh_in = _mm_call(
            dgu, up8_t,
            pl.BlockSpec((None, FF_HALF_BLOCKS, tms, FF_BLK), lambda i, j, k: (k, 0, i, 0)),
            pl.BlockSpec((FF_HALF_BLOCKS, FF_BLK, D_MODEL), lambda i, j, k: (k, 0, 0)),
            out_shape=jax.ShapeDtypeStruct((T, D_MODEL), F32), out_spec=row, acc_shape=(tms, D_MODEL),
            grid=(T // tms, 1, 2), name=f"ffn_dx_{tag}", res=dr, res_spec=row,
            res_scale=ALPHA, pieces=FF_HALF_BLOCKS)
        return dh_in, d_up, d_down, dg[0], db[0]

    def to_dest(l):
        return {n: grads[n][l] if n.endswith("w_up") else _to_dest(grads[n][l], ax - 1) for n, ax in _BIG}

    received1 = None
    for l in reversed(range(DEPTH)):
        sv = saved[l]
        lw, up = sv["lw"], sv["up"]
        dh, grads["ffn2_w_up"][l], grads["ffn2_w_down"][l], grads["ln_g"][l][2], grads["ln_b"][l][2] = ffn_bwd(
            dh, sv["f2"], sv["h2b"], up["ffn2_w_up_t"], lw["ffn2_down_t4"], S["ln_g"][l, 2:3], f"{l}b")
        dr2, dmb, dg, db = _ln_bwd(dh, sv["r2"], S["ln_g"][l, 1:2], scale=1.0, name=f"ln_bwd_{l}m", tm=tm)
        grads["ln_g"][l][1], grads["ln_b"][l][1] = dg[0], db[0]
        grads["w_o"][l] = _mm_tn(sv["mg"], dmb, name=f"dw_o_{l}", tm=D_MODEL, tn=D_MODEL, tk=tk)
        dpa, dpb, dgate, dbg = _wo_bwd(dmb, lw["wo_t"], sv["z"], sv["bg8"], sv["pa"], sv["pb"], name=f"wo_bwd_{l}", tm=tm)
        grads["mix_b_gate"][l] = dbg[0:2]
        grads["w_br_conv"][l] = _mm_tn(sv["yc"], dpa, name=f"dw_br_conv_{l}", tm=D_CONV, tn=D_MODEL, tk=tk)
        grads["w_br_mla"][l] = _mm_tn(sv["o"], dpb, name=f"dw_br_mla_{l}", tm=D_MLA, tn=D_MODEL, tk=tk)
        dyc = _mm(dpa, lw["wbc_t"], out_dtype=F32, name=f"d_yconv_{l}", tm=tms, tn=D_CONV, tk=D_MODEL)
        dym = _mm(dpb, lw["wbm_t"], out_dtype=BF16, name=f"d_ymla_{l}", tm=tms, tn=D_MLA, tk=D_MODEL)
        dz_conv, dcw = _conv_bwd(dyc, sv["z"], sv["conv_w8"], name=f"conv_bwd_{l}", tm=tms)
        grads["conv_w"][l] = dcw[0:CONV_WIDTH]
        if l == 0:
            dest1 = to_dest(1)
            dq, dk, dv, *got = _attn_bwd(
                sv["q"], sv["k"], sv["v"], sv["o"], dym, sv["lse"], name=f"attn_bwd_{l}", blk=blk,
                exchange=_Exchange("scatter", [dest1[n].astype(BF16) for n in big_names]))
            received1 = dict(zip(big_names, got))
        else:
            dq, dk, dv = _attn_bwd(sv["q"], sv["k"], sv["v"], sv["o"], dym, sv["lse"], name=f"attn_bwd_{l}", blk=blk)
        dz_mid, dqb, dkb, dvb, dgq, dgkv = _qkv_bwd(dq, dk, dv, sv["z"], sv["gq"], sv["gkv"], lw["wq_t"], lw["wk_t"],
                                                    lw["wv_t"], tabs, name=f"qkv_bwd_{l}", tm=tm)
        grads["q_norm_g"][l], grads["kv_norm_g"][l] = dgq[0], dgkv[0]
        d_wq = _mm_tn(sv["qn"], dqb, name=f"dw_uq_{l}", tm=Q_LORA, tn=D_QK, tk=tk)
        d_wk = _mm_tn(sv["kin"], dkb, name=f"dw_uk_{l}", tm=Q_LORA, tn=D_QK, tk=tk)
        d_wv = _mm_tn(sv["kin"], dvb, name=f"dw_uv_{l}", tm=Q_LORA, tn=D_MLA, tk=tk)
        grads["w_uq"][l] = d_wq.reshape(Q_LORA, MLA_HEADS, HEAD_PAD)[:, :, :QK_NOPE + QK_ROPE].reshape(Q_LORA, -1)
        d_kn = d_wk[:KV_LORA].reshape(KV_LORA, MLA_HEADS, HEAD_PAD)[:, :, :QK_NOPE]
        d_vv = d_wv[:KV_LORA].reshape(KV_LORA, MLA_HEADS, V_HEAD)
        grads["w_ukv"][l] = jnp.concatenate([d_kn, d_vv], axis=-1).reshape(KV_LORA, -1)
        dz = jnp.concatenate([dz_conv, dz_mid, dgate], axis=1)
        d_win = _mm_tn(sv["h1b"], dz, name=f"dw_in_{l}", tm=D_MODEL, tn=1024, tk=tk)
        grads["mix_w_in"][l] = jnp.concatenate([d_win[:, :Z_KR_END], d_win[:, Z_KR_END + D_IN_PAD - D_IN_REAL:]], axis=1)
        dh = _mm(dz, lw["w_in_t"], out_dtype=F32, name=f"mix_dx_{l}", res=dr2, res_scale=ALPHA,
                 tm=tms, tn=D_MODEL, tk=2048)
        dh, grads["ffn1_w_up"][l], grads["ffn1_w_down"][l], grads["ln_g"][l][0], grads["ln_b"][l][0] = ffn_bwd(
            dh, sv["f1"], sv["h_in_b"], up["ffn1_w_up_t"], lw["ffn1_down_t4"], S["ln_g"][l, 0:1], f"{l}a")

    small = {n: jnp.stack(grads[n]) for n in ("mix_b_gate", "conv_w", "q_norm_g", "kv_norm_g")}
    small["ln_g"] = jnp.stack([jnp.stack(g) for g in grads["ln_g"]])
    small["ln_b"] = jnp.stack([jnp.stack(g) for g in grads["ln_b"]])
    small["meta_tokens"] = dh[:N_META]
    return loss8, dh[N_META:t_real], to_dest(0), dest1, received1, small


def kernel(x, meta_tokens, ffn1_w_up, ffn1_w_down, mix_w_in, mix_b_gate, conv_w, q_norm_g, w_uq, kv_norm_g, w_ukv, w_br_conv, w_br_mla, w_o, ffn2_w_up, ffn2_w_down, ln_g, ln_b, loss_target, m_meta_tokens, m_ffn1_w_up, m_ffn1_w_down, m_mix_w_in, m_mix_b_gate, m_conv_w, m_q_norm_g, m_w_uq, m_kv_norm_g, m_w_ukv, m_w_br_conv, m_w_br_mla, m_w_o, m_ffn2_w_up, m_ffn2_w_down, m_ln_g, m_ln_b, v_meta_tokens, v_ffn1_w_up, v_ffn1_w_down, v_mix_w_in, v_mix_b_gate, v_conv_w, v_q_norm_g, v_w_uq, v_kv_norm_g, v_w_ukv, v_w_br_conv, v_w_br_mla, v_w_o, v_ffn2_w_up, v_ffn2_w_down, v_ln_g, v_ln_b):
    names = ["meta_tokens", "ffn1_w_up", "ffn1_w_down", "mix_w_in", "mix_b_gate", "conv_w", "q_norm_g", "w_uq",
             "kv_norm_g", "w_ukv", "w_br_conv", "w_br_mla", "w_o", "ffn2_w_up", "ffn2_w_down", "ln_g", "ln_b"]
    w = dict(zip(names, (meta_tokens, ffn1_w_up, ffn1_w_down, mix_w_in, mix_b_gate, conv_w, q_norm_g, w_uq,
                         kv_norm_g, w_ukv, w_br_conv, w_br_mla, w_o, ffn2_w_up, ffn2_w_down, ln_g, ln_b)))
    m = dict(zip(names, (m_meta_tokens, m_ffn1_w_up, m_ffn1_w_down, m_mix_w_in, m_mix_b_gate, m_conv_w, m_q_norm_g,
                         m_w_uq, m_kv_norm_g, m_w_ukv, m_w_br_conv, m_w_br_mla, m_w_o, m_ffn2_w_up, m_ffn2_w_down,
                         m_ln_g, m_ln_b)))
    v = dict(zip(names, (v_meta_tokens, v_ffn1_w_up, v_ffn1_w_down, v_mix_w_in, v_mix_b_gate, v_conv_w, v_q_norm_g,
                         v_w_uq, v_kv_norm_g, v_w_ukv, v_w_br_conv, v_w_br_mla, v_w_o, v_ffn2_w_up, v_ffn2_w_down,
                         v_ln_g, v_ln_b)))
    ix, iy, ic = lax.axis_index("x"), lax.axis_index("y"), lax.axis_index("c")
    dev = 4 * ix + 2 * iy + ic

    big_names = [n for n, _ in _BIG]
    big_axes = [a for _, a in _BIG]
    big_shapes = [w[n].shape for n in big_names]
    small_names = [n for n, _ in _SMALL_SHARDED]
    small_axes = [a for _, a in _SMALL_SHARDED]
    small_shapes = [w[n].shape for n in small_names]
    gathered = _all_gather([w[n][0].astype(BF16) for n in big_names] + [_pack([w[n] for n in small_names], F32)],
                           name="all_gather_weights")
    gathered0 = dict(zip(big_names, gathered[:-1]))
    w1_blocks = {n: w[n][1].astype(BF16) for n in big_names}
    S = dict(zip(small_names, _unpack_gathered(gathered[-1], small_shapes, small_axes)))
    S["q_norm_g"], S["kv_norm_g"] = q_norm_g, kv_norm_g

    loss8, grad_x, dest0, dest1, received1, G = _local_step(x[0], loss_target[0], gathered0, w1_blocks, S)

    from_sibling = _exchange_sibling([dest0[n] for n in big_names], name="rs_sibling")
    c_index = ic.reshape(1).astype(jnp.int32)
    pair = [_pair_sum(dest0[n], r, c_index, name=f"rs_sibling_sum_{n}") for n, r in zip(big_names, from_sibling)]
    from_chips = _exchange_chips([p[1] for p in pair], name="rs_chips")
    my_chip = (2 * ix + iy).reshape(1).astype(jnp.int32)
    my_dev = dev.reshape(1).astype(jnp.int32)
    big_res = [{}, {}, {}, {}]
    for n, p, fc in zip(big_names, pair, from_chips):
        res0 = _adamw(w[n], m[n], v[n], 0, [(p[0], None), (fc, 0), (fc, 1), (fc, 2)], my_chip, name=f"adamw0_{n}")
        res1 = _adamw(w[n], m[n], v[n], 1, [(dest1[n], None)] + [(received1[n], k) for k in range(N_DEV - 1)],
                      my_dev, name=f"adamw1_{n}")
        for kind in range(4):
            big_res[kind][n] = jnp.stack([res0[kind], res1[kind]])

    small_all = small_names + list(_SMALL_REPL)
    part = _pack([G[n] for n in small_all] + [loss8[0, 0:1]], F32)
    full_shapes = [G[n].shape for n in small_all] + [(1,)]
    summed = _sum8(_all_gather([part], name="all_gather_small_grads")[0], name="sum_small_grads")
    unpacked = _unpack(summed, full_shapes)
    loss = unpacked[-1][0]
    g_full = dict(zip(small_all, unpacked[:-1]))
    g_loc = []
    for n in small_all:
        if n in _SMALL_REPL:
            g_loc.append(g_full[n])
        else:
            ax = dict(_SMALL_SHARDED)[n]
            g_loc.append(lax.dynamic_slice_in_dim(g_full[n], dev * w[n].shape[ax], w[n].shape[ax], axis=ax))
    loc_shapes = [w[n].shape for n in small_all]
    g_pack = _pack(g_loc, F32)
    small_out = _adamw(_pack([w[n] for n in small_all], F32)[None], _pack([m[n] for n in small_all], F32)[None],
                       _pack([v[n] for n in small_all], F32)[None], 0, [(g_pack[None], 0)],
                       jnp.zeros((1,), jnp.int32), name="adamw_small")
    small_res = [dict(zip(small_all, _unpack(o, loc_shapes))) for o in small_out]

    outs = [loss, grad_x[None]]
    for kind in range(4):
        for n in names:
            outs.append(big_res[kind][n] if n in big_res[kind] else small_res[kind][n])
    return tuple(outs)
```

```python
import functools

import numpy as np
import jax
import jax.numpy as jnp
from jax import lax
from jax.experimental import pallas as pl
from jax.experimental.pallas import tpu as pltpu

F32 = jnp.float32
BF16 = jnp.bfloat16

D_MODEL = 1024
DEPTH = 2
N_META = 16
D_CONV = 512
CONV_WIDTH = 3
MLA_HEADS = 8
QK_NOPE = 64
QK_ROPE = 32
V_HEAD = 64
Q_LORA = 256
KV_LORA = 128
D_MLA = MLA_HEADS * V_HEAD
ROPE_BASE = 10000.0
NEG_INF = -1e30
D_FF = 2816
ALPHA = (2 * DEPTH) ** 0.25
LN_EPS = 1e-5
RMS_EPS = 1e-6
ATTN_SCALE = (QK_NOPE + QK_ROPE) ** -0.5
ADAM_LR = 0.001
ADAM_B1 = 0.9
ADAM_B2 = 0.999
ADAM_EPS = 1e-08
ADAM_WD = 0.01
ADAM_STEP = 10

N_DEV = 8
HEAD_PAD = 128
HEADS_PER_STEP = 2
FF_BLK = 2 * D_FF // N_DEV
FF_HALF_BLOCKS = N_DEV // 2
D_QK = MLA_HEADS * HEAD_PAD
Z_CONV = 0
Z_MID = 1536
Z_GATE = 2048
D_IN_PAD = 4096
D_IN_REAL = 4000
Z_KR_END = Z_MID + Q_LORA + KV_LORA + QK_ROPE

V7X_VMEM_LIMIT = 56 * 1024 * 1024
LANE = 128
ROW_ALIGN = 256


def _tile(n, cands):
    for c in cands:
        if n % c == 0:
            return c
    raise ValueError(f"no tile for {n} in {cands}")


def _params(sem):
    return pltpu.CompilerParams(dimension_semantics=sem, vmem_limit_bytes=V7X_VMEM_LIMIT)


def _mm_call(a, b, a_spec, b_spec, *, out_shape, out_spec, acc_shape, grid, name, trans_b=False,
             res=None, res_spec=None, res_scale=1.0, pieces=1, exchange=None):
    nk = grid[2]
    has_res = res is not None
    out_dtype = out_shape.dtype
    dims = (((1,), (1,)), ((), ())) if trans_b else (((1,), (0,)), ((), ()))

    def body(*refs):
        if has_res:
            a_ref, b_ref, r_ref, o_ref, acc = refs
        else:
            a_ref, b_ref, o_ref, acc = refs
        k = pl.program_id(2)
        if pieces == 1:
            part = lax.dot_general(a_ref[...], b_ref[...], dims, preferred_element_type=F32)
        else:
            part = lax.dot_general(a_ref[0], b_ref[0], dims, preferred_element_type=F32)
            for p in range(1, pieces):
                part = part + lax.dot_general(a_ref[p], b_ref[p], dims, preferred_element_type=F32)

        @pl.when(k == 0)
        def _():
            acc[...] = part

        @pl.when(k > 0)
        def _():
            acc[...] += part

        @pl.when(k == nk - 1)
        def _():
            out = acc[...]
            if has_res:
                out = out + res_scale * r_ref[...]
            o_ref[...] = out.astype(out_dtype)

    in_specs = [a_spec, b_spec]
    args = [a, b]
    if has_res:
        in_specs.append(res_spec)
        args.append(res)
    if exchange is None:
        return pl.pallas_call(
            body, name=name, grid=grid, in_specs=in_specs, out_specs=out_spec, out_shape=out_shape,
            scratch_shapes=[pltpu.VMEM(acc_shape, F32)],
            compiler_params=_params(("parallel", "parallel", "arbitrary")),
        )(*args)
    ex_args, ex_specs, ex_out, ex_scratch = _exchange_operands(exchange)
    return pl.pallas_call(
        _carry_exchange(body, exchange, 0, len(args), 1, grid), name=name, grid=grid,
        in_specs=in_specs + ex_specs, out_specs=[out_spec] + ex_specs, out_shape=[out_shape] + ex_out,
        scratch_shapes=[pltpu.VMEM(acc_shape, F32)] + ex_scratch,
        compiler_params=_params(("arbitrary", "arbitrary", "arbitrary")),
    )(*args, *ex_args)


def _mm(a, b, *, out_dtype, name, trans_b=False, res=None, res_scale=1.0, tm, tn, tk, exchange=None):
    M, K = a.shape
    N = b.shape[0] if trans_b else b.shape[1]
    assert M % tm == 0 and N % tn == 0 and K % tk == 0
    b_spec = (pl.BlockSpec((tn, tk), lambda i, j, k: (j, k)) if trans_b
              else pl.BlockSpec((tk, tn), lambda i, j, k: (k, j)))
    tile = pl.BlockSpec((tm, tn), lambda i, j, k: (i, j))
    return _mm_call(a, b, pl.BlockSpec((tm, tk), lambda i, j, k: (i, k)), b_spec,
                    out_shape=jax.ShapeDtypeStruct((M, N), out_dtype), out_spec=tile, acc_shape=(tm, tn),
                    grid=(M // tm, N // tn, K // tk), name=name, trans_b=trans_b,
                    res=res, res_spec=tile, res_scale=res_scale, exchange=exchange)


def _mm_tn_call(a, b, a_spec, b_spec, *, out_shape, out_spec, grid, name, exchange=None):
    def body(a_ref, b_ref, o_ref):
        k = pl.program_id(2)
        part = lax.dot_general(a_ref[...], b_ref[...], (((0,), (0,)), ((), ())),
                               preferred_element_type=F32)

        @pl.when(k == 0)
        def _():
            o_ref[...] = part

        @pl.when(k > 0)
        def _():
            o_ref[...] += part

    if exchange is None:
        return pl.pallas_call(
            body, name=name, grid=grid, in_specs=[a_spec, b_spec], out_specs=out_spec, out_shape=out_shape,
            compiler_params=_params(("parallel", "parallel", "arbitrary")),
        )(a, b)
    ex_args, ex_specs, ex_out, ex_scratch = _exchange_operands(exchange)
    return pl.pallas_call(
        _carry_exchange(body, exchange, 0, 2, 1, grid), name=name, grid=grid,
        in_specs=[a_spec, b_spec] + ex_specs, out_specs=[out_spec] + ex_specs, out_shape=[out_shape] + ex_out,
        scratch_shapes=ex_scratch,
        compiler_params=_params(("arbitrary", "arbitrary", "arbitrary")),
    )(a, b, *ex_args)


def _mm_tn(a, b, *, name, tm, tn, tk):
    T, M = a.shape
    N = b.shape[1]
    assert M % tm == 0 and N % tn == 0 and T % tk == 0
    return _mm_tn_call(a, b, pl.BlockSpec((tk, tm), lambda i, j, k: (k, i)),
                       pl.BlockSpec((tk, tn), lambda i, j, k: (k, j)),
                       out_shape=jax.ShapeDtypeStruct((M, N), F32),
                       out_spec=pl.BlockSpec((tm, tn), lambda i, j, k: (i, j)),
                       grid=(M // tm, N // tn, T // tk), name=name)


def _ffn_up(hb, w_up8, *, name, tm, exchange=None):
    T = hb.shape[0]

    def body(h_ref, wg_ref, wu_ref, gu_ref, a_ref):
        h = h_ref[...]
        g = jnp.dot(h, wg_ref[...], preferred_element_type=F32)
        u = jnp.dot(h, wu_ref[...], preferred_element_type=F32)
        sg = jax.nn.sigmoid(g)
        silu = g * sg
        gu_ref[0] = (u * (sg * (1.0 + g * (1.0 - sg)))).astype(BF16)
        gu_ref[1] = silu.astype(BF16)
        a_ref[...] = (silu * u).astype(BF16)

    grid = (FF_HALF_BLOCKS, T // tm)
    ex_args, ex_specs, ex_out, ex_scratch = _exchange_operands(exchange)
    return pl.pallas_call(
        _carry_exchange(body, exchange, 0, 3, 2, grid), name=name, grid=grid,
        in_specs=[pl.BlockSpec((tm, D_MODEL), lambda j, i: (i, 0)),
                  pl.BlockSpec((None, D_MODEL, FF_BLK), lambda j, i: (j, 0, 0)),
                  pl.BlockSpec((None, D_MODEL, FF_BLK), lambda j, i: (j + FF_HALF_BLOCKS, 0, 0))] + ex_specs,
        out_specs=[pl.BlockSpec((2, None, tm, FF_BLK), lambda j, i: (0, j, i, 0)),
                   pl.BlockSpec((None, tm, FF_BLK), lambda j, i: (j, i, 0))] + ex_specs,
        out_shape=[jax.ShapeDtypeStruct((2, FF_HALF_BLOCKS, T, FF_BLK), BF16),
                   jax.ShapeDtypeStruct((FF_HALF_BLOCKS, T, FF_BLK), BF16)] + ex_out,
        scratch_shapes=ex_scratch,
        compiler_params=_params(("arbitrary", "arbitrary") if exchange else ("parallel", "parallel")),
    )(hb, w_up8, w_up8, *ex_args)


def _mm_res_ln(a, w, res, g, b, *, scale, name, tm):
    split = a.ndim == 3
    if split:
        S, T, Ks = a.shape
        K = S * Ks
    else:
        T, K = a.shape

    def body(a_ref, w_ref, res_ref, g_ref, b_ref, r_ref, y_ref, yb_ref):
        if split:
            f = jnp.dot(a_ref[0], w_ref[0:Ks, :], preferred_element_type=F32)
            for s in range(1, S):
                f = f + jnp.dot(a_ref[s], w_ref[s * Ks:(s + 1) * Ks, :], preferred_element_type=F32)
        else:
            f = jnp.dot(a_ref[...], w_ref[...], preferred_element_type=F32)
        r = ALPHA * res_ref[...] + scale * f
        mu = jnp.mean(r, axis=-1, keepdims=True)
        xc = r - mu
        var = jnp.mean(xc * xc, axis=-1, keepdims=True)
        y = xc * lax.rsqrt(var + LN_EPS) * g_ref[...] + b_ref[...]
        r_ref[...] = r
        y_ref[...] = y
        yb_ref[...] = y.astype(BF16)

    row = pl.BlockSpec((tm, D_MODEL), lambda i: (i, 0))
    vec = pl.BlockSpec((1, D_MODEL), lambda i: (0, 0))
    return pl.pallas_call(
        body, name=name, grid=(T // tm,),
        in_specs=[pl.BlockSpec((S, tm, Ks), lambda i: (0, i, 0)) if split else pl.BlockSpec((tm, K), lambda i: (i, 0)),
                  pl.BlockSpec((K, D_MODEL), lambda i: (0, 0)), row, vec, vec],
        out_specs=[row, row, row],
        out_shape=[jax.ShapeDtypeStruct((T, D_MODEL), F32), jax.ShapeDtypeStruct((T, D_MODEL), F32),
                   jax.ShapeDtypeStruct((T, D_MODEL), BF16)],
        compiler_params=_params(("parallel",)),
    )(a, w, res, g, b)


def _conv_fwd(z, conv_w8, *, name, tm):
    T = z.shape[0]
    hb = tm // 8

    def body(b_ref, c_ref, h_ref, cp_ref, hp_ref, w_ref, y_ref):
        i = pl.program_id(0)
        u = c_ref[...] * h_ref[...]
        up = jnp.where(i > 0, cp_ref[...] * hp_ref[...], 0.0)
        ue = jnp.concatenate([up, u], axis=0)
        s1 = pltpu.roll(ue, 1, 0)[8:]
        s2 = pltpu.roll(ue, 2, 0)[8:]
        w = w_ref[...]
        conv = w[0:1] * s2 + w[1:2] * s1 + w[2:3] * u
        y_ref[...] = (b_ref[...] * conv).astype(BF16)

    def col(c):
        return pl.BlockSpec((tm, D_CONV), lambda i: (i, c))

    def prev(c):
        return pl.BlockSpec((8, D_CONV), lambda i: (jnp.maximum(i * hb - 1, 0), c))

    return pl.pallas_call(
        body, name=name, grid=(T // tm,),
        in_specs=[col(0), col(1), col(2), prev(1), prev(2), pl.BlockSpec((8, D_CONV), lambda i: (0, 0))],
        out_specs=pl.BlockSpec((tm, D_CONV), lambda i: (i, 0)),
        out_shape=jax.ShapeDtypeStruct((T, D_CONV), BF16),
        compiler_params=_params(("parallel",)),
    )(z, z, z, z, z, conv_w8)


def _rope(x, c, s1, s2):
    n = x.shape[-1]
    return x * c + pltpu.roll(x, 16, 1) * s1 + pltpu.roll(x, n - 16, 1) * s2


def _rope_t(d, c, s1, s2):
    n = d.shape[-1]
    return d * c + pltpu.roll(d * s1, n - 16, 1) + pltpu.roll(d * s2, 16, 1)


def _rms(x, g):
    rstd = lax.rsqrt(jnp.mean(x * x, axis=-1, keepdims=True) + RMS_EPS)
    return x * rstd * g


def _qkv_proj(z, gq, gkv, wq, wk, wv_ext, tabs, *, name, tm):
    T = z.shape[0]

    def body(z_ref, gq_ref, gkv_ref, wq_ref, wk_ref, wv_ref, c_ref, s1_ref, s2_ref,
             q_ref, k_ref, v_ref, qn_ref, kin_ref):
        zz = z_ref[...]
        qn = _rms(zz[:, :Q_LORA], gq_ref[...]).astype(BF16)
        kvn = _rms(zz[:, Q_LORA:Q_LORA + KV_LORA], gkv_ref[...]).astype(BF16)
        kin = jnp.concatenate([kvn, zz[:, Q_LORA + KV_LORA:].astype(BF16)], axis=-1)
        c = jnp.tile(c_ref[...], (1, MLA_HEADS))
        s1 = jnp.tile(s1_ref[...], (1, MLA_HEADS))
        s2 = jnp.tile(s2_ref[...], (1, MLA_HEADS))
        qpre = jnp.dot(qn, wq_ref[...], preferred_element_type=F32)
        kpre = jnp.dot(kin, wk_ref[...], preferred_element_type=F32)
        q_ref[...] = (_rope(qpre, c, s1, s2) * (ATTN_SCALE * LOG2_E)).astype(BF16)
        k_ref[...] = _rope(kpre, c, s1, s2).astype(BF16)
        vv = jnp.dot(kvn, wv_ref[...], preferred_element_type=F32)
        lane = lax.broadcasted_iota(jnp.int32, vv.shape, 1)
        v_ref[...] = jnp.where((lane & (HEAD_PAD - 1)) < V_HEAD, vv, 1.0).astype(BF16)
        qn_ref[...] = qn
        kin_ref[...] = kin

    def full(shape):
        return pl.BlockSpec(shape, lambda i: (0, 0))

    def rows(w, c=0):
        return pl.BlockSpec((tm, w), lambda i: (i, c))

    return pl.pallas_call(
        body, name=name, grid=(T // tm,),
        in_specs=[rows(512, Z_MID // 512), full((1, Q_LORA)), full((1, KV_LORA)),
                  full((Q_LORA, D_QK)), full((Q_LORA, D_QK)), full((KV_LORA, D_QK)),
                  rows(LANE), rows(LANE), rows(LANE)],
        out_specs=[rows(D_QK), rows(D_QK), rows(D_QK), rows(Q_LORA), rows(Q_LORA)],
        out_shape=[jax.ShapeDtypeStruct((T, D_QK), BF16), jax.ShapeDtypeStruct((T, D_QK), BF16),
                   jax.ShapeDtypeStruct((T, D_QK), BF16), jax.ShapeDtypeStruct((T, Q_LORA), BF16),
                   jax.ShapeDtypeStruct((T, Q_LORA), BF16)],
        compiler_params=_params(("parallel",)),
    )(z, gq, gkv, wq, wk, wv_ext, *tabs)


SOFTMAX_ROWS = 32
LOG2_E = 1.4426950408889634
_NT = (((1,), (1,)), ((), ()))
_TN = (((0,), (0,)), ((), ()))


def _diag_mask(s, row0, col0=0):
    row = lax.broadcasted_iota(jnp.int32, s.shape, 0) + row0
    col = lax.broadcasted_iota(jnp.int32, s.shape, 1) + col0
    return jnp.where(col <= row, s, NEG_INF)


_RELATIONS = tuple((rx, ry, rc) for rx in (0, 1) for ry in (0, 1) for rc in (0, 1))[1:]


class _Exchange:
    def __init__(self, kind, arrays):
        assert kind in ("gather", "scatter")
        self.kind, self.arrays, self.n = kind, list(arrays), len(arrays)

    def out_shapes(self):
        if self.kind == "gather":
            return [jax.ShapeDtypeStruct((N_DEV,) + a.shape, a.dtype) for a in self.arrays]
        return [jax.ShapeDtypeStruct((N_DEV - 1,) + a.shape[1:], a.dtype) for a in self.arrays]

    def scratch_shapes(self):
        sems = [pltpu.SemaphoreType.DMA((7 * self.n,)), pltpu.SemaphoreType.DMA((7 * self.n,))]
        if self.kind == "gather":
            sems.append(pltpu.SemaphoreType.DMA((self.n,)))
        return sems

    def _copies(self, src_refs, out_refs, sems):
        x, y, c = lax.axis_index("x"), lax.axis_index("y"), lax.axis_index("c")
        me = 4 * x + 2 * y + c
        sends, recvs, local = [], [], []
        for a in range(self.n):
            for k, rel in enumerate(_RELATIONS):
                peer = tuple((1 - p) if r else p for p, r in zip((x, y, c), rel))
                peer_index = 4 * peer[0] + 2 * peer[1] + peer[2]
                if self.kind == "gather":
                    src, lands_there, lands_here = src_refs[a], out_refs[a].at[me], out_refs[a].at[peer_index]
                else:
                    src, lands_there, lands_here = src_refs[a].at[peer_index], out_refs[a].at[k], out_refs[a].at[k]
                for dst, group in ((lands_there, sends), (lands_here, recvs)):
                    group.append(pltpu.make_async_remote_copy(
                        src_ref=src, dst_ref=dst, send_sem=sems[0].at[7 * a + k], recv_sem=sems[1].at[7 * a + k],
                        device_id=peer, device_id_type=_MESH_ID))
            if self.kind == "gather":
                local.append(pltpu.make_async_copy(src_refs[a], out_refs[a].at[me], sems[2].at[a]))
        return sends, recvs, local

    def start(self, src_refs, out_refs, sems):
        sends, _, local = self._copies(src_refs, out_refs, sems)
        for cp in local + sends:
            cp.start()

    def wait(self, src_refs, out_refs, sems):
        sends, recvs, local = self._copies(src_refs, out_refs, sems)
        for cp in recvs:
            cp.wait_recv()
        for cp in sends:
            cp.wait_send()
        for cp in local:
            cp.wait()


def _exchange_operands(exchange):
    if exchange is None:
        return [], [], [], []
    return exchange.arrays, [_ANY] * exchange.n, exchange.out_shapes(), exchange.scratch_shapes()


def _carry_exchange(body, exchange, n_prefetch, n_in, n_out, grid):
    if exchange is None:
        return body
    n = exchange.n
    last = tuple(g - 1 for g in grid)

    def wrapped(*refs):
        head = refs[:n_prefetch + n_in]
        src_refs = refs[n_prefetch + n_in:n_prefetch + n_in + n]
        rest = refs[n_prefetch + n_in + n:]
        outs, out_refs, rest = rest[:n_out], rest[n_out:n_out + n], rest[n_out + n:]
        n_sems = len(exchange.scratch_shapes())
        scratch, sems = rest[:len(rest) - n_sems], rest[len(rest) - n_sems:]
        at_first = functools.reduce(jnp.logical_and, [pl.program_id(d) == 0 for d in range(len(grid))])
        at_last = functools.reduce(jnp.logical_and, [pl.program_id(d) == last[d] for d in range(len(grid))])

        @pl.when(at_first)
        def _():
            exchange.start(src_refs, out_refs, sems)

        body(*head, *outs, *scratch)

        @pl.when(at_last)
        def _():
            exchange.wait(src_refs, out_refs, sems)

    return wrapped


def _attn_fwd(q, k, v, *, name, blk, exchange=None):
    T = q.shape[0]
    n = T // blk
    hp = HEADS_PER_STEP
    qi = np.array([i for i in range(n) for j in range(i + 1)], np.int32)
    kj = np.array([j for i in range(n) for j in range(i + 1)], np.int32)

    rc = _tile(blk, (SOFTMAX_ROWS,))

    def body(qi_ref, kj_ref, q_ref, k_ref, v_ref, o_ref, lse_ref, m_sc, acc_sc, s_sc, p_sc, red_sc):
        s_id = pl.program_id(1)
        i = qi_ref[s_id]
        j = kj_ref[s_id]

        @pl.when(j == 0)
        def _():
            m_sc[...] = jnp.full(m_sc.shape, NEG_INF, F32)
            acc_sc[...] = jnp.zeros(acc_sc.shape, F32)

        def head_step(hh, diagonal):
            hs = slice(hh * HEAD_PAD, (hh + 1) * HEAD_PAD)
            s_sc[hh] = lax.dot_general(q_ref[:, hs], k_ref[:, hs], _NT, preferred_element_type=F32)
            lanes = [slice(t * LANE, (t + 1) * LANE) for t in range(blk // LANE)]
            for r in range(blk // rc):
                rows = slice(r * rc, (r + 1) * rc)
                s = s_sc[hh, rows, :]
                if diagonal:
                    s = _diag_mask(s, r * rc)
                    s_sc[hh, rows, :] = s
                pm = s[:, lanes[0]]
                for t in lanes[1:]:
                    pm = jnp.maximum(pm, s[:, t])
                red_sc[hh, rows, :] = pm
            m_old = m_sc[hh]
            row_max = jnp.max(red_sc[hh], axis=-1, keepdims=True)
            m_new = jnp.maximum(m_old, jnp.broadcast_to(row_max, (blk, LANE)))
            a = jnp.exp2(m_old - m_new)
            m_sc[hh] = m_new
            for r in range(blk // rc):
                rows = slice(r * rc, (r + 1) * rc)
                mb = m_sc[hh, rows, :]
                for t in lanes:
                    p_sc[hh, rows, t] = jnp.exp2(s_sc[hh, rows, t] - mb).astype(BF16)
            acc_sc[hh] = a * acc_sc[hh] + jnp.dot(p_sc[hh], v_ref[:, hs], preferred_element_type=F32)

        @pl.when(j < i)
        def _():
            for hh in range(hp):
                head_step(hh, False)

        @pl.when(j == i)
        def _():
            for hh in range(hp):
                head_step(hh, True)
            for hh in range(hp):
                acc = acc_sc[hh]
                swapped = pltpu.roll(acc, V_HEAD, 1)
                o_ref[:, hh * V_HEAD:(hh + 1) * V_HEAD] = (acc / swapped)[:, :V_HEAD].astype(BF16)
                lane = lax.broadcasted_iota(jnp.int32, acc.shape, 1)
                denom = jnp.where(lane < V_HEAD, swapped, acc)
                lse_ref[hh] = m_sc[hh] + jnp.log(denom) * LOG2_E

    grid = (MLA_HEADS // hp, len(qi))
    ex_args, ex_specs, ex_out, ex_scratch = _exchange_operands(exchange)
    grid_spec = pltpu.PrefetchScalarGridSpec(
        num_scalar_prefetch=2, grid=grid,
        in_specs=[pl.BlockSpec((blk, hp * HEAD_PAD), lambda g, s, qi, kj: (qi[s], g)),
                  pl.BlockSpec((blk, hp * HEAD_PAD), lambda g, s, qi, kj: (kj[s], g)),
                  pl.BlockSpec((blk, hp * HEAD_PAD), lambda g, s, qi, kj: (kj[s], g))] + ex_specs,
        out_specs=[pl.BlockSpec((blk, hp * V_HEAD), lambda g, s, qi, kj: (qi[s], g)),
                   pl.BlockSpec((hp, blk, LANE), lambda g, s, qi, kj: (g, qi[s], 0))] + ex_specs,
        scratch_shapes=[pltpu.VMEM((hp, blk, LANE), F32), pltpu.VMEM((hp, blk, HEAD_PAD), F32),
                        pltpu.VMEM((hp, blk, blk), F32), pltpu.VMEM((hp, blk, blk), BF16),
                        pltpu.VMEM((hp, blk, LANE), F32)] + ex_scratch)
    return pl.pallas_call(
        _carry_exchange(body, exchange, 2, 3, 2, grid), name=name, grid_spec=grid_spec,
        out_shape=[jax.ShapeDtypeStruct((T, D_MLA), BF16),
                   jax.ShapeDtypeStruct((MLA_HEADS, T, LANE), F32)] + ex_out,
        compiler_params=_params(("arbitrary", "arbitrary") if exchange else ("parallel", "arbitrary")),
    )(jnp.asarray(qi), jnp.asarray(kj), q, k, v, *ex_args)


def _merge(yc, ym, wbc, wbm, z, bg, *, name, tm):
    T = yc.shape[0]

    def body(yc_ref, ym_ref, wbc_ref, wbm_ref, gc_ref, gm_ref, bg_ref, mg_ref, pa_ref, pb_ref):
        pa = jnp.dot(yc_ref[...], wbc_ref[...], preferred_element_type=F32)
        pb = jnp.dot(ym_ref[...], wbm_ref[...], preferred_element_type=F32)
        bgv = bg_ref[...]
        sa = jax.nn.sigmoid(gc_ref[...] + bgv[0:1])
        sb = jax.nn.sigmoid(gm_ref[...] + bgv[1:2])
        mg_ref[...] = (sa * pa + sb * pb).astype(BF16)
        pa_ref[...] = pa.astype(BF16)
        pb_ref[...] = pb.astype(BF16)

    row = pl.BlockSpec((tm, D_MODEL), lambda i: (i, 0))
    return pl.pallas_call(
        body, name=name, grid=(T // tm,),
        in_specs=[pl.BlockSpec((tm, D_CONV), lambda i: (i, 0)), pl.BlockSpec((tm, D_MLA), lambda i: (i, 0)),
                  pl.BlockSpec((D_CONV, D_MODEL), lambda i: (0, 0)), pl.BlockSpec((D_MLA, D_MODEL), lambda i: (0, 0)),
                  pl.BlockSpec((tm, D_MODEL), lambda i: (i, Z_GATE // D_MODEL)),
                  pl.BlockSpec((tm, D_MODEL), lambda i: (i, Z_GATE // D_MODEL + 1)),
                  pl.BlockSpec((8, D_MODEL), lambda i: (0, 0))],
        out_specs=[row, row, row],
        out_shape=[jax.ShapeDtypeStruct((T, D_MODEL), BF16)] * 3,
        compiler_params=_params(("parallel",)),
    )(yc, ym, wbc, wbm, z, z, bg)


def _loss_head(h, tgt, *, t_real, name, tm):
    T = h.shape[0]

    def body(h_ref, t_ref, dy_ref, loss_ref):
        i = pl.program_id(0)
        row = lax.broadcasted_iota(jnp.int32, (tm, 1), 0) + i * tm
        valid = (row >= N_META) & (row < t_real)
        err = jnp.where(valid, h_ref[...] - t_ref[...], 0.0)
        dy_ref[...] = err * (1.0 / D_MODEL)
        part = 0.5 * jnp.sum(jnp.sum(err * err, axis=-1, keepdims=True) * (1.0 / D_MODEL), axis=0, keepdims=True)

        @pl.when(i == 0)
        def _():
            loss_ref[...] = jnp.zeros(loss_ref.shape, F32)

        loss_ref[...] += jnp.broadcast_to(part, loss_ref.shape)

    row_spec = pl.BlockSpec((tm, D_MODEL), lambda i: (i, 0))
    return pl.pallas_call(
        body, name=name, grid=(T // tm,),
        in_specs=[row_spec, row_spec],
        out_specs=[row_spec, pl.BlockSpec((8, LANE), lambda i: (0, 0))],
        out_shape=[jax.ShapeDtypeStruct((T, D_MODEL), F32), jax.ShapeDtypeStruct((8, LANE), F32)],
        compiler_params=_params(("arbitrary",)),
    )(h, tgt)


def _ln_bwd(dy, r, g, *, scale, name, tm):
    T = dy.shape[0]

    def body(dy_ref, r_ref, g_ref, dr_ref, drb_ref, dg_ref, db_ref):
        i = pl.program_id(0)
        rr = r_ref[...]
        dyv = dy_ref[...]
        mu = jnp.mean(rr, axis=-1, keepdims=True)
        xc = rr - mu
        rstd = lax.rsqrt(jnp.mean(xc * xc, axis=-1, keepdims=True) + LN_EPS)
        xh = xc * rstd
        dxh = dyv * g_ref[...]
        m1 = jnp.mean(dxh, axis=-1, keepdims=True)
        m2 = jnp.mean(dxh * xh, axis=-1, keepdims=True)
        dr = rstd * (dxh - m1 - xh * m2)
        dr_ref[...] = dr
        drb_ref[...] = (scale * dr).astype(BF16)

        @pl.when(i == 0)
        def _():
            dg_ref[...] = jnp.zeros(dg_ref.shape, F32)
            db_ref[...] = jnp.zeros(db_ref.shape, F32)

        dg_ref[0:1, :] += jnp.sum(dyv * xh, axis=0, keepdims=True)
        db_ref[0:1, :] += jnp.sum(dyv, axis=0, keepdims=True)

    row = pl.BlockSpec((tm, D_MODEL), lambda i: (i, 0))
    acc = pl.BlockSpec((8, D_MODEL), lambda i: (0, 0))
    return pl.pallas_call(
        body, name=name, grid=(T // tm,),
        in_specs=[row, row, pl.BlockSpec((1, D_MODEL), lambda i: (0, 0))],
        out_specs=[row, row, acc, acc],
        out_shape=[jax.ShapeDtypeStruct((T, D_MODEL), F32), jax.ShapeDtypeStruct((T, D_MODEL), BF16),
                   jax.ShapeDtypeStruct((8, D_MODEL), F32), jax.ShapeDtypeStruct((8, D_MODEL), F32)],
        compiler_params=_params(("arbitrary",)),
    )(dy, r, g)


def _ffn_bwd_mid(dfb, w_down, gu, *, name, tm):
    T = dfb.shape[0]

    def body(df_ref, w_ref, gu_ref, o_ref):
        da = lax.dot_general(df_ref[...], w_ref[...], _NT, preferred_element_type=F32)
        o_ref[0] = (da * gu_ref[0].astype(F32)).astype(BF16)
        o_ref[1] = (da * gu_ref[1].astype(F32)).astype(BF16)

    return pl.pallas_call(
        body, name=name, grid=(FF_HALF_BLOCKS, T // tm),
        in_specs=[pl.BlockSpec((tm, D_MODEL), lambda j, i: (i, 0)),
                  pl.BlockSpec((FF_BLK, D_MODEL), lambda j, i: (j, 0)),
                  pl.BlockSpec((2, None, tm, FF_BLK), lambda j, i: (0, j, i, 0))],
        out_specs=pl.BlockSpec((2, None, tm, FF_BLK), lambda j, i: (0, j, i, 0)),
        out_shape=jax.ShapeDtypeStruct((2, FF_HALF_BLOCKS, T, FF_BLK), BF16),
        compiler_params=_params(("parallel", "parallel")),
    )(dfb, w_down, gu)


def _wo_bwd(dmb, wo_t, z, bg, pa, pb, *, name, tm):
    T = dmb.shape[0]

    def body(dm_ref, w_ref, gc_ref, gm_ref, bg_ref, pa_ref, pb_ref, dpa_ref, dpb_ref, dg_ref, dbg_ref):
        i = pl.program_id(0)
        dm = lax.dot_general(dm_ref[...], w_ref[...], _NT, preferred_element_type=F32)
        bgv = bg_ref[...]
        sa = jax.nn.sigmoid(gc_ref[...] + bgv[0:1])
        sb = jax.nn.sigmoid(gm_ref[...] + bgv[1:2])
        dpa_ref[...] = (dm * sa).astype(BF16)
        dpb_ref[...] = (dm * sb).astype(BF16)
        dga = dm * pa_ref[...].astype(F32) * (sa * (1.0 - sa))
        dgb = dm * pb_ref[...].astype(F32) * (sb * (1.0 - sb))
        dg_ref[:, :D_MODEL] = dga.astype(BF16)
        dg_ref[:, D_MODEL:] = dgb.astype(BF16)

        @pl.when(i == 0)
        def _():
            dbg_ref[...] = jnp.zeros(dbg_ref.shape, F32)

        dbg_ref[0:1, :] += jnp.sum(dga, axis=0, keepdims=True)
        dbg_ref[1:2, :] += jnp.sum(dgb, axis=0, keepdims=True)

    row = pl.BlockSpec((tm, D_MODEL), lambda i: (i, 0))
    return pl.pallas_call(
        body, name=name, grid=(T // tm,),
        in_specs=[row, pl.BlockSpec((D_MODEL, D_MODEL), lambda i: (0, 0)),
                  pl.BlockSpec((tm, D_MODEL), lambda i: (i, Z_GATE // D_MODEL)),
                  pl.BlockSpec((tm, D_MODEL), lambda i: (i, Z_GATE // D_MODEL + 1)),
                  pl.BlockSpec((8, D_MODEL), lambda i: (0, 0)), row, row],
        out_specs=[row, row, pl.BlockSpec((tm, 2 * D_MODEL), lambda i: (i, 0)),
                   pl.BlockSpec((8, D_MODEL), lambda i: (0, 0))],
        out_shape=[jax.ShapeDtypeStruct((T, D_MODEL), BF16), jax.ShapeDtypeStruct((T, D_MODEL), BF16),
                   jax.ShapeDtypeStruct((T, 2 * D_MODEL), BF16), jax.ShapeDtypeStruct((8, D_MODEL), F32)],
        compiler_params=_params(("arbitrary",)),
    )(dmb, wo_t, z, z, bg, pa, pb)


def _conv_bwd(dy, z, conv_w8, *, name, tm):
    T = dy.shape[0]
    n = T // tm
    hb = tm // 8

    def body(dy_ref, b_ref, c_ref, h_ref, cp_ref, hp_ref, dyn_ref, bn_ref, w_ref, dz_ref, dw_ref):
        i = pl.program_id(0)
        u = c_ref[...] * h_ref[...]
        up = jnp.where(i > 0, cp_ref[...] * hp_ref[...], 0.0)
        ue = jnp.concatenate([up, u], axis=0)
        s1 = pltpu.roll(ue, 1, 0)[8:]
        s2 = pltpu.roll(ue, 2, 0)[8:]
        w = w_ref[...]
        conv = w[0:1] * s2 + w[1:2] * s1 + w[2:3] * u
        dyv = dy_ref[...]
        e = dyv * b_ref[...]
        en = jnp.where(i < n - 1, dyn_ref[...] * bn_ref[...], 0.0)
        ee = jnp.concatenate([e, en], axis=0)
        e1 = pltpu.roll(ee, tm + 8 - 1, 0)[:tm]
        e2 = pltpu.roll(ee, tm + 8 - 2, 0)[:tm]
        du = w[2:3] * e + w[1:2] * e1 + w[0:1] * e2
        dz_ref[:, 0:D_CONV] = (dyv * conv).astype(BF16)
        dz_ref[:, D_CONV:2 * D_CONV] = (du * h_ref[...]).astype(BF16)
        dz_ref[:, 2 * D_CONV:] = (du * c_ref[...]).astype(BF16)

        @pl.when(i == 0)
        def _():
            dw_ref[...] = jnp.zeros(dw_ref.shape, F32)

        dw_ref[0:1, :] += jnp.sum(e * s2, axis=0, keepdims=True)
        dw_ref[1:2, :] += jnp.sum(e * s1, axis=0, keepdims=True)
        dw_ref[2:3, :] += jnp.sum(e * u, axis=0, keepdims=True)

    def col(c):
        return pl.BlockSpec((tm, D_CONV), lambda i: (i, c))

    def prev(c):
        return pl.BlockSpec((8, D_CONV), lambda i: (jnp.maximum(i * hb - 1, 0), c))

    def nxt(c):
        return pl.BlockSpec((8, D_CONV), lambda i: (jnp.minimum((i + 1) * hb, T // 8 - 1), c))

    return pl.pallas_call(
        body, name=name, grid=(n,),
        in_specs=[col(0), col(0), col(1), col(2), prev(1), prev(2), nxt(0), nxt(0),
                  pl.BlockSpec((8, D_CONV), lambda i: (0, 0))],
        out_specs=[pl.BlockSpec((tm, 3 * D_CONV), lambda i: (i, 0)), pl.BlockSpec((8, D_CONV), lambda i: (0, 0))],
        out_shape=[jax.ShapeDtypeStruct((T, 3 * D_CONV), BF16), jax.ShapeDtypeStruct((8, D_CONV), F32)],
        compiler_params=_params(("arbitrary",)),
    )(dy, z, z, z, z, z, dy, z, conv_w8)


def _attn_bwd(q, k, v, o, do, lse, *, name, blk, exchange=None):
    T = q.shape[0]
    n = T // blk
    hp = HEADS_PER_STEP
    qi = np.array([i for j in range(n) for i in range(j, n)], np.int32)
    kj = np.array([j for j in range(n) for i in range(j, n)], np.int32)

    rc = _tile(blk, (SOFTMAX_ROWS,))

    def body(qi_ref, kj_ref, q_ref, k_ref, v_ref, o_ref, do_ref, lse_ref, dq_ref, dk_ref, dv_ref,
             dk_sc, dv_sc, s_sc, dp_sc, p_sc, ds_sc, delta_sc):
        s_id = pl.program_id(1)
        i = qi_ref[s_id]
        j = kj_ref[s_id]

        @pl.when(s_id == 0)
        def _():
            dq_ref[...] = jnp.zeros(dq_ref.shape, F32)

        @pl.when(i == j)
        def _():
            dk_sc[...] = jnp.zeros(dk_sc.shape, F32)
            dv_sc[...] = jnp.zeros(dv_sc.shape, F32)

        q_rows = pl.ds(pl.multiple_of(i * blk, blk), blk)

        def head_step(hh, diagonal):
            hs = slice(hh * HEAD_PAD, (hh + 1) * HEAD_PAD)
            vs = slice(hh * V_HEAD, (hh + 1) * V_HEAD)
            qh = q_ref[:, hs]
            kh = k_ref[:, hs]
            doh = do_ref[:, vs]
            s_sc[hh] = lax.dot_general(qh, kh, _NT, preferred_element_type=F32)
            vh = v_ref[:, hh * HEAD_PAD:hh * HEAD_PAD + V_HEAD]
            dp_sc[hh] = lax.dot_general(doh, vh, _NT, preferred_element_type=F32)
            delta = jnp.sum(doh.astype(F32) * o_ref[:, vs].astype(F32), axis=-1, keepdims=True)
            delta_sc[hh] = jnp.broadcast_to(delta, (blk, LANE))
            for r in range(blk // rc):
                rows = slice(r * rc, (r + 1) * rc)
                lse = lse_ref[hh, rows, :]
                dl = delta_sc[hh, rows, :]
                for t in range(blk // LANE):
                    cols = slice(t * LANE, (t + 1) * LANE)
                    s = s_sc[hh, rows, cols]
                    if diagonal:
                        s = _diag_mask(s, r * rc, t * LANE)
                    p = jnp.exp2(s - lse)
                    p_sc[hh, rows, cols] = p.astype(BF16)
                    ds_sc[hh, rows, cols] = (p * (dp_sc[hh, rows, cols] - dl)).astype(BF16)
            dv_sc[hh] += lax.dot_general(p_sc[hh], doh, _TN, preferred_element_type=F32)
            dk_sc[hh] += lax.dot_general(ds_sc[hh], qh, _TN, preferred_element_type=F32)
            dq_ref[q_rows, hs] += jnp.dot(ds_sc[hh], kh, preferred_element_type=F32)

        @pl.when(j < i)
        def _():
            for hh in range(hp):
                head_step(hh, False)

        @pl.when(j == i)
        def _():
            for hh in range(hp):
                head_step(hh, True)

        @pl.when(i == n - 1)
        def _():
            for hh in range(hp):
                dk_ref[:, hh * HEAD_PAD:(hh + 1) * HEAD_PAD] = dk_sc[hh] * (1.0 / LOG2_E)
                dv_ref[:, hh * V_HEAD:(hh + 1) * V_HEAD] = dv_sc[hh]

    wq = hp * HEAD_PAD
    wv = hp * V_HEAD
    grid = (MLA_HEADS // hp, len(qi))
    ex_args, ex_specs, ex_out, ex_scratch = _exchange_operands(exchange)
    grid_spec = pltpu.PrefetchScalarGridSpec(
        num_scalar_prefetch=2, grid=grid,
        in_specs=[pl.BlockSpec((blk, wq), lambda g, s, qi, kj: (qi[s], g)),
                  pl.BlockSpec((blk, wq), lambda g, s, qi, kj: (kj[s], g)),
                  pl.BlockSpec((blk, wq), lambda g, s, qi, kj: (kj[s], g)),
                  pl.BlockSpec((blk, wv), lambda g, s, qi, kj: (qi[s], g)),
                  pl.BlockSpec((blk, wv), lambda g, s, qi, kj: (qi[s], g)),
                  pl.BlockSpec((hp, blk, LANE), lambda g, s, qi, kj: (g, qi[s], 0))] + ex_specs,
        out_specs=[pl.BlockSpec((T, wq), lambda g, s, qi, kj: (0, g)),
                   pl.BlockSpec((blk, wq), lambda g, s, qi, kj: (kj[s], g)),
                   pl.BlockSpec((blk, wv), lambda g, s, qi, kj: (kj[s], g))] + ex_specs,
        scratch_shapes=[pltpu.VMEM((hp, blk, HEAD_PAD), F32), pltpu.VMEM((hp, blk, V_HEAD), F32),
                        pltpu.VMEM((hp, blk, blk), F32), pltpu.VMEM((hp, blk, blk), F32),
                        pltpu.VMEM((hp, blk, blk), BF16), pltpu.VMEM((hp, blk, blk), BF16),
                        pltpu.VMEM((hp, blk, LANE), F32)] + ex_scratch)
    return pl.pallas_call(
        _carry_exchange(body, exchange, 2, 6, 3, grid), name=name, grid_spec=grid_spec,
        out_shape=[jax.ShapeDtypeStruct((T, D_QK), F32), jax.ShapeDtypeStruct((T, D_QK), F32),
                   jax.ShapeDtypeStruct((T, D_MLA), F32)] + ex_out,
        compiler_params=_params(("arbitrary", "arbitrary") if exchange else ("parallel", "arbitrary")),
    )(jnp.asarray(qi), jnp.asarray(kj), q, k, v, o, do, lse, *ex_args)


def _qkv_bwd(dq, dk, dv, z, gq, gkv, wq_t, wk_t, wv_t, tabs, *, name, tm):
    T = dq.shape[0]

    def body(dq_ref, dk_ref, dv_ref, z_ref, gq_ref, gkv_ref, wq_ref, wk_ref, wv_ref, c_ref, s1_ref, s2_ref,
             dz_ref, dqb_ref, dkb_ref, dvb_ref, dgq_ref, dgkv_ref):
        i = pl.program_id(0)
        c = jnp.tile(c_ref[...], (1, MLA_HEADS))
        s1 = jnp.tile(s1_ref[...], (1, MLA_HEADS))
        s2 = jnp.tile(s2_ref[...], (1, MLA_HEADS))
        dqp = _rope_t(dq_ref[...] * ATTN_SCALE, c, s1, s2).astype(BF16)
        dkp = _rope_t(dk_ref[...], c, s1, s2).astype(BF16)
        dvb = dv_ref[...].astype(BF16)
        dqb_ref[...] = dqp
        dkb_ref[...] = dkp
        dvb_ref[...] = dvb
        dqn = lax.dot_general(dqp, wq_ref[...], _NT, preferred_element_type=F32)
        dkin = lax.dot_general(dkp, wk_ref[...], _NT, preferred_element_type=F32)
        dkvn = dkin[:, :KV_LORA] + lax.dot_general(dvb, wv_ref[...], _NT, preferred_element_type=F32)
        zz = z_ref[...]

        def rms_bwd(x, g, dy):
            rstd = lax.rsqrt(jnp.mean(x * x, axis=-1, keepdims=True) + RMS_EPS)
            xh = x * rstd
            dxh = dy * g
            dx = rstd * (dxh - xh * jnp.mean(dxh * xh, axis=-1, keepdims=True))
            return dx, jnp.sum(dy * xh, axis=0, keepdims=True)

        dcq, dgq = rms_bwd(zz[:, :Q_LORA], gq_ref[...], dqn)
        dckv, dgkv = rms_bwd(zz[:, Q_LORA:Q_LORA + KV_LORA], gkv_ref[...], dkvn)
        dz_ref[:, :Q_LORA] = dcq.astype(BF16)
        dz_ref[:, Q_LORA:Q_LORA + KV_LORA] = dckv.astype(BF16)
        dz_ref[:, Q_LORA + KV_LORA:] = dkin[:, KV_LORA:].astype(BF16)

        @pl.when(i == 0)
        def _():
            dgq_ref[...] = jnp.zeros(dgq_ref.shape, F32)
            dgkv_ref[...] = jnp.zeros(dgkv_ref.shape, F32)

        dgq_ref[0:1, :] += dgq
        dgkv_ref[0:1, :] += dgkv

    def full(shape):
        return pl.BlockSpec(shape, lambda i: (0, 0))

    def rows(w, c=0):
        return pl.BlockSpec((tm, w), lambda i: (i, c))

    return pl.pallas_call(
        body, name=name, grid=(T // tm,),
        in_specs=[rows(D_QK), rows(D_QK), rows(D_MLA), rows(512, Z_MID // 512),
                  full((1, Q_LORA)), full((1, KV_LORA)),
                  full((Q_LORA, D_QK)), full((Q_LORA, D_QK)), full((KV_LORA, D_MLA)),
                  rows(LANE), rows(LANE), rows(LANE)],
        out_specs=[rows(512), rows(D_QK), rows(D_QK), rows(D_MLA), full((8, Q_LORA)), full((8, KV_LORA))],
        out_shape=[jax.ShapeDtypeStruct((T, 512), BF16), jax.ShapeDtypeStruct((T, D_QK), BF16),
                   jax.ShapeDtypeStruct((T, D_QK), BF16), jax.ShapeDtypeStruct((T, D_MLA), BF16),
                   jax.ShapeDtypeStruct((8, Q_LORA), F32), jax.ShapeDtypeStruct((8, KV_LORA), F32)],
        compiler_params=_params(("arbitrary",)),
    )(dq, dk, dv, z, gq, gkv, wq_t, wk_t, wv_t, *tabs)


FLAT_W = 1024


ELEMENTWISE_TILE_BYTES = 768 * 1024


def _row_tile(R, C):
    width = -(-C // LANE) * LANE * 4
    best = None
    for t in range(16, R + 1, 16):
        if R % t == 0 and t * width <= ELEMENTWISE_TILE_BYTES:
            best = t
    if best is None:
        best = R
    return best


def _adamw(w, m, v, layer, parts, part_index, *, name):
    _, R, C = w.shape
    tr = _row_tile(R, C)
    bc1 = 1.0 - ADAM_B1 ** ADAM_STEP
    bc2 = 1.0 - ADAM_B2 ** ADAM_STEP
    n_parts = len(parts)

    def body(idx_ref, w_ref, m_ref, v_ref, *refs):
        g_refs = refs[:n_parts]
        g_out, d_out, m_out, v_out = refs[n_parts:]
        g = g_refs[0][...].astype(F32)
        for r in g_refs[1:]:
            g = g + r[...].astype(F32)
        wv = w_ref[...]
        mn = ADAM_B1 * m_ref[...] + (1.0 - ADAM_B1) * g
        vn = ADAM_B2 * v_ref[...] + (1.0 - ADAM_B2) * (g * g)
        m_hat = mn / bc1
        v_hat = vn / bc2
        g_out[...] = g
        d_out[...] = -ADAM_LR * (m_hat / (jnp.sqrt(v_hat) + ADAM_EPS) + ADAM_WD * wv)
        m_out[...] = mn
        v_out[...] = vn

    layer_row = pl.BlockSpec((None, tr, C), lambda i, idx: (layer, i, 0))
    in_specs = [layer_row, layer_row, layer_row]
    args = [w, m, v]
    for arr, slot in parts:
        if slot is None:
            in_specs.append(pl.BlockSpec((None, tr, C), lambda i, idx: (idx[0], i, 0)))
        else:
            in_specs.append(pl.BlockSpec((None, tr, C), lambda i, idx, slot=slot: (slot, i, 0)))
        args.append(arr)
    grid_spec = pltpu.PrefetchScalarGridSpec(
        num_scalar_prefetch=1, grid=(R // tr,), in_specs=in_specs,
        out_specs=[pl.BlockSpec((tr, C), lambda i, idx: (i, 0))] * 4)
    return pl.pallas_call(
        body, name=name, grid_spec=grid_spec,
        out_shape=[jax.ShapeDtypeStruct((R, C), F32)] * 4,
        compiler_params=_params(("parallel",)),
    )(part_index, *args)


def _sum8(parts, *, name):
    _, R, _ = parts.shape

    def body(p_ref, o_ref):
        acc = p_ref[0]
        for d in range(1, N_DEV):
            acc = acc + p_ref[d]
        o_ref[...] = acc

    return pl.pallas_call(
        body, name=name, grid=(1,),
        in_specs=[pl.BlockSpec((N_DEV, R, FLAT_W), lambda i: (0, 0, 0))],
        out_specs=pl.BlockSpec((R, FLAT_W), lambda i: (0, 0)),
        out_shape=jax.ShapeDtypeStruct((R, FLAT_W), F32),
        compiler_params=_params(("arbitrary",)),
    )(parts)


_MESH_ID = pl.DeviceIdType.MESH
_ANY = pl.BlockSpec(memory_space=pl.ANY)


def _all_gather(shards, *, name):
    n = len(shards)

    def body(*refs):
        x_refs, out_refs = refs[:n], refs[n:2 * n]
        send_sems, recv_sems, local_sems = refs[2 * n:]
        x, y, c = lax.axis_index("x"), lax.axis_index("y"), lax.axis_index("c")
        me, sibling = (x, y, c), (x, y, 1 - c)
        chips = [(1 - x, y), (x, 1 - y), (1 - x, 1 - y)]

        def blk(a, px, py, pc):
            return out_refs[a].at[4 * px + 2 * py + pc]

        def copy(a, k, block, to, src=None):
            return pltpu.make_async_remote_copy(
                src_ref=blk(a, *block) if src is None else src, dst_ref=blk(a, *block),
                send_sem=send_sems.at[7 * a + k], recv_sem=recv_sems.at[7 * a + k],
                device_id=to, device_id_type=_MESH_ID)

        mine = [pltpu.make_async_copy(x_refs[a], blk(a, *me), local_sems.at[a]) for a in range(n)]
        for cp in mine:
            cp.start()
        first = []
        for a in range(n):
            first.append(copy(a, 0, me, sibling, src=x_refs[a]))
            first += [copy(a, 1 + j, me, (*chip, c), src=x_refs[a]) for j, chip in enumerate(chips)]
        for cp in first:
            cp.start()
        passed = []
        for j, chip in enumerate(chips):
            for a in range(n):
                copy(a, 1 + j, (*chip, c), me).wait_recv()
                fwd = copy(a, 4 + j, (*chip, c), sibling)
                fwd.start()
                passed.append(fwd)
        for a in range(n):
            copy(a, 0, sibling, me).wait_recv()
        for j, chip in enumerate(chips):
            for a in range(n):
                copy(a, 4 + j, (*chip, 1 - c), me).wait_recv()
        for cp in first + passed:
            cp.wait_send()
        for cp in mine:
            cp.wait()

    return pl.pallas_call(
        body, name=name,
        out_shape=[jax.ShapeDtypeStruct((N_DEV,) + s.shape, s.dtype) for s in shards],
        in_specs=[_ANY] * n, out_specs=[_ANY] * n,
        scratch_shapes=[pltpu.SemaphoreType.DMA((7 * n,)), pltpu.SemaphoreType.DMA((7 * n,)),
                        pltpu.SemaphoreType.DMA((n,))],
    )(*shards)


_BIG = (("ffn1_w_up", 2), ("ffn1_w_down", 1), ("mix_w_in", 2), ("w_uq", 2), ("w_ukv", 2),
        ("w_br_conv", 2), ("w_br_mla", 2), ("w_o", 1), ("ffn2_w_up", 2), ("ffn2_w_down", 1))
_SMALL_SHARDED = (("meta_tokens", 1), ("mix_b_gate", 2), ("conv_w", 2), ("ln_g", 2), ("ln_b", 2))
_SMALL_REPL = ("q_norm_g", "kv_norm_g")


def _pack(arrs, dtype, row_align=8):
    flat = jnp.concatenate([a.reshape(-1).astype(dtype) for a in arrs])
    n = flat.shape[0]
    rows = -(-n // (row_align * FLAT_W)) * row_align
    return jnp.pad(flat, (0, rows * FLAT_W - n)).reshape(rows, FLAT_W)


def _unpack(flat, shapes):
    flat = flat.reshape(-1)
    out, off = [], 0
    for s in shapes:
        n = int(np.prod(s))
        out.append(flat[off:off + n].reshape(s))
        off += n
    return out


def _unpack_gathered(gathered, shapes, axes):
    g2 = gathered.reshape(N_DEV, -1)
    out, off = [], 0
    for s, ax in zip(shapes, axes):
        n = int(np.prod(s))
        blocks = g2[:, off:off + n].reshape((N_DEV,) + tuple(s))
        full = jnp.moveaxis(blocks, 0, ax)
        out.append(full.reshape(tuple(s[:ax]) + (N_DEV * s[ax],) + tuple(s[ax + 1:])))
        off += n
    return out


def _to_dest(full, ax):
    s = full.shape
    split = full.reshape(s[:ax] + (N_DEV, s[ax] // N_DEV) + s[ax + 1:])
    return jnp.moveaxis(split, ax, 0)


def _from_blocks(g, ax):
    full = jnp.moveaxis(g, 0, ax)
    s = full.shape
    return full.reshape(s[:ax] + (s[ax] * s[ax + 1],) + s[ax + 2:])


def _rope_tables(T):
    inv_freq = 1.0 / (ROPE_BASE ** (jnp.arange(0, QK_ROPE, 2, dtype=F32) / QK_ROPE))
    ang = jnp.arange(T, dtype=F32)[:, None] * inv_freq[None, :]
    cos, sin = jnp.cos(ang), jnp.sin(ang)
    half = QK_ROPE // 2
    ones = jnp.ones((T, QK_NOPE), F32)
    zeros = lambda w: jnp.zeros((T, w), F32)
    c = jnp.concatenate([ones, cos, cos, zeros(HEAD_PAD - QK_NOPE - QK_ROPE)], axis=1)
    s1 = jnp.concatenate([zeros(QK_NOPE + half), sin, zeros(HEAD_PAD - QK_NOPE - QK_ROPE)], axis=1)
    s2 = jnp.concatenate([zeros(QK_NOPE), -sin, zeros(HEAD_PAD - QK_NOPE - half)], axis=1)
    return c, s1, s2


_FFN1 = ("ffn1_w_up", "ffn1_w_down")
_MIXER = ("mix_w_in", "w_uq", "w_ukv", "w_br_conv", "w_br_mla", "w_o")
_FFN2 = ("ffn2_w_up", "ffn2_w_down")


def _mixer_weights(gathered):
    axes = dict(_BIG)
    W = {n: _from_blocks(gathered[n], axes[n] - 1) for n in _MIXER}
    w_in = W["mix_w_in"]
    w_in_p = jnp.concatenate([w_in[:, :Z_KR_END], jnp.zeros((D_MODEL, D_IN_PAD - D_IN_REAL), BF16),
                              w_in[:, Z_KR_END:]], axis=1)
    w_uq = W["w_uq"].reshape(Q_LORA, MLA_HEADS, QK_NOPE + QK_ROPE)
    wq = jnp.pad(w_uq, ((0, 0), (0, 0), (0, HEAD_PAD - QK_NOPE - QK_ROPE))).reshape(Q_LORA, D_QK)
    w_ukv = W["w_ukv"].reshape(KV_LORA, MLA_HEADS, QK_NOPE + V_HEAD)
    wk_top = jnp.pad(w_ukv[:, :, :QK_NOPE], ((0, 0), (0, 0), (0, HEAD_PAD - QK_NOPE))).reshape(KV_LORA, D_QK)
    place = np.zeros((Q_LORA - KV_LORA, MLA_HEADS, HEAD_PAD), np.float32)
    for r in range(QK_ROPE):
        place[r, :, QK_NOPE + r] = 1.0
    wk = jnp.concatenate([wk_top, jnp.asarray(place.reshape(Q_LORA - KV_LORA, D_QK), BF16)], axis=0)
    wv = w_ukv[:, :, QK_NOPE:].reshape(KV_LORA, D_MLA)
    wv_ext = jnp.pad(w_ukv[:, :, QK_NOPE:], ((0, 0), (0, 0), (0, HEAD_PAD - V_HEAD))).reshape(KV_LORA, D_QK)
    return dict(w_in=w_in_p, wq=wq, wk=wk, wv=wv, wv_ext=wv_ext,
                wbc=W["w_br_conv"], wbm=W["w_br_mla"], wo=W["w_o"])


def _row8(v):
    return jnp.pad(v, ((0, 8 - v.shape[0]), (0, 0)))


def _local_step(x, tgt, gathered_ffn1, w_blocks, S):
    big_names = [n for n, _ in _BIG]
    axes = dict(_BIG)
    t_real = N_META + x.shape[0]
    T = -(-t_real // ROW_ALIGN) * ROW_ALIGN
    pad = T - t_real
    tm = _tile(T, (384, 256, 128))
    tms = _tile(T, (768, 256, 128))
    blk = _tile(T, (768, 256, 128))
    tabs = _rope_tables(T)

    h0 = jnp.concatenate([S["meta_tokens"], x, jnp.zeros((pad, D_MODEL), F32)], axis=0)
    tgt_p = jnp.concatenate([jnp.zeros((N_META, D_MODEL), F32), tgt, jnp.zeros((pad, D_MODEL), F32)], axis=0)

    def gather_of(l, names):
        return _Exchange("gather", [w_blocks[l][n] for n in names])

    def ffn_fwd(h, hb, up8, down, g, b, tag, exchange=None):
        gu, a, *got = _ffn_up(hb, up8, name=f"ffn_up_{tag}", tm=tms, exchange=exchange)
        r, y, yb = _mm_res_ln(a, down, h, g, b, scale=0.5, name=f"ffn_down_ln_{tag}", tm=tm)
        return dict(gu=gu, a=a, r=r), y, yb, got

    saved = []
    h, hb = h0, h0.astype(BF16)
    G = dict(gathered_ffn1)
    for l in range(DEPTH):
        first = l == 0
        sv = dict(h_in=h, h_in_b=hb, up1=G["ffn1_w_up"], down1=_from_blocks(G["ffn1_w_down"], 0))
        sv["f1"], h1, h1b, got = ffn_fwd(h, hb, sv["up1"], sv["down1"], S["ln_g"][l, 0:1], S["ln_b"][l, 0:1],
                                         f"{l}a", exchange=gather_of(0, _MIXER) if first else None)
        G.update(zip(_MIXER, got))
        lw = _mixer_weights(G)
        sv["lw"] = lw
        if first:
            z, *got = _mm(h1b, lw["w_in"], out_dtype=F32, name=f"mix_in_{l}", tm=tms, tn=1024, tk=D_MODEL,
                          exchange=gather_of(0, _FFN2))
            G.update(zip(_FFN2, got))
        else:
            z = _mm(h1b, lw["w_in"], out_dtype=F32, name=f"mix_in_{l}", tm=tms, tn=1024, tk=D_MODEL)
        sv.update(up2=G["ffn2_w_up"], down2=_from_blocks(G["ffn2_w_down"], 0))
        conv_w8 = _row8(S["conv_w"][l])
        bg8 = _row8(S["mix_b_gate"][l])
        yc = _conv_fwd(z, conv_w8, name=f"conv_fwd_{l}", tm=tms)
        gq, gkv = S["q_norm_g"][l:l + 1], S["kv_norm_g"][l:l + 1]
        q, k, v, qn, kin = _qkv_proj(z, gq, gkv, lw["wq"], lw["wk"], lw["wv_ext"], tabs, name=f"qkv_proj_{l}", tm=tm)
        o, lse, *nxt = _attn_fwd(q, k, v, name=f"attn_fwd_{l}", blk=blk,
                                 exchange=gather_of(1, big_names) if first else None)
        mg, pa, pb = _merge(yc, o, lw["wbc"], lw["wbm"], z, bg8, name=f"merge_{l}", tm=tm)
        r2, h2, h2b = _mm_res_ln(mg, lw["wo"], h1, S["ln_g"][l, 1:2], S["ln_b"][l, 1:2], scale=1.0,
                                 name=f"wo_ln_{l}", tm=tm)
        sv["f2"], h3, h3b, _ = ffn_fwd(h2, h2b, sv["up2"], sv["down2"],
                                       S["ln_g"][l, 2:3], S["ln_b"][l, 2:3], f"{l}b")
        if first:
            G = dict(zip(big_names, nxt))
        sv.update(h1b=h1b, z=z, conv_w8=conv_w8, bg8=bg8, yc=yc, gq=gq, gkv=gkv, q=q, k=k, v=v, qn=qn, kin=kin,
                  o=o, lse=lse, mg=mg, pa=pa, pb=pb, r2=r2, h2b=h2b)
        saved.append(sv)
        h, hb = h3, h3b

    dh, loss8 = _loss_head(h, tgt_p, t_real=t_real, name="loss_head", tm=tm)

    tk = _tile(T, (2816, 768, 256, 128))
    grads = {n: [None] * DEPTH for n, _ in _BIG}
    for n in ("mix_b_gate", "conv_w", "q_norm_g", "kv_norm_g"):
        grads[n] = [None] * DEPTH
    grads["ln_g"] = [[None] * 3 for _ in range(DEPTH)]
    grads["ln_b"] = [[None] * 3 for _ in range(DEPTH)]

    def dest_of(l, names):
        return {n: grads[n][l] if n.endswith("w_up") else _to_dest(grads[n][l], axes[n] - 1) for n in names}

    def scatter_of(dest):
        return _Exchange("scatter", [dest[n].astype(BF16) for n in dest])

    dest = [{}, {}]
    received = [{}, {}]

    def ffn_bwd(dy, f, h_in_b, up8, down, g, tag, names, l, carry=None):
        dr, dfb, dg, db = _ln_bwd(dy, f["r"], g, scale=0.5, name=f"ln_bwd_{tag}", tm=tm)
        dgu = _ffn_bwd_mid(dfb, down, f["gu"], name=f"ffn_bwd_mid_{tag}", tm=tms)
        grads[names[1]][l] = _mm_tn_call(
            f["a"], dfb,
            pl.BlockSpec((None, tk, FF_BLK), lambda i, j, k: (i, k, 0)),
            pl.BlockSpec((tk, D_MODEL), lambda i, j, k: (k, 0)),
            out_shape=jax.ShapeDtypeStruct((D_FF, D_MODEL), F32),
            out_spec=pl.BlockSpec((FF_BLK, D_MODEL), lambda i, j, k: (i, 0)),
            grid=(FF_HALF_BLOCKS, 1, T // tk), name=f"dw_down_{tag}")
        d_up = _mm_tn_call(
            h_in_b, dgu,
            pl.BlockSpec((tk, D_MODEL), lambda i, j, k: (k, 0)),
            pl.BlockSpec((None, None, tk, FF_BLK),
                         lambda i, j, k: (j // FF_HALF_BLOCKS, j % FF_HALF_BLOCKS, k, 0)),
            out_shape=jax.ShapeDtypeStruct((N_DEV, D_MODEL, FF_BLK), F32),
            out_spec=pl.BlockSpec((None, D_MODEL, FF_BLK), lambda i, j, k: (j, 0, 0)),
            grid=(1, N_DEV, T // tk), name=f"dw_up_{tag}",
            exchange=scatter_of(carry) if carry else None)
        own = None
        if carry:
            d_up, *got = d_up
            received[l].update(zip(carry, got))
            grads[names[0]][l] = d_up
            own = dest_of(l, names)
            dest[l].update(own)
        grads[names[0]][l] = d_up
        row = pl.BlockSpec((tms, D_MODEL), lambda i, j, k: (i, 0))
        dh_in = _mm_call(
            dgu, up8,
            pl.BlockSpec((None, FF_HALF_BLOCKS, tms, FF_BLK), lambda i, j, k: (k, 0, i, 0)),
            pl.BlockSpec((FF_HALF_BLOCKS, D_MODEL, FF_BLK), lambda i, j, k: (k, 0, 0)),
            out_shape=jax.ShapeDtypeStruct((T, D_MODEL), F32), out_spec=row, acc_shape=(tms, D_MODEL),
            grid=(T // tms, 1, 2), name=f"ffn_dx_{tag}", trans_b=True, res=dr, res_spec=row,
            res_scale=ALPHA, pieces=FF_HALF_BLOCKS, exchange=scatter_of(own) if own else None)
        if own:
            dh_in, *got = dh_in
            received[l].update(zip(own, got))
        return dh_in, dg[0], db[0]

    for l in reversed(range(DEPTH)):
        sv = saved[l]
        lw = sv["lw"]
        dh, grads["ln_g"][l][2], grads["ln_b"][l][2] = ffn_bwd(
            dh, sv["f2"], sv["h2b"], sv["up2"], sv["down2"], S["ln_g"][l, 2:3], f"{l}b", _FFN2, l)
        dr2, dmb, dg, db = _ln_bwd(dh, sv["r2"], S["ln_g"][l, 1:2], scale=1.0, name=f"ln_bwd_{l}m", tm=tm)
        grads["ln_g"][l][1], grads["ln_b"][l][1] = dg[0], db[0]
        grads["w_o"][l] = _mm_tn(sv["mg"], dmb, name=f"dw_o_{l}", tm=D_MODEL, tn=D_MODEL, tk=tk)
        dpa, dpb, dgate, dbg = _wo_bwd(dmb, lw["wo"], sv["z"], sv["bg8"], sv["pa"], sv["pb"], name=f"wo_bwd_{l}", tm=tm)
        grads["mix_b_gate"][l] = dbg[0:2]
        grads["w_br_conv"][l] = _mm_tn(sv["yc"], dpa, name=f"dw_br_conv_{l}", tm=D_CONV, tn=D_MODEL, tk=tk)
        grads["w_br_mla"][l] = _mm_tn(sv["o"], dpb, name=f"dw_br_mla_{l}", tm=D_MLA, tn=D_MODEL, tk=tk)
        dyc = _mm(dpa, lw["wbc"], trans_b=True, out_dtype=F32, name=f"d_yconv_{l}", tm=tms, tn=D_CONV, tk=D_MODEL)
        dym = _mm(dpb, lw["wbm"], trans_b=True, out_dtype=BF16, name=f"d_ymla_{l}", tm=tms, tn=D_MLA, tk=D_MODEL)
        dz_conv, dcw = _conv_bwd(dyc, sv["z"], sv["conv_w8"], name=f"conv_bwd_{l}", tm=tms)
        grads["conv_w"][l] = dcw[0:CONV_WIDTH]
        if l == 0:
            dest[1] = dest_of(1, big_names)
            dest[0].update(dest_of(0, _FFN2))
            sent = {(1, n): dest[1][n] for n in big_names}
            sent.update({(0, n): dest[0][n] for n in _FFN2})
            dq, dk, dv, *got = _attn_bwd(
                sv["q"], sv["k"], sv["v"], sv["o"], dym, sv["lse"], name=f"attn_bwd_{l}", blk=blk,
                exchange=scatter_of(sent))
            for (layer, n), r in zip(sent, got):
                received[layer][n] = r
        else:
            dq, dk, dv = _attn_bwd(sv["q"], sv["k"], sv["v"], sv["o"], dym, sv["lse"], name=f"attn_bwd_{l}", blk=blk)
        dz_mid, dqb, dkb, dvb, dgq, dgkv = _qkv_bwd(dq, dk, dv, sv["z"], sv["gq"], sv["gkv"], lw["wq"], lw["wk"],
                                                    lw["wv"], tabs, name=f"qkv_bwd_{l}", tm=tm)
        grads["q_norm_g"][l], grads["kv_norm_g"][l] = dgq[0], dgkv[0]
        d_wq = _mm_tn(sv["qn"], dqb, name=f"dw_uq_{l}", tm=Q_LORA, tn=D_QK, tk=tk)
        d_wk = _mm_tn(sv["kin"], dkb, name=f"dw_uk_{l}", tm=Q_LORA, tn=D_QK, tk=tk)
        d_wv = _mm_tn(sv["kin"], dvb, name=f"dw_uv_{l}", tm=Q_LORA, tn=D_MLA, tk=tk)
        grads["w_uq"][l] = d_wq.reshape(Q_LORA, MLA_HEADS, HEAD_PAD)[:, :, :QK_NOPE + QK_ROPE].reshape(Q_LORA, -1)
        d_kn = d_wk[:KV_LORA].reshape(KV_LORA, MLA_HEADS, HEAD_PAD)[:, :, :QK_NOPE]
        d_vv = d_wv[:KV_LORA].reshape(KV_LORA, MLA_HEADS, V_HEAD)
        grads["w_ukv"][l] = jnp.concatenate([d_kn, d_vv], axis=-1).reshape(KV_LORA, -1)
        dz = jnp.concatenate([dz_conv, dz_mid, dgate], axis=1)
        d_win = _mm_tn(sv["h1b"], dz, name=f"dw_in_{l}", tm=D_MODEL, tn=1024, tk=tk)
        grads["mix_w_in"][l] = jnp.concatenate([d_win[:, :Z_KR_END], d_win[:, Z_KR_END + D_IN_PAD - D_IN_REAL:]], axis=1)
        dh = _mm(dz, lw["w_in"], trans_b=True, out_dtype=F32, name=f"mix_dx_{l}", res=dr2, res_scale=ALPHA,
                 tm=tms, tn=D_MODEL, tk=2048)
        carry = None
        if l == 0:
            carry = dest_of(0, _MIXER)
            dest[0].update(carry)
        dh, grads["ln_g"][l][0], grads["ln_b"][l][0] = ffn_bwd(
            dh, sv["f1"], sv["h_in_b"], sv["up1"], sv["down1"], S["ln_g"][l, 0:1], f"{l}a", _FFN1, l, carry=carry)

    small = {n: jnp.stack(grads[n]) for n in ("mix_b_gate", "conv_w", "q_norm_g", "kv_norm_g")}
    small["ln_g"] = jnp.stack([jnp.stack(g) for g in grads["ln_g"]])
    small["ln_b"] = jnp.stack([jnp.stack(g) for g in grads["ln_b"]])
    small["meta_tokens"] = dh[:N_META]
    return loss8, dh[N_META:t_real], dest, received, small


def kernel(x, meta_tokens, ffn1_w_up, ffn1_w_down, mix_w_in, mix_b_gate, conv_w, q_norm_g, w_uq, kv_norm_g, w_ukv, w_br_conv, w_br_mla, w_o, ffn2_w_up, ffn2_w_down, ln_g, ln_b, loss_target, m_meta_tokens, m_ffn1_w_up, m_ffn1_w_down, m_mix_w_in, m_mix_b_gate, m_conv_w, m_q_norm_g, m_w_uq, m_kv_norm_g, m_w_ukv, m_w_br_conv, m_w_br_mla, m_w_o, m_ffn2_w_up, m_ffn2_w_down, m_ln_g, m_ln_b, v_meta_tokens, v_ffn1_w_up, v_ffn1_w_down, v_mix_w_in, v_mix_b_gate, v_conv_w, v_q_norm_g, v_w_uq, v_kv_norm_g, v_w_ukv, v_w_br_conv, v_w_br_mla, v_w_o, v_ffn2_w_up, v_ffn2_w_down, v_ln_g, v_ln_b):
    names = ["meta_tokens", "ffn1_w_up", "ffn1_w_down", "mix_w_in", "mix_b_gate", "conv_w", "q_norm_g", "w_uq",
             "kv_norm_g", "w_ukv", "w_br_conv", "w_br_mla", "w_o", "ffn2_w_up", "ffn2_w_down", "ln_g", "ln_b"]
    w = dict(zip(names, (meta_tokens, ffn1_w_up, ffn1_w_down, mix_w_in, mix_b_gate, conv_w, q_norm_g, w_uq,
                         kv_norm_g, w_ukv, w_br_conv, w_br_mla, w_o, ffn2_w_up, ffn2_w_down, ln_g, ln_b)))
    m = dict(zip(names, (m_meta_tokens, m_ffn1_w_up, m_ffn1_w_down, m_mix_w_in, m_mix_b_gate, m_conv_w, m_q_norm_g,
                         m_w_uq, m_kv_norm_g, m_w_ukv, m_w_br_conv, m_w_br_mla, m_w_o, m_ffn2_w_up, m_ffn2_w_down,
                         m_ln_g, m_ln_b)))
    v = dict(zip(names, (v_meta_tokens, v_ffn1_w_up, v_ffn1_w_down, v_mix_w_in, v_mix_b_gate, v_conv_w, v_q_norm_g,
                         v_w_uq, v_kv_norm_g, v_w_ukv, v_w_br_conv, v_w_br_mla, v_w_o, v_ffn2_w_up, v_ffn2_w_down,
                         v_ln_g, v_ln_b)))
    ix, iy, ic = lax.axis_index("x"), lax.axis_index("y"), lax.axis_index("c")
    dev = 4 * ix + 2 * iy + ic

    big_names = [n for n, _ in _BIG]
    small_names = [n for n, _ in _SMALL_SHARDED]
    small_axes = [a for _, a in _SMALL_SHARDED]
    small_shapes = [w[n].shape for n in small_names]
    gathered = _all_gather([w[n][0].astype(BF16) for n in _FFN1] + [_pack([w[n] for n in small_names], F32)],
                           name="all_gather_weights")
    gathered_ffn1 = dict(zip(_FFN1, gathered[:-1]))
    w_blocks = [{n: w[n][l].astype(BF16) for n in big_names} for l in range(DEPTH)]
    S = dict(zip(small_names, _unpack_gathered(gathered[-1], small_shapes, small_axes)))
    S["q_norm_g"], S["kv_norm_g"] = q_norm_g, kv_norm_g

    loss8, grad_x, dest, received, G = _local_step(x[0], loss_target[0], gathered_ffn1, w_blocks, S)

    my_dev = dev.reshape(1).astype(jnp.int32)
    big_res = [{}, {}, {}, {}]
    for n in big_names:
        res = [_adamw(w[n], m[n], v[n], l, [(dest[l][n], None)] + [(received[l][n], k) for k in range(N_DEV - 1)],
                      my_dev, name=f"adamw{l}_{n}") for l in range(DEPTH)]
        for kind in range(4):
            big_res[kind][n] = jnp.stack([r[kind] for r in res])

    small_all = small_names + list(_SMALL_REPL)
    part = _pack([G[n] for n in small_all] + [loss8[0, 0:1]], F32)
    full_shapes = [G[n].shape for n in small_all] + [(1,)]
    summed = _sum8(_all_gather([part], name="all_gather_small_grads")[0], name="sum_small_grads")
    unpacked = _unpack(summed, full_shapes)
    loss = unpacked[-1][0]
    g_full = dict(zip(small_all, unpacked[:-1]))
    g_loc = []
    for n in small_all:
        if n in _SMALL_REPL:
            g_loc.append(g_full[n])
        else:
            ax = dict(_SMALL_SHARDED)[n]
            g_loc.append(lax.dynamic_slice_in_dim(g_full[n], dev * w[n].shape[ax], w[n].shape[ax], axis=ax))
    loc_shapes = [w[n].shape for n in small_all]
    g_pack = _pack(g_loc, F32)
    small_out = _adamw(_pack([w[n] for n in small_all], F32)[None], _pack([m[n] for n in small_all], F32)[None],
                       _pack([v[n] for n in small_all], F32)[None], 0, [(g_pack[None], 0)],
                       jnp.zeros((1,), jnp.int32), name="adamw_small")
    small_res = [dict(zip(small_all, _unpack(o, loc_shapes))) for o in small_out]

    outs = [loss, grad_x[None]]
    for kind in range(4):
        for n in names:
            outs.append(big_res[kind][n] if n in big_res[kind] else small_res[kind][n])
    return tuple(outs)
```

```python
import functools

import numpy as np
import jax
import jax.numpy as jnp
from jax import lax
from jax.experimental import pallas as pl
from jax.experimental.pallas import tpu as pltpu

F32 = jnp.float32
BF16 = jnp.bfloat16

D_MODEL = 1024
DEPTH = 2
N_META = 16
D_CONV = 512
CONV_WIDTH = 3
MLA_HEADS = 8
QK_NOPE = 64
QK_ROPE = 32
V_HEAD = 64
Q_LORA = 256
KV_LORA = 128
D_MLA = MLA_HEADS * V_HEAD
ROPE_BASE = 10000.0
NEG_INF = -1e30
D_FF = 2816
ALPHA = (2 * DEPTH) ** 0.25
LN_EPS = 1e-5
RMS_EPS = 1e-6
ATTN_SCALE = (QK_NOPE + QK_ROPE) ** -0.5
ADAM_LR = 0.001
ADAM_B1 = 0.9
ADAM_B2 = 0.999
ADAM_EPS = 1e-08
ADAM_WD = 0.01
ADAM_STEP = 10

N_DEV = 8
HEAD_PAD = 128
HEADS_PER_STEP = 2
FF_BLK = 2 * D_FF // N_DEV
FF_HALF_BLOCKS = N_DEV // 2
D_QK = MLA_HEADS * HEAD_PAD
Z_CONV = 0
Z_MID = 1536
Z_GATE = 2048
D_IN_PAD = 4096
D_IN_REAL = 4000
Z_KR_END = Z_MID + Q_LORA + KV_LORA + QK_ROPE

V7X_VMEM_LIMIT = 56 * 1024 * 1024
LANE = 128
ROW_ALIGN = 256


def _tile(n, cands):
    for c in cands:
        if n % c == 0:
            return c
    raise ValueError(f"no tile for {n} in {cands}")


def _params(sem):
    return pltpu.CompilerParams(dimension_semantics=sem, vmem_limit_bytes=V7X_VMEM_LIMIT)


def _mm_call(a, b, a_spec, b_spec, *, out_shape, out_spec, acc_shape, grid, name, trans_b=False,
             res=None, res_spec=None, res_scale=1.0, pieces=1, exchange=None):
    nk = grid[2]
    has_res = res is not None
    out_dtype = out_shape.dtype
    dims = (((1,), (1,)), ((), ())) if trans_b else (((1,), (0,)), ((), ()))

    def body(*refs):
        if has_res:
            a_ref, b_ref, r_ref, o_ref, acc = refs
        else:
            a_ref, b_ref, o_ref, acc = refs
        k = pl.program_id(2)
        if pieces == 1:
            part = lax.dot_general(a_ref[...], b_ref[...], dims, preferred_element_type=F32)
        else:
            part = lax.dot_general(a_ref[0], b_ref[0], dims, preferred_element_type=F32)
            for p in range(1, pieces):
                part = part + lax.dot_general(a_ref[p], b_ref[p], dims, preferred_element_type=F32)

        @pl.when(k == 0)
        def _():
            acc[...] = part

        @pl.when(k > 0)
        def _():
            acc[...] += part

        @pl.when(k == nk - 1)
        def _():
            out = acc[...]
            if has_res:
                out = out + res_scale * r_ref[...]
            o_ref[...] = out.astype(out_dtype)

    in_specs = [a_spec, b_spec]
    args = [a, b]
    if has_res:
        in_specs.append(res_spec)
        args.append(res)
    if exchange is None:
        return pl.pallas_call(
            body, name=name, grid=grid, in_specs=in_specs, out_specs=out_spec, out_shape=out_shape,
            scratch_shapes=[pltpu.VMEM(acc_shape, F32)],
            compiler_params=_params(("parallel", "parallel", "arbitrary")),
        )(*args)
    ex_args, ex_specs, ex_out, ex_scratch = _exchange_operands(exchange)
    return pl.pallas_call(
        _carry_exchange(body, exchange, 0, len(args), 1, grid), name=name, grid=grid,
        in_specs=in_specs + ex_specs, out_specs=[out_spec] + ex_specs, out_shape=[out_shape] + ex_out,
        scratch_shapes=[pltpu.VMEM(acc_shape, F32)] + ex_scratch,
        compiler_params=_params(("arbitrary", "arbitrary", "arbitrary")),
    )(*args, *ex_args)


def _mm(a, b, *, out_dtype, name, trans_b=False, res=None, res_scale=1.0, tm, tn, tk, exchange=None):
    M, K = a.shape
    N = b.shape[0] if trans_b else b.shape[1]
    assert M % tm == 0 and N % tn == 0 and K % tk == 0
    b_spec = (pl.BlockSpec((tn, tk), lambda i, j, k: (j, k)) if trans_b
              else pl.BlockSpec((tk, tn), lambda i, j, k: (k, j)))
    tile = pl.BlockSpec((tm, tn), lambda i, j, k: (i, j))
    return _mm_call(a, b, pl.BlockSpec((tm, tk), lambda i, j, k: (i, k)), b_spec,
                    out_shape=jax.ShapeDtypeStruct((M, N), out_dtype), out_spec=tile, acc_shape=(tm, tn),
                    grid=(M // tm, N // tn, K // tk), name=name, trans_b=trans_b,
                    res=res, res_spec=tile, res_scale=res_scale, exchange=exchange)


def _mm_tn_call(a, b, a_spec, b_spec, *, out_shape, out_spec, grid, name, exchange=None):
    def body(a_ref, b_ref, o_ref):
        k = pl.program_id(2)
        part = lax.dot_general(a_ref[...], b_ref[...], (((0,), (0,)), ((), ())),
                               preferred_element_type=F32)

        @pl.when(k == 0)
        def _():
            o_ref[...] = part

        @pl.when(k > 0)
        def _():
            o_ref[...] += part

    if exchange is None:
        return pl.pallas_call(
            body, name=name, grid=grid, in_specs=[a_spec, b_spec], out_specs=out_spec, out_shape=out_shape,
            compiler_params=_params(("parallel", "parallel", "arbitrary")),
        )(a, b)
    ex_args, ex_specs, ex_out, ex_scratch = _exchange_operands(exchange)
    return pl.pallas_call(
        _carry_exchange(body, exchange, 0, 2, 1, grid), name=name, grid=grid,
        in_specs=[a_spec, b_spec] + ex_specs, out_specs=[out_spec] + ex_specs, out_shape=[out_shape] + ex_out,
        scratch_shapes=ex_scratch,
        compiler_params=_params(("arbitrary", "arbitrary", "arbitrary")),
    )(a, b, *ex_args)


def _mm_tn(a, b, *, name, tm, tn, tk):
    T, M = a.shape
    N = b.shape[1]
    assert M % tm == 0 and N % tn == 0 and T % tk == 0
    return _mm_tn_call(a, b, pl.BlockSpec((tk, tm), lambda i, j, k: (k, i)),
                       pl.BlockSpec((tk, tn), lambda i, j, k: (k, j)),
                       out_shape=jax.ShapeDtypeStruct((M, N), F32),
                       out_spec=pl.BlockSpec((tm, tn), lambda i, j, k: (i, j)),
                       grid=(M // tm, N // tn, T // tk), name=name)


def _ffn_up(hb, w_up8, *, name, tm, exchange=None):
    T = hb.shape[0]

    def body(h_ref, wg_ref, wu_ref, gu_ref, a_ref):
        h = h_ref[...]
        g = jnp.dot(h, wg_ref[...], preferred_element_type=F32)
        u = jnp.dot(h, wu_ref[...], preferred_element_type=F32)
        sg = jax.nn.sigmoid(g)
        silu = g * sg
        gu_ref[0] = (u * (sg * (1.0 + g * (1.0 - sg)))).astype(BF16)
        gu_ref[1] = silu.astype(BF16)
        a_ref[...] = (silu * u).astype(BF16)

    grid = (FF_HALF_BLOCKS, T // tm)
    ex_args, ex_specs, ex_out, ex_scratch = _exchange_operands(exchange)
    return pl.pallas_call(
        _carry_exchange(body, exchange, 0, 3, 2, grid), name=name, grid=grid,
        in_specs=[pl.BlockSpec((tm, D_MODEL), lambda j, i: (i, 0)),
                  pl.BlockSpec((None, D_MODEL, FF_BLK), lambda j, i: (j, 0, 0)),
                  pl.BlockSpec((None, D_MODEL, FF_BLK), lambda j, i: (j + FF_HALF_BLOCKS, 0, 0))] + ex_specs,
        out_specs=[pl.BlockSpec((2, None, tm, FF_BLK), lambda j, i: (0, j, i, 0)),
                   pl.BlockSpec((None, tm, FF_BLK), lambda j, i: (j, i, 0))] + ex_specs,
        out_shape=[jax.ShapeDtypeStruct((2, FF_HALF_BLOCKS, T, FF_BLK), BF16),
                   jax.ShapeDtypeStruct((FF_HALF_BLOCKS, T, FF_BLK), BF16)] + ex_out,
        scratch_shapes=ex_scratch,
        compiler_params=_params(("arbitrary", "arbitrary") if exchange else ("parallel", "parallel")),
    )(hb, w_up8, w_up8, *ex_args)


def _mm_res_ln(a, w, res, g, b, *, scale, name, tm):
    split = a.ndim == 3
    if split:
        S, T, Ks = a.shape
        K = S * Ks
    else:
        T, K = a.shape

    def body(a_ref, w_ref, res_ref, g_ref, b_ref, r_ref, y_ref, yb_ref):
        if split:
            f = jnp.dot(a_ref[0], w_ref[0:Ks, :], preferred_element_type=F32)
            for s in range(1, S):
                f = f + jnp.dot(a_ref[s], w_ref[s * Ks:(s + 1) * Ks, :], preferred_element_type=F32)
        else:
            f = jnp.dot(a_ref[...], w_ref[...], preferred_element_type=F32)
        r = ALPHA * res_ref[...] + scale * f
        mu = jnp.mean(r, axis=-1, keepdims=True)
        xc = r - mu
        var = jnp.mean(xc * xc, axis=-1, keepdims=True)
        y = xc * lax.rsqrt(var + LN_EPS) * g_ref[...] + b_ref[...]
        r_ref[...] = r
        y_ref[...] = y
        yb_ref[...] = y.astype(BF16)

    row = pl.BlockSpec((tm, D_MODEL), lambda i: (i, 0))
    vec = pl.BlockSpec((1, D_MODEL), lambda i: (0, 0))
    return pl.pallas_call(
        body, name=name, grid=(T // tm,),
        in_specs=[pl.BlockSpec((S, tm, Ks), lambda i: (0, i, 0)) if split else pl.BlockSpec((tm, K), lambda i: (i, 0)),
                  pl.BlockSpec((K, D_MODEL), lambda i: (0, 0)), row, vec, vec],
        out_specs=[row, row, row],
        out_shape=[jax.ShapeDtypeStruct((T, D_MODEL), F32), jax.ShapeDtypeStruct((T, D_MODEL), F32),
                   jax.ShapeDtypeStruct((T, D_MODEL), BF16)],
        compiler_params=_params(("parallel",)),
    )(a, w, res, g, b)


def _conv_fwd(z, conv_w8, *, name, tm):
    T = z.shape[0]
    hb = tm // 8

    def body(b_ref, c_ref, h_ref, cp_ref, hp_ref, w_ref, y_ref):
        i = pl.program_id(0)
        u = c_ref[...] * h_ref[...]
        up = jnp.where(i > 0, cp_ref[...] * hp_ref[...], 0.0)
        ue = jnp.concatenate([up, u], axis=0)
        s1 = pltpu.roll(ue, 1, 0)[8:]
        s2 = pltpu.roll(ue, 2, 0)[8:]
        w = w_ref[...]
        conv = w[0:1] * s2 + w[1:2] * s1 + w[2:3] * u
        y_ref[...] = (b_ref[...] * conv).astype(BF16)

    def col(c):
        return pl.BlockSpec((tm, D_CONV), lambda i: (i, c))

    def prev(c):
        return pl.BlockSpec((8, D_CONV), lambda i: (jnp.maximum(i * hb - 1, 0), c))

    return pl.pallas_call(
        body, name=name, grid=(T // tm,),
        in_specs=[col(0), col(1), col(2), prev(1), prev(2), pl.BlockSpec((8, D_CONV), lambda i: (0, 0))],
        out_specs=pl.BlockSpec((tm, D_CONV), lambda i: (i, 0)),
        out_shape=jax.ShapeDtypeStruct((T, D_CONV), BF16),
        compiler_params=_params(("parallel",)),
    )(z, z, z, z, z, conv_w8)


def _rope(x, c, s1, s2):
    n = x.shape[-1]
    return x * c + pltpu.roll(x, 16, 1) * s1 + pltpu.roll(x, n - 16, 1) * s2


def _rope_t(d, c, s1, s2):
    n = d.shape[-1]
    return d * c + pltpu.roll(d * s1, n - 16, 1) + pltpu.roll(d * s2, 16, 1)


def _rms(x, g):
    rstd = lax.rsqrt(jnp.mean(x * x, axis=-1, keepdims=True) + RMS_EPS)
    return x * rstd * g


def _qkv_proj(z, gq, gkv, wq, wk, wv_ext, tabs, *, name, tm):
    T = z.shape[0]

    def body(z_ref, gq_ref, gkv_ref, wq_ref, wk_ref, wv_ref, c_ref, s1_ref, s2_ref,
             q_ref, k_ref, v_ref, qn_ref, kin_ref):
        zz = z_ref[...]
        qn = _rms(zz[:, :Q_LORA], gq_ref[...]).astype(BF16)
        kvn = _rms(zz[:, Q_LORA:Q_LORA + KV_LORA], gkv_ref[...]).astype(BF16)
        kin = jnp.concatenate([kvn, zz[:, Q_LORA + KV_LORA:].astype(BF16)], axis=-1)
        c = jnp.tile(c_ref[...], (1, MLA_HEADS))
        s1 = jnp.tile(s1_ref[...], (1, MLA_HEADS))
        s2 = jnp.tile(s2_ref[...], (1, MLA_HEADS))
        qpre = jnp.dot(qn, wq_ref[...], preferred_element_type=F32)
        kpre = jnp.dot(kin, wk_ref[...], preferred_element_type=F32)
        q_ref[...] = (_rope(qpre, c, s1, s2) * (ATTN_SCALE * LOG2_E)).astype(BF16)
        k_ref[...] = _rope(kpre, c, s1, s2).astype(BF16)
        vv = jnp.dot(kvn, wv_ref[...], preferred_element_type=F32)
        lane = lax.broadcasted_iota(jnp.int32, vv.shape, 1)
        v_ref[...] = jnp.where((lane & (HEAD_PAD - 1)) < V_HEAD, vv, 1.0).astype(BF16)
        qn_ref[...] = qn
        kin_ref[...] = kin

    def full(shape):
        return pl.BlockSpec(shape, lambda i: (0, 0))

    def rows(w, c=0):
        return pl.BlockSpec((tm, w), lambda i: (i, c))

    return pl.pallas_call(
        body, name=name, grid=(T // tm,),
        in_specs=[rows(512, Z_MID // 512), full((1, Q_LORA)), full((1, KV_LORA)),
                  full((Q_LORA, D_QK)), full((Q_LORA, D_QK)), full((KV_LORA, D_QK)),
                  rows(LANE), rows(LANE), rows(LANE)],
        out_specs=[rows(D_QK), rows(D_QK), rows(D_QK), rows(Q_LORA), rows(Q_LORA)],
        out_shape=[jax.ShapeDtypeStruct((T, D_QK), BF16), jax.ShapeDtypeStruct((T, D_QK), BF16),
                   jax.ShapeDtypeStruct((T, D_QK), BF16), jax.ShapeDtypeStruct((T, Q_LORA), BF16),
                   jax.ShapeDtypeStruct((T, Q_LORA), BF16)],
        compiler_params=_params(("parallel",)),
    )(z, gq, gkv, wq, wk, wv_ext, *tabs)


SOFTMAX_ROWS = 32
LOG2_E = 1.4426950408889634
_NT = (((1,), (1,)), ((), ()))
_TN = (((0,), (0,)), ((), ()))


def _diag_mask(s, row0, col0=0):
    row = lax.broadcasted_iota(jnp.int32, s.shape, 0) + row0
    col = lax.broadcasted_iota(jnp.int32, s.shape, 1) + col0
    return jnp.where(col <= row, s, NEG_INF)


_RELATIONS = tuple((rx, ry, rc) for rx in (0, 1) for ry in (0, 1) for rc in (0, 1))[1:]


class _Exchange:
    def __init__(self, kind, arrays):
        assert kind in ("gather", "scatter")
        self.kind, self.arrays, self.n = kind, list(arrays), len(arrays)

    def out_shapes(self):
        if self.kind == "gather":
            return [jax.ShapeDtypeStruct((N_DEV,) + a.shape, a.dtype) for a in self.arrays]
        return [jax.ShapeDtypeStruct((N_DEV - 1,) + a.shape[1:], a.dtype) for a in self.arrays]

    def scratch_shapes(self):
        sems = [pltpu.SemaphoreType.DMA((7 * self.n,)), pltpu.SemaphoreType.DMA((7 * self.n,))]
        if self.kind == "gather":
            sems.append(pltpu.SemaphoreType.DMA((self.n,)))
        return sems

    def _copies(self, src_refs, out_refs, sems):
        x, y, c = lax.axis_index("x"), lax.axis_index("y"), lax.axis_index("c")
        me = 4 * x + 2 * y + c
        sends, recvs, local = [], [], []
        for a in range(self.n):
            for k, rel in enumerate(_RELATIONS):
                peer = tuple((1 - p) if r else p for p, r in zip((x, y, c), rel))
                peer_index = 4 * peer[0] + 2 * peer[1] + peer[2]
                if self.kind == "gather":
                    src, lands_there, lands_here = src_refs[a], out_refs[a].at[me], out_refs[a].at[peer_index]
                else:
                    src, lands_there, lands_here = src_refs[a].at[peer_index], out_refs[a].at[k], out_refs[a].at[k]
                for dst, group in ((lands_there, sends), (lands_here, recvs)):
                    group.append(pltpu.make_async_remote_copy(
                        src_ref=src, dst_ref=dst, send_sem=sems[0].at[7 * a + k], recv_sem=sems[1].at[7 * a + k],
                        device_id=peer, device_id_type=_MESH_ID))
            if self.kind == "gather":
                local.append(pltpu.make_async_copy(src_refs[a], out_refs[a].at[me], sems[2].at[a]))
        return sends, recvs, local

    def start(self, src_refs, out_refs, sems):
        sends, _, local = self._copies(src_refs, out_refs, sems)
        for cp in local + sends:
            cp.start()

    def wait(self, src_refs, out_refs, sems):
        sends, recvs, local = self._copies(src_refs, out_refs, sems)
        for cp in recvs:
            cp.wait_recv()
        for cp in sends:
            cp.wait_send()
        for cp in local:
            cp.wait()


def _exchange_operands(exchange):
    if exchange is None:
        return [], [], [], []
    return exchange.arrays, [_ANY] * exchange.n, exchange.out_shapes(), exchange.scratch_shapes()


def _carry_exchange(body, exchange, n_prefetch, n_in, n_out, grid):
    if exchange is None:
        return body
    n = exchange.n
    last = tuple(g - 1 for g in grid)

    def wrapped(*refs):
        head = refs[:n_prefetch + n_in]
        src_refs = refs[n_prefetch + n_in:n_prefetch + n_in + n]
        rest = refs[n_prefetch + n_in + n:]
        outs, out_refs, rest = rest[:n_out], rest[n_out:n_out + n], rest[n_out + n:]
        n_sems = len(exchange.scratch_shapes())
        scratch, sems = rest[:len(rest) - n_sems], rest[len(rest) - n_sems:]
        at_first = functools.reduce(jnp.logical_and, [pl.program_id(d) == 0 for d in range(len(grid))])
        at_last = functools.reduce(jnp.logical_and, [pl.program_id(d) == last[d] for d in range(len(grid))])

        @pl.when(at_first)
        def _():
            exchange.start(src_refs, out_refs, sems)

        body(*head, *outs, *scratch)

        @pl.when(at_last)
        def _():
            exchange.wait(src_refs, out_refs, sems)

    return wrapped


def _attn_fwd(q, k, v, *, name, blk, exchange=None):
    T = q.shape[0]
    n = T // blk
    hp = HEADS_PER_STEP
    qi = np.array([i for i in range(n) for j in range(i + 1)], np.int32)
    kj = np.array([j for i in range(n) for j in range(i + 1)], np.int32)

    rc = _tile(blk, (SOFTMAX_ROWS,))

    def body(qi_ref, kj_ref, q_ref, k_ref, v_ref, o_ref, lse_ref, m_sc, acc_sc, s_sc, p_sc, red_sc):
        s_id = pl.program_id(1)
        i = qi_ref[s_id]
        j = kj_ref[s_id]

        @pl.when(j == 0)
        def _():
            m_sc[...] = jnp.full(m_sc.shape, NEG_INF, F32)
            acc_sc[...] = jnp.zeros(acc_sc.shape, F32)

        def head_step(hh, diagonal):
            hs = slice(hh * HEAD_PAD, (hh + 1) * HEAD_PAD)
            s_sc[hh] = lax.dot_general(q_ref[:, hs], k_ref[:, hs], _NT, preferred_element_type=F32)
            lanes = [slice(t * LANE, (t + 1) * LANE) for t in range(blk // LANE)]
            for r in range(blk // rc):
                rows = slice(r * rc, (r + 1) * rc)
                s = s_sc[hh, rows, :]
                if diagonal:
                    s = _diag_mask(s, r * rc)
                    s_sc[hh, rows, :] = s
                pm = s[:, lanes[0]]
                for t in lanes[1:]:
                    pm = jnp.maximum(pm, s[:, t])
                red_sc[hh, rows, :] = pm
            m_old = m_sc[hh]
            row_max = jnp.max(red_sc[hh], axis=-1, keepdims=True)
            m_new = jnp.maximum(m_old, jnp.broadcast_to(row_max, (blk, LANE)))
            a = jnp.exp2(m_old - m_new)
            m_sc[hh] = m_new
            for r in range(blk // rc):
                rows = slice(r * rc, (r + 1) * rc)
                mb = m_sc[hh, rows, :]
                for t in lanes:
                    p_sc[hh, rows, t] = jnp.exp2(s_sc[hh, rows, t] - mb).astype(BF16)
            acc_sc[hh] = a * acc_sc[hh] + jnp.dot(p_sc[hh], v_ref[:, hs], preferred_element_type=F32)

        @pl.when(j < i)
        def _():
            for hh in range(hp):
                head_step(hh, False)

        @pl.when(j == i)
        def _():
            for hh in range(hp):
                head_step(hh, True)
            for hh in range(hp):
                acc = acc_sc[hh]
                swapped = pltpu.roll(acc, V_HEAD, 1)
                o_ref[:, hh * V_HEAD:(hh + 1) * V_HEAD] = (acc / swapped)[:, :V_HEAD].astype(BF16)
                lane = lax.broadcasted_iota(jnp.int32, acc.shape, 1)
                denom = jnp.where(lane < V_HEAD, swapped, acc)
                lse_ref[hh] = m_sc[hh] + jnp.log(denom) * LOG2_E

    grid = (MLA_HEADS // hp, len(qi))
    ex_args, ex_specs, ex_out, ex_scratch = _exchange_operands(exchange)
    grid_spec = pltpu.PrefetchScalarGridSpec(
        num_scalar_prefetch=2, grid=grid,
        in_specs=[pl.BlockSpec((blk, hp * HEAD_PAD), lambda g, s, qi, kj: (qi[s], g)),
                  pl.BlockSpec((blk, hp * HEAD_PAD), lambda g, s, qi, kj: (kj[s], g)),
                  pl.BlockSpec((blk, hp * HEAD_PAD), lambda g, s, qi, kj: (kj[s], g))] + ex_specs,
        out_specs=[pl.BlockSpec((blk, hp * V_HEAD), lambda g, s, qi, kj: (qi[s], g)),
                   pl.BlockSpec((hp, blk, LANE), lambda g, s, qi, kj: (g, qi[s], 0))] + ex_specs,
        scratch_shapes=[pltpu.VMEM((hp, blk, LANE), F32), pltpu.VMEM((hp, blk, HEAD_PAD), F32),
                        pltpu.VMEM((hp, blk, blk), F32), pltpu.VMEM((hp, blk, blk), BF16),
                        pltpu.VMEM((hp, blk, LANE), F32)] + ex_scratch)
    return pl.pallas_call(
        _carry_exchange(body, exchange, 2, 3, 2, grid), name=name, grid_spec=grid_spec,
        out_shape=[jax.ShapeDtypeStruct((T, D_MLA), BF16),
                   jax.ShapeDtypeStruct((MLA_HEADS, T, LANE), F32)] + ex_out,
        compiler_params=_params(("arbitrary", "arbitrary") if exchange else ("parallel", "arbitrary")),
    )(jnp.asarray(qi), jnp.asarray(kj), q, k, v, *ex_args)


def _merge(yc, ym, wbc, wbm, z, bg, *, name, tm):
    T = yc.shape[0]

    def body(yc_ref, ym_ref, wbc_ref, wbm_ref, gc_ref, gm_ref, bg_ref, mg_ref, pa_ref, pb_ref):
        pa = jnp.dot(yc_ref[...], wbc_ref[...], preferred_element_type=F32)
        pb = jnp.dot(ym_ref[...], wbm_ref[...], preferred_element_type=F32)
        bgv = bg_ref[...]
        sa = jax.nn.sigmoid(gc_ref[...] + bgv[0:1])
        sb = jax.nn.sigmoid(gm_ref[...] + bgv[1:2])
        mg_ref[...] = (sa * pa + sb * pb).astype(BF16)
        pa_ref[...] = pa.astype(BF16)
        pb_ref[...] = pb.astype(BF16)

    row = pl.BlockSpec((tm, D_MODEL), lambda i: (i, 0))
    return pl.pallas_call(
        body, name=name, grid=(T // tm,),
        in_specs=[pl.BlockSpec((tm, D_CONV), lambda i: (i, 0)), pl.BlockSpec((tm, D_MLA), lambda i: (i, 0)),
                  pl.BlockSpec((D_CONV, D_MODEL), lambda i: (0, 0)), pl.BlockSpec((D_MLA, D_MODEL), lambda i: (0, 0)),
                  pl.BlockSpec((tm, D_MODEL), lambda i: (i, Z_GATE // D_MODEL)),
                  pl.BlockSpec((tm, D_MODEL), lambda i: (i, Z_GATE // D_MODEL + 1)),
                  pl.BlockSpec((8, D_MODEL), lambda i: (0, 0))],
        out_specs=[row, row, row],
        out_shape=[jax.ShapeDtypeStruct((T, D_MODEL), BF16)] * 3,
        compiler_params=_params(("parallel",)),
    )(yc, ym, wbc, wbm, z, z, bg)


def _loss_head(h, tgt, *, t_real, name, tm):
    T = h.shape[0]

    def body(h_ref, t_ref, dy_ref, loss_ref):
        i = pl.program_id(0)
        row = lax.broadcasted_iota(jnp.int32, (tm, 1), 0) + i * tm
        valid = (row >= N_META) & (row < t_real)
        err = jnp.where(valid, h_ref[...] - t_ref[...], 0.0)
        dy_ref[...] = err * (1.0 / D_MODEL)
        part = 0.5 * jnp.sum(jnp.sum(err * err, axis=-1, keepdims=True) * (1.0 / D_MODEL), axis=0, keepdims=True)

        @pl.when(i == 0)
        def _():
            loss_ref[...] = jnp.zeros(loss_ref.shape, F32)

        loss_ref[...] += jnp.broadcast_to(part, loss_ref.shape)

    row_spec = pl.BlockSpec((tm, D_MODEL), lambda i: (i, 0))
    return pl.pallas_call(
        body, name=name, grid=(T // tm,),
        in_specs=[row_spec, row_spec],
        out_specs=[row_spec, pl.BlockSpec((8, LANE), lambda i: (0, 0))],
        out_shape=[jax.ShapeDtypeStruct((T, D_MODEL), F32), jax.ShapeDtypeStruct((8, LANE), F32)],
        compiler_params=_params(("arbitrary",)),
    )(h, tgt)


def _ln_bwd(dy, r, g, *, scale, name, tm):
    T = dy.shape[0]

    def body(dy_ref, r_ref, g_ref, dr_ref, drb_ref, dg_ref, db_ref):
        i = pl.program_id(0)
        rr = r_ref[...]
        dyv = dy_ref[...]
        mu = jnp.mean(rr, axis=-1, keepdims=True)
        xc = rr - mu
        rstd = lax.rsqrt(jnp.mean(xc * xc, axis=-1, keepdims=True) + LN_EPS)
        xh = xc * rstd
        dxh = dyv * g_ref[...]
        m1 = jnp.mean(dxh, axis=-1, keepdims=True)
        m2 = jnp.mean(dxh * xh, axis=-1, keepdims=True)
        dr = rstd * (dxh - m1 - xh * m2)
        dr_ref[...] = dr
        drb_ref[...] = (scale * dr).astype(BF16)

        @pl.when(i == 0)
        def _():
            dg_ref[...] = jnp.zeros(dg_ref.shape, F32)
            db_ref[...] = jnp.zeros(db_ref.shape, F32)

        dg_ref[0:1, :] += jnp.sum(dyv * xh, axis=0, keepdims=True)
        db_ref[0:1, :] += jnp.sum(dyv, axis=0, keepdims=True)

    row = pl.BlockSpec((tm, D_MODEL), lambda i: (i, 0))
    acc = pl.BlockSpec((8, D_MODEL), lambda i: (0, 0))
    return pl.pallas_call(
        body, name=name, grid=(T // tm,),
        in_specs=[row, row, pl.BlockSpec((1, D_MODEL), lambda i: (0, 0))],
        out_specs=[row, row, acc, acc],
        out_shape=[jax.ShapeDtypeStruct((T, D_MODEL), F32), jax.ShapeDtypeStruct((T, D_MODEL), BF16),
                   jax.ShapeDtypeStruct((8, D_MODEL), F32), jax.ShapeDtypeStruct((8, D_MODEL), F32)],
        compiler_params=_params(("arbitrary",)),
    )(dy, r, g)


def _ffn_bwd_mid(dfb, w_down, gu, *, name, tm, exchange=None):
    T = dfb.shape[0]

    def body(df_ref, w_ref, gu_ref, o_ref):
        da = lax.dot_general(df_ref[...], w_ref[...], _NT, preferred_element_type=F32)
        o_ref[0] = (da * gu_ref[0].astype(F32)).astype(BF16)
        o_ref[1] = (da * gu_ref[1].astype(F32)).astype(BF16)

    grid = (FF_HALF_BLOCKS, T // tm)
    ex_args, ex_specs, ex_out, ex_scratch = _exchange_operands(exchange)
    return pl.pallas_call(
        _carry_exchange(body, exchange, 0, 3, 1, grid), name=name, grid=grid,
        in_specs=[pl.BlockSpec((tm, D_MODEL), lambda j, i: (i, 0)),
                  pl.BlockSpec((FF_BLK, D_MODEL), lambda j, i: (j, 0)),
                  pl.BlockSpec((2, None, tm, FF_BLK), lambda j, i: (0, j, i, 0))] + ex_specs,
        out_specs=[pl.BlockSpec((2, None, tm, FF_BLK), lambda j, i: (0, j, i, 0))] + ex_specs,
        out_shape=[jax.ShapeDtypeStruct((2, FF_HALF_BLOCKS, T, FF_BLK), BF16)] + ex_out,
        scratch_shapes=ex_scratch,
        compiler_params=_params(("arbitrary", "arbitrary") if exchange else ("parallel", "parallel")),
    )(dfb, w_down, gu, *ex_args)


def _wo_bwd(dmb, wo_t, z, bg, pa, pb, *, name, tm):
    T = dmb.shape[0]

    def body(dm_ref, w_ref, gc_ref, gm_ref, bg_ref, pa_ref, pb_ref, dpa_ref, dpb_ref, dg_ref, dbg_ref):
        i = pl.program_id(0)
        dm = lax.dot_general(dm_ref[...], w_ref[...], _NT, preferred_element_type=F32)
        bgv = bg_ref[...]
        sa = jax.nn.sigmoid(gc_ref[...] + bgv[0:1])
        sb = jax.nn.sigmoid(gm_ref[...] + bgv[1:2])
        dpa_ref[...] = (dm * sa).astype(BF16)
        dpb_ref[...] = (dm * sb).astype(BF16)
        dga = dm * pa_ref[...].astype(F32) * (sa * (1.0 - sa))
        dgb = dm * pb_ref[...].astype(F32) * (sb * (1.0 - sb))
        dg_ref[:, :D_MODEL] = dga.astype(BF16)
        dg_ref[:, D_MODEL:] = dgb.astype(BF16)

        @pl.when(i == 0)
        def _():
            dbg_ref[...] = jnp.zeros(dbg_ref.shape, F32)

        dbg_ref[0:1, :] += jnp.sum(dga, axis=0, keepdims=True)
        dbg_ref[1:2, :] += jnp.sum(dgb, axis=0, keepdims=True)

    row = pl.BlockSpec((tm, D_MODEL), lambda i: (i, 0))
    return pl.pallas_call(
        body, name=name, grid=(T // tm,),
        in_specs=[row, pl.BlockSpec((D_MODEL, D_MODEL), lambda i: (0, 0)),
                  pl.BlockSpec((tm, D_MODEL), lambda i: (i, Z_GATE // D_MODEL)),
                  pl.BlockSpec((tm, D_MODEL), lambda i: (i, Z_GATE // D_MODEL + 1)),
                  pl.BlockSpec((8, D_MODEL), lambda i: (0, 0)), row, row],
        out_specs=[row, row, pl.BlockSpec((tm, 2 * D_MODEL), lambda i: (i, 0)),
                   pl.BlockSpec((8, D_MODEL), lambda i: (0, 0))],
        out_shape=[jax.ShapeDtypeStruct((T, D_MODEL), BF16), jax.ShapeDtypeStruct((T, D_MODEL), BF16),
                   jax.ShapeDtypeStruct((T, 2 * D_MODEL), BF16), jax.ShapeDtypeStruct((8, D_MODEL), F32)],
        compiler_params=_params(("arbitrary",)),
    )(dmb, wo_t, z, z, bg, pa, pb)


def _conv_bwd(dy, z, conv_w8, *, name, tm):
    T = dy.shape[0]
    n = T // tm
    hb = tm // 8

    def body(dy_ref, b_ref, c_ref, h_ref, cp_ref, hp_ref, dyn_ref, bn_ref, w_ref, dz_ref, dw_ref):
        i = pl.program_id(0)
        u = c_ref[...] * h_ref[...]
        up = jnp.where(i > 0, cp_ref[...] * hp_ref[...], 0.0)
        ue = jnp.concatenate([up, u], axis=0)
        s1 = pltpu.roll(ue, 1, 0)[8:]
        s2 = pltpu.roll(ue, 2, 0)[8:]
        w = w_ref[...]
        conv = w[0:1] * s2 + w[1:2] * s1 + w[2:3] * u
        dyv = dy_ref[...]
        e = dyv * b_ref[...]
        en = jnp.where(i < n - 1, dyn_ref[...] * bn_ref[...], 0.0)
        ee = jnp.concatenate([e, en], axis=0)
        e1 = pltpu.roll(ee, tm + 8 - 1, 0)[:tm]
        e2 = pltpu.roll(ee, tm + 8 - 2, 0)[:tm]
        du = w[2:3] * e + w[1:2] * e1 + w[0:1] * e2
        dz_ref[:, 0:D_CONV] = (dyv * conv).astype(BF16)
        dz_ref[:, D_CONV:2 * D_CONV] = (du * h_ref[...]).astype(BF16)
        dz_ref[:, 2 * D_CONV:] = (du * c_ref[...]).astype(BF16)

        @pl.when(i == 0)
        def _():
            dw_ref[...] = jnp.zeros(dw_ref.shape, F32)

        dw_ref[0:1, :] += jnp.sum(e * s2, axis=0, keepdims=True)
        dw_ref[1:2, :] += jnp.sum(e * s1, axis=0, keepdims=True)
        dw_ref[2:3, :] += jnp.sum(e * u, axis=0, keepdims=True)

    def col(c):
        return pl.BlockSpec((tm, D_CONV), lambda i: (i, c))

    def prev(c):
        return pl.BlockSpec((8, D_CONV), lambda i: (jnp.maximum(i * hb - 1, 0), c))

    def nxt(c):
        return pl.BlockSpec((8, D_CONV), lambda i: (jnp.minimum((i + 1) * hb, T // 8 - 1), c))

    return pl.pallas_call(
        body, name=name, grid=(n,),
        in_specs=[col(0), col(0), col(1), col(2), prev(1), prev(2), nxt(0), nxt(0),
                  pl.BlockSpec((8, D_CONV), lambda i: (0, 0))],
        out_specs=[pl.BlockSpec((tm, 3 * D_CONV), lambda i: (i, 0)), pl.BlockSpec((8, D_CONV), lambda i: (0, 0))],
        out_shape=[jax.ShapeDtypeStruct((T, 3 * D_CONV), BF16), jax.ShapeDtypeStruct((8, D_CONV), F32)],
        compiler_params=_params(("arbitrary",)),
    )(dy, z, z, z, z, z, dy, z, conv_w8)


def _attn_bwd(q, k, v, o, do, lse, *, name, blk, exchange=None):
    T = q.shape[0]
    n = T // blk
    hp = HEADS_PER_STEP
    qi = np.array([i for j in range(n) for i in range(j, n)], np.int32)
    kj = np.array([j for j in range(n) for i in range(j, n)], np.int32)

    rc = _tile(blk, (SOFTMAX_ROWS,))

    def body(qi_ref, kj_ref, q_ref, k_ref, v_ref, o_ref, do_ref, lse_ref, dq_ref, dk_ref, dv_ref,
             dk_sc, dv_sc, s_sc, dp_sc, p_sc, ds_sc, delta_sc):
        s_id = pl.program_id(1)
        i = qi_ref[s_id]
        j = kj_ref[s_id]

        @pl.when(s_id == 0)
        def _():
            dq_ref[...] = jnp.zeros(dq_ref.shape, F32)

        @pl.when(i == j)
        def _():
            dk_sc[...] = jnp.zeros(dk_sc.shape, F32)
            dv_sc[...] = jnp.zeros(dv_sc.shape, F32)

        q_rows = pl.ds(pl.multiple_of(i * blk, blk), blk)

        def head_step(hh, diagonal):
            hs = slice(hh * HEAD_PAD, (hh + 1) * HEAD_PAD)
            vs = slice(hh * V_HEAD, (hh + 1) * V_HEAD)
            qh = q_ref[:, hs]
            kh = k_ref[:, hs]
            doh = do_ref[:, vs]
            s_sc[hh] = lax.dot_general(qh, kh, _NT, preferred_element_type=F32)
            vh = v_ref[:, hh * HEAD_PAD:hh * HEAD_PAD + V_HEAD]
            dp_sc[hh] = lax.dot_general(doh, vh, _NT, preferred_element_type=F32)
            delta = jnp.sum(doh.astype(F32) * o_ref[:, vs].astype(F32), axis=-1, keepdims=True)
            delta_sc[hh] = jnp.broadcast_to(delta, (blk, LANE))
            for r in range(blk // rc):
                rows = slice(r * rc, (r + 1) * rc)
                lse = lse_ref[hh, rows, :]
                dl = delta_sc[hh, rows, :]
                for t in range(blk // LANE):
                    cols = slice(t * LANE, (t + 1) * LANE)
                    s = s_sc[hh, rows, cols]
                    if diagonal:
                        s = _diag_mask(s, r * rc, t * LANE)
                    p = jnp.exp2(s - lse)
                    p_sc[hh, rows, cols] = p.astype(BF16)
                    ds_sc[hh, rows, cols] = (p * (dp_sc[hh, rows, cols] - dl)).astype(BF16)
            dv_sc[hh] += lax.dot_general(p_sc[hh], doh, _TN, preferred_element_type=F32)
            dk_sc[hh] += lax.dot_general(ds_sc[hh], qh, _TN, preferred_element_type=F32)
            dq_ref[q_rows, hs] += jnp.dot(ds_sc[hh], kh, preferred_element_type=F32)

        @pl.when(j < i)
        def _():
            for hh in range(hp):
                head_step(hh, False)

        @pl.when(j == i)
        def _():
            for hh in range(hp):
                head_step(hh, True)

        @pl.when(i == n - 1)
        def _():
            for hh in range(hp):
                dk_ref[:, hh * HEAD_PAD:(hh + 1) * HEAD_PAD] = dk_sc[hh] * (1.0 / LOG2_E)
                dv_ref[:, hh * V_HEAD:(hh + 1) * V_HEAD] = dv_sc[hh]

    wq = hp * HEAD_PAD
    wv = hp * V_HEAD
    grid = (MLA_HEADS // hp, len(qi))
    ex_args, ex_specs, ex_out, ex_scratch = _exchange_operands(exchange)
    grid_spec = pltpu.PrefetchScalarGridSpec(
        num_scalar_prefetch=2, grid=grid,
        in_specs=[pl.BlockSpec((blk, wq), lambda g, s, qi, kj: (qi[s], g)),
                  pl.BlockSpec((blk, wq), lambda g, s, qi, kj: (kj[s], g)),
                  pl.BlockSpec((blk, wq), lambda g, s, qi, kj: (kj[s], g)),
                  pl.BlockSpec((blk, wv), lambda g, s, qi, kj: (qi[s], g)),
                  pl.BlockSpec((blk, wv), lambda g, s, qi, kj: (qi[s], g)),
                  pl.BlockSpec((hp, blk, LANE), lambda g, s, qi, kj: (g, qi[s], 0))] + ex_specs,
        out_specs=[pl.BlockSpec((T, wq), lambda g, s, qi, kj: (0, g)),
                   pl.BlockSpec((blk, wq), lambda g, s, qi, kj: (kj[s], g)),
                   pl.BlockSpec((blk, wv), lambda g, s, qi, kj: (kj[s], g))] + ex_specs,
        scratch_shapes=[pltpu.VMEM((hp, blk, HEAD_PAD), F32), pltpu.VMEM((hp, blk, V_HEAD), F32),
                        pltpu.VMEM((hp, blk, blk), F32), pltpu.VMEM((hp, blk, blk), F32),
                        pltpu.VMEM((hp, blk, blk), BF16), pltpu.VMEM((hp, blk, blk), BF16),
                        pltpu.VMEM((hp, blk, LANE), F32)] + ex_scratch)
    return pl.pallas_call(
        _carry_exchange(body, exchange, 2, 6, 3, grid), name=name, grid_spec=grid_spec,
        out_shape=[jax.ShapeDtypeStruct((T, D_QK), F32), jax.ShapeDtypeStruct((T, D_QK), F32),
                   jax.ShapeDtypeStruct((T, D_MLA), F32)] + ex_out,
        compiler_params=_params(("arbitrary", "arbitrary") if exchange else ("parallel", "arbitrary")),
    )(jnp.asarray(qi), jnp.asarray(kj), q, k, v, o, do, lse, *ex_args)


def _qkv_bwd(dq, dk, dv, z, gq, gkv, wq_t, wk_t, wv_t, tabs, *, name, tm):
    T = dq.shape[0]

    def body(dq_ref, dk_ref, dv_ref, z_ref, gq_ref, gkv_ref, wq_ref, wk_ref, wv_ref, c_ref, s1_ref, s2_ref,
             dz_ref, dqb_ref, dkb_ref, dvb_ref, dgq_ref, dgkv_ref):
        i = pl.program_id(0)
        c = jnp.tile(c_ref[...], (1, MLA_HEADS))
        s1 = jnp.tile(s1_ref[...], (1, MLA_HEADS))
        s2 = jnp.tile(s2_ref[...], (1, MLA_HEADS))
        dqp = _rope_t(dq_ref[...] * ATTN_SCALE, c, s1, s2).astype(BF16)
        dkp = _rope_t(dk_ref[...], c, s1, s2).astype(BF16)
        dvb = dv_ref[...].astype(BF16)
        dqb_ref[...] = dqp
        dkb_ref[...] = dkp
        dvb_ref[...] = dvb
        dqn = lax.dot_general(dqp, wq_ref[...], _NT, preferred_element_type=F32)
        dkin = lax.dot_general(dkp, wk_ref[...], _NT, preferred_element_type=F32)
        dkvn = dkin[:, :KV_LORA] + lax.dot_general(dvb, wv_ref[...], _NT, preferred_element_type=F32)
        zz = z_ref[...]

        def rms_bwd(x, g, dy):
            rstd = lax.rsqrt(jnp.mean(x * x, axis=-1, keepdims=True) + RMS_EPS)
            xh = x * rstd
            dxh = dy * g
            dx = rstd * (dxh - xh * jnp.mean(dxh * xh, axis=-1, keepdims=True))
            return dx, jnp.sum(dy * xh, axis=0, keepdims=True)

        dcq, dgq = rms_bwd(zz[:, :Q_LORA], gq_ref[...], dqn)
        dckv, dgkv = rms_bwd(zz[:, Q_LORA:Q_LORA + KV_LORA], gkv_ref[...], dkvn)
        dz_ref[:, :Q_LORA] = dcq.astype(BF16)
        dz_ref[:, Q_LORA:Q_LORA + KV_LORA] = dckv.astype(BF16)
        dz_ref[:, Q_LORA + KV_LORA:] = dkin[:, KV_LORA:].astype(BF16)

        @pl.when(i == 0)
        def _():
            dgq_ref[...] = jnp.zeros(dgq_ref.shape, F32)
            dgkv_ref[...] = jnp.zeros(dgkv_ref.shape, F32)

        dgq_ref[0:1, :] += dgq
        dgkv_ref[0:1, :] += dgkv

    def full(shape):
        return pl.BlockSpec(shape, lambda i: (0, 0))

    def rows(w, c=0):
        return pl.BlockSpec((tm, w), lambda i: (i, c))

    return pl.pallas_call(
        body, name=name, grid=(T // tm,),
        in_specs=[rows(D_QK), rows(D_QK), rows(D_MLA), rows(512, Z_MID // 512),
                  full((1, Q_LORA)), full((1, KV_LORA)),
                  full((Q_LORA, D_QK)), full((Q_LORA, D_QK)), full((KV_LORA, D_MLA)),
                  rows(LANE), rows(LANE), rows(LANE)],
        out_specs=[rows(512), rows(D_QK), rows(D_QK), rows(D_MLA), full((8, Q_LORA)), full((8, KV_LORA))],
        out_shape=[jax.ShapeDtypeStruct((T, 512), BF16), jax.ShapeDtypeStruct((T, D_QK), BF16),
                   jax.ShapeDtypeStruct((T, D_QK), BF16), jax.ShapeDtypeStruct((T, D_MLA), BF16),
                   jax.ShapeDtypeStruct((8, Q_LORA), F32), jax.ShapeDtypeStruct((8, KV_LORA), F32)],
        compiler_params=_params(("arbitrary",)),
    )(dq, dk, dv, z, gq, gkv, wq_t, wk_t, wv_t, *tabs)


FLAT_W = 1024


ELEMENTWISE_TILE_BYTES = 768 * 1024


def _row_tile(R, C):
    width = -(-C // LANE) * LANE * 4
    best = None
    for t in range(16, R + 1, 16):
        if R % t == 0 and t * width <= ELEMENTWISE_TILE_BYTES:
            best = t
    if best is None:
        best = R
    return best


def _adamw(w, m, v, layer, parts, part_index, *, name):
    _, R, C = w.shape
    tr = _row_tile(R, C)
    bc1 = 1.0 - ADAM_B1 ** ADAM_STEP
    bc2 = 1.0 - ADAM_B2 ** ADAM_STEP
    n_parts = len(parts)

    def body(idx_ref, w_ref, m_ref, v_ref, *refs):
        g_refs = refs[:n_parts]
        g_out, d_out, m_out, v_out = refs[n_parts:]
        g = g_refs[0][...].astype(F32)
        for r in g_refs[1:]:
            g = g + r[...].astype(F32)
        wv = w_ref[...]
        mn = ADAM_B1 * m_ref[...] + (1.0 - ADAM_B1) * g
        vn = ADAM_B2 * v_ref[...] + (1.0 - ADAM_B2) * (g * g)
        m_hat = mn / bc1
        v_hat = vn / bc2
        g_out[...] = g
        d_out[...] = -ADAM_LR * (m_hat / (jnp.sqrt(v_hat) + ADAM_EPS) + ADAM_WD * wv)
        m_out[...] = mn
        v_out[...] = vn

    layer_row = pl.BlockSpec((None, tr, C), lambda i, idx: (layer, i, 0))
    in_specs = [layer_row, layer_row, layer_row]
    args = [w, m, v]
    for arr, slot in parts:
        if slot is None:
            in_specs.append(pl.BlockSpec((None, tr, C), lambda i, idx: (idx[0], i, 0)))
        else:
            in_specs.append(pl.BlockSpec((None, tr, C), lambda i, idx, slot=slot: (slot, i, 0)))
        args.append(arr)
    grid_spec = pltpu.PrefetchScalarGridSpec(
        num_scalar_prefetch=1, grid=(R // tr,), in_specs=in_specs,
        out_specs=[pl.BlockSpec((tr, C), lambda i, idx: (i, 0))] * 4)
    return pl.pallas_call(
        body, name=name, grid_spec=grid_spec,
        out_shape=[jax.ShapeDtypeStruct((R, C), F32)] * 4,
        compiler_params=_params(("parallel",)),
    )(part_index, *args)


def _sum8(parts, *, name):
    _, R, _ = parts.shape

    def body(p_ref, o_ref):
        acc = p_ref[0]
        for d in range(1, N_DEV):
            acc = acc + p_ref[d]
        o_ref[...] = acc

    return pl.pallas_call(
        body, name=name, grid=(1,),
        in_specs=[pl.BlockSpec((N_DEV, R, FLAT_W), lambda i: (0, 0, 0))],
        out_specs=pl.BlockSpec((R, FLAT_W), lambda i: (0, 0)),
        out_shape=jax.ShapeDtypeStruct((R, FLAT_W), F32),
        compiler_params=_params(("arbitrary",)),
    )(parts)


_MESH_ID = pl.DeviceIdType.MESH
_ANY = pl.BlockSpec(memory_space=pl.ANY)


def _all_gather(shards, *, name):
    n = len(shards)

    def body(*refs):
        x_refs, out_refs = refs[:n], refs[n:2 * n]
        send_sems, recv_sems, local_sems = refs[2 * n:]
        x, y, c = lax.axis_index("x"), lax.axis_index("y"), lax.axis_index("c")
        me, sibling = (x, y, c), (x, y, 1 - c)
        chips = [(1 - x, y), (x, 1 - y), (1 - x, 1 - y)]

        def blk(a, px, py, pc):
            return out_refs[a].at[4 * px + 2 * py + pc]

        def copy(a, k, block, to, src=None):
            return pltpu.make_async_remote_copy(
                src_ref=blk(a, *block) if src is None else src, dst_ref=blk(a, *block),
                send_sem=send_sems.at[7 * a + k], recv_sem=recv_sems.at[7 * a + k],
                device_id=to, device_id_type=_MESH_ID)

        mine = [pltpu.make_async_copy(x_refs[a], blk(a, *me), local_sems.at[a]) for a in range(n)]
        for cp in mine:
            cp.start()
        first = []
        for a in range(n):
            first.append(copy(a, 0, me, sibling, src=x_refs[a]))
            first += [copy(a, 1 + j, me, (*chip, c), src=x_refs[a]) for j, chip in enumerate(chips)]
        for cp in first:
            cp.start()
        passed = []
        for j, chip in enumerate(chips):
            for a in range(n):
                copy(a, 1 + j, (*chip, c), me).wait_recv()
                fwd = copy(a, 4 + j, (*chip, c), sibling)
                fwd.start()
                passed.append(fwd)
        for a in range(n):
            copy(a, 0, sibling, me).wait_recv()
        for j, chip in enumerate(chips):
            for a in range(n):
                copy(a, 4 + j, (*chip, 1 - c), me).wait_recv()
        for cp in first + passed:
            cp.wait_send()
        for cp in mine:
            cp.wait()

    return pl.pallas_call(
        body, name=name,
        out_shape=[jax.ShapeDtypeStruct((N_DEV,) + s.shape, s.dtype) for s in shards],
        in_specs=[_ANY] * n, out_specs=[_ANY] * n,
        scratch_shapes=[pltpu.SemaphoreType.DMA((7 * n,)), pltpu.SemaphoreType.DMA((7 * n,)),
                        pltpu.SemaphoreType.DMA((n,))],
    )(*shards)


_BIG = (("ffn1_w_up", 2), ("ffn1_w_down", 1), ("mix_w_in", 2), ("w_uq", 2), ("w_ukv", 2),
        ("w_br_conv", 2), ("w_br_mla", 2), ("w_o", 1), ("ffn2_w_up", 2), ("ffn2_w_down", 1))
_SMALL_SHARDED = (("meta_tokens", 1), ("mix_b_gate", 2), ("conv_w", 2), ("ln_g", 2), ("ln_b", 2))
_SMALL_REPL = ("q_norm_g", "kv_norm_g")


def _pack(arrs, dtype, row_align=8):
    flat = jnp.concatenate([a.reshape(-1).astype(dtype) for a in arrs])
    n = flat.shape[0]
    rows = -(-n // (row_align * FLAT_W)) * row_align
    return jnp.pad(flat, (0, rows * FLAT_W - n)).reshape(rows, FLAT_W)


def _unpack(flat, shapes):
    flat = flat.reshape(-1)
    out, off = [], 0
    for s in shapes:
        n = int(np.prod(s))
        out.append(flat[off:off + n].reshape(s))
        off += n
    return out


def _unpack_gathered(gathered, shapes, axes):
    g2 = gathered.reshape(N_DEV, -1)
    out, off = [], 0
    for s, ax in zip(shapes, axes):
        n = int(np.prod(s))
        blocks = g2[:, off:off + n].reshape((N_DEV,) + tuple(s))
        full = jnp.moveaxis(blocks, 0, ax)
        out.append(full.reshape(tuple(s[:ax]) + (N_DEV * s[ax],) + tuple(s[ax + 1:])))
        off += n
    return out


def _to_dest(full, ax):
    s = full.shape
    split = full.reshape(s[:ax] + (N_DEV, s[ax] // N_DEV) + s[ax + 1:])
    return jnp.moveaxis(split, ax, 0)


def _from_blocks(g, ax):
    full = jnp.moveaxis(g, 0, ax)
    s = full.shape
    return full.reshape(s[:ax] + (s[ax] * s[ax + 1],) + s[ax + 2:])


def _rope_tables(T):
    inv_freq = 1.0 / (ROPE_BASE ** (jnp.arange(0, QK_ROPE, 2, dtype=F32) / QK_ROPE))
    ang = jnp.arange(T, dtype=F32)[:, None] * inv_freq[None, :]
    cos, sin = jnp.cos(ang), jnp.sin(ang)
    half = QK_ROPE // 2
    ones = jnp.ones((T, QK_NOPE), F32)
    zeros = lambda w: jnp.zeros((T, w), F32)
    c = jnp.concatenate([ones, cos, cos, zeros(HEAD_PAD - QK_NOPE - QK_ROPE)], axis=1)
    s1 = jnp.concatenate([zeros(QK_NOPE + half), sin, zeros(HEAD_PAD - QK_NOPE - QK_ROPE)], axis=1)
    s2 = jnp.concatenate([zeros(QK_NOPE), -sin, zeros(HEAD_PAD - QK_NOPE - half)], axis=1)
    return c, s1, s2


_FFN1 = ("ffn1_w_up", "ffn1_w_down")
_MIXER = ("mix_w_in", "w_uq", "w_ukv", "w_br_conv", "w_br_mla", "w_o")
_FFN2 = ("ffn2_w_up", "ffn2_w_down")


def _mixer_weights(gathered):
    axes = dict(_BIG)
    W = {n: _from_blocks(gathered[n], axes[n] - 1) for n in _MIXER}
    w_in = W["mix_w_in"]
    w_in_p = jnp.concatenate([w_in[:, :Z_KR_END], jnp.zeros((D_MODEL, D_IN_PAD - D_IN_REAL), BF16),
                              w_in[:, Z_KR_END:]], axis=1)
    w_uq = W["w_uq"].reshape(Q_LORA, MLA_HEADS, QK_NOPE + QK_ROPE)
    wq = jnp.pad(w_uq, ((0, 0), (0, 0), (0, HEAD_PAD - QK_NOPE - QK_ROPE))).reshape(Q_LORA, D_QK)
    w_ukv = W["w_ukv"].reshape(KV_LORA, MLA_HEADS, QK_NOPE + V_HEAD)
    wk_top = jnp.pad(w_ukv[:, :, :QK_NOPE], ((0, 0), (0, 0), (0, HEAD_PAD - QK_NOPE))).reshape(KV_LORA, D_QK)
    place = np.zeros((Q_LORA - KV_LORA, MLA_HEADS, HEAD_PAD), np.float32)
    for r in range(QK_ROPE):
        place[r, :, QK_NOPE + r] = 1.0
    wk = jnp.concatenate([wk_top, jnp.asarray(place.reshape(Q_LORA - KV_LORA, D_QK), BF16)], axis=0)
    wv = w_ukv[:, :, QK_NOPE:].reshape(KV_LORA, D_MLA)
    wv_ext = jnp.pad(w_ukv[:, :, QK_NOPE:], ((0, 0), (0, 0), (0, HEAD_PAD - V_HEAD))).reshape(KV_LORA, D_QK)
    return dict(w_in=w_in_p, wq=wq, wk=wk, wv=wv, wv_ext=wv_ext,
                wbc=W["w_br_conv"], wbm=W["w_br_mla"], wo=W["w_o"])


def _row8(v):
    return jnp.pad(v, ((0, 8 - v.shape[0]), (0, 0)))


def _local_step(x, tgt, gathered_ffn1, w_blocks, S):
    big_names = [n for n, _ in _BIG]
    axes = dict(_BIG)
    t_real = N_META + x.shape[0]
    T = -(-t_real // ROW_ALIGN) * ROW_ALIGN
    pad = T - t_real
    tm = _tile(T, (384, 256, 128))
    tms = _tile(T, (768, 256, 128))
    blk = _tile(T, (768, 256, 128))
    tabs = _rope_tables(T)

    h0 = jnp.concatenate([S["meta_tokens"], x, jnp.zeros((pad, D_MODEL), F32)], axis=0)
    tgt_p = jnp.concatenate([jnp.zeros((N_META, D_MODEL), F32), tgt, jnp.zeros((pad, D_MODEL), F32)], axis=0)

    def gather_of(l, names):
        return _Exchange("gather", [w_blocks[l][n] for n in names])

    def ffn_fwd(h, hb, up8, down, g, b, tag, exchange=None):
        gu, a, *got = _ffn_up(hb, up8, name=f"ffn_up_{tag}", tm=tms, exchange=exchange)
        r, y, yb = _mm_res_ln(a, down, h, g, b, scale=0.5, name=f"ffn_down_ln_{tag}", tm=tm)
        return dict(gu=gu, a=a, r=r), y, yb, got

    saved = []
    h, hb = h0, h0.astype(BF16)
    G = dict(gathered_ffn1)
    for l in range(DEPTH):
        first = l == 0
        sv = dict(h_in=h, h_in_b=hb, up1=G["ffn1_w_up"], down1=_from_blocks(G["ffn1_w_down"], 0))
        sv["f1"], h1, h1b, got = ffn_fwd(h, hb, sv["up1"], sv["down1"], S["ln_g"][l, 0:1], S["ln_b"][l, 0:1],
                                         f"{l}a", exchange=gather_of(0, _MIXER) if first else None)
        G.update(zip(_MIXER, got))
        lw = _mixer_weights(G)
        sv["lw"] = lw
        if first:
            z, *got = _mm(h1b, lw["w_in"], out_dtype=F32, name=f"mix_in_{l}", tm=tms, tn=1024, tk=D_MODEL,
                          exchange=gather_of(0, _FFN2[:1]))
            G.update(zip(_FFN2[:1], got))
        else:
            z = _mm(h1b, lw["w_in"], out_dtype=F32, name=f"mix_in_{l}", tm=tms, tn=1024, tk=D_MODEL)
        sv["up2"] = G["ffn2_w_up"]
        conv_w8 = _row8(S["conv_w"][l])
        bg8 = _row8(S["mix_b_gate"][l])
        yc = _conv_fwd(z, conv_w8, name=f"conv_fwd_{l}", tm=tms)
        gq, gkv = S["q_norm_g"][l:l + 1], S["kv_norm_g"][l:l + 1]
        q, k, v, qn, kin = _qkv_proj(z, gq, gkv, lw["wq"], lw["wk"], lw["wv_ext"], tabs, name=f"qkv_proj_{l}", tm=tm)
        o, lse, *nxt = _attn_fwd(q, k, v, name=f"attn_fwd_{l}", blk=blk,
                                 exchange=gather_of(1, big_names) if first else None)
        mg, pa, pb = _merge(yc, o, lw["wbc"], lw["wbm"], z, bg8, name=f"merge_{l}", tm=tm)
        r2, h2, h2b = _mm_res_ln(mg, lw["wo"], h1, S["ln_g"][l, 1:2], S["ln_b"][l, 1:2], scale=1.0,
                                 name=f"wo_ln_{l}", tm=tm)
        gu, a, *got = _ffn_up(h2b, sv["up2"], name=f"ffn_up_{l}b", tm=tms,
                              exchange=gather_of(0, _FFN2[1:]) if first else None)
        G.update(zip(_FFN2[1:], got))
        sv["down2"] = _from_blocks(G["ffn2_w_down"], 0)
        r, h3, h3b = _mm_res_ln(a, sv["down2"], h2, S["ln_g"][l, 2:3], S["ln_b"][l, 2:3], scale=0.5,
                                name=f"ffn_down_ln_{l}b", tm=tm)
        sv["f2"] = dict(gu=gu, a=a, r=r)
        if first:
            G = dict(zip(big_names, nxt))
        sv.update(h1b=h1b, z=z, conv_w8=conv_w8, bg8=bg8, yc=yc, gq=gq, gkv=gkv, q=q, k=k, v=v, qn=qn, kin=kin,
                  o=o, lse=lse, mg=mg, pa=pa, pb=pb, r2=r2, h2b=h2b)
        saved.append(sv)
        h, hb = h3, h3b

    dh, loss8 = _loss_head(h, tgt_p, t_real=t_real, name="loss_head", tm=tm)

    tk = _tile(T, (2816, 768, 256, 128))
    grads = {n: [None] * DEPTH for n, _ in _BIG}
    for n in ("mix_b_gate", "conv_w", "q_norm_g", "kv_norm_g"):
        grads[n] = [None] * DEPTH
    grads["ln_g"] = [[None] * 3 for _ in range(DEPTH)]
    grads["ln_b"] = [[None] * 3 for _ in range(DEPTH)]

    def dest_of(l, names):
        return {n: grads[n][l] if n.endswith("w_up") else _to_dest(grads[n][l], axes[n] - 1) for n in names}

    def scatter_of(dest):
        return _Exchange("scatter", [dest[n].astype(BF16) for n in dest])

    dest = [{}, {}]
    received = [{}, {}]

    def ffn_bwd(dy, f, h_in_b, up8, down, g, tag, names, l, carry=None):
        dr, dfb, dg, db = _ln_bwd(dy, f["r"], g, scale=0.5, name=f"ln_bwd_{tag}", tm=tm)
        dgu, *got = _ffn_bwd_mid(dfb, down, f["gu"], name=f"ffn_bwd_mid_{tag}", tm=tms,
                                 exchange=scatter_of(carry) if carry else None)
        if carry:
            received[l].update(zip(carry, got))
        grads[names[1]][l] = _mm_tn_call(
            f["a"], dfb,
            pl.BlockSpec((None, tk, FF_BLK), lambda i, j, k: (i, k, 0)),
            pl.BlockSpec((tk, D_MODEL), lambda i, j, k: (k, 0)),
            out_shape=jax.ShapeDtypeStruct((D_FF, D_MODEL), F32),
            out_spec=pl.BlockSpec((FF_BLK, D_MODEL), lambda i, j, k: (i, 0)),
            grid=(FF_HALF_BLOCKS, 1, T // tk), name=f"dw_down_{tag}")
        own_down = dest_of(l, names[1:]) if carry else None
        d_up = _mm_tn_call(
            h_in_b, dgu,
            pl.BlockSpec((tk, D_MODEL), lambda i, j, k: (k, 0)),
            pl.BlockSpec((None, None, tk, FF_BLK),
                         lambda i, j, k: (j // FF_HALF_BLOCKS, j % FF_HALF_BLOCKS, k, 0)),
            out_shape=jax.ShapeDtypeStruct((N_DEV, D_MODEL, FF_BLK), F32),
            out_spec=pl.BlockSpec((None, D_MODEL, FF_BLK), lambda i, j, k: (j, 0, 0)),
            grid=(1, N_DEV, T // tk), name=f"dw_up_{tag}",
            exchange=scatter_of(own_down) if carry else None)
        own = None
        if carry:
            d_up, *got = d_up
            received[l].update(zip(own_down, got))
        grads[names[0]][l] = d_up
        if carry:
            own = dest_of(l, names[:1])
            dest[l].update(own_down)
            dest[l].update(own)
        row = pl.BlockSpec((tms, D_MODEL), lambda i, j, k: (i, 0))
        dh_in = _mm_call(
            dgu, up8,
            pl.BlockSpec((None, FF_HALF_BLOCKS, tms, FF_BLK), lambda i, j, k: (k, 0, i, 0)),
            pl.BlockSpec((FF_HALF_BLOCKS, D_MODEL, FF_BLK), lambda i, j, k: (k, 0, 0)),
            out_shape=jax.ShapeDtypeStruct((T, D_MODEL), F32), out_spec=row, acc_shape=(tms, D_MODEL),
            grid=(T // tms, 1, 2), name=f"ffn_dx_{tag}", trans_b=True, res=dr, res_spec=row,
            res_scale=ALPHA, pieces=FF_HALF_BLOCKS, exchange=scatter_of(own) if own else None)
        if own:
            dh_in, *got = dh_in
            received[l].update(zip(own, got))
        return dh_in, dg[0], db[0]

    for l in reversed(range(DEPTH)):
        sv = saved[l]
        lw = sv["lw"]
        dh, grads["ln_g"][l][2], grads["ln_b"][l][2] = ffn_bwd(
            dh, sv["f2"], sv["h2b"], sv["up2"], sv["down2"], S["ln_g"][l, 2:3], f"{l}b", _FFN2, l)
        dr2, dmb, dg, db = _ln_bwd(dh, sv["r2"], S["ln_g"][l, 1:2], scale=1.0, name=f"ln_bwd_{l}m", tm=tm)
        grads["ln_g"][l][1], grads["ln_b"][l][1] = dg[0], db[0]
        grads["w_o"][l] = _mm_tn(sv["mg"], dmb, name=f"dw_o_{l}", tm=D_MODEL, tn=D_MODEL, tk=tk)
        dpa, dpb, dgate, dbg = _wo_bwd(dmb, lw["wo"], sv["z"], sv["bg8"], sv["pa"], sv["pb"], name=f"wo_bwd_{l}", tm=tm)
        grads["mix_b_gate"][l] = dbg[0:2]
        grads["w_br_conv"][l] = _mm_tn(sv["yc"], dpa, name=f"dw_br_conv_{l}", tm=D_CONV, tn=D_MODEL, tk=tk)
        grads["w_br_mla"][l] = _mm_tn(sv["o"], dpb, name=f"dw_br_mla_{l}", tm=D_MLA, tn=D_MODEL, tk=tk)
        dyc = _mm(dpa, lw["wbc"], trans_b=True, out_dtype=F32, name=f"d_yconv_{l}", tm=tms, tn=D_CONV, tk=D_MODEL)
        dym = _mm(dpb, lw["wbm"], trans_b=True, out_dtype=BF16, name=f"d_ymla_{l}", tm=tms, tn=D_MLA, tk=D_MODEL)
        dz_conv, dcw = _conv_bwd(dyc, sv["z"], sv["conv_w8"], name=f"conv_bwd_{l}", tm=tms)
        grads["conv_w"][l] = dcw[0:CONV_WIDTH]
        if l == 0:
            dest[1] = dest_of(1, big_names)
            dest[0].update(dest_of(0, _FFN2))
            sent = {(1, n): dest[1][n] for n in big_names}
            sent.update({(0, n): dest[0][n] for n in _FFN2})
            dq, dk, dv, *got = _attn_bwd(
                sv["q"], sv["k"], sv["v"], sv["o"], dym, sv["lse"], name=f"attn_bwd_{l}", blk=blk,
                exchange=scatter_of(sent))
            for (layer, n), r in zip(sent, got):
                received[layer][n] = r
        else:
            dq, dk, dv = _attn_bwd(sv["q"], sv["k"], sv["v"], sv["o"], dym, sv["lse"], name=f"attn_bwd_{l}", blk=blk)
        dz_mid, dqb, dkb, dvb, dgq, dgkv = _qkv_bwd(dq, dk, dv, sv["z"], sv["gq"], sv["gkv"], lw["wq"], lw["wk"],
                                                    lw["wv"], tabs, name=f"qkv_bwd_{l}", tm=tm)
        grads["q_norm_g"][l], grads["kv_norm_g"][l] = dgq[0], dgkv[0]
        d_wq = _mm_tn(sv["qn"], dqb, name=f"dw_uq_{l}", tm=Q_LORA, tn=D_QK, tk=tk)
        d_wk = _mm_tn(sv["kin"], dkb, name=f"dw_uk_{l}", tm=Q_LORA, tn=D_QK, tk=tk)
        d_wv = _mm_tn(sv["kin"], dvb, name=f"dw_uv_{l}", tm=Q_LORA, tn=D_MLA, tk=tk)
        grads["w_uq"][l] = d_wq.reshape(Q_LORA, MLA_HEADS, HEAD_PAD)[:, :, :QK_NOPE + QK_ROPE].reshape(Q_LORA, -1)
        d_kn = d_wk[:KV_LORA].reshape(KV_LORA, MLA_HEADS, HEAD_PAD)[:, :, :QK_NOPE]
        d_vv = d_wv[:KV_LORA].reshape(KV_LORA, MLA_HEADS, V_HEAD)
        grads["w_ukv"][l] = jnp.concatenate([d_kn, d_vv], axis=-1).reshape(KV_LORA, -1)
        dz = jnp.concatenate([dz_conv, dz_mid, dgate], axis=1)
        d_win = _mm_tn(sv["h1b"], dz, name=f"dw_in_{l}", tm=D_MODEL, tn=1024, tk=tk)
        grads["mix_w_in"][l] = jnp.concatenate([d_win[:, :Z_KR_END], d_win[:, Z_KR_END + D_IN_PAD - D_IN_REAL:]], axis=1)
        dh = _mm(dz, lw["w_in"], trans_b=True, out_dtype=F32, name=f"mix_dx_{l}", res=dr2, res_scale=ALPHA,
                 tm=tms, tn=D_MODEL, tk=2048)
        carry = None
        if l == 0:
            carry = dest_of(0, _MIXER)
            dest[0].update(carry)
        dh, grads["ln_g"][l][0], grads["ln_b"][l][0] = ffn_bwd(
            dh, sv["f1"], sv["h_in_b"], sv["up1"], sv["down1"], S["ln_g"][l, 0:1], f"{l}a", _FFN1, l, carry=carry)

    small = {n: jnp.stack(grads[n]) for n in ("mix_b_gate", "conv_w", "q_norm_g", "kv_norm_g")}
    small["ln_g"] = jnp.stack([jnp.stack(g) for g in grads["ln_g"]])
    small["ln_b"] = jnp.stack([jnp.stack(g) for g in grads["ln_b"]])
    small["meta_tokens"] = dh[:N_META]
    return loss8, dh[N_META:t_real], dest, received, small


def kernel(x, meta_tokens, ffn1_w_up, ffn1_w_down, mix_w_in, mix_b_gate, conv_w, q_norm_g, w_uq, kv_norm_g, w_ukv, w_br_conv, w_br_mla, w_o, ffn2_w_up, ffn2_w_down, ln_g, ln_b, loss_target, m_meta_tokens, m_ffn1_w_up, m_ffn1_w_down, m_mix_w_in, m_mix_b_gate, m_conv_w, m_q_norm_g, m_w_uq, m_kv_norm_g, m_w_ukv, m_w_br_conv, m_w_br_mla, m_w_o, m_ffn2_w_up, m_ffn2_w_down, m_ln_g, m_ln_b, v_meta_tokens, v_ffn1_w_up, v_ffn1_w_down, v_mix_w_in, v_mix_b_gate, v_conv_w, v_q_norm_g, v_w_uq, v_kv_norm_g, v_w_ukv, v_w_br_conv, v_w_br_mla, v_w_o, v_ffn2_w_up, v_ffn2_w_down, v_ln_g, v_ln_b):
    names = ["meta_tokens", "ffn1_w_up", "ffn1_w_down", "mix_w_in", "mix_b_gate", "conv_w", "q_norm_g", "w_uq",
             "kv_norm_g", "w_ukv", "w_br_conv", "w_br_mla", "w_o", "ffn2_w_up", "ffn2_w_down", "ln_g", "ln_b"]
    w = dict(zip(names, (meta_tokens, ffn1_w_up, ffn1_w_down, mix_w_in, mix_b_gate, conv_w, q_norm_g, w_uq,
                         kv_norm_g, w_ukv, w_br_conv, w_br_mla, w_o, ffn2_w_up, ffn2_w_down, ln_g, ln_b)))
    m = dict(zip(names, (m_meta_tokens, m_ffn1_w_up, m_ffn1_w_down, m_mix_w_in, m_mix_b_gate, m_conv_w, m_q_norm_g,
                         m_w_uq, m_kv_norm_g, m_w_ukv, m_w_br_conv, m_w_br_mla, m_w_o, m_ffn2_w_up, m_ffn2_w_down,
                         m_ln_g, m_ln_b)))
    v = dict(zip(names, (v_meta_tokens, v_ffn1_w_up, v_ffn1_w_down, v_mix_w_in, v_mix_b_gate, v_conv_w, v_q_norm_g,
                         v_w_uq, v_kv_norm_g, v_w_ukv, v_w_br_conv, v_w_br_mla, v_w_o, v_ffn2_w_up, v_ffn2_w_down,
                         v_ln_g, v_ln_b)))
    ix, iy, ic = lax.axis_index("x"), lax.axis_index("y"), lax.axis_index("c")
    dev = 4 * ix + 2 * iy + ic

    big_names = [n for n, _ in _BIG]
    small_names = [n for n, _ in _SMALL_SHARDED]
    small_axes = [a for _, a in _SMALL_SHARDED]
    small_shapes = [w[n].shape for n in small_names]
    gathered = _all_gather([w[n][0].astype(BF16) for n in _FFN1] + [_pack([w[n] for n in small_names], F32)],
                           name="all_gather_weights")
    gathered_ffn1 = dict(zip(_FFN1, gathered[:-1]))
    w_blocks = [{n: w[n][l].astype(BF16) for n in big_names} for l in range(DEPTH)]
    S = dict(zip(small_names, _unpack_gathered(gathered[-1], small_shapes, small_axes)))
    S["q_norm_g"], S["kv_norm_g"] = q_norm_g, kv_norm_g

    loss8, grad_x, dest, received, G = _local_step(x[0], loss_target[0], gathered_ffn1, w_blocks, S)

    my_dev = dev.reshape(1).astype(jnp.int32)
    big_res = [{}, {}, {}, {}]
    for n in big_names:
        res = [_adamw(w[n], m[n], v[n], l, [(dest[l][n], None)] + [(received[l][n], k) for k in range(N_DEV - 1)],
                      my_dev, name=f"adamw{l}_{n}") for l in range(DEPTH)]
        for kind in range(4):
            big_res[kind][n] = jnp.stack([r[kind] for r in res])

    small_all = small_names + list(_SMALL_REPL)
    part = _pack([G[n] for n in small_all] + [loss8[0, 0:1]], F32)
    full_shapes = [G[n].shape for n in small_all] + [(1,)]
    summed = _sum8(_all_gather([part], name="all_gather_small_grads")[0], name="sum_small_grads")
    unpacked = _unpack(summed, full_shapes)
    loss = unpacked[-1][0]
    g_full = dict(zip(small_all, unpacked[:-1]))
    g_loc = []
    for n in small_all:
        if n in _SMALL_REPL:
            g_loc.append(g_full[n])
        else:
            ax = dict(_SMALL_SHARDED)[n]
            g_loc.append(lax.dynamic_slice_in_dim(g_full[n], dev * w[n].shape[ax], w[n].shape[ax], axis=ax))
    loc_shapes = [w[n].shape for n in small_all]
    g_pack = _pack(g_loc, F32)
    small_out = _adamw(_pack([w[n] for n in small_all], F32)[None], _pack([m[n] for n in small_all], F32)[None],
                       _pack([v[n] for n in small_all], F32)[None], 0, [(g_pack[None], 0)],
                       jnp.zeros((1,), jnp.int32), name="adamw_small")
    small_res = [dict(zip(small_all, _unpack(o, loc_shapes))) for o in small_out]

    outs = [loss, grad_x[None]]
    for kind in range(4):
        for n in names:
            outs.append(big_res[kind][n] if n in big_res[kind] else small_res[kind][n])
    return tuple(outs)
```

```python
import functools

import numpy as np
import jax
import jax.numpy as jnp
from jax import lax
from jax.experimental import pallas as pl
from jax.experimental.pallas import tpu as pltpu

F32 = jnp.float32
BF16 = jnp.bfloat16

D_MODEL = 1024
DEPTH = 2
N_META = 16
D_CONV = 512
CONV_WIDTH = 3
MLA_HEADS = 8
QK_NOPE = 64
QK_ROPE = 32
V_HEAD = 64
Q_LORA = 256
KV_LORA = 128
D_MLA = MLA_HEADS * V_HEAD
ROPE_BASE = 10000.0
NEG_INF = -1e30
D_FF = 2816
ALPHA = (2 * DEPTH) ** 0.25
LN_EPS = 1e-5
RMS_EPS = 1e-6
ATTN_SCALE = (QK_NOPE + QK_ROPE) ** -0.5
ADAM_LR = 0.001
ADAM_B1 = 0.9
ADAM_B2 = 0.999
ADAM_EPS = 1e-08
ADAM_WD = 0.01
ADAM_STEP = 10

N_DEV = 8
HEAD_PAD = 128
HEADS_PER_STEP = 2
FWD_HEADS_PER_STEP = 4
FF_BLK = 2 * D_FF // N_DEV
FF_HALF_BLOCKS = N_DEV // 2
D_QK = MLA_HEADS * HEAD_PAD
Z_CONV = 0
Z_MID = 1536
Z_GATE = 2048
D_IN_PAD = 4096
D_IN_REAL = 4000
Z_KR_END = Z_MID + Q_LORA + KV_LORA + QK_ROPE

V7X_VMEM_LIMIT = 56 * 1024 * 1024
LANE = 128
ROW_ALIGN = 256


def _tile(n, cands):
    for c in cands:
        if n % c == 0:
            return c
    raise ValueError(f"no tile for {n} in {cands}")


def _params(sem):
    return pltpu.CompilerParams(dimension_semantics=sem, vmem_limit_bytes=V7X_VMEM_LIMIT)


def _mm_call(a, b, a_spec, b_spec, *, out_shape, out_spec, acc_shape, grid, name, trans_b=False,
             res=None, res_spec=None, res_scale=1.0, pieces=1, exchange=None):
    nk = grid[2]
    has_res = res is not None
    out_dtype = out_shape.dtype
    dims = (((1,), (1,)), ((), ())) if trans_b else (((1,), (0,)), ((), ()))

    def body(*refs):
        if has_res:
            a_ref, b_ref, r_ref, o_ref, acc = refs
        else:
            a_ref, b_ref, o_ref, acc = refs
        k = pl.program_id(2)
        if pieces == 1:
            part = lax.dot_general(a_ref[...], b_ref[...], dims, preferred_element_type=F32)
        else:
            part = lax.dot_general(a_ref[0], b_ref[0], dims, preferred_element_type=F32)
            for p in range(1, pieces):
                part = part + lax.dot_general(a_ref[p], b_ref[p], dims, preferred_element_type=F32)

        @pl.when(k == 0)
        def _():
            acc[...] = part

        @pl.when(k > 0)
        def _():
            acc[...] += part

        @pl.when(k == nk - 1)
        def _():
            out = acc[...]
            if has_res:
                out = out + res_scale * r_ref[...]
            o_ref[...] = out.astype(out_dtype)

    in_specs = [a_spec, b_spec]
    args = [a, b]
    if has_res:
        in_specs.append(res_spec)
        args.append(res)
    if exchange is None:
        return pl.pallas_call(
            body, name=name, grid=grid, in_specs=in_specs, out_specs=out_spec, out_shape=out_shape,
            scratch_shapes=[pltpu.VMEM(acc_shape, F32)],
            compiler_params=_params(("parallel", "parallel", "arbitrary")),
        )(*args)
    ex_args, ex_specs, ex_out, ex_scratch = _exchange_operands(exchange)
    return pl.pallas_call(
        _carry_exchange(body, exchange, 0, len(args), 1, grid), name=name, grid=grid,
        in_specs=in_specs + ex_specs, out_specs=[out_spec] + ex_specs, out_shape=[out_shape] + ex_out,
        scratch_shapes=[pltpu.VMEM(acc_shape, F32)] + ex_scratch,
        compiler_params=_params(("arbitrary", "arbitrary", "arbitrary")),
    )(*args, *ex_args)


def _mm(a, b, *, out_dtype, name, trans_b=False, res=None, res_scale=1.0, tm, tn, tk, exchange=None):
    M, K = a.shape
    N = b.shape[0] if trans_b else b.shape[1]
    assert M % tm == 0 and N % tn == 0 and K % tk == 0
    b_spec = (pl.BlockSpec((tn, tk), lambda i, j, k: (j, k)) if trans_b
              else pl.BlockSpec((tk, tn), lambda i, j, k: (k, j)))
    tile = pl.BlockSpec((tm, tn), lambda i, j, k: (i, j))
    return _mm_call(a, b, pl.BlockSpec((tm, tk), lambda i, j, k: (i, k)), b_spec,
                    out_shape=jax.ShapeDtypeStruct((M, N), out_dtype), out_spec=tile, acc_shape=(tm, tn),
                    grid=(M // tm, N // tn, K // tk), name=name, trans_b=trans_b,
                    res=res, res_spec=tile, res_scale=res_scale, exchange=exchange)


def _mm_tn_call(a, b, a_spec, b_spec, *, out_shape, out_spec, grid, name, exchange=None):
    def body(a_ref, b_ref, o_ref):
        k = pl.program_id(2)
        part = lax.dot_general(a_ref[...], b_ref[...], (((0,), (0,)), ((), ())),
                               preferred_element_type=F32)

        @pl.when(k == 0)
        def _():
            o_ref[...] = part

        @pl.when(k > 0)
        def _():
            o_ref[...] += part

    if exchange is None:
        return pl.pallas_call(
            body, name=name, grid=grid, in_specs=[a_spec, b_spec], out_specs=out_spec, out_shape=out_shape,
            compiler_params=_params(("parallel", "parallel", "arbitrary")),
        )(a, b)
    ex_args, ex_specs, ex_out, ex_scratch = _exchange_operands(exchange)
    return pl.pallas_call(
        _carry_exchange(body, exchange, 0, 2, 1, grid), name=name, grid=grid,
        in_specs=[a_spec, b_spec] + ex_specs, out_specs=[out_spec] + ex_specs, out_shape=[out_shape] + ex_out,
        scratch_shapes=ex_scratch,
        compiler_params=_params(("arbitrary", "arbitrary", "arbitrary")),
    )(a, b, *ex_args)


def _mm_tn(a, b, *, name, tm, tn, tk):
    T, M = a.shape
    N = b.shape[1]
    assert M % tm == 0 and N % tn == 0 and T % tk == 0
    return _mm_tn_call(a, b, pl.BlockSpec((tk, tm), lambda i, j, k: (k, i)),
                       pl.BlockSpec((tk, tn), lambda i, j, k: (k, j)),
                       out_shape=jax.ShapeDtypeStruct((M, N), F32),
                       out_spec=pl.BlockSpec((tm, tn), lambda i, j, k: (i, j)),
                       grid=(M // tm, N // tn, T // tk), name=name)


def _ffn_up(hb, w_up8, *, name, tm, exchange=None):
    T = hb.shape[0]

    def body(h_ref, wg_ref, wu_ref, gu_ref, a_ref):
        h = h_ref[...]
        g = jnp.dot(h, wg_ref[...], preferred_element_type=F32)
        u = jnp.dot(h, wu_ref[...], preferred_element_type=F32)
        sg = jax.nn.sigmoid(g)
        silu = g * sg
        gu_ref[0] = (u * (sg * (1.0 + g * (1.0 - sg)))).astype(BF16)
        gu_ref[1] = silu.astype(BF16)
        a_ref[...] = (silu * u).astype(BF16)

    grid = (FF_HALF_BLOCKS, T // tm)
    ex_args, ex_specs, ex_out, ex_scratch = _exchange_operands(exchange)
    return pl.pallas_call(
        _carry_exchange(body, exchange, 0, 3, 2, grid), name=name, grid=grid,
        in_specs=[pl.BlockSpec((tm, D_MODEL), lambda j, i: (i, 0)),
                  pl.BlockSpec((None, D_MODEL, FF_BLK), lambda j, i: (j, 0, 0)),
                  pl.BlockSpec((None, D_MODEL, FF_BLK), lambda j, i: (j + FF_HALF_BLOCKS, 0, 0))] + ex_specs,
        out_specs=[pl.BlockSpec((2, None, tm, FF_BLK), lambda j, i: (0, j, i, 0)),
                   pl.BlockSpec((None, tm, FF_BLK), lambda j, i: (j, i, 0))] + ex_specs,
        out_shape=[jax.ShapeDtypeStruct((2, FF_HALF_BLOCKS, T, FF_BLK), BF16),
                   jax.ShapeDtypeStruct((FF_HALF_BLOCKS, T, FF_BLK), BF16)] + ex_out,
        scratch_shapes=ex_scratch,
        compiler_params=_params(("arbitrary", "arbitrary") if exchange else ("parallel", "parallel")),
    )(hb, w_up8, w_up8, *ex_args)


def _mm_res_ln(a, w, res, g, b, *, scale, name, tm):
    split = a.ndim == 3
    if split:
        S, T, Ks = a.shape
        K = S * Ks
    else:
        T, K = a.shape

    def body(a_ref, w_ref, res_ref, g_ref, b_ref, r_ref, y_ref, yb_ref):
        if split:
            f = jnp.dot(a_ref[0], w_ref[0:Ks, :], preferred_element_type=F32)
            for s in range(1, S):
                f = f + jnp.dot(a_ref[s], w_ref[s * Ks:(s + 1) * Ks, :], preferred_element_type=F32)
        else:
            f = jnp.dot(a_ref[...], w_ref[...], preferred_element_type=F32)
        r = ALPHA * res_ref[...] + scale * f
        mu = jnp.mean(r, axis=-1, keepdims=True)
        xc = r - mu
        var = jnp.mean(xc * xc, axis=-1, keepdims=True)
        y = xc * lax.rsqrt(var + LN_EPS) * g_ref[...] + b_ref[...]
        r_ref[...] = r
        y_ref[...] = y
        yb_ref[...] = y.astype(BF16)

    row = pl.BlockSpec((tm, D_MODEL), lambda i: (i, 0))
    vec = pl.BlockSpec((1, D_MODEL), lambda i: (0, 0))
    return pl.pallas_call(
        body, name=name, grid=(T // tm,),
        in_specs=[pl.BlockSpec((S, tm, Ks), lambda i: (0, i, 0)) if split else pl.BlockSpec((tm, K), lambda i: (i, 0)),
                  pl.BlockSpec((K, D_MODEL), lambda i: (0, 0)), row, vec, vec],
        out_specs=[row, row, row],
        out_shape=[jax.ShapeDtypeStruct((T, D_MODEL), F32), jax.ShapeDtypeStruct((T, D_MODEL), F32),
                   jax.ShapeDtypeStruct((T, D_MODEL), BF16)],
        compiler_params=_params(("parallel",)),
    )(a, w, res, g, b)


def _conv_fwd(z, conv_w8, *, name, tm):
    T = z.shape[0]
    hb = tm // 8

    def body(b_ref, c_ref, h_ref, cp_ref, hp_ref, w_ref, y_ref):
        i = pl.program_id(0)
        u = c_ref[...] * h_ref[...]
        up = jnp.where(i > 0, cp_ref[...] * hp_ref[...], 0.0)
        ue = jnp.concatenate([up, u], axis=0)
        s1 = pltpu.roll(ue, 1, 0)[8:]
        s2 = pltpu.roll(ue, 2, 0)[8:]
        w = w_ref[...]
        conv = w[0:1] * s2 + w[1:2] * s1 + w[2:3] * u
        y_ref[...] = (b_ref[...] * conv).astype(BF16)

    def col(c):
        return pl.BlockSpec((tm, D_CONV), lambda i: (i, c))

    def prev(c):
        return pl.BlockSpec((8, D_CONV), lambda i: (jnp.maximum(i * hb - 1, 0), c))

    return pl.pallas_call(
        body, name=name, grid=(T // tm,),
        in_specs=[col(0), col(1), col(2), prev(1), prev(2), pl.BlockSpec((8, D_CONV), lambda i: (0, 0))],
        out_specs=pl.BlockSpec((tm, D_CONV), lambda i: (i, 0)),
        out_shape=jax.ShapeDtypeStruct((T, D_CONV), BF16),
        compiler_params=_params(("parallel",)),
    )(z, z, z, z, z, conv_w8)


def _rope(x, c, s1, s2):
    n = x.shape[-1]
    return x * c + pltpu.roll(x, 16, 1) * s1 + pltpu.roll(x, n - 16, 1) * s2


def _rope_t(d, c, s1, s2):
    n = d.shape[-1]
    return d * c + pltpu.roll(d * s1, n - 16, 1) + pltpu.roll(d * s2, 16, 1)


def _rms(x, g):
    rstd = lax.rsqrt(jnp.mean(x * x, axis=-1, keepdims=True) + RMS_EPS)
    return x * rstd * g


def _qkv_proj(z, gq, gkv, wq, wk, wv_ext, tabs, *, name, tm):
    T = z.shape[0]

    def body(z_ref, gq_ref, gkv_ref, wq_ref, wk_ref, wv_ref, c_ref, s1_ref, s2_ref,
             q_ref, k_ref, v_ref, qn_ref, kin_ref):
        zz = z_ref[...]
        qn = _rms(zz[:, :Q_LORA], gq_ref[...]).astype(BF16)
        kvn = _rms(zz[:, Q_LORA:Q_LORA + KV_LORA], gkv_ref[...]).astype(BF16)
        kin = jnp.concatenate([kvn, zz[:, Q_LORA + KV_LORA:].astype(BF16)], axis=-1)
        c = jnp.tile(c_ref[...], (1, MLA_HEADS))
        s1 = jnp.tile(s1_ref[...], (1, MLA_HEADS))
        s2 = jnp.tile(s2_ref[...], (1, MLA_HEADS))
        qpre = jnp.dot(qn, wq_ref[...], preferred_element_type=F32)
        kpre = jnp.dot(kin, wk_ref[...], preferred_element_type=F32)
        q_ref[...] = (_rope(qpre, c, s1, s2) * (ATTN_SCALE * LOG2_E)).astype(BF16)
        k_ref[...] = _rope(kpre, c, s1, s2).astype(BF16)
        vv = jnp.dot(kvn, wv_ref[...], preferred_element_type=F32)
        lane = lax.broadcasted_iota(jnp.int32, vv.shape, 1)
        v_ref[...] = jnp.where((lane & (HEAD_PAD - 1)) < V_HEAD, vv, 1.0).astype(BF16)
        qn_ref[...] = qn
        kin_ref[...] = kin

    def full(shape):
        return pl.BlockSpec(shape, lambda i: (0, 0))

    def rows(w, c=0):
        return pl.BlockSpec((tm, w), lambda i: (i, c))

    return pl.pallas_call(
        body, name=name, grid=(T // tm,),
        in_specs=[rows(512, Z_MID // 512), full((1, Q_LORA)), full((1, KV_LORA)),
                  full((Q_LORA, D_QK)), full((Q_LORA, D_QK)), full((KV_LORA, D_QK)),
                  rows(LANE), rows(LANE), rows(LANE)],
        out_specs=[rows(D_QK), rows(D_QK), rows(D_QK), rows(Q_LORA), rows(Q_LORA)],
        out_shape=[jax.ShapeDtypeStruct((T, D_QK), BF16), jax.ShapeDtypeStruct((T, D_QK), BF16),
                   jax.ShapeDtypeStruct((T, D_QK), BF16), jax.ShapeDtypeStruct((T, Q_LORA), BF16),
                   jax.ShapeDtypeStruct((T, Q_LORA), BF16)],
        compiler_params=_params(("parallel",)),
    )(z, gq, gkv, wq, wk, wv_ext, *tabs)


SOFTMAX_ROWS = 32
LOG2_E = 1.4426950408889634
_NT = (((1,), (1,)), ((), ()))
_TN = (((0,), (0,)), ((), ()))


def _diag_mask(s, row0, col0=0):
    row = lax.broadcasted_iota(jnp.int32, s.shape, 0) + row0
    col = lax.broadcasted_iota(jnp.int32, s.shape, 1) + col0
    return jnp.where(col <= row, s, NEG_INF)


_RELATIONS = tuple((rx, ry, rc) for rx in (0, 1) for ry in (0, 1) for rc in (0, 1))[1:]


class _Exchange:
    def __init__(self, kind, arrays):
        assert kind in ("gather", "scatter")
        self.kind, self.arrays, self.n = kind, list(arrays), len(arrays)

    def out_shapes(self):
        if self.kind == "gather":
            return [jax.ShapeDtypeStruct((N_DEV,) + a.shape, a.dtype) for a in self.arrays]
        return [jax.ShapeDtypeStruct((N_DEV - 1,) + a.shape[1:], a.dtype) for a in self.arrays]

    def scratch_shapes(self):
        sems = [pltpu.SemaphoreType.DMA((7 * self.n,)), pltpu.SemaphoreType.DMA((7 * self.n,))]
        if self.kind == "gather":
            sems.append(pltpu.SemaphoreType.DMA((self.n,)))
        return sems

    def _copies(self, src_refs, out_refs, sems):
        x, y, c = lax.axis_index("x"), lax.axis_index("y"), lax.axis_index("c")
        me = 4 * x + 2 * y + c
        sends, recvs, local = [], [], []
        for a in range(self.n):
            for k, rel in enumerate(_RELATIONS):
                peer = tuple((1 - p) if r else p for p, r in zip((x, y, c), rel))
                peer_index = 4 * peer[0] + 2 * peer[1] + peer[2]
                if self.kind == "gather":
                    src, lands_there, lands_here = src_refs[a], out_refs[a].at[me], out_refs[a].at[peer_index]
                else:
                    src, lands_there, lands_here = src_refs[a].at[peer_index], out_refs[a].at[k], out_refs[a].at[k]
                for dst, group in ((lands_there, sends), (lands_here, recvs)):
                    group.append(pltpu.make_async_remote_copy(
                        src_ref=src, dst_ref=dst, send_sem=sems[0].at[7 * a + k], recv_sem=sems[1].at[7 * a + k],
                        device_id=peer, device_id_type=_MESH_ID))
            if self.kind == "gather":
                local.append(pltpu.make_async_copy(src_refs[a], out_refs[a].at[me], sems[2].at[a]))
        return sends, recvs, local

    def start(self, src_refs, out_refs, sems):
        sends, _, local = self._copies(src_refs, out_refs, sems)
        for cp in local + sends:
            cp.start()

    def wait(self, src_refs, out_refs, sems):
        sends, recvs, local = self._copies(src_refs, out_refs, sems)
        for cp in recvs:
            cp.wait_recv()
        for cp in sends:
            cp.wait_send()
        for cp in local:
            cp.wait()


def _exchange_operands(exchange):
    if exchange is None:
        return [], [], [], []
    return exchange.arrays, [_ANY] * exchange.n, exchange.out_shapes(), exchange.scratch_shapes()


def _carry_exchange(body, exchange, n_prefetch, n_in, n_out, grid):
    if exchange is None:
        return body
    n = exchange.n
    last = tuple(g - 1 for g in grid)

    def wrapped(*refs):
        head = refs[:n_prefetch + n_in]
        src_refs = refs[n_prefetch + n_in:n_prefetch + n_in + n]
        rest = refs[n_prefetch + n_in + n:]
        outs, out_refs, rest = rest[:n_out], rest[n_out:n_out + n], rest[n_out + n:]
        n_sems = len(exchange.scratch_shapes())
        scratch, sems = rest[:len(rest) - n_sems], rest[len(rest) - n_sems:]
        at_first = functools.reduce(jnp.logical_and, [pl.program_id(d) == 0 for d in range(len(grid))])
        at_last = functools.reduce(jnp.logical_and, [pl.program_id(d) == last[d] for d in range(len(grid))])

        @pl.when(at_first)
        def _():
            exchange.start(src_refs, out_refs, sems)

        body(*head, *outs, *scratch)

        @pl.when(at_last)
        def _():
            exchange.wait(src_refs, out_refs, sems)

    return wrapped


def _attn_fwd(q, k, v, *, name, blk, exchange=None):
    T = q.shape[0]
    n = T // blk
    hp = FWD_HEADS_PER_STEP
    qi = np.array([i for i in range(n) for j in range(i + 1)], np.int32)
    kj = np.array([j for i in range(n) for j in range(i + 1)], np.int32)

    rc = _tile(blk, (SOFTMAX_ROWS,))

    def body(qi_ref, kj_ref, q_ref, k_ref, v_ref, o_ref, lse_ref, m_sc, acc_sc, s_sc, p_sc, red_sc):
        s_id = pl.program_id(1)
        i = qi_ref[s_id]
        j = kj_ref[s_id]

        @pl.when(j == 0)
        def _():
            m_sc[...] = jnp.full(m_sc.shape, NEG_INF, F32)
            acc_sc[...] = jnp.zeros(acc_sc.shape, F32)

        def head_step(hh, diagonal):
            hs = slice(hh * HEAD_PAD, (hh + 1) * HEAD_PAD)
            s_sc[hh] = lax.dot_general(q_ref[:, hs], k_ref[:, hs], _NT, preferred_element_type=F32)
            lanes = [slice(t * LANE, (t + 1) * LANE) for t in range(blk // LANE)]
            for r in range(blk // rc):
                rows = slice(r * rc, (r + 1) * rc)
                s = s_sc[hh, rows, :]
                if diagonal:
                    s = _diag_mask(s, r * rc)
                    s_sc[hh, rows, :] = s
                pm = s[:, lanes[0]]
                for t in lanes[1:]:
                    pm = jnp.maximum(pm, s[:, t])
                red_sc[hh, rows, :] = pm
            m_old = m_sc[hh]
            row_max = jnp.max(red_sc[hh], axis=-1, keepdims=True)
            m_new = jnp.maximum(m_old, jnp.broadcast_to(row_max, (blk, LANE)))
            a = jnp.exp2(m_old - m_new)
            m_sc[hh] = m_new
            for r in range(blk // rc):
                rows = slice(r * rc, (r + 1) * rc)
                mb = m_sc[hh, rows, :]
                for t in lanes:
                    p_sc[hh, rows, t] = jnp.exp2(s_sc[hh, rows, t] - mb).astype(BF16)
            acc_sc[hh] = a * acc_sc[hh] + jnp.dot(p_sc[hh], v_ref[:, hs], preferred_element_type=F32)

        @pl.when(j < i)
        def _():
            for hh in range(hp):
                head_step(hh, False)

        @pl.when(j == i)
        def _():
            for hh in range(hp):
                head_step(hh, True)
            for hh in range(hp):
                acc = acc_sc[hh]
                swapped = pltpu.roll(acc, V_HEAD, 1)
                o_ref[:, hh * V_HEAD:(hh + 1) * V_HEAD] = (acc / swapped)[:, :V_HEAD].astype(BF16)
                lane = lax.broadcasted_iota(jnp.int32, acc.shape, 1)
                denom = jnp.where(lane < V_HEAD, swapped, acc)
                lse_ref[hh] = m_sc[hh] + jnp.log(denom) * LOG2_E

    grid = (MLA_HEADS // hp, len(qi))
    ex_args, ex_specs, ex_out, ex_scratch = _exchange_operands(exchange)
    grid_spec = pltpu.PrefetchScalarGridSpec(
        num_scalar_prefetch=2, grid=grid,
        in_specs=[pl.BlockSpec((blk, hp * HEAD_PAD), lambda g, s, qi, kj: (qi[s], g)),
                  pl.BlockSpec((blk, hp * HEAD_PAD), lambda g, s, qi, kj: (kj[s], g)),
                  pl.BlockSpec((blk, hp * HEAD_PAD), lambda g, s, qi, kj: (kj[s], g))] + ex_specs,
        out_specs=[pl.BlockSpec((blk, hp * V_HEAD), lambda g, s, qi, kj: (qi[s], g)),
                   pl.BlockSpec((hp, blk, LANE), lambda g, s, qi, kj: (g, qi[s], 0))] + ex_specs,
        scratch_shapes=[pltpu.VMEM((hp, blk, LANE), F32), pltpu.VMEM((hp, blk, HEAD_PAD), F32),
                        pltpu.VMEM((hp, blk, blk), F32), pltpu.VMEM((hp, blk, blk), BF16),
                        pltpu.VMEM((hp, blk, LANE), F32)] + ex_scratch)
    return pl.pallas_call(
        _carry_exchange(body, exchange, 2, 3, 2, grid), name=name, grid_spec=grid_spec,
        out_shape=[jax.ShapeDtypeStruct((T, D_MLA), BF16),
                   jax.ShapeDtypeStruct((MLA_HEADS, T, LANE), F32)] + ex_out,
        compiler_params=_params(("arbitrary", "arbitrary") if exchange else ("parallel", "arbitrary")),
    )(jnp.asarray(qi), jnp.asarray(kj), q, k, v, *ex_args)


def _merge(yc, ym, wbc, wbm, z, bg, *, name, tm):
    T = yc.shape[0]

    def body(yc_ref, ym_ref, wbc_ref, wbm_ref, gc_ref, gm_ref, bg_ref, mg_ref, pa_ref, pb_ref):
        pa = jnp.dot(yc_ref[...], wbc_ref[...], preferred_element_type=F32)
        pb = jnp.dot(ym_ref[...], wbm_ref[...], preferred_element_type=F32)
        bgv = bg_ref[...]
        sa = jax.nn.sigmoid(gc_ref[...] + bgv[0:1])
        sb = jax.nn.sigmoid(gm_ref[...] + bgv[1:2])
        mg_ref[...] = (sa * pa + sb * pb).astype(BF16)
        pa_ref[...] = pa.astype(BF16)
        pb_ref[...] = pb.astype(BF16)

    row = pl.BlockSpec((tm, D_MODEL), lambda i: (i, 0))
    return pl.pallas_call(
        body, name=name, grid=(T // tm,),
        in_specs=[pl.BlockSpec((tm, D_CONV), lambda i: (i, 0)), pl.BlockSpec((tm, D_MLA), lambda i: (i, 0)),
                  pl.BlockSpec((D_CONV, D_MODEL), lambda i: (0, 0)), pl.BlockSpec((D_MLA, D_MODEL), lambda i: (0, 0)),
                  pl.BlockSpec((tm, D_MODEL), lambda i: (i, Z_GATE // D_MODEL)),
                  pl.BlockSpec((tm, D_MODEL), lambda i: (i, Z_GATE // D_MODEL + 1)),
                  pl.BlockSpec((8, D_MODEL), lambda i: (0, 0))],
        out_specs=[row, row, row],
        out_shape=[jax.ShapeDtypeStruct((T, D_MODEL), BF16)] * 3,
        compiler_params=_params(("parallel",)),
    )(yc, ym, wbc, wbm, z, z, bg)


def _loss_head(h, tgt, *, t_real, name, tm):
    T = h.shape[0]

    def body(h_ref, t_ref, dy_ref, loss_ref):
        i = pl.program_id(0)
        row = lax.broadcasted_iota(jnp.int32, (tm, 1), 0) + i * tm
        valid = (row >= N_META) & (row < t_real)
        err = jnp.where(valid, h_ref[...] - t_ref[...], 0.0)
        dy_ref[...] = err * (1.0 / D_MODEL)
        part = 0.5 * jnp.sum(jnp.sum(err * err, axis=-1, keepdims=True) * (1.0 / D_MODEL), axis=0, keepdims=True)

        @pl.when(i == 0)
        def _():
            loss_ref[...] = jnp.zeros(loss_ref.shape, F32)

        loss_ref[...] += jnp.broadcast_to(part, loss_ref.shape)

    row_spec = pl.BlockSpec((tm, D_MODEL), lambda i: (i, 0))
    return pl.pallas_call(
        body, name=name, grid=(T // tm,),
        in_specs=[row_spec, row_spec],
        out_specs=[row_spec, pl.BlockSpec((8, LANE), lambda i: (0, 0))],
        out_shape=[jax.ShapeDtypeStruct((T, D_MODEL), F32), jax.ShapeDtypeStruct((8, LANE), F32)],
        compiler_params=_params(("arbitrary",)),
    )(h, tgt)


def _ln_bwd(dy, r, g, *, scale, name, tm):
    T = dy.shape[0]

    def body(dy_ref, r_ref, g_ref, dr_ref, drb_ref, dg_ref, db_ref):
        i = pl.program_id(0)
        rr = r_ref[...]
        dyv = dy_ref[...]
        mu = jnp.mean(rr, axis=-1, keepdims=True)
        xc = rr - mu
        rstd = lax.rsqrt(jnp.mean(xc * xc, axis=-1, keepdims=True) + LN_EPS)
        xh = xc * rstd
        dxh = dyv * g_ref[...]
        m1 = jnp.mean(dxh, axis=-1, keepdims=True)
        m2 = jnp.mean(dxh * xh, axis=-1, keepdims=True)
        dr = rstd * (dxh - m1 - xh * m2)
        dr_ref[...] = dr
        drb_ref[...] = (scale * dr).astype(BF16)

        @pl.when(i == 0)
        def _():
            dg_ref[...] = jnp.zeros(dg_ref.shape, F32)
            db_ref[...] = jnp.zeros(db_ref.shape, F32)

        dg_ref[0:1, :] += jnp.sum(dyv * xh, axis=0, keepdims=True)
        db_ref[0:1, :] += jnp.sum(dyv, axis=0, keepdims=True)

    row = pl.BlockSpec((tm, D_MODEL), lambda i: (i, 0))
    acc = pl.BlockSpec((8, D_MODEL), lambda i: (0, 0))
    return pl.pallas_call(
        body, name=name, grid=(T // tm,),
        in_specs=[row, row, pl.BlockSpec((1, D_MODEL), lambda i: (0, 0))],
        out_specs=[row, row, acc, acc],
        out_shape=[jax.ShapeDtypeStruct((T, D_MODEL), F32), jax.ShapeDtypeStruct((T, D_MODEL), BF16),
                   jax.ShapeDtypeStruct((8, D_MODEL), F32), jax.ShapeDtypeStruct((8, D_MODEL), F32)],
        compiler_params=_params(("arbitrary",)),
    )(dy, r, g)


def _ffn_bwd_mid(dfb, w_down, gu, *, name, tm, exchange=None):
    T = dfb.shape[0]

    def body(df_ref, w_ref, gu_ref, o_ref):
        da = lax.dot_general(df_ref[...], w_ref[...], _NT, preferred_element_type=F32)
        o_ref[0] = (da * gu_ref[0].astype(F32)).astype(BF16)
        o_ref[1] = (da * gu_ref[1].astype(F32)).astype(BF16)

    grid = (FF_HALF_BLOCKS, T // tm)
    ex_args, ex_specs, ex_out, ex_scratch = _exchange_operands(exchange)
    return pl.pallas_call(
        _carry_exchange(body, exchange, 0, 3, 1, grid), name=name, grid=grid,
        in_specs=[pl.BlockSpec((tm, D_MODEL), lambda j, i: (i, 0)),
                  pl.BlockSpec((FF_BLK, D_MODEL), lambda j, i: (j, 0)),
                  pl.BlockSpec((2, None, tm, FF_BLK), lambda j, i: (0, j, i, 0))] + ex_specs,
        out_specs=[pl.BlockSpec((2, None, tm, FF_BLK), lambda j, i: (0, j, i, 0))] + ex_specs,
        out_shape=[jax.ShapeDtypeStruct((2, FF_HALF_BLOCKS, T, FF_BLK), BF16)] + ex_out,
        scratch_shapes=ex_scratch,
        compiler_params=_params(("arbitrary", "arbitrary") if exchange else ("parallel", "parallel")),
    )(dfb, w_down, gu, *ex_args)


def _wo_bwd(dmb, wo_t, z, bg, pa, pb, *, name, tm):
    T = dmb.shape[0]

    def body(dm_ref, w_ref, gc_ref, gm_ref, bg_ref, pa_ref, pb_ref, dpa_ref, dpb_ref, dg_ref, dbg_ref):
        i = pl.program_id(0)
        dm = lax.dot_general(dm_ref[...], w_ref[...], _NT, preferred_element_type=F32)
        bgv = bg_ref[...]
        sa = jax.nn.sigmoid(gc_ref[...] + bgv[0:1])
        sb = jax.nn.sigmoid(gm_ref[...] + bgv[1:2])
        dpa_ref[...] = (dm * sa).astype(BF16)
        dpb_ref[...] = (dm * sb).astype(BF16)
        dga = dm * pa_ref[...].astype(F32) * (sa * (1.0 - sa))
        dgb = dm * pb_ref[...].astype(F32) * (sb * (1.0 - sb))
        dg_ref[:, :D_MODEL] = dga.astype(BF16)
        dg_ref[:, D_MODEL:] = dgb.astype(BF16)

        @pl.when(i == 0)
        def _():
            dbg_ref[...] = jnp.zeros(dbg_ref.shape, F32)

        dbg_ref[0:1, :] += jnp.sum(dga, axis=0, keepdims=True)
        dbg_ref[1:2, :] += jnp.sum(dgb, axis=0, keepdims=True)

    row = pl.BlockSpec((tm, D_MODEL), lambda i: (i, 0))
    return pl.pallas_call(
        body, name=name, grid=(T // tm,),
        in_specs=[row, pl.BlockSpec((D_MODEL, D_MODEL), lambda i: (0, 0)),
                  pl.BlockSpec((tm, D_MODEL), lambda i: (i, Z_GATE // D_MODEL)),
                  pl.BlockSpec((tm, D_MODEL), lambda i: (i, Z_GATE // D_MODEL + 1)),
                  pl.BlockSpec((8, D_MODEL), lambda i: (0, 0)), row, row],
        out_specs=[row, row, pl.BlockSpec((tm, 2 * D_MODEL), lambda i: (i, 0)),
                   pl.BlockSpec((8, D_MODEL), lambda i: (0, 0))],
        out_shape=[jax.ShapeDtypeStruct((T, D_MODEL), BF16), jax.ShapeDtypeStruct((T, D_MODEL), BF16),
                   jax.ShapeDtypeStruct((T, 2 * D_MODEL), BF16), jax.ShapeDtypeStruct((8, D_MODEL), F32)],
        compiler_params=_params(("arbitrary",)),
    )(dmb, wo_t, z, z, bg, pa, pb)


def _conv_bwd(dy, z, conv_w8, *, name, tm):
    T = dy.shape[0]
    n = T // tm
    hb = tm // 8

    def body(dy_ref, b_ref, c_ref, h_ref, cp_ref, hp_ref, dyn_ref, bn_ref, w_ref, dz_ref, dw_ref):
        i = pl.program_id(0)
        u = c_ref[...] * h_ref[...]
        up = jnp.where(i > 0, cp_ref[...] * hp_ref[...], 0.0)
        ue = jnp.concatenate([up, u], axis=0)
        s1 = pltpu.roll(ue, 1, 0)[8:]
        s2 = pltpu.roll(ue, 2, 0)[8:]
        w = w_ref[...]
        conv = w[0:1] * s2 + w[1:2] * s1 + w[2:3] * u
        dyv = dy_ref[...]
        e = dyv * b_ref[...]
        en = jnp.where(i < n - 1, dyn_ref[...] * bn_ref[...], 0.0)
        ee = jnp.concatenate([e, en], axis=0)
        e1 = pltpu.roll(ee, tm + 8 - 1, 0)[:tm]
        e2 = pltpu.roll(ee, tm + 8 - 2, 0)[:tm]
        du = w[2:3] * e + w[1:2] * e1 + w[0:1] * e2
        dz_ref[:, 0:D_CONV] = (dyv * conv).astype(BF16)
        dz_ref[:, D_CONV:2 * D_CONV] = (du * h_ref[...]).astype(BF16)
        dz_ref[:, 2 * D_CONV:] = (du * c_ref[...]).astype(BF16)

        @pl.when(i == 0)
        def _():
            dw_ref[...] = jnp.zeros(dw_ref.shape, F32)

        dw_ref[0:1, :] += jnp.sum(e * s2, axis=0, keepdims=True)
        dw_ref[1:2, :] += jnp.sum(e * s1, axis=0, keepdims=True)
        dw_ref[2:3, :] += jnp.sum(e * u, axis=0, keepdims=True)

    def col(c):
        return pl.BlockSpec((tm, D_CONV), lambda i: (i, c))

    def prev(c):
        return pl.BlockSpec((8, D_CONV), lambda i: (jnp.maximum(i * hb - 1, 0), c))

    def nxt(c):
        return pl.BlockSpec((8, D_CONV), lambda i: (jnp.minimum((i + 1) * hb, T // 8 - 1), c))

    return pl.pallas_call(
        body, name=name, grid=(n,),
        in_specs=[col(0), col(0), col(1), col(2), prev(1), prev(2), nxt(0), nxt(0),
                  pl.BlockSpec((8, D_CONV), lambda i: (0, 0))],
        out_specs=[pl.BlockSpec((tm, 3 * D_CONV), lambda i: (i, 0)), pl.BlockSpec((8, D_CONV), lambda i: (0, 0))],
        out_shape=[jax.ShapeDtypeStruct((T, 3 * D_CONV), BF16), jax.ShapeDtypeStruct((8, D_CONV), F32)],
        compiler_params=_params(("arbitrary",)),
    )(dy, z, z, z, z, z, dy, z, conv_w8)


def _attn_bwd(q, k, v, o, do, lse, *, name, blk, exchange=None):
    T = q.shape[0]
    n = T // blk
    hp = HEADS_PER_STEP
    qi = np.array([i for j in range(n) for i in range(j, n)], np.int32)
    kj = np.array([j for j in range(n) for i in range(j, n)], np.int32)

    rc = _tile(blk, (SOFTMAX_ROWS,))

    def body(qi_ref, kj_ref, q_ref, k_ref, v_ref, o_ref, do_ref, lse_ref, dq_ref, dk_ref, dv_ref,
             dk_sc, dv_sc, s_sc, dp_sc, p_sc, ds_sc, delta_sc):
        s_id = pl.program_id(1)
        i = qi_ref[s_id]
        j = kj_ref[s_id]

        @pl.when(s_id == 0)
        def _():
            dq_ref[...] = jnp.zeros(dq_ref.shape, F32)

        @pl.when(i == j)
        def _():
            dk_sc[...] = jnp.zeros(dk_sc.shape, F32)
            dv_sc[...] = jnp.zeros(dv_sc.shape, F32)

        q_rows = pl.ds(pl.multiple_of(i * blk, blk), blk)

        def head_step(hh, diagonal):
            hs = slice(hh * HEAD_PAD, (hh + 1) * HEAD_PAD)
            vs = slice(hh * V_HEAD, (hh + 1) * V_HEAD)
            qh = q_ref[:, hs]
            kh = k_ref[:, hs]
            doh = do_ref[:, vs]
            s_sc[hh] = lax.dot_general(qh, kh, _NT, preferred_element_type=F32)
            vh = v_ref[:, hh * HEAD_PAD:hh * HEAD_PAD + V_HEAD]
            dp_sc[hh] = lax.dot_general(doh, vh, _NT, preferred_element_type=F32)
            delta = jnp.sum(doh.astype(F32) * o_ref[:, vs].astype(F32), axis=-1, keepdims=True)
            delta_sc[hh] = jnp.broadcast_to(delta, (blk, LANE))
            for r in range(blk // rc):
                rows = slice(r * rc, (r + 1) * rc)
                lse = lse_ref[hh, rows, :]
                dl = delta_sc[hh, rows, :]
                for t in range(blk // LANE):
                    cols = slice(t * LANE, (t + 1) * LANE)
                    s = s_sc[hh, rows, cols]
                    if diagonal:
                        s = _diag_mask(s, r * rc, t * LANE)
                    p = jnp.exp2(s - lse)
                    p_sc[hh, rows, cols] = p.astype(BF16)
                    ds_sc[hh, rows, cols] = (p * (dp_sc[hh, rows, cols] - dl)).astype(BF16)
            dv_sc[hh] += lax.dot_general(p_sc[hh], doh, _TN, preferred_element_type=F32)
            dk_sc[hh] += lax.dot_general(ds_sc[hh], qh, _TN, preferred_element_type=F32)
            dq_ref[q_rows, hs] += jnp.dot(ds_sc[hh], kh, preferred_element_type=F32)

        @pl.when(j < i)
        def _():
            for hh in range(hp):
                head_step(hh, False)

        @pl.when(j == i)
        def _():
            for hh in range(hp):
                head_step(hh, True)

        @pl.when(i == n - 1)
        def _():
            for hh in range(hp):
                dk_ref[:, hh * HEAD_PAD:(hh + 1) * HEAD_PAD] = dk_sc[hh] * (1.0 / LOG2_E)
                dv_ref[:, hh * V_HEAD:(hh + 1) * V_HEAD] = dv_sc[hh]

    wq = hp * HEAD_PAD
    wv = hp * V_HEAD
    grid = (MLA_HEADS // hp, len(qi))
    ex_args, ex_specs, ex_out, ex_scratch = _exchange_operands(exchange)
    grid_spec = pltpu.PrefetchScalarGridSpec(
        num_scalar_prefetch=2, grid=grid,
        in_specs=[pl.BlockSpec((blk, wq), lambda g, s, qi, kj: (qi[s], g)),
                  pl.BlockSpec((blk, wq), lambda g, s, qi, kj: (kj[s], g)),
                  pl.BlockSpec((blk, wq), lambda g, s, qi, kj: (kj[s], g)),
                  pl.BlockSpec((blk, wv), lambda g, s, qi, kj: (qi[s], g)),
                  pl.BlockSpec((blk, wv), lambda g, s, qi, kj: (qi[s], g)),
                  pl.BlockSpec((hp, blk, LANE), lambda g, s, qi, kj: (g, qi[s], 0))] + ex_specs,
        out_specs=[pl.BlockSpec((T, wq), lambda g, s, qi, kj: (0, g)),
                   pl.BlockSpec((blk, wq), lambda g, s, qi, kj: (kj[s], g)),
                   pl.BlockSpec((blk, wv), lambda g, s, qi, kj: (kj[s], g))] + ex_specs,
        scratch_shapes=[pltpu.VMEM((hp, blk, HEAD_PAD), F32), pltpu.VMEM((hp, blk, V_HEAD), F32),
                        pltpu.VMEM((hp, blk, blk), F32), pltpu.VMEM((hp, blk, blk), F32),
                        pltpu.VMEM((hp, blk, blk), BF16), pltpu.VMEM((hp, blk, blk), BF16),
                        pltpu.VMEM((hp, blk, LANE), F32)] + ex_scratch)
    return pl.pallas_call(
        _carry_exchange(body, exchange, 2, 6, 3, grid), name=name, grid_spec=grid_spec,
        out_shape=[jax.ShapeDtypeStruct((T, D_QK), F32), jax.ShapeDtypeStruct((T, D_QK), F32),
                   jax.ShapeDtypeStruct((T, D_MLA), F32)] + ex_out,
        compiler_params=_params(("arbitrary", "arbitrary") if exchange else ("parallel", "arbitrary")),
    )(jnp.asarray(qi), jnp.asarray(kj), q, k, v, o, do, lse, *ex_args)


def _qkv_bwd(dq, dk, dv, z, gq, gkv, wq_t, wk_t, wv_t, tabs, *, name, tm):
    T = dq.shape[0]

    def body(dq_ref, dk_ref, dv_ref, z_ref, gq_ref, gkv_ref, wq_ref, wk_ref, wv_ref, c_ref, s1_ref, s2_ref,
             dz_ref, dqb_ref, dkb_ref, dvb_ref, dgq_ref, dgkv_ref):
        i = pl.program_id(0)
        c = jnp.tile(c_ref[...], (1, MLA_HEADS))
        s1 = jnp.tile(s1_ref[...], (1, MLA_HEADS))
        s2 = jnp.tile(s2_ref[...], (1, MLA_HEADS))
        dqp = _rope_t(dq_ref[...] * ATTN_SCALE, c, s1, s2).astype(BF16)
        dkp = _rope_t(dk_ref[...], c, s1, s2).astype(BF16)
        dvb = dv_ref[...].astype(BF16)
        dqb_ref[...] = dqp
        dkb_ref[...] = dkp
        dvb_ref[...] = dvb
        dqn = lax.dot_general(dqp, wq_ref[...], _NT, preferred_element_type=F32)
        dkin = lax.dot_general(dkp, wk_ref[...], _NT, preferred_element_type=F32)
        dkvn = dkin[:, :KV_LORA] + lax.dot_general(dvb, wv_ref[...], _NT, preferred_element_type=F32)
        zz = z_ref[...]

        def rms_bwd(x, g, dy):
            rstd = lax.rsqrt(jnp.mean(x * x, axis=-1, keepdims=True) + RMS_EPS)
            xh = x * rstd
            dxh = dy * g
            dx = rstd * (dxh - xh * jnp.mean(dxh * xh, axis=-1, keepdims=True))
            return dx, jnp.sum(dy * xh, axis=0, keepdims=True)

        dcq, dgq = rms_bwd(zz[:, :Q_LORA], gq_ref[...], dqn)
        dckv, dgkv = rms_bwd(zz[:, Q_LORA:Q_LORA + KV_LORA], gkv_ref[...], dkvn)
        dz_ref[:, :Q_LORA] = dcq.astype(BF16)
        dz_ref[:, Q_LORA:Q_LORA + KV_LORA] = dckv.astype(BF16)
        dz_ref[:, Q_LORA + KV_LORA:] = dkin[:, KV_LORA:].astype(BF16)

        @pl.when(i == 0)
        def _():
            dgq_ref[...] = jnp.zeros(dgq_ref.shape, F32)
            dgkv_ref[...] = jnp.zeros(dgkv_ref.shape, F32)

        dgq_ref[0:1, :] += dgq
        dgkv_ref[0:1, :] += dgkv

    def full(shape):
        return pl.BlockSpec(shape, lambda i: (0, 0))

    def rows(w, c=0):
        return pl.BlockSpec((tm, w), lambda i: (i, c))

    return pl.pallas_call(
        body, name=name, grid=(T // tm,),
        in_specs=[rows(D_QK), rows(D_QK), rows(D_MLA), rows(512, Z_MID // 512),
                  full((1, Q_LORA)), full((1, KV_LORA)),
                  full((Q_LORA, D_QK)), full((Q_LORA, D_QK)), full((KV_LORA, D_MLA)),
                  rows(LANE), rows(LANE), rows(LANE)],
        out_specs=[rows(512), rows(D_QK), rows(D_QK), rows(D_MLA), full((8, Q_LORA)), full((8, KV_LORA))],
        out_shape=[jax.ShapeDtypeStruct((T, 512), BF16), jax.ShapeDtypeStruct((T, D_QK), BF16),
                   jax.ShapeDtypeStruct((T, D_QK), BF16), jax.ShapeDtypeStruct((T, D_MLA), BF16),
                   jax.ShapeDtypeStruct((8, Q_LORA), F32), jax.ShapeDtypeStruct((8, KV_LORA), F32)],
        compiler_params=_params(("arbitrary",)),
    )(dq, dk, dv, z, gq, gkv, wq_t, wk_t, wv_t, *tabs)


FLAT_W = 1024


ELEMENTWISE_TILE_BYTES = 768 * 1024


def _row_tile(R, C):
    width = -(-C // LANE) * LANE * 4
    best = None
    for t in range(16, R + 1, 16):
        if R % t == 0 and t * width <= ELEMENTWISE_TILE_BYTES:
            best = t
    if best is None:
        best = R
    return best


def _adamw(w, m, v, layer, parts, part_index, *, name):
    _, R, C = w.shape
    tr = _row_tile(R, C)
    bc1 = 1.0 - ADAM_B1 ** ADAM_STEP
    bc2 = 1.0 - ADAM_B2 ** ADAM_STEP
    n_parts = len(parts)

    def body(idx_ref, w_ref, m_ref, v_ref, *refs):
        g_refs = refs[:n_parts]
        g_out, d_out, m_out, v_out = refs[n_parts:]
        g = g_refs[0][...].astype(F32)
        for r in g_refs[1:]:
            g = g + r[...].astype(F32)
        wv = w_ref[...]
        mn = ADAM_B1 * m_ref[...] + (1.0 - ADAM_B1) * g
        vn = ADAM_B2 * v_ref[...] + (1.0 - ADAM_B2) * (g * g)
        m_hat = mn / bc1
        v_hat = vn / bc2
        g_out[...] = g
        d_out[...] = -ADAM_LR * (m_hat / (jnp.sqrt(v_hat) + ADAM_EPS) + ADAM_WD * wv)
        m_out[...] = mn
        v_out[...] = vn

    layer_row = pl.BlockSpec((None, tr, C), lambda i, idx: (layer, i, 0))
    in_specs = [layer_row, layer_row, layer_row]
    args = [w, m, v]
    for arr, slot in parts:
        if slot is None:
            in_specs.append(pl.BlockSpec((None, tr, C), lambda i, idx: (idx[0], i, 0)))
        else:
            in_specs.append(pl.BlockSpec((None, tr, C), lambda i, idx, slot=slot: (slot, i, 0)))
        args.append(arr)
    grid_spec = pltpu.PrefetchScalarGridSpec(
        num_scalar_prefetch=1, grid=(R // tr,), in_specs=in_specs,
        out_specs=[pl.BlockSpec((tr, C), lambda i, idx: (i, 0))] * 4)
    return pl.pallas_call(
        body, name=name, grid_spec=grid_spec,
        out_shape=[jax.ShapeDtypeStruct((R, C), F32)] * 4,
        compiler_params=_params(("parallel",)),
    )(part_index, *args)


def _sum8(parts, *, name):
    _, R, _ = parts.shape

    def body(p_ref, o_ref):
        acc = p_ref[0]
        for d in range(1, N_DEV):
            acc = acc + p_ref[d]
        o_ref[...] = acc

    return pl.pallas_call(
        body, name=name, grid=(1,),
        in_specs=[pl.BlockSpec((N_DEV, R, FLAT_W), lambda i: (0, 0, 0))],
        out_specs=pl.BlockSpec((R, FLAT_W), lambda i: (0, 0)),
        out_shape=jax.ShapeDtypeStruct((R, FLAT_W), F32),
        compiler_params=_params(("arbitrary",)),
    )(parts)


_MESH_ID = pl.DeviceIdType.MESH
_ANY = pl.BlockSpec(memory_space=pl.ANY)


def _all_gather(shards, *, name):
    n = len(shards)

    def body(*refs):
        x_refs, out_refs = refs[:n], refs[n:2 * n]
        send_sems, recv_sems, local_sems = refs[2 * n:]
        x, y, c = lax.axis_index("x"), lax.axis_index("y"), lax.axis_index("c")
        me, sibling = (x, y, c), (x, y, 1 - c)
        chips = [(1 - x, y), (x, 1 - y), (1 - x, 1 - y)]

        def blk(a, px, py, pc):
            return out_refs[a].at[4 * px + 2 * py + pc]

        def copy(a, k, block, to, src=None):
            return pltpu.make_async_remote_copy(
                src_ref=blk(a, *block) if src is None else src, dst_ref=blk(a, *block),
                send_sem=send_sems.at[7 * a + k], recv_sem=recv_sems.at[7 * a + k],
                device_id=to, device_id_type=_MESH_ID)

        mine = [pltpu.make_async_copy(x_refs[a], blk(a, *me), local_sems.at[a]) for a in range(n)]
        for cp in mine:
            cp.start()
        first = []
        for a in range(n):
            first.append(copy(a, 0, me, sibling, src=x_refs[a]))
            first += [copy(a, 1 + j, me, (*chip, c), src=x_refs[a]) for j, chip in enumerate(chips)]
        for cp in first:
            cp.start()
        passed = []
        for j, chip in enumerate(chips):
            for a in range(n):
                copy(a, 1 + j, (*chip, c), me).wait_recv()
                fwd = copy(a, 4 + j, (*chip, c), sibling)
                fwd.start()
                passed.append(fwd)
        for a in range(n):
            copy(a, 0, sibling, me).wait_recv()
        for j, chip in enumerate(chips):
            for a in range(n):
                copy(a, 4 + j, (*chip, 1 - c), me).wait_recv()
        for cp in first + passed:
            cp.wait_send()
        for cp in mine:
            cp.wait()

    return pl.pallas_call(
        body, name=name,
        out_shape=[jax.ShapeDtypeStruct((N_DEV,) + s.shape, s.dtype) for s in shards],
        in_specs=[_ANY] * n, out_specs=[_ANY] * n,
        scratch_shapes=[pltpu.SemaphoreType.DMA((7 * n,)), pltpu.SemaphoreType.DMA((7 * n,)),
                        pltpu.SemaphoreType.DMA((n,))],
    )(*shards)


_BIG = (("ffn1_w_up", 2), ("ffn1_w_down", 1), ("mix_w_in", 2), ("w_uq", 2), ("w_ukv", 2),
        ("w_br_conv", 2), ("w_br_mla", 2), ("w_o", 1), ("ffn2_w_up", 2), ("ffn2_w_down", 1))
_SMALL_SHARDED = (("meta_tokens", 1), ("mix_b_gate", 2), ("conv_w", 2), ("ln_g", 2), ("ln_b", 2))
_SMALL_REPL = ("q_norm_g", "kv_norm_g")


def _pack(arrs, dtype, row_align=8):
    flat = jnp.concatenate([a.reshape(-1).astype(dtype) for a in arrs])
    n = flat.shape[0]
    rows = -(-n // (row_align * FLAT_W)) * row_align
    return jnp.pad(flat, (0, rows * FLAT_W - n)).reshape(rows, FLAT_W)


def _unpack(flat, shapes):
    flat = flat.reshape(-1)
    out, off = [], 0
    for s in shapes:
        n = int(np.prod(s))
        out.append(flat[off:off + n].reshape(s))
        off += n
    return out


def _unpack_gathered(gathered, shapes, axes):
    g2 = gathered.reshape(N_DEV, -1)
    out, off = [], 0
    for s, ax in zip(shapes, axes):
        n = int(np.prod(s))
        blocks = g2[:, off:off + n].reshape((N_DEV,) + tuple(s))
        full = jnp.moveaxis(blocks, 0, ax)
        out.append(full.reshape(tuple(s[:ax]) + (N_DEV * s[ax],) + tuple(s[ax + 1:])))
        off += n
    return out


def _to_dest(full, ax):
    s = full.shape
    split = full.reshape(s[:ax] + (N_DEV, s[ax] // N_DEV) + s[ax + 1:])
    return jnp.moveaxis(split, ax, 0)


def _from_blocks(g, ax):
    full = jnp.moveaxis(g, 0, ax)
    s = full.shape
    return full.reshape(s[:ax] + (s[ax] * s[ax + 1],) + s[ax + 2:])


def _rope_tables(T):
    inv_freq = 1.0 / (ROPE_BASE ** (jnp.arange(0, QK_ROPE, 2, dtype=F32) / QK_ROPE))
    ang = jnp.arange(T, dtype=F32)[:, None] * inv_freq[None, :]
    cos, sin = jnp.cos(ang), jnp.sin(ang)
    half = QK_ROPE // 2
    ones = jnp.ones((T, QK_NOPE), F32)
    zeros = lambda w: jnp.zeros((T, w), F32)
    c = jnp.concatenate([ones, cos, cos, zeros(HEAD_PAD - QK_NOPE - QK_ROPE)], axis=1)
    s1 = jnp.concatenate([zeros(QK_NOPE + half), sin, zeros(HEAD_PAD - QK_NOPE - QK_ROPE)], axis=1)
    s2 = jnp.concatenate([zeros(QK_NOPE), -sin, zeros(HEAD_PAD - QK_NOPE - half)], axis=1)
    return c, s1, s2


_FFN1 = ("ffn1_w_up", "ffn1_w_down")
_MIXER = ("mix_w_in", "w_uq", "w_ukv", "w_br_conv", "w_br_mla", "w_o")
_FFN2 = ("ffn2_w_up", "ffn2_w_down")
_MIXER_EARLY = ("w_br_conv", "w_br_mla", "w_o")


def _mixer_weights(gathered):
    axes = dict(_BIG)
    W = {n: _from_blocks(gathered[n], axes[n] - 1) for n in _MIXER}
    w_in = W["mix_w_in"]
    w_in_p = jnp.concatenate([w_in[:, :Z_KR_END], jnp.zeros((D_MODEL, D_IN_PAD - D_IN_REAL), BF16),
                              w_in[:, Z_KR_END:]], axis=1)
    w_uq = W["w_uq"].reshape(Q_LORA, MLA_HEADS, QK_NOPE + QK_ROPE)
    wq = jnp.pad(w_uq, ((0, 0), (0, 0), (0, HEAD_PAD - QK_NOPE - QK_ROPE))).reshape(Q_LORA, D_QK)
    w_ukv = W["w_ukv"].reshape(KV_LORA, MLA_HEADS, QK_NOPE + V_HEAD)
    wk_top = jnp.pad(w_ukv[:, :, :QK_NOPE], ((0, 0), (0, 0), (0, HEAD_PAD - QK_NOPE))).reshape(KV_LORA, D_QK)
    place = np.zeros((Q_LORA - KV_LORA, MLA_HEADS, HEAD_PAD), np.float32)
    for r in range(QK_ROPE):
        place[r, :, QK_NOPE + r] = 1.0
    wk = jnp.concatenate([wk_top, jnp.asarray(place.reshape(Q_LORA - KV_LORA, D_QK), BF16)], axis=0)
    wv = w_ukv[:, :, QK_NOPE:].reshape(KV_LORA, D_MLA)
    wv_ext = jnp.pad(w_ukv[:, :, QK_NOPE:], ((0, 0), (0, 0), (0, HEAD_PAD - V_HEAD))).reshape(KV_LORA, D_QK)
    return dict(w_in=w_in_p, wq=wq, wk=wk, wv=wv, wv_ext=wv_ext,
                wbc=W["w_br_conv"], wbm=W["w_br_mla"], wo=W["w_o"])


def _row8(v):
    return jnp.pad(v, ((0, 8 - v.shape[0]), (0, 0)))


def _local_step(x, tgt, gathered_ffn1, w_blocks, S):
    big_names = [n for n, _ in _BIG]
    axes = dict(_BIG)
    t_real = N_META + x.shape[0]
    T = -(-t_real // ROW_ALIGN) * ROW_ALIGN
    pad = T - t_real
    tm = _tile(T, (384, 256, 128))
    tms = _tile(T, (768, 256, 128))
    blk = _tile(T, (768, 256, 128))
    tabs = _rope_tables(T)

    h0 = jnp.concatenate([S["meta_tokens"], x, jnp.zeros((pad, D_MODEL), F32)], axis=0)
    tgt_p = jnp.concatenate([jnp.zeros((N_META, D_MODEL), F32), tgt, jnp.zeros((pad, D_MODEL), F32)], axis=0)

    def gather_of(l, names):
        return _Exchange("gather", [w_blocks[l][n] for n in names])

    def ffn_fwd(h, hb, up8, down, g, b, tag, exchange=None):
        gu, a, *got = _ffn_up(hb, up8, name=f"ffn_up_{tag}", tm=tms, exchange=exchange)
        r, y, yb = _mm_res_ln(a, down, h, g, b, scale=0.5, name=f"ffn_down_ln_{tag}", tm=tm)
        return dict(gu=gu, a=a, r=r), y, yb, got

    saved = []
    h, hb = h0, h0.astype(BF16)
    G = dict(gathered_ffn1)
    for l in range(DEPTH):
        first = l == 0
        sv = dict(h_in=h, h_in_b=hb, up1=G["ffn1_w_up"], down1=_from_blocks(G["ffn1_w_down"], 0))
        sv["f1"], h1, h1b, got = ffn_fwd(h, hb, sv["up1"], sv["down1"], S["ln_g"][l, 0:1], S["ln_b"][l, 0:1],
                                         f"{l}a", exchange=gather_of(0, _MIXER) if first else None)
        G.update(zip(_MIXER, got))
        lw = _mixer_weights(G)
        sv["lw"] = lw
        if first:
            z, *got = _mm(h1b, lw["w_in"], out_dtype=F32, name=f"mix_in_{l}", tm=tms, tn=1024, tk=D_MODEL,
                          exchange=gather_of(0, _FFN2[:1]))
            G.update(zip(_FFN2[:1], got))
        else:
            z = _mm(h1b, lw["w_in"], out_dtype=F32, name=f"mix_in_{l}", tm=tms, tn=1024, tk=D_MODEL)
        sv["up2"] = G["ffn2_w_up"]
        conv_w8 = _row8(S["conv_w"][l])
        bg8 = _row8(S["mix_b_gate"][l])
        yc = _conv_fwd(z, conv_w8, name=f"conv_fwd_{l}", tm=tms)
        gq, gkv = S["q_norm_g"][l:l + 1], S["kv_norm_g"][l:l + 1]
        q, k, v, qn, kin = _qkv_proj(z, gq, gkv, lw["wq"], lw["wk"], lw["wv_ext"], tabs, name=f"qkv_proj_{l}", tm=tm)
        o, lse, *nxt = _attn_fwd(q, k, v, name=f"attn_fwd_{l}", blk=blk,
                                 exchange=gather_of(1, big_names) if first else None)
        mg, pa, pb = _merge(yc, o, lw["wbc"], lw["wbm"], z, bg8, name=f"merge_{l}", tm=tm)
        r2, h2, h2b = _mm_res_ln(mg, lw["wo"], h1, S["ln_g"][l, 1:2], S["ln_b"][l, 1:2], scale=1.0,
                                 name=f"wo_ln_{l}", tm=tm)
        gu, a, *got = _ffn_up(h2b, sv["up2"], name=f"ffn_up_{l}b", tm=tms,
                              exchange=gather_of(0, _FFN2[1:]) if first else None)
        G.update(zip(_FFN2[1:], got))
        sv["down2"] = _from_blocks(G["ffn2_w_down"], 0)
        r, h3, h3b = _mm_res_ln(a, sv["down2"], h2, S["ln_g"][l, 2:3], S["ln_b"][l, 2:3], scale=0.5,
                                name=f"ffn_down_ln_{l}b", tm=tm)
        sv["f2"] = dict(gu=gu, a=a, r=r)
        if first:
            G = dict(zip(big_names, nxt))
        sv.update(h1b=h1b, z=z, conv_w8=conv_w8, bg8=bg8, yc=yc, gq=gq, gkv=gkv, q=q, k=k, v=v, qn=qn, kin=kin,
                  o=o, lse=lse, mg=mg, pa=pa, pb=pb, r2=r2, h2b=h2b)
        saved.append(sv)
        h, hb = h3, h3b

    dh, loss8 = _loss_head(h, tgt_p, t_real=t_real, name="loss_head", tm=tm)

    tk = _tile(T, (2816, 768, 256, 128))
    grads = {n: [None] * DEPTH for n, _ in _BIG}
    for n in ("mix_b_gate", "conv_w", "q_norm_g", "kv_norm_g"):
        grads[n] = [None] * DEPTH
    grads["ln_g"] = [[None] * 3 for _ in range(DEPTH)]
    grads["ln_b"] = [[None] * 3 for _ in range(DEPTH)]

    def dest_of(l, names):
        return {n: grads[n][l] if n.endswith("w_up") else _to_dest(grads[n][l], axes[n] - 1) for n in names}

    def scatter_of(dest):
        return _Exchange("scatter", [dest[n].astype(BF16) for n in dest])

    dest = [{}, {}]
    received = [{}, {}]

    def ffn_bwd(dy, f, h_in_b, up8, down, g, tag, names, l, carry=None):
        dr, dfb, dg, db = _ln_bwd(dy, f["r"], g, scale=0.5, name=f"ln_bwd_{tag}", tm=tm)
        dgu, *got = _ffn_bwd_mid(dfb, down, f["gu"], name=f"ffn_bwd_mid_{tag}", tm=tms,
                                 exchange=scatter_of(carry) if carry else None)
        if carry:
            received[l].update(zip(carry, got))
        grads[names[1]][l] = _mm_tn_call(
            f["a"], dfb,
            pl.BlockSpec((None, tk, FF_BLK), lambda i, j, k: (i, k, 0)),
            pl.BlockSpec((tk, D_MODEL), lambda i, j, k: (k, 0)),
            out_shape=jax.ShapeDtypeStruct((D_FF, D_MODEL), F32),
            out_spec=pl.BlockSpec((FF_BLK, D_MODEL), lambda i, j, k: (i, 0)),
            grid=(FF_HALF_BLOCKS, 1, T // tk), name=f"dw_down_{tag}")
        own_down = dest_of(l, names[1:]) if carry else None
        d_up = _mm_tn_call(
            h_in_b, dgu,
            pl.BlockSpec((tk, D_MODEL), lambda i, j, k: (k, 0)),
            pl.BlockSpec((None, None, tk, FF_BLK),
                         lambda i, j, k: (j // FF_HALF_BLOCKS, j % FF_HALF_BLOCKS, k, 0)),
            out_shape=jax.ShapeDtypeStruct((N_DEV, D_MODEL, FF_BLK), F32),
            out_spec=pl.BlockSpec((None, D_MODEL, FF_BLK), lambda i, j, k: (j, 0, 0)),
            grid=(1, N_DEV, T // tk), name=f"dw_up_{tag}",
            exchange=scatter_of(own_down) if carry else None)
        own = None
        if carry:
            d_up, *got = d_up
            received[l].update(zip(own_down, got))
        grads[names[0]][l] = d_up
        if carry:
            own = dest_of(l, names[:1])
            dest[l].update(own_down)
            dest[l].update(own)
        row = pl.BlockSpec((tms, D_MODEL), lambda i, j, k: (i, 0))
        dh_in = _mm_call(
            dgu, up8,
            pl.BlockSpec((None, FF_HALF_BLOCKS, tms, FF_BLK), lambda i, j, k: (k, 0, i, 0)),
            pl.BlockSpec((FF_HALF_BLOCKS, D_MODEL, FF_BLK), lambda i, j, k: (k, 0, 0)),
            out_shape=jax.ShapeDtypeStruct((T, D_MODEL), F32), out_spec=row, acc_shape=(tms, D_MODEL),
            grid=(T // tms, 1, 2), name=f"ffn_dx_{tag}", trans_b=True, res=dr, res_spec=row,
            res_scale=ALPHA, pieces=FF_HALF_BLOCKS, exchange=scatter_of(own) if own else None)
        if own:
            dh_in, *got = dh_in
            received[l].update(zip(own, got))
        return dh_in, dg[0], db[0]

    for l in reversed(range(DEPTH)):
        sv = saved[l]
        lw = sv["lw"]
        dh, grads["ln_g"][l][2], grads["ln_b"][l][2] = ffn_bwd(
            dh, sv["f2"], sv["h2b"], sv["up2"], sv["down2"], S["ln_g"][l, 2:3], f"{l}b", _FFN2, l)
        dr2, dmb, dg, db = _ln_bwd(dh, sv["r2"], S["ln_g"][l, 1:2], scale=1.0, name=f"ln_bwd_{l}m", tm=tm)
        grads["ln_g"][l][1], grads["ln_b"][l][1] = dg[0], db[0]
        grads["w_o"][l] = _mm_tn(sv["mg"], dmb, name=f"dw_o_{l}", tm=D_MODEL, tn=D_MODEL, tk=tk)
        dpa, dpb, dgate, dbg = _wo_bwd(dmb, lw["wo"], sv["z"], sv["bg8"], sv["pa"], sv["pb"], name=f"wo_bwd_{l}", tm=tm)
        grads["mix_b_gate"][l] = dbg[0:2]
        grads["w_br_conv"][l] = _mm_tn(sv["yc"], dpa, name=f"dw_br_conv_{l}", tm=D_CONV, tn=D_MODEL, tk=tk)
        grads["w_br_mla"][l] = _mm_tn(sv["o"], dpb, name=f"dw_br_mla_{l}", tm=D_MLA, tn=D_MODEL, tk=tk)
        dyc = _mm(dpa, lw["wbc"], trans_b=True, out_dtype=F32, name=f"d_yconv_{l}", tm=tms, tn=D_CONV, tk=D_MODEL)
        dym = _mm(dpb, lw["wbm"], trans_b=True, out_dtype=BF16, name=f"d_ymla_{l}", tm=tms, tn=D_MLA, tk=D_MODEL)
        dz_conv, dcw = _conv_bwd(dyc, sv["z"], sv["conv_w8"], name=f"conv_bwd_{l}", tm=tms)
        grads["conv_w"][l] = dcw[0:CONV_WIDTH]
        if l == 0:
            dest[1] = dest_of(1, big_names)
            dest[0].update(dest_of(0, _FFN2 + _MIXER_EARLY))
            sent = {(1, n): dest[1][n] for n in big_names}
            sent.update({(0, n): dest[0][n] for n in _FFN2 + _MIXER_EARLY})
            dq, dk, dv, *got = _attn_bwd(
                sv["q"], sv["k"], sv["v"], sv["o"], dym, sv["lse"], name=f"attn_bwd_{l}", blk=blk,
                exchange=scatter_of(sent))
            for (layer, n), r in zip(sent, got):
                received[layer][n] = r
        else:
            dq, dk, dv = _attn_bwd(sv["q"], sv["k"], sv["v"], sv["o"], dym, sv["lse"], name=f"attn_bwd_{l}", blk=blk)
        dz_mid, dqb, dkb, dvb, dgq, dgkv = _qkv_bwd(dq, dk, dv, sv["z"], sv["gq"], sv["gkv"], lw["wq"], lw["wk"],
                                                    lw["wv"], tabs, name=f"qkv_bwd_{l}", tm=tm)
        grads["q_norm_g"][l], grads["kv_norm_g"][l] = dgq[0], dgkv[0]
        d_wq = _mm_tn(sv["qn"], dqb, name=f"dw_uq_{l}", tm=Q_LORA, tn=D_QK, tk=tk)
        d_wk = _mm_tn(sv["kin"], dkb, name=f"dw_uk_{l}", tm=Q_LORA, tn=D_QK, tk=tk)
        d_wv = _mm_tn(sv["kin"], dvb, name=f"dw_uv_{l}", tm=Q_LORA, tn=D_MLA, tk=tk)
        grads["w_uq"][l] = d_wq.reshape(Q_LORA, MLA_HEADS, HEAD_PAD)[:, :, :QK_NOPE + QK_ROPE].reshape(Q_LORA, -1)
        d_kn = d_wk[:KV_LORA].reshape(KV_LORA, MLA_HEADS, HEAD_PAD)[:, :, :QK_NOPE]
        d_vv = d_wv[:KV_LORA].reshape(KV_LORA, MLA_HEADS, V_HEAD)
        grads["w_ukv"][l] = jnp.concatenate([d_kn, d_vv], axis=-1).reshape(KV_LORA, -1)
        dz = jnp.concatenate([dz_conv, dz_mid, dgate], axis=1)
        d_win = _mm_tn(sv["h1b"], dz, name=f"dw_in_{l}", tm=D_MODEL, tn=1024, tk=tk)
        grads["mix_w_in"][l] = jnp.concatenate([d_win[:, :Z_KR_END], d_win[:, Z_KR_END + D_IN_PAD - D_IN_REAL:]], axis=1)
        dh = _mm(dz, lw["w_in"], trans_b=True, out_dtype=F32, name=f"mix_dx_{l}", res=dr2, res_scale=ALPHA,
                 tm=tms, tn=D_MODEL, tk=2048)
        carry = None
        if l == 0:
            carry = dest_of(0, [n for n in _MIXER if n not in _MIXER_EARLY])
            dest[0].update(carry)
        dh, grads["ln_g"][l][0], grads["ln_b"][l][0] = ffn_bwd(
            dh, sv["f1"], sv["h_in_b"], sv["up1"], sv["down1"], S["ln_g"][l, 0:1], f"{l}a", _FFN1, l, carry=carry)

    small = {n: jnp.stack(grads[n]) for n in ("mix_b_gate", "conv_w", "q_norm_g", "kv_norm_g")}
    small["ln_g"] = jnp.stack([jnp.stack(g) for g in grads["ln_g"]])
    small["ln_b"] = jnp.stack([jnp.stack(g) for g in grads["ln_b"]])
    small["meta_tokens"] = dh[:N_META]
    return loss8, dh[N_META:t_real], dest, received, small


def kernel(x, meta_tokens, ffn1_w_up, ffn1_w_down, mix_w_in, mix_b_gate, conv_w, q_norm_g, w_uq, kv_norm_g, w_ukv, w_br_conv, w_br_mla, w_o, ffn2_w_up, ffn2_w_down, ln_g, ln_b, loss_target, m_meta_tokens, m_ffn1_w_up, m_ffn1_w_down, m_mix_w_in, m_mix_b_gate, m_conv_w, m_q_norm_g, m_w_uq, m_kv_norm_g, m_w_ukv, m_w_br_conv, m_w_br_mla, m_w_o, m_ffn2_w_up, m_ffn2_w_down, m_ln_g, m_ln_b, v_meta_tokens, v_ffn1_w_up, v_ffn1_w_down, v_mix_w_in, v_mix_b_gate, v_conv_w, v_q_norm_g, v_w_uq, v_kv_norm_g, v_w_ukv, v_w_br_conv, v_w_br_mla, v_w_o, v_ffn2_w_up, v_ffn2_w_down, v_ln_g, v_ln_b):
    names = ["meta_tokens", "ffn1_w_up", "ffn1_w_down", "mix_w_in", "mix_b_gate", "conv_w", "q_norm_g", "w_uq",
             "kv_norm_g", "w_ukv", "w_br_conv", "w_br_mla", "w_o", "ffn2_w_up", "ffn2_w_down", "ln_g", "ln_b"]
    w = dict(zip(names, (meta_tokens, ffn1_w_up, ffn1_w_down, mix_w_in, mix_b_gate, conv_w, q_norm_g, w_uq,
                         kv_norm_g, w_ukv, w_br_conv, w_br_mla, w_o, ffn2_w_up, ffn2_w_down, ln_g, ln_b)))
    m = dict(zip(names, (m_meta_tokens, m_ffn1_w_up, m_ffn1_w_down, m_mix_w_in, m_mix_b_gate, m_conv_w, m_q_norm_g,
                         m_w_uq, m_kv_norm_g, m_w_ukv, m_w_br_conv, m_w_br_mla, m_w_o, m_ffn2_w_up, m_ffn2_w_down,
                         m_ln_g, m_ln_b)))
    v = dict(zip(names, (v_meta_tokens, v_ffn1_w_up, v_ffn1_w_down, v_mix_w_in, v_mix_b_gate, v_conv_w, v_q_norm_g,
                         v_w_uq, v_kv_norm_g, v_w_ukv, v_w_br_conv, v_w_br_mla, v_w_o, v_ffn2_w_up, v_ffn2_w_down,
                         v_ln_g, v_ln_b)))
    ix, iy, ic = lax.axis_index("x"), lax.axis_index("y"), lax.axis_index("c")
    dev = 4 * ix + 2 * iy + ic

    big_names = [n for n, _ in _BIG]
    small_names = [n for n, _ in _SMALL_SHARDED]
    small_axes = [a for _, a in _SMALL_SHARDED]
    small_shapes = [w[n].shape for n in small_names]
    gathered = _all_gather([w[n][0].astype(BF16) for n in _FFN1] + [_pack([w[n] for n in small_names], F32)],
                           name="all_gather_weights")
    gathered_ffn1 = dict(zip(_FFN1, gathered[:-1]))
    w_blocks = [{n: w[n][l].astype(BF16) for n in big_names} for l in range(DEPTH)]
    S = dict(zip(small_names, _unpack_gathered(gathered[-1], small_shapes, small_axes)))
    S["q_norm_g"], S["kv_norm_g"] = q_norm_g, kv_norm_g

    loss8, grad_x, dest, received, G = _local_step(x[0], loss_target[0], gathered_ffn1, w_blocks, S)

    my_dev = dev.reshape(1).astype(jnp.int32)
    big_res = [{}, {}, {}, {}]
    for n in big_names:
        res = [_adamw(w[n], m[n], v[n], l, [(dest[l][n], None)] + [(received[l][n], k) for k in range(N_DEV - 1)],
                      my_dev, name=f"adamw{l}_{n}") for l in range(DEPTH)]
        for kind in range(4):
            big_res[kind][n] = jnp.stack([r[kind] for r in res])

    small_all = small_names + list(_SMALL_REPL)
    part = _pack([G[n] for n in small_all] + [loss8[0, 0:1]], F32)
    full_shapes = [G[n].shape for n in small_all] + [(1,)]
    summed = _sum8(_all_gather([part], name="all_gather_small_grads")[0], name="sum_small_grads")
    unpacked = _unpack(summed, full_shapes)
    loss = unpacked[-1][0]
    g_full = dict(zip(small_all, unpacked[:-1]))
    g_loc = []
    for n in small_all:
        if n in _SMALL_REPL:
            g_loc.append(g_full[n])
        else:
            ax = dict(_SMALL_SHARDED)[n]
            g_loc.append(lax.dynamic_slice_in_dim(g_full[n], dev * w[n].shape[ax], w[n].shape[ax], axis=ax))
    loc_shapes = [w[n].shape for n in small_all]
    g_pack = _pack(g_loc, F32)
    small_out = _adamw(_pack([w[n] for n in small_all], F32)[None], _pack([m[n] for n in small_all], F32)[None],
                       _pack([v[n] for n in small_all], F32)[None], 0, [(g_pack[None], 0)],
                       jnp.zeros((1,), jnp.int32), name="adamw_small")
    small_res = [dict(zip(small_all, _unpack(o, loc_shapes))) for o in small_out]

    outs = [loss, grad_x[None]]
    for kind in range(4):
        for n in names:
            outs.append(big_res[kind][n] if n in big_res[kind] else small_res[kind][n])
    return tuple(outs)
```

```python
import functools

import numpy as np
import jax
import jax.numpy as jnp
from jax import lax
from jax.experimental import pallas as pl
from jax.experimental.pallas import tpu as pltpu

F32 = jnp.float32
BF16 = jnp.bfloat16

D_MODEL = 1024
DEPTH = 2
N_META = 16
D_CONV = 512
CONV_WIDTH = 3
MLA_HEADS = 8
QK_NOPE = 64
QK_ROPE = 32
V_HEAD = 64
Q_LORA = 256
KV_LORA = 128
D_MLA = MLA_HEADS * V_HEAD
ROPE_BASE = 10000.0
NEG_INF = -1e30
D_FF = 2816
ALPHA = (2 * DEPTH) ** 0.25
LN_EPS = 1e-5
RMS_EPS = 1e-6
ATTN_SCALE = (QK_NOPE + QK_ROPE) ** -0.5
ADAM_LR = 0.001
ADAM_B1 = 0.9
ADAM_B2 = 0.999
ADAM_EPS = 1e-08
ADAM_WD = 0.01
ADAM_STEP = 10

N_DEV = 8
HEAD_PAD = 128
HEADS_PER_STEP = 2
FWD_HEADS_PER_STEP = 4
FF_BLK = 2 * D_FF // N_DEV
FF_HALF_BLOCKS = N_DEV // 2
D_QK = MLA_HEADS * HEAD_PAD
Z_CONV = 0
Z_MID = 1536
Z_GATE = 2048
D_IN_PAD = 4096
D_IN_REAL = 4000
Z_KR_END = Z_MID + Q_LORA + KV_LORA + QK_ROPE

V7X_VMEM_LIMIT = 56 * 1024 * 1024
LANE = 128
ROW_ALIGN = 256


def _tile(n, cands):
    for c in cands:
        if n % c == 0:
            return c
    raise ValueError(f"no tile for {n} in {cands}")


def _params(sem):
    return pltpu.CompilerParams(dimension_semantics=sem, vmem_limit_bytes=V7X_VMEM_LIMIT)


def _mm_call(a, b, a_spec, b_spec, *, out_shape, out_spec, acc_shape, grid, name, trans_b=False,
             res=None, res_spec=None, res_scale=1.0, pieces=1, exchange=None):
    nk = grid[2]
    has_res = res is not None
    out_dtype = out_shape.dtype
    dims = (((1,), (1,)), ((), ())) if trans_b else (((1,), (0,)), ((), ()))

    def body(*refs):
        if has_res:
            a_ref, b_ref, r_ref, o_ref, acc = refs
        else:
            a_ref, b_ref, o_ref, acc = refs
        k = pl.program_id(2)
        if pieces == 1:
            part = lax.dot_general(a_ref[...], b_ref[...], dims, preferred_element_type=F32)
        else:
            part = lax.dot_general(a_ref[0], b_ref[0], dims, preferred_element_type=F32)
            for p in range(1, pieces):
                part = part + lax.dot_general(a_ref[p], b_ref[p], dims, preferred_element_type=F32)

        @pl.when(k == 0)
        def _():
            acc[...] = part

        @pl.when(k > 0)
        def _():
            acc[...] += part

        @pl.when(k == nk - 1)
        def _():
            out = acc[...]
            if has_res:
                out = out + res_scale * r_ref[...]
            o_ref[...] = out.astype(out_dtype)

    in_specs = [a_spec, b_spec]
    args = [a, b]
    if has_res:
        in_specs.append(res_spec)
        args.append(res)
    if exchange is None:
        return pl.pallas_call(
            body, name=name, grid=grid, in_specs=in_specs, out_specs=out_spec, out_shape=out_shape,
            scratch_shapes=[pltpu.VMEM(acc_shape, F32)],
            compiler_params=_params(("parallel", "parallel", "arbitrary")),
        )(*args)
    ex_args, ex_specs, ex_out, ex_scratch = _exchange_operands(exchange)
    return pl.pallas_call(
        _carry_exchange(body, exchange, 0, len(args), 1, grid), name=name, grid=grid,
        in_specs=in_specs + ex_specs, out_specs=[out_spec] + ex_specs, out_shape=[out_shape] + ex_out,
        scratch_shapes=[pltpu.VMEM(acc_shape, F32)] + ex_scratch,
        compiler_params=_params(("arbitrary", "arbitrary", "arbitrary")),
    )(*args, *ex_args)


def _mm(a, b, *, out_dtype, name, trans_b=False, res=None, res_scale=1.0, tm, tn, tk, exchange=None):
    M, K = a.shape
    N = b.shape[0] if trans_b else b.shape[1]
    assert M % tm == 0 and N % tn == 0 and K % tk == 0
    b_spec = (pl.BlockSpec((tn, tk), lambda i, j, k: (j, k)) if trans_b
              else pl.BlockSpec((tk, tn), lambda i, j, k: (k, j)))
    tile = pl.BlockSpec((tm, tn), lambda i, j, k: (i, j))
    return _mm_call(a, b, pl.BlockSpec((tm, tk), lambda i, j, k: (i, k)), b_spec,
                    out_shape=jax.ShapeDtypeStruct((M, N), out_dtype), out_spec=tile, acc_shape=(tm, tn),
                    grid=(M // tm, N // tn, K // tk), name=name, trans_b=trans_b,
                    res=res, res_spec=tile, res_scale=res_scale, exchange=exchange)


def _mm_tn_call(a, b, a_spec, b_spec, *, out_shape, out_spec, grid, name, exchange=None):
    def body(a_ref, b_ref, o_ref):
        k = pl.program_id(2)
        part = lax.dot_general(a_ref[...], b_ref[...], (((0,), (0,)), ((), ())),
                               preferred_element_type=F32)

        @pl.when(k == 0)
        def _():
            o_ref[...] = part

        @pl.when(k > 0)
        def _():
            o_ref[...] += part

    if exchange is None:
        return pl.pallas_call(
            body, name=name, grid=grid, in_specs=[a_spec, b_spec], out_specs=out_spec, out_shape=out_shape,
            compiler_params=_params(("parallel", "parallel", "arbitrary")),
        )(a, b)
    ex_args, ex_specs, ex_out, ex_scratch = _exchange_operands(exchange)
    return pl.pallas_call(
        _carry_exchange(body, exchange, 0, 2, 1, grid), name=name, grid=grid,
        in_specs=[a_spec, b_spec] + ex_specs, out_specs=[out_spec] + ex_specs, out_shape=[out_shape] + ex_out,
        scratch_shapes=ex_scratch,
        compiler_params=_params(("arbitrary", "arbitrary", "arbitrary")),
    )(a, b, *ex_args)


def _mm_tn(a, b, *, name, tm, tn, tk):
    T, M = a.shape
    N = b.shape[1]
    assert M % tm == 0 and N % tn == 0 and T % tk == 0
    return _mm_tn_call(a, b, pl.BlockSpec((tk, tm), lambda i, j, k: (k, i)),
                       pl.BlockSpec((tk, tn), lambda i, j, k: (k, j)),
                       out_shape=jax.ShapeDtypeStruct((M, N), F32),
                       out_spec=pl.BlockSpec((tm, tn), lambda i, j, k: (i, j)),
                       grid=(M // tm, N // tn, T // tk), name=name)


def _ffn_up(hb, w_up8, *, name, tm, exchange=None):
    T = hb.shape[0]

    def body(h_ref, wg_ref, wu_ref, gu_ref, a_ref):
        h = h_ref[...]
        g = jnp.dot(h, wg_ref[...], preferred_element_type=F32)
        u = jnp.dot(h, wu_ref[...], preferred_element_type=F32)
        sg = jax.nn.sigmoid(g)
        silu = g * sg
        gu_ref[0] = (u * (sg * (1.0 + g * (1.0 - sg)))).astype(BF16)
        gu_ref[1] = silu.astype(BF16)
        a_ref[...] = (silu * u).astype(BF16)

    grid = (FF_HALF_BLOCKS, T // tm)
    ex_args, ex_specs, ex_out, ex_scratch = _exchange_operands(exchange)
    return pl.pallas_call(
        _carry_exchange(body, exchange, 0, 3, 2, grid), name=name, grid=grid,
        in_specs=[pl.BlockSpec((tm, D_MODEL), lambda j, i: (i, 0)),
                  pl.BlockSpec((None, D_MODEL, FF_BLK), lambda j, i: (j, 0, 0)),
                  pl.BlockSpec((None, D_MODEL, FF_BLK), lambda j, i: (j + FF_HALF_BLOCKS, 0, 0))] + ex_specs,
        out_specs=[pl.BlockSpec((2, None, tm, FF_BLK), lambda j, i: (0, j, i, 0)),
                   pl.BlockSpec((None, tm, FF_BLK), lambda j, i: (j, i, 0))] + ex_specs,
        out_shape=[jax.ShapeDtypeStruct((2, FF_HALF_BLOCKS, T, FF_BLK), BF16),
                   jax.ShapeDtypeStruct((FF_HALF_BLOCKS, T, FF_BLK), BF16)] + ex_out,
        scratch_shapes=ex_scratch,
        compiler_params=_params(("arbitrary", "arbitrary") if exchange else ("parallel", "parallel")),
    )(hb, w_up8, w_up8, *ex_args)


def _mm_res_ln(a, w, res, g, b, *, scale, name, tm, exchange=None):
    split = a.ndim == 3
    if split:
        S, T, Ks = a.shape
        K = S * Ks
    else:
        T, K = a.shape

    def body(a_ref, w_ref, res_ref, g_ref, b_ref, r_ref, y_ref, yb_ref):
        if split:
            f = jnp.dot(a_ref[0], w_ref[0:Ks, :], preferred_element_type=F32)
            for s in range(1, S):
                f = f + jnp.dot(a_ref[s], w_ref[s * Ks:(s + 1) * Ks, :], preferred_element_type=F32)
        else:
            f = jnp.dot(a_ref[...], w_ref[...], preferred_element_type=F32)
        r = ALPHA * res_ref[...] + scale * f
        mu = jnp.mean(r, axis=-1, keepdims=True)
        xc = r - mu
        var = jnp.mean(xc * xc, axis=-1, keepdims=True)
        y = xc * lax.rsqrt(var + LN_EPS) * g_ref[...] + b_ref[...]
        r_ref[...] = r
        y_ref[...] = y
        yb_ref[...] = y.astype(BF16)

    row = pl.BlockSpec((tm, D_MODEL), lambda i: (i, 0))
    vec = pl.BlockSpec((1, D_MODEL), lambda i: (0, 0))
    grid = (T // tm,)
    ex_args, ex_specs, ex_out, ex_scratch = _exchange_operands(exchange)
    return pl.pallas_call(
        _carry_exchange(body, exchange, 0, 5, 3, grid), name=name, grid=grid,
        in_specs=[pl.BlockSpec((S, tm, Ks), lambda i: (0, i, 0)) if split else pl.BlockSpec((tm, K), lambda i: (i, 0)),
                  pl.BlockSpec((K, D_MODEL), lambda i: (0, 0)), row, vec, vec] + ex_specs,
        out_specs=[row, row, row] + ex_specs,
        out_shape=[jax.ShapeDtypeStruct((T, D_MODEL), F32), jax.ShapeDtypeStruct((T, D_MODEL), F32),
                   jax.ShapeDtypeStruct((T, D_MODEL), BF16)] + ex_out,
        scratch_shapes=ex_scratch,
        compiler_params=_params(("arbitrary",) if exchange else ("parallel",)),
    )(a, w, res, g, b, *ex_args)


def _conv_fwd(z, conv_w8, *, name, tm):
    T = z.shape[0]
    hb = tm // 8

    def body(b_ref, c_ref, h_ref, cp_ref, hp_ref, w_ref, y_ref):
        i = pl.program_id(0)
        u = c_ref[...] * h_ref[...]
        up = jnp.where(i > 0, cp_ref[...] * hp_ref[...], 0.0)
        ue = jnp.concatenate([up, u], axis=0)
        s1 = pltpu.roll(ue, 1, 0)[8:]
        s2 = pltpu.roll(ue, 2, 0)[8:]
        w = w_ref[...]
        conv = w[0:1] * s2 + w[1:2] * s1 + w[2:3] * u
        y_ref[...] = (b_ref[...] * conv).astype(BF16)

    def col(c):
        return pl.BlockSpec((tm, D_CONV), lambda i: (i, c))

    def prev(c):
        return pl.BlockSpec((8, D_CONV), lambda i: (jnp.maximum(i * hb - 1, 0), c))

    return pl.pallas_call(
        body, name=name, grid=(T // tm,),
        in_specs=[col(0), col(1), col(2), prev(1), prev(2), pl.BlockSpec((8, D_CONV), lambda i: (0, 0))],
        out_specs=pl.BlockSpec((tm, D_CONV), lambda i: (i, 0)),
        out_shape=jax.ShapeDtypeStruct((T, D_CONV), BF16),
        compiler_params=_params(("parallel",)),
    )(z, z, z, z, z, conv_w8)


def _rope(x, c, s1, s2):
    n = x.shape[-1]
    return x * c + pltpu.roll(x, 16, 1) * s1 + pltpu.roll(x, n - 16, 1) * s2


def _rope_t(d, c, s1, s2):
    n = d.shape[-1]
    return d * c + pltpu.roll(d * s1, n - 16, 1) + pltpu.roll(d * s2, 16, 1)


def _rms(x, g):
    rstd = lax.rsqrt(jnp.mean(x * x, axis=-1, keepdims=True) + RMS_EPS)
    return x * rstd * g


def _qkv_proj(z, gq, gkv, wq, wk, wv_ext, tabs, *, name, tm):
    T = z.shape[0]

    def body(z_ref, gq_ref, gkv_ref, wq_ref, wk_ref, wv_ref, c_ref, s1_ref, s2_ref,
             q_ref, k_ref, v_ref, qn_ref, kin_ref):
        zz = z_ref[...]
        qn = _rms(zz[:, :Q_LORA], gq_ref[...]).astype(BF16)
        kvn = _rms(zz[:, Q_LORA:Q_LORA + KV_LORA], gkv_ref[...]).astype(BF16)
        kin = jnp.concatenate([kvn, zz[:, Q_LORA + KV_LORA:].astype(BF16)], axis=-1)
        c = jnp.tile(c_ref[...], (1, MLA_HEADS))
        s1 = jnp.tile(s1_ref[...], (1, MLA_HEADS))
        s2 = jnp.tile(s2_ref[...], (1, MLA_HEADS))
        qpre = jnp.dot(qn, wq_ref[...], preferred_element_type=F32)
        kpre = jnp.dot(kin, wk_ref[...], preferred_element_type=F32)
        q_ref[...] = (_rope(qpre, c, s1, s2) * (ATTN_SCALE * LOG2_E)).astype(BF16)
        k_ref[...] = _rope(kpre, c, s1, s2).astype(BF16)
        vv = jnp.dot(kvn, wv_ref[...], preferred_element_type=F32)
        lane = lax.broadcasted_iota(jnp.int32, vv.shape, 1)
        v_ref[...] = jnp.where((lane & (HEAD_PAD - 1)) < V_HEAD, vv, 1.0).astype(BF16)
        qn_ref[...] = qn
        kin_ref[...] = kin

    def full(shape):
        return pl.BlockSpec(shape, lambda i: (0, 0))

    def rows(w, c=0):
        return pl.BlockSpec((tm, w), lambda i: (i, c))

    return pl.pallas_call(
        body, name=name, grid=(T // tm,),
        in_specs=[rows(512, Z_MID // 512), full((1, Q_LORA)), full((1, KV_LORA)),
                  full((Q_LORA, D_QK)), full((Q_LORA, D_QK)), full((KV_LORA, D_QK)),
                  rows(LANE), rows(LANE), rows(LANE)],
        out_specs=[rows(D_QK), rows(D_QK), rows(D_QK), rows(Q_LORA), rows(Q_LORA)],
        out_shape=[jax.ShapeDtypeStruct((T, D_QK), BF16), jax.ShapeDtypeStruct((T, D_QK), BF16),
                   jax.ShapeDtypeStruct((T, D_QK), BF16), jax.ShapeDtypeStruct((T, Q_LORA), BF16),
                   jax.ShapeDtypeStruct((T, Q_LORA), BF16)],
        compiler_params=_params(("parallel",)),
    )(z, gq, gkv, wq, wk, wv_ext, *tabs)


SOFTMAX_ROWS = 32
LOG2_E = 1.4426950408889634
_NT = (((1,), (1,)), ((), ()))
_TN = (((0,), (0,)), ((), ()))


def _diag_mask(s, row0, col0=0):
    row = lax.broadcasted_iota(jnp.int32, s.shape, 0) + row0
    col = lax.broadcasted_iota(jnp.int32, s.shape, 1) + col0
    return jnp.where(col <= row, s, NEG_INF)


_RELATIONS = tuple((rx, ry, rc) for rx in (0, 1) for ry in (0, 1) for rc in (0, 1))[1:]


class _Exchange:
    def __init__(self, kind, arrays):
        assert kind in ("gather", "scatter")
        self.kind, self.arrays, self.n = kind, list(arrays), len(arrays)

    def out_shapes(self):
        if self.kind == "gather":
            return [jax.ShapeDtypeStruct((N_DEV,) + a.shape, a.dtype) for a in self.arrays]
        return [jax.ShapeDtypeStruct((N_DEV - 1,) + a.shape[1:], a.dtype) for a in self.arrays]

    def scratch_shapes(self):
        sems = [pltpu.SemaphoreType.DMA((7 * self.n,)), pltpu.SemaphoreType.DMA((7 * self.n,))]
        if self.kind == "gather":
            sems.append(pltpu.SemaphoreType.DMA((self.n,)))
        return sems

    def _copies(self, src_refs, out_refs, sems):
        x, y, c = lax.axis_index("x"), lax.axis_index("y"), lax.axis_index("c")
        me = 4 * x + 2 * y + c
        sends, recvs, local = [], [], []
        for a in range(self.n):
            for k, rel in enumerate(_RELATIONS):
                peer = tuple((1 - p) if r else p for p, r in zip((x, y, c), rel))
                peer_index = 4 * peer[0] + 2 * peer[1] + peer[2]
                if self.kind == "gather":
                    src, lands_there, lands_here = src_refs[a], out_refs[a].at[me], out_refs[a].at[peer_index]
                else:
                    src, lands_there, lands_here = src_refs[a].at[peer_index], out_refs[a].at[k], out_refs[a].at[k]
                for dst, group in ((lands_there, sends), (lands_here, recvs)):
                    group.append(pltpu.make_async_remote_copy(
                        src_ref=src, dst_ref=dst, send_sem=sems[0].at[7 * a + k], recv_sem=sems[1].at[7 * a + k],
                        device_id=peer, device_id_type=_MESH_ID))
            if self.kind == "gather":
                local.append(pltpu.make_async_copy(src_refs[a], out_refs[a].at[me], sems[2].at[a]))
        return sends, recvs, local

    def start(self, src_refs, out_refs, sems):
        sends, _, local = self._copies(src_refs, out_refs, sems)
        for cp in local + sends:
            cp.start()

    def wait(self, src_refs, out_refs, sems):
        sends, recvs, local = self._copies(src_refs, out_refs, sems)
        for cp in recvs:
            cp.wait_recv()
        for cp in sends:
            cp.wait_send()
        for cp in local:
            cp.wait()


def _exchange_operands(exchange):
    if exchange is None:
        return [], [], [], []
    return exchange.arrays, [_ANY] * exchange.n, exchange.out_shapes(), exchange.scratch_shapes()


def _carry_exchange(body, exchange, n_prefetch, n_in, n_out, grid):
    if exchange is None:
        return body
    n = exchange.n
    last = tuple(g - 1 for g in grid)

    def wrapped(*refs):
        head = refs[:n_prefetch + n_in]
        src_refs = refs[n_prefetch + n_in:n_prefetch + n_in + n]
        rest = refs[n_prefetch + n_in + n:]
        outs, out_refs, rest = rest[:n_out], rest[n_out:n_out + n], rest[n_out + n:]
        n_sems = len(exchange.scratch_shapes())
        scratch, sems = rest[:len(rest) - n_sems], rest[len(rest) - n_sems:]
        at_first = functools.reduce(jnp.logical_and, [pl.program_id(d) == 0 for d in range(len(grid))])
        at_last = functools.reduce(jnp.logical_and, [pl.program_id(d) == last[d] for d in range(len(grid))])

        @pl.when(at_first)
        def _():
            exchange.start(src_refs, out_refs, sems)

        body(*head, *outs, *scratch)

        @pl.when(at_last)
        def _():
            exchange.wait(src_refs, out_refs, sems)

    return wrapped


def _attn_fwd(q, k, v, *, name, blk, exchange=None):
    T = q.shape[0]
    n = T // blk
    hp = FWD_HEADS_PER_STEP
    qi = np.array([i for i in range(n) for j in range(i + 1)], np.int32)
    kj = np.array([j for i in range(n) for j in range(i + 1)], np.int32)

    rc = _tile(blk, (SOFTMAX_ROWS,))

    def body(qi_ref, kj_ref, q_ref, k_ref, v_ref, o_ref, lse_ref, m_sc, acc_sc, s_sc, p_sc, red_sc):
        s_id = pl.program_id(1)
        i = qi_ref[s_id]
        j = kj_ref[s_id]

        @pl.when(j == 0)
        def _():
            m_sc[...] = jnp.full(m_sc.shape, NEG_INF, F32)
            acc_sc[...] = jnp.zeros(acc_sc.shape, F32)

        def head_step(hh, diagonal):
            hs = slice(hh * HEAD_PAD, (hh + 1) * HEAD_PAD)
            s_sc[hh] = lax.dot_general(q_ref[:, hs], k_ref[:, hs], _NT, preferred_element_type=F32)
            lanes = [slice(t * LANE, (t + 1) * LANE) for t in range(blk // LANE)]
            for r in range(blk // rc):
                rows = slice(r * rc, (r + 1) * rc)
                s = s_sc[hh, rows, :]
                if diagonal:
                    s = _diag_mask(s, r * rc)
                    s_sc[hh, rows, :] = s
                pm = s[:, lanes[0]]
                for t in lanes[1:]:
                    pm = jnp.maximum(pm, s[:, t])
                red_sc[hh, rows, :] = pm
            m_old = m_sc[hh]
            row_max = jnp.max(red_sc[hh], axis=-1, keepdims=True)
            m_new = jnp.maximum(m_old, jnp.broadcast_to(row_max, (blk, LANE)))
            a = jnp.exp2(m_old - m_new)
            m_sc[hh] = m_new
            for r in range(blk // rc):
                rows = slice(r * rc, (r + 1) * rc)
                mb = m_sc[hh, rows, :]
                for t in lanes:
                    p_sc[hh, rows, t] = jnp.exp2(s_sc[hh, rows, t] - mb).astype(BF16)
            acc_sc[hh] = a * acc_sc[hh] + jnp.dot(p_sc[hh], v_ref[:, hs], preferred_element_type=F32)

        @pl.when(j < i)
        def _():
            for hh in range(hp):
                head_step(hh, False)

        @pl.when(j == i)
        def _():
            for hh in range(hp):
                head_step(hh, True)
            for hh in range(hp):
                acc = acc_sc[hh]
                swapped = pltpu.roll(acc, V_HEAD, 1)
                o_ref[:, hh * V_HEAD:(hh + 1) * V_HEAD] = (acc / swapped)[:, :V_HEAD].astype(BF16)
                lane = lax.broadcasted_iota(jnp.int32, acc.shape, 1)
                denom = jnp.where(lane < V_HEAD, swapped, acc)
                lse_ref[hh] = m_sc[hh] + jnp.log(denom) * LOG2_E

    grid = (MLA_HEADS // hp, len(qi))
    ex_args, ex_specs, ex_out, ex_scratch = _exchange_operands(exchange)
    grid_spec = pltpu.PrefetchScalarGridSpec(
        num_scalar_prefetch=2, grid=grid,
        in_specs=[pl.BlockSpec((blk, hp * HEAD_PAD), lambda g, s, qi, kj: (qi[s], g)),
                  pl.BlockSpec((blk, hp * HEAD_PAD), lambda g, s, qi, kj: (kj[s], g)),
                  pl.BlockSpec((blk, hp * HEAD_PAD), lambda g, s, qi, kj: (kj[s], g))] + ex_specs,
        out_specs=[pl.BlockSpec((blk, hp * V_HEAD), lambda g, s, qi, kj: (qi[s], g)),
                   pl.BlockSpec((hp, blk, LANE), lambda g, s, qi, kj: (g, qi[s], 0))] + ex_specs,
        scratch_shapes=[pltpu.VMEM((hp, blk, LANE), F32), pltpu.VMEM((hp, blk, HEAD_PAD), F32),
                        pltpu.VMEM((hp, blk, blk), F32), pltpu.VMEM((hp, blk, blk), BF16),
                        pltpu.VMEM((hp, blk, LANE), F32)] + ex_scratch)
    return pl.pallas_call(
        _carry_exchange(body, exchange, 2, 3, 2, grid), name=name, grid_spec=grid_spec,
        out_shape=[jax.ShapeDtypeStruct((T, D_MLA), BF16),
                   jax.ShapeDtypeStruct((MLA_HEADS, T, LANE), F32)] + ex_out,
        compiler_params=_params(("arbitrary", "arbitrary") if exchange else ("parallel", "arbitrary")),
    )(jnp.asarray(qi), jnp.asarray(kj), q, k, v, *ex_args)


def _merge(yc, ym, wbc, wbm, z, bg, *, name, tm):
    T = yc.shape[0]

    def body(yc_ref, ym_ref, wbc_ref, wbm_ref, gc_ref, gm_ref, bg_ref, mg_ref, pa_ref, pb_ref):
        pa = jnp.dot(yc_ref[...], wbc_ref[...], preferred_element_type=F32)
        pb = jnp.dot(ym_ref[...], wbm_ref[...], preferred_element_type=F32)
        bgv = bg_ref[...]
        sa = jax.nn.sigmoid(gc_ref[...] + bgv[0:1])
        sb = jax.nn.sigmoid(gm_ref[...] + bgv[1:2])
        mg_ref[...] = (sa * pa + sb * pb).astype(BF16)
        pa_ref[...] = pa.astype(BF16)
        pb_ref[...] = pb.astype(BF16)

    row = pl.BlockSpec((tm, D_MODEL), lambda i: (i, 0))
    return pl.pallas_call(
        body, name=name, grid=(T // tm,),
        in_specs=[pl.BlockSpec((tm, D_CONV), lambda i: (i, 0)), pl.BlockSpec((tm, D_MLA), lambda i: (i, 0)),
                  pl.BlockSpec((D_CONV, D_MODEL), lambda i: (0, 0)), pl.BlockSpec((D_MLA, D_MODEL), lambda i: (0, 0)),
                  pl.BlockSpec((tm, D_MODEL), lambda i: (i, Z_GATE // D_MODEL)),
                  pl.BlockSpec((tm, D_MODEL), lambda i: (i, Z_GATE // D_MODEL + 1)),
                  pl.BlockSpec((8, D_MODEL), lambda i: (0, 0))],
        out_specs=[row, row, row],
        out_shape=[jax.ShapeDtypeStruct((T, D_MODEL), BF16)] * 3,
        compiler_params=_params(("parallel",)),
    )(yc, ym, wbc, wbm, z, z, bg)


def _loss_head(h, tgt, *, t_real, name, tm):
    T = h.shape[0]

    def body(h_ref, t_ref, dy_ref, loss_ref):
        i = pl.program_id(0)
        row = lax.broadcasted_iota(jnp.int32, (tm, 1), 0) + i * tm
        valid = (row >= N_META) & (row < t_real)
        err = jnp.where(valid, h_ref[...] - t_ref[...], 0.0)
        dy_ref[...] = err * (1.0 / D_MODEL)
        part = 0.5 * jnp.sum(jnp.sum(err * err, axis=-1, keepdims=True) * (1.0 / D_MODEL), axis=0, keepdims=True)

        @pl.when(i == 0)
        def _():
            loss_ref[...] = jnp.zeros(loss_ref.shape, F32)

        loss_ref[...] += jnp.broadcast_to(part, loss_ref.shape)

    row_spec = pl.BlockSpec((tm, D_MODEL), lambda i: (i, 0))
    return pl.pallas_call(
        body, name=name, grid=(T // tm,),
        in_specs=[row_spec, row_spec],
        out_specs=[row_spec, pl.BlockSpec((8, LANE), lambda i: (0, 0))],
        out_shape=[jax.ShapeDtypeStruct((T, D_MODEL), F32), jax.ShapeDtypeStruct((8, LANE), F32)],
        compiler_params=_params(("arbitrary",)),
    )(h, tgt)


def _ln_bwd(dy, r, g, *, scale, name, tm):
    T = dy.shape[0]

    def body(dy_ref, r_ref, g_ref, dr_ref, drb_ref, dg_ref, db_ref):
        i = pl.program_id(0)
        rr = r_ref[...]
        dyv = dy_ref[...]
        mu = jnp.mean(rr, axis=-1, keepdims=True)
        xc = rr - mu
        rstd = lax.rsqrt(jnp.mean(xc * xc, axis=-1, keepdims=True) + LN_EPS)
        xh = xc * rstd
        dxh = dyv * g_ref[...]
        m1 = jnp.mean(dxh, axis=-1, keepdims=True)
        m2 = jnp.mean(dxh * xh, axis=-1, keepdims=True)
        dr = rstd * (dxh - m1 - xh * m2)
        dr_ref[...] = dr
        drb_ref[...] = (scale * dr).astype(BF16)

        @pl.when(i == 0)
        def _():
            dg_ref[...] = jnp.zeros(dg_ref.shape, F32)
            db_ref[...] = jnp.zeros(db_ref.shape, F32)

        dg_ref[0:1, :] += jnp.sum(dyv * xh, axis=0, keepdims=True)
        db_ref[0:1, :] += jnp.sum(dyv, axis=0, keepdims=True)

    row = pl.BlockSpec((tm, D_MODEL), lambda i: (i, 0))
    acc = pl.BlockSpec((8, D_MODEL), lambda i: (0, 0))
    return pl.pallas_call(
        body, name=name, grid=(T // tm,),
        in_specs=[row, row, pl.BlockSpec((1, D_MODEL), lambda i: (0, 0))],
        out_specs=[row, row, acc, acc],
        out_shape=[jax.ShapeDtypeStruct((T, D_MODEL), F32), jax.ShapeDtypeStruct((T, D_MODEL), BF16),
                   jax.ShapeDtypeStruct((8, D_MODEL), F32), jax.ShapeDtypeStruct((8, D_MODEL), F32)],
        compiler_params=_params(("arbitrary",)),
    )(dy, r, g)


def _ffn_bwd_mid(dfb, w_down, gu, *, name, tm, exchange=None):
    T = dfb.shape[0]

    def body(df_ref, w_ref, gu_ref, o_ref):
        da = lax.dot_general(df_ref[...], w_ref[...], _NT, preferred_element_type=F32)
        o_ref[0] = (da * gu_ref[0].astype(F32)).astype(BF16)
        o_ref[1] = (da * gu_ref[1].astype(F32)).astype(BF16)

    grid = (FF_HALF_BLOCKS, T // tm)
    ex_args, ex_specs, ex_out, ex_scratch = _exchange_operands(exchange)
    return pl.pallas_call(
        _carry_exchange(body, exchange, 0, 3, 1, grid), name=name, grid=grid,
        in_specs=[pl.BlockSpec((tm, D_MODEL), lambda j, i: (i, 0)),
                  pl.BlockSpec((FF_BLK, D_MODEL), lambda j, i: (j, 0)),
                  pl.BlockSpec((2, None, tm, FF_BLK), lambda j, i: (0, j, i, 0))] + ex_specs,
        out_specs=[pl.BlockSpec((2, None, tm, FF_BLK), lambda j, i: (0, j, i, 0))] + ex_specs,
        out_shape=[jax.ShapeDtypeStruct((2, FF_HALF_BLOCKS, T, FF_BLK), BF16)] + ex_out,
        scratch_shapes=ex_scratch,
        compiler_params=_params(("arbitrary", "arbitrary") if exchange else ("parallel", "parallel")),
    )(dfb, w_down, gu, *ex_args)


def _wo_bwd(dmb, wo_t, z, bg, pa, pb, *, name, tm):
    T = dmb.shape[0]

    def body(dm_ref, w_ref, gc_ref, gm_ref, bg_ref, pa_ref, pb_ref, dpa_ref, dpb_ref, dg_ref, dbg_ref):
        i = pl.program_id(0)
        dm = lax.dot_general(dm_ref[...], w_ref[...], _NT, preferred_element_type=F32)
        bgv = bg_ref[...]
        sa = jax.nn.sigmoid(gc_ref[...] + bgv[0:1])
        sb = jax.nn.sigmoid(gm_ref[...] + bgv[1:2])
        dpa_ref[...] = (dm * sa).astype(BF16)
        dpb_ref[...] = (dm * sb).astype(BF16)
        dga = dm * pa_ref[...].astype(F32) * (sa * (1.0 - sa))
        dgb = dm * pb_ref[...].astype(F32) * (sb * (1.0 - sb))
        dg_ref[:, :D_MODEL] = dga.astype(BF16)
        dg_ref[:, D_MODEL:] = dgb.astype(BF16)

        @pl.when(i == 0)
        def _():
            dbg_ref[...] = jnp.zeros(dbg_ref.shape, F32)

        dbg_ref[0:1, :] += jnp.sum(dga, axis=0, keepdims=True)
        dbg_ref[1:2, :] += jnp.sum(dgb, axis=0, keepdims=True)

    row = pl.BlockSpec((tm, D_MODEL), lambda i: (i, 0))
    return pl.pallas_call(
        body, name=name, grid=(T // tm,),
        in_specs=[row, pl.BlockSpec((D_MODEL, D_MODEL), lambda i: (0, 0)),
                  pl.BlockSpec((tm, D_MODEL), lambda i: (i, Z_GATE // D_MODEL)),
                  pl.BlockSpec((tm, D_MODEL), lambda i: (i, Z_GATE // D_MODEL + 1)),
                  pl.BlockSpec((8, D_MODEL), lambda i: (0, 0)), row, row],
        out_specs=[row, row, pl.BlockSpec((tm, 2 * D_MODEL), lambda i: (i, 0)),
                   pl.BlockSpec((8, D_MODEL), lambda i: (0, 0))],
        out_shape=[jax.ShapeDtypeStruct((T, D_MODEL), BF16), jax.ShapeDtypeStruct((T, D_MODEL), BF16),
                   jax.ShapeDtypeStruct((T, 2 * D_MODEL), BF16), jax.ShapeDtypeStruct((8, D_MODEL), F32)],
        compiler_params=_params(("arbitrary",)),
    )(dmb, wo_t, z, z, bg, pa, pb)


def _conv_bwd(dy, z, conv_w8, *, name, tm):
    T = dy.shape[0]
    n = T // tm
    hb = tm // 8

    def body(dy_ref, b_ref, c_ref, h_ref, cp_ref, hp_ref, dyn_ref, bn_ref, w_ref, dz_ref, dw_ref):
        i = pl.program_id(0)
        u = c_ref[...] * h_ref[...]
        up = jnp.where(i > 0, cp_ref[...] * hp_ref[...], 0.0)
        ue = jnp.concatenate([up, u], axis=0)
        s1 = pltpu.roll(ue, 1, 0)[8:]
        s2 = pltpu.roll(ue, 2, 0)[8:]
        w = w_ref[...]
        conv = w[0:1] * s2 + w[1:2] * s1 + w[2:3] * u
        dyv = dy_ref[...]
        e = dyv * b_ref[...]
        en = jnp.where(i < n - 1, dyn_ref[...] * bn_ref[...], 0.0)
        ee = jnp.concatenate([e, en], axis=0)
        e1 = pltpu.roll(ee, tm + 8 - 1, 0)[:tm]
        e2 = pltpu.roll(ee, tm + 8 - 2, 0)[:tm]
        du = w[2:3] * e + w[1:2] * e1 + w[0:1] * e2
        dz_ref[:, 0:D_CONV] = (dyv * conv).astype(BF16)
        dz_ref[:, D_CONV:2 * D_CONV] = (du * h_ref[...]).astype(BF16)
        dz_ref[:, 2 * D_CONV:] = (du * c_ref[...]).astype(BF16)

        @pl.when(i == 0)
        def _():
            dw_ref[...] = jnp.zeros(dw_ref.shape, F32)

        dw_ref[0:1, :] += jnp.sum(e * s2, axis=0, keepdims=True)
        dw_ref[1:2, :] += jnp.sum(e * s1, axis=0, keepdims=True)
        dw_ref[2:3, :] += jnp.sum(e * u, axis=0, keepdims=True)

    def col(c):
        return pl.BlockSpec((tm, D_CONV), lambda i: (i, c))

    def prev(c):
        return pl.BlockSpec((8, D_CONV), lambda i: (jnp.maximum(i * hb - 1, 0), c))

    def nxt(c):
        return pl.BlockSpec((8, D_CONV), lambda i: (jnp.minimum((i + 1) * hb, T // 8 - 1), c))

    return pl.pallas_call(
        body, name=name, grid=(n,),
        in_specs=[col(0), col(0), col(1), col(2), prev(1), prev(2), nxt(0), nxt(0),
                  pl.BlockSpec((8, D_CONV), lambda i: (0, 0))],
        out_specs=[pl.BlockSpec((tm, 3 * D_CONV), lambda i: (i, 0)), pl.BlockSpec((8, D_CONV), lambda i: (0, 0))],
        out_shape=[jax.ShapeDtypeStruct((T, 3 * D_CONV), BF16), jax.ShapeDtypeStruct((8, D_CONV), F32)],
        compiler_params=_params(("arbitrary",)),
    )(dy, z, z, z, z, z, dy, z, conv_w8)


def _attn_bwd(q, k, v, o, do, lse, *, name, blk, exchange=None):
    T = q.shape[0]
    n = T // blk
    hp = HEADS_PER_STEP
    qi = np.array([i for j in range(n) for i in range(j, n)], np.int32)
    kj = np.array([j for j in range(n) for i in range(j, n)], np.int32)

    rc = _tile(blk, (SOFTMAX_ROWS,))

    def body(qi_ref, kj_ref, q_ref, k_ref, v_ref, o_ref, do_ref, lse_ref, dq_ref, dk_ref, dv_ref,
             dk_sc, dv_sc, s_sc, dp_sc, p_sc, ds_sc, delta_sc):
        s_id = pl.program_id(1)
        i = qi_ref[s_id]
        j = kj_ref[s_id]

        @pl.when(s_id == 0)
        def _():
            dq_ref[...] = jnp.zeros(dq_ref.shape, F32)

        @pl.when(i == j)
        def _():
            dk_sc[...] = jnp.zeros(dk_sc.shape, F32)
            dv_sc[...] = jnp.zeros(dv_sc.shape, F32)

        q_rows = pl.ds(pl.multiple_of(i * blk, blk), blk)

        def head_step(hh, diagonal):
            hs = slice(hh * HEAD_PAD, (hh + 1) * HEAD_PAD)
            vs = slice(hh * V_HEAD, (hh + 1) * V_HEAD)
            qh = q_ref[:, hs]
            kh = k_ref[:, hs]
            doh = do_ref[:, vs]
            s_sc[hh] = lax.dot_general(qh, kh, _NT, preferred_element_type=F32)
            vh = v_ref[:, hh * HEAD_PAD:hh * HEAD_PAD + V_HEAD]
            dp_sc[hh] = lax.dot_general(doh, vh, _NT, preferred_element_type=F32)
            delta = jnp.sum(doh.astype(F32) * o_ref[:, vs].astype(F32), axis=-1, keepdims=True)
            delta_sc[hh] = jnp.broadcast_to(delta, (blk, LANE))
            for r in range(blk // rc):
                rows = slice(r * rc, (r + 1) * rc)
                lse = lse_ref[hh, rows, :]
                dl = delta_sc[hh, rows, :]
                for t in range(blk // LANE):
                    cols = slice(t * LANE, (t + 1) * LANE)
                    s = s_sc[hh, rows, cols]
                    if diagonal:
                        s = _diag_mask(s, r * rc, t * LANE)
                    p = jnp.exp2(s - lse)
                    p_sc[hh, rows, cols] = p.astype(BF16)
                    ds_sc[hh, rows, cols] = (p * (dp_sc[hh, rows, cols] - dl)).astype(BF16)
            dv_sc[hh] += lax.dot_general(p_sc[hh], doh, _TN, preferred_element_type=F32)
            dk_sc[hh] += lax.dot_general(ds_sc[hh], qh, _TN, preferred_element_type=F32)
            dq_ref[q_rows, hs] += jnp.dot(ds_sc[hh], kh, preferred_element_type=F32)

        @pl.when(j < i)
        def _():
            for hh in range(hp):
                head_step(hh, False)

        @pl.when(j == i)
        def _():
            for hh in range(hp):
                head_step(hh, True)

        @pl.when(i == n - 1)
        def _():
            for hh in range(hp):
                dk_ref[:, hh * HEAD_PAD:(hh + 1) * HEAD_PAD] = dk_sc[hh] * (1.0 / LOG2_E)
                dv_ref[:, hh * V_HEAD:(hh + 1) * V_HEAD] = dv_sc[hh]

    wq = hp * HEAD_PAD
    wv = hp * V_HEAD
    grid = (MLA_HEADS // hp, len(qi))
    ex_args, ex_specs, ex_out, ex_scratch = _exchange_operands(exchange)
    grid_spec = pltpu.PrefetchScalarGridSpec(
        num_scalar_prefetch=2, grid=grid,
        in_specs=[pl.BlockSpec((blk, wq), lambda g, s, qi, kj: (qi[s], g)),
                  pl.BlockSpec((blk, wq), lambda g, s, qi, kj: (kj[s], g)),
                  pl.BlockSpec((blk, wq), lambda g, s, qi, kj: (kj[s], g)),
                  pl.BlockSpec((blk, wv), lambda g, s, qi, kj: (qi[s], g)),
                  pl.BlockSpec((blk, wv), lambda g, s, qi, kj: (qi[s], g)),
                  pl.BlockSpec((hp, blk, LANE), lambda g, s, qi, kj: (g, qi[s], 0))] + ex_specs,
        out_specs=[pl.BlockSpec((T, wq), lambda g, s, qi, kj: (0, g)),
                   pl.BlockSpec((blk, wq), lambda g, s, qi, kj: (kj[s], g)),
                   pl.BlockSpec((blk, wv), lambda g, s, qi, kj: (kj[s], g))] + ex_specs,
        scratch_shapes=[pltpu.VMEM((hp, blk, HEAD_PAD), F32), pltpu.VMEM((hp, blk, V_HEAD), F32),
                        pltpu.VMEM((hp, blk, blk), F32), pltpu.VMEM((hp, blk, blk), F32),
                        pltpu.VMEM((hp, blk, blk), BF16), pltpu.VMEM((hp, blk, blk), BF16),
                        pltpu.VMEM((hp, blk, LANE), F32)] + ex_scratch)
    return pl.pallas_call(
        _carry_exchange(body, exchange, 2, 6, 3, grid), name=name, grid_spec=grid_spec,
        out_shape=[jax.ShapeDtypeStruct((T, D_QK), F32), jax.ShapeDtypeStruct((T, D_QK), F32),
                   jax.ShapeDtypeStruct((T, D_MLA), F32)] + ex_out,
        compiler_params=_params(("arbitrary", "arbitrary") if exchange else ("parallel", "arbitrary")),
    )(jnp.asarray(qi), jnp.asarray(kj), q, k, v, o, do, lse, *ex_args)


def _qkv_bwd(dq, dk, dv, z, gq, gkv, wq_t, wk_t, wv_t, tabs, *, name, tm):
    T = dq.shape[0]

    def body(dq_ref, dk_ref, dv_ref, z_ref, gq_ref, gkv_ref, wq_ref, wk_ref, wv_ref, c_ref, s1_ref, s2_ref,
             dz_ref, dqb_ref, dkb_ref, dvb_ref, dgq_ref, dgkv_ref):
        i = pl.program_id(0)
        c = jnp.tile(c_ref[...], (1, MLA_HEADS))
        s1 = jnp.tile(s1_ref[...], (1, MLA_HEADS))
        s2 = jnp.tile(s2_ref[...], (1, MLA_HEADS))
        dqp = _rope_t(dq_ref[...] * ATTN_SCALE, c, s1, s2).astype(BF16)
        dkp = _rope_t(dk_ref[...], c, s1, s2).astype(BF16)
        dvb = dv_ref[...].astype(BF16)
        dqb_ref[...] = dqp
        dkb_ref[...] = dkp
        dvb_ref[...] = dvb
        dqn = lax.dot_general(dqp, wq_ref[...], _NT, preferred_element_type=F32)
        dkin = lax.dot_general(dkp, wk_ref[...], _NT, preferred_element_type=F32)
        dkvn = dkin[:, :KV_LORA] + lax.dot_general(dvb, wv_ref[...], _NT, preferred_element_type=F32)
        zz = z_ref[...]

        def rms_bwd(x, g, dy):
            rstd = lax.rsqrt(jnp.mean(x * x, axis=-1, keepdims=True) + RMS_EPS)
            xh = x * rstd
            dxh = dy * g
            dx = rstd * (dxh - xh * jnp.mean(dxh * xh, axis=-1, keepdims=True))
            return dx, jnp.sum(dy * xh, axis=0, keepdims=True)

        dcq, dgq = rms_bwd(zz[:, :Q_LORA], gq_ref[...], dqn)
        dckv, dgkv = rms_bwd(zz[:, Q_LORA:Q_LORA + KV_LORA], gkv_ref[...], dkvn)
        dz_ref[:, :Q_LORA] = dcq.astype(BF16)
        dz_ref[:, Q_LORA:Q_LORA + KV_LORA] = dckv.astype(BF16)
        dz_ref[:, Q_LORA + KV_LORA:] = dkin[:, KV_LORA:].astype(BF16)

        @pl.when(i == 0)
        def _():
            dgq_ref[...] = jnp.zeros(dgq_ref.shape, F32)
            dgkv_ref[...] = jnp.zeros(dgkv_ref.shape, F32)

        dgq_ref[0:1, :] += dgq
        dgkv_ref[0:1, :] += dgkv

    def full(shape):
        return pl.BlockSpec(shape, lambda i: (0, 0))

    def rows(w, c=0):
        return pl.BlockSpec((tm, w), lambda i: (i, c))

    return pl.pallas_call(
        body, name=name, grid=(T // tm,),
        in_specs=[rows(D_QK), rows(D_QK), rows(D_MLA), rows(512, Z_MID // 512),
                  full((1, Q_LORA)), full((1, KV_LORA)),
                  full((Q_LORA, D_QK)), full((Q_LORA, D_QK)), full((KV_LORA, D_MLA)),
                  rows(LANE), rows(LANE), rows(LANE)],
        out_specs=[rows(512), rows(D_QK), rows(D_QK), rows(D_MLA), full((8, Q_LORA)), full((8, KV_LORA))],
        out_shape=[jax.ShapeDtypeStruct((T, 512), BF16), jax.ShapeDtypeStruct((T, D_QK), BF16),
                   jax.ShapeDtypeStruct((T, D_QK), BF16), jax.ShapeDtypeStruct((T, D_MLA), BF16),
                   jax.ShapeDtypeStruct((8, Q_LORA), F32), jax.ShapeDtypeStruct((8, KV_LORA), F32)],
        compiler_params=_params(("arbitrary",)),
    )(dq, dk, dv, z, gq, gkv, wq_t, wk_t, wv_t, *tabs)


FLAT_W = 1024


ELEMENTWISE_TILE_BYTES = 768 * 1024


def _row_tile(R, C):
    width = -(-C // LANE) * LANE * 4
    best = None
    for t in range(16, R + 1, 16):
        if R % t == 0 and t * width <= ELEMENTWISE_TILE_BYTES:
            best = t
    if best is None:
        best = R
    return best


def _adamw(w, m, v, layer, parts, part_index, *, name):
    _, R, C = w.shape
    tr = _row_tile(R, C)
    bc1 = 1.0 - ADAM_B1 ** ADAM_STEP
    bc2 = 1.0 - ADAM_B2 ** ADAM_STEP
    n_parts = len(parts)

    def body(idx_ref, w_ref, m_ref, v_ref, *refs):
        g_refs = refs[:n_parts]
        g_out, d_out, m_out, v_out = refs[n_parts:]
        g = g_refs[0][...].astype(F32)
        for r in g_refs[1:]:
            g = g + r[...].astype(F32)
        wv = w_ref[...]
        mn = ADAM_B1 * m_ref[...] + (1.0 - ADAM_B1) * g
        vn = ADAM_B2 * v_ref[...] + (1.0 - ADAM_B2) * (g * g)
        m_hat = mn / bc1
        v_hat = vn / bc2
        g_out[...] = g
        d_out[...] = -ADAM_LR * (m_hat / (jnp.sqrt(v_hat) + ADAM_EPS) + ADAM_WD * wv)
        m_out[...] = mn
        v_out[...] = vn

    layer_row = pl.BlockSpec((None, tr, C), lambda i, idx: (layer, i, 0))
    in_specs = [layer_row, layer_row, layer_row]
    args = [w, m, v]
    for arr, slot in parts:
        if slot is None:
            in_specs.append(pl.BlockSpec((None, tr, C), lambda i, idx: (idx[0], i, 0)))
        else:
            in_specs.append(pl.BlockSpec((None, tr, C), lambda i, idx, slot=slot: (slot, i, 0)))
        args.append(arr)
    grid_spec = pltpu.PrefetchScalarGridSpec(
        num_scalar_prefetch=1, grid=(R // tr,), in_specs=in_specs,
        out_specs=[pl.BlockSpec((tr, C), lambda i, idx: (i, 0))] * 4)
    return pl.pallas_call(
        body, name=name, grid_spec=grid_spec,
        out_shape=[jax.ShapeDtypeStruct((R, C), F32)] * 4,
        compiler_params=_params(("parallel",)),
    )(part_index, *args)


def _sum8(parts, *, name):
    _, R, _ = parts.shape

    def body(p_ref, o_ref):
        acc = p_ref[0]
        for d in range(1, N_DEV):
            acc = acc + p_ref[d]
        o_ref[...] = acc

    return pl.pallas_call(
        body, name=name, grid=(1,),
        in_specs=[pl.BlockSpec((N_DEV, R, FLAT_W), lambda i: (0, 0, 0))],
        out_specs=pl.BlockSpec((R, FLAT_W), lambda i: (0, 0)),
        out_shape=jax.ShapeDtypeStruct((R, FLAT_W), F32),
        compiler_params=_params(("arbitrary",)),
    )(parts)


_MESH_ID = pl.DeviceIdType.MESH
_ANY = pl.BlockSpec(memory_space=pl.ANY)


def _all_gather(shards, *, name):
    n = len(shards)

    def body(*refs):
        x_refs, out_refs = refs[:n], refs[n:2 * n]
        send_sems, recv_sems, local_sems = refs[2 * n:]
        x, y, c = lax.axis_index("x"), lax.axis_index("y"), lax.axis_index("c")
        me, sibling = (x, y, c), (x, y, 1 - c)
        chips = [(1 - x, y), (x, 1 - y), (1 - x, 1 - y)]

        def blk(a, px, py, pc):
            return out_refs[a].at[4 * px + 2 * py + pc]

        def copy(a, k, block, to, src=None):
            return pltpu.make_async_remote_copy(
                src_ref=blk(a, *block) if src is None else src, dst_ref=blk(a, *block),
                send_sem=send_sems.at[7 * a + k], recv_sem=recv_sems.at[7 * a + k],
                device_id=to, device_id_type=_MESH_ID)

        mine = [pltpu.make_async_copy(x_refs[a], blk(a, *me), local_sems.at[a]) for a in range(n)]
        for cp in mine:
            cp.start()
        first = []
        for a in range(n):
            first.append(copy(a, 0, me, sibling, src=x_refs[a]))
            first += [copy(a, 1 + j, me, (*chip, c), src=x_refs[a]) for j, chip in enumerate(chips)]
        for cp in first:
            cp.start()
        passed = []
        for j, chip in enumerate(chips):
            for a in range(n):
                copy(a, 1 + j, (*chip, c), me).wait_recv()
                fwd = copy(a, 4 + j, (*chip, c), sibling)
                fwd.start()
                passed.append(fwd)
        for a in range(n):
            copy(a, 0, sibling, me).wait_recv()
        for j, chip in enumerate(chips):
            for a in range(n):
                copy(a, 4 + j, (*chip, 1 - c), me).wait_recv()
        for cp in first + passed:
            cp.wait_send()
        for cp in mine:
            cp.wait()

    return pl.pallas_call(
        body, name=name,
        out_shape=[jax.ShapeDtypeStruct((N_DEV,) + s.shape, s.dtype) for s in shards],
        in_specs=[_ANY] * n, out_specs=[_ANY] * n,
        scratch_shapes=[pltpu.SemaphoreType.DMA((7 * n,)), pltpu.SemaphoreType.DMA((7 * n,)),
                        pltpu.SemaphoreType.DMA((n,))],
    )(*shards)


_BIG = (("ffn1_w_up", 2), ("ffn1_w_down", 1), ("mix_w_in", 2), ("w_uq", 2), ("w_ukv", 2),
        ("w_br_conv", 2), ("w_br_mla", 2), ("w_o", 1), ("ffn2_w_up", 2), ("ffn2_w_down", 1))
_SMALL_SHARDED = (("meta_tokens", 1), ("mix_b_gate", 2), ("conv_w", 2), ("ln_g", 2), ("ln_b", 2))
_SMALL_REPL = ("q_norm_g", "kv_norm_g")


def _pack(arrs, dtype, row_align=8):
    flat = jnp.concatenate([a.reshape(-1).astype(dtype) for a in arrs])
    n = flat.shape[0]
    rows = -(-n // (row_align * FLAT_W)) * row_align
    return jnp.pad(flat, (0, rows * FLAT_W - n)).reshape(rows, FLAT_W)


def _unpack(flat, shapes):
    flat = flat.reshape(-1)
    out, off = [], 0
    for s in shapes:
        n = int(np.prod(s))
        out.append(flat[off:off + n].reshape(s))
        off += n
    return out


def _unpack_gathered(gathered, shapes, axes):
    g2 = gathered.reshape(N_DEV, -1)
    out, off = [], 0
    for s, ax in zip(shapes, axes):
        n = int(np.prod(s))
        blocks = g2[:, off:off + n].reshape((N_DEV,) + tuple(s))
        full = jnp.moveaxis(blocks, 0, ax)
        out.append(full.reshape(tuple(s[:ax]) + (N_DEV * s[ax],) + tuple(s[ax + 1:])))
        off += n
    return out


def _to_dest(full, ax):
    s = full.shape
    split = full.reshape(s[:ax] + (N_DEV, s[ax] // N_DEV) + s[ax + 1:])
    return jnp.moveaxis(split, ax, 0)


def _from_blocks(g, ax):
    full = jnp.moveaxis(g, 0, ax)
    s = full.shape
    return full.reshape(s[:ax] + (s[ax] * s[ax + 1],) + s[ax + 2:])


def _rope_tables(T):
    inv_freq = 1.0 / (ROPE_BASE ** (jnp.arange(0, QK_ROPE, 2, dtype=F32) / QK_ROPE))
    ang = jnp.arange(T, dtype=F32)[:, None] * inv_freq[None, :]
    cos, sin = jnp.cos(ang), jnp.sin(ang)
    half = QK_ROPE // 2
    ones = jnp.ones((T, QK_NOPE), F32)
    zeros = lambda w: jnp.zeros((T, w), F32)
    c = jnp.concatenate([ones, cos, cos, zeros(HEAD_PAD - QK_NOPE - QK_ROPE)], axis=1)
    s1 = jnp.concatenate([zeros(QK_NOPE + half), sin, zeros(HEAD_PAD - QK_NOPE - QK_ROPE)], axis=1)
    s2 = jnp.concatenate([zeros(QK_NOPE), -sin, zeros(HEAD_PAD - QK_NOPE - half)], axis=1)
    return c, s1, s2


_FFN1 = ("ffn1_w_up", "ffn1_w_down")
_MIXER = ("mix_w_in", "w_uq", "w_ukv", "w_br_conv", "w_br_mla", "w_o")
_FFN2 = ("ffn2_w_up", "ffn2_w_down")
_MIXER_EARLY = ("w_br_conv", "w_br_mla", "w_o")


def _mixer_weights(gathered):
    axes = dict(_BIG)
    W = {n: _from_blocks(gathered[n], axes[n] - 1) for n in _MIXER}
    w_in = W["mix_w_in"]
    w_in_p = jnp.concatenate([w_in[:, :Z_KR_END], jnp.zeros((D_MODEL, D_IN_PAD - D_IN_REAL), BF16),
                              w_in[:, Z_KR_END:]], axis=1)
    w_uq = W["w_uq"].reshape(Q_LORA, MLA_HEADS, QK_NOPE + QK_ROPE)
    wq = jnp.pad(w_uq, ((0, 0), (0, 0), (0, HEAD_PAD - QK_NOPE - QK_ROPE))).reshape(Q_LORA, D_QK)
    w_ukv = W["w_ukv"].reshape(KV_LORA, MLA_HEADS, QK_NOPE + V_HEAD)
    wk_top = jnp.pad(w_ukv[:, :, :QK_NOPE], ((0, 0), (0, 0), (0, HEAD_PAD - QK_NOPE))).reshape(KV_LORA, D_QK)
    place = np.zeros((Q_LORA - KV_LORA, MLA_HEADS, HEAD_PAD), np.float32)
    for r in range(QK_ROPE):
        place[r, :, QK_NOPE + r] = 1.0
    wk = jnp.concatenate([wk_top, jnp.asarray(place.reshape(Q_LORA - KV_LORA, D_QK), BF16)], axis=0)
    wv = w_ukv[:, :, QK_NOPE:].reshape(KV_LORA, D_MLA)
    wv_ext = jnp.pad(w_ukv[:, :, QK_NOPE:], ((0, 0), (0, 0), (0, HEAD_PAD - V_HEAD))).reshape(KV_LORA, D_QK)
    return dict(w_in=w_in_p, wq=wq, wk=wk, wv=wv, wv_ext=wv_ext,
                wbc=W["w_br_conv"], wbm=W["w_br_mla"], wo=W["w_o"])


def _row8(v):
    return jnp.pad(v, ((0, 8 - v.shape[0]), (0, 0)))


def _local_step(x, tgt, gathered_up1, w_blocks, S):
    big_names = [n for n, _ in _BIG]
    axes = dict(_BIG)
    t_real = N_META + x.shape[0]
    T = -(-t_real // ROW_ALIGN) * ROW_ALIGN
    pad = T - t_real
    tm = _tile(T, (384, 256, 128))
    tms = _tile(T, (768, 256, 128))
    blk = _tile(T, (768, 256, 128))
    tabs = _rope_tables(T)

    h0 = jnp.concatenate([S["meta_tokens"], x, jnp.zeros((pad, D_MODEL), F32)], axis=0)
    tgt_p = jnp.concatenate([jnp.zeros((N_META, D_MODEL), F32), tgt, jnp.zeros((pad, D_MODEL), F32)], axis=0)

    def gather_of(l, names):
        return _Exchange("gather", [w_blocks[l][n] for n in names])

    saved = []
    h, hb = h0, h0.astype(BF16)
    G = dict(gathered_up1)
    for l in range(DEPTH):
        first = l == 0
        sv = dict(h_in=h, h_in_b=hb, up1=G["ffn1_w_up"])
        names = ("ffn1_w_down", "mix_w_in")
        gu, a, *got = _ffn_up(hb, sv["up1"], name=f"ffn_up_{l}a", tm=tms,
                              exchange=gather_of(0, names) if first else None)
        G.update(zip(names, got))
        sv["down1"] = _from_blocks(G["ffn1_w_down"], 0)
        names = tuple(n for n in _MIXER if n != "mix_w_in")
        r, h1, h1b, *got = _mm_res_ln(a, sv["down1"], h, S["ln_g"][l, 0:1], S["ln_b"][l, 0:1], scale=0.5,
                                      name=f"ffn_down_ln_{l}a", tm=tm,
                                      exchange=gather_of(0, names) if first else None)
        G.update(zip(names, got))
        sv["f1"] = dict(gu=gu, a=a, r=r)
        lw = _mixer_weights(G)
        sv["lw"] = lw
        if first:
            z, *got = _mm(h1b, lw["w_in"], out_dtype=F32, name=f"mix_in_{l}", tm=tms, tn=1024, tk=D_MODEL,
                          exchange=gather_of(0, _FFN2[:1]))
            G.update(zip(_FFN2[:1], got))
        else:
            z = _mm(h1b, lw["w_in"], out_dtype=F32, name=f"mix_in_{l}", tm=tms, tn=1024, tk=D_MODEL)
        conv_w8 = _row8(S["conv_w"][l])
        bg8 = _row8(S["mix_b_gate"][l])
        yc = _conv_fwd(z, conv_w8, name=f"conv_fwd_{l}", tm=tms)
        gq, gkv = S["q_norm_g"][l:l + 1], S["kv_norm_g"][l:l + 1]
        q, k, v, qn, kin = _qkv_proj(z, gq, gkv, lw["wq"], lw["wk"], lw["wv_ext"], tabs, name=f"qkv_proj_{l}", tm=tm)
        o, lse, *nxt = _attn_fwd(q, k, v, name=f"attn_fwd_{l}", blk=blk,
                                 exchange=gather_of(1, _FFN1 + _MIXER if first else _FFN2))
        if not first:
            G.update(zip(_FFN2, nxt))
        sv["up2"] = G["ffn2_w_up"]
        mg, pa, pb = _merge(yc, o, lw["wbc"], lw["wbm"], z, bg8, name=f"merge_{l}", tm=tm)
        r2, h2, h2b = _mm_res_ln(mg, lw["wo"], h1, S["ln_g"][l, 1:2], S["ln_b"][l, 1:2], scale=1.0,
                                 name=f"wo_ln_{l}", tm=tm)
        gu, a, *got = _ffn_up(h2b, sv["up2"], name=f"ffn_up_{l}b", tm=tms,
                              exchange=gather_of(0, _FFN2[1:]) if first else None)
        G.update(zip(_FFN2[1:], got))
        sv["down2"] = _from_blocks(G["ffn2_w_down"], 0)
        r, h3, h3b = _mm_res_ln(a, sv["down2"], h2, S["ln_g"][l, 2:3], S["ln_b"][l, 2:3], scale=0.5,
                                name=f"ffn_down_ln_{l}b", tm=tm)
        sv["f2"] = dict(gu=gu, a=a, r=r)
        if first:
            G = dict(zip(_FFN1 + _MIXER, nxt))
        sv.update(h1b=h1b, z=z, conv_w8=conv_w8, bg8=bg8, yc=yc, gq=gq, gkv=gkv, q=q, k=k, v=v, qn=qn, kin=kin,
                  o=o, lse=lse, mg=mg, pa=pa, pb=pb, r2=r2, h2b=h2b)
        saved.append(sv)
        h, hb = h3, h3b

    dh, loss8 = _loss_head(h, tgt_p, t_real=t_real, name="loss_head", tm=tm)

    tk = _tile(T, (2816, 768, 256, 128))
    grads = {n: [None] * DEPTH for n, _ in _BIG}
    for n in ("mix_b_gate", "conv_w", "q_norm_g", "kv_norm_g"):
        grads[n] = [None] * DEPTH
    grads["ln_g"] = [[None] * 3 for _ in range(DEPTH)]
    grads["ln_b"] = [[None] * 3 for _ in range(DEPTH)]

    def dest_of(l, names):
        return {n: grads[n][l] if n.endswith("w_up") else _to_dest(grads[n][l], axes[n] - 1) for n in names}

    def scatter_of(dest):
        return _Exchange("scatter", [dest[n].astype(BF16) for n in dest])

    dest = [{}, {}]
    received = [{}, {}]

    def ffn_bwd(dy, f, h_in_b, up8, down, g, tag, names, l, carry=None):
        dr, dfb, dg, db = _ln_bwd(dy, f["r"], g, scale=0.5, name=f"ln_bwd_{tag}", tm=tm)
        dgu, *got = _ffn_bwd_mid(dfb, down, f["gu"], name=f"ffn_bwd_mid_{tag}", tm=tms,
                                 exchange=scatter_of(carry) if carry else None)
        if carry:
            received[l].update(zip(carry, got))
        grads[names[1]][l] = _mm_tn_call(
            f["a"], dfb,
            pl.BlockSpec((None, tk, FF_BLK), lambda i, j, k: (i, k, 0)),
            pl.BlockSpec((tk, D_MODEL), lambda i, j, k: (k, 0)),
            out_shape=jax.ShapeDtypeStruct((D_FF, D_MODEL), F32),
            out_spec=pl.BlockSpec((FF_BLK, D_MODEL), lambda i, j, k: (i, 0)),
            grid=(FF_HALF_BLOCKS, 1, T // tk), name=f"dw_down_{tag}")
        own_down = dest_of(l, names[1:]) if carry else None
        d_up = _mm_tn_call(
            h_in_b, dgu,
            pl.BlockSpec((tk, D_MODEL), lambda i, j, k: (k, 0)),
            pl.BlockSpec((None, None, tk, FF_BLK),
                         lambda i, j, k: (j // FF_HALF_BLOCKS, j % FF_HALF_BLOCKS, k, 0)),
            out_shape=jax.ShapeDtypeStruct((N_DEV, D_MODEL, FF_BLK), F32),
            out_spec=pl.BlockSpec((None, D_MODEL, FF_BLK), lambda i, j, k: (j, 0, 0)),
            grid=(1, N_DEV, T // tk), name=f"dw_up_{tag}",
            exchange=scatter_of(own_down) if carry else None)
        own = None
        if carry:
            d_up, *got = d_up
            received[l].update(zip(own_down, got))
        grads[names[0]][l] = d_up
        if carry:
            own = dest_of(l, names[:1])
            dest[l].update(own_down)
            dest[l].update(own)
        row = pl.BlockSpec((tms, D_MODEL), lambda i, j, k: (i, 0))
        dh_in = _mm_call(
            dgu, up8,
            pl.BlockSpec((None, FF_HALF_BLOCKS, tms, FF_BLK), lambda i, j, k: (k, 0, i, 0)),
            pl.BlockSpec((FF_HALF_BLOCKS, D_MODEL, FF_BLK), lambda i, j, k: (k, 0, 0)),
            out_shape=jax.ShapeDtypeStruct((T, D_MODEL), F32), out_spec=row, acc_shape=(tms, D_MODEL),
            grid=(T // tms, 1, 2), name=f"ffn_dx_{tag}", trans_b=True, res=dr, res_spec=row,
            res_scale=ALPHA, pieces=FF_HALF_BLOCKS, exchange=scatter_of(own) if own else None)
        if own:
            dh_in, *got = dh_in
            received[l].update(zip(own, got))
        return dh_in, dg[0], db[0]

    for l in reversed(range(DEPTH)):
        sv = saved[l]
        lw = sv["lw"]
        dh, grads["ln_g"][l][2], grads["ln_b"][l][2] = ffn_bwd(
            dh, sv["f2"], sv["h2b"], sv["up2"], sv["down2"], S["ln_g"][l, 2:3], f"{l}b", _FFN2, l)
        dr2, dmb, dg, db = _ln_bwd(dh, sv["r2"], S["ln_g"][l, 1:2], scale=1.0, name=f"ln_bwd_{l}m", tm=tm)
        grads["ln_g"][l][1], grads["ln_b"][l][1] = dg[0], db[0]
        grads["w_o"][l] = _mm_tn(sv["mg"], dmb, name=f"dw_o_{l}", tm=D_MODEL, tn=D_MODEL, tk=tk)
        dpa, dpb, dgate, dbg = _wo_bwd(dmb, lw["wo"], sv["z"], sv["bg8"], sv["pa"], sv["pb"], name=f"wo_bwd_{l}", tm=tm)
        grads["mix_b_gate"][l] = dbg[0:2]
        grads["w_br_conv"][l] = _mm_tn(sv["yc"], dpa, name=f"dw_br_conv_{l}", tm=D_CONV, tn=D_MODEL, tk=tk)
        grads["w_br_mla"][l] = _mm_tn(sv["o"], dpb, name=f"dw_br_mla_{l}", tm=D_MLA, tn=D_MODEL, tk=tk)
        dyc = _mm(dpa, lw["wbc"], trans_b=True, out_dtype=F32, name=f"d_yconv_{l}", tm=tms, tn=D_CONV, tk=D_MODEL)
        dym = _mm(dpb, lw["wbm"], trans_b=True, out_dtype=BF16, name=f"d_ymla_{l}", tm=tms, tn=D_MLA, tk=D_MODEL)
        dz_conv, dcw = _conv_bwd(dyc, sv["z"], sv["conv_w8"], name=f"conv_bwd_{l}", tm=tms)
        grads["conv_w"][l] = dcw[0:CONV_WIDTH]
        if l == 0:
            dest[1] = dest_of(1, big_names)
            dest[0].update(dest_of(0, _FFN2 + _MIXER_EARLY))
            sent = {(1, n): dest[1][n] for n in big_names}
            sent.update({(0, n): dest[0][n] for n in _FFN2 + _MIXER_EARLY})
            dq, dk, dv, *got = _attn_bwd(
                sv["q"], sv["k"], sv["v"], sv["o"], dym, sv["lse"], name=f"attn_bwd_{l}", blk=blk,
                exchange=scatter_of(sent))
            for (layer, n), r in zip(sent, got):
                received[layer][n] = r
        else:
            dq, dk, dv = _attn_bwd(sv["q"], sv["k"], sv["v"], sv["o"], dym, sv["lse"], name=f"attn_bwd_{l}", blk=blk)
        dz_mid, dqb, dkb, dvb, dgq, dgkv = _qkv_bwd(dq, dk, dv, sv["z"], sv["gq"], sv["gkv"], lw["wq"], lw["wk"],
                                                    lw["wv"], tabs, name=f"qkv_bwd_{l}", tm=tm)
        grads["q_norm_g"][l], grads["kv_norm_g"][l] = dgq[0], dgkv[0]
        d_wq = _mm_tn(sv["qn"], dqb, name=f"dw_uq_{l}", tm=Q_LORA, tn=D_QK, tk=tk)
        d_wk = _mm_tn(sv["kin"], dkb, name=f"dw_uk_{l}", tm=Q_LORA, tn=D_QK, tk=tk)
        d_wv = _mm_tn(sv["kin"], dvb, name=f"dw_uv_{l}", tm=Q_LORA, tn=D_MLA, tk=tk)
        grads["w_uq"][l] = d_wq.reshape(Q_LORA, MLA_HEADS, HEAD_PAD)[:, :, :QK_NOPE + QK_ROPE].reshape(Q_LORA, -1)
        d_kn = d_wk[:KV_LORA].reshape(KV_LORA, MLA_HEADS, HEAD_PAD)[:, :, :QK_NOPE]
        d_vv = d_wv[:KV_LORA].reshape(KV_LORA, MLA_HEADS, V_HEAD)
        grads["w_ukv"][l] = jnp.concatenate([d_kn, d_vv], axis=-1).reshape(KV_LORA, -1)
        dz = jnp.concatenate([dz_conv, dz_mid, dgate], axis=1)
        d_win = _mm_tn(sv["h1b"], dz, name=f"dw_in_{l}", tm=D_MODEL, tn=1024, tk=tk)
        grads["mix_w_in"][l] = jnp.concatenate([d_win[:, :Z_KR_END], d_win[:, Z_KR_END + D_IN_PAD - D_IN_REAL:]], axis=1)
        dh = _mm(dz, lw["w_in"], trans_b=True, out_dtype=F32, name=f"mix_dx_{l}", res=dr2, res_scale=ALPHA,
                 tm=tms, tn=D_MODEL, tk=2048)
        carry = None
        if l == 0:
            carry = dest_of(0, [n for n in _MIXER if n not in _MIXER_EARLY])
            dest[0].update(carry)
        dh, grads["ln_g"][l][0], grads["ln_b"][l][0] = ffn_bwd(
            dh, sv["f1"], sv["h_in_b"], sv["up1"], sv["down1"], S["ln_g"][l, 0:1], f"{l}a", _FFN1, l, carry=carry)

    small = {n: jnp.stack(grads[n]) for n in ("mix_b_gate", "conv_w", "q_norm_g", "kv_norm_g")}
    small["ln_g"] = jnp.stack([jnp.stack(g) for g in grads["ln_g"]])
    small["ln_b"] = jnp.stack([jnp.stack(g) for g in grads["ln_b"]])
    small["meta_tokens"] = dh[:N_META]
    return loss8, dh[N_META:t_real], dest, received, small


def kernel(x, meta_tokens, ffn1_w_up, ffn1_w_down, mix_w_in, mix_b_gate, conv_w, q_norm_g, w_uq, kv_norm_g, w_ukv, w_br_conv, w_br_mla, w_o, ffn2_w_up, ffn2_w_down, ln_g, ln_b, loss_target, m_meta_tokens, m_ffn1_w_up, m_ffn1_w_down, m_mix_w_in, m_mix_b_gate, m_conv_w, m_q_norm_g, m_w_uq, m_kv_norm_g, m_w_ukv, m_w_br_conv, m_w_br_mla, m_w_o, m_ffn2_w_up, m_ffn2_w_down, m_ln_g, m_ln_b, v_meta_tokens, v_ffn1_w_up, v_ffn1_w_down, v_mix_w_in, v_mix_b_gate, v_conv_w, v_q_norm_g, v_w_uq, v_kv_norm_g, v_w_ukv, v_w_br_conv, v_w_br_mla, v_w_o, v_ffn2_w_up, v_ffn2_w_down, v_ln_g, v_ln_b):
    names = ["meta_tokens", "ffn1_w_up", "ffn1_w_down", "mix_w_in", "mix_b_gate", "conv_w", "q_norm_g", "w_uq",
             "kv_norm_g", "w_ukv", "w_br_conv", "w_br_mla", "w_o", "ffn2_w_up", "ffn2_w_down", "ln_g", "ln_b"]
    w = dict(zip(names, (meta_tokens, ffn1_w_up, ffn1_w_down, mix_w_in, mix_b_gate, conv_w, q_norm_g, w_uq,
                         kv_norm_g, w_ukv, w_br_conv, w_br_mla, w_o, ffn2_w_up, ffn2_w_down, ln_g, ln_b)))
    m = dict(zip(names, (m_meta_tokens, m_ffn1_w_up, m_ffn1_w_down, m_mix_w_in, m_mix_b_gate, m_conv_w, m_q_norm_g,
                         m_w_uq, m_kv_norm_g, m_w_ukv, m_w_br_conv, m_w_br_mla, m_w_o, m_ffn2_w_up, m_ffn2_w_down,
                         m_ln_g, m_ln_b)))
    v = dict(zip(names, (v_meta_tokens, v_ffn1_w_up, v_ffn1_w_down, v_mix_w_in, v_mix_b_gate, v_conv_w, v_q_norm_g,
                         v_w_uq, v_kv_norm_g, v_w_ukv, v_w_br_conv, v_w_br_mla, v_w_o, v_ffn2_w_up, v_ffn2_w_down,
                         v_ln_g, v_ln_b)))
    ix, iy, ic = lax.axis_index("x"), lax.axis_index("y"), lax.axis_index("c")
    dev = 4 * ix + 2 * iy + ic

    big_names = [n for n, _ in _BIG]
    small_names = [n for n, _ in _SMALL_SHARDED]
    small_axes = [a for _, a in _SMALL_SHARDED]
    small_shapes = [w[n].shape for n in small_names]
    gathered = _all_gather([w[n][0].astype(BF16) for n in _FFN1[:1]] + [_pack([w[n] for n in small_names], F32)],
                           name="all_gather_weights")
    gathered_up1 = dict(zip(_FFN1[:1], gathered[:-1]))
    w_blocks = [{n: w[n][l].astype(BF16) for n in big_names} for l in range(DEPTH)]
    S = dict(zip(small_names, _unpack_gathered(gathered[-1], small_shapes, small_axes)))
    S["q_norm_g"], S["kv_norm_g"] = q_norm_g, kv_norm_g

    loss8, grad_x, dest, received, G = _local_step(x[0], loss_target[0], gathered_up1, w_blocks, S)

    my_dev = dev.reshape(1).astype(jnp.int32)
    big_res = [{}, {}, {}, {}]
    for n in big_names:
        res = [_adamw(w[n], m[n], v[n], l, [(dest[l][n], None)] + [(received[l][n], k) for k in range(N_DEV - 1)],
                      my_dev, name=f"adamw{l}_{n}") for l in range(DEPTH)]
        for kind in range(4):
            big_res[kind][n] = jnp.stack([r[kind] for r in res])

    small_all = small_names + list(_SMALL_REPL)
    part = _pack([G[n] for n in small_all] + [loss8[0, 0:1]], F32)
    full_shapes = [G[n].shape for n in small_all] + [(1,)]
    summed = _sum8(_all_gather([part], name="all_gather_small_grads")[0], name="sum_small_grads")
    unpacked = _unpack(summed, full_shapes)
    loss = unpacked[-1][0]
    g_full = dict(zip(small_all, unpacked[:-1]))
    g_loc = []
    for n in small_all:
        if n in _SMALL_REPL:
            g_loc.append(g_full[n])
        else:
            ax = dict(_SMALL_SHARDED)[n]
            g_loc.append(lax.dynamic_slice_in_dim(g_full[n], dev * w[n].shape[ax], w[n].shape[ax], axis=ax))
    loc_shapes = [w[n].shape for n in small_all]
    g_pack = _pack(g_loc, F32)
    small_out = _adamw(_pack([w[n] for n in small_all], F32)[None], _pack([m[n] for n in small_all], F32)[None],
                       _pack([v[n] for n in small_all], F32)[None], 0, [(g_pack[None], 0)],
                       jnp.zeros((1,), jnp.int32), name="adamw_small")
    small_res = [dict(zip(small_all, _unpack(o, loc_shapes))) for o in small_out]

    outs = [loss, grad_x[None]]
    for kind in range(4):
        for n in names:
            outs.append(big_res[kind][n] if n in big_res[kind] else small_res[kind][n])
    return tuple(outs)
```

```python
import functools

import numpy as np
import jax
import jax.numpy as jnp
from jax import lax
from jax.experimental import pallas as pl
from jax.experimental.pallas import tpu as pltpu

F32 = jnp.float32
BF16 = jnp.bfloat16

D_MODEL = 1024
DEPTH = 2
N_META = 16
D_CONV = 512
CONV_WIDTH = 3
MLA_HEADS = 8
QK_NOPE = 64
QK_ROPE = 32
V_HEAD = 64
Q_LORA = 256
KV_LORA = 128
D_MLA = MLA_HEADS * V_HEAD
ROPE_BASE = 10000.0
NEG_INF = -1e30
D_FF = 2816
ALPHA = (2 * DEPTH) ** 0.25
LN_EPS = 1e-5
RMS_EPS = 1e-6
ATTN_SCALE = (QK_NOPE + QK_ROPE) ** -0.5
ADAM_LR = 0.001
ADAM_B1 = 0.9
ADAM_B2 = 0.999
ADAM_EPS = 1e-08
ADAM_WD = 0.01
ADAM_STEP = 10

N_DEV = 8
HEAD_PAD = 128
HEADS_PER_STEP = 4
FWD_HEADS_PER_STEP = 4
FF_BLK = 2 * D_FF // N_DEV
FF_HALF_BLOCKS = N_DEV // 2
D_QK = MLA_HEADS * HEAD_PAD
Z_CONV = 0
Z_MID = 1536
Z_GATE = 2048
D_IN_PAD = 4096
D_IN_REAL = 4000
Z_KR_END = Z_MID + Q_LORA + KV_LORA + QK_ROPE

V7X_VMEM_LIMIT = 56 * 1024 * 1024
LANE = 128
ROW_ALIGN = 256


def _tile(n, cands):
    for c in cands:
        if n % c == 0:
            return c
    raise ValueError(f"no tile for {n} in {cands}")


def _params(sem):
    return pltpu.CompilerParams(dimension_semantics=sem, vmem_limit_bytes=V7X_VMEM_LIMIT)


def _mm_call(a, b, a_spec, b_spec, *, out_shape, out_spec, acc_shape, grid, name, trans_b=False,
             res=None, res_spec=None, res_scale=1.0, pieces=1, exchange=None):
    nk = grid[2]
    has_res = res is not None
    out_dtype = out_shape.dtype
    dims = (((1,), (1,)), ((), ())) if trans_b else (((1,), (0,)), ((), ()))

    def body(*refs):
        if has_res:
            a_ref, b_ref, r_ref, o_ref, acc = refs
        else:
            a_ref, b_ref, o_ref, acc = refs
        k = pl.program_id(2)
        if pieces == 1:
            part = lax.dot_general(a_ref[...], b_ref[...], dims, preferred_element_type=F32)
        else:
            part = lax.dot_general(a_ref[0], b_ref[0], dims, preferred_element_type=F32)
            for p in range(1, pieces):
                part = part + lax.dot_general(a_ref[p], b_ref[p], dims, preferred_element_type=F32)

        @pl.when(k == 0)
        def _():
            acc[...] = part

        @pl.when(k > 0)
        def _():
            acc[...] += part

        @pl.when(k == nk - 1)
        def _():
            out = acc[...]
            if has_res:
                out = out + res_scale * r_ref[...]
            o_ref[...] = out.astype(out_dtype)

    in_specs = [a_spec, b_spec]
    args = [a, b]
    if has_res:
        in_specs.append(res_spec)
        args.append(res)
    if exchange is None:
        return pl.pallas_call(
            body, name=name, grid=grid, in_specs=in_specs, out_specs=out_spec, out_shape=out_shape,
            scratch_shapes=[pltpu.VMEM(acc_shape, F32)],
            compiler_params=_params(("parallel", "parallel", "arbitrary")),
        )(*args)
    ex_args, ex_specs, ex_out, ex_scratch = _exchange_operands(exchange)
    return pl.pallas_call(
        _carry_exchange(body, exchange, 0, len(args), 1, grid), name=name, grid=grid,
        in_specs=in_specs + ex_specs, out_specs=[out_spec] + ex_specs, out_shape=[out_shape] + ex_out,
        scratch_shapes=[pltpu.VMEM(acc_shape, F32)] + ex_scratch,
        compiler_params=_params(("arbitrary", "arbitrary", "arbitrary")),
    )(*args, *ex_args)


def _mm(a, b, *, out_dtype, name, trans_b=False, res=None, res_scale=1.0, tm, tn, tk, exchange=None):
    M, K = a.shape
    N = b.shape[0] if trans_b else b.shape[1]
    assert M % tm == 0 and N % tn == 0 and K % tk == 0
    b_spec = (pl.BlockSpec((tn, tk), lambda i, j, k: (j, k)) if trans_b
              else pl.BlockSpec((tk, tn), lambda i, j, k: (k, j)))
    tile = pl.BlockSpec((tm, tn), lambda i, j, k: (i, j))
    return _mm_call(a, b, pl.BlockSpec((tm, tk), lambda i, j, k: (i, k)), b_spec,
                    out_shape=jax.ShapeDtypeStruct((M, N), out_dtype), out_spec=tile, acc_shape=(tm, tn),
                    grid=(M // tm, N // tn, K // tk), name=name, trans_b=trans_b,
                    res=res, res_spec=tile, res_scale=res_scale, exchange=exchange)


def _mm_tn_call(a, b, a_spec, b_spec, *, out_shape, out_spec, grid, name, exchange=None):
    def body(a_ref, b_ref, o_ref):
        k = pl.program_id(2)
        part = lax.dot_general(a_ref[...], b_ref[...], (((0,), (0,)), ((), ())),
                               preferred_element_type=F32)

        @pl.when(k == 0)
        def _():
            o_ref[...] = part

        @pl.when(k > 0)
        def _():
            o_ref[...] += part

    if exchange is None:
        return pl.pallas_call(
            body, name=name, grid=grid, in_specs=[a_spec, b_spec], out_specs=out_spec, out_shape=out_shape,
            compiler_params=_params(("parallel", "parallel", "arbitrary")),
        )(a, b)
    ex_args, ex_specs, ex_out, ex_scratch = _exchange_operands(exchange)
    return pl.pallas_call(
        _carry_exchange(body, exchange, 0, 2, 1, grid), name=name, grid=grid,
        in_specs=[a_spec, b_spec] + ex_specs, out_specs=[out_spec] + ex_specs, out_shape=[out_shape] + ex_out,
        scratch_shapes=ex_scratch,
        compiler_params=_params(("arbitrary", "arbitrary", "arbitrary")),
    )(a, b, *ex_args)


def _mm_tn(a, b, *, name, tm, tn, tk):
    T, M = a.shape
    N = b.shape[1]
    assert M % tm == 0 and N % tn == 0 and T % tk == 0
    return _mm_tn_call(a, b, pl.BlockSpec((tk, tm), lambda i, j, k: (k, i)),
                       pl.BlockSpec((tk, tn), lambda i, j, k: (k, j)),
                       out_shape=jax.ShapeDtypeStruct((M, N), F32),
                       out_spec=pl.BlockSpec((tm, tn), lambda i, j, k: (i, j)),
                       grid=(M // tm, N // tn, T // tk), name=name)


def _ffn_up(hb, w_up8, *, name, tm, exchange=None):
    T = hb.shape[0]

    def body(h_ref, wg_ref, wu_ref, gu_ref, a_ref):
        h = h_ref[...]
        g = jnp.dot(h, wg_ref[...], preferred_element_type=F32)
        u = jnp.dot(h, wu_ref[...], preferred_element_type=F32)
        sg = jax.nn.sigmoid(g)
        silu = g * sg
        gu_ref[0] = (u * (sg * (1.0 + g * (1.0 - sg)))).astype(BF16)
        gu_ref[1] = silu.astype(BF16)
        a_ref[...] = (silu * u).astype(BF16)

    grid = (FF_HALF_BLOCKS, T // tm)
    ex_args, ex_specs, ex_out, ex_scratch = _exchange_operands(exchange)
    return pl.pallas_call(
        _carry_exchange(body, exchange, 0, 3, 2, grid), name=name, grid=grid,
        in_specs=[pl.BlockSpec((tm, D_MODEL), lambda j, i: (i, 0)),
                  pl.BlockSpec((None, D_MODEL, FF_BLK), lambda j, i: (j, 0, 0)),
                  pl.BlockSpec((None, D_MODEL, FF_BLK), lambda j, i: (j + FF_HALF_BLOCKS, 0, 0))] + ex_specs,
        out_specs=[pl.BlockSpec((2, None, tm, FF_BLK), lambda j, i: (0, j, i, 0)),
                   pl.BlockSpec((None, tm, FF_BLK), lambda j, i: (j, i, 0))] + ex_specs,
        out_shape=[jax.ShapeDtypeStruct((2, FF_HALF_BLOCKS, T, FF_BLK), BF16),
                   jax.ShapeDtypeStruct((FF_HALF_BLOCKS, T, FF_BLK), BF16)] + ex_out,
        scratch_shapes=ex_scratch,
        compiler_params=_params(("arbitrary", "arbitrary") if exchange else ("parallel", "parallel")),
    )(hb, w_up8, w_up8, *ex_args)


def _mm_res_ln(a, w, res, g, b, *, scale, name, tm, exchange=None):
    split = a.ndim == 3
    if split:
        S, T, Ks = a.shape
        K = S * Ks
    else:
        T, K = a.shape

    def body(a_ref, w_ref, res_ref, g_ref, b_ref, r_ref, y_ref, yb_ref):
        if split:
            f = jnp.dot(a_ref[0], w_ref[0:Ks, :], preferred_element_type=F32)
            for s in range(1, S):
                f = f + jnp.dot(a_ref[s], w_ref[s * Ks:(s + 1) * Ks, :], preferred_element_type=F32)
        else:
            f = jnp.dot(a_ref[...], w_ref[...], preferred_element_type=F32)
        r = ALPHA * res_ref[...] + scale * f
        mu = jnp.mean(r, axis=-1, keepdims=True)
        xc = r - mu
        var = jnp.mean(xc * xc, axis=-1, keepdims=True)
        y = xc * lax.rsqrt(var + LN_EPS) * g_ref[...] + b_ref[...]
        r_ref[...] = r
        y_ref[...] = y
        yb_ref[...] = y.astype(BF16)

    row = pl.BlockSpec((tm, D_MODEL), lambda i: (i, 0))
    vec = pl.BlockSpec((1, D_MODEL), lambda i: (0, 0))
    grid = (T // tm,)
    ex_args, ex_specs, ex_out, ex_scratch = _exchange_operands(exchange)
    return pl.pallas_call(
        _carry_exchange(body, exchange, 0, 5, 3, grid), name=name, grid=grid,
        in_specs=[pl.BlockSpec((S, tm, Ks), lambda i: (0, i, 0)) if split else pl.BlockSpec((tm, K), lambda i: (i, 0)),
                  pl.BlockSpec((K, D_MODEL), lambda i: (0, 0)), row, vec, vec] + ex_specs,
        out_specs=[row, row, row] + ex_specs,
        out_shape=[jax.ShapeDtypeStruct((T, D_MODEL), F32), jax.ShapeDtypeStruct((T, D_MODEL), F32),
                   jax.ShapeDtypeStruct((T, D_MODEL), BF16)] + ex_out,
        scratch_shapes=ex_scratch,
        compiler_params=_params(("arbitrary",) if exchange else ("parallel",)),
    )(a, w, res, g, b, *ex_args)


def _conv_fwd(z, conv_w8, *, name, tm):
    T = z.shape[0]
    hb = tm // 8

    def body(b_ref, c_ref, h_ref, cp_ref, hp_ref, w_ref, y_ref):
        i = pl.program_id(0)
        u = c_ref[...] * h_ref[...]
        up = jnp.where(i > 0, cp_ref[...] * hp_ref[...], 0.0)
        ue = jnp.concatenate([up, u], axis=0)
        s1 = pltpu.roll(ue, 1, 0)[8:]
        s2 = pltpu.roll(ue, 2, 0)[8:]
        w = w_ref[...]
        conv = w[0:1] * s2 + w[1:2] * s1 + w[2:3] * u
        y_ref[...] = (b_ref[...] * conv).astype(BF16)

    def col(c):
        return pl.BlockSpec((tm, D_CONV), lambda i: (i, c))

    def prev(c):
        return pl.BlockSpec((8, D_CONV), lambda i: (jnp.maximum(i * hb - 1, 0), c))

    return pl.pallas_call(
        body, name=name, grid=(T // tm,),
        in_specs=[col(0), col(1), col(2), prev(1), prev(2), pl.BlockSpec((8, D_CONV), lambda i: (0, 0))],
        out_specs=pl.BlockSpec((tm, D_CONV), lambda i: (i, 0)),
        out_shape=jax.ShapeDtypeStruct((T, D_CONV), BF16),
        compiler_params=_params(("parallel",)),
    )(z, z, z, z, z, conv_w8)


def _rope(x, c, s1, s2):
    n = x.shape[-1]
    return x * c + pltpu.roll(x, 16, 1) * s1 + pltpu.roll(x, n - 16, 1) * s2


def _rope_t(d, c, s1, s2):
    n = d.shape[-1]
    return d * c + pltpu.roll(d * s1, n - 16, 1) + pltpu.roll(d * s2, 16, 1)


def _rms(x, g):
    rstd = lax.rsqrt(jnp.mean(x * x, axis=-1, keepdims=True) + RMS_EPS)
    return x * rstd * g


def _qkv_proj(z, gq, gkv, wq, wk, wv_ext, tabs, *, name, tm):
    T = z.shape[0]

    def body(z_ref, gq_ref, gkv_ref, wq_ref, wk_ref, wv_ref, c_ref, s1_ref, s2_ref,
             q_ref, k_ref, v_ref, qn_ref, kin_ref):
        zz = z_ref[...]
        qn = _rms(zz[:, :Q_LORA], gq_ref[...]).astype(BF16)
        kvn = _rms(zz[:, Q_LORA:Q_LORA + KV_LORA], gkv_ref[...]).astype(BF16)
        kin = jnp.concatenate([kvn, zz[:, Q_LORA + KV_LORA:].astype(BF16)], axis=-1)
        c = jnp.tile(c_ref[...], (1, MLA_HEADS))
        s1 = jnp.tile(s1_ref[...], (1, MLA_HEADS))
        s2 = jnp.tile(s2_ref[...], (1, MLA_HEADS))
        qpre = jnp.dot(qn, wq_ref[...], preferred_element_type=F32)
        kpre = jnp.dot(kin, wk_ref[...], preferred_element_type=F32)
        q_ref[...] = (_rope(qpre, c, s1, s2) * (ATTN_SCALE * LOG2_E)).astype(BF16)
        k_ref[...] = _rope(kpre, c, s1, s2).astype(BF16)
        vv = jnp.dot(kvn, wv_ref[...], preferred_element_type=F32)
        lane = lax.broadcasted_iota(jnp.int32, vv.shape, 1)
        v_ref[...] = jnp.where((lane & (HEAD_PAD - 1)) < V_HEAD, vv, 1.0).astype(BF16)
        qn_ref[...] = qn
        kin_ref[...] = kin

    def full(shape):
        return pl.BlockSpec(shape, lambda i: (0, 0))

    def rows(w, c=0):
        return pl.BlockSpec((tm, w), lambda i: (i, c))

    return pl.pallas_call(
        body, name=name, grid=(T // tm,),
        in_specs=[rows(512, Z_MID // 512), full((1, Q_LORA)), full((1, KV_LORA)),
                  full((Q_LORA, D_QK)), full((Q_LORA, D_QK)), full((KV_LORA, D_QK)),
                  rows(LANE), rows(LANE), rows(LANE)],
        out_specs=[rows(D_QK), rows(D_QK), rows(D_QK), rows(Q_LORA), rows(Q_LORA)],
        out_shape=[jax.ShapeDtypeStruct((T, D_QK), BF16), jax.ShapeDtypeStruct((T, D_QK), BF16),
                   jax.ShapeDtypeStruct((T, D_QK), BF16), jax.ShapeDtypeStruct((T, Q_LORA), BF16),
                   jax.ShapeDtypeStruct((T, Q_LORA), BF16)],
        compiler_params=_params(("parallel",)),
    )(z, gq, gkv, wq, wk, wv_ext, *tabs)


SOFTMAX_ROWS = 32
LOG2_E = 1.4426950408889634
_NT = (((1,), (1,)), ((), ()))
_TN = (((0,), (0,)), ((), ()))


def _diag_mask(s, row0, col0=0):
    row = lax.broadcasted_iota(jnp.int32, s.shape, 0) + row0
    col = lax.broadcasted_iota(jnp.int32, s.shape, 1) + col0
    return jnp.where(col <= row, s, NEG_INF)


_RELATIONS = tuple((rx, ry, rc) for rx in (0, 1) for ry in (0, 1) for rc in (0, 1))[1:]


class _Exchange:
    def __init__(self, kind, arrays):
        assert kind in ("gather", "scatter")
        self.kind, self.arrays, self.n = kind, list(arrays), len(arrays)

    def out_shapes(self):
        if self.kind == "gather":
            return [jax.ShapeDtypeStruct((N_DEV,) + a.shape, a.dtype) for a in self.arrays]
        return [jax.ShapeDtypeStruct((N_DEV - 1,) + a.shape[1:], a.dtype) for a in self.arrays]

    def scratch_shapes(self):
        sems = [pltpu.SemaphoreType.DMA((7 * self.n,)), pltpu.SemaphoreType.DMA((7 * self.n,))]
        if self.kind == "gather":
            sems.append(pltpu.SemaphoreType.DMA((self.n,)))
        return sems

    def _copies(self, src_refs, out_refs, sems):
        x, y, c = lax.axis_index("x"), lax.axis_index("y"), lax.axis_index("c")
        me = 4 * x + 2 * y + c
        sends, recvs, local = [], [], []
        for a in range(self.n):
            for k, rel in enumerate(_RELATIONS):
                peer = tuple((1 - p) if r else p for p, r in zip((x, y, c), rel))
                peer_index = 4 * peer[0] + 2 * peer[1] + peer[2]
                if self.kind == "gather":
                    src, lands_there, lands_here = src_refs[a], out_refs[a].at[me], out_refs[a].at[peer_index]
                else:
                    src, lands_there, lands_here = src_refs[a].at[peer_index], out_refs[a].at[k], out_refs[a].at[k]
                for dst, group in ((lands_there, sends), (lands_here, recvs)):
                    group.append(pltpu.make_async_remote_copy(
                        src_ref=src, dst_ref=dst, send_sem=sems[0].at[7 * a + k], recv_sem=sems[1].at[7 * a + k],
                        device_id=peer, device_id_type=_MESH_ID))
            if self.kind == "gather":
                local.append(pltpu.make_async_copy(src_refs[a], out_refs[a].at[me], sems[2].at[a]))
        return sends, recvs, local

    def start(self, src_refs, out_refs, sems):
        sends, _, local = self._copies(src_refs, out_refs, sems)
        for cp in local + sends:
            cp.start()

    def wait(self, src_refs, out_refs, sems):
        sends, recvs, local = self._copies(src_refs, out_refs, sems)
        for cp in recvs:
            cp.wait_recv()
        for cp in sends:
            cp.wait_send()
        for cp in local:
            cp.wait()


def _exchange_operands(exchange):
    if exchange is None:
        return [], [], [], []
    return exchange.arrays, [_ANY] * exchange.n, exchange.out_shapes(), exchange.scratch_shapes()


def _carry_exchange(body, exchange, n_prefetch, n_in, n_out, grid):
    if exchange is None:
        return body
    n = exchange.n
    last = tuple(g - 1 for g in grid)

    def wrapped(*refs):
        head = refs[:n_prefetch + n_in]
        src_refs = refs[n_prefetch + n_in:n_prefetch + n_in + n]
        rest = refs[n_prefetch + n_in + n:]
        outs, out_refs, rest = rest[:n_out], rest[n_out:n_out + n], rest[n_out + n:]
        n_sems = len(exchange.scratch_shapes())
        scratch, sems = rest[:len(rest) - n_sems], rest[len(rest) - n_sems:]
        at_first = functools.reduce(jnp.logical_and, [pl.program_id(d) == 0 for d in range(len(grid))])
        at_last = functools.reduce(jnp.logical_and, [pl.program_id(d) == last[d] for d in range(len(grid))])

        @pl.when(at_first)
        def _():
            exchange.start(src_refs, out_refs, sems)

        body(*head, *outs, *scratch)

        @pl.when(at_last)
        def _():
            exchange.wait(src_refs, out_refs, sems)

    return wrapped


def _attn_fwd(q, k, v, *, name, blk, exchange=None):
    T = q.shape[0]
    n = T // blk
    hp = FWD_HEADS_PER_STEP
    qi = np.array([i for i in range(n) for j in range(i + 1)], np.int32)
    kj = np.array([j for i in range(n) for j in range(i + 1)], np.int32)

    rc = _tile(blk, (SOFTMAX_ROWS,))

    def body(qi_ref, kj_ref, q_ref, k_ref, v_ref, o_ref, lse_ref, m_sc, acc_sc, s_sc, p_sc, red_sc):
        s_id = pl.program_id(1)
        i = qi_ref[s_id]
        j = kj_ref[s_id]

        @pl.when(j == 0)
        def _():
            m_sc[...] = jnp.full(m_sc.shape, NEG_INF, F32)
            acc_sc[...] = jnp.zeros(acc_sc.shape, F32)

        def head_step(hh, diagonal):
            hs = slice(hh * HEAD_PAD, (hh + 1) * HEAD_PAD)
            s_sc[hh] = lax.dot_general(q_ref[:, hs], k_ref[:, hs], _NT, preferred_element_type=F32)
            lanes = [slice(t * LANE, (t + 1) * LANE) for t in range(blk // LANE)]
            for r in range(blk // rc):
                rows = slice(r * rc, (r + 1) * rc)
                s = s_sc[hh, rows, :]
                if diagonal:
                    s = _diag_mask(s, r * rc)
                    s_sc[hh, rows, :] = s
                pm = s[:, lanes[0]]
                for t in lanes[1:]:
                    pm = jnp.maximum(pm, s[:, t])
                red_sc[hh, rows, :] = pm
            m_old = m_sc[hh]
            row_max = jnp.max(red_sc[hh], axis=-1, keepdims=True)
            m_new = jnp.maximum(m_old, jnp.broadcast_to(row_max, (blk, LANE)))
            a = jnp.exp2(m_old - m_new)
            m_sc[hh] = m_new
            for r in range(blk // rc):
                rows = slice(r * rc, (r + 1) * rc)
                mb = m_sc[hh, rows, :]
                for t in lanes:
                    p_sc[hh, rows, t] = jnp.exp2(s_sc[hh, rows, t] - mb).astype(BF16)
            acc_sc[hh] = a * acc_sc[hh] + jnp.dot(p_sc[hh], v_ref[:, hs], preferred_element_type=F32)

        @pl.when(j < i)
        def _():
            for hh in range(hp):
                head_step(hh, False)

        @pl.when(j == i)
        def _():
            for hh in range(hp):
                head_step(hh, True)
            for hh in range(hp):
                acc = acc_sc[hh]
                swapped = pltpu.roll(acc, V_HEAD, 1)
                o_ref[:, hh * V_HEAD:(hh + 1) * V_HEAD] = (acc / swapped)[:, :V_HEAD].astype(BF16)
                lane = lax.broadcasted_iota(jnp.int32, acc.shape, 1)
                denom = jnp.where(lane < V_HEAD, swapped, acc)
                lse_ref[hh] = m_sc[hh] + jnp.log(denom) * LOG2_E

    grid = (MLA_HEADS // hp, len(qi))
    ex_args, ex_specs, ex_out, ex_scratch = _exchange_operands(exchange)
    grid_spec = pltpu.PrefetchScalarGridSpec(
        num_scalar_prefetch=2, grid=grid,
        in_specs=[pl.BlockSpec((blk, hp * HEAD_PAD), lambda g, s, qi, kj: (qi[s], g)),
                  pl.BlockSpec((blk, hp * HEAD_PAD), lambda g, s, qi, kj: (kj[s], g)),
                  pl.BlockSpec((blk, hp * HEAD_PAD), lambda g, s, qi, kj: (kj[s], g))] + ex_specs,
        out_specs=[pl.BlockSpec((blk, hp * V_HEAD), lambda g, s, qi, kj: (qi[s], g)),
                   pl.BlockSpec((hp, blk, LANE), lambda g, s, qi, kj: (g, qi[s], 0))] + ex_specs,
        scratch_shapes=[pltpu.VMEM((hp, blk, LANE), F32), pltpu.VMEM((hp, blk, HEAD_PAD), F32),
                        pltpu.VMEM((hp, blk, blk), F32), pltpu.VMEM((hp, blk, blk), BF16),
                        pltpu.VMEM((hp, blk, LANE), F32)] + ex_scratch)
    return pl.pallas_call(
        _carry_exchange(body, exchange, 2, 3, 2, grid), name=name, grid_spec=grid_spec,
        out_shape=[jax.ShapeDtypeStruct((T, D_MLA), BF16),
                   jax.ShapeDtypeStruct((MLA_HEADS, T, LANE), F32)] + ex_out,
        compiler_params=_params(("arbitrary", "arbitrary") if exchange else ("parallel", "arbitrary")),
    )(jnp.asarray(qi), jnp.asarray(kj), q, k, v, *ex_args)


def _merge(yc, ym, wbc, wbm, z, bg, *, name, tm):
    T = yc.shape[0]

    def body(yc_ref, ym_ref, wbc_ref, wbm_ref, gc_ref, gm_ref, bg_ref, mg_ref, pa_ref, pb_ref):
        pa = jnp.dot(yc_ref[...], wbc_ref[...], preferred_element_type=F32)
        pb = jnp.dot(ym_ref[...], wbm_ref[...], preferred_element_type=F32)
        bgv = bg_ref[...]
        sa = jax.nn.sigmoid(gc_ref[...] + bgv[0:1])
        sb = jax.nn.sigmoid(gm_ref[...] + bgv[1:2])
        mg_ref[...] = (sa * pa + sb * pb).astype(BF16)
        pa_ref[...] = pa.astype(BF16)
        pb_ref[...] = pb.astype(BF16)

    row = pl.BlockSpec((tm, D_MODEL), lambda i: (i, 0))
    return pl.pallas_call(
        body, name=name, grid=(T // tm,),
        in_specs=[pl.BlockSpec((tm, D_CONV), lambda i: (i, 0)), pl.BlockSpec((tm, D_MLA), lambda i: (i, 0)),
                  pl.BlockSpec((D_CONV, D_MODEL), lambda i: (0, 0)), pl.BlockSpec((D_MLA, D_MODEL), lambda i: (0, 0)),
                  pl.BlockSpec((tm, D_MODEL), lambda i: (i, Z_GATE // D_MODEL)),
                  pl.BlockSpec((tm, D_MODEL), lambda i: (i, Z_GATE // D_MODEL + 1)),
                  pl.BlockSpec((8, D_MODEL), lambda i: (0, 0))],
        out_specs=[row, row, row],
        out_shape=[jax.ShapeDtypeStruct((T, D_MODEL), BF16)] * 3,
        compiler_params=_params(("parallel",)),
    )(yc, ym, wbc, wbm, z, z, bg)


def _loss_head(h, tgt, *, t_real, name, tm):
    T = h.shape[0]

    def body(h_ref, t_ref, dy_ref, loss_ref):
        i = pl.program_id(0)
        row = lax.broadcasted_iota(jnp.int32, (tm, 1), 0) + i * tm
        valid = (row >= N_META) & (row < t_real)
        err = jnp.where(valid, h_ref[...] - t_ref[...], 0.0)
        dy_ref[...] = err * (1.0 / D_MODEL)
        part = 0.5 * jnp.sum(jnp.sum(err * err, axis=-1, keepdims=True) * (1.0 / D_MODEL), axis=0, keepdims=True)

        @pl.when(i == 0)
        def _():
            loss_ref[...] = jnp.zeros(loss_ref.shape, F32)

        loss_ref[...] += jnp.broadcast_to(part, loss_ref.shape)

    row_spec = pl.BlockSpec((tm, D_MODEL), lambda i: (i, 0))
    return pl.pallas_call(
        body, name=name, grid=(T // tm,),
        in_specs=[row_spec, row_spec],
        out_specs=[row_spec, pl.BlockSpec((8, LANE), lambda i: (0, 0))],
        out_shape=[jax.ShapeDtypeStruct((T, D_MODEL), F32), jax.ShapeDtypeStruct((8, LANE), F32)],
        compiler_params=_params(("arbitrary",)),
    )(h, tgt)


def _ln_bwd(dy, r, g, *, scale, name, tm):
    T = dy.shape[0]

    def body(dy_ref, r_ref, g_ref, dr_ref, drb_ref, dg_ref, db_ref):
        i = pl.program_id(0)
        rr = r_ref[...]
        dyv = dy_ref[...]
        mu = jnp.mean(rr, axis=-1, keepdims=True)
        xc = rr - mu
        rstd = lax.rsqrt(jnp.mean(xc * xc, axis=-1, keepdims=True) + LN_EPS)
        xh = xc * rstd
        dxh = dyv * g_ref[...]
        m1 = jnp.mean(dxh, axis=-1, keepdims=True)
        m2 = jnp.mean(dxh * xh, axis=-1, keepdims=True)
        dr = rstd * (dxh - m1 - xh * m2)
        dr_ref[...] = dr
        drb_ref[...] = (scale * dr).astype(BF16)

        @pl.when(i == 0)
        def _():
            dg_ref[...] = jnp.zeros(dg_ref.shape, F32)
            db_ref[...] = jnp.zeros(db_ref.shape, F32)

        dg_ref[0:1, :] += jnp.sum(dyv * xh, axis=0, keepdims=True)
        db_ref[0:1, :] += jnp.sum(dyv, axis=0, keepdims=True)

    row = pl.BlockSpec((tm, D_MODEL), lambda i: (i, 0))
    acc = pl.BlockSpec((8, D_MODEL), lambda i: (0, 0))
    return pl.pallas_call(
        body, name=name, grid=(T // tm,),
        in_specs=[row, row, pl.BlockSpec((1, D_MODEL), lambda i: (0, 0))],
        out_specs=[row, row, acc, acc],
        out_shape=[jax.ShapeDtypeStruct((T, D_MODEL), F32), jax.ShapeDtypeStruct((T, D_MODEL), BF16),
                   jax.ShapeDtypeStruct((8, D_MODEL), F32), jax.ShapeDtypeStruct((8, D_MODEL), F32)],
        compiler_params=_params(("arbitrary",)),
    )(dy, r, g)


def _ffn_bwd_mid(dfb, w_down, gu, *, name, tm, exchange=None):
    T = dfb.shape[0]

    def body(df_ref, w_ref, gu_ref, o_ref):
        da = lax.dot_general(df_ref[...], w_ref[...], _NT, preferred_element_type=F32)
        o_ref[0] = (da * gu_ref[0].astype(F32)).astype(BF16)
        o_ref[1] = (da * gu_ref[1].astype(F32)).astype(BF16)

    grid = (FF_HALF_BLOCKS, T // tm)
    ex_args, ex_specs, ex_out, ex_scratch = _exchange_operands(exchange)
    return pl.pallas_call(
        _carry_exchange(body, exchange, 0, 3, 1, grid), name=name, grid=grid,
        in_specs=[pl.BlockSpec((tm, D_MODEL), lambda j, i: (i, 0)),
                  pl.BlockSpec((FF_BLK, D_MODEL), lambda j, i: (j, 0)),
                  pl.BlockSpec((2, None, tm, FF_BLK), lambda j, i: (0, j, i, 0))] + ex_specs,
        out_specs=[pl.BlockSpec((2, None, tm, FF_BLK), lambda j, i: (0, j, i, 0))] + ex_specs,
        out_shape=[jax.ShapeDtypeStruct((2, FF_HALF_BLOCKS, T, FF_BLK), BF16)] + ex_out,
        scratch_shapes=ex_scratch,
        compiler_params=_params(("arbitrary", "arbitrary") if exchange else ("parallel", "parallel")),
    )(dfb, w_down, gu, *ex_args)


def _wo_bwd(dmb, wo_t, z, bg, pa, pb, *, name, tm):
    T = dmb.shape[0]

    def body(dm_ref, w_ref, gc_ref, gm_ref, bg_ref, pa_ref, pb_ref, dpa_ref, dpb_ref, dg_ref, dbg_ref):
        i = pl.program_id(0)
        dm = lax.dot_general(dm_ref[...], w_ref[...], _NT, preferred_element_type=F32)
        bgv = bg_ref[...]
        sa = jax.nn.sigmoid(gc_ref[...] + bgv[0:1])
        sb = jax.nn.sigmoid(gm_ref[...] + bgv[1:2])
        dpa_ref[...] = (dm * sa).astype(BF16)
        dpb_ref[...] = (dm * sb).astype(BF16)
        dga = dm * pa_ref[...].astype(F32) * (sa * (1.0 - sa))
        dgb = dm * pb_ref[...].astype(F32) * (sb * (1.0 - sb))
        dg_ref[:, :D_MODEL] = dga.astype(BF16)
        dg_ref[:, D_MODEL:] = dgb.astype(BF16)

        @pl.when(i == 0)
        def _():
            dbg_ref[...] = jnp.zeros(dbg_ref.shape, F32)

        dbg_ref[0:1, :] += jnp.sum(dga, axis=0, keepdims=True)
        dbg_ref[1:2, :] += jnp.sum(dgb, axis=0, keepdims=True)

    row = pl.BlockSpec((tm, D_MODEL), lambda i: (i, 0))
    return pl.pallas_call(
        body, name=name, grid=(T // tm,),
        in_specs=[row, pl.BlockSpec((D_MODEL, D_MODEL), lambda i: (0, 0)),
                  pl.BlockSpec((tm, D_MODEL), lambda i: (i, Z_GATE // D_MODEL)),
                  pl.BlockSpec((tm, D_MODEL), lambda i: (i, Z_GATE // D_MODEL + 1)),
                  pl.BlockSpec((8, D_MODEL), lambda i: (0, 0)), row, row],
        out_specs=[row, row, pl.BlockSpec((tm, 2 * D_MODEL), lambda i: (i, 0)),
                   pl.BlockSpec((8, D_MODEL), lambda i: (0, 0))],
        out_shape=[jax.ShapeDtypeStruct((T, D_MODEL), BF16), jax.ShapeDtypeStruct((T, D_MODEL), BF16),
                   jax.ShapeDtypeStruct((T, 2 * D_MODEL), BF16), jax.ShapeDtypeStruct((8, D_MODEL), F32)],
        compiler_params=_params(("arbitrary",)),
    )(dmb, wo_t, z, z, bg, pa, pb)


def _conv_bwd(dy, z, conv_w8, *, name, tm):
    T = dy.shape[0]
    n = T // tm
    hb = tm // 8

    def body(dy_ref, b_ref, c_ref, h_ref, cp_ref, hp_ref, dyn_ref, bn_ref, w_ref, dz_ref, dw_ref):
        i = pl.program_id(0)
        u = c_ref[...] * h_ref[...]
        up = jnp.where(i > 0, cp_ref[...] * hp_ref[...], 0.0)
        ue = jnp.concatenate([up, u], axis=0)
        s1 = pltpu.roll(ue, 1, 0)[8:]
        s2 = pltpu.roll(ue, 2, 0)[8:]
        w = w_ref[...]
        conv = w[0:1] * s2 + w[1:2] * s1 + w[2:3] * u
        dyv = dy_ref[...]
        e = dyv * b_ref[...]
        en = jnp.where(i < n - 1, dyn_ref[...] * bn_ref[...], 0.0)
        ee = jnp.concatenate([e, en], axis=0)
        e1 = pltpu.roll(ee, tm + 8 - 1, 0)[:tm]
        e2 = pltpu.roll(ee, tm + 8 - 2, 0)[:tm]
        du = w[2:3] * e + w[1:2] * e1 + w[0:1] * e2
        dz_ref[:, 0:D_CONV] = (dyv * conv).astype(BF16)
        dz_ref[:, D_CONV:2 * D_CONV] = (du * h_ref[...]).astype(BF16)
        dz_ref[:, 2 * D_CONV:] = (du * c_ref[...]).astype(BF16)

        @pl.when(i == 0)
        def _():
            dw_ref[...] = jnp.zeros(dw_ref.shape, F32)

        dw_ref[0:1, :] += jnp.sum(e * s2, axis=0, keepdims=True)
        dw_ref[1:2, :] += jnp.sum(e * s1, axis=0, keepdims=True)
        dw_ref[2:3, :] += jnp.sum(e * u, axis=0, keepdims=True)

    def col(c):
        return pl.BlockSpec((tm, D_CONV), lambda i: (i, c))

    def prev(c):
        return pl.BlockSpec((8, D_CONV), lambda i: (jnp.maximum(i * hb - 1, 0), c))

    def nxt(c):
        return pl.BlockSpec((8, D_CONV), lambda i: (jnp.minimum((i + 1) * hb, T // 8 - 1), c))

    return pl.pallas_call(
        body, name=name, grid=(n,),
        in_specs=[col(0), col(0), col(1), col(2), prev(1), prev(2), nxt(0), nxt(0),
                  pl.BlockSpec((8, D_CONV), lambda i: (0, 0))],
        out_specs=[pl.BlockSpec((tm, 3 * D_CONV), lambda i: (i, 0)), pl.BlockSpec((8, D_CONV), lambda i: (0, 0))],
        out_shape=[jax.ShapeDtypeStruct((T, 3 * D_CONV), BF16), jax.ShapeDtypeStruct((8, D_CONV), F32)],
        compiler_params=_params(("arbitrary",)),
    )(dy, z, z, z, z, z, dy, z, conv_w8)


def _attn_bwd(q, k, v, o, do, lse, *, name, blk, exchange=None):
    T = q.shape[0]
    n = T // blk
    hp = HEADS_PER_STEP
    qi = np.array([i for j in range(n) for i in range(j, n)], np.int32)
    kj = np.array([j for j in range(n) for i in range(j, n)], np.int32)

    rc = _tile(blk, (SOFTMAX_ROWS,))

    def body(qi_ref, kj_ref, q_ref, k_ref, v_ref, o_ref, do_ref, lse_ref, dq_ref, dk_ref, dv_ref,
             dk_sc, dv_sc, s_sc, dp_sc, p_sc, ds_sc, delta_sc):
        s_id = pl.program_id(1)
        i = qi_ref[s_id]
        j = kj_ref[s_id]

        @pl.when(s_id == 0)
        def _():
            dq_ref[...] = jnp.zeros(dq_ref.shape, F32)

        @pl.when(i == j)
        def _():
            dk_sc[...] = jnp.zeros(dk_sc.shape, F32)
            dv_sc[...] = jnp.zeros(dv_sc.shape, F32)

        q_rows = pl.ds(pl.multiple_of(i * blk, blk), blk)

        def head_step(hh, diagonal):
            hs = slice(hh * HEAD_PAD, (hh + 1) * HEAD_PAD)
            vs = slice(hh * V_HEAD, (hh + 1) * V_HEAD)
            qh = q_ref[:, hs]
            kh = k_ref[:, hs]
            doh = do_ref[:, vs]
            s_sc[hh] = lax.dot_general(qh, kh, _NT, preferred_element_type=F32)
            vh = v_ref[:, hh * HEAD_PAD:hh * HEAD_PAD + V_HEAD]
            dp_sc[hh] = lax.dot_general(doh, vh, _NT, preferred_element_type=F32)
            delta = jnp.sum(doh.astype(F32) * o_ref[:, vs].astype(F32), axis=-1, keepdims=True)
            delta_sc[hh] = jnp.broadcast_to(delta, (blk, LANE))
            for r in range(blk // rc):
                rows = slice(r * rc, (r + 1) * rc)
                lse = lse_ref[hh, rows, :]
                dl = delta_sc[hh, rows, :]
                for t in range(blk // LANE):
                    cols = slice(t * LANE, (t + 1) * LANE)
                    s = s_sc[hh, rows, cols]
                    if diagonal:
                        s = _diag_mask(s, r * rc, t * LANE)
                    p = jnp.exp2(s - lse)
                    p_sc[hh, rows, cols] = p.astype(BF16)
                    ds_sc[hh, rows, cols] = (p * (dp_sc[hh, rows, cols] - dl)).astype(BF16)
            dv_sc[hh] += lax.dot_general(p_sc[hh], doh, _TN, preferred_element_type=F32)
            dk_sc[hh] += lax.dot_general(ds_sc[hh], qh, _TN, preferred_element_type=F32)
            dq_ref[q_rows, hs] += jnp.dot(ds_sc[hh], kh, preferred_element_type=F32)

        @pl.when(j < i)
        def _():
            for hh in range(hp):
                head_step(hh, False)

        @pl.when(j == i)
        def _():
            for hh in range(hp):
                head_step(hh, True)

        @pl.when(i == n - 1)
        def _():
            for hh in range(hp):
                dk_ref[:, hh * HEAD_PAD:(hh + 1) * HEAD_PAD] = dk_sc[hh] * (1.0 / LOG2_E)
                dv_ref[:, hh * V_HEAD:(hh + 1) * V_HEAD] = dv_sc[hh]

    wq = hp * HEAD_PAD
    wv = hp * V_HEAD
    grid = (MLA_HEADS // hp, len(qi))
    ex_args, ex_specs, ex_out, ex_scratch = _exchange_operands(exchange)
    grid_spec = pltpu.PrefetchScalarGridSpec(
        num_scalar_prefetch=2, grid=grid,
        in_specs=[pl.BlockSpec((blk, wq), lambda g, s, qi, kj: (qi[s], g)),
                  pl.BlockSpec((blk, wq), lambda g, s, qi, kj: (kj[s], g)),
                  pl.BlockSpec((blk, wq), lambda g, s, qi, kj: (kj[s], g)),
                  pl.BlockSpec((blk, wv), lambda g, s, qi, kj: (qi[s], g)),
                  pl.BlockSpec((blk, wv), lambda g, s, qi, kj: (qi[s], g)),
                  pl.BlockSpec((hp, blk, LANE), lambda g, s, qi, kj: (g, qi[s], 0))] + ex_specs,
        out_specs=[pl.BlockSpec((T, wq), lambda g, s, qi, kj: (0, g), pipeline_mode=pl.Buffered(1)),
                   pl.BlockSpec((blk, wq), lambda g, s, qi, kj: (kj[s], g)),
                   pl.BlockSpec((blk, wv), lambda g, s, qi, kj: (kj[s], g))] + ex_specs,
        scratch_shapes=[pltpu.VMEM((hp, blk, HEAD_PAD), F32), pltpu.VMEM((hp, blk, V_HEAD), F32),
                        pltpu.VMEM((hp, blk, blk), F32), pltpu.VMEM((hp, blk, blk), F32),
                        pltpu.VMEM((hp, blk, blk), BF16), pltpu.VMEM((hp, blk, blk), BF16),
                        pltpu.VMEM((hp, blk, LANE), F32)] + ex_scratch)
    return pl.pallas_call(
        _carry_exchange(body, exchange, 2, 6, 3, grid), name=name, grid_spec=grid_spec,
        out_shape=[jax.ShapeDtypeStruct((T, D_QK), F32), jax.ShapeDtypeStruct((T, D_QK), F32),
                   jax.ShapeDtypeStruct((T, D_MLA), F32)] + ex_out,
        compiler_params=_params(("arbitrary", "arbitrary") if exchange else ("parallel", "arbitrary")),
    )(jnp.asarray(qi), jnp.asarray(kj), q, k, v, o, do, lse, *ex_args)


def _qkv_bwd(dq, dk, dv, z, gq, gkv, wq_t, wk_t, wv_t, tabs, *, name, tm):
    T = dq.shape[0]

    def body(dq_ref, dk_ref, dv_ref, z_ref, gq_ref, gkv_ref, wq_ref, wk_ref, wv_ref, c_ref, s1_ref, s2_ref,
             dz_ref, dqb_ref, dkb_ref, dvb_ref, dgq_ref, dgkv_ref):
        i = pl.program_id(0)
        c = jnp.tile(c_ref[...], (1, MLA_HEADS))
        s1 = jnp.tile(s1_ref[...], (1, MLA_HEADS))
        s2 = jnp.tile(s2_ref[...], (1, MLA_HEADS))
        dqp = _rope_t(dq_ref[...] * ATTN_SCALE, c, s1, s2).astype(BF16)
        dkp = _rope_t(dk_ref[...], c, s1, s2).astype(BF16)
        dvb = dv_ref[...].astype(BF16)
        dqb_ref[...] = dqp
        dkb_ref[...] = dkp
        dvb_ref[...] = dvb
        dqn = lax.dot_general(dqp, wq_ref[...], _NT, preferred_element_type=F32)
        dkin = lax.dot_general(dkp, wk_ref[...], _NT, preferred_element_type=F32)
        dkvn = dkin[:, :KV_LORA] + lax.dot_general(dvb, wv_ref[...], _NT, preferred_element_type=F32)
        zz = z_ref[...]

        def rms_bwd(x, g, dy):
            rstd = lax.rsqrt(jnp.mean(x * x, axis=-1, keepdims=True) + RMS_EPS)
            xh = x * rstd
            dxh = dy * g
            dx = rstd * (dxh - xh * jnp.mean(dxh * xh, axis=-1, keepdims=True))
            return dx, jnp.sum(dy * xh, axis=0, keepdims=True)

        dcq, dgq = rms_bwd(zz[:, :Q_LORA], gq_ref[...], dqn)
        dckv, dgkv = rms_bwd(zz[:, Q_LORA:Q_LORA + KV_LORA], gkv_ref[...], dkvn)
        dz_ref[:, :Q_LORA] = dcq.astype(BF16)
        dz_ref[:, Q_LORA:Q_LORA + KV_LORA] = dckv.astype(BF16)
        dz_ref[:, Q_LORA + KV_LORA:] = dkin[:, KV_LORA:].astype(BF16)

        @pl.when(i == 0)
        def _():
            dgq_ref[...] = jnp.zeros(dgq_ref.shape, F32)
            dgkv_ref[...] = jnp.zeros(dgkv_ref.shape, F32)

        dgq_ref[0:1, :] += dgq
        dgkv_ref[0:1, :] += dgkv

    def full(shape):
        return pl.BlockSpec(shape, lambda i: (0, 0))

    def rows(w, c=0):
        return pl.BlockSpec((tm, w), lambda i: (i, c))

    return pl.pallas_call(
        body, name=name, grid=(T // tm,),
        in_specs=[rows(D_QK), rows(D_QK), rows(D_MLA), rows(512, Z_MID // 512),
                  full((1, Q_LORA)), full((1, KV_LORA)),
                  full((Q_LORA, D_QK)), full((Q_LORA, D_QK)), full((KV_LORA, D_MLA)),
                  rows(LANE), rows(LANE), rows(LANE)],
        out_specs=[rows(512), rows(D_QK), rows(D_QK), rows(D_MLA), full((8, Q_LORA)), full((8, KV_LORA))],
        out_shape=[jax.ShapeDtypeStruct((T, 512), BF16), jax.ShapeDtypeStruct((T, D_QK), BF16),
                   jax.ShapeDtypeStruct((T, D_QK), BF16), jax.ShapeDtypeStruct((T, D_MLA), BF16),
                   jax.ShapeDtypeStruct((8, Q_LORA), F32), jax.ShapeDtypeStruct((8, KV_LORA), F32)],
        compiler_params=_params(("arbitrary",)),
    )(dq, dk, dv, z, gq, gkv, wq_t, wk_t, wv_t, *tabs)


FLAT_W = 1024


ELEMENTWISE_TILE_BYTES = 768 * 1024


def _row_tile(R, C):
    width = -(-C // LANE) * LANE * 4
    best = None
    for t in range(16, R + 1, 16):
        if R % t == 0 and t * width <= ELEMENTWISE_TILE_BYTES:
            best = t
    if best is None:
        best = R
    return best


def _adamw(w, m, v, layer, parts, part_index, *, name):
    _, R, C = w.shape
    tr = _row_tile(R, C)
    bc1 = 1.0 - ADAM_B1 ** ADAM_STEP
    bc2 = 1.0 - ADAM_B2 ** ADAM_STEP
    n_parts = len(parts)

    def body(idx_ref, w_ref, m_ref, v_ref, *refs):
        g_refs = refs[:n_parts]
        g_out, d_out, m_out, v_out = refs[n_parts:]
        g = g_refs[0][...].astype(F32)
        for r in g_refs[1:]:
            g = g + r[...].astype(F32)
        wv = w_ref[...]
        mn = ADAM_B1 * m_ref[...] + (1.0 - ADAM_B1) * g
        vn = ADAM_B2 * v_ref[...] + (1.0 - ADAM_B2) * (g * g)
        m_hat = mn / bc1
        v_hat = vn / bc2
        g_out[...] = g
        d_out[...] = -ADAM_LR * (m_hat / (jnp.sqrt(v_hat) + ADAM_EPS) + ADAM_WD * wv)
        m_out[...] = mn
        v_out[...] = vn

    layer_row = pl.BlockSpec((None, tr, C), lambda i, idx: (layer, i, 0))
    in_specs = [layer_row, layer_row, layer_row]
    args = [w, m, v]
    for arr, slot in parts:
        if slot is None:
            in_specs.append(pl.BlockSpec((None, tr, C), lambda i, idx: (idx[0], i, 0)))
        else:
            in_specs.append(pl.BlockSpec((None, tr, C), lambda i, idx, slot=slot: (slot, i, 0)))
        args.append(arr)
    grid_spec = pltpu.PrefetchScalarGridSpec(
        num_scalar_prefetch=1, grid=(R // tr,), in_specs=in_specs,
        out_specs=[pl.BlockSpec((tr, C), lambda i, idx: (i, 0))] * 4)
    return pl.pallas_call(
        body, name=name, grid_spec=grid_spec,
        out_shape=[jax.ShapeDtypeStruct((R, C), F32)] * 4,
        compiler_params=_params(("parallel",)),
    )(part_index, *args)


def _sum8(parts, *, name):
    _, R, _ = parts.shape

    def body(p_ref, o_ref):
        acc = p_ref[0]
        for d in range(1, N_DEV):
            acc = acc + p_ref[d]
        o_ref[...] = acc

    return pl.pallas_call(
        body, name=name, grid=(1,),
        in_specs=[pl.BlockSpec((N_DEV, R, FLAT_W), lambda i: (0, 0, 0))],
        out_specs=pl.BlockSpec((R, FLAT_W), lambda i: (0, 0)),
        out_shape=jax.ShapeDtypeStruct((R, FLAT_W), F32),
        compiler_params=_params(("arbitrary",)),
    )(parts)


_MESH_ID = pl.DeviceIdType.MESH
_ANY = pl.BlockSpec(memory_space=pl.ANY)


def _all_gather(shards, *, name):
    n = len(shards)

    def body(*refs):
        x_refs, out_refs = refs[:n], refs[n:2 * n]
        send_sems, recv_sems, local_sems = refs[2 * n:]
        x, y, c = lax.axis_index("x"), lax.axis_index("y"), lax.axis_index("c")
        me, sibling = (x, y, c), (x, y, 1 - c)
        chips = [(1 - x, y), (x, 1 - y), (1 - x, 1 - y)]

        def blk(a, px, py, pc):
            return out_refs[a].at[4 * px + 2 * py + pc]

        def copy(a, k, block, to, src=None):
            return pltpu.make_async_remote_copy(
                src_ref=blk(a, *block) if src is None else src, dst_ref=blk(a, *block),
                send_sem=send_sems.at[7 * a + k], recv_sem=recv_sems.at[7 * a + k],
                device_id=to, device_id_type=_MESH_ID)

        mine = [pltpu.make_async_copy(x_refs[a], blk(a, *me), local_sems.at[a]) for a in range(n)]
        for cp in mine:
            cp.start()
        first = []
        for a in range(n):
            first.append(copy(a, 0, me, sibling, src=x_refs[a]))
            first += [copy(a, 1 + j, me, (*chip, c), src=x_refs[a]) for j, chip in enumerate(chips)]
        for cp in first:
            cp.start()
        passed = []
        for j, chip in enumerate(chips):
            for a in range(n):
                copy(a, 1 + j, (*chip, c), me).wait_recv()
                fwd = copy(a, 4 + j, (*chip, c), sibling)
                fwd.start()
                passed.append(fwd)
        for a in range(n):
            copy(a, 0, sibling, me).wait_recv()
        for j, chip in enumerate(chips):
            for a in range(n):
                copy(a, 4 + j, (*chip, 1 - c), me).wait_recv()
        for cp in first + passed:
            cp.wait_send()
        for cp in mine:
            cp.wait()

    return pl.pallas_call(
        body, name=name,
        out_shape=[jax.ShapeDtypeStruct((N_DEV,) + s.shape, s.dtype) for s in shards],
        in_specs=[_ANY] * n, out_specs=[_ANY] * n,
        scratch_shapes=[pltpu.SemaphoreType.DMA((7 * n,)), pltpu.SemaphoreType.DMA((7 * n,)),
                        pltpu.SemaphoreType.DMA((n,))],
    )(*shards)


_BIG = (("ffn1_w_up", 2), ("ffn1_w_down", 1), ("mix_w_in", 2), ("w_uq", 2), ("w_ukv", 2),
        ("w_br_conv", 2), ("w_br_mla", 2), ("w_o", 1), ("ffn2_w_up", 2), ("ffn2_w_down", 1))
_SMALL_SHARDED = (("meta_tokens", 1), ("mix_b_gate", 2), ("conv_w", 2), ("ln_g", 2), ("ln_b", 2))
_SMALL_REPL = ("q_norm_g", "kv_norm_g")


def _pack(arrs, dtype, row_align=8):
    flat = jnp.concatenate([a.reshape(-1).astype(dtype) for a in arrs])
    n = flat.shape[0]
    rows = -(-n // (row_align * FLAT_W)) * row_align
    return jnp.pad(flat, (0, rows * FLAT_W - n)).reshape(rows, FLAT_W)


def _unpack(flat, shapes):
    flat = flat.reshape(-1)
    out, off = [], 0
    for s in shapes:
        n = int(np.prod(s))
        out.append(flat[off:off + n].reshape(s))
        off += n
    return out


def _unpack_gathered(gathered, shapes, axes):
    g2 = gathered.reshape(N_DEV, -1)
    out, off = [], 0
    for s, ax in zip(shapes, axes):
        n = int(np.prod(s))
        blocks = g2[:, off:off + n].reshape((N_DEV,) + tuple(s))
        full = jnp.moveaxis(blocks, 0, ax)
        out.append(full.reshape(tuple(s[:ax]) + (N_DEV * s[ax],) + tuple(s[ax + 1:])))
        off += n
    return out


def _to_dest(full, ax):
    s = full.shape
    split = full.reshape(s[:ax] + (N_DEV, s[ax] // N_DEV) + s[ax + 1:])
    return jnp.moveaxis(split, ax, 0)


def _from_blocks(g, ax):
    full = jnp.moveaxis(g, 0, ax)
    s = full.shape
    return full.reshape(s[:ax] + (s[ax] * s[ax + 1],) + s[ax + 2:])


def _rope_tables(T):
    inv_freq = 1.0 / (ROPE_BASE ** (jnp.arange(0, QK_ROPE, 2, dtype=F32) / QK_ROPE))
    ang = jnp.arange(T, dtype=F32)[:, None] * inv_freq[None, :]
    cos, sin = jnp.cos(ang), jnp.sin(ang)
    half = QK_ROPE // 2
    ones = jnp.ones((T, QK_NOPE), F32)
    zeros = lambda w: jnp.zeros((T, w), F32)
    c = jnp.concatenate([ones, cos, cos, zeros(HEAD_PAD - QK_NOPE - QK_ROPE)], axis=1)
    s1 = jnp.concatenate([zeros(QK_NOPE + half), sin, zeros(HEAD_PAD - QK_NOPE - QK_ROPE)], axis=1)
    s2 = jnp.concatenate([zeros(QK_NOPE), -sin, zeros(HEAD_PAD - QK_NOPE - half)], axis=1)
    return c, s1, s2


_FFN1 = ("ffn1_w_up", "ffn1_w_down")
_MIXER = ("mix_w_in", "w_uq", "w_ukv", "w_br_conv", "w_br_mla", "w_o")
_FFN2 = ("ffn2_w_up", "ffn2_w_down")
_MIXER_EARLY = ("w_br_conv", "w_br_mla", "w_o")


def _mixer_weights(gathered):
    axes = dict(_BIG)
    W = {n: _from_blocks(gathered[n], axes[n] - 1) for n in _MIXER}
    w_in = W["mix_w_in"]
    w_in_p = jnp.concatenate([w_in[:, :Z_KR_END], jnp.zeros((D_MODEL, D_IN_PAD - D_IN_REAL), BF16),
                              w_in[:, Z_KR_END:]], axis=1)
    w_uq = W["w_uq"].reshape(Q_LORA, MLA_HEADS, QK_NOPE + QK_ROPE)
    wq = jnp.pad(w_uq, ((0, 0), (0, 0), (0, HEAD_PAD - QK_NOPE - QK_ROPE))).reshape(Q_LORA, D_QK)
    w_ukv = W["w_ukv"].reshape(KV_LORA, MLA_HEADS, QK_NOPE + V_HEAD)
    wk_top = jnp.pad(w_ukv[:, :, :QK_NOPE], ((0, 0), (0, 0), (0, HEAD_PAD - QK_NOPE))).reshape(KV_LORA, D_QK)
    place = np.zeros((Q_LORA - KV_LORA, MLA_HEADS, HEAD_PAD), np.float32)
    for r in range(QK_ROPE):
        place[r, :, QK_NOPE + r] = 1.0
    wk = jnp.concatenate([wk_top, jnp.asarray(place.reshape(Q_LORA - KV_LORA, D_QK), BF16)], axis=0)
    wv = w_ukv[:, :, QK_NOPE:].reshape(KV_LORA, D_MLA)
    wv_ext = jnp.pad(w_ukv[:, :, QK_NOPE:], ((0, 0), (0, 0), (0, HEAD_PAD - V_HEAD))).reshape(KV_LORA, D_QK)
    return dict(w_in=w_in_p, wq=wq, wk=wk, wv=wv, wv_ext=wv_ext,
                wbc=W["w_br_conv"], wbm=W["w_br_mla"], wo=W["w_o"])


def _row8(v):
    return jnp.pad(v, ((0, 8 - v.shape[0]), (0, 0)))


def _local_step(x, tgt, gathered_up1, w_blocks, S):
    big_names = [n for n, _ in _BIG]
    axes = dict(_BIG)
    t_real = N_META + x.shape[0]
    T = -(-t_real // ROW_ALIGN) * ROW_ALIGN
    pad = T - t_real
    tm = _tile(T, (384, 256, 128))
    tms = _tile(T, (768, 256, 128))
    blk = _tile(T, (768, 256, 128))
    tabs = _rope_tables(T)

    h0 = jnp.concatenate([S["meta_tokens"], x, jnp.zeros((pad, D_MODEL), F32)], axis=0)
    tgt_p = jnp.concatenate([jnp.zeros((N_META, D_MODEL), F32), tgt, jnp.zeros((pad, D_MODEL), F32)], axis=0)

    def gather_of(l, names):
        return _Exchange("gather", [w_blocks[l][n] for n in names])

    saved = []
    h, hb = h0, h0.astype(BF16)
    G = dict(gathered_up1)
    for l in range(DEPTH):
        first = l == 0
        sv = dict(h_in=h, h_in_b=hb, up1=G["ffn1_w_up"])
        names = ("ffn1_w_down", "mix_w_in")
        gu, a, *got = _ffn_up(hb, sv["up1"], name=f"ffn_up_{l}a", tm=tms,
                              exchange=gather_of(0, names) if first else None)
        G.update(zip(names, got))
        sv["down1"] = _from_blocks(G["ffn1_w_down"], 0)
        names = tuple(n for n in _MIXER if n != "mix_w_in")
        r, h1, h1b, *got = _mm_res_ln(a, sv["down1"], h, S["ln_g"][l, 0:1], S["ln_b"][l, 0:1], scale=0.5,
                                      name=f"ffn_down_ln_{l}a", tm=tm,
                                      exchange=gather_of(0, names) if first else None)
        G.update(zip(names, got))
        sv["f1"] = dict(gu=gu, a=a, r=r)
        lw = _mixer_weights(G)
        sv["lw"] = lw
        if first:
            z, *got = _mm(h1b, lw["w_in"], out_dtype=F32, name=f"mix_in_{l}", tm=tms, tn=1024, tk=D_MODEL,
                          exchange=gather_of(0, _FFN2[:1]))
            G.update(zip(_FFN2[:1], got))
        else:
            z = _mm(h1b, lw["w_in"], out_dtype=F32, name=f"mix_in_{l}", tm=tms, tn=1024, tk=D_MODEL)
        conv_w8 = _row8(S["conv_w"][l])
        bg8 = _row8(S["mix_b_gate"][l])
        yc = _conv_fwd(z, conv_w8, name=f"conv_fwd_{l}", tm=tms)
        gq, gkv = S["q_norm_g"][l:l + 1], S["kv_norm_g"][l:l + 1]
        q, k, v, qn, kin = _qkv_proj(z, gq, gkv, lw["wq"], lw["wk"], lw["wv_ext"], tabs, name=f"qkv_proj_{l}", tm=tm)
        o, lse, *nxt = _attn_fwd(q, k, v, name=f"attn_fwd_{l}", blk=blk,
                                 exchange=gather_of(1, _FFN1 + _MIXER if first else _FFN2))
        if not first:
            G.update(zip(_FFN2, nxt))
        sv["up2"] = G["ffn2_w_up"]
        mg, pa, pb = _merge(yc, o, lw["wbc"], lw["wbm"], z, bg8, name=f"merge_{l}", tm=tm)
        r2, h2, h2b = _mm_res_ln(mg, lw["wo"], h1, S["ln_g"][l, 1:2], S["ln_b"][l, 1:2], scale=1.0,
                                 name=f"wo_ln_{l}", tm=tm)
        gu, a, *got = _ffn_up(h2b, sv["up2"], name=f"ffn_up_{l}b", tm=tms,
                              exchange=gather_of(0, _FFN2[1:]) if first else None)
        G.update(zip(_FFN2[1:], got))
        sv["down2"] = _from_blocks(G["ffn2_w_down"], 0)
        r, h3, h3b = _mm_res_ln(a, sv["down2"], h2, S["ln_g"][l, 2:3], S["ln_b"][l, 2:3], scale=0.5,
                                name=f"ffn_down_ln_{l}b", tm=tm)
        sv["f2"] = dict(gu=gu, a=a, r=r)
        if first:
            G = dict(zip(_FFN1 + _MIXER, nxt))
        sv.update(h1b=h1b, z=z, conv_w8=conv_w8, bg8=bg8, yc=yc, gq=gq, gkv=gkv, q=q, k=k, v=v, qn=qn, kin=kin,
                  o=o, lse=lse, mg=mg, pa=pa, pb=pb, r2=r2, h2b=h2b)
        saved.append(sv)
        h, hb = h3, h3b

    dh, loss8 = _loss_head(h, tgt_p, t_real=t_real, name="loss_head", tm=tm)

    tk = _tile(T, (2816, 768, 256, 128))
    grads = {n: [None] * DEPTH for n, _ in _BIG}
    for n in ("mix_b_gate", "conv_w", "q_norm_g", "kv_norm_g"):
        grads[n] = [None] * DEPTH
    grads["ln_g"] = [[None] * 3 for _ in range(DEPTH)]
    grads["ln_b"] = [[None] * 3 for _ in range(DEPTH)]

    def dest_of(l, names):
        return {n: grads[n][l] if n.endswith("w_up") else _to_dest(grads[n][l], axes[n] - 1) for n in names}

    def scatter_of(dest):
        return _Exchange("scatter", [dest[n].astype(BF16) for n in dest])

    dest = [{}, {}]
    received = [{}, {}]

    def ffn_bwd(dy, f, h_in_b, up8, down, g, tag, names, l, carry=None):
        dr, dfb, dg, db = _ln_bwd(dy, f["r"], g, scale=0.5, name=f"ln_bwd_{tag}", tm=tm)
        dgu, *got = _ffn_bwd_mid(dfb, down, f["gu"], name=f"ffn_bwd_mid_{tag}", tm=tms,
                                 exchange=scatter_of(carry) if carry else None)
        if carry:
            received[l].update(zip(carry, got))
        grads[names[1]][l] = _mm_tn_call(
            f["a"], dfb,
            pl.BlockSpec((None, tk, FF_BLK), lambda i, j, k: (i, k, 0)),
            pl.BlockSpec((tk, D_MODEL), lambda i, j, k: (k, 0)),
            out_shape=jax.ShapeDtypeStruct((D_FF, D_MODEL), F32),
            out_spec=pl.BlockSpec((FF_BLK, D_MODEL), lambda i, j, k: (i, 0)),
            grid=(FF_HALF_BLOCKS, 1, T // tk), name=f"dw_down_{tag}")
        own_down = dest_of(l, names[1:]) if carry else None
        d_up = _mm_tn_call(
            h_in_b, dgu,
            pl.BlockSpec((tk, D_MODEL), lambda i, j, k: (k, 0)),
            pl.BlockSpec((None, None, tk, FF_BLK),
                         lambda i, j, k: (j // FF_HALF_BLOCKS, j % FF_HALF_BLOCKS, k, 0)),
            out_shape=jax.ShapeDtypeStruct((N_DEV, D_MODEL, FF_BLK), F32),
            out_spec=pl.BlockSpec((None, D_MODEL, FF_BLK), lambda i, j, k: (j, 0, 0)),
            grid=(1, N_DEV, T // tk), name=f"dw_up_{tag}",
            exchange=scatter_of(own_down) if carry else None)
        own = None
        if carry:
            d_up, *got = d_up
            received[l].update(zip(own_down, got))
        grads[names[0]][l] = d_up
        if carry:
            own = dest_of(l, names[:1])
            dest[l].update(own_down)
            dest[l].update(own)
        row = pl.BlockSpec((tms, D_MODEL), lambda i, j, k: (i, 0))
        dh_in = _mm_call(
            dgu, up8,
            pl.BlockSpec((None, FF_HALF_BLOCKS, tms, FF_BLK), lambda i, j, k: (k, 0, i, 0)),
            pl.BlockSpec((FF_HALF_BLOCKS, D_MODEL, FF_BLK), lambda i, j, k: (k, 0, 0)),
            out_shape=jax.ShapeDtypeStruct((T, D_MODEL), F32), out_spec=row, acc_shape=(tms, D_MODEL),
            grid=(T // tms, 1, 2), name=f"ffn_dx_{tag}", trans_b=True, res=dr, res_spec=row,
            res_scale=ALPHA, pieces=FF_HALF_BLOCKS, exchange=scatter_of(own) if own else None)
        if own:
            dh_in, *got = dh_in
            received[l].update(zip(own, got))
        return dh_in, dg[0], db[0]

    for l in reversed(range(DEPTH)):
        sv = saved[l]
        lw = sv["lw"]
        dh, grads["ln_g"][l][2], grads["ln_b"][l][2] = ffn_bwd(
            dh, sv["f2"], sv["h2b"], sv["up2"], sv["down2"], S["ln_g"][l, 2:3], f"{l}b", _FFN2, l)
        dr2, dmb, dg, db = _ln_bwd(dh, sv["r2"], S["ln_g"][l, 1:2], scale=1.0, name=f"ln_bwd_{l}m", tm=tm)
        grads["ln_g"][l][1], grads["ln_b"][l][1] = dg[0], db[0]
        grads["w_o"][l] = _mm_tn(sv["mg"], dmb, name=f"dw_o_{l}", tm=D_MODEL, tn=D_MODEL, tk=tk)
        dpa, dpb, dgate, dbg = _wo_bwd(dmb, lw["wo"], sv["z"], sv["bg8"], sv["pa"], sv["pb"], name=f"wo_bwd_{l}", tm=tm)
        grads["mix_b_gate"][l] = dbg[0:2]
        grads["w_br_conv"][l] = _mm_tn(sv["yc"], dpa, name=f"dw_br_conv_{l}", tm=D_CONV, tn=D_MODEL, tk=tk)
        grads["w_br_mla"][l] = _mm_tn(sv["o"], dpb, name=f"dw_br_mla_{l}", tm=D_MLA, tn=D_MODEL, tk=tk)
        dyc = _mm(dpa, lw["wbc"], trans_b=True, out_dtype=F32, name=f"d_yconv_{l}", tm=tms, tn=D_CONV, tk=D_MODEL)
        dym = _mm(dpb, lw["wbm"], trans_b=True, out_dtype=BF16, name=f"d_ymla_{l}", tm=tms, tn=D_MLA, tk=D_MODEL)
        dz_conv, dcw = _conv_bwd(dyc, sv["z"], sv["conv_w8"], name=f"conv_bwd_{l}", tm=tms)
        grads["conv_w"][l] = dcw[0:CONV_WIDTH]
        if l == 0:
            dest[1] = dest_of(1, big_names)
            dest[0].update(dest_of(0, _FFN2 + _MIXER_EARLY))
            sent = {(1, n): dest[1][n] for n in big_names}
            sent.update({(0, n): dest[0][n] for n in _FFN2 + _MIXER_EARLY})
            dq, dk, dv, *got = _attn_bwd(
                sv["q"], sv["k"], sv["v"], sv["o"], dym, sv["lse"], name=f"attn_bwd_{l}", blk=blk,
                exchange=scatter_of(sent))
            for (layer, n), r in zip(sent, got):
                received[layer][n] = r
        else:
            dq, dk, dv = _attn_bwd(sv["q"], sv["k"], sv["v"], sv["o"], dym, sv["lse"], name=f"attn_bwd_{l}", blk=blk)
        dz_mid, dqb, dkb, dvb, dgq, dgkv = _qkv_bwd(dq, dk, dv, sv["z"], sv["gq"], sv["gkv"], lw["wq"], lw["wk"],
                                                    lw["wv"], tabs, name=f"qkv_bwd_{l}", tm=tm)
        grads["q_norm_g"][l], grads["kv_norm_g"][l] = dgq[0], dgkv[0]
        d_wq = _mm_tn(sv["qn"], dqb, name=f"dw_uq_{l}", tm=Q_LORA, tn=D_QK, tk=tk)
        d_wk = _mm_tn(sv["kin"], dkb, name=f"dw_uk_{l}", tm=Q_LORA, tn=D_QK, tk=tk)
        d_wv = _mm_tn(sv["kin"], dvb, name=f"dw_uv_{l}", tm=Q_LORA, tn=D_MLA, tk=tk)
        grads["w_uq"][l] = d_wq.reshape(Q_LORA, MLA_HEADS, HEAD_PAD)[:, :, :QK_NOPE + QK_ROPE].reshape(Q_LORA, -1)
        d_kn = d_wk[:KV_LORA].reshape(KV_LORA, MLA_HEADS, HEAD_PAD)[:, :, :QK_NOPE]
        d_vv = d_wv[:KV_LORA].reshape(KV_LORA, MLA_HEADS, V_HEAD)
        grads["w_ukv"][l] = jnp.concatenate([d_kn, d_vv], axis=-1).reshape(KV_LORA, -1)
        dz = jnp.concatenate([dz_conv, dz_mid, dgate], axis=1)
        d_win = _mm_tn(sv["h1b"], dz, name=f"dw_in_{l}", tm=D_MODEL, tn=1024, tk=tk)
        grads["mix_w_in"][l] = jnp.concatenate([d_win[:, :Z_KR_END], d_win[:, Z_KR_END + D_IN_PAD - D_IN_REAL:]], axis=1)
        dh = _mm(dz, lw["w_in"], trans_b=True, out_dtype=F32, name=f"mix_dx_{l}", res=dr2, res_scale=ALPHA,
                 tm=tms, tn=D_MODEL, tk=2048)
        carry = None
        if l == 0:
            carry = dest_of(0, [n for n in _MIXER if n not in _MIXER_EARLY])
            dest[0].update(carry)
        dh, grads["ln_g"][l][0], grads["ln_b"][l][0] = ffn_bwd(
            dh, sv["f1"], sv["h_in_b"], sv["up1"], sv["down1"], S["ln_g"][l, 0:1], f"{l}a", _FFN1, l, carry=carry)

    small = {n: jnp.stack(grads[n]) for n in ("mix_b_gate", "conv_w", "q_norm_g", "kv_norm_g")}
    small["ln_g"] = jnp.stack([jnp.stack(g) for g in grads["ln_g"]])
    small["ln_b"] = jnp.stack([jnp.stack(g) for g in grads["ln_b"]])
    small["meta_tokens"] = dh[:N_META]
    return loss8, dh[N_META:t_real], dest, received, small


def kernel(x, meta_tokens, ffn1_w_up, ffn1_w_down, mix_w_in, mix_b_gate, conv_w, q_norm_g, w_uq, kv_norm_g, w_ukv, w_br_conv, w_br_mla, w_o, ffn2_w_up, ffn2_w_down, ln_g, ln_b, loss_target, m_meta_tokens, m_ffn1_w_up, m_ffn1_w_down, m_mix_w_in, m_mix_b_gate, m_conv_w, m_q_norm_g, m_w_uq, m_kv_norm_g, m_w_ukv, m_w_br_conv, m_w_br_mla, m_w_o, m_ffn2_w_up, m_ffn2_w_down, m_ln_g, m_ln_b, v_meta_tokens, v_ffn1_w_up, v_ffn1_w_down, v_mix_w_in, v_mix_b_gate, v_conv_w, v_q_norm_g, v_w_uq, v_kv_norm_g, v_w_ukv, v_w_br_conv, v_w_br_mla, v_w_o, v_ffn2_w_up, v_ffn2_w_down, v_ln_g, v_ln_b):
    names = ["meta_tokens", "ffn1_w_up", "ffn1_w_down", "mix_w_in", "mix_b_gate", "conv_w", "q_norm_g", "w_uq",
             "kv_norm_g", "w_ukv", "w_br_conv", "w_br_mla", "w_o", "ffn2_w_up", "ffn2_w_down", "ln_g", "ln_b"]
    w = dict(zip(names, (meta_tokens, ffn1_w_up, ffn1_w_down, mix_w_in, mix_b_gate, conv_w, q_norm_g, w_uq,
                         kv_norm_g, w_ukv, w_br_conv, w_br_mla, w_o, ffn2_w_up, ffn2_w_down, ln_g, ln_b)))
    m = dict(zip(names, (m_meta_tokens, m_ffn1_w_up, m_ffn1_w_down, m_mix_w_in, m_mix_b_gate, m_conv_w, m_q_norm_g,
                         m_w_uq, m_kv_norm_g, m_w_ukv, m_w_br_conv, m_w_br_mla, m_w_o, m_ffn2_w_up, m_ffn2_w_down,
                         m_ln_g, m_ln_b)))
    v = dict(zip(names, (v_meta_tokens, v_ffn1_w_up, v_ffn1_w_down, v_mix_w_in, v_mix_b_gate, v_conv_w, v_q_norm_g,
                         v_w_uq, v_kv_norm_g, v_w_ukv, v_w_br_conv, v_w_br_mla, v_w_o, v_ffn2_w_up, v_ffn2_w_down,
                         v_ln_g, v_ln_b)))
    ix, iy, ic = lax.axis_index("x"), lax.axis_index("y"), lax.axis_index("c")
    dev = 4 * ix + 2 * iy + ic

    big_names = [n for n, _ in _BIG]
    small_names = [n for n, _ in _SMALL_SHARDED]
    small_axes = [a for _, a in _SMALL_SHARDED]
    small_shapes = [w[n].shape for n in small_names]
    gathered = _all_gather([w[n][0].astype(BF16) for n in _FFN1[:1]] + [_pack([w[n] for n in small_names], F32)],
                           name="all_gather_weights")
    gathered_up1 = dict(zip(_FFN1[:1], gathered[:-1]))
    w_blocks = [{n: w[n][l].astype(BF16) for n in big_names} for l in range(DEPTH)]
    S = dict(zip(small_names, _unpack_gathered(gathered[-1], small_shapes, small_axes)))
    S["q_norm_g"], S["kv_norm_g"] = q_norm_g, kv_norm_g

    loss8, grad_x, dest, received, G = _local_step(x[0], loss_target[0], gathered_up1, w_blocks, S)

    my_dev = dev.reshape(1).astype(jnp.int32)
    big_res = [{}, {}, {}, {}]
    for n in big_names:
        res = [_adamw(w[n], m[n], v[n], l, [(dest[l][n], None)] + [(received[l][n], k) for k in range(N_DEV - 1)],
                      my_dev, name=f"adamw{l}_{n}") for l in range(DEPTH)]
        for kind in range(4):
            big_res[kind][n] = jnp.stack([r[kind] for r in res])

    small_all = small_names + list(_SMALL_REPL)
    part = _pack([G[n] for n in small_all] + [loss8[0, 0:1]], F32)
    full_shapes = [G[n].shape for n in small_all] + [(1,)]
    summed = _sum8(_all_gather([part], name="all_gather_small_grads")[0], name="sum_small_grads")
    unpacked = _unpack(summed, full_shapes)
    loss = unpacked[-1][0]
    g_full = dict(zip(small_all, unpacked[:-1]))
    g_loc = []
    for n in small_all:
        if n in _SMALL_REPL:
            g_loc.append(g_full[n])
        else:
            ax = dict(_SMALL_SHARDED)[n]
            g_loc.append(lax.dynamic_slice_in_dim(g_full[n], dev * w[n].shape[ax], w[n].shape[ax], axis=ax))
    loc_shapes = [w[n].shape for n in small_all]
    g_pack = _pack(g_loc, F32)
    small_out = _adamw(_pack([w[n] for n in small_all], F32)[None], _pack([m[n] for n in small_all], F32)[None],
                       _pack([v[n] for n in small_all], F32)[None], 0, [(g_pack[None], 0)],
                       jnp.zeros((1,), jnp.int32), name="adamw_small")
    small_res = [dict(zip(small_all, _unpack(o, loc_shapes))) for o in small_out]

    outs = [loss, grad_x[None]]
    for kind in range(4):
        for n in names:
            outs.append(big_res[kind][n] if n in big_res[kind] else small_res[kind][n])
    return tuple(outs)
```

```python
import functools

import numpy as np
import jax
import jax.numpy as jnp
from jax import lax
from jax.experimental import pallas as pl
from jax.experimental.pallas import tpu as pltpu

F32 = jnp.float32
BF16 = jnp.bfloat16

D_MODEL = 1024
DEPTH = 2
N_META = 16
D_CONV = 512
CONV_WIDTH = 3
MLA_HEADS = 8
QK_NOPE = 64
QK_ROPE = 32
V_HEAD = 64
Q_LORA = 256
KV_LORA = 128
D_MLA = MLA_HEADS * V_HEAD
ROPE_BASE = 10000.0
NEG_INF = -1e30
D_FF = 2816
ALPHA = (2 * DEPTH) ** 0.25
LN_EPS = 1e-5
RMS_EPS = 1e-6
ATTN_SCALE = (QK_NOPE + QK_ROPE) ** -0.5
ADAM_LR = 0.001
ADAM_B1 = 0.9
ADAM_B2 = 0.999
ADAM_EPS = 1e-08
ADAM_WD = 0.01
ADAM_STEP = 10

N_DEV = 8
HEAD_PAD = 128
HEADS_PER_STEP = 4
FWD_HEADS_PER_STEP = 4
FF_BLK = 2 * D_FF // N_DEV
FF_HALF_BLOCKS = N_DEV // 2
D_QK = MLA_HEADS * HEAD_PAD
Z_CONV = 0
Z_MID = 1536
Z_GATE = 2048
D_IN_PAD = 4096
D_IN_REAL = 4000
Z_KR_END = Z_MID + Q_LORA + KV_LORA + QK_ROPE

V7X_VMEM_LIMIT = 56 * 1024 * 1024
LANE = 128
ROW_ALIGN = 256


def _tile(n, cands):
    for c in cands:
        if n % c == 0:
            return c
    raise ValueError(f"no tile for {n} in {cands}")


def _params(sem):
    return pltpu.CompilerParams(dimension_semantics=sem, vmem_limit_bytes=V7X_VMEM_LIMIT)


def _mm_call(a, b, a_spec, b_spec, *, out_shape, out_spec, acc_shape, grid, name, trans_b=False,
             res=None, res_spec=None, res_scale=1.0, pieces=1, exchange=None):
    nk = grid[2]
    has_res = res is not None
    out_dtype = out_shape.dtype
    dims = (((1,), (1,)), ((), ())) if trans_b else (((1,), (0,)), ((), ()))

    def body(*refs):
        if has_res:
            a_ref, b_ref, r_ref, o_ref, acc = refs
        else:
            a_ref, b_ref, o_ref, acc = refs
        k = pl.program_id(2)
        if pieces == 1:
            part = lax.dot_general(a_ref[...], b_ref[...], dims, preferred_element_type=F32)
        else:
            part = lax.dot_general(a_ref[0], b_ref[0], dims, preferred_element_type=F32)
            for p in range(1, pieces):
                part = part + lax.dot_general(a_ref[p], b_ref[p], dims, preferred_element_type=F32)

        @pl.when(k == 0)
        def _():
            acc[...] = part

        @pl.when(k > 0)
        def _():
            acc[...] += part

        @pl.when(k == nk - 1)
        def _():
            out = acc[...]
            if has_res:
                out = out + res_scale * r_ref[...]
            o_ref[...] = out.astype(out_dtype)

    in_specs = [a_spec, b_spec]
    args = [a, b]
    if has_res:
        in_specs.append(res_spec)
        args.append(res)
    if exchange is None:
        return pl.pallas_call(
            body, name=name, grid=grid, in_specs=in_specs, out_specs=out_spec, out_shape=out_shape,
            scratch_shapes=[pltpu.VMEM(acc_shape, F32)],
            compiler_params=_params(("parallel", "parallel", "arbitrary")),
        )(*args)
    ex_args, ex_specs, ex_out, ex_scratch = _exchange_operands(exchange)
    return pl.pallas_call(
        _carry_exchange(body, exchange, 0, len(args), 1, grid), name=name, grid=grid,
        in_specs=in_specs + ex_specs, out_specs=[out_spec] + ex_specs, out_shape=[out_shape] + ex_out,
        scratch_shapes=[pltpu.VMEM(acc_shape, F32)] + ex_scratch,
        compiler_params=_params(("arbitrary", "arbitrary", "arbitrary")),
    )(*args, *ex_args)


def _mm(a, b, *, out_dtype, name, trans_b=False, res=None, res_scale=1.0, tm, tn, tk, exchange=None):
    M, K = a.shape
    N = b.shape[0] if trans_b else b.shape[1]
    assert M % tm == 0 and N % tn == 0 and K % tk == 0
    b_spec = (pl.BlockSpec((tn, tk), lambda i, j, k: (j, k)) if trans_b
              else pl.BlockSpec((tk, tn), lambda i, j, k: (k, j)))
    tile = pl.BlockSpec((tm, tn), lambda i, j, k: (i, j))
    return _mm_call(a, b, pl.BlockSpec((tm, tk), lambda i, j, k: (i, k)), b_spec,
                    out_shape=jax.ShapeDtypeStruct((M, N), out_dtype), out_spec=tile, acc_shape=(tm, tn),
                    grid=(M // tm, N // tn, K // tk), name=name, trans_b=trans_b,
                    res=res, res_spec=tile, res_scale=res_scale, exchange=exchange)


def _mm_tn_call(a, b, a_spec, b_spec, *, out_shape, out_spec, grid, name, exchange=None):
    def body(a_ref, b_ref, o_ref):
        k = pl.program_id(2)
        part = lax.dot_general(a_ref[...], b_ref[...], (((0,), (0,)), ((), ())),
                               preferred_element_type=F32)

        @pl.when(k == 0)
        def _():
            o_ref[...] = part

        @pl.when(k > 0)
        def _():
            o_ref[...] += part

    if exchange is None:
        return pl.pallas_call(
            body, name=name, grid=grid, in_specs=[a_spec, b_spec], out_specs=out_spec, out_shape=out_shape,
            compiler_params=_params(("parallel", "parallel", "arbitrary")),
        )(a, b)
    ex_args, ex_specs, ex_out, ex_scratch = _exchange_operands(exchange)
    return pl.pallas_call(
        _carry_exchange(body, exchange, 0, 2, 1, grid), name=name, grid=grid,
        in_specs=[a_spec, b_spec] + ex_specs, out_specs=[out_spec] + ex_specs, out_shape=[out_shape] + ex_out,
        scratch_shapes=ex_scratch,
        compiler_params=_params(("arbitrary", "arbitrary", "arbitrary")),
    )(a, b, *ex_args)


def _mm_tn(a, b, *, name, tm, tn, tk):
    T, M = a.shape
    N = b.shape[1]
    assert M % tm == 0 and N % tn == 0 and T % tk == 0
    return _mm_tn_call(a, b, pl.BlockSpec((tk, tm), lambda i, j, k: (k, i)),
                       pl.BlockSpec((tk, tn), lambda i, j, k: (k, j)),
                       out_shape=jax.ShapeDtypeStruct((M, N), F32),
                       out_spec=pl.BlockSpec((tm, tn), lambda i, j, k: (i, j)),
                       grid=(M // tm, N // tn, T // tk), name=name)


def _ffn_up(hb, w_up8, *, name, tm, exchange=None):
    T = hb.shape[0]

    def body(h_ref, wg_ref, wu_ref, gu_ref, a_ref):
        h = h_ref[...]
        g = jnp.dot(h, wg_ref[...], preferred_element_type=F32)
        u = jnp.dot(h, wu_ref[...], preferred_element_type=F32)
        sg = jax.nn.sigmoid(g)
        silu = g * sg
        gu_ref[0] = (u * (sg * (1.0 + g * (1.0 - sg)))).astype(BF16)
        gu_ref[1] = silu.astype(BF16)
        a_ref[...] = (silu * u).astype(BF16)

    grid = (FF_HALF_BLOCKS, T // tm)
    ex_args, ex_specs, ex_out, ex_scratch = _exchange_operands(exchange)
    return pl.pallas_call(
        _carry_exchange(body, exchange, 0, 3, 2, grid), name=name, grid=grid,
        in_specs=[pl.BlockSpec((tm, D_MODEL), lambda j, i: (i, 0)),
                  pl.BlockSpec((None, D_MODEL, FF_BLK), lambda j, i: (j, 0, 0)),
                  pl.BlockSpec((None, D_MODEL, FF_BLK), lambda j, i: (j + FF_HALF_BLOCKS, 0, 0))] + ex_specs,
        out_specs=[pl.BlockSpec((2, None, tm, FF_BLK), lambda j, i: (0, j, i, 0)),
                   pl.BlockSpec((None, tm, FF_BLK), lambda j, i: (j, i, 0))] + ex_specs,
        out_shape=[jax.ShapeDtypeStruct((2, FF_HALF_BLOCKS, T, FF_BLK), BF16),
                   jax.ShapeDtypeStruct((FF_HALF_BLOCKS, T, FF_BLK), BF16)] + ex_out,
        scratch_shapes=ex_scratch,
        compiler_params=_params(("arbitrary", "arbitrary") if exchange else ("parallel", "parallel")),
    )(hb, w_up8, w_up8, *ex_args)


def _mm_res_ln(a, w, res, g, b, *, scale, name, tm, exchange=None):
    split = a.ndim == 3
    if split:
        S, T, Ks = a.shape
        K = S * Ks
    else:
        T, K = a.shape

    def body(a_ref, w_ref, res_ref, g_ref, b_ref, r_ref, y_ref, yb_ref):
        if split:
            f = jnp.dot(a_ref[0], w_ref[0:Ks, :], preferred_element_type=F32)
            for s in range(1, S):
                f = f + jnp.dot(a_ref[s], w_ref[s * Ks:(s + 1) * Ks, :], preferred_element_type=F32)
        else:
            f = jnp.dot(a_ref[...], w_ref[...], preferred_element_type=F32)
        r = ALPHA * res_ref[...] + scale * f
        mu = jnp.mean(r, axis=-1, keepdims=True)
        xc = r - mu
        var = jnp.mean(xc * xc, axis=-1, keepdims=True)
        y = xc * lax.rsqrt(var + LN_EPS) * g_ref[...] + b_ref[...]
        r_ref[...] = r
        y_ref[...] = y
        yb_ref[...] = y.astype(BF16)

    row = pl.BlockSpec((tm, D_MODEL), lambda i: (i, 0))
    vec = pl.BlockSpec((1, D_MODEL), lambda i: (0, 0))
    grid = (T // tm,)
    ex_args, ex_specs, ex_out, ex_scratch = _exchange_operands(exchange)
    return pl.pallas_call(
        _carry_exchange(body, exchange, 0, 5, 3, grid), name=name, grid=grid,
        in_specs=[pl.BlockSpec((S, tm, Ks), lambda i: (0, i, 0)) if split else pl.BlockSpec((tm, K), lambda i: (i, 0)),
                  pl.BlockSpec((K, D_MODEL), lambda i: (0, 0)), row, vec, vec] + ex_specs,
        out_specs=[row, row, row] + ex_specs,
        out_shape=[jax.ShapeDtypeStruct((T, D_MODEL), F32), jax.ShapeDtypeStruct((T, D_MODEL), F32),
                   jax.ShapeDtypeStruct((T, D_MODEL), BF16)] + ex_out,
        scratch_shapes=ex_scratch,
        compiler_params=_params(("arbitrary",) if exchange else ("parallel",)),
    )(a, w, res, g, b, *ex_args)


def _conv_fwd(z, conv_w8, *, name, tm):
    T = z.shape[0]
    hb = tm // 8

    def body(b_ref, c_ref, h_ref, cp_ref, hp_ref, w_ref, y_ref):
        i = pl.program_id(0)
        u = c_ref[...] * h_ref[...]
        up = jnp.where(i > 0, cp_ref[...] * hp_ref[...], 0.0)
        ue = jnp.concatenate([up, u], axis=0)
        s1 = pltpu.roll(ue, 1, 0)[8:]
        s2 = pltpu.roll(ue, 2, 0)[8:]
        w = w_ref[...]
        conv = w[0:1] * s2 + w[1:2] * s1 + w[2:3] * u
        y_ref[...] = (b_ref[...] * conv).astype(BF16)

    def col(c):
        return pl.BlockSpec((tm, D_CONV), lambda i: (i, c))

    def prev(c):
        return pl.BlockSpec((8, D_CONV), lambda i: (jnp.maximum(i * hb - 1, 0), c))

    return pl.pallas_call(
        body, name=name, grid=(T // tm,),
        in_specs=[col(0), col(1), col(2), prev(1), prev(2), pl.BlockSpec((8, D_CONV), lambda i: (0, 0))],
        out_specs=pl.BlockSpec((tm, D_CONV), lambda i: (i, 0)),
        out_shape=jax.ShapeDtypeStruct((T, D_CONV), BF16),
        compiler_params=_params(("parallel",)),
    )(z, z, z, z, z, conv_w8)


def _rope(x, c, s1, s2):
    n = x.shape[-1]
    return x * c + pltpu.roll(x, 16, 1) * s1 + pltpu.roll(x, n - 16, 1) * s2


def _rope_t(d, c, s1, s2):
    n = d.shape[-1]
    return d * c + pltpu.roll(d * s1, n - 16, 1) + pltpu.roll(d * s2, 16, 1)


def _rms(x, g):
    rstd = lax.rsqrt(jnp.mean(x * x, axis=-1, keepdims=True) + RMS_EPS)
    return x * rstd * g


def _qkv_proj(z, gq, gkv, wq, wk, wv_ext, tabs, *, name, tm):
    T = z.shape[0]

    def body(z_ref, gq_ref, gkv_ref, wq_ref, wk_ref, wv_ref, c_ref, s1_ref, s2_ref,
             q_ref, k_ref, v_ref, qn_ref, kin_ref):
        zz = z_ref[...]
        qn = _rms(zz[:, :Q_LORA], gq_ref[...]).astype(BF16)
        kvn = _rms(zz[:, Q_LORA:Q_LORA + KV_LORA], gkv_ref[...]).astype(BF16)
        kin = jnp.concatenate([kvn, zz[:, Q_LORA + KV_LORA:].astype(BF16)], axis=-1)
        c = jnp.tile(c_ref[...], (1, MLA_HEADS))
        s1 = jnp.tile(s1_ref[...], (1, MLA_HEADS))
        s2 = jnp.tile(s2_ref[...], (1, MLA_HEADS))
        qpre = jnp.dot(qn, wq_ref[...], preferred_element_type=F32)
        kpre = jnp.dot(kin, wk_ref[...], preferred_element_type=F32)
        q_ref[...] = (_rope(qpre, c, s1, s2) * (ATTN_SCALE * LOG2_E)).astype(BF16)
        k_ref[...] = _rope(kpre, c, s1, s2).astype(BF16)
        vv = jnp.dot(kvn, wv_ref[...], preferred_element_type=F32)
        lane = lax.broadcasted_iota(jnp.int32, vv.shape, 1)
        v_ref[...] = jnp.where((lane & (HEAD_PAD - 1)) < V_HEAD, vv, 1.0).astype(BF16)
        qn_ref[...] = qn
        kin_ref[...] = kin

    def full(shape):
        return pl.BlockSpec(shape, lambda i: (0, 0))

    def rows(w, c=0):
        return pl.BlockSpec((tm, w), lambda i: (i, c))

    return pl.pallas_call(
        body, name=name, grid=(T // tm,),
        in_specs=[rows(512, Z_MID // 512), full((1, Q_LORA)), full((1, KV_LORA)),
                  full((Q_LORA, D_QK)), full((Q_LORA, D_QK)), full((KV_LORA, D_QK)),
                  rows(LANE), rows(LANE), rows(LANE)],
        out_specs=[rows(D_QK), rows(D_QK), rows(D_QK), rows(Q_LORA), rows(Q_LORA)],
        out_shape=[jax.ShapeDtypeStruct((T, D_QK), BF16), jax.ShapeDtypeStruct((T, D_QK), BF16),
                   jax.ShapeDtypeStruct((T, D_QK), BF16), jax.ShapeDtypeStruct((T, Q_LORA), BF16),
                   jax.ShapeDtypeStruct((T, Q_LORA), BF16)],
        compiler_params=_params(("parallel",)),
    )(z, gq, gkv, wq, wk, wv_ext, *tabs)


SOFTMAX_ROWS = 32
LOG2_E = 1.4426950408889634
_NT = (((1,), (1,)), ((), ()))
_TN = (((0,), (0,)), ((), ()))


def _diag_mask(s, row0, col0=0):
    row = lax.broadcasted_iota(jnp.int32, s.shape, 0) + row0
    col = lax.broadcasted_iota(jnp.int32, s.shape, 1) + col0
    return jnp.where(col <= row, s, NEG_INF)


_RELATIONS = tuple((rx, ry, rc) for rx in (0, 1) for ry in (0, 1) for rc in (0, 1))[1:]


class _Exchange:
    def __init__(self, kind, arrays):
        assert kind in ("gather", "scatter")
        self.kind, self.arrays, self.n = kind, list(arrays), len(arrays)

    def out_shapes(self):
        if self.kind == "gather":
            return [jax.ShapeDtypeStruct((N_DEV,) + a.shape, a.dtype) for a in self.arrays]
        return [jax.ShapeDtypeStruct((N_DEV - 1,) + a.shape[1:], a.dtype) for a in self.arrays]

    def scratch_shapes(self):
        sems = [pltpu.SemaphoreType.DMA((7 * self.n,)), pltpu.SemaphoreType.DMA((7 * self.n,))]
        if self.kind == "gather":
            sems.append(pltpu.SemaphoreType.DMA((self.n,)))
        return sems

    def _copies(self, src_refs, out_refs, sems):
        x, y, c = lax.axis_index("x"), lax.axis_index("y"), lax.axis_index("c")
        me = 4 * x + 2 * y + c
        sends, recvs, local = [], [], []
        for a in range(self.n):
            for k, rel in enumerate(_RELATIONS):
                peer = tuple((1 - p) if r else p for p, r in zip((x, y, c), rel))
                peer_index = 4 * peer[0] + 2 * peer[1] + peer[2]
                if self.kind == "gather":
                    src, lands_there, lands_here = src_refs[a], out_refs[a].at[me], out_refs[a].at[peer_index]
                else:
                    src, lands_there, lands_here = src_refs[a].at[peer_index], out_refs[a].at[k], out_refs[a].at[k]
                for dst, group in ((lands_there, sends), (lands_here, recvs)):
                    group.append(pltpu.make_async_remote_copy(
                        src_ref=src, dst_ref=dst, send_sem=sems[0].at[7 * a + k], recv_sem=sems[1].at[7 * a + k],
                        device_id=peer, device_id_type=_MESH_ID))
            if self.kind == "gather":
                local.append(pltpu.make_async_copy(src_refs[a], out_refs[a].at[me], sems[2].at[a]))
        return sends, recvs, local

    def start(self, src_refs, out_refs, sems):
        sends, _, local = self._copies(src_refs, out_refs, sems)
        for cp in local + sends:
            cp.start()

    def wait(self, src_refs, out_refs, sems):
        sends, recvs, local = self._copies(src_refs, out_refs, sems)
        for cp in recvs:
            cp.wait_recv()
        for cp in sends:
            cp.wait_send()
        for cp in local:
            cp.wait()


def _exchange_operands(exchange):
    if exchange is None:
        return [], [], [], []
    return exchange.arrays, [_ANY] * exchange.n, exchange.out_shapes(), exchange.scratch_shapes()


def _carry_exchange(body, exchange, n_prefetch, n_in, n_out, grid):
    if exchange is None:
        return body
    n = exchange.n
    last = tuple(g - 1 for g in grid)

    def wrapped(*refs):
        head = refs[:n_prefetch + n_in]
        src_refs = refs[n_prefetch + n_in:n_prefetch + n_in + n]
        rest = refs[n_prefetch + n_in + n:]
        outs, out_refs, rest = rest[:n_out], rest[n_out:n_out + n], rest[n_out + n:]
        n_sems = len(exchange.scratch_shapes())
        scratch, sems = rest[:len(rest) - n_sems], rest[len(rest) - n_sems:]
        at_first = functools.reduce(jnp.logical_and, [pl.program_id(d) == 0 for d in range(len(grid))])
        at_last = functools.reduce(jnp.logical_and, [pl.program_id(d) == last[d] for d in range(len(grid))])

        @pl.when(at_first)
        def _():
            exchange.start(src_refs, out_refs, sems)

        body(*head, *outs, *scratch)

        @pl.when(at_last)
        def _():
            exchange.wait(src_refs, out_refs, sems)

    return wrapped


def _attn_fwd(q, k, v, *, name, blk, exchange=None):
    T = q.shape[0]
    n = T // blk
    hp = FWD_HEADS_PER_STEP
    qi = np.array([i for i in range(n) for j in range(i + 1)], np.int32)
    kj = np.array([j for i in range(n) for j in range(i + 1)], np.int32)

    rc = _tile(blk, (SOFTMAX_ROWS,))

    def body(qi_ref, kj_ref, q_ref, k_ref, v_ref, o_ref, lse_ref, m_sc, acc_sc, s_sc, p_sc, red_sc):
        s_id = pl.program_id(1)
        i = qi_ref[s_id]
        j = kj_ref[s_id]

        @pl.when(j == 0)
        def _():
            m_sc[...] = jnp.full(m_sc.shape, NEG_INF, F32)
            acc_sc[...] = jnp.zeros(acc_sc.shape, F32)

        def head_step(hh, diagonal):
            hs = slice(hh * HEAD_PAD, (hh + 1) * HEAD_PAD)
            s_sc[hh] = lax.dot_general(q_ref[:, hs], k_ref[:, hs], _NT, preferred_element_type=F32)
            lanes = [slice(t * LANE, (t + 1) * LANE) for t in range(blk // LANE)]
            for r in range(blk // rc):
                rows = slice(r * rc, (r + 1) * rc)
                s = s_sc[hh, rows, :]
                if diagonal:
                    s = _diag_mask(s, r * rc)
                    s_sc[hh, rows, :] = s
                pm = s[:, lanes[0]]
                for t in lanes[1:]:
                    pm = jnp.maximum(pm, s[:, t])
                red_sc[hh, rows, :] = pm
            m_old = m_sc[hh]
            row_max = jnp.max(red_sc[hh], axis=-1, keepdims=True)
            m_new = jnp.maximum(m_old, jnp.broadcast_to(row_max, (blk, LANE)))
            a = jnp.exp2(m_old - m_new)
            m_sc[hh] = m_new
            for r in range(blk // rc):
                rows = slice(r * rc, (r + 1) * rc)
                mb = m_sc[hh, rows, :]
                for t in lanes:
                    p_sc[hh, rows, t] = jnp.exp2(s_sc[hh, rows, t] - mb).astype(BF16)
            acc_sc[hh] = a * acc_sc[hh] + jnp.dot(p_sc[hh], v_ref[:, hs], preferred_element_type=F32)

        @pl.when(j < i)
        def _():
            for hh in range(hp):
                head_step(hh, False)

        @pl.when(j == i)
        def _():
            for hh in range(hp):
                head_step(hh, True)
            for hh in range(hp):
                acc = acc_sc[hh]
                swapped = pltpu.roll(acc, V_HEAD, 1)
                o_ref[:, hh * V_HEAD:(hh + 1) * V_HEAD] = (acc / swapped)[:, :V_HEAD].astype(BF16)
                lane = lax.broadcasted_iota(jnp.int32, acc.shape, 1)
                denom = jnp.where(lane < V_HEAD, swapped, acc)
                lse_ref[hh] = m_sc[hh] + jnp.log(denom) * LOG2_E

    grid = (MLA_HEADS // hp, len(qi))
    ex_args, ex_specs, ex_out, ex_scratch = _exchange_operands(exchange)
    grid_spec = pltpu.PrefetchScalarGridSpec(
        num_scalar_prefetch=2, grid=grid,
        in_specs=[pl.BlockSpec((blk, hp * HEAD_PAD), lambda g, s, qi, kj: (qi[s], g)),
                  pl.BlockSpec((blk, hp * HEAD_PAD), lambda g, s, qi, kj: (kj[s], g)),
                  pl.BlockSpec((blk, hp * HEAD_PAD), lambda g, s, qi, kj: (kj[s], g))] + ex_specs,
        out_specs=[pl.BlockSpec((blk, hp * V_HEAD), lambda g, s, qi, kj: (qi[s], g)),
                   pl.BlockSpec((hp, blk, LANE), lambda g, s, qi, kj: (g, qi[s], 0))] + ex_specs,
        scratch_shapes=[pltpu.VMEM((hp, blk, LANE), F32), pltpu.VMEM((hp, blk, HEAD_PAD), F32),
                        pltpu.VMEM((hp, blk, blk), F32), pltpu.VMEM((hp, blk, blk), BF16),
                        pltpu.VMEM((hp, blk, LANE), F32)] + ex_scratch)
    return pl.pallas_call(
        _carry_exchange(body, exchange, 2, 3, 2, grid), name=name, grid_spec=grid_spec,
        out_shape=[jax.ShapeDtypeStruct((T, D_MLA), BF16),
                   jax.ShapeDtypeStruct((MLA_HEADS, T, LANE), F32)] + ex_out,
        compiler_params=_params(("arbitrary", "arbitrary") if exchange else ("parallel", "arbitrary")),
    )(jnp.asarray(qi), jnp.asarray(kj), q, k, v, *ex_args)


def _merge(yc, ym, wbc, wbm, z, bg, *, name, tm):
    T = yc.shape[0]

    def body(yc_ref, ym_ref, wbc_ref, wbm_ref, gc_ref, gm_ref, bg_ref, mg_ref, pa_ref, pb_ref):
        pa = jnp.dot(yc_ref[...], wbc_ref[...], preferred_element_type=F32)
        pb = jnp.dot(ym_ref[...], wbm_ref[...], preferred_element_type=F32)
        bgv = bg_ref[...]
        sa = jax.nn.sigmoid(gc_ref[...] + bgv[0:1])
        sb = jax.nn.sigmoid(gm_ref[...] + bgv[1:2])
        mg_ref[...] = (sa * pa + sb * pb).astype(BF16)
        pa_ref[...] = pa.astype(BF16)
        pb_ref[...] = pb.astype(BF16)

    row = pl.BlockSpec((tm, D_MODEL), lambda i: (i, 0))
    return pl.pallas_call(
        body, name=name, grid=(T // tm,),
        in_specs=[pl.BlockSpec((tm, D_CONV), lambda i: (i, 0)), pl.BlockSpec((tm, D_MLA), lambda i: (i, 0)),
                  pl.BlockSpec((D_CONV, D_MODEL), lambda i: (0, 0)), pl.BlockSpec((D_MLA, D_MODEL), lambda i: (0, 0)),
                  pl.BlockSpec((tm, D_MODEL), lambda i: (i, Z_GATE // D_MODEL)),
                  pl.BlockSpec((tm, D_MODEL), lambda i: (i, Z_GATE // D_MODEL + 1)),
                  pl.BlockSpec((8, D_MODEL), lambda i: (0, 0))],
        out_specs=[row, row, row],
        out_shape=[jax.ShapeDtypeStruct((T, D_MODEL), BF16)] * 3,
        compiler_params=_params(("parallel",)),
    )(yc, ym, wbc, wbm, z, z, bg)


def _loss_head(h, tgt, *, t_real, name, tm):
    T = h.shape[0]

    def body(h_ref, t_ref, dy_ref, loss_ref):
        i = pl.program_id(0)
        row = lax.broadcasted_iota(jnp.int32, (tm, 1), 0) + i * tm
        valid = (row >= N_META) & (row < t_real)
        err = jnp.where(valid, h_ref[...] - t_ref[...], 0.0)
        dy_ref[...] = err * (1.0 / D_MODEL)
        part = 0.5 * jnp.sum(jnp.sum(err * err, axis=-1, keepdims=True) * (1.0 / D_MODEL), axis=0, keepdims=True)

        @pl.when(i == 0)
        def _():
            loss_ref[...] = jnp.zeros(loss_ref.shape, F32)

        loss_ref[...] += jnp.broadcast_to(part, loss_ref.shape)

    row_spec = pl.BlockSpec((tm, D_MODEL), lambda i: (i, 0))
    return pl.pallas_call(
        body, name=name, grid=(T // tm,),
        in_specs=[row_spec, row_spec],
        out_specs=[row_spec, pl.BlockSpec((8, LANE), lambda i: (0, 0))],
        out_shape=[jax.ShapeDtypeStruct((T, D_MODEL), F32), jax.ShapeDtypeStruct((8, LANE), F32)],
        compiler_params=_params(("arbitrary",)),
    )(h, tgt)


def _ln_bwd(dy, r, g, *, scale, name, tm):
    T = dy.shape[0]

    def body(dy_ref, r_ref, g_ref, dr_ref, drb_ref, dg_ref, db_ref):
        i = pl.program_id(0)
        rr = r_ref[...]
        dyv = dy_ref[...]
        mu = jnp.mean(rr, axis=-1, keepdims=True)
        xc = rr - mu
        rstd = lax.rsqrt(jnp.mean(xc * xc, axis=-1, keepdims=True) + LN_EPS)
        xh = xc * rstd
        dxh = dyv * g_ref[...]
        m1 = jnp.mean(dxh, axis=-1, keepdims=True)
        m2 = jnp.mean(dxh * xh, axis=-1, keepdims=True)
        dr = rstd * (dxh - m1 - xh * m2)
        dr_ref[...] = dr
        drb_ref[...] = (scale * dr).astype(BF16)

        @pl.when(i == 0)
        def _():
            dg_ref[...] = jnp.zeros(dg_ref.shape, F32)
            db_ref[...] = jnp.zeros(db_ref.shape, F32)

        dg_ref[0:1, :] += jnp.sum(dyv * xh, axis=0, keepdims=True)
        db_ref[0:1, :] += jnp.sum(dyv, axis=0, keepdims=True)

    row = pl.BlockSpec((tm, D_MODEL), lambda i: (i, 0))
    acc = pl.BlockSpec((8, D_MODEL), lambda i: (0, 0))
    return pl.pallas_call(
        body, name=name, grid=(T // tm,),
        in_specs=[row, row, pl.BlockSpec((1, D_MODEL), lambda i: (0, 0))],
        out_specs=[row, row, acc, acc],
        out_shape=[jax.ShapeDtypeStruct((T, D_MODEL), F32), jax.ShapeDtypeStruct((T, D_MODEL), BF16),
                   jax.ShapeDtypeStruct((8, D_MODEL), F32), jax.ShapeDtypeStruct((8, D_MODEL), F32)],
        compiler_params=_params(("arbitrary",)),
    )(dy, r, g)


def _ffn_bwd_mid(dfb, w_down, gu, *, name, tm, exchange=None):
    T = dfb.shape[0]

    def body(df_ref, w_ref, gu_ref, o_ref):
        da = lax.dot_general(df_ref[...], w_ref[...], _NT, preferred_element_type=F32)
        o_ref[0] = (da * gu_ref[0].astype(F32)).astype(BF16)
        o_ref[1] = (da * gu_ref[1].astype(F32)).astype(BF16)

    grid = (FF_HALF_BLOCKS, T // tm)
    ex_args, ex_specs, ex_out, ex_scratch = _exchange_operands(exchange)
    return pl.pallas_call(
        _carry_exchange(body, exchange, 0, 3, 1, grid), name=name, grid=grid,
        in_specs=[pl.BlockSpec((tm, D_MODEL), lambda j, i: (i, 0)),
                  pl.BlockSpec((FF_BLK, D_MODEL), lambda j, i: (j, 0)),
                  pl.BlockSpec((2, None, tm, FF_BLK), lambda j, i: (0, j, i, 0))] + ex_specs,
        out_specs=[pl.BlockSpec((2, None, tm, FF_BLK), lambda j, i: (0, j, i, 0))] + ex_specs,
        out_shape=[jax.ShapeDtypeStruct((2, FF_HALF_BLOCKS, T, FF_BLK), BF16)] + ex_out,
        scratch_shapes=ex_scratch,
        compiler_params=_params(("arbitrary", "arbitrary") if exchange else ("parallel", "parallel")),
    )(dfb, w_down, gu, *ex_args)


def _wo_bwd(dmb, wo_t, z, bg, pa, pb, *, name, tm):
    T = dmb.shape[0]

    def body(dm_ref, w_ref, gc_ref, gm_ref, bg_ref, pa_ref, pb_ref, dpa_ref, dpb_ref, dg_ref, dbg_ref):
        i = pl.program_id(0)
        dm = lax.dot_general(dm_ref[...], w_ref[...], _NT, preferred_element_type=F32)
        bgv = bg_ref[...]
        sa = jax.nn.sigmoid(gc_ref[...] + bgv[0:1])
        sb = jax.nn.sigmoid(gm_ref[...] + bgv[1:2])
        dpa_ref[...] = (dm * sa).astype(BF16)
        dpb_ref[...] = (dm * sb).astype(BF16)
        dga = dm * pa_ref[...].astype(F32) * (sa * (1.0 - sa))
        dgb = dm * pb_ref[...].astype(F32) * (sb * (1.0 - sb))
        dg_ref[:, :D_MODEL] = dga.astype(BF16)
        dg_ref[:, D_MODEL:] = dgb.astype(BF16)

        @pl.when(i == 0)
        def _():
            dbg_ref[...] = jnp.zeros(dbg_ref.shape, F32)

        dbg_ref[0:1, :] += jnp.sum(dga, axis=0, keepdims=True)
        dbg_ref[1:2, :] += jnp.sum(dgb, axis=0, keepdims=True)

    row = pl.BlockSpec((tm, D_MODEL), lambda i: (i, 0))
    return pl.pallas_call(
        body, name=name, grid=(T // tm,),
        in_specs=[row, pl.BlockSpec((D_MODEL, D_MODEL), lambda i: (0, 0)),
                  pl.BlockSpec((tm, D_MODEL), lambda i: (i, Z_GATE // D_MODEL)),
                  pl.BlockSpec((tm, D_MODEL), lambda i: (i, Z_GATE // D_MODEL + 1)),
                  pl.BlockSpec((8, D_MODEL), lambda i: (0, 0)), row, row],
        out_specs=[row, row, pl.BlockSpec((tm, 2 * D_MODEL), lambda i: (i, 0)),
                   pl.BlockSpec((8, D_MODEL), lambda i: (0, 0))],
        out_shape=[jax.ShapeDtypeStruct((T, D_MODEL), BF16), jax.ShapeDtypeStruct((T, D_MODEL), BF16),
                   jax.ShapeDtypeStruct((T, 2 * D_MODEL), BF16), jax.ShapeDtypeStruct((8, D_MODEL), F32)],
        compiler_params=_params(("arbitrary",)),
    )(dmb, wo_t, z, z, bg, pa, pb)


def _conv_bwd(dy, z, conv_w8, *, name, tm):
    T = dy.shape[0]
    n = T // tm
    hb = tm // 8

    def body(dy_ref, b_ref, c_ref, h_ref, cp_ref, hp_ref, dyn_ref, bn_ref, w_ref, dz_ref, dw_ref):
        i = pl.program_id(0)
        u = c_ref[...] * h_ref[...]
        up = jnp.where(i > 0, cp_ref[...] * hp_ref[...], 0.0)
        ue = jnp.concatenate([up, u], axis=0)
        s1 = pltpu.roll(ue, 1, 0)[8:]
        s2 = pltpu.roll(ue, 2, 0)[8:]
        w = w_ref[...]
        conv = w[0:1] * s2 + w[1:2] * s1 + w[2:3] * u
        dyv = dy_ref[...]
        e = dyv * b_ref[...]
        en = jnp.where(i < n - 1, dyn_ref[...] * bn_ref[...], 0.0)
        ee = jnp.concatenate([e, en], axis=0)
        e1 = pltpu.roll(ee, tm + 8 - 1, 0)[:tm]
        e2 = pltpu.roll(ee, tm + 8 - 2, 0)[:tm]
        du = w[2:3] * e + w[1:2] * e1 + w[0:1] * e2
        dz_ref[:, 0:D_CONV] = (dyv * conv).astype(BF16)
        dz_ref[:, D_CONV:2 * D_CONV] = (du * h_ref[...]).astype(BF16)
        dz_ref[:, 2 * D_CONV:] = (du * c_ref[...]).astype(BF16)

        @pl.when(i == 0)
        def _():
            dw_ref[...] = jnp.zeros(dw_ref.shape, F32)

        dw_ref[0:1, :] += jnp.sum(e * s2, axis=0, keepdims=True)
        dw_ref[1:2, :] += jnp.sum(e * s1, axis=0, keepdims=True)
        dw_ref[2:3, :] += jnp.sum(e * u, axis=0, keepdims=True)

    def col(c):
        return pl.BlockSpec((tm, D_CONV), lambda i: (i, c))

    def prev(c):
        return pl.BlockSpec((8, D_CONV), lambda i: (jnp.maximum(i * hb - 1, 0), c))

    def nxt(c):
        return pl.BlockSpec((8, D_CONV), lambda i: (jnp.minimum((i + 1) * hb, T // 8 - 1), c))

    return pl.pallas_call(
        body, name=name, grid=(n,),
        in_specs=[col(0), col(0), col(1), col(2), prev(1), prev(2), nxt(0), nxt(0),
                  pl.BlockSpec((8, D_CONV), lambda i: (0, 0))],
        out_specs=[pl.BlockSpec((tm, 3 * D_CONV), lambda i: (i, 0)), pl.BlockSpec((8, D_CONV), lambda i: (0, 0))],
        out_shape=[jax.ShapeDtypeStruct((T, 3 * D_CONV), BF16), jax.ShapeDtypeStruct((8, D_CONV), F32)],
        compiler_params=_params(("arbitrary",)),
    )(dy, z, z, z, z, z, dy, z, conv_w8)


def _attn_bwd(q, k, v, o, do, lse, *, name, blk, exchange=None):
    T = q.shape[0]
    n = T // blk
    hp = HEADS_PER_STEP
    qi = np.array([i for j in range(n) for i in range(j, n)], np.int32)
    kj = np.array([j for j in range(n) for i in range(j, n)], np.int32)

    rc = _tile(blk, (SOFTMAX_ROWS,))

    def body(qi_ref, kj_ref, q_ref, k_ref, v_ref, o_ref, do_ref, lse_ref, dq_ref, dk_ref, dv_ref,
             dk_sc, dv_sc, s_sc, dp_sc, p_sc, ds_sc, delta_sc):
        s_id = pl.program_id(1)
        i = qi_ref[s_id]
        j = kj_ref[s_id]

        @pl.when(s_id == 0)
        def _():
            dq_ref[...] = jnp.zeros(dq_ref.shape, F32)

        @pl.when(i == j)
        def _():
            dk_sc[...] = jnp.zeros(dk_sc.shape, F32)
            dv_sc[...] = jnp.zeros(dv_sc.shape, F32)

        q_rows = pl.ds(pl.multiple_of(i * blk, blk), blk)

        def head_step(hh, diagonal):
            hs = slice(hh * HEAD_PAD, (hh + 1) * HEAD_PAD)
            vs = slice(hh * V_HEAD, (hh + 1) * V_HEAD)
            qh = q_ref[:, hs]
            kh = k_ref[:, hs]
            doh = do_ref[:, vs]
            s_sc[hh] = lax.dot_general(qh, kh, _NT, preferred_element_type=F32)
            vh = v_ref[:, hh * HEAD_PAD:hh * HEAD_PAD + V_HEAD]
            dp_sc[hh] = lax.dot_general(doh, vh, _NT, preferred_element_type=F32)
            delta = jnp.sum(doh.astype(F32) * o_ref[:, vs].astype(F32), axis=-1, keepdims=True)
            delta_sc[hh] = jnp.broadcast_to(delta, (blk, LANE))
            for r in range(blk // rc):
                rows = slice(r * rc, (r + 1) * rc)
                lse = lse_ref[hh, rows, :]
                dl = delta_sc[hh, rows, :]
                for t in range(blk // LANE):
                    cols = slice(t * LANE, (t + 1) * LANE)
                    s = s_sc[hh, rows, cols]
                    if diagonal:
                        s = _diag_mask(s, r * rc, t * LANE)
                    p = jnp.exp2(s - lse)
                    p_sc[hh, rows, cols] = p.astype(BF16)
                    ds_sc[hh, rows, cols] = (p * (dp_sc[hh, rows, cols] - dl)).astype(BF16)
            dv_sc[hh] += lax.dot_general(p_sc[hh], doh, _TN, preferred_element_type=F32)
            dk_sc[hh] += lax.dot_general(ds_sc[hh], qh, _TN, preferred_element_type=F32)
            dq_ref[q_rows, hs] += jnp.dot(ds_sc[hh], kh, preferred_element_type=F32)

        @pl.when(j < i)
        def _():
            for hh in range(hp):
                head_step(hh, False)

        @pl.when(j == i)
        def _():
            for hh in range(hp):
                head_step(hh, True)

        @pl.when(i == n - 1)
        def _():
            for hh in range(hp):
                dk_ref[:, hh * HEAD_PAD:(hh + 1) * HEAD_PAD] = dk_sc[hh] * (1.0 / LOG2_E)
                dv_ref[:, hh * V_HEAD:(hh + 1) * V_HEAD] = dv_sc[hh]

    wq = hp * HEAD_PAD
    wv = hp * V_HEAD
    grid = (MLA_HEADS // hp, len(qi))
    ex_args, ex_specs, ex_out, ex_scratch = _exchange_operands(exchange)
    grid_spec = pltpu.PrefetchScalarGridSpec(
        num_scalar_prefetch=2, grid=grid,
        in_specs=[pl.BlockSpec((blk, wq), lambda g, s, qi, kj: (qi[s], g)),
                  pl.BlockSpec((blk, wq), lambda g, s, qi, kj: (kj[s], g)),
                  pl.BlockSpec((blk, wq), lambda g, s, qi, kj: (kj[s], g)),
                  pl.BlockSpec((blk, wv), lambda g, s, qi, kj: (qi[s], g)),
                  pl.BlockSpec((blk, wv), lambda g, s, qi, kj: (qi[s], g)),
                  pl.BlockSpec((hp, blk, LANE), lambda g, s, qi, kj: (g, qi[s], 0))] + ex_specs,
        out_specs=[pl.BlockSpec((T, wq), lambda g, s, qi, kj: (0, g), pipeline_mode=pl.Buffered(1)),
                   pl.BlockSpec((blk, wq), lambda g, s, qi, kj: (kj[s], g)),
                   pl.BlockSpec((blk, wv), lambda g, s, qi, kj: (kj[s], g))] + ex_specs,
        scratch_shapes=[pltpu.VMEM((hp, blk, HEAD_PAD), F32), pltpu.VMEM((hp, blk, V_HEAD), F32),
                        pltpu.VMEM((hp, blk, blk), F32), pltpu.VMEM((hp, blk, blk), F32),
                        pltpu.VMEM((hp, blk, blk), BF16), pltpu.VMEM((hp, blk, blk), BF16),
                        pltpu.VMEM((hp, blk, LANE), F32)] + ex_scratch)
    return pl.pallas_call(
        _carry_exchange(body, exchange, 2, 6, 3, grid), name=name, grid_spec=grid_spec,
        out_shape=[jax.ShapeDtypeStruct((T, D_QK), F32), jax.ShapeDtypeStruct((T, D_QK), F32),
                   jax.ShapeDtypeStruct((T, D_MLA), F32)] + ex_out,
        compiler_params=_params(("arbitrary", "arbitrary") if exchange else ("parallel", "arbitrary")),
    )(jnp.asarray(qi), jnp.asarray(kj), q, k, v, o, do, lse, *ex_args)


def _qkv_bwd(dq, dk, dv, z, gq, gkv, wq_t, wk_t, wv_t, tabs, *, name, tm):
    T = dq.shape[0]

    def body(dq_ref, dk_ref, dv_ref, z_ref, gq_ref, gkv_ref, wq_ref, wk_ref, wv_ref, c_ref, s1_ref, s2_ref,
             dz_ref, dqb_ref, dkb_ref, dvb_ref, dgq_ref, dgkv_ref):
        i = pl.program_id(0)
        c = jnp.tile(c_ref[...], (1, MLA_HEADS))
        s1 = jnp.tile(s1_ref[...], (1, MLA_HEADS))
        s2 = jnp.tile(s2_ref[...], (1, MLA_HEADS))
        dqp = _rope_t(dq_ref[...] * ATTN_SCALE, c, s1, s2).astype(BF16)
        dkp = _rope_t(dk_ref[...], c, s1, s2).astype(BF16)
        dvb = dv_ref[...].astype(BF16)
        dqb_ref[...] = dqp
        dkb_ref[...] = dkp
        dvb_ref[...] = dvb
        dqn = lax.dot_general(dqp, wq_ref[...], _NT, preferred_element_type=F32)
        dkin = lax.dot_general(dkp, wk_ref[...], _NT, preferred_element_type=F32)
        dkvn = dkin[:, :KV_LORA] + lax.dot_general(dvb, wv_ref[...], _NT, preferred_element_type=F32)
        zz = z_ref[...]

        def rms_bwd(x, g, dy):
            rstd = lax.rsqrt(jnp.mean(x * x, axis=-1, keepdims=True) + RMS_EPS)
            xh = x * rstd
            dxh = dy * g
            dx = rstd * (dxh - xh * jnp.mean(dxh * xh, axis=-1, keepdims=True))
            return dx, jnp.sum(dy * xh, axis=0, keepdims=True)

        dcq, dgq = rms_bwd(zz[:, :Q_LORA], gq_ref[...], dqn)
        dckv, dgkv = rms_bwd(zz[:, Q_LORA:Q_LORA + KV_LORA], gkv_ref[...], dkvn)
        dz_ref[:, :Q_LORA] = dcq.astype(BF16)
        dz_ref[:, Q_LORA:Q_LORA + KV_LORA] = dckv.astype(BF16)
        dz_ref[:, Q_LORA + KV_LORA:] = dkin[:, KV_LORA:].astype(BF16)

        @pl.when(i == 0)
        def _():
            dgq_ref[...] = jnp.zeros(dgq_ref.shape, F32)
            dgkv_ref[...] = jnp.zeros(dgkv_ref.shape, F32)

        dgq_ref[0:1, :] += dgq
        dgkv_ref[0:1, :] += dgkv

    def full(shape):
        return pl.BlockSpec(shape, lambda i: (0, 0))

    def rows(w, c=0):
        return pl.BlockSpec((tm, w), lambda i: (i, c))

    return pl.pallas_call(
        body, name=name, grid=(T // tm,),
        in_specs=[rows(D_QK), rows(D_QK), rows(D_MLA), rows(512, Z_MID // 512),
                  full((1, Q_LORA)), full((1, KV_LORA)),
                  full((Q_LORA, D_QK)), full((Q_LORA, D_QK)), full((KV_LORA, D_MLA)),
                  rows(LANE), rows(LANE), rows(LANE)],
        out_specs=[rows(512), rows(D_QK), rows(D_QK), rows(D_MLA), full((8, Q_LORA)), full((8, KV_LORA))],
        out_shape=[jax.ShapeDtypeStruct((T, 512), BF16), jax.ShapeDtypeStruct((T, D_QK), BF16),
                   jax.ShapeDtypeStruct((T, D_QK), BF16), jax.ShapeDtypeStruct((T, D_MLA), BF16),
                   jax.ShapeDtypeStruct((8, Q_LORA), F32), jax.ShapeDtypeStruct((8, KV_LORA), F32)],
        compiler_params=_params(("arbitrary",)),
    )(dq, dk, dv, z, gq, gkv, wq_t, wk_t, wv_t, *tabs)


FLAT_W = 1024


ELEMENTWISE_TILE_BYTES = 768 * 1024


def _row_tile(R, C):
    width = -(-C // LANE) * LANE * 4
    best = None
    for t in range(16, R + 1, 16):
        if R % t == 0 and t * width <= ELEMENTWISE_TILE_BYTES:
            best = t
    if best is None:
        best = R
    return best


def _adamw(w, m, v, layer, parts, part_index, *, name):
    _, R, C = w.shape
    tr = _row_tile(R, C)
    bc1 = 1.0 - ADAM_B1 ** ADAM_STEP
    bc2 = 1.0 - ADAM_B2 ** ADAM_STEP
    n_parts = len(parts)

    def body(idx_ref, w_ref, m_ref, v_ref, *refs):
        g_refs = refs[:n_parts]
        g_out, d_out, m_out, v_out = refs[n_parts:]
        g = g_refs[0][...].astype(F32)
        for r in g_refs[1:]:
            g = g + r[...].astype(F32)
        wv = w_ref[...]
        mn = ADAM_B1 * m_ref[...] + (1.0 - ADAM_B1) * g
        vn = ADAM_B2 * v_ref[...] + (1.0 - ADAM_B2) * (g * g)
        m_hat = mn / bc1
        v_hat = vn / bc2
        g_out[...] = g
        d_out[...] = -ADAM_LR * (m_hat / (jnp.sqrt(v_hat) + ADAM_EPS) + ADAM_WD * wv)
        m_out[...] = mn
        v_out[...] = vn

    layer_row = pl.BlockSpec((None, tr, C), lambda i, idx: (layer, i, 0))
    in_specs = [layer_row, layer_row, layer_row]
    args = [w, m, v]
    for arr, slot in parts:
        if slot is None:
            in_specs.append(pl.BlockSpec((None, tr, C), lambda i, idx: (idx[0], i, 0)))
        else:
            in_specs.append(pl.BlockSpec((None, tr, C), lambda i, idx, slot=slot: (slot, i, 0)))
        args.append(arr)
    grid_spec = pltpu.PrefetchScalarGridSpec(
        num_scalar_prefetch=1, grid=(R // tr,), in_specs=in_specs,
        out_specs=[pl.BlockSpec((tr, C), lambda i, idx: (i, 0))] * 4)
    return pl.pallas_call(
        body, name=name, grid_spec=grid_spec,
        out_shape=[jax.ShapeDtypeStruct((R, C), F32)] * 4,
        compiler_params=_params(("parallel",)),
    )(part_index, *args)


def _sum8(parts, *, name):
    _, R, _ = parts.shape

    def body(p_ref, o_ref):
        acc = p_ref[0]
        for d in range(1, N_DEV):
            acc = acc + p_ref[d]
        o_ref[...] = acc

    return pl.pallas_call(
        body, name=name, grid=(1,),
        in_specs=[pl.BlockSpec((N_DEV, R, FLAT_W), lambda i: (0, 0, 0))],
        out_specs=pl.BlockSpec((R, FLAT_W), lambda i: (0, 0)),
        out_shape=jax.ShapeDtypeStruct((R, FLAT_W), F32),
        compiler_params=_params(("arbitrary",)),
    )(parts)


_MESH_ID = pl.DeviceIdType.MESH
_ANY = pl.BlockSpec(memory_space=pl.ANY)


def _all_gather(shards, *, name):
    n = len(shards)

    def body(*refs):
        x_refs, out_refs = refs[:n], refs[n:2 * n]
        send_sems, recv_sems, local_sems = refs[2 * n:]
        x, y, c = lax.axis_index("x"), lax.axis_index("y"), lax.axis_index("c")
        me, sibling = (x, y, c), (x, y, 1 - c)
        chips = [(1 - x, y), (x, 1 - y), (1 - x, 1 - y)]

        def blk(a, px, py, pc):
            return out_refs[a].at[4 * px + 2 * py + pc]

        def copy(a, k, block, to, src=None):
            return pltpu.make_async_remote_copy(
                src_ref=blk(a, *block) if src is None else src, dst_ref=blk(a, *block),
                send_sem=send_sems.at[7 * a + k], recv_sem=recv_sems.at[7 * a + k],
                device_id=to, device_id_type=_MESH_ID)

        mine = [pltpu.make_async_copy(x_refs[a], blk(a, *me), local_sems.at[a]) for a in range(n)]
        for cp in mine:
            cp.start()
        first = []
        for a in range(n):
            first.append(copy(a, 0, me, sibling, src=x_refs[a]))
            first += [copy(a, 1 + j, me, (*chip, c), src=x_refs[a]) for j, chip in enumerate(chips)]
        for cp in first:
            cp.start()
        passed = []
        for j, chip in enumerate(chips):
            for a in range(n):
                copy(a, 1 + j, (*chip, c), me).wait_recv()
                fwd = copy(a, 4 + j, (*chip, c), sibling)
                fwd.start()
                passed.append(fwd)
        for a in range(n):
            copy(a, 0, sibling, me).wait_recv()
        for j, chip in enumerate(chips):
            for a in range(n):
                copy(a, 4 + j, (*chip, 1 - c), me).wait_recv()
        for cp in first + passed:
            cp.wait_send()
        for cp in mine:
            cp.wait()

    return pl.pallas_call(
        body, name=name,
        out_shape=[jax.ShapeDtypeStruct((N_DEV,) + s.shape, s.dtype) for s in shards],
        in_specs=[_ANY] * n, out_specs=[_ANY] * n,
        scratch_shapes=[pltpu.SemaphoreType.DMA((7 * n,)), pltpu.SemaphoreType.DMA((7 * n,)),
                        pltpu.SemaphoreType.DMA((n,))],
    )(*shards)


_BIG = (("ffn1_w_up", 2), ("ffn1_w_down", 1), ("mix_w_in", 2), ("w_uq", 2), ("w_ukv", 2),
        ("w_br_conv", 2), ("w_br_mla", 2), ("w_o", 1), ("ffn2_w_up", 2), ("ffn2_w_down", 1))
_SMALL_SHARDED = (("meta_tokens", 1), ("mix_b_gate", 2), ("conv_w", 2), ("ln_g", 2), ("ln_b", 2))
_SMALL_REPL = ("q_norm_g", "kv_norm_g")


def _pack(arrs, dtype, row_align=8):
    flat = jnp.concatenate([a.reshape(-1).astype(dtype) for a in arrs])
    n = flat.shape[0]
    rows = -(-n // (row_align * FLAT_W)) * row_align
    return jnp.pad(flat, (0, rows * FLAT_W - n)).reshape(rows, FLAT_W)


def _unpack(flat, shapes):
    flat = flat.reshape(-1)
    out, off = [], 0
    for s in shapes:
        n = int(np.prod(s))
        out.append(flat[off:off + n].reshape(s))
        off += n
    return out


def _unpack_gathered(gathered, shapes, axes):
    g2 = gathered.reshape(N_DEV, -1)
    out, off = [], 0
    for s, ax in zip(shapes, axes):
        n = int(np.prod(s))
        blocks = g2[:, off:off + n].reshape((N_DEV,) + tuple(s))
        full = jnp.moveaxis(blocks, 0, ax)
        out.append(full.reshape(tuple(s[:ax]) + (N_DEV * s[ax],) + tuple(s[ax + 1:])))
        off += n
    return out


def _to_dest(full, ax):
    s = full.shape
    split = full.reshape(s[:ax] + (N_DEV, s[ax] // N_DEV) + s[ax + 1:])
    return jnp.moveaxis(split, ax, 0)


def _from_blocks(g, ax):
    full = jnp.moveaxis(g, 0, ax)
    s = full.shape
    return full.reshape(s[:ax] + (s[ax] * s[ax + 1],) + s[ax + 2:])


def _rope_tables(T):
    inv_freq = 1.0 / (ROPE_BASE ** (jnp.arange(0, QK_ROPE, 2, dtype=F32) / QK_ROPE))
    ang = jnp.arange(T, dtype=F32)[:, None] * inv_freq[None, :]
    cos, sin = jnp.cos(ang), jnp.sin(ang)
    half = QK_ROPE // 2
    ones = jnp.ones((T, QK_NOPE), F32)
    zeros = lambda w: jnp.zeros((T, w), F32)
    c = jnp.concatenate([ones, cos, cos, zeros(HEAD_PAD - QK_NOPE - QK_ROPE)], axis=1)
    s1 = jnp.concatenate([zeros(QK_NOPE + half), sin, zeros(HEAD_PAD - QK_NOPE - QK_ROPE)], axis=1)
    s2 = jnp.concatenate([zeros(QK_NOPE), -sin, zeros(HEAD_PAD - QK_NOPE - half)], axis=1)
    return c, s1, s2


_FFN1 = ("ffn1_w_up", "ffn1_w_down")
_MIXER = ("mix_w_in", "w_uq", "w_ukv", "w_br_conv", "w_br_mla", "w_o")
_FFN2 = ("ffn2_w_up", "ffn2_w_down")
_MIXER_EARLY = ("w_br_conv", "w_br_mla", "w_o")


def _mixer_weights(gathered):
    axes = dict(_BIG)
    W = {n: _from_blocks(gathered[n], axes[n] - 1) for n in _MIXER}
    w_in = W["mix_w_in"]
    w_in_p = jnp.concatenate([w_in[:, :Z_KR_END], jnp.zeros((D_MODEL, D_IN_PAD - D_IN_REAL), BF16),
                              w_in[:, Z_KR_END:]], axis=1)
    w_uq = W["w_uq"].reshape(Q_LORA, MLA_HEADS, QK_NOPE + QK_ROPE)
    wq = jnp.pad(w_uq, ((0, 0), (0, 0), (0, HEAD_PAD - QK_NOPE - QK_ROPE))).reshape(Q_LORA, D_QK)
    w_ukv = W["w_ukv"].reshape(KV_LORA, MLA_HEADS, QK_NOPE + V_HEAD)
    wk_top = jnp.pad(w_ukv[:, :, :QK_NOPE], ((0, 0), (0, 0), (0, HEAD_PAD - QK_NOPE))).reshape(KV_LORA, D_QK)
    place = np.zeros((Q_LORA - KV_LORA, MLA_HEADS, HEAD_PAD), np.float32)
    for r in range(QK_ROPE):
        place[r, :, QK_NOPE + r] = 1.0
    wk = jnp.concatenate([wk_top, jnp.asarray(place.reshape(Q_LORA - KV_LORA, D_QK), BF16)], axis=0)
    wv = w_ukv[:, :, QK_NOPE:].reshape(KV_LORA, D_MLA)
    wv_ext = jnp.pad(w_ukv[:, :, QK_NOPE:], ((0, 0), (0, 0), (0, HEAD_PAD - V_HEAD))).reshape(KV_LORA, D_QK)
    return dict(w_in=w_in_p, wq=wq, wk=wk, wv=wv, wv_ext=wv_ext,
                wbc=W["w_br_conv"], wbm=W["w_br_mla"], wo=W["w_o"])


def _row8(v):
    return jnp.pad(v, ((0, 8 - v.shape[0]), (0, 0)))


def _local_step(x, tgt, gathered_up1, w_blocks, S):
    big_names = [n for n, _ in _BIG]
    axes = dict(_BIG)
    t_real = N_META + x.shape[0]
    T = -(-t_real // ROW_ALIGN) * ROW_ALIGN
    pad = T - t_real
    tm = _tile(T, (384, 256, 128))
    tms = _tile(T, (768, 256, 128))
    blk = _tile(T, (768, 256, 128))
    tabs = _rope_tables(T)

    h0 = jnp.concatenate([S["meta_tokens"], x, jnp.zeros((pad, D_MODEL), F32)], axis=0)
    tgt_p = jnp.concatenate([jnp.zeros((N_META, D_MODEL), F32), tgt, jnp.zeros((pad, D_MODEL), F32)], axis=0)

    def gather_of(l, names):
        return _Exchange("gather", [w_blocks[l][n] for n in names])

    saved = []
    h, hb = h0, h0.astype(BF16)
    G = dict(gathered_up1)
    for l in range(DEPTH):
        first = l == 0
        sv = dict(h_in=h, h_in_b=hb, up1=G["ffn1_w_up"])
        names = ("ffn1_w_down", "mix_w_in")
        gu, a, *got = _ffn_up(hb, sv["up1"], name=f"ffn_up_{l}a", tm=tms,
                              exchange=gather_of(0, names) if first else None)
        G.update(zip(names, got))
        sv["down1"] = _from_blocks(G["ffn1_w_down"], 0)
        names = tuple(n for n in _MIXER if n != "mix_w_in")
        r, h1, h1b, *got = _mm_res_ln(a, sv["down1"], h, S["ln_g"][l, 0:1], S["ln_b"][l, 0:1], scale=0.5,
                                      name=f"ffn_down_ln_{l}a", tm=tm,
                                      exchange=gather_of(0, names) if first else None)
        G.update(zip(names, got))
        sv["f1"] = dict(gu=gu, a=a, r=r)
        lw = _mixer_weights(G)
        sv["lw"] = lw
        z = _mm(h1b, lw["w_in"], out_dtype=F32, name=f"mix_in_{l}", tm=tms, tn=1024, tk=D_MODEL)
        conv_w8 = _row8(S["conv_w"][l])
        bg8 = _row8(S["mix_b_gate"][l])
        yc = _conv_fwd(z, conv_w8, name=f"conv_fwd_{l}", tm=tms)
        gq, gkv = S["q_norm_g"][l:l + 1], S["kv_norm_g"][l:l + 1]
        q, k, v, qn, kin = _qkv_proj(z, gq, gkv, lw["wq"], lw["wk"], lw["wv_ext"], tabs, name=f"qkv_proj_{l}", tm=tm)
        if first:
            exchange = _Exchange("gather", [w_blocks[1][n] for n in _FFN1 + _MIXER] + [w_blocks[0]["ffn2_w_up"]])
        else:
            exchange = gather_of(1, _FFN2)
        o, lse, *nxt = _attn_fwd(q, k, v, name=f"attn_fwd_{l}", blk=blk, exchange=exchange)
        if first:
            G["ffn2_w_up"] = nxt.pop()
        else:
            G.update(zip(_FFN2, nxt))
        sv["up2"] = G["ffn2_w_up"]
        mg, pa, pb = _merge(yc, o, lw["wbc"], lw["wbm"], z, bg8, name=f"merge_{l}", tm=tm)
        r2, h2, h2b = _mm_res_ln(mg, lw["wo"], h1, S["ln_g"][l, 1:2], S["ln_b"][l, 1:2], scale=1.0,
                                 name=f"wo_ln_{l}", tm=tm)
        gu, a, *got = _ffn_up(h2b, sv["up2"], name=f"ffn_up_{l}b", tm=tms,
                              exchange=gather_of(0, _FFN2[1:]) if first else None)
        G.update(zip(_FFN2[1:], got))
        sv["down2"] = _from_blocks(G["ffn2_w_down"], 0)
        r, h3, h3b = _mm_res_ln(a, sv["down2"], h2, S["ln_g"][l, 2:3], S["ln_b"][l, 2:3], scale=0.5,
                                name=f"ffn_down_ln_{l}b", tm=tm)
        sv["f2"] = dict(gu=gu, a=a, r=r)
        if first:
            G = dict(zip(_FFN1 + _MIXER, nxt))
        sv.update(h1b=h1b, z=z, conv_w8=conv_w8, bg8=bg8, yc=yc, gq=gq, gkv=gkv, q=q, k=k, v=v, qn=qn, kin=kin,
                  o=o, lse=lse, mg=mg, pa=pa, pb=pb, r2=r2, h2b=h2b)
        saved.append(sv)
        h, hb = h3, h3b

    dh, loss8 = _loss_head(h, tgt_p, t_real=t_real, name="loss_head", tm=tm)

    tk = _tile(T, (2816, 768, 256, 128))
    grads = {n: [None] * DEPTH for n, _ in _BIG}
    for n in ("mix_b_gate", "conv_w", "q_norm_g", "kv_norm_g"):
        grads[n] = [None] * DEPTH
    grads["ln_g"] = [[None] * 3 for _ in range(DEPTH)]
    grads["ln_b"] = [[None] * 3 for _ in range(DEPTH)]

    def dest_of(l, names):
        return {n: grads[n][l] if n.endswith("w_up") else _to_dest(grads[n][l], axes[n] - 1) for n in names}

    def scatter_of(dest):
        return _Exchange("scatter", [dest[n].astype(BF16) for n in dest])

    dest = [{}, {}]
    received = [{}, {}]

    def ffn_bwd(dy, f, h_in_b, up8, down, g, tag, names, l, carry=None):
        dr, dfb, dg, db = _ln_bwd(dy, f["r"], g, scale=0.5, name=f"ln_bwd_{tag}", tm=tm)
        dgu, *got = _ffn_bwd_mid(dfb, down, f["gu"], name=f"ffn_bwd_mid_{tag}", tm=tms,
                                 exchange=scatter_of(carry) if carry else None)
        if carry:
            received[l].update(zip(carry, got))
        grads[names[1]][l] = _mm_tn_call(
            f["a"], dfb,
            pl.BlockSpec((None, tk, FF_BLK), lambda i, j, k: (i, k, 0)),
            pl.BlockSpec((tk, D_MODEL), lambda i, j, k: (k, 0)),
            out_shape=jax.ShapeDtypeStruct((D_FF, D_MODEL), F32),
            out_spec=pl.BlockSpec((FF_BLK, D_MODEL), lambda i, j, k: (i, 0)),
            grid=(FF_HALF_BLOCKS, 1, T // tk), name=f"dw_down_{tag}")
        own_down = dest_of(l, names[1:]) if carry else None
        d_up = _mm_tn_call(
            h_in_b, dgu,
            pl.BlockSpec((tk, D_MODEL), lambda i, j, k: (k, 0)),
            pl.BlockSpec((None, None, tk, FF_BLK),
                         lambda i, j, k: (j // FF_HALF_BLOCKS, j % FF_HALF_BLOCKS, k, 0)),
            out_shape=jax.ShapeDtypeStruct((N_DEV, D_MODEL, FF_BLK), F32),
            out_spec=pl.BlockSpec((None, D_MODEL, FF_BLK), lambda i, j, k: (j, 0, 0)),
            grid=(1, N_DEV, T // tk), name=f"dw_up_{tag}",
            exchange=scatter_of(own_down) if carry else None)
        own = None
        if carry:
            d_up, *got = d_up
            received[l].update(zip(own_down, got))
        grads[names[0]][l] = d_up
        if carry:
            own = dest_of(l, names[:1])
            dest[l].update(own_down)
            dest[l].update(own)
        row = pl.BlockSpec((tms, D_MODEL), lambda i, j, k: (i, 0))
        dh_in = _mm_call(
            dgu, up8,
            pl.BlockSpec((None, FF_HALF_BLOCKS, tms, FF_BLK), lambda i, j, k: (k, 0, i, 0)),
            pl.BlockSpec((FF_HALF_BLOCKS, D_MODEL, FF_BLK), lambda i, j, k: (k, 0, 0)),
            out_shape=jax.ShapeDtypeStruct((T, D_MODEL), F32), out_spec=row, acc_shape=(tms, D_MODEL),
            grid=(T // tms, 1, 2), name=f"ffn_dx_{tag}", trans_b=True, res=dr, res_spec=row,
            res_scale=ALPHA, pieces=FF_HALF_BLOCKS, exchange=scatter_of(own) if own else None)
        if own:
            dh_in, *got = dh_in
            received[l].update(zip(own, got))
        return dh_in, dg[0], db[0]

    for l in reversed(range(DEPTH)):
        sv = saved[l]
        lw = sv["lw"]
        dh, grads["ln_g"][l][2], grads["ln_b"][l][2] = ffn_bwd(
            dh, sv["f2"], sv["h2b"], sv["up2"], sv["down2"], S["ln_g"][l, 2:3], f"{l}b", _FFN2, l)
        dr2, dmb, dg, db = _ln_bwd(dh, sv["r2"], S["ln_g"][l, 1:2], scale=1.0, name=f"ln_bwd_{l}m", tm=tm)
        grads["ln_g"][l][1], grads["ln_b"][l][1] = dg[0], db[0]
        grads["w_o"][l] = _mm_tn(sv["mg"], dmb, name=f"dw_o_{l}", tm=D_MODEL, tn=D_MODEL, tk=tk)
        dpa, dpb, dgate, dbg = _wo_bwd(dmb, lw["wo"], sv["z"], sv["bg8"], sv["pa"], sv["pb"], name=f"wo_bwd_{l}", tm=tm)
        grads["mix_b_gate"][l] = dbg[0:2]
        grads["w_br_conv"][l] = _mm_tn(sv["yc"], dpa, name=f"dw_br_conv_{l}", tm=D_CONV, tn=D_MODEL, tk=tk)
        grads["w_br_mla"][l] = _mm_tn(sv["o"], dpb, name=f"dw_br_mla_{l}", tm=D_MLA, tn=D_MODEL, tk=tk)
        dyc = _mm(dpa, lw["wbc"], trans_b=True, out_dtype=F32, name=f"d_yconv_{l}", tm=tms, tn=D_CONV, tk=D_MODEL)
        dym = _mm(dpb, lw["wbm"], trans_b=True, out_dtype=BF16, name=f"d_ymla_{l}", tm=tms, tn=D_MLA, tk=D_MODEL)
        dz_conv, dcw = _conv_bwd(dyc, sv["z"], sv["conv_w8"], name=f"conv_bwd_{l}", tm=tms)
        grads["conv_w"][l] = dcw[0:CONV_WIDTH]
        if l == 0:
            dest[1] = dest_of(1, big_names)
            dest[0].update(dest_of(0, _FFN2 + _MIXER_EARLY))
            sent = {(1, n): dest[1][n] for n in big_names}
            sent.update({(0, n): dest[0][n] for n in _FFN2 + _MIXER_EARLY})
            dq, dk, dv, *got = _attn_bwd(
                sv["q"], sv["k"], sv["v"], sv["o"], dym, sv["lse"], name=f"attn_bwd_{l}", blk=blk,
                exchange=scatter_of(sent))
            for (layer, n), r in zip(sent, got):
                received[layer][n] = r
        else:
            dq, dk, dv = _attn_bwd(sv["q"], sv["k"], sv["v"], sv["o"], dym, sv["lse"], name=f"attn_bwd_{l}", blk=blk)
        dz_mid, dqb, dkb, dvb, dgq, dgkv = _qkv_bwd(dq, dk, dv, sv["z"], sv["gq"], sv["gkv"], lw["wq"], lw["wk"],
                                                    lw["wv"], tabs, name=f"qkv_bwd_{l}", tm=tm)
        grads["q_norm_g"][l], grads["kv_norm_g"][l] = dgq[0], dgkv[0]
        d_wq = _mm_tn(sv["qn"], dqb, name=f"dw_uq_{l}", tm=Q_LORA, tn=D_QK, tk=tk)
        d_wk = _mm_tn(sv["kin"], dkb, name=f"dw_uk_{l}", tm=Q_LORA, tn=D_QK, tk=tk)
        d_wv = _mm_tn(sv["kin"], dvb, name=f"dw_uv_{l}", tm=Q_LORA, tn=D_MLA, tk=tk)
        grads["w_uq"][l] = d_wq.reshape(Q_LORA, MLA_HEADS, HEAD_PAD)[:, :, :QK_NOPE + QK_ROPE].reshape(Q_LORA, -1)
        d_kn = d_wk[:KV_LORA].reshape(KV_LORA, MLA_HEADS, HEAD_PAD)[:, :, :QK_NOPE]
        d_vv = d_wv[:KV_LORA].reshape(KV_LORA, MLA_HEADS, V_HEAD)
        grads["w_ukv"][l] = jnp.concatenate([d_kn, d_vv], axis=-1).reshape(KV_LORA, -1)
        dz = jnp.concatenate([dz_conv, dz_mid, dgate], axis=1)
        d_win = _mm_tn(sv["h1b"], dz, name=f"dw_in_{l}", tm=D_MODEL, tn=1024, tk=tk)
        grads["mix_w_in"][l] = jnp.concatenate([d_win[:, :Z_KR_END], d_win[:, Z_KR_END + D_IN_PAD - D_IN_REAL:]], axis=1)
        dh = _mm(dz, lw["w_in"], trans_b=True, out_dtype=F32, name=f"mix_dx_{l}", res=dr2, res_scale=ALPHA,
                 tm=tms, tn=D_MODEL, tk=2048)
        carry = None
        if l == 0:
            carry = dest_of(0, [n for n in _MIXER if n not in _MIXER_EARLY])
            dest[0].update(carry)
        dh, grads["ln_g"][l][0], grads["ln_b"][l][0] = ffn_bwd(
            dh, sv["f1"], sv["h_in_b"], sv["up1"], sv["down1"], S["ln_g"][l, 0:1], f"{l}a", _FFN1, l, carry=carry)

    small = {n: jnp.stack(grads[n]) for n in ("mix_b_gate", "conv_w", "q_norm_g", "kv_norm_g")}
    small["ln_g"] = jnp.stack([jnp.stack(g) for g in grads["ln_g"]])
    small["ln_b"] = jnp.stack([jnp.stack(g) for g in grads["ln_b"]])
    small["meta_tokens"] = dh[:N_META]
    return loss8, dh[N_META:t_real], dest, received, small


def kernel(x, meta_tokens, ffn1_w_up, ffn1_w_down, mix_w_in, mix_b_gate, conv_w, q_norm_g, w_uq, kv_norm_g, w_ukv, w_br_conv, w_br_mla, w_o, ffn2_w_up, ffn2_w_down, ln_g, ln_b, loss_target, m_meta_tokens, m_ffn1_w_up, m_ffn1_w_down, m_mix_w_in, m_mix_b_gate, m_conv_w, m_q_norm_g, m_w_uq, m_kv_norm_g, m_w_ukv, m_w_br_conv, m_w_br_mla, m_w_o, m_ffn2_w_up, m_ffn2_w_down, m_ln_g, m_ln_b, v_meta_tokens, v_ffn1_w_up, v_ffn1_w_down, v_mix_w_in, v_mix_b_gate, v_conv_w, v_q_norm_g, v_w_uq, v_kv_norm_g, v_w_ukv, v_w_br_conv, v_w_br_mla, v_w_o, v_ffn2_w_up, v_ffn2_w_down, v_ln_g, v_ln_b):
    names = ["meta_tokens", "ffn1_w_up", "ffn1_w_down", "mix_w_in", "mix_b_gate", "conv_w", "q_norm_g", "w_uq",
             "kv_norm_g", "w_ukv", "w_br_conv", "w_br_mla", "w_o", "ffn2_w_up", "ffn2_w_down", "ln_g", "ln_b"]
    w = dict(zip(names, (meta_tokens, ffn1_w_up, ffn1_w_down, mix_w_in, mix_b_gate, conv_w, q_norm_g, w_uq,
                         kv_norm_g, w_ukv, w_br_conv, w_br_mla, w_o, ffn2_w_up, ffn2_w_down, ln_g, ln_b)))
    m = dict(zip(names, (m_meta_tokens, m_ffn1_w_up, m_ffn1_w_down, m_mix_w_in, m_mix_b_gate, m_conv_w, m_q_norm_g,
                         m_w_uq, m_kv_norm_g, m_w_ukv, m_w_br_conv, m_w_br_mla, m_w_o, m_ffn2_w_up, m_ffn2_w_down,
                         m_ln_g, m_ln_b)))
    v = dict(zip(names, (v_meta_tokens, v_ffn1_w_up, v_ffn1_w_down, v_mix_w_in, v_mix_b_gate, v_conv_w, v_q_norm_g,
                         v_w_uq, v_kv_norm_g, v_w_ukv, v_w_br_conv, v_w_br_mla, v_w_o, v_ffn2_w_up, v_ffn2_w_down,
                         v_ln_g, v_ln_b)))
    ix, iy, ic = lax.axis_index("x"), lax.axis_index("y"), lax.axis_index("c")
    dev = 4 * ix + 2 * iy + ic

    big_names = [n for n, _ in _BIG]
    small_names = [n for n, _ in _SMALL_SHARDED]
    small_axes = [a for _, a in _SMALL_SHARDED]
    small_shapes = [w[n].shape for n in small_names]
    gathered = _all_gather([w[n][0].astype(BF16) for n in _FFN1[:1]] + [_pack([w[n] for n in small_names], F32)],
                           name="all_gather_weights")
    gathered_up1 = dict(zip(_FFN1[:1], gathered[:-1]))
    w_blocks = [{n: w[n][l].astype(BF16) for n in big_names} for l in range(DEPTH)]
    S = dict(zip(small_names, _unpack_gathered(gathered[-1], small_shapes, small_axes)))
    S["q_norm_g"], S["kv_norm_g"] = q_norm_g, kv_norm_g

    loss8, grad_x, dest, received, G = _local_step(x[0], loss_target[0], gathered_up1, w_blocks, S)

    my_dev = dev.reshape(1).astype(jnp.int32)
    big_res = [{}, {}, {}, {}]
    for n in big_names:
        res = [_adamw(w[n], m[n], v[n], l, [(dest[l][n], None)] + [(received[l][n], k) for k in range(N_DEV - 1)],
                      my_dev, name=f"adamw{l}_{n}") for l in range(DEPTH)]
        for kind in range(4):
            big_res[kind][n] = jnp.stack([r[kind] for r in res])

    small_all = small_names + list(_SMALL_REPL)
    part = _pack([G[n] for n in small_all] + [loss8[0, 0:1]], F32)
    full_shapes = [G[n].shape for n in small_all] + [(1,)]
    summed = _sum8(_all_gather([part], name="all_gather_small_grads")[0], name="sum_small_grads")
    unpacked = _unpack(summed, full_shapes)
    loss = unpacked[-1][0]
    g_full = dict(zip(small_all, unpacked[:-1]))
    g_loc = []
    for n in small_all:
        if n in _SMALL_REPL:
            g_loc.append(g_full[n])
        else:
            ax = dict(_SMALL_SHARDED)[n]
            g_loc.append(lax.dynamic_slice_in_dim(g_full[n], dev * w[n].shape[ax], w[n].shape[ax], axis=ax))
    loc_shapes = [w[n].shape for n in small_all]
    g_pack = _pack(g_loc, F32)
    small_out = _adamw(_pack([w[n] for n in small_all], F32)[None], _pack([m[n] for n in small_all], F32)[None],
                       _pack([v[n] for n in small_all], F32)[None], 0, [(g_pack[None], 0)],
                       jnp.zeros((1,), jnp.int32), name="adamw_small")
    small_res = [dict(zip(small_all, _unpack(o, loc_shapes))) for o in small_out]

    outs = [loss, grad_x[None]]
    for kind in range(4):
        for n in names:
            outs.append(big_res[kind][n] if n in big_res[kind] else small_res[kind][n])
    return tuple(outs)
```

```python
import functools

import numpy as np
import jax
import jax.numpy as jnp
from jax import lax
from jax.experimental import pallas as pl
from jax.experimental.pallas import tpu as pltpu

F32 = jnp.float32
BF16 = jnp.bfloat16

D_MODEL = 1024
DEPTH = 2
N_META = 16
D_CONV = 512
CONV_WIDTH = 3
MLA_HEADS = 8
QK_NOPE = 64
QK_ROPE = 32
V_HEAD = 64
Q_LORA = 256
KV_LORA = 128
D_MLA = MLA_HEADS * V_HEAD
ROPE_BASE = 10000.0
NEG_INF = -1e30
D_FF = 2816
ALPHA = (2 * DEPTH) ** 0.25
LN_EPS = 1e-5
RMS_EPS = 1e-6
ATTN_SCALE = (QK_NOPE + QK_ROPE) ** -0.5
ADAM_LR = 0.001
ADAM_B1 = 0.9
ADAM_B2 = 0.999
ADAM_EPS = 1e-08
ADAM_WD = 0.01
ADAM_STEP = 10

N_DEV = 8
HEAD_PAD = 128
HEADS_PER_STEP = 4
FWD_HEADS_PER_STEP = 4
FF_BLK = 2 * D_FF // N_DEV
FF_HALF_BLOCKS = N_DEV // 2
D_QK = MLA_HEADS * HEAD_PAD
Z_CONV = 0
Z_MID = 1536
Z_GATE = 2048
D_IN_PAD = 4096
D_IN_REAL = 4000
Z_KR_END = Z_MID + Q_LORA + KV_LORA + QK_ROPE

V7X_VMEM_LIMIT = 56 * 1024 * 1024
LANE = 128
ROW_ALIGN = 256


def _tile(n, cands):
    for c in cands:
        if n % c == 0:
            return c
    raise ValueError(f"no tile for {n} in {cands}")


def _params(sem):
    return pltpu.CompilerParams(dimension_semantics=sem, vmem_limit_bytes=V7X_VMEM_LIMIT)


def _mm_call(a, b, a_spec, b_spec, *, out_shape, out_spec, acc_shape, grid, name, trans_b=False,
             res=None, res_spec=None, res_scale=1.0, pieces=1, exchange=None):
    nk = grid[2]
    has_res = res is not None
    out_dtype = out_shape.dtype
    dims = (((1,), (1,)), ((), ())) if trans_b else (((1,), (0,)), ((), ()))

    def body(*refs):
        if has_res:
            a_ref, b_ref, r_ref, o_ref, acc = refs
        else:
            a_ref, b_ref, o_ref, acc = refs
        k = pl.program_id(2)
        if pieces == 1:
            part = lax.dot_general(a_ref[...], b_ref[...], dims, preferred_element_type=F32)
        else:
            part = lax.dot_general(a_ref[0], b_ref[0], dims, preferred_element_type=F32)
            for p in range(1, pieces):
                part = part + lax.dot_general(a_ref[p], b_ref[p], dims, preferred_element_type=F32)

        @pl.when(k == 0)
        def _():
            acc[...] = part

        @pl.when(k > 0)
        def _():
            acc[...] += part

        @pl.when(k == nk - 1)
        def _():
            out = acc[...]
            if has_res:
                out = out + res_scale * r_ref[...]
            o_ref[...] = out.astype(out_dtype)

    in_specs = [a_spec, b_spec]
    args = [a, b]
    if has_res:
        in_specs.append(res_spec)
        args.append(res)
    if exchange is None:
        return pl.pallas_call(
            body, name=name, grid=grid, in_specs=in_specs, out_specs=out_spec, out_shape=out_shape,
            scratch_shapes=[pltpu.VMEM(acc_shape, F32)],
            compiler_params=_params(("parallel", "parallel", "arbitrary")),
        )(*args)
    ex_args, ex_specs, ex_out, ex_scratch = _exchange_operands(exchange)
    return pl.pallas_call(
        _carry_exchange(body, exchange, 0, len(args), 1, grid), name=name, grid=grid,
        in_specs=in_specs + ex_specs, out_specs=[out_spec] + ex_specs, out_shape=[out_shape] + ex_out,
        scratch_shapes=[pltpu.VMEM(acc_shape, F32)] + ex_scratch,
        compiler_params=_params(("arbitrary", "arbitrary", "arbitrary")),
    )(*args, *ex_args)


def _mm(a, b, *, out_dtype, name, trans_b=False, res=None, res_scale=1.0, tm, tn, tk, exchange=None):
    M, K = a.shape
    N = b.shape[0] if trans_b else b.shape[1]
    assert M % tm == 0 and N % tn == 0 and K % tk == 0
    b_spec = (pl.BlockSpec((tn, tk), lambda i, j, k: (j, k)) if trans_b
              else pl.BlockSpec((tk, tn), lambda i, j, k: (k, j)))
    tile = pl.BlockSpec((tm, tn), lambda i, j, k: (i, j))
    return _mm_call(a, b, pl.BlockSpec((tm, tk), lambda i, j, k: (i, k)), b_spec,
                    out_shape=jax.ShapeDtypeStruct((M, N), out_dtype), out_spec=tile, acc_shape=(tm, tn),
                    grid=(M // tm, N // tn, K // tk), name=name, trans_b=trans_b,
                    res=res, res_spec=tile, res_scale=res_scale, exchange=exchange)


def _mm_tn_call(a, b, a_spec, b_spec, *, out_shape, out_spec, grid, name, exchange=None):
    def body(a_ref, b_ref, o_ref):
        k = pl.program_id(2)
        part = lax.dot_general(a_ref[...], b_ref[...], (((0,), (0,)), ((), ())),
                               preferred_element_type=F32)

        @pl.when(k == 0)
        def _():
            o_ref[...] = part

        @pl.when(k > 0)
        def _():
            o_ref[...] += part

    if exchange is None:
        return pl.pallas_call(
            body, name=name, grid=grid, in_specs=[a_spec, b_spec], out_specs=out_spec, out_shape=out_shape,
            compiler_params=_params(("parallel", "parallel", "arbitrary")),
        )(a, b)
    ex_args, ex_specs, ex_out, ex_scratch = _exchange_operands(exchange)
    return pl.pallas_call(
        _carry_exchange(body, exchange, 0, 2, 1, grid), name=name, grid=grid,
        in_specs=[a_spec, b_spec] + ex_specs, out_specs=[out_spec] + ex_specs, out_shape=[out_shape] + ex_out,
        scratch_shapes=ex_scratch,
        compiler_params=_params(("arbitrary", "arbitrary", "arbitrary")),
    )(a, b, *ex_args)


def _mm_tn(a, b, *, name, tm, tn, tk):
    T, M = a.shape
    N = b.shape[1]
    assert M % tm == 0 and N % tn == 0 and T % tk == 0
    return _mm_tn_call(a, b, pl.BlockSpec((tk, tm), lambda i, j, k: (k, i)),
                       pl.BlockSpec((tk, tn), lambda i, j, k: (k, j)),
                       out_shape=jax.ShapeDtypeStruct((M, N), F32),
                       out_spec=pl.BlockSpec((tm, tn), lambda i, j, k: (i, j)),
                       grid=(M // tm, N // tn, T // tk), name=name)


def _ffn_up(hb, w_up8, *, name, tm, exchange=None):
    T = hb.shape[0]

    def body(h_ref, wg_ref, wu_ref, gu_ref, a_ref):
        h = h_ref[...]
        g = jnp.dot(h, wg_ref[...], preferred_element_type=F32)
        u = jnp.dot(h, wu_ref[...], preferred_element_type=F32)
        sg = jax.nn.sigmoid(g)
        silu = g * sg
        gu_ref[0] = (u * (sg * (1.0 + g * (1.0 - sg)))).astype(BF16)
        gu_ref[1] = silu.astype(BF16)
        a_ref[...] = (silu * u).astype(BF16)

    grid = (FF_HALF_BLOCKS, T // tm)
    ex_args, ex_specs, ex_out, ex_scratch = _exchange_operands(exchange)
    return pl.pallas_call(
        _carry_exchange(body, exchange, 0, 3, 2, grid), name=name, grid=grid,
        in_specs=[pl.BlockSpec((tm, D_MODEL), lambda j, i: (i, 0)),
                  pl.BlockSpec((None, D_MODEL, FF_BLK), lambda j, i: (j, 0, 0)),
                  pl.BlockSpec((None, D_MODEL, FF_BLK), lambda j, i: (j + FF_HALF_BLOCKS, 0, 0))] + ex_specs,
        out_specs=[pl.BlockSpec((2, None, tm, FF_BLK), lambda j, i: (0, j, i, 0)),
                   pl.BlockSpec((None, tm, FF_BLK), lambda j, i: (j, i, 0))] + ex_specs,
        out_shape=[jax.ShapeDtypeStruct((2, FF_HALF_BLOCKS, T, FF_BLK), BF16),
                   jax.ShapeDtypeStruct((FF_HALF_BLOCKS, T, FF_BLK), BF16)] + ex_out,
        scratch_shapes=ex_scratch,
        compiler_params=_params(("arbitrary", "arbitrary") if exchange else ("parallel", "parallel")),
    )(hb, w_up8, w_up8, *ex_args)


def _mm_res_ln(a, w, res, g, b, *, scale, name, tm, exchange=None):
    split = a.ndim == 3
    if split:
        S, T, Ks = a.shape
        K = S * Ks
    else:
        T, K = a.shape

    def body(a_ref, w_ref, res_ref, g_ref, b_ref, r_ref, y_ref, yb_ref):
        if split:
            f = jnp.dot(a_ref[0], w_ref[0:Ks, :], preferred_element_type=F32)
            for s in range(1, S):
                f = f + jnp.dot(a_ref[s], w_ref[s * Ks:(s + 1) * Ks, :], preferred_element_type=F32)
        else:
            f = jnp.dot(a_ref[...], w_ref[...], preferred_element_type=F32)
        r = ALPHA * res_ref[...] + scale * f
        mu = jnp.mean(r, axis=-1, keepdims=True)
        xc = r - mu
        var = jnp.mean(xc * xc, axis=-1, keepdims=True)
        y = xc * lax.rsqrt(var + LN_EPS) * g_ref[...] + b_ref[...]
        r_ref[...] = r
        y_ref[...] = y
        yb_ref[...] = y.astype(BF16)

    row = pl.BlockSpec((tm, D_MODEL), lambda i: (i, 0))
    vec = pl.BlockSpec((1, D_MODEL), lambda i: (0, 0))
    grid = (T // tm,)
    ex_args, ex_specs, ex_out, ex_scratch = _exchange_operands(exchange)
    return pl.pallas_call(
        _carry_exchange(body, exchange, 0, 5, 3, grid), name=name, grid=grid,
        in_specs=[pl.BlockSpec((S, tm, Ks), lambda i: (0, i, 0)) if split else pl.BlockSpec((tm, K), lambda i: (i, 0)),
                  pl.BlockSpec((K, D_MODEL), lambda i: (0, 0)), row, vec, vec] + ex_specs,
        out_specs=[row, row, row] + ex_specs,
        out_shape=[jax.ShapeDtypeStruct((T, D_MODEL), F32), jax.ShapeDtypeStruct((T, D_MODEL), F32),
                   jax.ShapeDtypeStruct((T, D_MODEL), BF16)] + ex_out,
        scratch_shapes=ex_scratch,
        compiler_params=_params(("arbitrary",) if exchange else ("parallel",)),
    )(a, w, res, g, b, *ex_args)


def _conv_fwd(z, conv_w8, *, name, tm):
    T = z.shape[0]
    hb = tm // 8

    def body(b_ref, c_ref, h_ref, cp_ref, hp_ref, w_ref, y_ref):
        i = pl.program_id(0)
        u = c_ref[...] * h_ref[...]
        up = jnp.where(i > 0, cp_ref[...] * hp_ref[...], 0.0)
        ue = jnp.concatenate([up, u], axis=0)
        s1 = pltpu.roll(ue, 1, 0)[8:]
        s2 = pltpu.roll(ue, 2, 0)[8:]
        w = w_ref[...]
        conv = w[0:1] * s2 + w[1:2] * s1 + w[2:3] * u
        y_ref[...] = (b_ref[...] * conv).astype(BF16)

    def col(c):
        return pl.BlockSpec((tm, D_CONV), lambda i: (i, c))

    def prev(c):
        return pl.BlockSpec((8, D_CONV), lambda i: (jnp.maximum(i * hb - 1, 0), c))

    return pl.pallas_call(
        body, name=name, grid=(T // tm,),
        in_specs=[col(0), col(1), col(2), prev(1), prev(2), pl.BlockSpec((8, D_CONV), lambda i: (0, 0))],
        out_specs=pl.BlockSpec((tm, D_CONV), lambda i: (i, 0)),
        out_shape=jax.ShapeDtypeStruct((T, D_CONV), BF16),
        compiler_params=_params(("parallel",)),
    )(z, z, z, z, z, conv_w8)


def _rope(x, c, s1, s2):
    n = x.shape[-1]
    return x * c + pltpu.roll(x, 16, 1) * s1 + pltpu.roll(x, n - 16, 1) * s2


def _rope_t(d, c, s1, s2):
    n = d.shape[-1]
    return d * c + pltpu.roll(d * s1, n - 16, 1) + pltpu.roll(d * s2, 16, 1)


def _rms(x, g):
    rstd = lax.rsqrt(jnp.mean(x * x, axis=-1, keepdims=True) + RMS_EPS)
    return x * rstd * g


def _qkv_proj(z, gq, gkv, wq, wk, wv_ext, tabs, *, name, tm):
    T = z.shape[0]

    def body(z_ref, gq_ref, gkv_ref, wq_ref, wk_ref, wv_ref, c_ref, s1_ref, s2_ref,
             q_ref, k_ref, v_ref, qn_ref, kin_ref):
        zz = z_ref[...]
        qn = _rms(zz[:, :Q_LORA], gq_ref[...]).astype(BF16)
        kvn = _rms(zz[:, Q_LORA:Q_LORA + KV_LORA], gkv_ref[...]).astype(BF16)
        kin = jnp.concatenate([kvn, zz[:, Q_LORA + KV_LORA:].astype(BF16)], axis=-1)
        c = jnp.tile(c_ref[...], (1, MLA_HEADS))
        s1 = jnp.tile(s1_ref[...], (1, MLA_HEADS))
        s2 = jnp.tile(s2_ref[...], (1, MLA_HEADS))
        qpre = jnp.dot(qn, wq_ref[...], preferred_element_type=F32)
        kpre = jnp.dot(kin, wk_ref[...], preferred_element_type=F32)
        q_ref[...] = (_rope(qpre, c, s1, s2) * (ATTN_SCALE * LOG2_E)).astype(BF16)
        k_ref[...] = _rope(kpre, c, s1, s2).astype(BF16)
        vv = jnp.dot(kvn, wv_ref[...], preferred_element_type=F32)
        lane = lax.broadcasted_iota(jnp.int32, vv.shape, 1)
        v_ref[...] = jnp.where((lane & (HEAD_PAD - 1)) < V_HEAD, vv, 1.0).astype(BF16)
        qn_ref[...] = qn
        kin_ref[...] = kin

    def full(shape):
        return pl.BlockSpec(shape, lambda i: (0, 0))

    def rows(w, c=0):
        return pl.BlockSpec((tm, w), lambda i: (i, c))

    return pl.pallas_call(
        body, name=name, grid=(T // tm,),
        in_specs=[rows(512, Z_MID // 512), full((1, Q_LORA)), full((1, KV_LORA)),
                  full((Q_LORA, D_QK)), full((Q_LORA, D_QK)), full((KV_LORA, D_QK)),
                  rows(LANE), rows(LANE), rows(LANE)],
        out_specs=[rows(D_QK), rows(D_QK), rows(D_QK), rows(Q_LORA), rows(Q_LORA)],
        out_shape=[jax.ShapeDtypeStruct((T, D_QK), BF16), jax.ShapeDtypeStruct((T, D_QK), BF16),
                   jax.ShapeDtypeStruct((T, D_QK), BF16), jax.ShapeDtypeStruct((T, Q_LORA), BF16),
                   jax.ShapeDtypeStruct((T, Q_LORA), BF16)],
        compiler_params=_params(("parallel",)),
    )(z, gq, gkv, wq, wk, wv_ext, *tabs)


SOFTMAX_ROWS = 32
LOG2_E = 1.4426950408889634
_NT = (((1,), (1,)), ((), ()))
_TN = (((0,), (0,)), ((), ()))


def _diag_mask(s, row0, col0=0):
    row = lax.broadcasted_iota(jnp.int32, s.shape, 0) + row0
    col = lax.broadcasted_iota(jnp.int32, s.shape, 1) + col0
    return jnp.where(col <= row, s, NEG_INF)


_RELATIONS = tuple((rx, ry, rc) for rx in (0, 1) for ry in (0, 1) for rc in (0, 1))[1:]


class _Exchange:
    def __init__(self, kind, arrays):
        assert kind in ("gather", "scatter")
        self.kind, self.arrays, self.n = kind, list(arrays), len(arrays)

    def out_shapes(self):
        if self.kind == "gather":
            return [jax.ShapeDtypeStruct((N_DEV,) + a.shape, a.dtype) for a in self.arrays]
        return [jax.ShapeDtypeStruct((N_DEV - 1,) + a.shape[1:], a.dtype) for a in self.arrays]

    def scratch_shapes(self):
        sems = [pltpu.SemaphoreType.DMA((7 * self.n,)), pltpu.SemaphoreType.DMA((7 * self.n,))]
        if self.kind == "gather":
            sems.append(pltpu.SemaphoreType.DMA((self.n,)))
        return sems

    def _copies(self, src_refs, out_refs, sems):
        x, y, c = lax.axis_index("x"), lax.axis_index("y"), lax.axis_index("c")
        me = 4 * x + 2 * y + c
        sends, recvs, local = [], [], []
        for a in range(self.n):
            for k, rel in enumerate(_RELATIONS):
                peer = tuple((1 - p) if r else p for p, r in zip((x, y, c), rel))
                peer_index = 4 * peer[0] + 2 * peer[1] + peer[2]
                if self.kind == "gather":
                    src, lands_there, lands_here = src_refs[a], out_refs[a].at[me], out_refs[a].at[peer_index]
                else:
                    src, lands_there, lands_here = src_refs[a].at[peer_index], out_refs[a].at[k], out_refs[a].at[k]
                for dst, group in ((lands_there, sends), (lands_here, recvs)):
                    group.append(pltpu.make_async_remote_copy(
                        src_ref=src, dst_ref=dst, send_sem=sems[0].at[7 * a + k], recv_sem=sems[1].at[7 * a + k],
                        device_id=peer, device_id_type=_MESH_ID))
            if self.kind == "gather":
                local.append(pltpu.make_async_copy(src_refs[a], out_refs[a].at[me], sems[2].at[a]))
        return sends, recvs, local

    def start(self, src_refs, out_refs, sems):
        sends, _, local = self._copies(src_refs, out_refs, sems)
        for cp in local + sends:
            cp.start()

    def wait(self, src_refs, out_refs, sems):
        sends, recvs, local = self._copies(src_refs, out_refs, sems)
        for cp in recvs:
            cp.wait_recv()
        for cp in sends:
            cp.wait_send()
        for cp in local:
            cp.wait()


def _exchange_operands(exchange):
    if exchange is None:
        return [], [], [], []
    return exchange.arrays, [_ANY] * exchange.n, exchange.out_shapes(), exchange.scratch_shapes()


def _carry_exchange(body, exchange, n_prefetch, n_in, n_out, grid):
    if exchange is None:
        return body
    n = exchange.n
    last = tuple(g - 1 for g in grid)

    def wrapped(*refs):
        head = refs[:n_prefetch + n_in]
        src_refs = refs[n_prefetch + n_in:n_prefetch + n_in + n]
        rest = refs[n_prefetch + n_in + n:]
        outs, out_refs, rest = rest[:n_out], rest[n_out:n_out + n], rest[n_out + n:]
        n_sems = len(exchange.scratch_shapes())
        scratch, sems = rest[:len(rest) - n_sems], rest[len(rest) - n_sems:]
        at_first = functools.reduce(jnp.logical_and, [pl.program_id(d) == 0 for d in range(len(grid))])
        at_last = functools.reduce(jnp.logical_and, [pl.program_id(d) == last[d] for d in range(len(grid))])

        @pl.when(at_first)
        def _():
            exchange.start(src_refs, out_refs, sems)

        body(*head, *outs, *scratch)

        @pl.when(at_last)
        def _():
            exchange.wait(src_refs, out_refs, sems)

    return wrapped


def _attn_fwd(q, k, v, *, name, blk, exchange=None):
    T = q.shape[0]
    n = T // blk
    hp = FWD_HEADS_PER_STEP
    qi = np.array([i for i in range(n) for j in range(i + 1)], np.int32)
    kj = np.array([j for i in range(n) for j in range(i + 1)], np.int32)

    rc = _tile(blk, (SOFTMAX_ROWS,))

    def body(qi_ref, kj_ref, q_ref, k_ref, v_ref, o_ref, lse_ref, m_sc, acc_sc, s_sc, p_sc, red_sc):
        s_id = pl.program_id(1)
        i = qi_ref[s_id]
        j = kj_ref[s_id]

        @pl.when(j == 0)
        def _():
            m_sc[...] = jnp.full(m_sc.shape, NEG_INF, F32)
            acc_sc[...] = jnp.zeros(acc_sc.shape, F32)

        def head_step(hh, diagonal):
            hs = slice(hh * HEAD_PAD, (hh + 1) * HEAD_PAD)
            s_sc[hh] = lax.dot_general(q_ref[:, hs], k_ref[:, hs], _NT, preferred_element_type=F32)
            lanes = [slice(t * LANE, (t + 1) * LANE) for t in range(blk // LANE)]
            for r in range(blk // rc):
                rows = slice(r * rc, (r + 1) * rc)
                s = s_sc[hh, rows, :]
                if diagonal:
                    s = _diag_mask(s, r * rc)
                    s_sc[hh, rows, :] = s
                pm = s[:, lanes[0]]
                for t in lanes[1:]:
                    pm = jnp.maximum(pm, s[:, t])
                red_sc[hh, rows, :] = pm
            m_old = m_sc[hh]
            row_max = jnp.max(red_sc[hh], axis=-1, keepdims=True)
            m_new = jnp.maximum(m_old, jnp.broadcast_to(row_max, (blk, LANE)))
            a = jnp.exp2(m_old - m_new)
            m_sc[hh] = m_new
            for r in range(blk // rc):
                rows = slice(r * rc, (r + 1) * rc)
                mb = m_sc[hh, rows, :]
                for t in lanes:
                    p_sc[hh, rows, t] = jnp.exp2(s_sc[hh, rows, t] - mb).astype(BF16)
            acc_sc[hh] = a * acc_sc[hh] + jnp.dot(p_sc[hh], v_ref[:, hs], preferred_element_type=F32)

        @pl.when(j < i)
        def _():
            for hh in range(hp):
                head_step(hh, False)

        @pl.when(j == i)
        def _():
            for hh in range(hp):
                head_step(hh, True)
            for hh in range(hp):
                acc = acc_sc[hh]
                swapped = pltpu.roll(acc, V_HEAD, 1)
                o_ref[:, hh * V_HEAD:(hh + 1) * V_HEAD] = (acc / swapped)[:, :V_HEAD].astype(BF16)
                lane = lax.broadcasted_iota(jnp.int32, acc.shape, 1)
                denom = jnp.where(lane < V_HEAD, swapped, acc)
                lse_ref[hh] = m_sc[hh] + jnp.log(denom) * LOG2_E

    grid = (MLA_HEADS // hp, len(qi))
    ex_args, ex_specs, ex_out, ex_scratch = _exchange_operands(exchange)
    grid_spec = pltpu.PrefetchScalarGridSpec(
        num_scalar_prefetch=2, grid=grid,
        in_specs=[pl.BlockSpec((blk, hp * HEAD_PAD), lambda g, s, qi, kj: (qi[s], g)),
                  pl.BlockSpec((blk, hp * HEAD_PAD), lambda g, s, qi, kj: (kj[s], g)),
                  pl.BlockSpec((blk, hp * HEAD_PAD), lambda g, s, qi, kj: (kj[s], g))] + ex_specs,
        out_specs=[pl.BlockSpec((blk, hp * V_HEAD), lambda g, s, qi, kj: (qi[s], g)),
                   pl.BlockSpec((hp, blk, LANE), lambda g, s, qi, kj: (g, qi[s], 0))] + ex_specs,
        scratch_shapes=[pltpu.VMEM((hp, blk, LANE), F32), pltpu.VMEM((hp, blk, HEAD_PAD), F32),
                        pltpu.VMEM((hp, blk, blk), F32), pltpu.VMEM((hp, blk, blk), BF16),
                        pltpu.VMEM((hp, blk, LANE), F32)] + ex_scratch)
    return pl.pallas_call(
        _carry_exchange(body, exchange, 2, 3, 2, grid), name=name, grid_spec=grid_spec,
        out_shape=[jax.ShapeDtypeStruct((T, D_MLA), BF16),
                   jax.ShapeDtypeStruct((MLA_HEADS, T, LANE), F32)] + ex_out,
        compiler_params=_params(("arbitrary", "arbitrary") if exchange else ("parallel", "arbitrary")),
    )(jnp.asarray(qi), jnp.asarray(kj), q, k, v, *ex_args)


def _merge(yc, ym, wbc, wbm, z, bg, *, name, tm):
    T = yc.shape[0]

    def body(yc_ref, ym_ref, wbc_ref, wbm_ref, gc_ref, gm_ref, bg_ref, mg_ref, pa_ref, pb_ref):
        pa = jnp.dot(yc_ref[...], wbc_ref[...], preferred_element_type=F32)
        pb = jnp.dot(ym_ref[...], wbm_ref[...], preferred_element_type=F32)
        bgv = bg_ref[...]
        sa = jax.nn.sigmoid(gc_ref[...] + bgv[0:1])
        sb = jax.nn.sigmoid(gm_ref[...] + bgv[1:2])
        mg_ref[...] = (sa * pa + sb * pb).astype(BF16)
        pa_ref[...] = pa.astype(BF16)
        pb_ref[...] = pb.astype(BF16)

    row = pl.BlockSpec((tm, D_MODEL), lambda i: (i, 0))
    return pl.pallas_call(
        body, name=name, grid=(T // tm,),
        in_specs=[pl.BlockSpec((tm, D_CONV), lambda i: (i, 0)), pl.BlockSpec((tm, D_MLA), lambda i: (i, 0)),
                  pl.BlockSpec((D_CONV, D_MODEL), lambda i: (0, 0)), pl.BlockSpec((D_MLA, D_MODEL), lambda i: (0, 0)),
                  pl.BlockSpec((tm, D_MODEL), lambda i: (i, Z_GATE // D_MODEL)),
                  pl.BlockSpec((tm, D_MODEL), lambda i: (i, Z_GATE // D_MODEL + 1)),
                  pl.BlockSpec((8, D_MODEL), lambda i: (0, 0))],
        out_specs=[row, row, row],
        out_shape=[jax.ShapeDtypeStruct((T, D_MODEL), BF16)] * 3,
        compiler_params=_params(("parallel",)),
    )(yc, ym, wbc, wbm, z, z, bg)


def _merge_wo_ln(yc, ym, wbc, wbm, z, bg, wo, res, g, b, *, name, tm):
    T = yc.shape[0]

    def body(yc_ref, ym_ref, wbc_ref, wbm_ref, gc_ref, gm_ref, bg_ref, wo_ref, res_ref, g_ref, b_ref,
             mg_ref, pa_ref, pb_ref, r_ref, y_ref, yb_ref):
        pa = jnp.dot(yc_ref[...], wbc_ref[...], preferred_element_type=F32)
        pb = jnp.dot(ym_ref[...], wbm_ref[...], preferred_element_type=F32)
        bgv = bg_ref[...]
        sa = jax.nn.sigmoid(gc_ref[...] + bgv[0:1])
        sb = jax.nn.sigmoid(gm_ref[...] + bgv[1:2])
        mg = (sa * pa + sb * pb).astype(BF16)
        mg_ref[...] = mg
        pa_ref[...] = pa.astype(BF16)
        pb_ref[...] = pb.astype(BF16)
        r = ALPHA * res_ref[...] + jnp.dot(mg, wo_ref[...], preferred_element_type=F32)
        mu = jnp.mean(r, axis=-1, keepdims=True)
        xc = r - mu
        var = jnp.mean(xc * xc, axis=-1, keepdims=True)
        y = xc * lax.rsqrt(var + LN_EPS) * g_ref[...] + b_ref[...]
        r_ref[...] = r
        y_ref[...] = y
        yb_ref[...] = y.astype(BF16)

    row = pl.BlockSpec((tm, D_MODEL), lambda i: (i, 0))
    vec = pl.BlockSpec((1, D_MODEL), lambda i: (0, 0))
    return pl.pallas_call(
        body, name=name, grid=(T // tm,),
        in_specs=[pl.BlockSpec((tm, D_CONV), lambda i: (i, 0)), pl.BlockSpec((tm, D_MLA), lambda i: (i, 0)),
                  pl.BlockSpec((D_CONV, D_MODEL), lambda i: (0, 0)), pl.BlockSpec((D_MLA, D_MODEL), lambda i: (0, 0)),
                  pl.BlockSpec((tm, D_MODEL), lambda i: (i, Z_GATE // D_MODEL)),
                  pl.BlockSpec((tm, D_MODEL), lambda i: (i, Z_GATE // D_MODEL + 1)),
                  pl.BlockSpec((8, D_MODEL), lambda i: (0, 0)),
                  pl.BlockSpec((D_MODEL, D_MODEL), lambda i: (0, 0)), row, vec, vec],
        out_specs=[row] * 6,
        out_shape=[jax.ShapeDtypeStruct((T, D_MODEL), BF16)] * 3
        + [jax.ShapeDtypeStruct((T, D_MODEL), F32), jax.ShapeDtypeStruct((T, D_MODEL), F32),
           jax.ShapeDtypeStruct((T, D_MODEL), BF16)],
        compiler_params=_params(("parallel",)),
    )(yc, ym, wbc, wbm, z, z, bg, wo, res, g, b)


def _loss_head(h, tgt, *, t_real, name, tm):
    T = h.shape[0]

    def body(h_ref, t_ref, dy_ref, loss_ref):
        i = pl.program_id(0)
        row = lax.broadcasted_iota(jnp.int32, (tm, 1), 0) + i * tm
        valid = (row >= N_META) & (row < t_real)
        err = jnp.where(valid, h_ref[...] - t_ref[...], 0.0)
        dy_ref[...] = err * (1.0 / D_MODEL)
        part = 0.5 * jnp.sum(jnp.sum(err * err, axis=-1, keepdims=True) * (1.0 / D_MODEL), axis=0, keepdims=True)

        @pl.when(i == 0)
        def _():
            loss_ref[...] = jnp.zeros(loss_ref.shape, F32)

        loss_ref[...] += jnp.broadcast_to(part, loss_ref.shape)

    row_spec = pl.BlockSpec((tm, D_MODEL), lambda i: (i, 0))
    return pl.pallas_call(
        body, name=name, grid=(T // tm,),
        in_specs=[row_spec, row_spec],
        out_specs=[row_spec, pl.BlockSpec((8, LANE), lambda i: (0, 0))],
        out_shape=[jax.ShapeDtypeStruct((T, D_MODEL), F32), jax.ShapeDtypeStruct((8, LANE), F32)],
        compiler_params=_params(("arbitrary",)),
    )(h, tgt)


def _ln_bwd(dy, r, g, *, scale, name, tm):
    T = dy.shape[0]

    def body(dy_ref, r_ref, g_ref, dr_ref, drb_ref, dg_ref, db_ref):
        i = pl.program_id(0)
        rr = r_ref[...]
        dyv = dy_ref[...]
        mu = jnp.mean(rr, axis=-1, keepdims=True)
        xc = rr - mu
        rstd = lax.rsqrt(jnp.mean(xc * xc, axis=-1, keepdims=True) + LN_EPS)
        xh = xc * rstd
        dxh = dyv * g_ref[...]
        m1 = jnp.mean(dxh, axis=-1, keepdims=True)
        m2 = jnp.mean(dxh * xh, axis=-1, keepdims=True)
        dr = rstd * (dxh - m1 - xh * m2)
        dr_ref[...] = dr
        drb_ref[...] = (scale * dr).astype(BF16)

        @pl.when(i == 0)
        def _():
            dg_ref[...] = jnp.zeros(dg_ref.shape, F32)
            db_ref[...] = jnp.zeros(db_ref.shape, F32)

        dg_ref[0:1, :] += jnp.sum(dyv * xh, axis=0, keepdims=True)
        db_ref[0:1, :] += jnp.sum(dyv, axis=0, keepdims=True)

    row = pl.BlockSpec((tm, D_MODEL), lambda i: (i, 0))
    acc = pl.BlockSpec((8, D_MODEL), lambda i: (0, 0))
    return pl.pallas_call(
        body, name=name, grid=(T // tm,),
        in_specs=[row, row, pl.BlockSpec((1, D_MODEL), lambda i: (0, 0))],
        out_specs=[row, row, acc, acc],
        out_shape=[jax.ShapeDtypeStruct((T, D_MODEL), F32), jax.ShapeDtypeStruct((T, D_MODEL), BF16),
                   jax.ShapeDtypeStruct((8, D_MODEL), F32), jax.ShapeDtypeStruct((8, D_MODEL), F32)],
        compiler_params=_params(("arbitrary",)),
    )(dy, r, g)


def _ffn_bwd_mid(dfb, w_down, gu, *, name, tm, exchange=None):
    T = dfb.shape[0]

    def body(df_ref, w_ref, gu_ref, o_ref):
        da = lax.dot_general(df_ref[...], w_ref[...], _NT, preferred_element_type=F32)
        o_ref[0] = (da * gu_ref[0].astype(F32)).astype(BF16)
        o_ref[1] = (da * gu_ref[1].astype(F32)).astype(BF16)

    grid = (FF_HALF_BLOCKS, T // tm)
    ex_args, ex_specs, ex_out, ex_scratch = _exchange_operands(exchange)
    return pl.pallas_call(
        _carry_exchange(body, exchange, 0, 3, 1, grid), name=name, grid=grid,
        in_specs=[pl.BlockSpec((tm, D_MODEL), lambda j, i: (i, 0)),
                  pl.BlockSpec((FF_BLK, D_MODEL), lambda j, i: (j, 0)),
                  pl.BlockSpec((2, None, tm, FF_BLK), lambda j, i: (0, j, i, 0))] + ex_specs,
        out_specs=[pl.BlockSpec((2, None, tm, FF_BLK), lambda j, i: (0, j, i, 0))] + ex_specs,
        out_shape=[jax.ShapeDtypeStruct((2, FF_HALF_BLOCKS, T, FF_BLK), BF16)] + ex_out,
        scratch_shapes=ex_scratch,
        compiler_params=_params(("arbitrary", "arbitrary") if exchange else ("parallel", "parallel")),
    )(dfb, w_down, gu, *ex_args)


def _wo_bwd(dmb, wo_t, z, bg, pa, pb, *, name, tm):
    T = dmb.shape[0]

    def body(dm_ref, w_ref, gc_ref, gm_ref, bg_ref, pa_ref, pb_ref, dpa_ref, dpb_ref, dg_ref, dbg_ref):
        i = pl.program_id(0)
        dm = lax.dot_general(dm_ref[...], w_ref[...], _NT, preferred_element_type=F32)
        bgv = bg_ref[...]
        sa = jax.nn.sigmoid(gc_ref[...] + bgv[0:1])
        sb = jax.nn.sigmoid(gm_ref[...] + bgv[1:2])
        dpa_ref[...] = (dm * sa).astype(BF16)
        dpb_ref[...] = (dm * sb).astype(BF16)
        dga = dm * pa_ref[...].astype(F32) * (sa * (1.0 - sa))
        dgb = dm * pb_ref[...].astype(F32) * (sb * (1.0 - sb))
        dg_ref[:, :D_MODEL] = dga.astype(BF16)
        dg_ref[:, D_MODEL:] = dgb.astype(BF16)

        @pl.when(i == 0)
        def _():
            dbg_ref[...] = jnp.zeros(dbg_ref.shape, F32)

        dbg_ref[0:1, :] += jnp.sum(dga, axis=0, keepdims=True)
        dbg_ref[1:2, :] += jnp.sum(dgb, axis=0, keepdims=True)

    row = pl.BlockSpec((tm, D_MODEL), lambda i: (i, 0))
    return pl.pallas_call(
        body, name=name, grid=(T // tm,),
        in_specs=[row, pl.BlockSpec((D_MODEL, D_MODEL), lambda i: (0, 0)),
                  pl.BlockSpec((tm, D_MODEL), lambda i: (i, Z_GATE // D_MODEL)),
                  pl.BlockSpec((tm, D_MODEL), lambda i: (i, Z_GATE // D_MODEL + 1)),
                  pl.BlockSpec((8, D_MODEL), lambda i: (0, 0)), row, row],
        out_specs=[row, row, pl.BlockSpec((tm, 2 * D_MODEL), lambda i: (i, 0)),
                   pl.BlockSpec((8, D_MODEL), lambda i: (0, 0))],
        out_shape=[jax.ShapeDtypeStruct((T, D_MODEL), BF16), jax.ShapeDtypeStruct((T, D_MODEL), BF16),
                   jax.ShapeDtypeStruct((T, 2 * D_MODEL), BF16), jax.ShapeDtypeStruct((8, D_MODEL), F32)],
        compiler_params=_params(("arbitrary",)),
    )(dmb, wo_t, z, z, bg, pa, pb)


def _conv_bwd(dy, z, conv_w8, *, name, tm):
    T = dy.shape[0]
    n = T // tm
    hb = tm // 8

    def body(dy_ref, b_ref, c_ref, h_ref, cp_ref, hp_ref, dyn_ref, bn_ref, w_ref, dz_ref, dw_ref):
        i = pl.program_id(0)
        u = c_ref[...] * h_ref[...]
        up = jnp.where(i > 0, cp_ref[...] * hp_ref[...], 0.0)
        ue = jnp.concatenate([up, u], axis=0)
        s1 = pltpu.roll(ue, 1, 0)[8:]
        s2 = pltpu.roll(ue, 2, 0)[8:]
        w = w_ref[...]
        conv = w[0:1] * s2 + w[1:2] * s1 + w[2:3] * u
        dyv = dy_ref[...]
        e = dyv * b_ref[...]
        en = jnp.where(i < n - 1, dyn_ref[...] * bn_ref[...], 0.0)
        ee = jnp.concatenate([e, en], axis=0)
        e1 = pltpu.roll(ee, tm + 8 - 1, 0)[:tm]
        e2 = pltpu.roll(ee, tm + 8 - 2, 0)[:tm]
        du = w[2:3] * e + w[1:2] * e1 + w[0:1] * e2
        dz_ref[:, 0:D_CONV] = (dyv * conv).astype(BF16)
        dz_ref[:, D_CONV:2 * D_CONV] = (du * h_ref[...]).astype(BF16)
        dz_ref[:, 2 * D_CONV:] = (du * c_ref[...]).astype(BF16)

        @pl.when(i == 0)
        def _():
            dw_ref[...] = jnp.zeros(dw_ref.shape, F32)

        dw_ref[0:1, :] += jnp.sum(e * s2, axis=0, keepdims=True)
        dw_ref[1:2, :] += jnp.sum(e * s1, axis=0, keepdims=True)
        dw_ref[2:3, :] += jnp.sum(e * u, axis=0, keepdims=True)

    def col(c):
        return pl.BlockSpec((tm, D_CONV), lambda i: (i, c))

    def prev(c):
        return pl.BlockSpec((8, D_CONV), lambda i: (jnp.maximum(i * hb - 1, 0), c))

    def nxt(c):
        return pl.BlockSpec((8, D_CONV), lambda i: (jnp.minimum((i + 1) * hb, T // 8 - 1), c))

    return pl.pallas_call(
        body, name=name, grid=(n,),
        in_specs=[col(0), col(0), col(1), col(2), prev(1), prev(2), nxt(0), nxt(0),
                  pl.BlockSpec((8, D_CONV), lambda i: (0, 0))],
        out_specs=[pl.BlockSpec((tm, 3 * D_CONV), lambda i: (i, 0)), pl.BlockSpec((8, D_CONV), lambda i: (0, 0))],
        out_shape=[jax.ShapeDtypeStruct((T, 3 * D_CONV), BF16), jax.ShapeDtypeStruct((8, D_CONV), F32)],
        compiler_params=_params(("arbitrary",)),
    )(dy, z, z, z, z, z, dy, z, conv_w8)


def _attn_bwd(q, k, v, o, do, lse, *, name, blk, exchange=None):
    T = q.shape[0]
    n = T // blk
    hp = HEADS_PER_STEP
    qi = np.array([i for j in range(n) for i in range(j, n)], np.int32)
    kj = np.array([j for j in range(n) for i in range(j, n)], np.int32)

    rc = _tile(blk, (SOFTMAX_ROWS,))

    def body(qi_ref, kj_ref, q_ref, k_ref, v_ref, o_ref, do_ref, lse_ref, dq_ref, dk_ref, dv_ref,
             dk_sc, dv_sc, s_sc, dp_sc, p_sc, ds_sc, delta_sc):
        s_id = pl.program_id(1)
        i = qi_ref[s_id]
        j = kj_ref[s_id]

        @pl.when(s_id == 0)
        def _():
            dq_ref[...] = jnp.zeros(dq_ref.shape, F32)

        @pl.when(i == j)
        def _():
            dk_sc[...] = jnp.zeros(dk_sc.shape, F32)
            dv_sc[...] = jnp.zeros(dv_sc.shape, F32)

        q_rows = pl.ds(pl.multiple_of(i * blk, blk), blk)

        def head_step(hh, diagonal):
            hs = slice(hh * HEAD_PAD, (hh + 1) * HEAD_PAD)
            vs = slice(hh * V_HEAD, (hh + 1) * V_HEAD)
            qh = q_ref[:, hs]
            kh = k_ref[:, hs]
            doh = do_ref[:, vs]
            s_sc[hh] = lax.dot_general(qh, kh, _NT, preferred_element_type=F32)
            vh = v_ref[:, hh * HEAD_PAD:hh * HEAD_PAD + V_HEAD]
            dp_sc[hh] = lax.dot_general(doh, vh, _NT, preferred_element_type=F32)
            delta = jnp.sum(doh.astype(F32) * o_ref[:, vs].astype(F32), axis=-1, keepdims=True)
            delta_sc[hh] = jnp.broadcast_to(delta, (blk, LANE))
            for r in range(blk // rc):
                rows = slice(r * rc, (r + 1) * rc)
                lse = lse_ref[hh, rows, :]
                dl = delta_sc[hh, rows, :]
                for t in range(blk // LANE):
                    cols = slice(t * LANE, (t + 1) * LANE)
                    s = s_sc[hh, rows, cols]
                    if diagonal:
                        s = _diag_mask(s, r * rc, t * LANE)
                    p = jnp.exp2(s - lse)
                    p_sc[hh, rows, cols] = p.astype(BF16)
                    ds_sc[hh, rows, cols] = (p * (dp_sc[hh, rows, cols] - dl)).astype(BF16)
            dv_sc[hh] += lax.dot_general(p_sc[hh], doh, _TN, preferred_element_type=F32)
            dk_sc[hh] += lax.dot_general(ds_sc[hh], qh, _TN, preferred_element_type=F32)
            dq_ref[q_rows, hs] += jnp.dot(ds_sc[hh], kh, preferred_element_type=F32)

        @pl.when(j < i)
        def _():
            for hh in range(hp):
                head_step(hh, False)

        @pl.when(j == i)
        def _():
            for hh in range(hp):
                head_step(hh, True)

        @pl.when(i == n - 1)
        def _():
            for hh in range(hp):
                dk_ref[:, hh * HEAD_PAD:(hh + 1) * HEAD_PAD] = dk_sc[hh] * (1.0 / LOG2_E)
                dv_ref[:, hh * V_HEAD:(hh + 1) * V_HEAD] = dv_sc[hh]

    wq = hp * HEAD_PAD
    wv = hp * V_HEAD
    grid = (MLA_HEADS // hp, len(qi))
    ex_args, ex_specs, ex_out, ex_scratch = _exchange_operands(exchange)
    grid_spec = pltpu.PrefetchScalarGridSpec(
        num_scalar_prefetch=2, grid=grid,
        in_specs=[pl.BlockSpec((blk, wq), lambda g, s, qi, kj: (qi[s], g)),
                  pl.BlockSpec((blk, wq), lambda g, s, qi, kj: (kj[s], g)),
                  pl.BlockSpec((blk, wq), lambda g, s, qi, kj: (kj[s], g)),
                  pl.BlockSpec((blk, wv), lambda g, s, qi, kj: (qi[s], g)),
                  pl.BlockSpec((blk, wv), lambda g, s, qi, kj: (qi[s], g)),
                  pl.BlockSpec((hp, blk, LANE), lambda g, s, qi, kj: (g, qi[s], 0))] + ex_specs,
        out_specs=[pl.BlockSpec((T, wq), lambda g, s, qi, kj: (0, g), pipeline_mode=pl.Buffered(1)),
                   pl.BlockSpec((blk, wq), lambda g, s, qi, kj: (kj[s], g)),
                   pl.BlockSpec((blk, wv), lambda g, s, qi, kj: (kj[s], g))] + ex_specs,
        scratch_shapes=[pltpu.VMEM((hp, blk, HEAD_PAD), F32), pltpu.VMEM((hp, blk, V_HEAD), F32),
                        pltpu.VMEM((hp, blk, blk), F32), pltpu.VMEM((hp, blk, blk), F32),
                        pltpu.VMEM((hp, blk, blk), BF16), pltpu.VMEM((hp, blk, blk), BF16),
                        pltpu.VMEM((hp, blk, LANE), F32)] + ex_scratch)
    return pl.pallas_call(
        _carry_exchange(body, exchange, 2, 6, 3, grid), name=name, grid_spec=grid_spec,
        out_shape=[jax.ShapeDtypeStruct((T, D_QK), F32), jax.ShapeDtypeStruct((T, D_QK), F32),
                   jax.ShapeDtypeStruct((T, D_MLA), F32)] + ex_out,
        compiler_params=_params(("arbitrary", "arbitrary") if exchange else ("parallel", "arbitrary")),
    )(jnp.asarray(qi), jnp.asarray(kj), q, k, v, o, do, lse, *ex_args)


def _qkv_bwd(dq, dk, dv, z, gq, gkv, wq_t, wk_t, wv_t, tabs, *, name, tm):
    T = dq.shape[0]

    def body(dq_ref, dk_ref, dv_ref, z_ref, gq_ref, gkv_ref, wq_ref, wk_ref, wv_ref, c_ref, s1_ref, s2_ref,
             dz_ref, dqb_ref, dkb_ref, dvb_ref, dgq_ref, dgkv_ref):
        i = pl.program_id(0)
        c = jnp.tile(c_ref[...], (1, MLA_HEADS))
        s1 = jnp.tile(s1_ref[...], (1, MLA_HEADS))
        s2 = jnp.tile(s2_ref[...], (1, MLA_HEADS))
        dqp = _rope_t(dq_ref[...] * ATTN_SCALE, c, s1, s2).astype(BF16)
        dkp = _rope_t(dk_ref[...], c, s1, s2).astype(BF16)
        dvb = dv_ref[...].astype(BF16)
        dqb_ref[...] = dqp
        dkb_ref[...] = dkp
        dvb_ref[...] = dvb
        dqn = lax.dot_general(dqp, wq_ref[...], _NT, preferred_element_type=F32)
        dkin = lax.dot_general(dkp, wk_ref[...], _NT, preferred_element_type=F32)
        dkvn = dkin[:, :KV_LORA] + lax.dot_general(dvb, wv_ref[...], _NT, preferred_element_type=F32)
        zz = z_ref[...]

        def rms_bwd(x, g, dy):
            rstd = lax.rsqrt(jnp.mean(x * x, axis=-1, keepdims=True) + RMS_EPS)
            xh = x * rstd
            dxh = dy * g
            dx = rstd * (dxh - xh * jnp.mean(dxh * xh, axis=-1, keepdims=True))
            return dx, jnp.sum(dy * xh, axis=0, keepdims=True)

        dcq, dgq = rms_bwd(zz[:, :Q_LORA], gq_ref[...], dqn)
        dckv, dgkv = rms_bwd(zz[:, Q_LORA:Q_LORA + KV_LORA], gkv_ref[...], dkvn)
        dz_ref[:, :Q_LORA] = dcq.astype(BF16)
        dz_ref[:, Q_LORA:Q_LORA + KV_LORA] = dckv.astype(BF16)
        dz_ref[:, Q_LORA + KV_LORA:] = dkin[:, KV_LORA:].astype(BF16)

        @pl.when(i == 0)
        def _():
            dgq_ref[...] = jnp.zeros(dgq_ref.shape, F32)
            dgkv_ref[...] = jnp.zeros(dgkv_ref.shape, F32)

        dgq_ref[0:1, :] += dgq
        dgkv_ref[0:1, :] += dgkv

    def full(shape):
        return pl.BlockSpec(shape, lambda i: (0, 0))

    def rows(w, c=0):
        return pl.BlockSpec((tm, w), lambda i: (i, c))

    return pl.pallas_call(
        body, name=name, grid=(T // tm,),
        in_specs=[rows(D_QK), rows(D_QK), rows(D_MLA), rows(512, Z_MID // 512),
                  full((1, Q_LORA)), full((1, KV_LORA)),
                  full((Q_LORA, D_QK)), full((Q_LORA, D_QK)), full((KV_LORA, D_MLA)),
                  rows(LANE), rows(LANE), rows(LANE)],
        out_specs=[rows(512), rows(D_QK), rows(D_QK), rows(D_MLA), full((8, Q_LORA)), full((8, KV_LORA))],
        out_shape=[jax.ShapeDtypeStruct((T, 512), BF16), jax.ShapeDtypeStruct((T, D_QK), BF16),
                   jax.ShapeDtypeStruct((T, D_QK), BF16), jax.ShapeDtypeStruct((T, D_MLA), BF16),
                   jax.ShapeDtypeStruct((8, Q_LORA), F32), jax.ShapeDtypeStruct((8, KV_LORA), F32)],
        compiler_params=_params(("arbitrary",)),
    )(dq, dk, dv, z, gq, gkv, wq_t, wk_t, wv_t, *tabs)


FLAT_W = 1024


ELEMENTWISE_TILE_BYTES = 768 * 1024


def _row_tile(R, C):
    width = -(-C // LANE) * LANE * 4
    best = None
    for t in range(16, R + 1, 16):
        if R % t == 0 and t * width <= ELEMENTWISE_TILE_BYTES:
            best = t
    if best is None:
        best = R
    return best


def _adamw(w, m, v, layer, parts, part_index, *, name):
    _, R, C = w.shape
    tr = _row_tile(R, C)
    bc1 = 1.0 - ADAM_B1 ** ADAM_STEP
    bc2 = 1.0 - ADAM_B2 ** ADAM_STEP
    n_parts = len(parts)

    def body(idx_ref, w_ref, m_ref, v_ref, *refs):
        g_refs = refs[:n_parts]
        g_out, d_out, m_out, v_out = refs[n_parts:]
        g = g_refs[0][...].astype(F32)
        for r in g_refs[1:]:
            g = g + r[...].astype(F32)
        wv = w_ref[...]
        mn = ADAM_B1 * m_ref[...] + (1.0 - ADAM_B1) * g
        vn = ADAM_B2 * v_ref[...] + (1.0 - ADAM_B2) * (g * g)
        m_hat = mn / bc1
        v_hat = vn / bc2
        g_out[...] = g
        d_out[...] = -ADAM_LR * (m_hat / (jnp.sqrt(v_hat) + ADAM_EPS) + ADAM_WD * wv)
        m_out[...] = mn
        v_out[...] = vn

    layer_row = pl.BlockSpec((None, tr, C), lambda i, idx: (layer, i, 0))
    in_specs = [layer_row, layer_row, layer_row]
    args = [w, m, v]
    for arr, slot in parts:
        if slot is None:
            in_specs.append(pl.BlockSpec((None, tr, C), lambda i, idx: (idx[0], i, 0)))
        else:
            in_specs.append(pl.BlockSpec((None, tr, C), lambda i, idx, slot=slot: (slot, i, 0)))
        args.append(arr)
    grid_spec = pltpu.PrefetchScalarGridSpec(
        num_scalar_prefetch=1, grid=(R // tr,), in_specs=in_specs,
        out_specs=[pl.BlockSpec((tr, C), lambda i, idx: (i, 0))] * 4)
    return pl.pallas_call(
        body, name=name, grid_spec=grid_spec,
        out_shape=[jax.ShapeDtypeStruct((R, C), F32)] * 4,
        compiler_params=_params(("parallel",)),
    )(part_index, *args)


def _sum8(parts, *, name):
    _, R, _ = parts.shape

    def body(p_ref, o_ref):
        acc = p_ref[0]
        for d in range(1, N_DEV):
            acc = acc + p_ref[d]
        o_ref[...] = acc

    return pl.pallas_call(
        body, name=name, grid=(1,),
        in_specs=[pl.BlockSpec((N_DEV, R, FLAT_W), lambda i: (0, 0, 0))],
        out_specs=pl.BlockSpec((R, FLAT_W), lambda i: (0, 0)),
        out_shape=jax.ShapeDtypeStruct((R, FLAT_W), F32),
        compiler_params=_params(("arbitrary",)),
    )(parts)


_MESH_ID = pl.DeviceIdType.MESH
_ANY = pl.BlockSpec(memory_space=pl.ANY)


def _all_gather(shards, *, name):
    n = len(shards)

    def body(*refs):
        x_refs, out_refs = refs[:n], refs[n:2 * n]
        send_sems, recv_sems, local_sems = refs[2 * n:]
        x, y, c = lax.axis_index("x"), lax.axis_index("y"), lax.axis_index("c")
        me, sibling = (x, y, c), (x, y, 1 - c)
        chips = [(1 - x, y), (x, 1 - y), (1 - x, 1 - y)]

        def blk(a, px, py, pc):
            return out_refs[a].at[4 * px + 2 * py + pc]

        def copy(a, k, block, to, src=None):
            return pltpu.make_async_remote_copy(
                src_ref=blk(a, *block) if src is None else src, dst_ref=blk(a, *block),
                send_sem=send_sems.at[7 * a + k], recv_sem=recv_sems.at[7 * a + k],
                device_id=to, device_id_type=_MESH_ID)

        mine = [pltpu.make_async_copy(x_refs[a], blk(a, *me), local_sems.at[a]) for a in range(n)]
        for cp in mine:
            cp.start()
        first = []
        for a in range(n):
            first.append(copy(a, 0, me, sibling, src=x_refs[a]))
            first += [copy(a, 1 + j, me, (*chip, c), src=x_refs[a]) for j, chip in enumerate(chips)]
        for cp in first:
            cp.start()
        passed = []
        for j, chip in enumerate(chips):
            for a in range(n):
                copy(a, 1 + j, (*chip, c), me).wait_recv()
                fwd = copy(a, 4 + j, (*chip, c), sibling)
                fwd.start()
                passed.append(fwd)
        for a in range(n):
            copy(a, 0, sibling, me).wait_recv()
        for j, chip in enumerate(chips):
            for a in range(n):
                copy(a, 4 + j, (*chip, 1 - c), me).wait_recv()
        for cp in first + passed:
            cp.wait_send()
        for cp in mine:
            cp.wait()

    return pl.pallas_call(
        body, name=name,
        out_shape=[jax.ShapeDtypeStruct((N_DEV,) + s.shape, s.dtype) for s in shards],
        in_specs=[_ANY] * n, out_specs=[_ANY] * n,
        scratch_shapes=[pltpu.SemaphoreType.DMA((7 * n,)), pltpu.SemaphoreType.DMA((7 * n,)),
                        pltpu.SemaphoreType.DMA((n,))],
    )(*shards)


_BIG = (("ffn1_w_up", 2), ("ffn1_w_down", 1), ("mix_w_in", 2), ("w_uq", 2), ("w_ukv", 2),
        ("w_br_conv", 2), ("w_br_mla", 2), ("w_o", 1), ("ffn2_w_up", 2), ("ffn2_w_down", 1))
_SMALL_SHARDED = (("meta_tokens", 1), ("mix_b_gate", 2), ("conv_w", 2), ("ln_g", 2), ("ln_b", 2))
_SMALL_REPL = ("q_norm_g", "kv_norm_g")


def _pack(arrs, dtype, row_align=8):
    flat = jnp.concatenate([a.reshape(-1).astype(dtype) for a in arrs])
    n = flat.shape[0]
    rows = -(-n // (row_align * FLAT_W)) * row_align
    return jnp.pad(flat, (0, rows * FLAT_W - n)).reshape(rows, FLAT_W)


def _unpack(flat, shapes):
    flat = flat.reshape(-1)
    out, off = [], 0
    for s in shapes:
        n = int(np.prod(s))
        out.append(flat[off:off + n].reshape(s))
        off += n
    return out


def _unpack_gathered(gathered, shapes, axes):
    g2 = gathered.reshape(N_DEV, -1)
    out, off = [], 0
    for s, ax in zip(shapes, axes):
        n = int(np.prod(s))
        blocks = g2[:, off:off + n].reshape((N_DEV,) + tuple(s))
        full = jnp.moveaxis(blocks, 0, ax)
        out.append(full.reshape(tuple(s[:ax]) + (N_DEV * s[ax],) + tuple(s[ax + 1:])))
        off += n
    return out


def _to_dest(full, ax):
    s = full.shape
    split = full.reshape(s[:ax] + (N_DEV, s[ax] // N_DEV) + s[ax + 1:])
    return jnp.moveaxis(split, ax, 0)


def _from_blocks(g, ax):
    full = jnp.moveaxis(g, 0, ax)
    s = full.shape
    return full.reshape(s[:ax] + (s[ax] * s[ax + 1],) + s[ax + 2:])


def _rope_tables(T):
    inv_freq = 1.0 / (ROPE_BASE ** (jnp.arange(0, QK_ROPE, 2, dtype=F32) / QK_ROPE))
    ang = jnp.arange(T, dtype=F32)[:, None] * inv_freq[None, :]
    cos, sin = jnp.cos(ang), jnp.sin(ang)
    half = QK_ROPE // 2
    ones = jnp.ones((T, QK_NOPE), F32)
    zeros = lambda w: jnp.zeros((T, w), F32)
    c = jnp.concatenate([ones, cos, cos, zeros(HEAD_PAD - QK_NOPE - QK_ROPE)], axis=1)
    s1 = jnp.concatenate([zeros(QK_NOPE + half), sin, zeros(HEAD_PAD - QK_NOPE - QK_ROPE)], axis=1)
    s2 = jnp.concatenate([zeros(QK_NOPE), -sin, zeros(HEAD_PAD - QK_NOPE - half)], axis=1)
    return c, s1, s2


_FFN1 = ("ffn1_w_up", "ffn1_w_down")
_MIXER = ("mix_w_in", "w_uq", "w_ukv", "w_br_conv", "w_br_mla", "w_o")
_FFN2 = ("ffn2_w_up", "ffn2_w_down")
_MIXER_EARLY = ("w_br_conv", "w_br_mla", "w_o")


def _mixer_weights(gathered):
    axes = dict(_BIG)
    W = {n: _from_blocks(gathered[n], axes[n] - 1) for n in _MIXER}
    w_in = W["mix_w_in"]
    w_in_p = jnp.concatenate([w_in[:, :Z_KR_END], jnp.zeros((D_MODEL, D_IN_PAD - D_IN_REAL), BF16),
                              w_in[:, Z_KR_END:]], axis=1)
    w_uq = W["w_uq"].reshape(Q_LORA, MLA_HEADS, QK_NOPE + QK_ROPE)
    wq = jnp.pad(w_uq, ((0, 0), (0, 0), (0, HEAD_PAD - QK_NOPE - QK_ROPE))).reshape(Q_LORA, D_QK)
    w_ukv = W["w_ukv"].reshape(KV_LORA, MLA_HEADS, QK_NOPE + V_HEAD)
    wk_top = jnp.pad(w_ukv[:, :, :QK_NOPE], ((0, 0), (0, 0), (0, HEAD_PAD - QK_NOPE))).reshape(KV_LORA, D_QK)
    place = np.zeros((Q_LORA - KV_LORA, MLA_HEADS, HEAD_PAD), np.float32)
    for r in range(QK_ROPE):
        place[r, :, QK_NOPE + r] = 1.0
    wk = jnp.concatenate([wk_top, jnp.asarray(place.reshape(Q_LORA - KV_LORA, D_QK), BF16)], axis=0)
    wv = w_ukv[:, :, QK_NOPE:].reshape(KV_LORA, D_MLA)
    wv_ext = jnp.pad(w_ukv[:, :, QK_NOPE:], ((0, 0), (0, 0), (0, HEAD_PAD - V_HEAD))).reshape(KV_LORA, D_QK)
    return dict(w_in=w_in_p, wq=wq, wk=wk, wv=wv, wv_ext=wv_ext,
                wbc=W["w_br_conv"], wbm=W["w_br_mla"], wo=W["w_o"])


def _row8(v):
    return jnp.pad(v, ((0, 8 - v.shape[0]), (0, 0)))


def _local_step(x, tgt, gathered_up1, w_blocks, S):
    big_names = [n for n, _ in _BIG]
    axes = dict(_BIG)
    t_real = N_META + x.shape[0]
    T = -(-t_real // ROW_ALIGN) * ROW_ALIGN
    pad = T - t_real
    tm = _tile(T, (384, 256, 128))
    tms = _tile(T, (768, 256, 128))
    blk = _tile(T, (768, 256, 128))
    tabs = _rope_tables(T)

    h0 = jnp.concatenate([S["meta_tokens"], x, jnp.zeros((pad, D_MODEL), F32)], axis=0)
    tgt_p = jnp.concatenate([jnp.zeros((N_META, D_MODEL), F32), tgt, jnp.zeros((pad, D_MODEL), F32)], axis=0)

    def gather_of(l, names):
        return _Exchange("gather", [w_blocks[l][n] for n in names])

    saved = []
    h, hb = h0, h0.astype(BF16)
    G = dict(gathered_up1)
    for l in range(DEPTH):
        first = l == 0
        sv = dict(h_in=h, h_in_b=hb, up1=G["ffn1_w_up"])
        names = ("ffn1_w_down", "mix_w_in")
        gu, a, *got = _ffn_up(hb, sv["up1"], name=f"ffn_up_{l}a", tm=tms,
                              exchange=gather_of(0, names) if first else None)
        G.update(zip(names, got))
        sv["down1"] = _from_blocks(G["ffn1_w_down"], 0)
        names = tuple(n for n in _MIXER if n != "mix_w_in")
        r, h1, h1b, *got = _mm_res_ln(a, sv["down1"], h, S["ln_g"][l, 0:1], S["ln_b"][l, 0:1], scale=0.5,
                                      name=f"ffn_down_ln_{l}a", tm=tm,
                                      exchange=gather_of(0, names) if first else None)
        G.update(zip(names, got))
        sv["f1"] = dict(gu=gu, a=a, r=r)
        lw = _mixer_weights(G)
        sv["lw"] = lw
        z = _mm(h1b, lw["w_in"], out_dtype=F32, name=f"mix_in_{l}", tm=tms, tn=1024, tk=D_MODEL)
        conv_w8 = _row8(S["conv_w"][l])
        bg8 = _row8(S["mix_b_gate"][l])
        yc = _conv_fwd(z, conv_w8, name=f"conv_fwd_{l}", tm=tms)
        gq, gkv = S["q_norm_g"][l:l + 1], S["kv_norm_g"][l:l + 1]
        q, k, v, qn, kin = _qkv_proj(z, gq, gkv, lw["wq"], lw["wk"], lw["wv_ext"], tabs, name=f"qkv_proj_{l}", tm=tm)
        if first:
            exchange = _Exchange("gather", [w_blocks[1][n] for n in _FFN1 + _MIXER] + [w_blocks[0]["ffn2_w_up"]])
        else:
            exchange = gather_of(1, _FFN2)
        o, lse, *nxt = _attn_fwd(q, k, v, name=f"attn_fwd_{l}", blk=blk, exchange=exchange)
        if first:
            G["ffn2_w_up"] = nxt.pop()
        else:
            G.update(zip(_FFN2, nxt))
        sv["up2"] = G["ffn2_w_up"]
        mg, pa, pb, r2, h2, h2b = _merge_wo_ln(yc, o, lw["wbc"], lw["wbm"], z, bg8, lw["wo"], h1,
                                               S["ln_g"][l, 1:2], S["ln_b"][l, 1:2], name=f"merge_wo_ln_{l}", tm=tm)
        gu, a, *got = _ffn_up(h2b, sv["up2"], name=f"ffn_up_{l}b", tm=tms,
                              exchange=gather_of(0, _FFN2[1:]) if first else None)
        G.update(zip(_FFN2[1:], got))
        sv["down2"] = _from_blocks(G["ffn2_w_down"], 0)
        r, h3, h3b = _mm_res_ln(a, sv["down2"], h2, S["ln_g"][l, 2:3], S["ln_b"][l, 2:3], scale=0.5,
                                name=f"ffn_down_ln_{l}b", tm=tm)
        sv["f2"] = dict(gu=gu, a=a, r=r)
        if first:
            G = dict(zip(_FFN1 + _MIXER, nxt))
        sv.update(h1b=h1b, z=z, conv_w8=conv_w8, bg8=bg8, yc=yc, gq=gq, gkv=gkv, q=q, k=k, v=v, qn=qn, kin=kin,
                  o=o, lse=lse, mg=mg, pa=pa, pb=pb, r2=r2, h2b=h2b)
        saved.append(sv)
        h, hb = h3, h3b

    dh, loss8 = _loss_head(h, tgt_p, t_real=t_real, name="loss_head", tm=tm)

    tk = _tile(T, (2816, 768, 256, 128))
    grads = {n: [None] * DEPTH for n, _ in _BIG}
    for n in ("mix_b_gate", "conv_w", "q_norm_g", "kv_norm_g"):
        grads[n] = [None] * DEPTH
    grads["ln_g"] = [[None] * 3 for _ in range(DEPTH)]
    grads["ln_b"] = [[None] * 3 for _ in range(DEPTH)]

    def dest_of(l, names):
        return {n: grads[n][l] if n.endswith("w_up") else _to_dest(grads[n][l], axes[n] - 1) for n in names}

    def scatter_of(dest):
        return _Exchange("scatter", [dest[n].astype(BF16) for n in dest])

    dest = [{}, {}]
    received = [{}, {}]

    def ffn_bwd(dy, f, h_in_b, up8, down, g, tag, names, l, carry=None):
        dr, dfb, dg, db = _ln_bwd(dy, f["r"], g, scale=0.5, name=f"ln_bwd_{tag}", tm=tm)
        dgu, *got = _ffn_bwd_mid(dfb, down, f["gu"], name=f"ffn_bwd_mid_{tag}", tm=tms,
                                 exchange=scatter_of(carry) if carry else None)
        if carry:
            received[l].update(zip(carry, got))
        grads[names[1]][l] = _mm_tn_call(
            f["a"], dfb,
            pl.BlockSpec((None, tk, FF_BLK), lambda i, j, k: (i, k, 0)),
            pl.BlockSpec((tk, D_MODEL), lambda i, j, k: (k, 0)),
            out_shape=jax.ShapeDtypeStruct((D_FF, D_MODEL), F32),
            out_spec=pl.BlockSpec((FF_BLK, D_MODEL), lambda i, j, k: (i, 0)),
            grid=(FF_HALF_BLOCKS, 1, T // tk), name=f"dw_down_{tag}")
        own_down = dest_of(l, names[1:]) if carry else None
        d_up = _mm_tn_call(
            h_in_b, dgu,
            pl.BlockSpec((tk, D_MODEL), lambda i, j, k: (k, 0)),
            pl.BlockSpec((None, None, tk, FF_BLK),
                         lambda i, j, k: (j // FF_HALF_BLOCKS, j % FF_HALF_BLOCKS, k, 0)),
            out_shape=jax.ShapeDtypeStruct((N_DEV, D_MODEL, FF_BLK), F32),
            out_spec=pl.BlockSpec((None, D_MODEL, FF_BLK), lambda i, j, k: (j, 0, 0)),
            grid=(1, N_DEV, T // tk), name=f"dw_up_{tag}",
            exchange=scatter_of(own_down) if carry else None)
        own = None
        if carry:
            d_up, *got = d_up
            received[l].update(zip(own_down, got))
        grads[names[0]][l] = d_up
        if carry:
            own = dest_of(l, names[:1])
            dest[l].update(own_down)
            dest[l].update(own)
        row = pl.BlockSpec((tms, D_MODEL), lambda i, j, k: (i, 0))
        dh_in = _mm_call(
            dgu, up8,
            pl.BlockSpec((None, FF_HALF_BLOCKS, tms, FF_BLK), lambda i, j, k: (k, 0, i, 0)),
            pl.BlockSpec((FF_HALF_BLOCKS, D_MODEL, FF_BLK), lambda i, j, k: (k, 0, 0)),
            out_shape=jax.ShapeDtypeStruct((T, D_MODEL), F32), out_spec=row, acc_shape=(tms, D_MODEL),
            grid=(T // tms, 1, 2), name=f"ffn_dx_{tag}", trans_b=True, res=dr, res_spec=row,
            res_scale=ALPHA, pieces=FF_HALF_BLOCKS, exchange=scatter_of(own) if own else None)
        if own:
            dh_in, *got = dh_in
            received[l].update(zip(own, got))
        return dh_in, dg[0], db[0]

    for l in reversed(range(DEPTH)):
        sv = saved[l]
        lw = sv["lw"]
        dh, grads["ln_g"][l][2], grads["ln_b"][l][2] = ffn_bwd(
            dh, sv["f2"], sv["h2b"], sv["up2"], sv["down2"], S["ln_g"][l, 2:3], f"{l}b", _FFN2, l)
        dr2, dmb, dg, db = _ln_bwd(dh, sv["r2"], S["ln_g"][l, 1:2], scale=1.0, name=f"ln_bwd_{l}m", tm=tm)
        grads["ln_g"][l][1], grads["ln_b"][l][1] = dg[0], db[0]
        grads["w_o"][l] = _mm_tn(sv["mg"], dmb, name=f"dw_o_{l}", tm=D_MODEL, tn=D_MODEL, tk=tk)
        dpa, dpb, dgate, dbg = _wo_bwd(dmb, lw["wo"], sv["z"], sv["bg8"], sv["pa"], sv["pb"], name=f"wo_bwd_{l}", tm=tm)
        grads["mix_b_gate"][l] = dbg[0:2]
        grads["w_br_conv"][l] = _mm_tn(sv["yc"], dpa, name=f"dw_br_conv_{l}", tm=D_CONV, tn=D_MODEL, tk=tk)
        grads["w_br_mla"][l] = _mm_tn(sv["o"], dpb, name=f"dw_br_mla_{l}", tm=D_MLA, tn=D_MODEL, tk=tk)
        dyc = _mm(dpa, lw["wbc"], trans_b=True, out_dtype=F32, name=f"d_yconv_{l}", tm=tms, tn=D_CONV, tk=D_MODEL)
        dym = _mm(dpb, lw["wbm"], trans_b=True, out_dtype=BF16, name=f"d_ymla_{l}", tm=tms, tn=D_MLA, tk=D_MODEL)
        dz_conv, dcw = _conv_bwd(dyc, sv["z"], sv["conv_w8"], name=f"conv_bwd_{l}", tm=tms)
        grads["conv_w"][l] = dcw[0:CONV_WIDTH]
        if l == 0:
            dest[1] = dest_of(1, big_names)
            dest[0].update(dest_of(0, _FFN2 + _MIXER_EARLY))
            sent = {(1, n): dest[1][n] for n in big_names}
            sent.update({(0, n): dest[0][n] for n in _FFN2 + _MIXER_EARLY})
            dq, dk, dv, *got = _attn_bwd(
                sv["q"], sv["k"], sv["v"], sv["o"], dym, sv["lse"], name=f"attn_bwd_{l}", blk=blk,
                exchange=scatter_of(sent))
            for (layer, n), r in zip(sent, got):
                received[layer][n] = r
        else:
            dq, dk, dv = _attn_bwd(sv["q"], sv["k"], sv["v"], sv["o"], dym, sv["lse"], name=f"attn_bwd_{l}", blk=blk)
        dz_mid, dqb, dkb, dvb, dgq, dgkv = _qkv_bwd(dq, dk, dv, sv["z"], sv["gq"], sv["gkv"], lw["wq"], lw["wk"],
                                                    lw["wv"], tabs, name=f"qkv_bwd_{l}", tm=tm)
        grads["q_norm_g"][l], grads["kv_norm_g"][l] = dgq[0], dgkv[0]
        d_wq = _mm_tn(sv["qn"], dqb, name=f"dw_uq_{l}", tm=Q_LORA, tn=D_QK, tk=tk)
        d_wk = _mm_tn(sv["kin"], dkb, name=f"dw_uk_{l}", tm=Q_LORA, tn=D_QK, tk=tk)
        d_wv = _mm_tn(sv["kin"], dvb, name=f"dw_uv_{l}", tm=Q_LORA, tn=D_MLA, tk=tk)
        grads["w_uq"][l] = d_wq.reshape(Q_LORA, MLA_HEADS, HEAD_PAD)[:, :, :QK_NOPE + QK_ROPE].reshape(Q_LORA, -1)
        d_kn = d_wk[:KV_LORA].reshape(KV_LORA, MLA_HEADS, HEAD_PAD)[:, :, :QK_NOPE]
        d_vv = d_wv[:KV_LORA].reshape(KV_LORA, MLA_HEADS, V_HEAD)
        grads["w_ukv"][l] = jnp.concatenate([d_kn, d_vv], axis=-1).reshape(KV_LORA, -1)
        dz = jnp.concatenate([dz_conv, dz_mid, dgate], axis=1)
        d_win = _mm_tn(sv["h1b"], dz, name=f"dw_in_{l}", tm=D_MODEL, tn=1024, tk=tk)
        grads["mix_w_in"][l] = jnp.concatenate([d_win[:, :Z_KR_END], d_win[:, Z_KR_END + D_IN_PAD - D_IN_REAL:]], axis=1)
        dh = _mm(dz, lw["w_in"], trans_b=True, out_dtype=F32, name=f"mix_dx_{l}", res=dr2, res_scale=ALPHA,
                 tm=tms, tn=D_MODEL, tk=2048)
        carry = None
        if l == 0:
            carry = dest_of(0, [n for n in _MIXER if n not in _MIXER_EARLY])
            dest[0].update(carry)
        dh, grads["ln_g"][l][0], grads["ln_b"][l][0] = ffn_bwd(
            dh, sv["f1"], sv["h_in_b"], sv["up1"], sv["down1"], S["ln_g"][l, 0:1], f"{l}a", _FFN1, l, carry=carry)

    small = {n: jnp.stack(grads[n]) for n in ("mix_b_gate", "conv_w", "q_norm_g", "kv_norm_g")}
    small["ln_g"] = jnp.stack([jnp.stack(g) for g in grads["ln_g"]])
    small["ln_b"] = jnp.stack([jnp.stack(g) for g in grads["ln_b"]])
    small["meta_tokens"] = dh[:N_META]
    return loss8, dh[N_META:t_real], dest, received, small


def kernel(x, meta_tokens, ffn1_w_up, ffn1_w_down, mix_w_in, mix_b_gate, conv_w, q_norm_g, w_uq, kv_norm_g, w_ukv, w_br_conv, w_br_mla, w_o, ffn2_w_up, ffn2_w_down, ln_g, ln_b, loss_target, m_meta_tokens, m_ffn1_w_up, m_ffn1_w_down, m_mix_w_in, m_mix_b_gate, m_conv_w, m_q_norm_g, m_w_uq, m_kv_norm_g, m_w_ukv, m_w_br_conv, m_w_br_mla, m_w_o, m_ffn2_w_up, m_ffn2_w_down, m_ln_g, m_ln_b, v_meta_tokens, v_ffn1_w_up, v_ffn1_w_down, v_mix_w_in, v_mix_b_gate, v_conv_w, v_q_norm_g, v_w_uq, v_kv_norm_g, v_w_ukv, v_w_br_conv, v_w_br_mla, v_w_o, v_ffn2_w_up, v_ffn2_w_down, v_ln_g, v_ln_b):
    names = ["meta_tokens", "ffn1_w_up", "ffn1_w_down", "mix_w_in", "mix_b_gate", "conv_w", "q_norm_g", "w_uq",
             "kv_norm_g", "w_ukv", "w_br_conv", "w_br_mla", "w_o", "ffn2_w_up", "ffn2_w_down", "ln_g", "ln_b"]
    w = dict(zip(names, (meta_tokens, ffn1_w_up, ffn1_w_down, mix_w_in, mix_b_gate, conv_w, q_norm_g, w_uq,
                         kv_norm_g, w_ukv, w_br_conv, w_br_mla, w_o, ffn2_w_up, ffn2_w_down, ln_g, ln_b)))
    m = dict(zip(names, (m_meta_tokens, m_ffn1_w_up, m_ffn1_w_down, m_mix_w_in, m_mix_b_gate, m_conv_w, m_q_norm_g,
                         m_w_uq, m_kv_norm_g, m_w_ukv, m_w_br_conv, m_w_br_mla, m_w_o, m_ffn2_w_up, m_ffn2_w_down,
                         m_ln_g, m_ln_b)))
    v = dict(zip(names, (v_meta_tokens, v_ffn1_w_up, v_ffn1_w_down, v_mix_w_in, v_mix_b_gate, v_conv_w, v_q_norm_g,
                         v_w_uq, v_kv_norm_g, v_w_ukv, v_w_br_conv, v_w_br_mla, v_w_o, v_ffn2_w_up, v_ffn2_w_down,
                         v_ln_g, v_ln_b)))
    ix, iy, ic = lax.axis_index("x"), lax.axis_index("y"), lax.axis_index("c")
    dev = 4 * ix + 2 * iy + ic

    big_names = [n for n, _ in _BIG]
    small_names = [n for n, _ in _SMALL_SHARDED]
    small_axes = [a for _, a in _SMALL_SHARDED]
    small_shapes = [w[n].shape for n in small_names]
    gathered = _all_gather([w[n][0].astype(BF16) for n in _FFN1[:1]] + [_pack([w[n] for n in small_names], F32)],
                           name="all_gather_weights")
    gathered_up1 = dict(zip(_FFN1[:1], gathered[:-1]))
    w_blocks = [{n: w[n][l].astype(BF16) for n in big_names} for l in range(DEPTH)]
    S = dict(zip(small_names, _unpack_gathered(gathered[-1], small_shapes, small_axes)))
    S["q_norm_g"], S["kv_norm_g"] = q_norm_g, kv_norm_g

    loss8, grad_x, dest, received, G = _local_step(x[0], loss_target[0], gathered_up1, w_blocks, S)

    my_dev = dev.reshape(1).astype(jnp.int32)
    big_res = [{}, {}, {}, {}]
    for n in big_names:
        res = [_adamw(w[n], m[n], v[n], l, [(dest[l][n], None)] + [(received[l][n], k) for k in range(N_DEV - 1)],
                      my_dev, name=f"adamw{l}_{n}") for l in range(DEPTH)]
        for kind in range(4):
            big_res[kind][n] = jnp.stack([r[kind] for r in res])

    small_all = small_names + list(_SMALL_REPL)
    part = _pack([G[n] for n in small_all] + [loss8[0, 0:1]], F32)
    full_shapes = [G[n].shape for n in small_all] + [(1,)]
    summed = _sum8(_all_gather([part], name="all_gather_small_grads")[0], name="sum_small_grads")
    unpacked = _unpack(summed, full_shapes)
    loss = unpacked[-1][0]
    g_full = dict(zip(small_all, unpacked[:-1]))
    g_loc = []
    for n in small_all:
        if n in _SMALL_REPL:
            g_loc.append(g_full[n])
        else:
            ax = dict(_SMALL_SHARDED)[n]
            g_loc.append(lax.dynamic_slice_in_dim(g_full[n], dev * w[n].shape[ax], w[n].shape[ax], axis=ax))
    loc_shapes = [w[n].shape for n in small_all]
    g_pack = _pack(g_loc, F32)
    small_out = _adamw(_pack([w[n] for n in small_all], F32)[None], _pack([m[n] for n in small_all], F32)[None],
                       _pack([v[n] for n in small_all], F32)[None], 0, [(g_pack[None], 0)],
                       jnp.zeros((1,), jnp.int32), name="adamw_small")
    small_res = [dict(zip(small_all, _unpack(o, loc_shapes))) for o in small_out]

    outs = [loss, grad_x[None]]
    for kind in range(4):
        for n in names:
            outs.append(big_res[kind][n] if n in big_res[kind] else small_res[kind][n])
    return tuple(outs)
```
